```python
import math
import jax, jax.numpy as jnp
from jax import lax
import numpy as np

D_MODEL = 1024
BATCH = 8
SEQ = 8192
DEPTH = 1

CONV_CH = D_MODEL // 2
CONV_WIDTH = 31
MLA_HEADS = 8
QK_NOPE = D_MODEL // 16
QK_ROPE = D_MODEL // 32
V_DIM = D_MODEL // 16
Q_LORA = 3 * D_MODEL // 8
KV_LORA = D_MODEL // 4
N_BRANCH = 2
IN_COLS = 2 * CONV_CH + Q_LORA + KV_LORA + QK_ROPE + N_BRANCH * D_MODEL
MEM_LEN = 256
X_HEADS = 4
X_HEAD_DIM = D_MODEL // 8
D_FF = 4 * D_MODEL
Q_BLOCK = 128
ROPE_THETA = 10000.0
EPS = 1e-6

kernel_name = "hybrid_conformer_mla_gated_block"


def rms_norm(x, g):
    xf = x.astype(jnp.float32)
    y = xf * lax.rsqrt(jnp.mean(xf * xf, axis=-1, keepdims=True) + EPS)
    return (y * g.astype(jnp.float32)).astype(x.dtype)


def layer_norm(x, g, b):
    xf = x.astype(jnp.float32)
    mu = jnp.mean(xf, axis=-1, keepdims=True)
    var = jnp.mean(jnp.square(xf - mu), axis=-1, keepdims=True)
    y = (xf - mu) * lax.rsqrt(var + EPS)
    return (y * g.astype(jnp.float32) + b.astype(jnp.float32)).astype(x.dtype)


def rope_tables(positions):
    half = QK_ROPE // 2
    inv_freq = ROPE_THETA ** (-jnp.arange(half, dtype=jnp.float32) / half)
    ang = positions.astype(jnp.float32)[..., None] * inv_freq
    return jnp.cos(ang), jnp.sin(ang)


def apply_rope(t, cos, sin):
    half = t.shape[-1] // 2
    t1, t2 = t[..., :half], t[..., half:]
    c, s = cos.astype(t.dtype), sin.astype(t.dtype)
    return jnp.concatenate([t1 * c - t2 * s, t2 * c + t1 * s], axis=-1)


def conformer_conv(conv_in, conv_w, conv_b, ln_g, ln_b, w_conv_out):
    a, gt = jnp.split(conv_in, 2, axis=-1)
    z = a * jax.nn.sigmoid(gt)
    rhs = conv_w.astype(z.dtype).reshape(CONV_WIDTH, 1, CONV_CH)
    z = lax.conv_general_dilated(
        z, rhs, window_strides=(1,), padding=[(CONV_WIDTH - 1, 0)],
        dimension_numbers=("NWC", "WIO", "NWC"), feature_group_count=CONV_CH)
    z = z + conv_b
    z = layer_norm(z, ln_g, ln_b)
    z = jax.nn.silu(z)
    return z @ w_conv_out


def mla_attention(c_q, c_kv, k_rope_raw, cos, sin, q_norm_g, w_uq, kv_norm_g, w_ukv, w_mla_out):
    B, S, _ = c_q.shape
    q = rms_norm(c_q, q_norm_g) @ w_uq
    q = q.reshape(B, S, MLA_HEADS, QK_NOPE + QK_ROPE)
    q_nope, q_rope = q[..., :QK_NOPE], q[..., QK_NOPE:]
    q_rope = apply_rope(q_rope, cos[:, :, None, :], sin[:, :, None, :])
    kv = rms_norm(c_kv, kv_norm_g) @ w_ukv
    kv = kv.reshape(B, S, MLA_HEADS, QK_NOPE + V_DIM)
    k_nope, v = kv[..., :QK_NOPE], kv[..., QK_NOPE:]
    k_rope = apply_rope(k_rope_raw, cos, sin)

    scale = (QK_NOPE + QK_ROPE) ** -0.5
    n_blk = S // Q_BLOCK
    qn = (q_nope * scale).reshape(B, n_blk, Q_BLOCK, MLA_HEADS, QK_NOPE).transpose(1, 0, 2, 3, 4)
    qr = (q_rope * scale).reshape(B, n_blk, Q_BLOCK, MLA_HEADS, QK_ROPE).transpose(1, 0, 2, 3, 4)
    key_idx = jnp.arange(S)
    neg = jnp.finfo(jnp.float32).min

    def attend(args):
        qn_b, qr_b, blk = args
        s = jnp.einsum("bqhd,bkhd->bhqk", qn_b, k_nope, preferred_element_type=jnp.float32)
        s = s + jnp.einsum("bqhr,bkr->bhqk", qr_b, k_rope, preferred_element_type=jnp.float32)
        q_idx = blk * Q_BLOCK + jnp.arange(Q_BLOCK)
        mask = key_idx[None, :] <= q_idx[:, None]
        s = jnp.where(mask[None, None], s, neg)
        p = jax.nn.softmax(s, axis=-1).astype(v.dtype)
        return jnp.einsum("bhqk,bkhd->bqhd", p, v)

    o = lax.map(attend, (qn, qr, jnp.arange(n_blk)))
    o = o.transpose(1, 0, 2, 3, 4).reshape(B, S, MLA_HEADS * V_DIM)
    return o @ w_mla_out


def memory_cross_attention(u, mem_n, w_xq, w_xkv, w_xo):
    B, S, _ = u.shape
    q = (u @ w_xq).reshape(B, S, X_HEADS, X_HEAD_DIM) * (X_HEAD_DIM ** -0.5)
    kv = (mem_n @ w_xkv).reshape(B, MEM_LEN, 2, X_HEADS, X_HEAD_DIM)
    k, v = kv[:, :, 0], kv[:, :, 1]
    s = jnp.einsum("bqhd,bkhd->bhqk", q, k, preferred_element_type=jnp.float32)
    p = jax.nn.softmax(s, axis=-1).astype(v.dtype)
    o = jnp.einsum("bhqk,bkhd->bqhd", p, v).reshape(B, S, X_HEADS * X_HEAD_DIM)
    return o @ w_xo


def _fwd_setup_inputs(seed: int = 0) -> dict:
    key = jax.random.key(seed)
    ks = jax.random.split(key, 32)
    f32 = jnp.float32

    def w(k, shape, fan_in):
        return jax.random.normal(k, shape, f32) * (fan_in ** -0.5)

    def gain(k, shape):
        return 1.0 + 0.02 * jax.random.normal(k, shape, f32)

    L = DEPTH
    x = jax.random.normal(ks[0], (BATCH, SEQ, D_MODEL), f32)
    mem = jax.random.normal(ks[1], (BATCH, MEM_LEN, D_MODEL), f32)
    offsets = jax.random.randint(ks[2], (BATCH, 1), 0, 4096, dtype=jnp.int32)
    positions = offsets + jnp.arange(SEQ, dtype=jnp.int32)[None, :]
    return {
        "x": x,
        "mem": mem,
        "positions": positions,
        "norm_mix_g": gain(ks[3], (L, D_MODEL)),
        "w_in": w(ks[4], (L, D_MODEL, IN_COLS), D_MODEL),
        "conv_w": w(ks[5], (L, CONV_WIDTH, CONV_CH), CONV_WIDTH),
        "conv_b": 0.02 * jax.random.normal(ks[6], (L, CONV_CH), f32),
        "conv_ln_g": gain(ks[7], (L, CONV_CH)),
        "conv_ln_b": 0.02 * jax.random.normal(ks[8], (L, CONV_CH), f32),
        "w_conv_out": w(ks[9], (L, CONV_CH, D_MODEL), CONV_CH),
        "q_norm_g": gain(ks[10], (L, Q_LORA)),
        "w_uq": w(ks[11], (L, Q_LORA, MLA_HEADS * (QK_NOPE + QK_ROPE)), Q_LORA),
        "kv_norm_g": gain(ks[12], (L, KV_LORA)),
        "w_ukv": w(ks[13], (L, KV_LORA, MLA_HEADS * (QK_NOPE + V_DIM)), KV_LORA),
        "w_mla_out": w(ks[14], (L, MLA_HEADS * V_DIM, D_MODEL), MLA_HEADS * V_DIM),
        "w_out": w(ks[15], (L, D_MODEL, D_MODEL), D_MODEL),
        "norm_xattn_g": gain(ks[16], (L, D_MODEL)),
        "norm_mem_g": gain(ks[17], (L, D_MODEL)),
        "w_xq": w(ks[18], (L, D_MODEL, X_HEADS * X_HEAD_DIM), D_MODEL),
        "w_xkv": w(ks[19], (L, D_MODEL, 2 * X_HEADS * X_HEAD_DIM), D_MODEL),
        "w_xo": w(ks[20], (L, X_HEADS * X_HEAD_DIM, D_MODEL), X_HEADS * X_HEAD_DIM),
        "norm_mlp_g": gain(ks[21], (L, D_MODEL)),
        "w_mlp1": w(ks[22], (L, D_MODEL, D_FF), D_MODEL),
        "w_mlp2": w(ks[23], (L, D_FF, D_MODEL), D_FF),
        "final_norm_g": gain(ks[24], (D_MODEL,)),
    }


def _fwd_reference(x, mem, positions, norm_mix_g, w_in, conv_w, conv_b, conv_ln_g, conv_ln_b,
              w_conv_out, q_norm_g, w_uq, kv_norm_g, w_ukv, w_mla_out, w_out,
              norm_xattn_g, norm_mem_g, w_xq, w_xkv, w_xo, norm_mlp_g, w_mlp1, w_mlp2,
              final_norm_g):
    cos, sin = rope_tables(positions)
    B, S, _ = x.shape
    cut = np.cumsum([2 * CONV_CH, Q_LORA, KV_LORA, QK_ROPE]).tolist()
    h = x
    for l in range(DEPTH):
        u = rms_norm(h, norm_mix_g[l])
        proj = u @ w_in[l]
        conv_in, c_q, c_kv, k_rope_raw, gate_logits = jnp.split(proj, cut, axis=-1)
        conv_out = conformer_conv(conv_in, conv_w[l], conv_b[l], conv_ln_g[l], conv_ln_b[l],
                                  w_conv_out[l])
        mla_out = mla_attention(c_q, c_kv, k_rope_raw, cos, sin, q_norm_g[l], w_uq[l],
                                kv_norm_g[l], w_ukv[l], w_mla_out[l])
        gates = jax.nn.sigmoid(gate_logits).reshape(B, S, N_BRANCH, D_MODEL)
        merged = gates[:, :, 0] * conv_out + gates[:, :, 1] * mla_out
        h = h + merged @ w_out[l]
        u = rms_norm(h, norm_xattn_g[l])
        mem_n = rms_norm(mem, norm_mem_g[l])
        h = h + memory_cross_attention(u, mem_n, w_xq[l], w_xkv[l], w_xo[l])
        u = rms_norm(h, norm_mlp_g[l])
        h = h + jnp.square(jax.nn.relu(u @ w_mlp1[l])) @ w_mlp2[l]
    return rms_norm(h, final_norm_g)


import jax as _jax
import jax.numpy as _jnp

TWIN_FORMAT = 'train_step'
FWD_PARAMS = ['x', 'mem', 'positions', 'norm_mix_g', 'w_in', 'conv_w', 'conv_b', 'conv_ln_g', 'conv_ln_b', 'w_conv_out', 'q_norm_g', 'w_uq', 'kv_norm_g', 'w_ukv', 'w_mla_out', 'w_out', 'norm_xattn_g', 'norm_mem_g', 'w_xq', 'w_xkv', 'w_xo', 'norm_mlp_g', 'w_mlp1', 'w_mlp2', 'final_norm_g']
TWIN_WEIGHTS = ['norm_mix_g', 'w_in', 'conv_w', 'conv_b', 'conv_ln_g', 'conv_ln_b', 'w_conv_out', 'q_norm_g', 'w_uq', 'kv_norm_g', 'w_ukv', 'w_mla_out', 'w_out', 'norm_xattn_g', 'norm_mem_g', 'w_xq', 'w_xkv', 'w_xo', 'norm_mlp_g', 'w_mlp1', 'w_mlp2', 'final_norm_g']
TWIN_DIFF_INPUT = 'x'
TWIN_INPUTS = ['x', 'mem', 'positions', 'norm_mix_g', 'w_in', 'conv_w', 'conv_b', 'conv_ln_g', 'conv_ln_b', 'w_conv_out', 'q_norm_g', 'w_uq', 'kv_norm_g', 'w_ukv', 'w_mla_out', 'w_out', 'norm_xattn_g', 'norm_mem_g', 'w_xq', 'w_xkv', 'w_xo', 'norm_mlp_g', 'w_mlp1', 'w_mlp2', 'final_norm_g', 'loss_target', 'm_norm_mix_g', 'm_w_in', 'm_conv_w', 'm_conv_b', 'm_conv_ln_g', 'm_conv_ln_b', 'm_w_conv_out', 'm_q_norm_g', 'm_w_uq', 'm_kv_norm_g', 'm_w_ukv', 'm_w_mla_out', 'm_w_out', 'm_norm_xattn_g', 'm_norm_mem_g', 'm_w_xq', 'm_w_xkv', 'm_w_xo', 'm_norm_mlp_g', 'm_w_mlp1', 'm_w_mlp2', 'm_final_norm_g', 'v_norm_mix_g', 'v_w_in', 'v_conv_w', 'v_conv_b', 'v_conv_ln_g', 'v_conv_ln_b', 'v_w_conv_out', 'v_q_norm_g', 'v_w_uq', 'v_kv_norm_g', 'v_w_ukv', 'v_w_mla_out', 'v_w_out', 'v_norm_xattn_g', 'v_norm_mem_g', 'v_w_xq', 'v_w_xkv', 'v_w_xo', 'v_norm_mlp_g', 'v_w_mlp1', 'v_w_mlp2', 'v_final_norm_g']
TWIN_OUTPUTS = ['loss', 'grad_x', 'grad_norm_mix_g', 'grad_w_in', 'grad_conv_w', 'grad_conv_b', 'grad_conv_ln_g', 'grad_conv_ln_b', 'grad_w_conv_out', 'grad_q_norm_g', 'grad_w_uq', 'grad_kv_norm_g', 'grad_w_ukv', 'grad_w_mla_out', 'grad_w_out', 'grad_norm_xattn_g', 'grad_norm_mem_g', 'grad_w_xq', 'grad_w_xkv', 'grad_w_xo', 'grad_norm_mlp_g', 'grad_w_mlp1', 'grad_w_mlp2', 'grad_final_norm_g', 'delta_norm_mix_g', 'delta_w_in', 'delta_conv_w', 'delta_conv_b', 'delta_conv_ln_g', 'delta_conv_ln_b', 'delta_w_conv_out', 'delta_q_norm_g', 'delta_w_uq', 'delta_kv_norm_g', 'delta_w_ukv', 'delta_w_mla_out', 'delta_w_out', 'delta_norm_xattn_g', 'delta_norm_mem_g', 'delta_w_xq', 'delta_w_xkv', 'delta_w_xo', 'delta_norm_mlp_g', 'delta_w_mlp1', 'delta_w_mlp2', 'delta_final_norm_g', 'new_m_norm_mix_g', 'new_m_w_in', 'new_m_conv_w', 'new_m_conv_b', 'new_m_conv_ln_g', 'new_m_conv_ln_b', 'new_m_w_conv_out', 'new_m_q_norm_g', 'new_m_w_uq', 'new_m_kv_norm_g', 'new_m_w_ukv', 'new_m_w_mla_out', 'new_m_w_out', 'new_m_norm_xattn_g', 'new_m_norm_mem_g', 'new_m_w_xq', 'new_m_w_xkv', 'new_m_w_xo', 'new_m_norm_mlp_g', 'new_m_w_mlp1', 'new_m_w_mlp2', 'new_m_final_norm_g', 'new_v_norm_mix_g', 'new_v_w_in', 'new_v_conv_w', 'new_v_conv_b', 'new_v_conv_ln_g', 'new_v_conv_ln_b', 'new_v_w_conv_out', 'new_v_q_norm_g', 'new_v_w_uq', 'new_v_kv_norm_g', 'new_v_w_ukv', 'new_v_w_mla_out', 'new_v_w_out', 'new_v_norm_xattn_g', 'new_v_norm_mem_g', 'new_v_w_xq', 'new_v_w_xkv', 'new_v_w_xo', 'new_v_norm_mlp_g', 'new_v_w_mlp1', 'new_v_w_mlp2', 'new_v_final_norm_g']
TWIN_LEAF_KINDS = {'loss': 'loss', 'grad_x': 'grad_x', 'grad_norm_mix_g': 'grad_w', 'grad_w_in': 'grad_w', 'grad_conv_w': 'grad_w', 'grad_conv_b': 'grad_w', 'grad_conv_ln_g': 'grad_w', 'grad_conv_ln_b': 'grad_w', 'grad_w_conv_out': 'grad_w', 'grad_q_norm_g': 'grad_w', 'grad_w_uq': 'grad_w', 'grad_kv_norm_g': 'grad_w', 'grad_w_ukv': 'grad_w', 'grad_w_mla_out': 'grad_w', 'grad_w_out': 'grad_w', 'grad_norm_xattn_g': 'grad_w', 'grad_norm_mem_g': 'grad_w', 'grad_w_xq': 'grad_w', 'grad_w_xkv': 'grad_w', 'grad_w_xo': 'grad_w', 'grad_norm_mlp_g': 'grad_w', 'grad_w_mlp1': 'grad_w', 'grad_w_mlp2': 'grad_w', 'grad_final_norm_g': 'grad_w', 'delta_norm_mix_g': 'delta_w', 'delta_w_in': 'delta_w', 'delta_conv_w': 'delta_w', 'delta_conv_b': 'delta_w', 'delta_conv_ln_g': 'delta_w', 'delta_conv_ln_b': 'delta_w', 'delta_w_conv_out': 'delta_w', 'delta_q_norm_g': 'delta_w', 'delta_w_uq': 'delta_w', 'delta_kv_norm_g': 'delta_w', 'delta_w_ukv': 'delta_w', 'delta_w_mla_out': 'delta_w', 'delta_w_out': 'delta_w', 'delta_norm_xattn_g': 'delta_w', 'delta_norm_mem_g': 'delta_w', 'delta_w_xq': 'delta_w', 'delta_w_xkv': 'delta_w', 'delta_w_xo': 'delta_w', 'delta_norm_mlp_g': 'delta_w', 'delta_w_mlp1': 'delta_w', 'delta_w_mlp2': 'delta_w', 'delta_final_norm_g': 'delta_w', 'new_m_norm_mix_g': 'new_m', 'new_m_w_in': 'new_m', 'new_m_conv_w': 'new_m', 'new_m_conv_b': 'new_m', 'new_m_conv_ln_g': 'new_m', 'new_m_conv_ln_b': 'new_m', 'new_m_w_conv_out': 'new_m', 'new_m_q_norm_g': 'new_m', 'new_m_w_uq': 'new_m', 'new_m_kv_norm_g': 'new_m', 'new_m_w_ukv': 'new_m', 'new_m_w_mla_out': 'new_m', 'new_m_w_out': 'new_m', 'new_m_norm_xattn_g': 'new_m', 'new_m_norm_mem_g': 'new_m', 'new_m_w_xq': 'new_m', 'new_m_w_xkv': 'new_m', 'new_m_w_xo': 'new_m', 'new_m_norm_mlp_g': 'new_m', 'new_m_w_mlp1': 'new_m', 'new_m_w_mlp2': 'new_m', 'new_m_final_norm_g': 'new_m', 'new_v_norm_mix_g': 'new_v', 'new_v_w_in': 'new_v', 'new_v_conv_w': 'new_v', 'new_v_conv_b': 'new_v', 'new_v_conv_ln_g': 'new_v', 'new_v_conv_ln_b': 'new_v', 'new_v_w_conv_out': 'new_v', 'new_v_q_norm_g': 'new_v', 'new_v_w_uq': 'new_v', 'new_v_kv_norm_g': 'new_v', 'new_v_w_ukv': 'new_v', 'new_v_w_mla_out': 'new_v', 'new_v_w_out': 'new_v', 'new_v_norm_xattn_g': 'new_v', 'new_v_norm_mem_g': 'new_v', 'new_v_w_xq': 'new_v', 'new_v_w_xkv': 'new_v', 'new_v_w_xo': 'new_v', 'new_v_norm_mlp_g': 'new_v', 'new_v_w_mlp1': 'new_v', 'new_v_w_mlp2': 'new_v', 'new_v_final_norm_g': 'new_v'}


def _forward(args):
    return _fwd_reference(*[args[k] for k in FWD_PARAMS])


def _output_shape():
    def fwd():
        inp = _fwd_setup_inputs(0)
        return _fwd_reference(*[inp[k] for k in FWD_PARAMS])
    out = _jax.eval_shape(fwd)
    return out.shape, out.dtype

N_MICROBATCH = 1
ADAM_LR = 0.001
ADAM_B1 = 0.9
ADAM_B2 = 0.999
ADAM_EPS = 1e-08
ADAM_WD = 0.01
ADAM_STEP = 10
PER_EXAMPLE_BATCH_AXIS = {'x': 0, 'mem': 0, 'positions': 0, 'loss_target': 0}
SHARED_INPUTS = []
_WEIGHT_DTYPES = {'norm_mix_g': _jnp.float32, 'w_in': _jnp.float32, 'conv_w': _jnp.float32, 'conv_b': _jnp.float32, 'conv_ln_g': _jnp.float32, 'conv_ln_b': _jnp.float32, 'w_conv_out': _jnp.float32, 'q_norm_g': _jnp.float32, 'w_uq': _jnp.float32, 'kv_norm_g': _jnp.float32, 'w_ukv': _jnp.float32, 'w_mla_out': _jnp.float32, 'w_out': _jnp.float32, 'norm_xattn_g': _jnp.float32, 'norm_mem_g': _jnp.float32, 'w_xq': _jnp.float32, 'w_xkv': _jnp.float32, 'w_xo': _jnp.float32, 'norm_mlp_g': _jnp.float32, 'w_mlp1': _jnp.float32, 'w_mlp2': _jnp.float32, 'final_norm_g': _jnp.float32}
MOMENT_SCALE = {'norm_mix_g': 1.106838e-01, 'w_in': 5.742791e-02, 'conv_w': 1.252517e-01, 'conv_b': 3.133176e-01, 'conv_ln_g': 1.850377e-01, 'conv_ln_b': 1.933002e-01, 'w_conv_out': 9.180695e-02, 'q_norm_g': 4.217368e-02, 'w_uq': 2.949026e-02, 'kv_norm_g': 7.798760e-02, 'w_ukv': 3.706131e-02, 'w_mla_out': 3.031935e-02, 'w_out': 8.955207e-02, 'norm_xattn_g': 2.728358e-02, 'norm_mem_g': 4.044418e-02, 'w_xq': 3.863234e-02, 'w_xkv': 3.933391e-02, 'w_xo': 2.859160e-02, 'norm_mlp_g': 2.315502e-01, 'w_mlp1': 1.106790e-01, 'w_mlp2': 2.347617e-01, 'final_norm_g': 6.468820e+01}


def _to_microbatches(a, axis):
    t = _jnp.moveaxis(a, axis, 0)
    t = t.reshape((N_MICROBATCH, t.shape[0] // N_MICROBATCH) + t.shape[1:])
    return _jnp.moveaxis(t, 1, axis + 1)


def setup_inputs(seed: int = 0) -> dict:
    inp = _fwd_setup_inputs(seed)
    key = _jax.random.fold_in(_jax.random.key(seed), 7919)
    shape, _ = _output_shape()
    out = dict(inp)
    out["loss_target"] = _jax.random.normal(_jax.random.fold_in(key, 0), shape, _jnp.float32)
    for i, name in enumerate(TWIN_WEIGHTS):
        w = inp[name].astype(_jnp.float32)
        if MOMENT_SCALE is None:
            s = _jnp.sqrt(_jnp.mean(_jnp.square(w)) + 1e-30)
        else:
            s = MOMENT_SCALE[name]
        km, kv = _jax.random.split(_jax.random.fold_in(key, i + 1))
        out[name] = w
        out["m_" + name] = s * _jax.random.normal(km, w.shape, _jnp.float32)
        out["v_" + name] = (s * s) * _jax.random.uniform(kv, w.shape, _jnp.float32, 0.5, 1.5)
    if N_MICROBATCH > 1:
        for name, axis in PER_EXAMPLE_BATCH_AXIS.items():
            out[name] = _to_microbatches(out[name], axis)
    return {'x': out['x'], 'mem': out['mem'], 'positions': out['positions'], 'norm_mix_g': out['norm_mix_g'], 'w_in': out['w_in'], 'conv_w': out['conv_w'], 'conv_b': out['conv_b'], 'conv_ln_g': out['conv_ln_g'], 'conv_ln_b': out['conv_ln_b'], 'w_conv_out': out['w_conv_out'], 'q_norm_g': out['q_norm_g'], 'w_uq': out['w_uq'], 'kv_norm_g': out['kv_norm_g'], 'w_ukv': out['w_ukv'], 'w_mla_out': out['w_mla_out'], 'w_out': out['w_out'], 'norm_xattn_g': out['norm_xattn_g'], 'norm_mem_g': out['norm_mem_g'], 'w_xq': out['w_xq'], 'w_xkv': out['w_xkv'], 'w_xo': out['w_xo'], 'norm_mlp_g': out['norm_mlp_g'], 'w_mlp1': out['w_mlp1'], 'w_mlp2': out['w_mlp2'], 'final_norm_g': out['final_norm_g'], 'loss_target': out['loss_target'], 'm_norm_mix_g': out['m_norm_mix_g'], 'm_w_in': out['m_w_in'], 'm_conv_w': out['m_conv_w'], 'm_conv_b': out['m_conv_b'], 'm_conv_ln_g': out['m_conv_ln_g'], 'm_conv_ln_b': out['m_conv_ln_b'], 'm_w_conv_out': out['m_w_conv_out'], 'm_q_norm_g': out['m_q_norm_g'], 'm_w_uq': out['m_w_uq'], 'm_kv_norm_g': out['m_kv_norm_g'], 'm_w_ukv': out['m_w_ukv'], 'm_w_mla_out': out['m_w_mla_out'], 'm_w_out': out['m_w_out'], 'm_norm_xattn_g': out['m_norm_xattn_g'], 'm_norm_mem_g': out['m_norm_mem_g'], 'm_w_xq': out['m_w_xq'], 'm_w_xkv': out['m_w_xkv'], 'm_w_xo': out['m_w_xo'], 'm_norm_mlp_g': out['m_norm_mlp_g'], 'm_w_mlp1': out['m_w_mlp1'], 'm_w_mlp2': out['m_w_mlp2'], 'm_final_norm_g': out['m_final_norm_g'], 'v_norm_mix_g': out['v_norm_mix_g'], 'v_w_in': out['v_w_in'], 'v_conv_w': out['v_conv_w'], 'v_conv_b': out['v_conv_b'], 'v_conv_ln_g': out['v_conv_ln_g'], 'v_conv_ln_b': out['v_conv_ln_b'], 'v_w_conv_out': out['v_w_conv_out'], 'v_q_norm_g': out['v_q_norm_g'], 'v_w_uq': out['v_w_uq'], 'v_kv_norm_g': out['v_kv_norm_g'], 'v_w_ukv': out['v_w_ukv'], 'v_w_mla_out': out['v_w_mla_out'], 'v_w_out': out['v_w_out'], 'v_norm_xattn_g': out['v_norm_xattn_g'], 'v_norm_mem_g': out['v_norm_mem_g'], 'v_w_xq': out['v_w_xq'], 'v_w_xkv': out['v_w_xkv'], 'v_w_xo': out['v_w_xo'], 'v_norm_mlp_g': out['v_norm_mlp_g'], 'v_w_mlp1': out['v_w_mlp1'], 'v_w_mlp2': out['v_w_mlp2'], 'v_final_norm_g': out['v_final_norm_g']}


def _loss(weights, diff, rest, loss_target):
    with _jax.named_scope("forward"):
        args = {**rest, TWIN_DIFF_INPUT: diff, **{k: w.astype(_WEIGHT_DTYPES[k]) for k, w in weights.items()}}
        y = _forward(args)
    with _jax.named_scope("loss_head"):
        err = _jnp.square(y.astype(_jnp.float32) - loss_target)
        return 0.5 * _jnp.sum(_jnp.mean(err, axis=-1)) if err.ndim else 0.5 * err


def _adamw(w, g, m, v):
    m = ADAM_B1 * m + (1.0 - ADAM_B1) * g
    v = ADAM_B2 * v + (1.0 - ADAM_B2) * _jnp.square(g)
    m_hat = m / (1.0 - ADAM_B1 ** ADAM_STEP)
    v_hat = v / (1.0 - ADAM_B2 ** ADAM_STEP)
    delta = -ADAM_LR * (m_hat / (_jnp.sqrt(v_hat) + ADAM_EPS) + ADAM_WD * w)
    return delta, m, v


def reference(x, mem, positions, norm_mix_g, w_in, conv_w, conv_b, conv_ln_g, conv_ln_b, w_conv_out, q_norm_g, w_uq, kv_norm_g, w_ukv, w_mla_out, w_out, norm_xattn_g, norm_mem_g, w_xq, w_xkv, w_xo, norm_mlp_g, w_mlp1, w_mlp2, final_norm_g, loss_target, m_norm_mix_g, m_w_in, m_conv_w, m_conv_b, m_conv_ln_g, m_conv_ln_b, m_w_conv_out, m_q_norm_g, m_w_uq, m_kv_norm_g, m_w_ukv, m_w_mla_out, m_w_out, m_norm_xattn_g, m_norm_mem_g, m_w_xq, m_w_xkv, m_w_xo, m_norm_mlp_g, m_w_mlp1, m_w_mlp2, m_final_norm_g, v_norm_mix_g, v_w_in, v_conv_w, v_conv_b, v_conv_ln_g, v_conv_ln_b, v_w_conv_out, v_q_norm_g, v_w_uq, v_kv_norm_g, v_w_ukv, v_w_mla_out, v_w_out, v_norm_xattn_g, v_norm_mem_g, v_w_xq, v_w_xkv, v_w_xo, v_norm_mlp_g, v_w_mlp1, v_w_mlp2, v_final_norm_g):
    given = dict(x=x, mem=mem, positions=positions, norm_mix_g=norm_mix_g, w_in=w_in, conv_w=conv_w, conv_b=conv_b, conv_ln_g=conv_ln_g, conv_ln_b=conv_ln_b, w_conv_out=w_conv_out, q_norm_g=q_norm_g, w_uq=w_uq, kv_norm_g=kv_norm_g, w_ukv=w_ukv, w_mla_out=w_mla_out, w_out=w_out, norm_xattn_g=norm_xattn_g, norm_mem_g=norm_mem_g, w_xq=w_xq, w_xkv=w_xkv, w_xo=w_xo, norm_mlp_g=norm_mlp_g, w_mlp1=w_mlp1, w_mlp2=w_mlp2, final_norm_g=final_norm_g, loss_target=loss_target, m_norm_mix_g=m_norm_mix_g, m_w_in=m_w_in, m_conv_w=m_conv_w, m_conv_b=m_conv_b, m_conv_ln_g=m_conv_ln_g, m_conv_ln_b=m_conv_ln_b, m_w_conv_out=m_w_conv_out, m_q_norm_g=m_q_norm_g, m_w_uq=m_w_uq, m_kv_norm_g=m_kv_norm_g, m_w_ukv=m_w_ukv, m_w_mla_out=m_w_mla_out, m_w_out=m_w_out, m_norm_xattn_g=m_norm_xattn_g, m_norm_mem_g=m_norm_mem_g, m_w_xq=m_w_xq, m_w_xkv=m_w_xkv, m_w_xo=m_w_xo, m_norm_mlp_g=m_norm_mlp_g, m_w_mlp1=m_w_mlp1, m_w_mlp2=m_w_mlp2, m_final_norm_g=m_final_norm_g, v_norm_mix_g=v_norm_mix_g, v_w_in=v_w_in, v_conv_w=v_conv_w, v_conv_b=v_conv_b, v_conv_ln_g=v_conv_ln_g, v_conv_ln_b=v_conv_ln_b, v_w_conv_out=v_w_conv_out, v_q_norm_g=v_q_norm_g, v_w_uq=v_w_uq, v_kv_norm_g=v_kv_norm_g, v_w_ukv=v_w_ukv, v_w_mla_out=v_w_mla_out, v_w_out=v_w_out, v_norm_xattn_g=v_norm_xattn_g, v_norm_mem_g=v_norm_mem_g, v_w_xq=v_w_xq, v_w_xkv=v_w_xkv, v_w_xo=v_w_xo, v_norm_mlp_g=v_norm_mlp_g, v_w_mlp1=v_w_mlp1, v_w_mlp2=v_w_mlp2, v_final_norm_g=v_final_norm_g)
    weights = {n: given[n] for n in TWIN_WEIGHTS}
    shared = {n: given[n] for n in SHARED_INPUTS}
    per_example = {n: given[n] for n in ['x', 'mem', 'positions']}
    grad_fn = _jax.value_and_grad(_loss, argnums=(0, 1))

    def one_microbatch(ex, loss_target):
        ex = dict(ex)
        diff = ex.pop(TWIN_DIFF_INPUT)
        return grad_fn(weights, diff, {**shared, **ex}, loss_target)

    if N_MICROBATCH == 1:
        loss, (grad_w, grad_x) = one_microbatch(per_example, given["loss_target"])
    else:
        def body(carry, xs):
            loss_sum, grad_sum = carry
            l_k, (gw_k, gx_k) = one_microbatch(xs[0], xs[1])
            with _jax.named_scope("update"):
                return (loss_sum + l_k, _jax.tree.map(_jnp.add, grad_sum, gw_k)), gx_k

        init = (_jnp.zeros((), _jnp.float32), _jax.tree.map(_jnp.zeros_like, weights))
        (loss, grad_w), grad_x = _jax.lax.scan(body, init, (per_example, given["loss_target"]))
    with _jax.named_scope("update"):
        delta_w, new_m, new_v = {}, {}, {}
        for n in TWIN_WEIGHTS:
            delta_w[n], new_m[n], new_v[n] = _adamw(weights[n], grad_w[n], given["m_" + n], given["v_" + n])
    return (loss, grad_x, *[grad_w[n] for n in TWIN_WEIGHTS], *[delta_w[n] for n in TWIN_WEIGHTS],
            *[new_m[n] for n in TWIN_WEIGHTS], *[new_v[n] for n in TWIN_WEIGHTS])
```

```python
import functools
import math

import jax
import jax.numpy as jnp
from jax import lax
from jax.experimental import pallas as pl
from jax.experimental.pallas import tpu as pltpu

F32 = jnp.float32
BF16 = jnp.bfloat16
MESH = pl.DeviceIdType.MESH

D_MODEL = 1024
CONV_CH = 512
CONV_WIDTH = 31
MLA_HEADS = 8
QK_NOPE = 64
QK_ROPE = 32
V_DIM = 64
Q_LORA = 384
KV_LORA = 256
MEM_LEN = 256
X_HEADS = 4
X_HEAD_DIM = 128
D_FF = 4096
ROPE_THETA = 10000.0
EPS = 1e-6
HEAD_PAD = 128
STAT_COL_QK = QK_NOPE + QK_ROPE
STAT_COL_V = V_DIM
HALO = 32
N_CHIPS = 4
LANES = 128

ADAM_LR = 0.001
ADAM_B1 = 0.9
ADAM_B2 = 0.999
ADAM_EPS = 1e-08
ADAM_WD = 0.01
ADAM_STEP = 10

VMEM_LIMIT = 52 * 1024 * 1024
NEG = -1e30

BIG = ["w_in", "w_conv_out", "w_uq", "w_ukv", "w_mla_out", "w_out", "w_xq", "w_xkv", "w_xo", "w_mlp1", "w_mlp2", "conv_w"]
SHARD_AXIS = {"w_in": 1, "w_conv_out": 1, "w_uq": 1, "w_ukv": 1, "w_mla_out": 1, "w_out": 0, "w_xq": 0, "w_xkv": 0,
              "w_xo": 1, "w_mlp1": 1, "w_mlp2": 0, "conv_w": 1}
SMALL = ["norm_mix_g", "conv_b", "conv_ln_g", "conv_ln_b", "q_norm_g", "kv_norm_g", "norm_xattn_g", "norm_mem_g",
         "norm_mlp_g", "final_norm_g"]
WEIGHTS = ["norm_mix_g", "w_in", "conv_w", "conv_b", "conv_ln_g", "conv_ln_b", "w_conv_out", "q_norm_g", "w_uq",
           "kv_norm_g", "w_ukv", "w_mla_out", "w_out", "norm_xattn_g", "norm_mem_g", "w_xq", "w_xkv", "w_xo",
           "norm_mlp_g", "w_mlp1", "w_mlp2", "final_norm_g"]


def _pick(n, prefs):
    for p in prefs:
        if n % p == 0:
            return p
    return n


def _params(sem):
    return pltpu.CompilerParams(dimension_semantics=sem, vmem_limit_bytes=VMEM_LIMIT)


_DIMS = {"nn": (((1,), (0,)), ((), ())), "nt": (((1,), (1,)), ((), ())), "tn": (((0,), (0,)), ((), ()))}


def _mm(name, a, b, mode, outs, epi, row_x=(), tile_x=(), vec_x=(), tm=None, tn=None, tk=None):
    if mode == "nn":
        (M, K), (_, N) = a.shape, b.shape
    elif mode == "nt":
        (M, K), (N, _) = a.shape, b.shape
    else:
        (K, M), (_, N) = a.shape, b.shape
    tm = tm or _pick(M, (512, 384, 256, 128))
    tn = tn or _pick(N, (1024, 768, 512, 384, 256, 128))
    tk = tk or _pick(K, (1024, 768, 512, 384, 256, 128))
    nk = K // tk
    grid = (M // tm, N // tn, nk)
    a_spec = pl.BlockSpec((tk, tm), lambda i, j, k: (k, i)) if mode == "tn" else pl.BlockSpec((tm, tk), lambda i, j, k: (i, k))
    b_spec = pl.BlockSpec((tn, tk), lambda i, j, k: (j, k)) if mode == "nt" else pl.BlockSpec((tk, tn), lambda i, j, k: (k, j))
    in_specs = [a_spec, b_spec]
    in_specs += [pl.BlockSpec((tm, r.shape[1]), lambda i, j, k: (i, 0)) for r in row_x]
    in_specs += [pl.BlockSpec((tm, tn), lambda i, j, k: (i, j)) for _ in tile_x]
    in_specs += [pl.BlockSpec(v.shape, lambda i, j, k: (0, 0)) for v in vec_x]
    out_specs, out_shape = [], []
    for w, dt in outs:
        if tn == N:
            out_specs.append(pl.BlockSpec((tm, w), lambda i, j, k: (i, 0)))
        else:
            assert w == N, (name, w, N)
            out_specs.append(pl.BlockSpec((tm, tn), lambda i, j, k: (i, j)))
        out_shape.append(jax.ShapeDtypeStruct((M, w), dt))
    nx = len(row_x) + len(tile_x) + len(vec_x)
    dims = _DIMS[mode]

    def body(a_ref, b_ref, *rest):
        x_refs, out_refs, acc_ref = rest[:nx], rest[nx:nx + len(outs)], rest[-1]
        av, bv = a_ref[...], b_ref[...]
        if av.dtype != BF16:
            av = av.astype(BF16)
        if bv.dtype != BF16:
            bv = bv.astype(BF16)
        prod = lax.dot_general(av, bv, dims, preferred_element_type=F32)
        if nk == 1:
            acc_ref[...] = prod
            epi(acc_ref, x_refs, out_refs)
        else:
            k = pl.program_id(2)

            @pl.when(k == 0)
            def _():
                acc_ref[...] = prod

            @pl.when(k > 0)
            def _():
                acc_ref[...] += prod

            @pl.when(k == nk - 1)
            def _():
                epi(acc_ref, x_refs, out_refs)

    res = pl.pallas_call(
        body, name=name, grid=grid, in_specs=in_specs, out_specs=out_specs, out_shape=out_shape,
        scratch_shapes=[pltpu.VMEM((tm, tn), F32)],
        compiler_params=_params(("parallel", "parallel", "arbitrary")),
    )(a, b, *row_x, *tile_x, *vec_x)
    return res


def _epi_store(acc_ref, x_refs, out_refs):
    for o in out_refs:
        o[...] = acc_ref[...].astype(o.dtype)


def _mm_plain(name, a, b, mode, dtype=F32, **kw):
    n = b.shape[0] if mode == "nt" else b.shape[1]
    return _mm(name, a, b, mode, [(n, dtype)], _epi_store, **kw)[0]


def _rows(name, body, row_ins, vec_ins, row_outs, acc_outs=(), tile=512):
    S = row_ins[0].shape[0]
    t = _pick(S, (tile, 256, 128, 64, 32, 16, 8))
    in_specs = [pl.BlockSpec((t, r.shape[1]), lambda i: (i, 0)) for r in row_ins]
    in_specs += [pl.BlockSpec(v.shape, lambda i: (0, 0)) for v in vec_ins]
    out_specs = [pl.BlockSpec((t, w), lambda i: (i, 0)) for w, _ in row_outs]
    out_specs += [pl.BlockSpec(shp, lambda i: (0, 0)) for shp in acc_outs]
    out_shape = [jax.ShapeDtypeStruct((S, w), dt) for w, dt in row_outs]
    out_shape += [jax.ShapeDtypeStruct(shp, F32) for shp in acc_outs]
    sem = ("arbitrary",) if acc_outs else ("parallel",)
    return pl.pallas_call(
        functools.partial(body), name=name, grid=(S // t,), in_specs=in_specs, out_specs=out_specs,
        out_shape=out_shape, compiler_params=_params(sem),
    )(*row_ins, *vec_ins)


def _accum(ref, val):
    @pl.when(pl.program_id(0) == 0)
    def _():
        ref[...] = jnp.zeros_like(ref)

    ref[...] += val


def _colsum(v):
    return jnp.sum(v, axis=0, keepdims=True)


def _rms_fwd(x, g):
    r = lax.rsqrt(jnp.mean(x * x, axis=-1, keepdims=True) + EPS)
    return x * r * g


def _rms_bwd(x, g, du):
    r = lax.rsqrt(jnp.mean(x * x, axis=-1, keepdims=True) + EPS)
    xn = x * r
    gdu = du * g
    dx = r * (gdu - xn * jnp.mean(xn * gdu, axis=-1, keepdims=True))
    return dx, _colsum(du * xn)


def _sigmoid(v):
    return 1.0 / (1.0 + jnp.exp(-v))


def _rope(v, c, sa, sb, sign):
    return v * c + sign * (pltpu.roll(v, HEAD_PAD - QK_ROPE // 2, 1) * sa + pltpu.roll(v, QK_ROPE // 2, 1) * sb)


def _split3(v):
    hi = v.astype(BF16)
    r1 = v - hi.astype(F32)
    mid = r1.astype(BF16)
    lo = (r1 - mid.astype(F32)).astype(BF16)
    return hi, mid, lo


def _put_stats(base, stat, col):
    hi, mid, lo = _split3(stat)
    lane = lax.broadcasted_iota(jnp.int32, base.shape, 1)
    out = jnp.where(lane == col, hi, base)
    out = jnp.where(lane == col + 1, mid, out)
    return jnp.where(lane == col + 2, lo, out)


def _neg_ones(shape, col):
    lane = lax.broadcasted_iota(jnp.int32, shape, 1)
    return jnp.where((lane >= col) & (lane < col + 3), -1.0, 0.0).astype(F32)


def _shifted(ext, t):
    p = ext.shape[0]
    for b in range(8):
        rb = ext if b == 0 else pltpu.roll(ext, p - b, 0)
        for a in range(HALO // 8 + 1):
            if 8 * a + b <= HALO:
                yield 8 * a + b, rb[8 * a:8 * a + t]


def _conv_fwd(z0, conv_w, conv_b, ln_g, ln_b):
    S, C = z0.shape
    t = _pick(S, (512, 256, 128, 64, 32))
    per = t // HALO

    def body(cur_ref, prev_ref, w_ref, b_ref, g_ref, beta_ref, z1_ref, z3_ref, ext_ref):
        i = pl.program_id(0)
        ext_ref[0:HALO, :] = jnp.where(i > 0, prev_ref[...], 0.0)
        ext_ref[HALO:, :] = cur_ref[...]
        ext = ext_ref[...]
        acc = jnp.zeros((t, C), F32)
        for d, win in _shifted(ext, t):
            k = d - (HALO - CONV_WIDTH + 1)
            if 0 <= k < CONV_WIDTH:
                acc = acc + win * w_ref[k:k + 1, :]
        z1 = acc + b_ref[...]
        z1_ref[...] = z1
        mu = jnp.mean(z1, axis=-1, keepdims=True)
        zc = z1 - mu
        rs = lax.rsqrt(jnp.mean(zc * zc, axis=-1, keepdims=True) + EPS)
        z2 = zc * rs * g_ref[...] + beta_ref[...]
        z3_ref[...] = (z2 * _sigmoid(z2)).astype(BF16)

    vec = lambda v: pl.BlockSpec(v.shape, lambda i: (0, 0))
    return pl.pallas_call(
        body, name="conv_fwd", grid=(S // t,),
        in_specs=[pl.BlockSpec((t, C), lambda i: (i, 0)),
                  pl.BlockSpec((HALO, C), lambda i: (jnp.maximum(i * per - 1, 0), 0)),
                  vec(conv_w), vec(conv_b), vec(ln_g), vec(ln_b)],
        out_specs=[pl.BlockSpec((t, C), lambda i: (i, 0)), pl.BlockSpec((t, C), lambda i: (i, 0))],
        out_shape=[jax.ShapeDtypeStruct((S, C), F32), jax.ShapeDtypeStruct((S, C), BF16)],
        scratch_shapes=[pltpu.VMEM((t + HALO, C), F32)],
        compiler_params=_params(("parallel",)),
    )(z0, z0, conv_w, conv_b, ln_g, ln_b)


def _conv_bwd_norm(dz3, z1, ln_g, ln_b):
    C = z1.shape[1]

    def body(dz3_ref, z1_ref, g_ref, beta_ref, dz1_ref, dg_ref, dbeta_ref, dbias_ref):
        z1 = z1_ref[...]
        mu = jnp.mean(z1, axis=-1, keepdims=True)
        zc = z1 - mu
        rs = lax.rsqrt(jnp.mean(zc * zc, axis=-1, keepdims=True) + EPS)
        xh = zc * rs
        z2 = xh * g_ref[...] + beta_ref[...]
        sg = _sigmoid(z2)
        dz2 = dz3_ref[...] * (sg * (1.0 + z2 * (1.0 - sg)))
        dxh = dz2 * g_ref[...]
        dz1 = rs * (dxh - jnp.mean(dxh, axis=-1, keepdims=True) - xh * jnp.mean(dxh * xh, axis=-1, keepdims=True))
        dz1_ref[...] = dz1
        _accum(dg_ref, _colsum(dz2 * xh))
        _accum(dbeta_ref, _colsum(dz2))
        _accum(dbias_ref, _colsum(dz1))

    return _rows("conv_bwd_norm", body, [dz3, z1], [ln_g, ln_b], [(C, F32)], [(1, C)] * 3)


def _conv_bwd_taps(dz1, z0, conv_in, conv_w):
    S, C = z0.shape
    t = _pick(S, (512, 256, 128, 64, 32))
    per = t // HALO
    last = S // HALO - 1
    nt = S // t

    def body(dcur_ref, dnext_ref, zcur_ref, zprev_ref, cin_ref, w_ref, dcin_ref, dw_ref, dext_ref, zext_ref):
        i = pl.program_id(0)
        dcur = dcur_ref[...]
        dext_ref[0:t, :] = dcur
        dext_ref[t:, :] = jnp.where(i < nt - 1, dnext_ref[...], 0.0)
        zext_ref[0:HALO, :] = jnp.where(i > 0, zprev_ref[...], 0.0)
        zext_ref[HALO:, :] = zcur_ref[...]

        @pl.when(i == 0)
        def _():
            dw_ref[...] = jnp.zeros_like(dw_ref)

        dz0 = jnp.zeros((t, C), F32)
        for d, win in _shifted(dext_ref[...], t):
            k = CONV_WIDTH - 1 - d
            if 0 <= k < CONV_WIDTH:
                dz0 = dz0 + win * w_ref[k:k + 1, :]
        for d, win in _shifted(zext_ref[...], t):
            k = d - (HALO - CONV_WIDTH + 1)
            if 0 <= k < CONV_WIDTH:
                dw_ref[k:k + 1, :] += _colsum(dcur * win)
        a = cin_ref[:, 0:C]
        sg = _sigmoid(cin_ref[:, C:2 * C])
        dcin_ref[:, 0:C] = (dz0 * sg).astype(BF16)
        dcin_ref[:, C:2 * C] = (dz0 * a * sg * (1.0 - sg)).astype(BF16)

    return pl.pallas_call(
        body, name="conv_bwd_taps", grid=(nt,),
        in_specs=[pl.BlockSpec((t, C), lambda i: (i, 0)),
                  pl.BlockSpec((HALO, C), lambda i: (jnp.minimum((i + 1) * per, last), 0)),
                  pl.BlockSpec((t, C), lambda i: (i, 0)),
                  pl.BlockSpec((HALO, C), lambda i: (jnp.maximum(i * per - 1, 0), 0)),
                  pl.BlockSpec((t, 2 * C), lambda i: (i, 0)),
                  pl.BlockSpec(conv_w.shape, lambda i: (0, 0))],
        out_specs=[pl.BlockSpec((t, 2 * C), lambda i: (i, 0)), pl.BlockSpec((HALO, C), lambda i: (0, 0))],
        out_shape=[jax.ShapeDtypeStruct((S, 2 * C), BF16), jax.ShapeDtypeStruct((HALO, C), F32)],
        scratch_shapes=[pltpu.VMEM((t + HALO, C), F32), pltpu.VMEM((t + HALO, C), F32)],
        compiler_params=_params(("arbitrary",)),
    )(dz1, dz1, z0, z0, conv_in, conv_w)


def _causal_mask(val, fill):
    row = lax.broadcasted_iota(jnp.int32, val.shape, 0)
    col = lax.broadcasted_iota(jnp.int32, val.shape, 1)
    return row, col, fill


def _dot_nt(a, b):
    return lax.dot_general(a, b, _DIMS["nt"], preferred_element_type=F32)


def _dot_nn(a, b):
    return lax.dot_general(a, b, _DIMS["nn"], preferred_element_type=F32)


def _dot_tn(a, b):
    return lax.dot_general(a, b, _DIMS["tn"], preferred_element_type=F32)


def _flash_fwd(q, k, v):
    S = q.shape[0]
    t = _pick(S, (512, 256, 128))
    n = S // t

    def body(q_ref, k_ref, v_ref, o_ref, qa_ref, m_ref, l_ref, acc_ref):
        qi, ki = pl.program_id(1), pl.program_id(2)

        @pl.when(ki == 0)
        def _():
            m_ref[...] = jnp.full_like(m_ref, NEG)
            l_ref[...] = jnp.zeros_like(l_ref)
            acc_ref[...] = jnp.zeros_like(acc_ref)

        def step(diag):
            s = _dot_nt(q_ref[...], k_ref[...])
            if diag:
                row = lax.broadcasted_iota(jnp.int32, s.shape, 0)
                col = lax.broadcasted_iota(jnp.int32, s.shape, 1)
                s = jnp.where(col <= row, s, NEG)
            m_old = m_ref[...]
            m_new = jnp.maximum(m_old, jnp.max(s, axis=-1, keepdims=True))
            alpha = jnp.exp(m_old - m_new)
            p = jnp.exp(s - m_new)
            l_ref[...] = alpha * l_ref[...] + jnp.sum(p, axis=-1, keepdims=True)
            acc_ref[...] = alpha * acc_ref[...] + _dot_nn(p.astype(BF16), v_ref[...])
            m_ref[...] = m_new

        @pl.when(ki < qi)
        def _():
            step(False)

        @pl.when(ki == qi)
        def _():
            step(True)
            l = l_ref[...]
            o_ref[...] = (acc_ref[...] / l).astype(BF16)
            lse = m_ref[...] + jnp.log(l)
            qa_ref[...] = _put_stats(q_ref[...], lse, STAT_COL_QK)

    qspec = pl.BlockSpec((t, HEAD_PAD), lambda h, qi, ki: (qi, h))
    kspec = pl.BlockSpec((t, HEAD_PAD), lambda h, qi, ki: (jnp.minimum(ki, qi), h))
    return pl.pallas_call(
        body, name="mla_flash_fwd", grid=(MLA_HEADS, n, n),
        in_specs=[qspec, kspec, kspec], out_specs=[qspec, qspec],
        out_shape=[jax.ShapeDtypeStruct(q.shape, BF16), jax.ShapeDtypeStruct(q.shape, BF16)],
        scratch_shapes=[pltpu.VMEM((t, 1), F32), pltpu.VMEM((t, 1), F32), pltpu.VMEM((t, HEAD_PAD), F32)],
        compiler_params=_params(("parallel", "parallel", "arbitrary")),
    )(q, k, v)


def _flash_dq(qa, k, v, doa):
    S = qa.shape[0]
    t = _pick(S, (512, 256, 128))
    n = S // t

    def body(qa_ref, k_ref, v_ref, do_ref, dq_ref, acc_ref):
        qi, ki = pl.program_id(1), pl.program_id(2)

        @pl.when(ki == 0)
        def _():
            acc_ref[...] = jnp.zeros_like(acc_ref)

        def step(diag):
            s = _dot_nt(qa_ref[...], k_ref[...])
            if diag:
                row = lax.broadcasted_iota(jnp.int32, s.shape, 0)
                col = lax.broadcasted_iota(jnp.int32, s.shape, 1)
                s = jnp.where(col <= row, s, NEG)
            p = jnp.exp(s)
            dp = _dot_nt(do_ref[...], v_ref[...])
            acc_ref[...] += _dot_nn((p * dp).astype(BF16), k_ref[...])

        @pl.when(ki < qi)
        def _():
            step(False)

        @pl.when(ki == qi)
        def _():
            step(True)
            dq_ref[...] = acc_ref[...]

    qspec = pl.BlockSpec((t, HEAD_PAD), lambda h, qi, ki: (qi, h))
    kspec = pl.BlockSpec((t, HEAD_PAD), lambda h, qi, ki: (jnp.minimum(ki, qi), h))
    return pl.pallas_call(
        body, name="mla_flash_dq", grid=(MLA_HEADS, n, n),
        in_specs=[qspec, kspec, kspec, qspec], out_specs=qspec,
        out_shape=jax.ShapeDtypeStruct(qa.shape, F32),
        scratch_shapes=[pltpu.VMEM((t, HEAD_PAD), F32)],
        compiler_params=_params(("parallel", "parallel", "arbitrary")),
    )(qa, k, v, doa)


def _flash_dkv(qa, k, v, doa):
    S = qa.shape[0]
    t = _pick(S, (512, 256, 128))
    n = S // t

    def body(qa_ref, k_ref, v_ref, do_ref, dk_ref, dv_ref, dk_acc, dv_acc):
        kj, qi = pl.program_id(1), pl.program_id(2)

        @pl.when(qi == 0)
        def _():
            dk_acc[...] = jnp.zeros_like(dk_acc)
            dv_acc[...] = jnp.zeros_like(dv_acc)

        def step(diag):
            st = _dot_nt(k_ref[...], qa_ref[...])
            if diag:
                row = lax.broadcasted_iota(jnp.int32, st.shape, 0)
                col = lax.broadcasted_iota(jnp.int32, st.shape, 1)
                st = jnp.where(row <= col, st, NEG)
            pt = jnp.exp(st)
            dpt = _dot_nt(v_ref[...], do_ref[...])
            dv_acc[...] += _dot_nn(pt.astype(BF16), do_ref[...])
            dk_acc[...] += _dot_nn((pt * dpt).astype(BF16), qa_ref[...])

        @pl.when(qi > kj)
        def _():
            step(False)

        @pl.when(qi == kj)
        def _():
            step(True)

        @pl.when(qi == n - 1)
        def _():
            dk_ref[...] = dk_acc[...]
            dv_ref[...] = dv_acc[...].astype(BF16)

    kspec = pl.BlockSpec((t, HEAD_PAD), lambda h, kj, qi: (kj, h))
    qspec = pl.BlockSpec((t, HEAD_PAD), lambda h, kj, qi: (jnp.maximum(qi, kj), h))
    return pl.pallas_call(
        body, name="mla_flash_dkv", grid=(MLA_HEADS, n, n),
        in_specs=[qspec, kspec, kspec, qspec], out_specs=[kspec, kspec],
        out_shape=[jax.ShapeDtypeStruct(qa.shape, F32), jax.ShapeDtypeStruct(qa.shape, BF16)],
        scratch_shapes=[pltpu.VMEM((t, HEAD_PAD), F32), pltpu.VMEM((t, HEAD_PAD), F32)],
        compiler_params=_params(("parallel", "parallel", "arbitrary")),
    )(qa, k, v, doa)


def _xattn_fwd(xq, kvx):
    W = X_HEADS * X_HEAD_DIM

    def body(q_ref, kv_ref, o_ref):
        for h in range(X_HEADS):
            lo = h * X_HEAD_DIM
            s = _dot_nt(q_ref[:, lo:lo + X_HEAD_DIM], kv_ref[:, lo:lo + X_HEAD_DIM])
            p = jnp.exp(s - jnp.max(s, axis=-1, keepdims=True))
            p = p / jnp.sum(p, axis=-1, keepdims=True)
            o_ref[:, lo:lo + X_HEAD_DIM] = _dot_nn(p.astype(BF16), kv_ref[:, W + lo:W + lo + X_HEAD_DIM]).astype(BF16)

    return _rows("xattn_fwd", body, [xq], [kvx], [(W, BF16)])[0]


def _xattn_bwd(xq, kvx, dox):
    W = X_HEADS * X_HEAD_DIM
    scale = X_HEAD_DIM ** -0.5

    def body(q_ref, do_ref, kv_ref, dq_ref, dkv_ref):
        @pl.when(pl.program_id(0) == 0)
        def _():
            dkv_ref[...] = jnp.zeros_like(dkv_ref)

        for h in range(X_HEADS):
            lo = h * X_HEAD_DIM
            q, k = q_ref[:, lo:lo + X_HEAD_DIM], kv_ref[:, lo:lo + X_HEAD_DIM]
            v, do = kv_ref[:, W + lo:W + lo + X_HEAD_DIM], do_ref[:, lo:lo + X_HEAD_DIM]
            s = _dot_nt(q, k)
            p = jnp.exp(s - jnp.max(s, axis=-1, keepdims=True))
            p = p / jnp.sum(p, axis=-1, keepdims=True)
            dp = _dot_nt(do, v)
            ds = (p * (dp - jnp.sum(dp * p, axis=-1, keepdims=True))).astype(BF16)
            dq_ref[:, lo:lo + X_HEAD_DIM] = (_dot_nn(ds, k) * scale).astype(BF16)
            dkv_ref[:, lo:lo + X_HEAD_DIM] += _dot_tn(ds, q)
            dkv_ref[:, W + lo:W + lo + X_HEAD_DIM] += _dot_tn(p.astype(BF16), do)

    return _rows("xattn_bwd", body, [xq, dox], [kvx], [(W, BF16)], [kvx.shape])


def _adamw(name, w, g, m, v):
    shape = w.shape
    w2, g2, m2, v2 = [a.reshape(-1, shape[-1]) for a in (w, g, m, v)]
    R, C = w2.shape
    t = _pick(R, (256, 128, 64, 32, 16, 8))
    c1 = 1.0 / (1.0 - ADAM_B1 ** ADAM_STEP)
    c2 = 1.0 / (1.0 - ADAM_B2 ** ADAM_STEP)

    def body(w_ref, g_ref, m_ref, v_ref, d_ref, nm_ref, nv_ref):
        g = g_ref[...]
        nm = ADAM_B1 * m_ref[...] + (1.0 - ADAM_B1) * g
        nv = ADAM_B2 * v_ref[...] + (1.0 - ADAM_B2) * (g * g)
        d_ref[...] = -ADAM_LR * ((nm * c1) / (jnp.sqrt(nv * c2) + ADAM_EPS) + ADAM_WD * w_ref[...])
        nm_ref[...] = nm
        nv_ref[...] = nv

    spec = pl.BlockSpec((t, C), lambda i: (i, 0))
    outs = pl.pallas_call(
        body, name=name, grid=(R // t,), in_specs=[spec] * 4, out_specs=[spec] * 3,
        out_shape=[jax.ShapeDtypeStruct((R, C), F32)] * 3, compiler_params=_params(("parallel",)),
    )(w2, g2, m2, v2)
    return [o.reshape(shape) for o in outs]


def _place():
    x, y, c = lax.axis_index("x"), lax.axis_index("y"), lax.axis_index("c")
    return x, y, c, [(1 - x, y), (x, 1 - y), (1 - x, 1 - y)]


_ANY = pl.BlockSpec(memory_space=pl.ANY)


def _all_gather_chips(shards):
    n = len(shards)

    def body(*refs):
        ins, outs = refs[:n], refs[n:2 * n]
        send, recv, loc = refs[2 * n:]
        x, y, c, chips = _place()
        mine = 2 * x + y
        local, remote = [], []
        for w in range(n):
            lc = pltpu.make_async_copy(ins[w], outs[w].at[mine], loc.at[w])
            lc.start()
            local.append(lc)
            for j, (px, py) in enumerate(chips):
                cp = pltpu.make_async_remote_copy(src_ref=ins[w], dst_ref=outs[w].at[mine], send_sem=send.at[3 * w + j],
                                                  recv_sem=recv.at[3 * w + j], device_id=(px, py, c), device_id_type=MESH)
                cp.start()
                remote.append(cp)
        for w in range(n):
            for j, (px, py) in enumerate(chips):
                pltpu.make_async_remote_copy(src_ref=ins[w], dst_ref=outs[w].at[2 * px + py], send_sem=send.at[3 * w + j],
                                             recv_sem=recv.at[3 * w + j], device_id=(px, py, c), device_id_type=MESH).wait_recv()
        for cp in remote:
            cp.wait_send()
        for lc in local:
            lc.wait()

    return pl.pallas_call(
        body, name="gather_weights", in_specs=[_ANY] * n, out_specs=[_ANY] * n,
        out_shape=[jax.ShapeDtypeStruct((N_CHIPS,) + s.shape, s.dtype) for s in shards],
        scratch_shapes=[pltpu.SemaphoreType.DMA((3 * n,)), pltpu.SemaphoreType.DMA((3 * n,)), pltpu.SemaphoreType.DMA((n,))],
    )(*shards)


def _pair_exchange(pack):
    def body(p_ref, o_ref, send, recv):
        x, y, c, _ = _place()
        cp = pltpu.make_async_remote_copy(src_ref=p_ref.at[1 - c], dst_ref=o_ref, send_sem=send, recv_sem=recv,
                                          device_id=(x, y, 1 - c), device_id_type=MESH)
        cp.start()
        cp.wait()

    return pl.pallas_call(
        body, name="grad_pair_exchange", in_specs=[_ANY], out_specs=_ANY,
        out_shape=jax.ShapeDtypeStruct(pack.shape[1:], pack.dtype),
        scratch_shapes=[pltpu.SemaphoreType.DMA, pltpu.SemaphoreType.DMA],
    )(pack)


def _chip_exchange(part):
    def body(p_ref, o_ref, send, recv, loc):
        x, y, c, chips = _place()
        mine = 2 * x + y
        lc = pltpu.make_async_copy(p_ref.at[mine], o_ref.at[mine], loc)
        lc.start()
        cps = []
        for j, (px, py) in enumerate(chips):
            cp = pltpu.make_async_remote_copy(src_ref=p_ref.at[2 * px + py], dst_ref=o_ref.at[mine], send_sem=send.at[j],
                                              recv_sem=recv.at[j], device_id=(px, py, c), device_id_type=MESH)
            cp.start()
            cps.append(cp)
        for j, (px, py) in enumerate(chips):
            pltpu.make_async_remote_copy(src_ref=p_ref.at[mine], dst_ref=o_ref.at[2 * px + py], send_sem=send.at[j],
                                         recv_sem=recv.at[j], device_id=(px, py, c), device_id_type=MESH).wait_recv()
        for cp in cps:
            cp.wait_send()
        lc.wait()

    return pl.pallas_call(
        body, name="grad_chip_exchange", in_specs=[_ANY], out_specs=_ANY,
        out_shape=jax.ShapeDtypeStruct(part.shape, part.dtype),
        scratch_shapes=[pltpu.SemaphoreType.DMA((3,)), pltpu.SemaphoreType.DMA((3,)), pltpu.SemaphoreType.DMA],
    )(part)


def _pair_share(half):
    def body(h_ref, o_ref, send, recv, loc):
        x, y, c, _ = _place()
        lc = pltpu.make_async_copy(h_ref, o_ref.at[c], loc)
        lc.start()
        cp = pltpu.make_async_remote_copy(src_ref=h_ref, dst_ref=o_ref.at[c], send_sem=send, recv_sem=recv,
                                          device_id=(x, y, 1 - c), device_id_type=MESH)
        cp.start()
        pltpu.make_async_remote_copy(src_ref=h_ref, dst_ref=o_ref.at[1 - c], send_sem=send, recv_sem=recv,
                                     device_id=(x, y, 1 - c), device_id_type=MESH).wait_recv()
        cp.wait_send()
        lc.wait()

    return pl.pallas_call(
        body, name="grad_pair_share", in_specs=[_ANY], out_specs=_ANY,
        out_shape=jax.ShapeDtypeStruct((2,) + half.shape, half.dtype),
        scratch_shapes=[pltpu.SemaphoreType.DMA, pltpu.SemaphoreType.DMA, pltpu.SemaphoreType.DMA],
    )(half)


def _pair_add(pack, got):
    _, nchip, R, L = pack.shape
    t = _pick(R, (1024, 512, 256, 128, 64, 32, 16, 8))
    c = lax.axis_index("c").astype(jnp.int32).reshape(1)

    def body(c_ref, p_ref, g_ref, o_ref):
        o_ref[...] = p_ref[...] + g_ref[...]

    return pl.pallas_call(
        body, name="grad_pair_add",
        grid_spec=pltpu.PrefetchScalarGridSpec(
            num_scalar_prefetch=1, grid=(nchip, R // t),
            in_specs=[pl.BlockSpec((None, None, t, L), lambda k, i, c_ref: (c_ref[0], k, i, 0)),
                      pl.BlockSpec((None, t, L), lambda k, i, c_ref: (k, i, 0))],
            out_specs=pl.BlockSpec((None, t, L), lambda k, i, c_ref: (k, i, 0))),
        out_shape=jax.ShapeDtypeStruct(got.shape, F32), compiler_params=_params(("parallel", "parallel")),
    )(c, pack, got)


def _chip_add(parts):
    _, R, L = parts.shape
    t = _pick(R, (1024, 512, 256, 128, 64, 32, 16, 8))

    def body(p_ref, o_ref):
        o_ref[...] = ((p_ref[0] + p_ref[1]) + p_ref[2]) + p_ref[3]

    return pl.pallas_call(
        body, name="grad_chip_add", grid=(R // t,),
        in_specs=[pl.BlockSpec((N_CHIPS, t, L), lambda i: (0, i, 0))], out_specs=pl.BlockSpec((t, L), lambda i: (i, 0)),
        out_shape=jax.ShapeDtypeStruct((R, L), F32), compiler_params=_params(("parallel",)),
    )(parts)


_CUT = (2 * CONV_CH, 2 * CONV_CH + Q_LORA, 2 * CONV_CH + Q_LORA + KV_LORA, 2 * CONV_CH + Q_LORA + KV_LORA + QK_ROPE)
_KR_AT = _CUT[2] + QK_NOPE


def _pad_last(a, n):
    return jnp.pad(a, [(0, 0)] * (a.ndim - 1) + [(0, n - a.shape[-1])])


def _layout_weights(w):
    w_in = w["w_in"]
    kr = jnp.pad(w_in[:, _CUT[2]:_CUT[3]], ((0, 0), (QK_NOPE, HEAD_PAD - QK_NOPE - QK_ROPE)))
    uq = _pad_last(w["w_uq"].reshape(Q_LORA, MLA_HEADS, QK_NOPE + QK_ROPE), HEAD_PAD).reshape(Q_LORA, MLA_HEADS * HEAD_PAD)
    ukv = w["w_ukv"].reshape(KV_LORA, MLA_HEADS, QK_NOPE + V_DIM)
    uk = _pad_last(ukv[:, :, :QK_NOPE], HEAD_PAD).reshape(KV_LORA, MLA_HEADS * HEAD_PAD)
    uv = _pad_last(ukv[:, :, QK_NOPE:], HEAD_PAD).reshape(KV_LORA, MLA_HEADS * HEAD_PAD)
    mo = jnp.pad(w["w_mla_out"].reshape(MLA_HEADS, V_DIM, D_MODEL), ((0, 0), (0, HEAD_PAD - V_DIM), (0, 0)))
    return dict(
        w_in=jnp.concatenate([w_in[:, :_CUT[2]], kr, w_in[:, _CUT[3]:]], axis=1),
        w_uq=uq, w_ukv=jnp.concatenate([uk, uv], axis=1), w_mla_out=mo.reshape(MLA_HEADS * HEAD_PAD, D_MODEL),
        w_conv_out=w["w_conv_out"], w_out=w["w_out"], w_xq=w["w_xq"], w_xkv=w["w_xkv"], w_xo=w["w_xo"],
        w_mlp1=w["w_mlp1"], w_mlp2=w["w_mlp2"])


def _unlayout_grads(g):
    gi = g["w_in"]
    w_in = jnp.concatenate([gi[:, :_CUT[2]], gi[:, _KR_AT:_KR_AT + QK_ROPE], gi[:, _CUT[2] + HEAD_PAD:]], axis=1)
    uq = g["w_uq"].reshape(Q_LORA, MLA_HEADS, HEAD_PAD)[:, :, :QK_NOPE + QK_ROPE].reshape(Q_LORA, -1)
    gk = g["w_ukv"][:, :MLA_HEADS * HEAD_PAD].reshape(KV_LORA, MLA_HEADS, HEAD_PAD)[:, :, :QK_NOPE]
    gv = g["w_ukv"][:, MLA_HEADS * HEAD_PAD:].reshape(KV_LORA, MLA_HEADS, HEAD_PAD)[:, :, :V_DIM]
    ukv = jnp.concatenate([gk, gv], axis=2).reshape(KV_LORA, -1)
    mo = g["w_mla_out"].reshape(MLA_HEADS, HEAD_PAD, D_MODEL)[:, :V_DIM].reshape(MLA_HEADS * V_DIM, D_MODEL)
    out = dict(g)
    out.update(w_in=w_in, w_uq=uq, w_ukv=ukv, w_mla_out=mo)
    return out


def _rope_tables(positions):
    half = QK_ROPE // 2
    inv_freq = ROPE_THETA ** (-jnp.arange(half, dtype=F32) / half)
    ang = positions.astype(F32).reshape(-1, 1) * inv_freq
    cos, sin = jnp.cos(ang), jnp.sin(ang)
    S = cos.shape[0]
    z16, z32, z64 = jnp.zeros((S, half), F32), jnp.zeros((S, QK_ROPE), F32), jnp.zeros((S, QK_NOPE), F32)
    c = jnp.concatenate([jnp.ones((S, QK_NOPE), F32), cos, cos, z32], axis=1)
    sa = jnp.concatenate([z64, -sin, z16, z32], axis=1)
    sb = jnp.concatenate([z64, z16, sin, z32], axis=1)
    return c, sa, sb


def _local_step(x, mem, positions, target, wl, sm):
    S = x.shape[0]
    HW = MLA_HEADS * HEAD_PAD
    rope_c, rope_sa, rope_sb = _rope_tables(positions)
    qk_scale = (QK_NOPE + QK_ROPE) ** -0.5
    w_in = wl["w_in"]
    c0, c1, c2, c3 = _CUT[0], _CUT[1], _CUT[2], _CUT[2] + HEAD_PAD

    def k_rms1(x_ref, g_ref, u_ref):
        u_ref[...] = _rms_fwd(x_ref[...], g_ref[...]).astype(BF16)

    u1, = _rows("rms_mix", k_rms1, [x], [sm["norm_mix_g"]], [(D_MODEL, BF16)])

    def epi_glu(acc, xs, outs):
        a, gt = acc[:, 0:CONV_CH], acc[:, CONV_CH:2 * CONV_CH]
        outs[0][...] = acc[...]
        outs[1][...] = a * _sigmoid(gt)

    conv_in, z0 = _mm("proj_conv", u1, w_in[:, :c0], "nn", [(2 * CONV_CH, F32), (CONV_CH, F32)], epi_glu)
    c_q = _mm_plain("proj_cq", u1, w_in[:, c0:c1], "nn")
    c_kv = _mm_plain("proj_ckv", u1, w_in[:, c1:c2], "nn")
    kr_raw = _mm_plain("proj_krope", u1, w_in[:, c2:c3], "nn")

    def epi_sigmoid(acc, xs, outs):
        outs[0][...] = _sigmoid(acc[...])

    gates, = _mm("proj_gates", u1, w_in[:, c3:], "nn", [(2 * D_MODEL, F32)], epi_sigmoid)

    z1, z3 = _conv_fwd(z0, sm["conv_w"], sm["conv_b"], sm["conv_ln_g"], sm["conv_ln_b"])
    conv_out = _mm_plain("conv_out", z3, wl["w_conv_out"], "nn")

    def k_lora_norm(cq_ref, ckv_ref, gq_ref, gkv_ref, qn_ref, kvn_ref):
        qn_ref[...] = _rms_fwd(cq_ref[...], gq_ref[...]).astype(BF16)
        kvn_ref[...] = _rms_fwd(ckv_ref[...], gkv_ref[...]).astype(BF16)

    qn, kvn = _rows("lora_norm", k_lora_norm, [c_q, c_kv], [sm["q_norm_g"], sm["kv_norm_g"]],
                    [(Q_LORA, BF16), (KV_LORA, BF16)])

    def epi_q(acc, xs, outs):
        c, sa, sb = xs[0][...], xs[1][...], xs[2][...]
        for h in range(MLA_HEADS):
            lo = h * HEAD_PAD
            outs[0][:, lo:lo + HEAD_PAD] = (_rope(acc[:, lo:lo + HEAD_PAD], c, sa, sb, 1.0) * qk_scale).astype(BF16)

    q_att, = _mm("q_up", qn, wl["w_uq"], "nn", [(HW, BF16)], epi_q, row_x=[rope_c, rope_sa, rope_sb], tn=HW)

    def epi_kv(acc, xs, outs):
        kr = _rope(xs[0][...], xs[1][...], xs[2][...], xs[3][...], 1.0)
        kr = kr + _neg_ones(kr.shape, STAT_COL_QK)
        vconst = _neg_ones(kr.shape, STAT_COL_V)
        for h in range(MLA_HEADS):
            lo = h * HEAD_PAD
            outs[0][:, lo:lo + HEAD_PAD] = (acc[:, lo:lo + HEAD_PAD] + kr).astype(BF16)
            outs[1][:, lo:lo + HEAD_PAD] = (acc[:, HW + lo:HW + lo + HEAD_PAD] + vconst).astype(BF16)

    k_att, v_att = _mm("kv_up", kvn, wl["w_ukv"], "nn", [(HW, BF16), (HW, BF16)], epi_kv,
                       row_x=[kr_raw, rope_c, rope_sa, rope_sb], tn=2 * HW)

    o_att, q_aug = _flash_fwd(q_att, k_att, v_att)
    mla_out = _mm_plain("mla_out", o_att, wl["w_mla_out"], "nn")

    def k_merge(g_ref, co_ref, mo_ref, out_ref):
        out_ref[...] = (g_ref[:, 0:D_MODEL] * co_ref[...] + g_ref[:, D_MODEL:] * mo_ref[...]).astype(BF16)

    merged, = _rows("merge", k_merge, [gates, conv_out, mla_out], [], [(D_MODEL, BF16)], tile=256)

    def epi_res_norm(acc, xs, outs):
        h = xs[0][...] + acc[...]
        outs[0][...] = h
        outs[1][...] = _rms_fwd(h, xs[1][...]).astype(BF16)

    h1, u2 = _mm("mix_out", merged, wl["w_out"], "nn", [(D_MODEL, F32), (D_MODEL, BF16)], epi_res_norm,
                 row_x=[x], vec_x=[sm["norm_xattn_g"]], tn=D_MODEL)

    xscale = X_HEAD_DIM ** -0.5

    def epi_scale(acc, xs, outs):
        outs[0][...] = (acc[...] * xscale).astype(BF16)

    xq, = _mm("xattn_q", u2, wl["w_xq"], "nn", [(X_HEADS * X_HEAD_DIM, BF16)], epi_scale)

    def k_mem_norm(m_ref, g_ref, o_ref):
        o_ref[...] = _rms_fwd(m_ref[...], g_ref[...]).astype(BF16)

    mem_n, = _rows("mem_norm", k_mem_norm, [mem], [sm["norm_mem_g"]], [(D_MODEL, BF16)])
    kvx = _mm_plain("xattn_kv", mem_n, wl["w_xkv"], "nn", dtype=BF16)
    ox = _xattn_fwd(xq, kvx)
    h2, u3 = _mm("xattn_out", ox, wl["w_xo"], "nn", [(D_MODEL, F32), (D_MODEL, BF16)], epi_res_norm,
                 row_x=[h1], vec_x=[sm["norm_mlp_g"]], tn=D_MODEL)

    def epi_relu2(acc, xs, outs):
        r = jnp.maximum(acc[...], 0.0)
        outs[0][...] = (r * r).astype(BF16)

    hid, = _mm("mlp_up", u3, wl["w_mlp1"], "nn", [(D_FF, BF16)], epi_relu2)

    def epi_res(acc, xs, outs):
        outs[0][...] = xs[0][...] + acc[...]

    h3, = _mm("mlp_down", hid, wl["w_mlp2"], "nn", [(D_MODEL, F32)], epi_res, row_x=[h2], tn=D_MODEL)

    def k_final(h_ref, t_ref, g_ref, dh_ref, dhb_ref, loss_ref, dg_ref):
        h, g = h_ref[...], g_ref[...]
        e = _rms_fwd(h, g) - t_ref[...]
        part = 0.5 * jnp.sum(jnp.mean(e * e, axis=-1, keepdims=True), axis=0, keepdims=True)
        _accum(loss_ref, jnp.broadcast_to(part, loss_ref.shape))
        dh, dg = _rms_bwd(h, g, e * (1.0 / D_MODEL))
        dh_ref[...] = dh
        dhb_ref[...] = dh.astype(BF16)
        _accum(dg_ref, dg)

    dh3, dh3b, loss, g_final = _rows("final_loss", k_final, [h3, target], [sm["final_norm_g"]],
                                     [(D_MODEL, F32), (D_MODEL, BF16)], [(1, LANES), (1, D_MODEL)])

    def epi_drelu2(acc, xs, outs):
        outs[0][...] = (acc[...] * (2.0 * jnp.sqrt(xs[0][...].astype(F32)))).astype(BF16)

    da1, = _mm("mlp_down_dx", dh3b, wl["w_mlp2"], "nt", [(D_FF, BF16)], epi_drelu2, tile_x=[hid])
    g_mlp2 = _mm_plain("mlp_down_dw", hid, dh3b, "tn")
    g_mlp1 = _mm_plain("mlp_up_dw", u3, da1, "tn")
    du3 = _mm_plain("mlp_up_dx", da1, wl["w_mlp1"], "nt")

    def k_norm_bwd(x_ref, du_ref, dres_ref, g_ref, dh_ref, dhb_ref, dg_ref):
        dx, dg = _rms_bwd(x_ref[...], g_ref[...], du_ref[...])
        dh = dres_ref[...] + dx
        dh_ref[...] = dh
        dhb_ref[...] = dh.astype(BF16)
        _accum(dg_ref, dg)

    def norm_bwd(name, xin, du, dres, g):
        return _rows(name, k_norm_bwd, [xin, du, dres], [g], [(D_MODEL, F32), (D_MODEL, BF16)], [(1, D_MODEL)], tile=256)

    dh2, dh2b, g_norm_mlp = norm_bwd("norm_mlp_bwd", h2, du3, dh3, sm["norm_mlp_g"])

    dox = _mm_plain("xattn_out_dx", dh2b, wl["w_xo"], "nt", dtype=BF16)
    g_xo = _mm_plain("xattn_out_dw", ox, dh2b, "tn")
    dxq, dkvx = _xattn_bwd(xq, kvx, dox)
    g_xq = _mm_plain("xattn_q_dw", u2, dxq, "tn")
    du2 = _mm_plain("xattn_q_dx", dxq, wl["w_xq"], "nt")
    g_xkv = _mm_plain("xattn_kv_dw", mem_n, dkvx, "tn")
    dmem_n = _mm_plain("xattn_kv_dx", dkvx, wl["w_xkv"], "nt")

    def k_mem_bwd(m_ref, d_ref, g_ref, dg_ref):
        _, dg = _rms_bwd(m_ref[...], g_ref[...], d_ref[...])
        _accum(dg_ref, dg)

    g_norm_mem, = _rows("mem_norm_bwd", k_mem_bwd, [mem, dmem_n], [sm["norm_mem_g"]], [], [(1, D_MODEL)])
    dh1, dh1b, g_norm_xattn = norm_bwd("norm_xattn_bwd", h1, du2, dh2, sm["norm_xattn_g"])

    dmerged = _mm_plain("mix_out_dx", dh1b, wl["w_out"], "nt")
    g_out = _mm_plain("mix_out_dw", merged, dh1b, "tn")

    def k_merge_bwd(dm_ref, g_ref, co_ref, mo_ref, dco_ref, dmo_ref, dgl_ref):
        dm = dm_ref[...]
        g0, g1 = g_ref[:, 0:D_MODEL], g_ref[:, D_MODEL:]
        dco_ref[...] = (dm * g0).astype(BF16)
        dmo_ref[...] = (dm * g1).astype(BF16)
        dgl_ref[:, 0:D_MODEL] = (dm * co_ref[...] * g0 * (1.0 - g0)).astype(BF16)
        dgl_ref[:, D_MODEL:] = (dm * mo_ref[...] * g1 * (1.0 - g1)).astype(BF16)

    dconv_out, dmla_out, dgl = _rows("merge_bwd", k_merge_bwd, [dmerged, gates, conv_out, mla_out], [],
                                     [(D_MODEL, BF16), (D_MODEL, BF16), (2 * D_MODEL, BF16)], tile=256)

    def epi_do(acc, xs, outs):
        for h in range(MLA_HEADS):
            lo = h * HEAD_PAD
            do = acc[:, lo:lo + HEAD_PAD]
            delta = jnp.sum(do * xs[0][:, lo:lo + HEAD_PAD].astype(F32), axis=-1, keepdims=True)
            outs[0][:, lo:lo + HEAD_PAD] = _put_stats(do.astype(BF16), delta, STAT_COL_V)

    do_aug, = _mm("mla_out_dx", dmla_out, wl["w_mla_out"], "nt", [(HW, BF16)], epi_do, row_x=[o_att], tn=HW)
    g_mla_out = _mm_plain("mla_out_dw", o_att, dmla_out, "tn")
    dq_att = _flash_dq(q_aug, k_att, v_att, do_aug)
    dk_att, dv_att = _flash_dkv(q_aug, k_att, v_att, do_aug)

    def k_rope_bwd(dq_ref, dk_ref, dv_ref, c_ref, sa_ref, sb_ref, dqr_ref, dkv_ref, dkr_ref):
        c, sa, sb = c_ref[...], sa_ref[...], sb_ref[...]
        lane = lax.broadcasted_iota(jnp.int32, c.shape, 1)
        nope = (lane < QK_NOPE).astype(F32)
        ropem = ((lane >= QK_NOPE) & (lane < QK_NOPE + QK_ROPE)).astype(F32)
        dkr = jnp.zeros(c.shape, F32)
        for h in range(MLA_HEADS):
            lo = h * HEAD_PAD
            dqr_ref[:, lo:lo + HEAD_PAD] = (_rope(dq_ref[:, lo:lo + HEAD_PAD], c, sa, sb, -1.0) * qk_scale).astype(BF16)
            dk = dk_ref[:, lo:lo + HEAD_PAD]
            dkv_ref[:, lo:lo + HEAD_PAD] = (dk * nope).astype(BF16)
            dkr = dkr + dk
        dkv_ref[:, HW:] = dv_ref[...]
        dkr_ref[...] = (_rope(dkr * ropem, c, sa, sb, -1.0) * ropem).astype(BF16)

    dq_raw, dkv_cat, dkr = _rows("rope_bwd", k_rope_bwd, [dq_att, dk_att, dv_att, rope_c, rope_sa, rope_sb], [],
                                 [(HW, BF16), (2 * HW, BF16), (HEAD_PAD, BF16)], tile=256)
    g_uq = _mm_plain("q_up_dw", qn, dq_raw, "tn")
    dqn = _mm_plain("q_up_dx", dq_raw, wl["w_uq"], "nt")
    g_ukv = _mm_plain("kv_up_dw", kvn, dkv_cat, "tn")
    dkvn = _mm_plain("kv_up_dx", dkv_cat, wl["w_ukv"], "nt")

    def k_lora_bwd(cq_ref, ckv_ref, dqn_ref, dkvn_ref, gq_ref, gkv_ref, dcq_ref, dckv_ref, dgq_ref, dgkv_ref):
        dcq, dgq = _rms_bwd(cq_ref[...], gq_ref[...], dqn_ref[...])
        dckv, dgkv = _rms_bwd(ckv_ref[...], gkv_ref[...], dkvn_ref[...])
        dcq_ref[...] = dcq.astype(BF16)
        dckv_ref[...] = dckv.astype(BF16)
        _accum(dgq_ref, dgq)
        _accum(dgkv_ref, dgkv)

    dc_q, dc_kv, g_q_norm, g_kv_norm = _rows("lora_norm_bwd", k_lora_bwd, [c_q, c_kv, dqn, dkvn],
                                              [sm["q_norm_g"], sm["kv_norm_g"]], [(Q_LORA, BF16), (KV_LORA, BF16)],
                                              [(1, Q_LORA), (1, KV_LORA)])

    dz3 = _mm_plain("conv_out_dx", dconv_out, wl["w_conv_out"], "nt")
    g_conv_out = _mm_plain("conv_out_dw", z3, dconv_out, "tn")
    dz1, g_ln_g, g_ln_b, g_conv_b = _conv_bwd_norm(dz3, z1, sm["conv_ln_g"], sm["conv_ln_b"])
    dconv_in, g_conv_w = _conv_bwd_taps(dz1, z0, conv_in, sm["conv_w"])

    dproj = jnp.concatenate([dconv_in, dc_q, dc_kv, dkr, dgl], axis=1)
    g_in = _mm_plain("proj_dw", u1, dproj, "tn")
    du1 = _mm_plain("proj_dx", dproj, w_in, "nt")

    def k_norm1_bwd(x_ref, du_ref, dres_ref, g_ref, dx_ref, dg_ref):
        dx, dg = _rms_bwd(x_ref[...], g_ref[...], du_ref[...])
        dx_ref[...] = dres_ref[...] + dx
        _accum(dg_ref, dg)

    grad_x, g_norm_mix = _rows("norm_mix_bwd", k_norm1_bwd, [x, du1, dh1], [sm["norm_mix_g"]], [(D_MODEL, F32)],
                               [(1, D_MODEL)], tile=256)

    grads = dict(norm_mix_g=g_norm_mix, w_in=g_in, conv_w=g_conv_w[:CONV_WIDTH], conv_b=g_conv_b, conv_ln_g=g_ln_g,
                 conv_ln_b=g_ln_b, w_conv_out=g_conv_out, q_norm_g=g_q_norm, w_uq=g_uq, kv_norm_g=g_kv_norm, w_ukv=g_ukv,
                 w_mla_out=g_mla_out, w_out=g_out, norm_xattn_g=g_norm_xattn, norm_mem_g=g_norm_mem, w_xq=g_xq,
                 w_xkv=g_xkv, w_xo=g_xo, norm_mlp_g=g_norm_mlp, w_mlp1=g_mlp1, w_mlp2=g_mlp2, final_norm_g=g_final)
    return loss, grad_x, grads


def _shard(a, k, axis):
    n = a.shape[axis] // N_CHIPS
    return lax.slice_in_dim(a, k * n, (k + 1) * n, axis=axis)


def _pack_rows(n_elems):
    rows = -(-n_elems // LANES)
    return -(-rows // 16) * 16


def _pack_grads(grads, loss):
    small = jnp.concatenate([grads[n].reshape(-1) for n in SMALL] + [loss.reshape(-1)[:1]])
    packs = []
    for k in range(N_CHIPS):
        flat = jnp.concatenate([_shard(grads[n], k, SHARD_AXIS[n]).reshape(-1) for n in BIG] + [small])
        rows = _pack_rows(flat.shape[0])
        packs.append(jnp.pad(flat, (0, rows * LANES - flat.shape[0])).reshape(2, rows // 2, LANES))
    return jnp.stack(packs, axis=1)


def _unpack_grads(flat, shard_shapes):
    out, at = {}, 0
    for n in BIG:
        size = math.prod(shard_shapes[n])
        out[n] = flat[at:at + size].reshape(shard_shapes[n])
        at += size
    for n in SMALL:
        size = math.prod(shard_shapes[n])
        out[n] = flat[at:at + size].reshape(shard_shapes[n])
        at += size
    return out, flat[at]


def kernel(x, mem, positions, norm_mix_g, w_in, conv_w, conv_b, conv_ln_g, conv_ln_b, w_conv_out, q_norm_g, w_uq, kv_norm_g, w_ukv, w_mla_out, w_out, norm_xattn_g, norm_mem_g, w_xq, w_xkv, w_xo, norm_mlp_g, w_mlp1, w_mlp2, final_norm_g, loss_target, m_norm_mix_g, m_w_in, m_conv_w, m_conv_b, m_conv_ln_g, m_conv_ln_b, m_w_conv_out, m_q_norm_g, m_w_uq, m_kv_norm_g, m_w_ukv, m_w_mla_out, m_w_out, m_norm_xattn_g, m_norm_mem_g, m_w_xq, m_w_xkv, m_w_xo, m_norm_mlp_g, m_w_mlp1, m_w_mlp2, m_final_norm_g, v_norm_mix_g, v_w_in, v_conv_w, v_conv_b, v_conv_ln_g, v_conv_ln_b, v_w_conv_out, v_q_norm_g, v_w_uq, v_kv_norm_g, v_w_ukv, v_w_mla_out, v_w_out, v_norm_xattn_g, v_norm_mem_g, v_w_xq, v_w_xkv, v_w_xo, v_norm_mlp_g, v_w_mlp1, v_w_mlp2, v_final_norm_g):
    args = dict(locals())
    w = {n: args[n] for n in WEIGHTS}
    m = {n: args["m_" + n] for n in WEIGHTS}
    v = {n: args["v_" + n] for n in WEIGHTS}

    shards = [w[n][0].astype(F32 if n == "conv_w" else BF16) for n in BIG]
    gathered = _all_gather_chips(shards)
    full = {}
    for n, g in zip(BIG, gathered):
        ax = SHARD_AXIS[n]
        full[n] = jnp.moveaxis(g, 0, ax).reshape(g.shape[1:1 + ax] + (N_CHIPS * g.shape[1 + ax],) + g.shape[2 + ax:])
    wl = _layout_weights(full)
    sm = {n: w[n].reshape(1, -1) for n in SMALL}
    sm["conv_w"] = full["conv_w"]

    loss, grad_x, grads = _local_step(x[0], mem[0], positions, loss_target[0], wl, sm)
    grads = _unlayout_grads(grads)

    pack = _pack_grads(grads, loss)
    pair = _pair_add(pack, _pair_exchange(pack))
    half = _chip_add(_chip_exchange(pair))
    flat = _pair_share(half).reshape(-1)
    shard_shapes = {n: w[n].shape[1:] if n in BIG else w[n].shape for n in WEIGHTS}
    g_sum, loss_sum = _unpack_grads(flat, shard_shapes)

    out_g, out_d, out_m, out_v = [], [], [], []
    for n in WEIGHTS:
        g = g_sum[n].reshape(w[n].shape)
        d, nm, nv = _adamw("adamw_" + n, w[n], g, m[n], v[n])
        out_g.append(g)
        out_d.append(d)
        out_m.append(nm)
        out_v.append(nv)
    return (loss_sum, grad_x[None], *out_g, *out_d, *out_m, *out_v)
```

```python
import functools
import math

import jax
import jax.numpy as jnp
from jax import lax
from jax.experimental import pallas as pl
from jax.experimental.pallas import tpu as pltpu

F32 = jnp.float32
BF16 = jnp.bfloat16
MESH = pl.DeviceIdType.MESH

D_MODEL = 1024
CONV_CH = 512
CONV_WIDTH = 31
MLA_HEADS = 8
QK_NOPE = 64
QK_ROPE = 32
V_DIM = 64
Q_LORA = 384
KV_LORA = 256
MEM_LEN = 256
X_HEADS = 4
X_HEAD_DIM = 128
D_FF = 4096
ROPE_THETA = 10000.0
EPS = 1e-6
HEAD_PAD = 128
STAT_COL_QK = QK_NOPE + QK_ROPE
STAT_COL_V = V_DIM
HALO = 32
N_CHIPS = 4
LANES = 128

ADAM_LR = 0.001
ADAM_B1 = 0.9
ADAM_B2 = 0.999
ADAM_EPS = 1e-08
ADAM_WD = 0.01
ADAM_STEP = 10

VMEM_LIMIT = 52 * 1024 * 1024
PACK_TILES = (1536, 1024, 512, 256, 128, 64, 32, 16, 8)
N_DEV = 8
NEG = -1e30

BIG = ["w_in", "w_conv_out", "w_uq", "w_ukv", "w_mla_out", "w_out", "w_xq", "w_xkv", "w_xo", "w_mlp1", "w_mlp2", "conv_w"]
SHARD_AXIS = {"w_in": 1, "w_conv_out": 1, "w_uq": 1, "w_ukv": 1, "w_mla_out": 1, "w_out": 0, "w_xq": 0, "w_xkv": 0,
              "w_xo": 1, "w_mlp1": 1, "w_mlp2": 0, "conv_w": 1}
SMALL = ["norm_mix_g", "conv_b", "conv_ln_g", "conv_ln_b", "q_norm_g", "kv_norm_g", "norm_xattn_g", "norm_mem_g",
         "norm_mlp_g", "final_norm_g"]
WEIGHTS = ["norm_mix_g", "w_in", "conv_w", "conv_b", "conv_ln_g", "conv_ln_b", "w_conv_out", "q_norm_g", "w_uq",
           "kv_norm_g", "w_ukv", "w_mla_out", "w_out", "norm_xattn_g", "norm_mem_g", "w_xq", "w_xkv", "w_xo",
           "norm_mlp_g", "w_mlp1", "w_mlp2", "final_norm_g"]


def _pick(n, prefs):
    for p in prefs:
        if n % p == 0:
            return p
    return n


def _params(sem):
    return pltpu.CompilerParams(dimension_semantics=sem, vmem_limit_bytes=VMEM_LIMIT)


_DIMS = {"nn": (((1,), (0,)), ((), ())), "nt": (((1,), (1,)), ((), ())), "tn": (((0,), (0,)), ((), ()))}


def _mm(name, a, b, mode, outs, epi, row_x=(), tile_x=(), vec_x=(), tm=None, tn=None, tk=None):
    if mode == "nn":
        (M, K), (_, N) = a.shape, b.shape
    elif mode == "nt":
        (M, K), (N, _) = a.shape, b.shape
    else:
        (K, M), (_, N) = a.shape, b.shape
    tm = tm or _pick(M, (512, 384, 256, 128))
    tn = tn or _pick(N, (1024, 768, 512, 384, 256, 128))
    tk = tk or _pick(K, (1024, 768, 512, 384, 256, 128))
    nk = K // tk
    grid = (M // tm, N // tn, nk)
    a_spec = pl.BlockSpec((tk, tm), lambda i, j, k: (k, i)) if mode == "tn" else pl.BlockSpec((tm, tk), lambda i, j, k: (i, k))
    b_spec = pl.BlockSpec((tn, tk), lambda i, j, k: (j, k)) if mode == "nt" else pl.BlockSpec((tk, tn), lambda i, j, k: (k, j))
    in_specs = [a_spec, b_spec]
    in_specs += [pl.BlockSpec((tm, r.shape[1]), lambda i, j, k: (i, 0)) for r in row_x]
    in_specs += [pl.BlockSpec((tm, tn), lambda i, j, k: (i, j)) for _ in tile_x]
    in_specs += [pl.BlockSpec(v.shape, lambda i, j, k: (0, 0)) for v in vec_x]
    out_specs, out_shape = [], []
    for w, dt in outs:
        if tn == N:
            out_specs.append(pl.BlockSpec((tm, w), lambda i, j, k: (i, 0)))
        else:
            assert w == N, (name, w, N)
            out_specs.append(pl.BlockSpec((tm, tn), lambda i, j, k: (i, j)))
        out_shape.append(jax.ShapeDtypeStruct((M, w), dt))
    nx = len(row_x) + len(tile_x) + len(vec_x)
    dims = _DIMS[mode]

    def body(a_ref, b_ref, *rest):
        x_refs, out_refs, acc_ref = rest[:nx], rest[nx:nx + len(outs)], rest[-1]
        av, bv = a_ref[...], b_ref[...]
        if av.dtype != BF16:
            av = av.astype(BF16)
        if bv.dtype != BF16:
            bv = bv.astype(BF16)
        prod = lax.dot_general(av, bv, dims, preferred_element_type=F32)
        if nk == 1:
            acc_ref[...] = prod
            epi(acc_ref, x_refs, out_refs)
        else:
            k = pl.program_id(2)

            @pl.when(k == 0)
            def _():
                acc_ref[...] = prod

            @pl.when(k > 0)
            def _():
                acc_ref[...] += prod

            @pl.when(k == nk - 1)
            def _():
                epi(acc_ref, x_refs, out_refs)

    res = pl.pallas_call(
        body, name=name, grid=grid, in_specs=in_specs, out_specs=out_specs, out_shape=out_shape,
        scratch_shapes=[pltpu.VMEM((tm, tn), F32)],
        compiler_params=_params(("parallel", "parallel", "arbitrary")),
    )(a, b, *row_x, *tile_x, *vec_x)
    return res


def _epi_store(acc_ref, x_refs, out_refs):
    for o in out_refs:
        o[...] = acc_ref[...].astype(o.dtype)


def _mm_plain(name, a, b, mode, dtype=F32, **kw):
    n = b.shape[0] if mode == "nt" else b.shape[1]
    return _mm(name, a, b, mode, [(n, dtype)], _epi_store, **kw)[0]


def _rows(name, body, row_ins, vec_ins, row_outs, acc_outs=(), tile=512):
    S = row_ins[0].shape[0]
    t = _pick(S, (tile, 256, 128, 64, 32, 16, 8))
    in_specs = [pl.BlockSpec((t, r.shape[1]), lambda i: (i, 0)) for r in row_ins]
    in_specs += [pl.BlockSpec(v.shape, lambda i: (0, 0)) for v in vec_ins]
    out_specs = [pl.BlockSpec((t, w), lambda i: (i, 0)) for w, _ in row_outs]
    out_specs += [pl.BlockSpec(shp, lambda i: (0, 0)) for shp in acc_outs]
    out_shape = [jax.ShapeDtypeStruct((S, w), dt) for w, dt in row_outs]
    out_shape += [jax.ShapeDtypeStruct(shp, F32) for shp in acc_outs]
    sem = ("arbitrary",) if acc_outs else ("parallel",)
    return pl.pallas_call(
        functools.partial(body), name=name, grid=(S // t,), in_specs=in_specs, out_specs=out_specs,
        out_shape=out_shape, compiler_params=_params(sem),
    )(*row_ins, *vec_ins)


def _accum(ref, val):
    @pl.when(pl.program_id(0) == 0)
    def _():
        ref[...] = jnp.zeros_like(ref)

    ref[...] += val


def _colsum(v):
    return jnp.sum(v, axis=0, keepdims=True)


def _rms_fwd(x, g):
    r = lax.rsqrt(jnp.mean(x * x, axis=-1, keepdims=True) + EPS)
    return x * r * g


def _rms_bwd(x, g, du):
    r = lax.rsqrt(jnp.mean(x * x, axis=-1, keepdims=True) + EPS)
    xn = x * r
    gdu = du * g
    dx = r * (gdu - xn * jnp.mean(xn * gdu, axis=-1, keepdims=True))
    return dx, _colsum(du * xn)


def _sigmoid(v):
    return 1.0 / (1.0 + jnp.exp(-v))


def _rope(v, c, sa, sb, sign):
    return v * c + sign * (pltpu.roll(v, HEAD_PAD - QK_ROPE // 2, 1) * sa + pltpu.roll(v, QK_ROPE // 2, 1) * sb)


def _split3(v):
    hi = v.astype(BF16)
    r1 = v - hi.astype(F32)
    mid = r1.astype(BF16)
    lo = (r1 - mid.astype(F32)).astype(BF16)
    return hi, mid, lo


def _put_stats(base, stat, col):
    hi, mid, lo = _split3(stat)
    lane = lax.broadcasted_iota(jnp.int32, base.shape, 1)
    out = jnp.where(lane == col, hi, base)
    out = jnp.where(lane == col + 1, mid, out)
    return jnp.where(lane == col + 2, lo, out)


def _neg_ones(shape, col):
    lane = lax.broadcasted_iota(jnp.int32, shape, 1)
    return jnp.where((lane >= col) & (lane < col + 3), -1.0, 0.0).astype(F32)


def _shifted(ext, t):
    p = ext.shape[0]
    for b in range(8):
        rb = ext if b == 0 else pltpu.roll(ext, p - b, 0)
        for a in range(HALO // 8 + 1):
            if 8 * a + b <= HALO:
                yield 8 * a + b, rb[8 * a:8 * a + t]


def _conv_fwd(z0, conv_w, conv_b, ln_g, ln_b):
    S, C = z0.shape
    t = _pick(S, (512, 256, 128, 64, 32))
    per = t // HALO

    def body(cur_ref, prev_ref, w_ref, b_ref, g_ref, beta_ref, z1_ref, z3_ref, ext_ref):
        i = pl.program_id(0)
        ext_ref[0:HALO, :] = jnp.where(i > 0, prev_ref[...], 0.0)
        ext_ref[HALO:, :] = cur_ref[...]
        ext = ext_ref[...]
        acc = jnp.zeros((t, C), F32)
        for d, win in _shifted(ext, t):
            k = d - (HALO - CONV_WIDTH + 1)
            if 0 <= k < CONV_WIDTH:
                acc = acc + win * w_ref[k:k + 1, :]
        z1 = acc + b_ref[...]
        z1_ref[...] = z1
        mu = jnp.mean(z1, axis=-1, keepdims=True)
        zc = z1 - mu
        rs = lax.rsqrt(jnp.mean(zc * zc, axis=-1, keepdims=True) + EPS)
        z2 = zc * rs * g_ref[...] + beta_ref[...]
        z3_ref[...] = (z2 * _sigmoid(z2)).astype(BF16)

    vec = lambda v: pl.BlockSpec(v.shape, lambda i: (0, 0))
    return pl.pallas_call(
        body, name="conv_fwd", grid=(S // t,),
        in_specs=[pl.BlockSpec((t, C), lambda i: (i, 0)),
                  pl.BlockSpec((HALO, C), lambda i: (jnp.maximum(i * per - 1, 0), 0)),
                  vec(conv_w), vec(conv_b), vec(ln_g), vec(ln_b)],
        out_specs=[pl.BlockSpec((t, C), lambda i: (i, 0)), pl.BlockSpec((t, C), lambda i: (i, 0))],
        out_shape=[jax.ShapeDtypeStruct((S, C), F32), jax.ShapeDtypeStruct((S, C), BF16)],
        scratch_shapes=[pltpu.VMEM((t + HALO, C), F32)],
        compiler_params=_params(("parallel",)),
    )(z0, z0, conv_w, conv_b, ln_g, ln_b)


def _conv_bwd_norm(dz3, z1, ln_g, ln_b):
    C = z1.shape[1]

    def body(dz3_ref, z1_ref, g_ref, beta_ref, dz1_ref, dg_ref, dbeta_ref, dbias_ref):
        z1 = z1_ref[...]
        mu = jnp.mean(z1, axis=-1, keepdims=True)
        zc = z1 - mu
        rs = lax.rsqrt(jnp.mean(zc * zc, axis=-1, keepdims=True) + EPS)
        xh = zc * rs
        z2 = xh * g_ref[...] + beta_ref[...]
        sg = _sigmoid(z2)
        dz2 = dz3_ref[...] * (sg * (1.0 + z2 * (1.0 - sg)))
        dxh = dz2 * g_ref[...]
        dz1 = rs * (dxh - jnp.mean(dxh, axis=-1, keepdims=True) - xh * jnp.mean(dxh * xh, axis=-1, keepdims=True))
        dz1_ref[...] = dz1
        _accum(dg_ref, _colsum(dz2 * xh))
        _accum(dbeta_ref, _colsum(dz2))
        _accum(dbias_ref, _colsum(dz1))

    return _rows("conv_bwd_norm", body, [dz3, z1], [ln_g, ln_b], [(C, F32)], [(1, C)] * 3)


def _conv_bwd_taps(dz1, z0, conv_in, conv_w):
    S, C = z0.shape
    t = _pick(S, (512, 256, 128, 64, 32))
    per = t // HALO
    last = S // HALO - 1
    nt = S // t

    def body(dcur_ref, dnext_ref, zcur_ref, zprev_ref, cin_ref, w_ref, dcin_ref, dw_ref, dext_ref, zext_ref):
        i = pl.program_id(0)
        dcur = dcur_ref[...]
        dext_ref[0:t, :] = dcur
        dext_ref[t:, :] = jnp.where(i < nt - 1, dnext_ref[...], 0.0)
        zext_ref[0:HALO, :] = jnp.where(i > 0, zprev_ref[...], 0.0)
        zext_ref[HALO:, :] = zcur_ref[...]

        @pl.when(i == 0)
        def _():
            dw_ref[...] = jnp.zeros_like(dw_ref)

        dz0 = jnp.zeros((t, C), F32)
        for d, win in _shifted(dext_ref[...], t):
            k = CONV_WIDTH - 1 - d
            if 0 <= k < CONV_WIDTH:
                dz0 = dz0 + win * w_ref[k:k + 1, :]
        for d, win in _shifted(zext_ref[...], t):
            k = d - (HALO - CONV_WIDTH + 1)
            if 0 <= k < CONV_WIDTH:
                dw_ref[k:k + 1, :] += _colsum(dcur * win)
        a = cin_ref[:, 0:C]
        sg = _sigmoid(cin_ref[:, C:2 * C])
        dcin_ref[:, 0:C] = (dz0 * sg).astype(BF16)
        dcin_ref[:, C:2 * C] = (dz0 * a * sg * (1.0 - sg)).astype(BF16)

    return pl.pallas_call(
        body, name="conv_bwd_taps", grid=(nt,),
        in_specs=[pl.BlockSpec((t, C), lambda i: (i, 0)),
                  pl.BlockSpec((HALO, C), lambda i: (jnp.minimum((i + 1) * per, last), 0)),
                  pl.BlockSpec((t, C), lambda i: (i, 0)),
                  pl.BlockSpec((HALO, C), lambda i: (jnp.maximum(i * per - 1, 0), 0)),
                  pl.BlockSpec((t, 2 * C), lambda i: (i, 0)),
                  pl.BlockSpec(conv_w.shape, lambda i: (0, 0))],
        out_specs=[pl.BlockSpec((t, 2 * C), lambda i: (i, 0)), pl.BlockSpec((HALO, C), lambda i: (0, 0))],
        out_shape=[jax.ShapeDtypeStruct((S, 2 * C), BF16), jax.ShapeDtypeStruct((HALO, C), F32)],
        scratch_shapes=[pltpu.VMEM((t + HALO, C), F32), pltpu.VMEM((t + HALO, C), F32)],
        compiler_params=_params(("arbitrary",)),
    )(dz1, dz1, z0, z0, conv_in, conv_w)


def _lower_tri(shape, rows_are_queries):
    row = lax.broadcasted_iota(jnp.int32, shape, 0)
    col = lax.broadcasted_iota(jnp.int32, shape, 1)
    return (col <= row) if rows_are_queries else (row <= col)


def _flash_specs(S, t):
    blk = pl.BlockSpec((t, HEAD_PAD), lambda h, i: (i, h))
    head = pl.BlockSpec((S, HEAD_PAD), lambda h, i: (0, h))
    return blk, head


def _dot_nt(a, b):
    return lax.dot_general(a, b, _DIMS["nt"], preferred_element_type=F32)


def _dot_nn(a, b):
    return lax.dot_general(a, b, _DIMS["nn"], preferred_element_type=F32)


def _dot_tn(a, b):
    return lax.dot_general(a, b, _DIMS["tn"], preferred_element_type=F32)


def _flash_fwd(q, k, v):
    S = q.shape[0]
    t = _pick(S, (512, 256, 128))

    def body(q_ref, k_ref, v_ref, o_ref, qa_ref, m_ref, acc_ref):
        qi = pl.program_id(1)
        m_ref[...] = jnp.full_like(m_ref, NEG)
        acc_ref[...] = jnp.zeros_like(acc_ref)
        q = q_ref[...]

        def step(ki, diag):
            rows = pl.ds(pl.multiple_of(ki * t, t), t)
            s = _dot_nt(q, k_ref[rows, :])
            if diag:
                s = jnp.where(_lower_tri(s.shape, True), s, NEG)
            m_old = m_ref[...]
            m_new = jnp.maximum(m_old, jnp.max(s, axis=-1, keepdims=True))
            p = jnp.exp(s - m_new).astype(BF16)
            acc_ref[...] = jnp.exp(m_old - m_new) * acc_ref[...] + _dot_nn(p, v_ref[rows, :])
            m_ref[...] = m_new

        def loop(ki, carry):
            step(ki, False)
            return carry

        lax.fori_loop(0, qi, loop, 0)
        step(qi, True)
        acc = acc_ref[...]
        l = -acc[:, STAT_COL_V:STAT_COL_V + 1]
        o_ref[...] = (acc / l).astype(BF16)
        qa_ref[...] = _put_stats(q, m_ref[...] + jnp.log(l), STAT_COL_QK)

    blk, head = _flash_specs(S, t)
    return pl.pallas_call(
        body, name="mla_flash_fwd", grid=(MLA_HEADS, S // t),
        in_specs=[blk, head, head], out_specs=[blk, blk],
        out_shape=[jax.ShapeDtypeStruct(q.shape, BF16), jax.ShapeDtypeStruct(q.shape, BF16)],
        scratch_shapes=[pltpu.VMEM((t, 1), F32), pltpu.VMEM((t, HEAD_PAD), F32)],
        compiler_params=_params(("parallel", "arbitrary")),
    )(q, k, v)


def _flash_dq(qa, k, v, doa):
    S = qa.shape[0]
    t = _pick(S, (512, 256, 128))

    def body(qa_ref, k_ref, v_ref, do_ref, dq_ref, acc_ref):
        qi = pl.program_id(1)
        acc_ref[...] = jnp.zeros_like(acc_ref)
        qa, do = qa_ref[...], do_ref[...]

        def step(ki, diag):
            rows = pl.ds(pl.multiple_of(ki * t, t), t)
            kk = k_ref[rows, :]
            s = _dot_nt(qa, kk)
            if diag:
                s = jnp.where(_lower_tri(s.shape, True), s, NEG)
            ds = (jnp.exp(s) * _dot_nt(do, v_ref[rows, :])).astype(BF16)
            acc_ref[...] += _dot_nn(ds, kk)

        def loop(ki, carry):
            step(ki, False)
            return carry

        lax.fori_loop(0, qi, loop, 0)
        step(qi, True)
        dq_ref[...] = acc_ref[...]

    blk, head = _flash_specs(S, t)
    return pl.pallas_call(
        body, name="mla_flash_dq", grid=(MLA_HEADS, S // t),
        in_specs=[blk, head, head, blk], out_specs=blk,
        out_shape=jax.ShapeDtypeStruct(qa.shape, F32),
        scratch_shapes=[pltpu.VMEM((t, HEAD_PAD), F32)],
        compiler_params=_params(("parallel", "arbitrary")),
    )(qa, k, v, doa)


def _flash_dkv(qa, k, v, doa):
    S = qa.shape[0]
    t = _pick(S, (512, 256, 128))
    n = S // t

    def body(qa_ref, k_ref, v_ref, do_ref, dk_ref, dv_ref, dk_acc, dv_acc):
        kj = pl.program_id(1)
        dk_acc[...] = jnp.zeros_like(dk_acc)
        dv_acc[...] = jnp.zeros_like(dv_acc)
        kk, vv = k_ref[...], v_ref[...]

        def step(qi, diag):
            rows = pl.ds(pl.multiple_of(qi * t, t), t)
            qa, do = qa_ref[rows, :], do_ref[rows, :]
            st = _dot_nt(kk, qa)
            if diag:
                st = jnp.where(_lower_tri(st.shape, False), st, NEG)
            pt = jnp.exp(st)
            dst = (pt * _dot_nt(vv, do)).astype(BF16)
            dv_acc[...] += _dot_nn(pt.astype(BF16), do)
            dk_acc[...] += _dot_nn(dst, qa)

        def loop(qi, carry):
            step(qi, False)
            return carry

        step(kj, True)
        lax.fori_loop(kj + 1, n, loop, 0)
        dk_ref[...] = dk_acc[...]
        dv_ref[...] = dv_acc[...].astype(BF16)

    blk, head = _flash_specs(S, t)
    return pl.pallas_call(
        body, name="mla_flash_dkv", grid=(MLA_HEADS, n),
        in_specs=[head, blk, blk, head], out_specs=[blk, blk],
        out_shape=[jax.ShapeDtypeStruct(qa.shape, F32), jax.ShapeDtypeStruct(qa.shape, BF16)],
        scratch_shapes=[pltpu.VMEM((t, HEAD_PAD), F32), pltpu.VMEM((t, HEAD_PAD), F32)],
        compiler_params=_params(("parallel", "arbitrary")),
    )(qa, k, v, doa)


def _xattn_fwd(xq, kvx):
    W = X_HEADS * X_HEAD_DIM

    def body(q_ref, kv_ref, o_ref):
        for h in range(X_HEADS):
            lo = h * X_HEAD_DIM
            s = _dot_nt(q_ref[:, lo:lo + X_HEAD_DIM], kv_ref[:, lo:lo + X_HEAD_DIM])
            p = jnp.exp(s - jnp.max(s, axis=-1, keepdims=True))
            p = p / jnp.sum(p, axis=-1, keepdims=True)
            o_ref[:, lo:lo + X_HEAD_DIM] = _dot_nn(p.astype(BF16), kv_ref[:, W + lo:W + lo + X_HEAD_DIM]).astype(BF16)

    return _rows("xattn_fwd", body, [xq], [kvx], [(W, BF16)])[0]


def _xattn_bwd(xq, kvx, dox):
    W = X_HEADS * X_HEAD_DIM
    scale = X_HEAD_DIM ** -0.5

    def body(q_ref, do_ref, kv_ref, dq_ref, dkv_ref):
        @pl.when(pl.program_id(0) == 0)
        def _():
            dkv_ref[...] = jnp.zeros_like(dkv_ref)

        for h in range(X_HEADS):
            lo = h * X_HEAD_DIM
            q, k = q_ref[:, lo:lo + X_HEAD_DIM], kv_ref[:, lo:lo + X_HEAD_DIM]
            v, do = kv_ref[:, W + lo:W + lo + X_HEAD_DIM], do_ref[:, lo:lo + X_HEAD_DIM]
            s = _dot_nt(q, k)
            p = jnp.exp(s - jnp.max(s, axis=-1, keepdims=True))
            p = p / jnp.sum(p, axis=-1, keepdims=True)
            dp = _dot_nt(do, v)
            ds = (p * (dp - jnp.sum(dp * p, axis=-1, keepdims=True))).astype(BF16)
            dq_ref[:, lo:lo + X_HEAD_DIM] = (_dot_nn(ds, k) * scale).astype(BF16)
            dkv_ref[:, lo:lo + X_HEAD_DIM] += _dot_tn(ds, q)
            dkv_ref[:, W + lo:W + lo + X_HEAD_DIM] += _dot_tn(p.astype(BF16), do)

    return _rows("xattn_bwd", body, [xq, dox], [kvx], [(W, BF16)], [kvx.shape])


def _adamw(name, w, g, m, v):
    shape = w.shape
    w2, g2, m2, v2 = [a.reshape(-1, shape[-1]) for a in (w, g, m, v)]
    R, C = w2.shape
    t = _pick(R, (256, 128, 64, 32, 16, 8))
    c1 = 1.0 / (1.0 - ADAM_B1 ** ADAM_STEP)
    c2 = 1.0 / (1.0 - ADAM_B2 ** ADAM_STEP)

    def body(w_ref, g_ref, m_ref, v_ref, d_ref, nm_ref, nv_ref):
        g = g_ref[...]
        nm = ADAM_B1 * m_ref[...] + (1.0 - ADAM_B1) * g
        nv = ADAM_B2 * v_ref[...] + (1.0 - ADAM_B2) * (g * g)
        d_ref[...] = -ADAM_LR * ((nm * c1) / (jnp.sqrt(nv * c2) + ADAM_EPS) + ADAM_WD * w_ref[...])
        nm_ref[...] = nm
        nv_ref[...] = nv

    spec = pl.BlockSpec((t, C), lambda i: (i, 0))
    outs = pl.pallas_call(
        body, name=name, grid=(R // t,), in_specs=[spec] * 4, out_specs=[spec] * 3,
        out_shape=[jax.ShapeDtypeStruct((R, C), F32)] * 3, compiler_params=_params(("parallel",)),
    )(w2, g2, m2, v2)
    return [o.reshape(shape) for o in outs]


def _place():
    x, y, c = lax.axis_index("x"), lax.axis_index("y"), lax.axis_index("c")
    return x, y, c, [(1 - x, y), (x, 1 - y), (1 - x, 1 - y)]


_ANY = pl.BlockSpec(memory_space=pl.ANY)


def _all_gather_chips(shards):
    n = len(shards)

    def body(*refs):
        ins, outs = refs[:n], refs[n:2 * n]
        send, recv, loc = refs[2 * n:]
        x, y, c, chips = _place()
        mine = 2 * x + y
        local, remote = [], []
        for w in range(n):
            lc = pltpu.make_async_copy(ins[w], outs[w].at[mine], loc.at[w])
            lc.start()
            local.append(lc)
            for j, (px, py) in enumerate(chips):
                cp = pltpu.make_async_remote_copy(src_ref=ins[w], dst_ref=outs[w].at[mine], send_sem=send.at[3 * w + j],
                                                  recv_sem=recv.at[3 * w + j], device_id=(px, py, c), device_id_type=MESH)
                cp.start()
                remote.append(cp)
        for w in range(n):
            for j, (px, py) in enumerate(chips):
                pltpu.make_async_remote_copy(src_ref=ins[w], dst_ref=outs[w].at[2 * px + py], send_sem=send.at[3 * w + j],
                                             recv_sem=recv.at[3 * w + j], device_id=(px, py, c), device_id_type=MESH).wait_recv()
        for cp in remote:
            cp.wait_send()
        for lc in local:
            lc.wait()

    return pl.pallas_call(
        body, name="gather_weights", in_specs=[_ANY] * n, out_specs=[_ANY] * n,
        out_shape=[jax.ShapeDtypeStruct((N_CHIPS,) + s.shape, s.dtype) for s in shards],
        scratch_shapes=[pltpu.SemaphoreType.DMA((3 * n,)), pltpu.SemaphoreType.DMA((3 * n,)), pltpu.SemaphoreType.DMA((n,))],
    )(*shards)


def _pair_exchange(pack):
    def body(p_ref, o_ref, send, recv):
        x, y, c, _ = _place()
        cp = pltpu.make_async_remote_copy(src_ref=p_ref.at[1 - c], dst_ref=o_ref, send_sem=send, recv_sem=recv,
                                          device_id=(x, y, 1 - c), device_id_type=MESH)
        cp.start()
        cp.wait()

    return pl.pallas_call(
        body, name="grad_pair_exchange", in_specs=[_ANY], out_specs=_ANY,
        out_shape=jax.ShapeDtypeStruct(pack.shape[1:], pack.dtype),
        scratch_shapes=[pltpu.SemaphoreType.DMA, pltpu.SemaphoreType.DMA],
    )(pack)


def _chip_exchange(part):
    def body(p_ref, o_ref, send, recv, loc):
        x, y, c, chips = _place()
        mine = 2 * x + y
        lc = pltpu.make_async_copy(p_ref.at[mine], o_ref.at[mine], loc)
        lc.start()
        cps = []
        for j, (px, py) in enumerate(chips):
            cp = pltpu.make_async_remote_copy(src_ref=p_ref.at[2 * px + py], dst_ref=o_ref.at[mine], send_sem=send.at[j],
                                              recv_sem=recv.at[j], device_id=(px, py, c), device_id_type=MESH)
            cp.start()
            cps.append(cp)
        for j, (px, py) in enumerate(chips):
            pltpu.make_async_remote_copy(src_ref=p_ref.at[mine], dst_ref=o_ref.at[2 * px + py], send_sem=send.at[j],
                                         recv_sem=recv.at[j], device_id=(px, py, c), device_id_type=MESH).wait_recv()
        for cp in cps:
            cp.wait_send()
        lc.wait()

    return pl.pallas_call(
        body, name="grad_chip_exchange", in_specs=[_ANY], out_specs=_ANY,
        out_shape=jax.ShapeDtypeStruct(part.shape, part.dtype),
        scratch_shapes=[pltpu.SemaphoreType.DMA((3,)), pltpu.SemaphoreType.DMA((3,)), pltpu.SemaphoreType.DMA],
    )(part)


def _pair_share(half):
    def body(h_ref, o_ref, send, recv, loc):
        x, y, c, _ = _place()
        lc = pltpu.make_async_copy(h_ref, o_ref.at[c], loc)
        lc.start()
        cp = pltpu.make_async_remote_copy(src_ref=h_ref, dst_ref=o_ref.at[c], send_sem=send, recv_sem=recv,
                                          device_id=(x, y, 1 - c), device_id_type=MESH)
        cp.start()
        pltpu.make_async_remote_copy(src_ref=h_ref, dst_ref=o_ref.at[1 - c], send_sem=send, recv_sem=recv,
                                     device_id=(x, y, 1 - c), device_id_type=MESH).wait_recv()
        cp.wait_send()
        lc.wait()

    return pl.pallas_call(
        body, name="grad_pair_share", in_specs=[_ANY], out_specs=_ANY,
        out_shape=jax.ShapeDtypeStruct((2,) + half.shape, half.dtype),
        scratch_shapes=[pltpu.SemaphoreType.DMA, pltpu.SemaphoreType.DMA, pltpu.SemaphoreType.DMA],
    )(half)


def _sum_over_devices(block):
    R, L = block.shape

    def gather(b_ref, o_ref, send, recv, loc):
        x, y, c, _ = _place()
        lc = pltpu.make_async_copy(b_ref, o_ref.at[4 * x + 2 * y + c], loc)
        lc.start()
        peers = [(1 - x if dx else x, 1 - y if dy else y, 1 - c if dc else c)
                 for dx in (0, 1) for dy in (0, 1) for dc in (0, 1) if dx or dy or dc]
        cps = []
        for j, peer in enumerate(peers):
            cp = pltpu.make_async_remote_copy(src_ref=b_ref, dst_ref=o_ref.at[4 * x + 2 * y + c], send_sem=send.at[j],
                                              recv_sem=recv.at[j], device_id=peer, device_id_type=MESH)
            cp.start()
            cps.append(cp)
        for j, (px, py, pc) in enumerate(peers):
            pltpu.make_async_remote_copy(src_ref=b_ref, dst_ref=o_ref.at[4 * px + 2 * py + pc], send_sem=send.at[j],
                                         recv_sem=recv.at[j], device_id=(px, py, pc), device_id_type=MESH).wait_recv()
        for cp in cps:
            cp.wait_send()
        lc.wait()

    blocks = pl.pallas_call(
        gather, name="small_grads_gather", in_specs=[_ANY], out_specs=_ANY,
        out_shape=jax.ShapeDtypeStruct((N_DEV, R, L), F32),
        scratch_shapes=[pltpu.SemaphoreType.DMA((N_DEV - 1,)), pltpu.SemaphoreType.DMA((N_DEV - 1,)), pltpu.SemaphoreType.DMA],
    )(block)

    def add(b_ref, o_ref):
        total = b_ref[0]
        for d in range(1, N_DEV):
            total = total + b_ref[d]
        o_ref[...] = total

    return pl.pallas_call(add, name="small_grads_add", out_shape=jax.ShapeDtypeStruct((R, L), F32))(blocks)


def _pair_add(pack, got):
    _, nchip, R, L = pack.shape
    t = _pick(R, PACK_TILES)
    c = lax.axis_index("c").astype(jnp.int32).reshape(1)

    def body(c_ref, p_ref, g_ref, o_ref):
        o_ref[...] = (p_ref[...] + g_ref[...]).astype(BF16)

    return pl.pallas_call(
        body, name="grad_pair_add",
        grid_spec=pltpu.PrefetchScalarGridSpec(
            num_scalar_prefetch=1, grid=(nchip, R // t),
            in_specs=[pl.BlockSpec((None, None, t, L), lambda k, i, c_ref: (c_ref[0], k, i, 0)),
                      pl.BlockSpec((None, t, L), lambda k, i, c_ref: (k, i, 0))],
            out_specs=pl.BlockSpec((None, t, L), lambda k, i, c_ref: (k, i, 0))),
        out_shape=jax.ShapeDtypeStruct(got.shape, BF16), compiler_params=_params(("parallel", "parallel")),
    )(c, pack, got)


def _chip_add(parts):
    _, R, L = parts.shape
    t = _pick(R, PACK_TILES)

    def body(p_ref, o_ref):
        p = [p_ref[k].astype(F32) for k in range(N_CHIPS)]
        o_ref[...] = ((p[0] + p[1]) + p[2]) + p[3]

    return pl.pallas_call(
        body, name="grad_chip_add", grid=(R // t,),
        in_specs=[pl.BlockSpec((N_CHIPS, t, L), lambda i: (0, i, 0))], out_specs=pl.BlockSpec((t, L), lambda i: (i, 0)),
        out_shape=jax.ShapeDtypeStruct((R, L), F32), compiler_params=_params(("parallel",)),
    )(parts)


_CUT = (2 * CONV_CH, 2 * CONV_CH + Q_LORA, 2 * CONV_CH + Q_LORA + KV_LORA, 2 * CONV_CH + Q_LORA + KV_LORA + QK_ROPE)
_KR_AT = _CUT[2] + QK_NOPE


def _pad_last(a, n):
    return jnp.pad(a, [(0, 0)] * (a.ndim - 1) + [(0, n - a.shape[-1])])


def _layout_weights(w):
    w_in = w["w_in"]
    kr = jnp.pad(w_in[:, _CUT[2]:_CUT[3]], ((0, 0), (QK_NOPE, HEAD_PAD - QK_NOPE - QK_ROPE)))
    uq = _pad_last(w["w_uq"].reshape(Q_LORA, MLA_HEADS, QK_NOPE + QK_ROPE), HEAD_PAD).reshape(Q_LORA, MLA_HEADS * HEAD_PAD)
    ukv = w["w_ukv"].reshape(KV_LORA, MLA_HEADS, QK_NOPE + V_DIM)
    uk = _pad_last(ukv[:, :, :QK_NOPE], HEAD_PAD).reshape(KV_LORA, MLA_HEADS * HEAD_PAD)
    uv = _pad_last(ukv[:, :, QK_NOPE:], HEAD_PAD).reshape(KV_LORA, MLA_HEADS * HEAD_PAD)
    mo = jnp.pad(w["w_mla_out"].reshape(MLA_HEADS, V_DIM, D_MODEL), ((0, 0), (0, HEAD_PAD - V_DIM), (0, 0)))
    return dict(
        w_in=jnp.concatenate([w_in[:, :_CUT[2]], kr, w_in[:, _CUT[3]:]], axis=1),
        w_uq=uq, w_ukv=jnp.concatenate([uk, uv], axis=1), w_mla_out=mo.reshape(MLA_HEADS * HEAD_PAD, D_MODEL),
        w_conv_out=w["w_conv_out"], w_out=w["w_out"], w_xq=w["w_xq"], w_xkv=w["w_xkv"], w_xo=w["w_xo"],
        w_mlp1=w["w_mlp1"], w_mlp2=w["w_mlp2"])


def _unlayout_grads(g):
    gi = g["w_in"]
    w_in = jnp.concatenate([gi[:, :_CUT[2]], gi[:, _KR_AT:_KR_AT + QK_ROPE], gi[:, _CUT[2] + HEAD_PAD:]], axis=1)
    uq = g["w_uq"].reshape(Q_LORA, MLA_HEADS, HEAD_PAD)[:, :, :QK_NOPE + QK_ROPE].reshape(Q_LORA, -1)
    gk = g["w_ukv"][:, :MLA_HEADS * HEAD_PAD].reshape(KV_LORA, MLA_HEADS, HEAD_PAD)[:, :, :QK_NOPE]
    gv = g["w_ukv"][:, MLA_HEADS * HEAD_PAD:].reshape(KV_LORA, MLA_HEADS, HEAD_PAD)[:, :, :V_DIM]
    ukv = jnp.concatenate([gk, gv], axis=2).reshape(KV_LORA, -1)
    mo = g["w_mla_out"].reshape(MLA_HEADS, HEAD_PAD, D_MODEL)[:, :V_DIM].reshape(MLA_HEADS * V_DIM, D_MODEL)
    out = dict(g)
    out.update(w_in=w_in, w_uq=uq, w_ukv=ukv, w_mla_out=mo)
    return out


def _rope_tables(positions):
    half = QK_ROPE // 2
    inv_freq = ROPE_THETA ** (-jnp.arange(half, dtype=F32) / half)
    ang = positions.astype(F32).reshape(-1, 1) * inv_freq
    cos, sin = jnp.cos(ang), jnp.sin(ang)
    S = cos.shape[0]
    z16, z32, z64 = jnp.zeros((S, half), F32), jnp.zeros((S, QK_ROPE), F32), jnp.zeros((S, QK_NOPE), F32)
    c = jnp.concatenate([jnp.ones((S, QK_NOPE), F32), cos, cos, z32], axis=1)
    sa = jnp.concatenate([z64, -sin, z16, z32], axis=1)
    sb = jnp.concatenate([z64, z16, sin, z32], axis=1)
    return c, sa, sb


def _local_step(x, mem, positions, target, wl, sm):
    S = x.shape[0]
    HW = MLA_HEADS * HEAD_PAD
    rope_c, rope_sa, rope_sb = _rope_tables(positions)
    qk_scale = (QK_NOPE + QK_ROPE) ** -0.5
    w_in = wl["w_in"]
    c0, c1, c2, c3 = _CUT[0], _CUT[1], _CUT[2], _CUT[2] + HEAD_PAD

    def k_rms1(x_ref, g_ref, u_ref):
        u_ref[...] = _rms_fwd(x_ref[...], g_ref[...]).astype(BF16)

    u1, = _rows("rms_mix", k_rms1, [x], [sm["norm_mix_g"]], [(D_MODEL, BF16)])

    def epi_glu(acc, xs, outs):
        a, gt = acc[:, 0:CONV_CH], acc[:, CONV_CH:2 * CONV_CH]
        outs[0][...] = acc[...]
        outs[1][...] = a * _sigmoid(gt)

    conv_in, z0 = _mm("proj_conv", u1, w_in[:, :c0], "nn", [(2 * CONV_CH, F32), (CONV_CH, F32)], epi_glu)
    c_q = _mm_plain("proj_cq", u1, w_in[:, c0:c1], "nn")
    c_kv = _mm_plain("proj_ckv", u1, w_in[:, c1:c2], "nn")
    kr_raw = _mm_plain("proj_krope", u1, w_in[:, c2:c3], "nn")

    def epi_sigmoid(acc, xs, outs):
        outs[0][...] = _sigmoid(acc[...])

    gates, = _mm("proj_gates", u1, w_in[:, c3:], "nn", [(2 * D_MODEL, F32)], epi_sigmoid)

    z1, z3 = _conv_fwd(z0, sm["conv_w"], sm["conv_b"], sm["conv_ln_g"], sm["conv_ln_b"])
    conv_out = _mm_plain("conv_out", z3, wl["w_conv_out"], "nn")

    def k_lora_norm(cq_ref, ckv_ref, gq_ref, gkv_ref, qn_ref, kvn_ref):
        qn_ref[...] = _rms_fwd(cq_ref[...], gq_ref[...]).astype(BF16)
        kvn_ref[...] = _rms_fwd(ckv_ref[...], gkv_ref[...]).astype(BF16)

    qn, kvn = _rows("lora_norm", k_lora_norm, [c_q, c_kv], [sm["q_norm_g"], sm["kv_norm_g"]],
                    [(Q_LORA, BF16), (KV_LORA, BF16)])

    def epi_q(acc, xs, outs):
        c, sa, sb = xs[0][...], xs[1][...], xs[2][...]
        for h in range(MLA_HEADS):
            lo = h * HEAD_PAD
            outs[0][:, lo:lo + HEAD_PAD] = (_rope(acc[:, lo:lo + HEAD_PAD], c, sa, sb, 1.0) * qk_scale).astype(BF16)

    q_att, = _mm("q_up", qn, wl["w_uq"], "nn", [(HW, BF16)], epi_q, row_x=[rope_c, rope_sa, rope_sb], tn=HW)

    def epi_kv(acc, xs, outs):
        kr = _rope(xs[0][...], xs[1][...], xs[2][...], xs[3][...], 1.0)
        kr = kr + _neg_ones(kr.shape, STAT_COL_QK)
        vconst = _neg_ones(kr.shape, STAT_COL_V)
        for h in range(MLA_HEADS):
            lo = h * HEAD_PAD
            outs[0][:, lo:lo + HEAD_PAD] = (acc[:, lo:lo + HEAD_PAD] + kr).astype(BF16)
            outs[1][:, lo:lo + HEAD_PAD] = (acc[:, HW + lo:HW + lo + HEAD_PAD] + vconst).astype(BF16)

    k_att, v_att = _mm("kv_up", kvn, wl["w_ukv"], "nn", [(HW, BF16), (HW, BF16)], epi_kv,
                       row_x=[kr_raw, rope_c, rope_sa, rope_sb], tn=2 * HW)

    o_att, q_aug = _flash_fwd(q_att, k_att, v_att)
    mla_out = _mm_plain("mla_out", o_att, wl["w_mla_out"], "nn")

    def k_merge(g_ref, co_ref, mo_ref, out_ref):
        out_ref[...] = (g_ref[:, 0:D_MODEL] * co_ref[...] + g_ref[:, D_MODEL:] * mo_ref[...]).astype(BF16)

    merged, = _rows("merge", k_merge, [gates, conv_out, mla_out], [], [(D_MODEL, BF16)], tile=256)

    def epi_res_norm(acc, xs, outs):
        h = xs[0][...] + acc[...]
        outs[0][...] = h
        outs[1][...] = _rms_fwd(h, xs[1][...]).astype(BF16)

    h1, u2 = _mm("mix_out", merged, wl["w_out"], "nn", [(D_MODEL, F32), (D_MODEL, BF16)], epi_res_norm,
                 row_x=[x], vec_x=[sm["norm_xattn_g"]], tn=D_MODEL)

    xscale = X_HEAD_DIM ** -0.5

    def epi_scale(acc, xs, outs):
        outs[0][...] = (acc[...] * xscale).astype(BF16)

    xq, = _mm("xattn_q", u2, wl["w_xq"], "nn", [(X_HEADS * X_HEAD_DIM, BF16)], epi_scale)

    def k_mem_norm(m_ref, g_ref, o_ref):
        o_ref[...] = _rms_fwd(m_ref[...], g_ref[...]).astype(BF16)

    mem_n, = _rows("mem_norm", k_mem_norm, [mem], [sm["norm_mem_g"]], [(D_MODEL, BF16)])
    kvx = _mm_plain("xattn_kv", mem_n, wl["w_xkv"], "nn", dtype=BF16)
    ox = _xattn_fwd(xq, kvx)
    h2, u3 = _mm("xattn_out", ox, wl["w_xo"], "nn", [(D_MODEL, F32), (D_MODEL, BF16)], epi_res_norm,
                 row_x=[h1], vec_x=[sm["norm_mlp_g"]], tn=D_MODEL)

    def epi_relu2(acc, xs, outs):
        r = jnp.maximum(acc[...], 0.0)
        outs[0][...] = (r * r).astype(BF16)

    hid, = _mm("mlp_up", u3, wl["w_mlp1"], "nn", [(D_FF, BF16)], epi_relu2)

    def epi_res(acc, xs, outs):
        outs[0][...] = xs[0][...] + acc[...]

    h3, = _mm("mlp_down", hid, wl["w_mlp2"], "nn", [(D_MODEL, F32)], epi_res, row_x=[h2], tn=D_MODEL)

    def k_final(h_ref, t_ref, g_ref, dh_ref, dhb_ref, loss_ref, dg_ref):
        h, g = h_ref[...], g_ref[...]
        e = _rms_fwd(h, g) - t_ref[...]
        part = 0.5 * jnp.sum(jnp.mean(e * e, axis=-1, keepdims=True), axis=0, keepdims=True)
        _accum(loss_ref, jnp.broadcast_to(part, loss_ref.shape))
        dh, dg = _rms_bwd(h, g, e * (1.0 / D_MODEL))
        dh_ref[...] = dh
        dhb_ref[...] = dh.astype(BF16)
        _accum(dg_ref, dg)

    dh3, dh3b, loss, g_final = _rows("final_loss", k_final, [h3, target], [sm["final_norm_g"]],
                                     [(D_MODEL, F32), (D_MODEL, BF16)], [(1, LANES), (1, D_MODEL)])

    def epi_drelu2(acc, xs, outs):
        outs[0][...] = (acc[...] * (2.0 * jnp.sqrt(xs[0][...].astype(F32)))).astype(BF16)

    da1, = _mm("mlp_down_dx", dh3b, wl["w_mlp2"], "nt", [(D_FF, BF16)], epi_drelu2, tile_x=[hid])
    g_mlp2 = _mm_plain("mlp_down_dw", hid, dh3b, "tn")
    g_mlp1 = _mm_plain("mlp_up_dw", u3, da1, "tn")
    du3 = _mm_plain("mlp_up_dx", da1, wl["w_mlp1"], "nt")

    def k_norm_bwd(x_ref, du_ref, dres_ref, g_ref, dh_ref, dhb_ref, dg_ref):
        dx, dg = _rms_bwd(x_ref[...], g_ref[...], du_ref[...])
        dh = dres_ref[...] + dx
        dh_ref[...] = dh
        dhb_ref[...] = dh.astype(BF16)
        _accum(dg_ref, dg)

    def norm_bwd(name, xin, du, dres, g):
        return _rows(name, k_norm_bwd, [xin, du, dres], [g], [(D_MODEL, F32), (D_MODEL, BF16)], [(1, D_MODEL)], tile=256)

    dh2, dh2b, g_norm_mlp = norm_bwd("norm_mlp_bwd", h2, du3, dh3, sm["norm_mlp_g"])

    dox = _mm_plain("xattn_out_dx", dh2b, wl["w_xo"], "nt", dtype=BF16)
    g_xo = _mm_plain("xattn_out_dw", ox, dh2b, "tn")
    dxq, dkvx = _xattn_bwd(xq, kvx, dox)
    g_xq = _mm_plain("xattn_q_dw", u2, dxq, "tn")
    du2 = _mm_plain("xattn_q_dx", dxq, wl["w_xq"], "nt")
    g_xkv = _mm_plain("xattn_kv_dw", mem_n, dkvx, "tn")
    dmem_n = _mm_plain("xattn_kv_dx", dkvx, wl["w_xkv"], "nt")

    def k_mem_bwd(m_ref, d_ref, g_ref, dg_ref):
        _, dg = _rms_bwd(m_ref[...], g_ref[...], d_ref[...])
        _accum(dg_ref, dg)

    g_norm_mem, = _rows("mem_norm_bwd", k_mem_bwd, [mem, dmem_n], [sm["norm_mem_g"]], [], [(1, D_MODEL)])
    dh1, dh1b, g_norm_xattn = norm_bwd("norm_xattn_bwd", h1, du2, dh2, sm["norm_xattn_g"])

    dmerged = _mm_plain("mix_out_dx", dh1b, wl["w_out"], "nt")
    g_out = _mm_plain("mix_out_dw", merged, dh1b, "tn")

    def k_merge_bwd(dm_ref, g_ref, co_ref, mo_ref, dco_ref, dmo_ref, dgl_ref):
        dm = dm_ref[...]
        g0, g1 = g_ref[:, 0:D_MODEL], g_ref[:, D_MODEL:]
        dco_ref[...] = (dm * g0).astype(BF16)
        dmo_ref[...] = (dm * g1).astype(BF16)
        dgl_ref[:, 0:D_MODEL] = (dm * co_ref[...] * g0 * (1.0 - g0)).astype(BF16)
        dgl_ref[:, D_MODEL:] = (dm * mo_ref[...] * g1 * (1.0 - g1)).astype(BF16)

    dconv_out, dmla_out, dgl = _rows("merge_bwd", k_merge_bwd, [dmerged, gates, conv_out, mla_out], [],
                                     [(D_MODEL, BF16), (D_MODEL, BF16), (2 * D_MODEL, BF16)], tile=256)

    def epi_do(acc, xs, outs):
        for h in range(MLA_HEADS):
            lo = h * HEAD_PAD
            do = acc[:, lo:lo + HEAD_PAD]
            delta = jnp.sum(do * xs[0][:, lo:lo + HEAD_PAD].astype(F32), axis=-1, keepdims=True)
            outs[0][:, lo:lo + HEAD_PAD] = _put_stats(do.astype(BF16), delta, STAT_COL_V)

    do_aug, = _mm("mla_out_dx", dmla_out, wl["w_mla_out"], "nt", [(HW, BF16)], epi_do, row_x=[o_att], tn=HW)
    g_mla_out = _mm_plain("mla_out_dw", o_att, dmla_out, "tn")
    dq_att = _flash_dq(q_aug, k_att, v_att, do_aug)
    dk_att, dv_att = _flash_dkv(q_aug, k_att, v_att, do_aug)

    def k_rope_bwd(dq_ref, dk_ref, dv_ref, c_ref, sa_ref, sb_ref, dqr_ref, dkv_ref, dkr_ref):
        c, sa, sb = c_ref[...], sa_ref[...], sb_ref[...]
        lane = lax.broadcasted_iota(jnp.int32, c.shape, 1)
        nope = (lane < QK_NOPE).astype(F32)
        ropem = ((lane >= QK_NOPE) & (lane < QK_NOPE + QK_ROPE)).astype(F32)
        dkr = jnp.zeros(c.shape, F32)
        for h in range(MLA_HEADS):
            lo = h * HEAD_PAD
            dqr_ref[:, lo:lo + HEAD_PAD] = (_rope(dq_ref[:, lo:lo + HEAD_PAD], c, sa, sb, -1.0) * qk_scale).astype(BF16)
            dk = dk_ref[:, lo:lo + HEAD_PAD]
            dkv_ref[:, lo:lo + HEAD_PAD] = (dk * nope).astype(BF16)
            dkr = dkr + dk
        dkv_ref[:, HW:] = dv_ref[...]
        dkr_ref[...] = (_rope(dkr * ropem, c, sa, sb, -1.0) * ropem).astype(BF16)

    dq_raw, dkv_cat, dkr = _rows("rope_bwd", k_rope_bwd, [dq_att, dk_att, dv_att, rope_c, rope_sa, rope_sb], [],
                                 [(HW, BF16), (2 * HW, BF16), (HEAD_PAD, BF16)], tile=256)
    g_uq = _mm_plain("q_up_dw", qn, dq_raw, "tn")
    dqn = _mm_plain("q_up_dx", dq_raw, wl["w_uq"], "nt")
    g_ukv = _mm_plain("kv_up_dw", kvn, dkv_cat, "tn")
    dkvn = _mm_plain("kv_up_dx", dkv_cat, wl["w_ukv"], "nt")

    def k_lora_bwd(cq_ref, ckv_ref, dqn_ref, dkvn_ref, gq_ref, gkv_ref, dcq_ref, dckv_ref, dgq_ref, dgkv_ref):
        dcq, dgq = _rms_bwd(cq_ref[...], gq_ref[...], dqn_ref[...])
        dckv, dgkv = _rms_bwd(ckv_ref[...], gkv_ref[...], dkvn_ref[...])
        dcq_ref[...] = dcq.astype(BF16)
        dckv_ref[...] = dckv.astype(BF16)
        _accum(dgq_ref, dgq)
        _accum(dgkv_ref, dgkv)

    dc_q, dc_kv, g_q_norm, g_kv_norm = _rows("lora_norm_bwd", k_lora_bwd, [c_q, c_kv, dqn, dkvn],
                                              [sm["q_norm_g"], sm["kv_norm_g"]], [(Q_LORA, BF16), (KV_LORA, BF16)],
                                              [(1, Q_LORA), (1, KV_LORA)])

    dz3 = _mm_plain("conv_out_dx", dconv_out, wl["w_conv_out"], "nt")
    g_conv_out = _mm_plain("conv_out_dw", z3, dconv_out, "tn")
    dz1, g_ln_g, g_ln_b, g_conv_b = _conv_bwd_norm(dz3, z1, sm["conv_ln_g"], sm["conv_ln_b"])
    dconv_in, g_conv_w = _conv_bwd_taps(dz1, z0, conv_in, sm["conv_w"])

    dproj = jnp.concatenate([dconv_in, dc_q, dc_kv, dkr, dgl], axis=1)
    g_in = _mm_plain("proj_dw", u1, dproj, "tn")
    du1 = _mm_plain("proj_dx", dproj, w_in, "nt")

    def k_norm1_bwd(x_ref, du_ref, dres_ref, g_ref, dx_ref, dg_ref):
        dx, dg = _rms_bwd(x_ref[...], g_ref[...], du_ref[...])
        dx_ref[...] = dres_ref[...] + dx
        _accum(dg_ref, dg)

    grad_x, g_norm_mix = _rows("norm_mix_bwd", k_norm1_bwd, [x, du1, dh1], [sm["norm_mix_g"]], [(D_MODEL, F32)],
                               [(1, D_MODEL)], tile=256)

    grads = dict(norm_mix_g=g_norm_mix, w_in=g_in, conv_w=g_conv_w[:CONV_WIDTH], conv_b=g_conv_b, conv_ln_g=g_ln_g,
                 conv_ln_b=g_ln_b, w_conv_out=g_conv_out, q_norm_g=g_q_norm, w_uq=g_uq, kv_norm_g=g_kv_norm, w_ukv=g_ukv,
                 w_mla_out=g_mla_out, w_out=g_out, norm_xattn_g=g_norm_xattn, norm_mem_g=g_norm_mem, w_xq=g_xq,
                 w_xkv=g_xkv, w_xo=g_xo, norm_mlp_g=g_norm_mlp, w_mlp1=g_mlp1, w_mlp2=g_mlp2, final_norm_g=g_final)
    return loss, grad_x, grads


def _shard(a, k, axis):
    n = a.shape[axis] // N_CHIPS
    return lax.slice_in_dim(a, k * n, (k + 1) * n, axis=axis)


def _pack_rows(n_elems):
    half = -(-n_elems // (2 * LANES))
    return 2 * (-(-half // PACK_TILES[0]) * PACK_TILES[0])


def _pack_small(grads, loss):
    flat = jnp.concatenate([grads[n].reshape(-1) for n in SMALL] + [loss.reshape(-1)[:1]])
    rows = -(-flat.shape[0] // (8 * LANES)) * 8
    return jnp.pad(flat, (0, rows * LANES - flat.shape[0])).reshape(rows, LANES)


def _pack_grads(grads):
    packs = []
    for k in range(N_CHIPS):
        flat = jnp.concatenate([_shard(grads[n], k, SHARD_AXIS[n]).reshape(-1) for n in BIG])
        rows = _pack_rows(flat.shape[0])
        packs.append(jnp.pad(flat, (0, rows * LANES - flat.shape[0])).reshape(2, rows // 2, LANES))
    return jnp.stack(packs, axis=1)


def _unpack(flat, names, shapes):
    out, at = {}, 0
    for n in names:
        size = math.prod(shapes[n])
        out[n] = flat[at:at + size].reshape(shapes[n])
        at += size
    return out, at


def kernel(x, mem, positions, norm_mix_g, w_in, conv_w, conv_b, conv_ln_g, conv_ln_b, w_conv_out, q_norm_g, w_uq, kv_norm_g, w_ukv, w_mla_out, w_out, norm_xattn_g, norm_mem_g, w_xq, w_xkv, w_xo, norm_mlp_g, w_mlp1, w_mlp2, final_norm_g, loss_target, m_norm_mix_g, m_w_in, m_conv_w, m_conv_b, m_conv_ln_g, m_conv_ln_b, m_w_conv_out, m_q_norm_g, m_w_uq, m_kv_norm_g, m_w_ukv, m_w_mla_out, m_w_out, m_norm_xattn_g, m_norm_mem_g, m_w_xq, m_w_xkv, m_w_xo, m_norm_mlp_g, m_w_mlp1, m_w_mlp2, m_final_norm_g, v_norm_mix_g, v_w_in, v_conv_w, v_conv_b, v_conv_ln_g, v_conv_ln_b, v_w_conv_out, v_q_norm_g, v_w_uq, v_kv_norm_g, v_w_ukv, v_w_mla_out, v_w_out, v_norm_xattn_g, v_norm_mem_g, v_w_xq, v_w_xkv, v_w_xo, v_norm_mlp_g, v_w_mlp1, v_w_mlp2, v_final_norm_g):
    args = dict(locals())
    w = {n: args[n] for n in WEIGHTS}
    m = {n: args["m_" + n] for n in WEIGHTS}
    v = {n: args["v_" + n] for n in WEIGHTS}

    shards = [w[n][0].astype(F32 if n == "conv_w" else BF16) for n in BIG]
    gathered = _all_gather_chips(shards)
    full = {}
    for n, g in zip(BIG, gathered):
        ax = SHARD_AXIS[n]
        full[n] = jnp.moveaxis(g, 0, ax).reshape(g.shape[1:1 + ax] + (N_CHIPS * g.shape[1 + ax],) + g.shape[2 + ax:])
    wl = _layout_weights(full)
    sm = {n: w[n].reshape(1, -1) for n in SMALL}
    sm["conv_w"] = full["conv_w"]

    loss, grad_x, grads = _local_step(x[0], mem[0], positions, loss_target[0], wl, sm)
    grads = _unlayout_grads(grads)

    small_flat = _sum_over_devices(_pack_small(grads, loss)).reshape(-1)
    pack = _pack_grads(grads)
    pair = _pair_add(pack, _pair_exchange(pack))
    half = _chip_add(_chip_exchange(pair))
    flat = _pair_share(half).reshape(-1)
    shapes = {n: w[n].shape[1:] if n in BIG else w[n].shape for n in WEIGHTS}
    g_sum, _ = _unpack(flat, BIG, shapes)
    g_small, at = _unpack(small_flat, SMALL, shapes)
    g_sum.update(g_small)
    loss_sum = small_flat[at]

    out_g, out_d, out_m, out_v = [], [], [], []
    for n in WEIGHTS:
        g = g_sum[n].reshape(w[n].shape)
        d, nm, nv = _adamw("adamw_" + n, w[n], g, m[n], v[n])
        out_g.append(g)
        out_d.append(d)
        out_m.append(nm)
        out_v.append(nv)
    return (loss_sum, grad_x[None], *out_g, *out_d, *out_m, *out_v)
```

```python
import functools
import math

import jax
import jax.numpy as jnp
from jax import lax
from jax.experimental import pallas as pl
from jax.experimental.pallas import tpu as pltpu

F32 = jnp.float32
BF16 = jnp.bfloat16
MESH = pl.DeviceIdType.MESH

D_MODEL = 1024
CONV_CH = 512
CONV_WIDTH = 31
MLA_HEADS = 8
QK_NOPE = 64
QK_ROPE = 32
V_DIM = 64
Q_LORA = 384
KV_LORA = 256
MEM_LEN = 256
X_HEADS = 4
X_HEAD_DIM = 128
D_FF = 4096
ROPE_THETA = 10000.0
EPS = 1e-6
HEAD_PAD = 128
STAT_COL_QK = QK_NOPE + QK_ROPE
STAT_COL_V = V_DIM
HALO = 32
N_CHIPS = 4
LANES = 128

ADAM_LR = 0.001
ADAM_B1 = 0.9
ADAM_B2 = 0.999
ADAM_EPS = 1e-08
ADAM_WD = 0.01
ADAM_STEP = 10

VMEM_LIMIT = 52 * 1024 * 1024
PACK_TILES = (1536, 1024, 512, 256, 128, 64, 32, 16, 8)
N_DEV = 8
NEG = -1e30

BIG = ["w_in", "w_conv_out", "w_uq", "w_ukv", "w_mla_out", "w_out", "w_xq", "w_xkv", "w_xo", "w_mlp1", "w_mlp2", "conv_w"]
SHARD_AXIS = {"w_in": 1, "w_conv_out": 1, "w_uq": 1, "w_ukv": 1, "w_mla_out": 1, "w_out": 0, "w_xq": 0, "w_xkv": 0,
              "w_xo": 1, "w_mlp1": 1, "w_mlp2": 0, "conv_w": 1}
SMALL = ["norm_mix_g", "conv_b", "conv_ln_g", "conv_ln_b", "q_norm_g", "kv_norm_g", "norm_xattn_g", "norm_mem_g",
         "norm_mlp_g", "final_norm_g"]
WEIGHTS = ["norm_mix_g", "w_in", "conv_w", "conv_b", "conv_ln_g", "conv_ln_b", "w_conv_out", "q_norm_g", "w_uq",
           "kv_norm_g", "w_ukv", "w_mla_out", "w_out", "norm_xattn_g", "norm_mem_g", "w_xq", "w_xkv", "w_xo",
           "norm_mlp_g", "w_mlp1", "w_mlp2", "final_norm_g"]


def _pick(n, prefs):
    for p in prefs:
        if n % p == 0:
            return p
    return n


def _params(sem):
    return pltpu.CompilerParams(dimension_semantics=sem, vmem_limit_bytes=VMEM_LIMIT)


_DIMS = {"nn": (((1,), (0,)), ((), ())), "nt": (((1,), (1,)), ((), ())), "tn": (((0,), (0,)), ((), ()))}


def _mm(name, a, b, mode, outs, epi, row_x=(), tile_x=(), vec_x=(), tm=None, tn=None, tk=None):
    if mode == "nn":
        (M, K), (_, N) = a.shape, b.shape
    elif mode == "nt":
        (M, K), (N, _) = a.shape, b.shape
    else:
        (K, M), (_, N) = a.shape, b.shape
    tm = tm or _pick(M, (1024, 512, 384, 256, 128))
    tn = tn or _pick(N, (1024, 768, 512, 384, 256, 128))
    tk = tk or _pick(K, (1024, 768, 512, 384, 256, 128))
    nk = K // tk
    rows_inner = nk == 1 and N // tn > 1
    grid = (N // tn, M // tm, nk) if rows_inner else (M // tm, N // tn, nk)

    def spec(shape, f):
        return pl.BlockSpec(shape, (lambda j, i, k: f(i, j, k)) if rows_inner else f)

    a_spec = spec((tk, tm), lambda i, j, k: (k, i)) if mode == "tn" else spec((tm, tk), lambda i, j, k: (i, k))
    b_spec = spec((tn, tk), lambda i, j, k: (j, k)) if mode == "nt" else spec((tk, tn), lambda i, j, k: (k, j))
    in_specs = [a_spec, b_spec]
    in_specs += [spec((tm, r.shape[1]), lambda i, j, k: (i, 0)) for r in row_x]
    in_specs += [spec((tm, tn), lambda i, j, k: (i, j)) for _ in tile_x]
    in_specs += [spec(v.shape, lambda i, j, k: (0, 0)) for v in vec_x]
    out_specs, out_shape = [], []
    for w, dt in outs:
        if tn == N:
            out_specs.append(spec((tm, w), lambda i, j, k: (i, 0)))
        else:
            assert w == N, (name, w, N)
            out_specs.append(spec((tm, tn), lambda i, j, k: (i, j)))
        out_shape.append(jax.ShapeDtypeStruct((M, w), dt))
    nx = len(row_x) + len(tile_x) + len(vec_x)
    dims = _DIMS[mode]

    def body(a_ref, b_ref, *rest):
        x_refs, out_refs, acc_ref = rest[:nx], rest[nx:nx + len(outs)], rest[-1]
        av, bv = a_ref[...], b_ref[...]
        if av.dtype != BF16:
            av = av.astype(BF16)
        if bv.dtype != BF16:
            bv = bv.astype(BF16)
        prod = lax.dot_general(av, bv, dims, preferred_element_type=F32)
        if nk == 1:
            acc_ref[...] = prod
            epi(acc_ref, x_refs, out_refs)
        else:
            k = pl.program_id(2)

            @pl.when(k == 0)
            def _():
                acc_ref[...] = prod

            @pl.when(k > 0)
            def _():
                acc_ref[...] += prod

            @pl.when(k == nk - 1)
            def _():
                epi(acc_ref, x_refs, out_refs)

    res = pl.pallas_call(
        body, name=name, grid=grid, in_specs=in_specs, out_specs=out_specs, out_shape=out_shape,
        scratch_shapes=[pltpu.VMEM((tm, tn), F32)],
        compiler_params=_params(("parallel", "parallel", "arbitrary")),
    )(a, b, *row_x, *tile_x, *vec_x)
    return res


def _epi_store(acc_ref, x_refs, out_refs):
    for o in out_refs:
        o[...] = acc_ref[...].astype(o.dtype)


def _mm_plain(name, a, b, mode, dtype=F32, **kw):
    n = b.shape[0] if mode == "nt" else b.shape[1]
    return _mm(name, a, b, mode, [(n, dtype)], _epi_store, **kw)[0]


def _rows(name, body, row_ins, vec_ins, row_outs, acc_outs=(), tile=512):
    S = row_ins[0].shape[0]
    t = _pick(S, (tile, 256, 128, 64, 32, 16, 8))
    in_specs = [pl.BlockSpec((t, r.shape[1]), lambda i: (i, 0)) for r in row_ins]
    in_specs += [pl.BlockSpec(v.shape, lambda i: (0, 0)) for v in vec_ins]
    out_specs = [pl.BlockSpec((t, w), lambda i: (i, 0)) for w, _ in row_outs]
    out_specs += [pl.BlockSpec(shp, lambda i: (0, 0)) for shp in acc_outs]
    out_shape = [jax.ShapeDtypeStruct((S, w), dt) for w, dt in row_outs]
    out_shape += [jax.ShapeDtypeStruct(shp, F32) for shp in acc_outs]
    sem = ("arbitrary",) if acc_outs else ("parallel",)
    return pl.pallas_call(
        functools.partial(body), name=name, grid=(S // t,), in_specs=in_specs, out_specs=out_specs,
        out_shape=out_shape, compiler_params=_params(sem),
    )(*row_ins, *vec_ins)


def _accum(ref, val):
    @pl.when(pl.program_id(0) == 0)
    def _():
        ref[...] = jnp.zeros_like(ref)

    ref[...] += val


def _colsum(v):
    return jnp.sum(v, axis=0, keepdims=True)


def _rms_fwd(x, g):
    r = lax.rsqrt(jnp.mean(x * x, axis=-1, keepdims=True) + EPS)
    return x * r * g


def _rms_bwd(x, g, du):
    r = lax.rsqrt(jnp.mean(x * x, axis=-1, keepdims=True) + EPS)
    xn = x * r
    gdu = du * g
    dx = r * (gdu - xn * jnp.mean(xn * gdu, axis=-1, keepdims=True))
    return dx, _colsum(du * xn)


def _sigmoid(v):
    return 1.0 / (1.0 + jnp.exp(-v))


def _rope(v, c, sa, sb, sign):
    return v * c + sign * (pltpu.roll(v, HEAD_PAD - QK_ROPE // 2, 1) * sa + pltpu.roll(v, QK_ROPE // 2, 1) * sb)


def _split3(v):
    hi = v.astype(BF16)
    r1 = v - hi.astype(F32)
    mid = r1.astype(BF16)
    lo = (r1 - mid.astype(F32)).astype(BF16)
    return hi, mid, lo


def _put_stats(base, stat, col):
    hi, mid, lo = _split3(stat)
    lane = lax.broadcasted_iota(jnp.int32, base.shape, 1)
    out = jnp.where(lane == col, hi, base)
    out = jnp.where(lane == col + 1, mid, out)
    return jnp.where(lane == col + 2, lo, out)


def _neg_ones(shape, col):
    lane = lax.broadcasted_iota(jnp.int32, shape, 1)
    return jnp.where((lane >= col) & (lane < col + 3), -1.0, 0.0).astype(F32)


def _shifted(ext, t):
    p = ext.shape[0]
    for b in range(8):
        rb = ext if b == 0 else pltpu.roll(ext, p - b, 0)
        for a in range(HALO // 8 + 1):
            if 8 * a + b <= HALO:
                yield 8 * a + b, rb[8 * a:8 * a + t]


def _conv_fwd(z0, conv_w, conv_b, ln_g, ln_b):
    S, C = z0.shape
    t = _pick(S, (512, 256, 128, 64, 32))
    per = t // HALO

    def body(cur_ref, prev_ref, w_ref, b_ref, g_ref, beta_ref, z1_ref, z3_ref, ext_ref):
        i = pl.program_id(0)
        ext_ref[0:HALO, :] = jnp.where(i > 0, prev_ref[...], 0.0)
        ext_ref[HALO:, :] = cur_ref[...]
        ext = ext_ref[...]
        acc = jnp.zeros((t, C), F32)
        for d, win in _shifted(ext, t):
            k = d - (HALO - CONV_WIDTH + 1)
            if 0 <= k < CONV_WIDTH:
                acc = acc + win * w_ref[k:k + 1, :]
        z1 = acc + b_ref[...]
        z1_ref[...] = z1
        mu = jnp.mean(z1, axis=-1, keepdims=True)
        zc = z1 - mu
        rs = lax.rsqrt(jnp.mean(zc * zc, axis=-1, keepdims=True) + EPS)
        z2 = zc * rs * g_ref[...] + beta_ref[...]
        z3_ref[...] = (z2 * _sigmoid(z2)).astype(BF16)

    vec = lambda v: pl.BlockSpec(v.shape, lambda i: (0, 0))
    return pl.pallas_call(
        body, name="conv_fwd", grid=(S // t,),
        in_specs=[pl.BlockSpec((t, C), lambda i: (i, 0)),
                  pl.BlockSpec((HALO, C), lambda i: (jnp.maximum(i * per - 1, 0), 0)),
                  vec(conv_w), vec(conv_b), vec(ln_g), vec(ln_b)],
        out_specs=[pl.BlockSpec((t, C), lambda i: (i, 0)), pl.BlockSpec((t, C), lambda i: (i, 0))],
        out_shape=[jax.ShapeDtypeStruct((S, C), F32), jax.ShapeDtypeStruct((S, C), BF16)],
        scratch_shapes=[pltpu.VMEM((t + HALO, C), F32)],
        compiler_params=_params(("parallel",)),
    )(z0, z0, conv_w, conv_b, ln_g, ln_b)


def _conv_bwd_norm(dz3, z1, ln_g, ln_b):
    C = z1.shape[1]

    def body(dz3_ref, z1_ref, g_ref, beta_ref, dz1_ref, dg_ref, dbeta_ref, dbias_ref):
        z1 = z1_ref[...]
        mu = jnp.mean(z1, axis=-1, keepdims=True)
        zc = z1 - mu
        rs = lax.rsqrt(jnp.mean(zc * zc, axis=-1, keepdims=True) + EPS)
        xh = zc * rs
        z2 = xh * g_ref[...] + beta_ref[...]
        sg = _sigmoid(z2)
        dz2 = dz3_ref[...] * (sg * (1.0 + z2 * (1.0 - sg)))
        dxh = dz2 * g_ref[...]
        dz1 = rs * (dxh - jnp.mean(dxh, axis=-1, keepdims=True) - xh * jnp.mean(dxh * xh, axis=-1, keepdims=True))
        dz1_ref[...] = dz1
        _accum(dg_ref, _colsum(dz2 * xh))
        _accum(dbeta_ref, _colsum(dz2))
        _accum(dbias_ref, _colsum(dz1))

    return _rows("conv_bwd_norm", body, [dz3, z1], [ln_g, ln_b], [(C, F32)], [(1, C)] * 3)


def _conv_bwd_taps(dz1, z0, conv_in, conv_w):
    S, C = z0.shape
    t = _pick(S, (512, 256, 128, 64, 32))
    per = t // HALO
    last = S // HALO - 1
    nt = S // t

    def body(dcur_ref, dnext_ref, zcur_ref, zprev_ref, cin_ref, w_ref, dcin_ref, dw_ref, dext_ref, zext_ref):
        i = pl.program_id(0)
        dcur = dcur_ref[...]
        dext_ref[0:t, :] = dcur
        dext_ref[t:, :] = jnp.where(i < nt - 1, dnext_ref[...], 0.0)
        zext_ref[0:HALO, :] = jnp.where(i > 0, zprev_ref[...], 0.0)
        zext_ref[HALO:, :] = zcur_ref[...]

        @pl.when(i == 0)
        def _():
            dw_ref[...] = jnp.zeros_like(dw_ref)

        dz0 = jnp.zeros((t, C), F32)
        for d, win in _shifted(dext_ref[...], t):
            k = CONV_WIDTH - 1 - d
            if 0 <= k < CONV_WIDTH:
                dz0 = dz0 + win * w_ref[k:k + 1, :]
        for d, win in _shifted(zext_ref[...], t):
            k = d - (HALO - CONV_WIDTH + 1)
            if 0 <= k < CONV_WIDTH:
                dw_ref[k:k + 1, :] += _colsum(dcur * win)
        a = cin_ref[:, 0:C]
        sg = _sigmoid(cin_ref[:, C:2 * C])
        dcin_ref[:, 0:C] = (dz0 * sg).astype(BF16)
        dcin_ref[:, C:2 * C] = (dz0 * a * sg * (1.0 - sg)).astype(BF16)

    return pl.pallas_call(
        body, name="conv_bwd_taps", grid=(nt,),
        in_specs=[pl.BlockSpec((t, C), lambda i: (i, 0)),
                  pl.BlockSpec((HALO, C), lambda i: (jnp.minimum((i + 1) * per, last), 0)),
                  pl.BlockSpec((t, C), lambda i: (i, 0)),
                  pl.BlockSpec((HALO, C), lambda i: (jnp.maximum(i * per - 1, 0), 0)),
                  pl.BlockSpec((t, 2 * C), lambda i: (i, 0)),
                  pl.BlockSpec(conv_w.shape, lambda i: (0, 0))],
        out_specs=[pl.BlockSpec((t, 2 * C), lambda i: (i, 0)), pl.BlockSpec((HALO, C), lambda i: (0, 0))],
        out_shape=[jax.ShapeDtypeStruct((S, 2 * C), BF16), jax.ShapeDtypeStruct((HALO, C), F32)],
        scratch_shapes=[pltpu.VMEM((t + HALO, C), F32), pltpu.VMEM((t + HALO, C), F32)],
        compiler_params=_params(("arbitrary",)),
    )(dz1, dz1, z0, z0, conv_in, conv_w)


def _lower_tri(shape, rows_are_queries):
    row = lax.broadcasted_iota(jnp.int32, shape, 0)
    col = lax.broadcasted_iota(jnp.int32, shape, 1)
    return (col <= row) if rows_are_queries else (row <= col)


HEADS_PER_STEP = 2
FWD_KEY_TILES = 4


def _flash_specs(S, t):
    w = HEADS_PER_STEP * HEAD_PAD
    blk = pl.BlockSpec((t, w), lambda h, i: (i, h))
    head = pl.BlockSpec((S, w), lambda h, i: (0, h))
    return blk, head


def _head_lanes(g):
    return slice(g * HEAD_PAD, (g + 1) * HEAD_PAD)


def _dot_nt(a, b):
    return lax.dot_general(a, b, _DIMS["nt"], preferred_element_type=F32)


def _dot_nn(a, b):
    return lax.dot_general(a, b, _DIMS["nn"], preferred_element_type=F32)


def _dot_tn(a, b):
    return lax.dot_general(a, b, _DIMS["tn"], preferred_element_type=F32)


def _flash_fwd(q, k, v):
    S = q.shape[0]
    t = _pick(S, (512, 256, 128))

    def body(q_ref, k_ref, v_ref, o_ref, qa_ref, m_ref, acc_ref):
        qi = pl.program_id(1)
        m_ref[...] = jnp.full_like(m_ref, NEG)
        acc_ref[...] = jnp.zeros_like(acc_ref)

        def step(first, width, diag):
            rows = pl.ds(pl.multiple_of(first, width), width)
            for g in range(HEADS_PER_STEP):
                hl = _head_lanes(g)
                s = _dot_nt(q_ref[:, hl], k_ref[rows, hl])
                if diag:
                    s = jnp.where(_lower_tri(s.shape, True), s, NEG)
                m_old = m_ref[g]
                m_new = jnp.maximum(m_old, jnp.max(s, axis=-1, keepdims=True))
                p = jnp.exp(s - m_new).astype(BF16)
                acc_ref[g] = jnp.exp(m_old - m_new) * acc_ref[g] + _dot_nn(p, v_ref[rows, hl])
                m_ref[g] = m_new

        def wide(kb, carry):
            step(kb * (FWD_KEY_TILES * t), FWD_KEY_TILES * t, False)
            return carry

        def single(ki, carry):
            step(ki * t, t, False)
            return carry

        lax.fori_loop(0, qi // FWD_KEY_TILES, wide, 0)
        lax.fori_loop((qi // FWD_KEY_TILES) * FWD_KEY_TILES, qi, single, 0)
        step(qi * t, t, True)
        for g in range(HEADS_PER_STEP):
            hl = _head_lanes(g)
            acc = acc_ref[g]
            l = -acc[:, STAT_COL_V:STAT_COL_V + 1]
            o_ref[:, hl] = (acc / l).astype(BF16)
            qa_ref[:, hl] = _put_stats(q_ref[:, hl], m_ref[g] + jnp.log(l), STAT_COL_QK)

    blk, head = _flash_specs(S, t)
    return pl.pallas_call(
        body, name="mla_flash_fwd", grid=(MLA_HEADS // HEADS_PER_STEP, S // t),
        in_specs=[blk, head, head], out_specs=[blk, blk],
        out_shape=[jax.ShapeDtypeStruct(q.shape, BF16), jax.ShapeDtypeStruct(q.shape, BF16)],
        scratch_shapes=[pltpu.VMEM((HEADS_PER_STEP, t, 1), F32), pltpu.VMEM((HEADS_PER_STEP, t, HEAD_PAD), F32)],
        compiler_params=_params(("parallel", "arbitrary")),
    )(q, k, v)


def _flash_dq(qa, k, v, doa):
    S = qa.shape[0]
    t = _pick(S, (512, 256, 128))

    def body(qa_ref, k_ref, v_ref, do_ref, dq_ref, acc_ref):
        qi = pl.program_id(1)
        acc_ref[...] = jnp.zeros_like(acc_ref)

        def step(ki, diag):
            rows = pl.ds(pl.multiple_of(ki * t, t), t)
            for g in range(HEADS_PER_STEP):
                hl = _head_lanes(g)
                kk = k_ref[rows, hl]
                s = _dot_nt(qa_ref[:, hl], kk)
                if diag:
                    s = jnp.where(_lower_tri(s.shape, True), s, NEG)
                ds = (jnp.exp(s) * _dot_nt(do_ref[:, hl], v_ref[rows, hl])).astype(BF16)
                acc_ref[:, hl] += _dot_nn(ds, kk)

        def loop(ki, carry):
            step(ki, False)
            return carry

        lax.fori_loop(0, qi, loop, 0)
        step(qi, True)
        dq_ref[...] = acc_ref[...]

    blk, head = _flash_specs(S, t)
    return pl.pallas_call(
        body, name="mla_flash_dq", grid=(MLA_HEADS // HEADS_PER_STEP, S // t),
        in_specs=[blk, head, head, blk], out_specs=blk,
        out_shape=jax.ShapeDtypeStruct(qa.shape, F32),
        scratch_shapes=[pltpu.VMEM((t, HEADS_PER_STEP * HEAD_PAD), F32)],
        compiler_params=_params(("parallel", "arbitrary")),
    )(qa, k, v, doa)


def _flash_dkv(qa, k, v, doa):
    S = qa.shape[0]
    t = _pick(S, (512, 256, 128))
    n = S // t

    def body(qa_ref, k_ref, v_ref, do_ref, dk_ref, dv_ref, dk_acc, dv_acc):
        kj = pl.program_id(1)
        dk_acc[...] = jnp.zeros_like(dk_acc)
        dv_acc[...] = jnp.zeros_like(dv_acc)

        def step(qi, diag):
            rows = pl.ds(pl.multiple_of(qi * t, t), t)
            for g in range(HEADS_PER_STEP):
                hl = _head_lanes(g)
                qa, do = qa_ref[rows, hl], do_ref[rows, hl]
                st = _dot_nt(k_ref[:, hl], qa)
                if diag:
                    st = jnp.where(_lower_tri(st.shape, False), st, NEG)
                pt = jnp.exp(st)
                dst = (pt * _dot_nt(v_ref[:, hl], do)).astype(BF16)
                dv_acc[:, hl] += _dot_nn(pt.astype(BF16), do)
                dk_acc[:, hl] += _dot_nn(dst, qa)

        def loop(qi, carry):
            step(qi, False)
            return carry

        step(kj, True)
        lax.fori_loop(kj + 1, n, loop, 0)
        dk_ref[...] = dk_acc[...]
        dv_ref[...] = dv_acc[...].astype(BF16)

    blk, head = _flash_specs(S, t)
    w = HEADS_PER_STEP * HEAD_PAD
    return pl.pallas_call(
        body, name="mla_flash_dkv", grid=(MLA_HEADS // HEADS_PER_STEP, n),
        in_specs=[head, blk, blk, head], out_specs=[blk, blk],
        out_shape=[jax.ShapeDtypeStruct(qa.shape, F32), jax.ShapeDtypeStruct(qa.shape, BF16)],
        scratch_shapes=[pltpu.VMEM((t, w), F32), pltpu.VMEM((t, w), F32)],
        compiler_params=_params(("parallel", "arbitrary")),
    )(qa, k, v, doa)


def _xattn_fwd(xq, kvx):
    W = X_HEADS * X_HEAD_DIM

    def body(q_ref, kv_ref, o_ref):
        for h in range(X_HEADS):
            lo = h * X_HEAD_DIM
            s = _dot_nt(q_ref[:, lo:lo + X_HEAD_DIM], kv_ref[:, lo:lo + X_HEAD_DIM])
            p = jnp.exp(s - jnp.max(s, axis=-1, keepdims=True))
            p = p / jnp.sum(p, axis=-1, keepdims=True)
            o_ref[:, lo:lo + X_HEAD_DIM] = _dot_nn(p.astype(BF16), kv_ref[:, W + lo:W + lo + X_HEAD_DIM]).astype(BF16)

    return _rows("xattn_fwd", body, [xq], [kvx], [(W, BF16)])[0]


def _xattn_bwd(xq, kvx, dox):
    W = X_HEADS * X_HEAD_DIM
    scale = X_HEAD_DIM ** -0.5

    def body(q_ref, do_ref, kv_ref, dq_ref, dkv_ref):
        @pl.when(pl.program_id(0) == 0)
        def _():
            dkv_ref[...] = jnp.zeros_like(dkv_ref)

        for h in range(X_HEADS):
            lo = h * X_HEAD_DIM
            q, k = q_ref[:, lo:lo + X_HEAD_DIM], kv_ref[:, lo:lo + X_HEAD_DIM]
            v, do = kv_ref[:, W + lo:W + lo + X_HEAD_DIM], do_ref[:, lo:lo + X_HEAD_DIM]
            s = _dot_nt(q, k)
            p = jnp.exp(s - jnp.max(s, axis=-1, keepdims=True))
            p = p / jnp.sum(p, axis=-1, keepdims=True)
            dp = _dot_nt(do, v)
            ds = (p * (dp - jnp.sum(dp * p, axis=-1, keepdims=True))).astype(BF16)
            dq_ref[:, lo:lo + X_HEAD_DIM] = (_dot_nn(ds, k) * scale).astype(BF16)
            dkv_ref[:, lo:lo + X_HEAD_DIM] += _dot_tn(ds, q)
            dkv_ref[:, W + lo:W + lo + X_HEAD_DIM] += _dot_tn(p.astype(BF16), do)

    return _rows("xattn_bwd", body, [xq, dox], [kvx], [(W, BF16)], [kvx.shape])


def _adamw(name, w, g, m, v):
    shape = w.shape
    w2, g2, m2, v2 = [a.reshape(-1, shape[-1]) for a in (w, g, m, v)]
    R, C = w2.shape
    t = _pick(R, (256, 128, 64, 32, 16, 8))
    c1 = 1.0 / (1.0 - ADAM_B1 ** ADAM_STEP)
    c2 = 1.0 / (1.0 - ADAM_B2 ** ADAM_STEP)

    def body(w_ref, g_ref, m_ref, v_ref, d_ref, nm_ref, nv_ref):
        g = g_ref[...]
        nm = ADAM_B1 * m_ref[...] + (1.0 - ADAM_B1) * g
        nv = ADAM_B2 * v_ref[...] + (1.0 - ADAM_B2) * (g * g)
        d_ref[...] = -ADAM_LR * ((nm * c1) / (jnp.sqrt(nv * c2) + ADAM_EPS) + ADAM_WD * w_ref[...])
        nm_ref[...] = nm
        nv_ref[...] = nv

    spec = pl.BlockSpec((t, C), lambda i: (i, 0))
    outs = pl.pallas_call(
        body, name=name, grid=(R // t,), in_specs=[spec] * 4, out_specs=[spec] * 3,
        out_shape=[jax.ShapeDtypeStruct((R, C), F32)] * 3, compiler_params=_params(("parallel",)),
    )(w2, g2, m2, v2)
    return [o.reshape(shape) for o in outs]


def _place():
    x, y, c = lax.axis_index("x"), lax.axis_index("y"), lax.axis_index("c")
    return x, y, c, [(1 - x, y), (x, 1 - y), (1 - x, 1 - y)]


_ANY = pl.BlockSpec(memory_space=pl.ANY)


def _all_gather_chips(shards):
    n = len(shards)

    def body(*refs):
        ins, outs = refs[:n], refs[n:2 * n]
        send, recv, loc = refs[2 * n:]
        x, y, c, chips = _place()
        mine = 2 * x + y
        local, remote = [], []
        for w in range(n):
            lc = pltpu.make_async_copy(ins[w], outs[w].at[mine], loc.at[w])
            lc.start()
            local.append(lc)
            for j, (px, py) in enumerate(chips):
                cp = pltpu.make_async_remote_copy(src_ref=ins[w], dst_ref=outs[w].at[mine], send_sem=send.at[3 * w + j],
                                                  recv_sem=recv.at[3 * w + j], device_id=(px, py, c), device_id_type=MESH)
                cp.start()
                remote.append(cp)
        for w in range(n):
            for j, (px, py) in enumerate(chips):
                pltpu.make_async_remote_copy(src_ref=ins[w], dst_ref=outs[w].at[2 * px + py], send_sem=send.at[3 * w + j],
                                             recv_sem=recv.at[3 * w + j], device_id=(px, py, c), device_id_type=MESH).wait_recv()
        for cp in remote:
            cp.wait_send()
        for lc in local:
            lc.wait()

    return pl.pallas_call(
        body, name="gather_weights", in_specs=[_ANY] * n, out_specs=[_ANY] * n,
        out_shape=[jax.ShapeDtypeStruct((N_CHIPS,) + s.shape, s.dtype) for s in shards],
        scratch_shapes=[pltpu.SemaphoreType.DMA((3 * n,)), pltpu.SemaphoreType.DMA((3 * n,)), pltpu.SemaphoreType.DMA((n,))],
    )(*shards)


def _pair_exchange(pack):
    def body(p_ref, o_ref, send, recv):
        x, y, c, _ = _place()
        cp = pltpu.make_async_remote_copy(src_ref=p_ref.at[1 - c], dst_ref=o_ref, send_sem=send, recv_sem=recv,
                                          device_id=(x, y, 1 - c), device_id_type=MESH)
        cp.start()
        cp.wait()

    return pl.pallas_call(
        body, name="grad_pair_exchange", in_specs=[_ANY], out_specs=_ANY,
        out_shape=jax.ShapeDtypeStruct(pack.shape[1:], pack.dtype),
        scratch_shapes=[pltpu.SemaphoreType.DMA, pltpu.SemaphoreType.DMA],
    )(pack)


def _chip_exchange(part):
    def body(p_ref, o_ref, send, recv, loc):
        x, y, c, chips = _place()
        mine = 2 * x + y
        lc = pltpu.make_async_copy(p_ref.at[mine], o_ref.at[mine], loc)
        lc.start()
        cps = []
        for j, (px, py) in enumerate(chips):
            cp = pltpu.make_async_remote_copy(src_ref=p_ref.at[2 * px + py], dst_ref=o_ref.at[mine], send_sem=send.at[j],
                                              recv_sem=recv.at[j], device_id=(px, py, c), device_id_type=MESH)
            cp.start()
            cps.append(cp)
        for j, (px, py) in enumerate(chips):
            pltpu.make_async_remote_copy(src_ref=p_ref.at[mine], dst_ref=o_ref.at[2 * px + py], send_sem=send.at[j],
                                         recv_sem=recv.at[j], device_id=(px, py, c), device_id_type=MESH).wait_recv()
        for cp in cps:
            cp.wait_send()
        lc.wait()

    return pl.pallas_call(
        body, name="grad_chip_exchange", in_specs=[_ANY], out_specs=_ANY,
        out_shape=jax.ShapeDtypeStruct(part.shape, part.dtype),
        scratch_shapes=[pltpu.SemaphoreType.DMA((3,)), pltpu.SemaphoreType.DMA((3,)), pltpu.SemaphoreType.DMA],
    )(part)


def _pair_share(half):
    def body(h_ref, o_ref, send, recv, loc):
        x, y, c, _ = _place()
        lc = pltpu.make_async_copy(h_ref, o_ref.at[c], loc)
        lc.start()
        cp = pltpu.make_async_remote_copy(src_ref=h_ref, dst_ref=o_ref.at[c], send_sem=send, recv_sem=recv,
                                          device_id=(x, y, 1 - c), device_id_type=MESH)
        cp.start()
        pltpu.make_async_remote_copy(src_ref=h_ref, dst_ref=o_ref.at[1 - c], send_sem=send, recv_sem=recv,
                                     device_id=(x, y, 1 - c), device_id_type=MESH).wait_recv()
        cp.wait_send()
        lc.wait()

    return pl.pallas_call(
        body, name="grad_pair_share", in_specs=[_ANY], out_specs=_ANY,
        out_shape=jax.ShapeDtypeStruct((2,) + half.shape, half.dtype),
        scratch_shapes=[pltpu.SemaphoreType.DMA, pltpu.SemaphoreType.DMA, pltpu.SemaphoreType.DMA],
    )(half)


def _sum_over_devices(block):
    R, L = block.shape

    def gather(b_ref, o_ref, send, recv, loc):
        x, y, c, _ = _place()
        lc = pltpu.make_async_copy(b_ref, o_ref.at[4 * x + 2 * y + c], loc)
        lc.start()
        peers = [(1 - x if dx else x, 1 - y if dy else y, 1 - c if dc else c)
                 for dx in (0, 1) for dy in (0, 1) for dc in (0, 1) if dx or dy or dc]
        cps = []
        for j, peer in enumerate(peers):
            cp = pltpu.make_async_remote_copy(src_ref=b_ref, dst_ref=o_ref.at[4 * x + 2 * y + c], send_sem=send.at[j],
                                              recv_sem=recv.at[j], device_id=peer, device_id_type=MESH)
            cp.start()
            cps.append(cp)
        for j, (px, py, pc) in enumerate(peers):
            pltpu.make_async_remote_copy(src_ref=b_ref, dst_ref=o_ref.at[4 * px + 2 * py + pc], send_sem=send.at[j],
                                         recv_sem=recv.at[j], device_id=(px, py, pc), device_id_type=MESH).wait_recv()
        for cp in cps:
            cp.wait_send()
        lc.wait()

    blocks = pl.pallas_call(
        gather, name="small_grads_gather", in_specs=[_ANY], out_specs=_ANY,
        out_shape=jax.ShapeDtypeStruct((N_DEV, R, L), F32),
        scratch_shapes=[pltpu.SemaphoreType.DMA((N_DEV - 1,)), pltpu.SemaphoreType.DMA((N_DEV - 1,)), pltpu.SemaphoreType.DMA],
    )(block)

    def add(b_ref, o_ref):
        total = b_ref[0]
        for d in range(1, N_DEV):
            total = total + b_ref[d]
        o_ref[...] = total

    return pl.pallas_call(add, name="small_grads_add", out_shape=jax.ShapeDtypeStruct((R, L), F32))(blocks)


def _pair_add(pack, got):
    _, nchip, R, L = pack.shape
    t = _pick(R, PACK_TILES)
    c = lax.axis_index("c").astype(jnp.int32).reshape(1)

    def body(c_ref, p_ref, g_ref, o_ref):
        o_ref[...] = (p_ref[...] + g_ref[...]).astype(BF16)

    return pl.pallas_call(
        body, name="grad_pair_add",
        grid_spec=pltpu.PrefetchScalarGridSpec(
            num_scalar_prefetch=1, grid=(nchip, R // t),
            in_specs=[pl.BlockSpec((None, None, t, L), lambda k, i, c_ref: (c_ref[0], k, i, 0)),
                      pl.BlockSpec((None, t, L), lambda k, i, c_ref: (k, i, 0))],
            out_specs=pl.BlockSpec((None, t, L), lambda k, i, c_ref: (k, i, 0))),
        out_shape=jax.ShapeDtypeStruct(got.shape, BF16), compiler_params=_params(("parallel", "parallel")),
    )(c, pack, got)


def _chip_add(parts):
    _, R, L = parts.shape
    t = _pick(R, PACK_TILES)

    def body(p_ref, o_ref):
        p = [p_ref[k].astype(F32) for k in range(N_CHIPS)]
        o_ref[...] = ((p[0] + p[1]) + p[2]) + p[3]

    return pl.pallas_call(
        body, name="grad_chip_add", grid=(R // t,),
        in_specs=[pl.BlockSpec((N_CHIPS, t, L), lambda i: (0, i, 0))], out_specs=pl.BlockSpec((t, L), lambda i: (i, 0)),
        out_shape=jax.ShapeDtypeStruct((R, L), F32), compiler_params=_params(("parallel",)),
    )(parts)


_CUT = (2 * CONV_CH, 2 * CONV_CH + Q_LORA, 2 * CONV_CH + Q_LORA + KV_LORA, 2 * CONV_CH + Q_LORA + KV_LORA + QK_ROPE)
_KR_AT = _CUT[2] + QK_NOPE


def _pad_last(a, n):
    return jnp.pad(a, [(0, 0)] * (a.ndim - 1) + [(0, n - a.shape[-1])])


def _layout_weights(w):
    w_in = w["w_in"]
    kr = jnp.pad(w_in[:, _CUT[2]:_CUT[3]], ((0, 0), (QK_NOPE, HEAD_PAD - QK_NOPE - QK_ROPE)))
    uq = _pad_last(w["w_uq"].reshape(Q_LORA, MLA_HEADS, QK_NOPE + QK_ROPE), HEAD_PAD).reshape(Q_LORA, MLA_HEADS * HEAD_PAD)
    ukv = w["w_ukv"].reshape(KV_LORA, MLA_HEADS, QK_NOPE + V_DIM)
    uk = _pad_last(ukv[:, :, :QK_NOPE], HEAD_PAD).reshape(KV_LORA, MLA_HEADS * HEAD_PAD)
    uv = _pad_last(ukv[:, :, QK_NOPE:], HEAD_PAD).reshape(KV_LORA, MLA_HEADS * HEAD_PAD)
    mo = jnp.pad(w["w_mla_out"].reshape(MLA_HEADS, V_DIM, D_MODEL), ((0, 0), (0, HEAD_PAD - V_DIM), (0, 0)))
    return dict(
        w_in=jnp.concatenate([w_in[:, :_CUT[2]], kr, w_in[:, _CUT[3]:]], axis=1),
        w_uq=uq, w_ukv=jnp.concatenate([uk, uv], axis=1), w_mla_out=mo.reshape(MLA_HEADS * HEAD_PAD, D_MODEL),
        w_conv_out=w["w_conv_out"], w_out=w["w_out"], w_xq=w["w_xq"], w_xkv=w["w_xkv"], w_xo=w["w_xo"],
        w_mlp1=w["w_mlp1"], w_mlp2=w["w_mlp2"])


def _unlayout_grads(g):
    gi = g["w_in"]
    w_in = jnp.concatenate([gi[:, :_CUT[2]], gi[:, _KR_AT:_KR_AT + QK_ROPE], gi[:, _CUT[2] + HEAD_PAD:]], axis=1)
    uq = g["w_uq"].reshape(Q_LORA, MLA_HEADS, HEAD_PAD)[:, :, :QK_NOPE + QK_ROPE].reshape(Q_LORA, -1)
    gk = g["w_ukv"][:, :MLA_HEADS * HEAD_PAD].reshape(KV_LORA, MLA_HEADS, HEAD_PAD)[:, :, :QK_NOPE]
    gv = g["w_ukv"][:, MLA_HEADS * HEAD_PAD:].reshape(KV_LORA, MLA_HEADS, HEAD_PAD)[:, :, :V_DIM]
    ukv = jnp.concatenate([gk, gv], axis=2).reshape(KV_LORA, -1)
    mo = g["w_mla_out"].reshape(MLA_HEADS, HEAD_PAD, D_MODEL)[:, :V_DIM].reshape(MLA_HEADS * V_DIM, D_MODEL)
    out = dict(g)
    out.update(w_in=w_in, w_uq=uq, w_ukv=ukv, w_mla_out=mo)
    return out


def _rope_tables(positions):
    half = QK_ROPE // 2
    inv_freq = ROPE_THETA ** (-jnp.arange(half, dtype=F32) / half)
    ang = positions.astype(F32).reshape(-1, 1) * inv_freq
    cos, sin = jnp.cos(ang), jnp.sin(ang)
    S = cos.shape[0]
    z16, z32, z64 = jnp.zeros((S, half), F32), jnp.zeros((S, QK_ROPE), F32), jnp.zeros((S, QK_NOPE), F32)
    c = jnp.concatenate([jnp.ones((S, QK_NOPE), F32), cos, cos, z32], axis=1)
    sa = jnp.concatenate([z64, -sin, z16, z32], axis=1)
    sb = jnp.concatenate([z64, z16, sin, z32], axis=1)
    return c, sa, sb


def _local_step(x, mem, positions, target, wl, sm):
    S = x.shape[0]
    HW = MLA_HEADS * HEAD_PAD
    rope_c, rope_sa, rope_sb = _rope_tables(positions)
    qk_scale = (QK_NOPE + QK_ROPE) ** -0.5
    w_in = wl["w_in"]
    c0, c1, c2, c3 = _CUT[0], _CUT[1], _CUT[2], _CUT[2] + HEAD_PAD

    def k_rms1(x_ref, g_ref, u_ref):
        u_ref[...] = _rms_fwd(x_ref[...], g_ref[...]).astype(BF16)

    u1, = _rows("rms_mix", k_rms1, [x], [sm["norm_mix_g"]], [(D_MODEL, BF16)])

    def epi_glu(acc, xs, outs):
        a, gt = acc[:, 0:CONV_CH], acc[:, CONV_CH:2 * CONV_CH]
        outs[0][...] = acc[...]
        outs[1][...] = a * _sigmoid(gt)

    conv_in, z0 = _mm("proj_conv", u1, w_in[:, :c0], "nn", [(2 * CONV_CH, F32), (CONV_CH, F32)], epi_glu)
    c_q = _mm_plain("proj_cq", u1, w_in[:, c0:c1], "nn")
    c_kv = _mm_plain("proj_ckv", u1, w_in[:, c1:c2], "nn")
    kr_raw = _mm_plain("proj_krope", u1, w_in[:, c2:c3], "nn")

    def epi_sigmoid(acc, xs, outs):
        outs[0][...] = _sigmoid(acc[...])

    gates, = _mm("proj_gates", u1, w_in[:, c3:], "nn", [(2 * D_MODEL, F32)], epi_sigmoid)

    z1, z3 = _conv_fwd(z0, sm["conv_w"], sm["conv_b"], sm["conv_ln_g"], sm["conv_ln_b"])
    conv_out = _mm_plain("conv_out", z3, wl["w_conv_out"], "nn")

    def k_lora_norm(cq_ref, ckv_ref, gq_ref, gkv_ref, qn_ref, kvn_ref):
        qn_ref[...] = _rms_fwd(cq_ref[...], gq_ref[...]).astype(BF16)
        kvn_ref[...] = _rms_fwd(ckv_ref[...], gkv_ref[...]).astype(BF16)

    qn, kvn = _rows("lora_norm", k_lora_norm, [c_q, c_kv], [sm["q_norm_g"], sm["kv_norm_g"]],
                    [(Q_LORA, BF16), (KV_LORA, BF16)])

    def epi_q(acc, xs, outs):
        c, sa, sb = xs[0][...], xs[1][...], xs[2][...]
        for h in range(MLA_HEADS):
            lo = h * HEAD_PAD
            outs[0][:, lo:lo + HEAD_PAD] = (_rope(acc[:, lo:lo + HEAD_PAD], c, sa, sb, 1.0) * qk_scale).astype(BF16)

    q_att, = _mm("q_up", qn, wl["w_uq"], "nn", [(HW, BF16)], epi_q, row_x=[rope_c, rope_sa, rope_sb], tn=HW)

    def epi_kv(acc, xs, outs):
        kr = _rope(xs[0][...], xs[1][...], xs[2][...], xs[3][...], 1.0)
        kr = kr + _neg_ones(kr.shape, STAT_COL_QK)
        vconst = _neg_ones(kr.shape, STAT_COL_V)
        for h in range(MLA_HEADS):
            lo = h * HEAD_PAD
            outs[0][:, lo:lo + HEAD_PAD] = (acc[:, lo:lo + HEAD_PAD] + kr).astype(BF16)
            outs[1][:, lo:lo + HEAD_PAD] = (acc[:, HW + lo:HW + lo + HEAD_PAD] + vconst).astype(BF16)

    k_att, v_att = _mm("kv_up", kvn, wl["w_ukv"], "nn", [(HW, BF16), (HW, BF16)], epi_kv,
                       row_x=[kr_raw, rope_c, rope_sa, rope_sb], tn=2 * HW)

    o_att, q_aug = _flash_fwd(q_att, k_att, v_att)
    mla_out = _mm_plain("mla_out", o_att, wl["w_mla_out"], "nn")

    def k_merge(g_ref, co_ref, mo_ref, out_ref):
        out_ref[...] = (g_ref[:, 0:D_MODEL] * co_ref[...] + g_ref[:, D_MODEL:] * mo_ref[...]).astype(BF16)

    merged, = _rows("merge", k_merge, [gates, conv_out, mla_out], [], [(D_MODEL, BF16)], tile=256)

    def epi_res_norm(acc, xs, outs):
        h = xs[0][...] + acc[...]
        outs[0][...] = h
        outs[1][...] = _rms_fwd(h, xs[1][...]).astype(BF16)

    h1, u2 = _mm("mix_out", merged, wl["w_out"], "nn", [(D_MODEL, F32), (D_MODEL, BF16)], epi_res_norm,
                 row_x=[x], vec_x=[sm["norm_xattn_g"]], tn=D_MODEL)

    xscale = X_HEAD_DIM ** -0.5

    def epi_scale(acc, xs, outs):
        outs[0][...] = (acc[...] * xscale).astype(BF16)

    xq, = _mm("xattn_q", u2, wl["w_xq"], "nn", [(X_HEADS * X_HEAD_DIM, BF16)], epi_scale)

    def k_mem_norm(m_ref, g_ref, o_ref):
        o_ref[...] = _rms_fwd(m_ref[...], g_ref[...]).astype(BF16)

    mem_n, = _rows("mem_norm", k_mem_norm, [mem], [sm["norm_mem_g"]], [(D_MODEL, BF16)])
    kvx = _mm_plain("xattn_kv", mem_n, wl["w_xkv"], "nn", dtype=BF16)
    ox = _xattn_fwd(xq, kvx)
    h2, u3 = _mm("xattn_out", ox, wl["w_xo"], "nn", [(D_MODEL, F32), (D_MODEL, BF16)], epi_res_norm,
                 row_x=[h1], vec_x=[sm["norm_mlp_g"]], tn=D_MODEL)

    def epi_relu2(acc, xs, outs):
        r = jnp.maximum(acc[...], 0.0)
        outs[0][...] = (r * r).astype(BF16)

    hid, = _mm("mlp_up", u3, wl["w_mlp1"], "nn", [(D_FF, BF16)], epi_relu2)

    def epi_res(acc, xs, outs):
        outs[0][...] = xs[0][...] + acc[...]

    h3, = _mm("mlp_down", hid, wl["w_mlp2"], "nn", [(D_MODEL, F32)], epi_res, row_x=[h2], tn=D_MODEL)

    def k_final(h_ref, t_ref, g_ref, dh_ref, dhb_ref, loss_ref, dg_ref):
        h, g = h_ref[...], g_ref[...]
        e = _rms_fwd(h, g) - t_ref[...]
        part = 0.5 * jnp.sum(jnp.mean(e * e, axis=-1, keepdims=True), axis=0, keepdims=True)
        _accum(loss_ref, jnp.broadcast_to(part, loss_ref.shape))
        dh, dg = _rms_bwd(h, g, e * (1.0 / D_MODEL))
        dh_ref[...] = dh
        dhb_ref[...] = dh.astype(BF16)
        _accum(dg_ref, dg)

    dh3, dh3b, loss, g_final = _rows("final_loss", k_final, [h3, target], [sm["final_norm_g"]],
                                     [(D_MODEL, F32), (D_MODEL, BF16)], [(1, LANES), (1, D_MODEL)])

    def epi_drelu2(acc, xs, outs):
        outs[0][...] = (acc[...] * (2.0 * jnp.sqrt(xs[0][...].astype(F32)))).astype(BF16)

    da1, = _mm("mlp_down_dx", dh3b, wl["w_mlp2"], "nt", [(D_FF, BF16)], epi_drelu2, tile_x=[hid])
    g_mlp2 = _mm_plain("mlp_down_dw", hid, dh3b, "tn")
    g_mlp1 = _mm_plain("mlp_up_dw", u3, da1, "tn")
    du3 = _mm_plain("mlp_up_dx", da1, wl["w_mlp1"], "nt")

    def k_norm_bwd(x_ref, du_ref, dres_ref, g_ref, dh_ref, dhb_ref, dg_ref):
        dx, dg = _rms_bwd(x_ref[...], g_ref[...], du_ref[...])
        dh = dres_ref[...] + dx
        dh_ref[...] = dh
        dhb_ref[...] = dh.astype(BF16)
        _accum(dg_ref, dg)

    def norm_bwd(name, xin, du, dres, g):
        return _rows(name, k_norm_bwd, [xin, du, dres], [g], [(D_MODEL, F32), (D_MODEL, BF16)], [(1, D_MODEL)], tile=256)

    dh2, dh2b, g_norm_mlp = norm_bwd("norm_mlp_bwd", h2, du3, dh3, sm["norm_mlp_g"])

    dox = _mm_plain("xattn_out_dx", dh2b, wl["w_xo"], "nt", dtype=BF16)
    g_xo = _mm_plain("xattn_out_dw", ox, dh2b, "tn")
    dxq, dkvx = _xattn_bwd(xq, kvx, dox)
    g_xq = _mm_plain("xattn_q_dw", u2, dxq, "tn")
    du2 = _mm_plain("xattn_q_dx", dxq, wl["w_xq"], "nt")
    g_xkv = _mm_plain("xattn_kv_dw", mem_n, dkvx, "tn")
    dmem_n = _mm_plain("xattn_kv_dx", dkvx, wl["w_xkv"], "nt")

    def k_mem_bwd(m_ref, d_ref, g_ref, dg_ref):
        _, dg = _rms_bwd(m_ref[...], g_ref[...], d_ref[...])
        _accum(dg_ref, dg)

    g_norm_mem, = _rows("mem_norm_bwd", k_mem_bwd, [mem, dmem_n], [sm["norm_mem_g"]], [], [(1, D_MODEL)])
    dh1, dh1b, g_norm_xattn = norm_bwd("norm_xattn_bwd", h1, du2, dh2, sm["norm_xattn_g"])

    dmerged = _mm_plain("mix_out_dx", dh1b, wl["w_out"], "nt")
    g_out = _mm_plain("mix_out_dw", merged, dh1b, "tn")

    def k_merge_bwd(dm_ref, g_ref, co_ref, mo_ref, dco_ref, dmo_ref, dgl_ref):
        dm = dm_ref[...]
        g0, g1 = g_ref[:, 0:D_MODEL], g_ref[:, D_MODEL:]
        dco_ref[...] = (dm * g0).astype(BF16)
        dmo_ref[...] = (dm * g1).astype(BF16)
        dgl_ref[:, 0:D_MODEL] = (dm * co_ref[...] * g0 * (1.0 - g0)).astype(BF16)
        dgl_ref[:, D_MODEL:] = (dm * mo_ref[...] * g1 * (1.0 - g1)).astype(BF16)

    dconv_out, dmla_out, dgl = _rows("merge_bwd", k_merge_bwd, [dmerged, gates, conv_out, mla_out], [],
                                     [(D_MODEL, BF16), (D_MODEL, BF16), (2 * D_MODEL, BF16)], tile=256)

    def epi_do(acc, xs, outs):
        for h in range(MLA_HEADS):
            lo = h * HEAD_PAD
            do = acc[:, lo:lo + HEAD_PAD]
            delta = jnp.sum(do * xs[0][:, lo:lo + HEAD_PAD].astype(F32), axis=-1, keepdims=True)
            outs[0][:, lo:lo + HEAD_PAD] = _put_stats(do.astype(BF16), delta, STAT_COL_V)

    do_aug, = _mm("mla_out_dx", dmla_out, wl["w_mla_out"], "nt", [(HW, BF16)], epi_do, row_x=[o_att], tn=HW)
    g_mla_out = _mm_plain("mla_out_dw", o_att, dmla_out, "tn")
    dq_att = _flash_dq(q_aug, k_att, v_att, do_aug)
    dk_att, dv_att = _flash_dkv(q_aug, k_att, v_att, do_aug)

    def k_rope_bwd(dq_ref, dk_ref, dv_ref, c_ref, sa_ref, sb_ref, dqr_ref, dkv_ref, dkr_ref):
        c, sa, sb = c_ref[...], sa_ref[...], sb_ref[...]
        lane = lax.broadcasted_iota(jnp.int32, c.shape, 1)
        nope = (lane < QK_NOPE).astype(F32)
        ropem = ((lane >= QK_NOPE) & (lane < QK_NOPE + QK_ROPE)).astype(F32)
        dkr = jnp.zeros(c.shape, F32)
        for h in range(MLA_HEADS):
            lo = h * HEAD_PAD
            dqr_ref[:, lo:lo + HEAD_PAD] = (_rope(dq_ref[:, lo:lo + HEAD_PAD], c, sa, sb, -1.0) * qk_scale).astype(BF16)
            dk = dk_ref[:, lo:lo + HEAD_PAD]
            dkv_ref[:, lo:lo + HEAD_PAD] = (dk * nope).astype(BF16)
            dkr = dkr + dk
        dkv_ref[:, HW:] = dv_ref[...]
        dkr_ref[...] = (_rope(dkr * ropem, c, sa, sb, -1.0) * ropem).astype(BF16)

    dq_raw, dkv_cat, dkr = _rows("rope_bwd", k_rope_bwd, [dq_att, dk_att, dv_att, rope_c, rope_sa, rope_sb], [],
                                 [(HW, BF16), (2 * HW, BF16), (HEAD_PAD, BF16)], tile=256)
    g_uq = _mm_plain("q_up_dw", qn, dq_raw, "tn")
    dqn = _mm_plain("q_up_dx", dq_raw, wl["w_uq"], "nt")
    g_ukv = _mm_plain("kv_up_dw", kvn, dkv_cat, "tn")
    dkvn = _mm_plain("kv_up_dx", dkv_cat, wl["w_ukv"], "nt")

    def k_lora_bwd(cq_ref, ckv_ref, dqn_ref, dkvn_ref, gq_ref, gkv_ref, dcq_ref, dckv_ref, dgq_ref, dgkv_ref):
        dcq, dgq = _rms_bwd(cq_ref[...], gq_ref[...], dqn_ref[...])
        dckv, dgkv = _rms_bwd(ckv_ref[...], gkv_ref[...], dkvn_ref[...])
        dcq_ref[...] = dcq.astype(BF16)
        dckv_ref[...] = dckv.astype(BF16)
        _accum(dgq_ref, dgq)
        _accum(dgkv_ref, dgkv)

    dc_q, dc_kv, g_q_norm, g_kv_norm = _rows("lora_norm_bwd", k_lora_bwd, [c_q, c_kv, dqn, dkvn],
                                              [sm["q_norm_g"], sm["kv_norm_g"]], [(Q_LORA, BF16), (KV_LORA, BF16)],
                                              [(1, Q_LORA), (1, KV_LORA)])

    dz3 = _mm_plain("conv_out_dx", dconv_out, wl["w_conv_out"], "nt")
    g_conv_out = _mm_plain("conv_out_dw", z3, dconv_out, "tn")
    dz1, g_ln_g, g_ln_b, g_conv_b = _conv_bwd_norm(dz3, z1, sm["conv_ln_g"], sm["conv_ln_b"])
    dconv_in, g_conv_w = _conv_bwd_taps(dz1, z0, conv_in, sm["conv_w"])

    dproj = jnp.concatenate([dconv_in, dc_q, dc_kv, dkr, dgl], axis=1)
    g_in = _mm_plain("proj_dw", u1, dproj, "tn")
    du1 = _mm_plain("proj_dx", dproj, w_in, "nt")

    def k_norm1_bwd(x_ref, du_ref, dres_ref, g_ref, dx_ref, dg_ref):
        dx, dg = _rms_bwd(x_ref[...], g_ref[...], du_ref[...])
        dx_ref[...] = dres_ref[...] + dx
        _accum(dg_ref, dg)

    grad_x, g_norm_mix = _rows("norm_mix_bwd", k_norm1_bwd, [x, du1, dh1], [sm["norm_mix_g"]], [(D_MODEL, F32)],
                               [(1, D_MODEL)], tile=256)

    grads = dict(norm_mix_g=g_norm_mix, w_in=g_in, conv_w=g_conv_w[:CONV_WIDTH], conv_b=g_conv_b, conv_ln_g=g_ln_g,
                 conv_ln_b=g_ln_b, w_conv_out=g_conv_out, q_norm_g=g_q_norm, w_uq=g_uq, kv_norm_g=g_kv_norm, w_ukv=g_ukv,
                 w_mla_out=g_mla_out, w_out=g_out, norm_xattn_g=g_norm_xattn, norm_mem_g=g_norm_mem, w_xq=g_xq,
                 w_xkv=g_xkv, w_xo=g_xo, norm_mlp_g=g_norm_mlp, w_mlp1=g_mlp1, w_mlp2=g_mlp2, final_norm_g=g_final)
    return loss, grad_x, grads


def _shard(a, k, axis):
    n = a.shape[axis] // N_CHIPS
    return lax.slice_in_dim(a, k * n, (k + 1) * n, axis=axis)


def _pack_rows(n_elems):
    half = -(-n_elems // (2 * LANES))
    return 2 * (-(-half // PACK_TILES[0]) * PACK_TILES[0])


def _pack_small(grads, loss):
    flat = jnp.concatenate([grads[n].reshape(-1) for n in SMALL] + [loss.reshape(-1)[:1]])
    rows = -(-flat.shape[0] // (8 * LANES)) * 8
    return jnp.pad(flat, (0, rows * LANES - flat.shape[0])).reshape(rows, LANES)


def _pack_grads(grads):
    packs = []
    for k in range(N_CHIPS):
        flat = jnp.concatenate([_shard(grads[n], k, SHARD_AXIS[n]).reshape(-1) for n in BIG])
        rows = _pack_rows(flat.shape[0])
        packs.append(jnp.pad(flat, (0, rows * LANES - flat.shape[0])).reshape(2, rows // 2, LANES))
    return jnp.stack(packs, axis=1)


def _unpack(flat, names, shapes):
    out, at = {}, 0
    for n in names:
        size = math.prod(shapes[n])
        out[n] = flat[at:at + size].reshape(shapes[n])
        at += size
    return out, at


def kernel(x, mem, positions, norm_mix_g, w_in, conv_w, conv_b, conv_ln_g, conv_ln_b, w_conv_out, q_norm_g, w_uq, kv_norm_g, w_ukv, w_mla_out, w_out, norm_xattn_g, norm_mem_g, w_xq, w_xkv, w_xo, norm_mlp_g, w_mlp1, w_mlp2, final_norm_g, loss_target, m_norm_mix_g, m_w_in, m_conv_w, m_conv_b, m_conv_ln_g, m_conv_ln_b, m_w_conv_out, m_q_norm_g, m_w_uq, m_kv_norm_g, m_w_ukv, m_w_mla_out, m_w_out, m_norm_xattn_g, m_norm_mem_g, m_w_xq, m_w_xkv, m_w_xo, m_norm_mlp_g, m_w_mlp1, m_w_mlp2, m_final_norm_g, v_norm_mix_g, v_w_in, v_conv_w, v_conv_b, v_conv_ln_g, v_conv_ln_b, v_w_conv_out, v_q_norm_g, v_w_uq, v_kv_norm_g, v_w_ukv, v_w_mla_out, v_w_out, v_norm_xattn_g, v_norm_mem_g, v_w_xq, v_w_xkv, v_w_xo, v_norm_mlp_g, v_w_mlp1, v_w_mlp2, v_final_norm_g):
    args = dict(locals())
    w = {n: args[n] for n in WEIGHTS}
    m = {n: args["m_" + n] for n in WEIGHTS}
    v = {n: args["v_" + n] for n in WEIGHTS}

    shards = [w[n][0].astype(F32 if n == "conv_w" else BF16) for n in BIG]
    gathered = _all_gather_chips(shards)
    full = {}
    for n, g in zip(BIG, gathered):
        ax = SHARD_AXIS[n]
        full[n] = jnp.moveaxis(g, 0, ax).reshape(g.shape[1:1 + ax] + (N_CHIPS * g.shape[1 + ax],) + g.shape[2 + ax:])
    wl = _layout_weights(full)
    sm = {n: w[n].reshape(1, -1) for n in SMALL}
    sm["conv_w"] = full["conv_w"]

    loss, grad_x, grads = _local_step(x[0], mem[0], positions, loss_target[0], wl, sm)
    grads = _unlayout_grads(grads)

    small_flat = _sum_over_devices(_pack_small(grads, loss)).reshape(-1)
    pack = _pack_grads(grads)
    pair = _pair_add(pack, _pair_exchange(pack))
    half = _chip_add(_chip_exchange(pair))
    flat = _pair_share(half).reshape(-1)
    shapes = {n: w[n].shape[1:] if n in BIG else w[n].shape for n in WEIGHTS}
    g_sum, _ = _unpack(flat, BIG, shapes)
    g_small, at = _unpack(small_flat, SMALL, shapes)
    g_sum.update(g_small)
    loss_sum = small_flat[at]

    out_g, out_d, out_m, out_v = [], [], [], []
    for n in WEIGHTS:
        g = g_sum[n].reshape(w[n].shape)
        d, nm, nv = _adamw("adamw_" + n, w[n], g, m[n], v[n])
        out_g.append(g)
        out_d.append(d)
        out_m.append(nm)
        out_v.append(nv)
    return (loss_sum, grad_x[None], *out_g, *out_d, *out_m, *out_v)
```

```python
import functools
import math

import jax
import jax.numpy as jnp
from jax import lax
from jax.experimental import pallas as pl
from jax.experimental.pallas import tpu as pltpu

F32 = jnp.float32
BF16 = jnp.bfloat16
MESH = pl.DeviceIdType.MESH

D_MODEL = 1024
CONV_CH = 512
CONV_WIDTH = 31
MLA_HEADS = 8
QK_NOPE = 64
QK_ROPE = 32
V_DIM = 64
Q_LORA = 384
KV_LORA = 256
MEM_LEN = 256
X_HEADS = 4
X_HEAD_DIM = 128
D_FF = 4096
ROPE_THETA = 10000.0
EPS = 1e-6
HEAD_PAD = 128
STAT_COL_QK = QK_NOPE + QK_ROPE
STAT_COL_V = V_DIM
HALO = 32
N_CHIPS = 4
LANES = 128

ADAM_LR = 0.001
ADAM_B1 = 0.9
ADAM_B2 = 0.999
ADAM_EPS = 1e-08
ADAM_WD = 0.01
ADAM_STEP = 10

VMEM_LIMIT = 52 * 1024 * 1024
PACK_TILES = (1536, 1024, 512, 256, 128, 64, 32, 16, 8)
N_DEV = 8
NEG = -1e30

BIG = ["w_in", "w_conv_out", "w_uq", "w_ukv", "w_mla_out", "w_out", "w_xq", "w_xkv", "w_xo", "w_mlp1", "w_mlp2", "conv_w"]
SHARD_AXIS = {"w_in": 1, "w_conv_out": 1, "w_uq": 1, "w_ukv": 1, "w_mla_out": 1, "w_out": 0, "w_xq": 0, "w_xkv": 0,
              "w_xo": 1, "w_mlp1": 1, "w_mlp2": 0, "conv_w": 1}
SMALL = ["norm_mix_g", "conv_b", "conv_ln_g", "conv_ln_b", "q_norm_g", "kv_norm_g", "norm_xattn_g", "norm_mem_g",
         "norm_mlp_g", "final_norm_g"]
WEIGHTS = ["norm_mix_g", "w_in", "conv_w", "conv_b", "conv_ln_g", "conv_ln_b", "w_conv_out", "q_norm_g", "w_uq",
           "kv_norm_g", "w_ukv", "w_mla_out", "w_out", "norm_xattn_g", "norm_mem_g", "w_xq", "w_xkv", "w_xo",
           "norm_mlp_g", "w_mlp1", "w_mlp2", "final_norm_g"]


def _pick(n, prefs):
    for p in prefs:
        if n % p == 0:
            return p
    return n


def _params(sem):
    return pltpu.CompilerParams(dimension_semantics=sem, vmem_limit_bytes=VMEM_LIMIT)


_DIMS = {"nn": (((1,), (0,)), ((), ())), "nt": (((1,), (1,)), ((), ())), "tn": (((0,), (0,)), ((), ()))}


def _mm(name, a, b, mode, outs, epi, row_x=(), tile_x=(), vec_x=(), tm=None, tn=None, tk=None):
    if mode == "nn":
        (M, K), (_, N) = a.shape, b.shape
    elif mode == "nt":
        (M, K), (N, _) = a.shape, b.shape
    else:
        (K, M), (_, N) = a.shape, b.shape
    tm = tm or _pick(M, (1024, 512, 384, 256, 128))
    tn = tn or _pick(N, (1024, 768, 512, 384, 256, 128))
    tk = tk or _pick(K, (1024, 768, 512, 384, 256, 128))
    nk = K // tk
    rows_inner = nk == 1 and N // tn > 1
    grid = (N // tn, M // tm, nk) if rows_inner else (M // tm, N // tn, nk)

    def spec(shape, f):
        return pl.BlockSpec(shape, (lambda j, i, k: f(i, j, k)) if rows_inner else f)

    a_spec = spec((tk, tm), lambda i, j, k: (k, i)) if mode == "tn" else spec((tm, tk), lambda i, j, k: (i, k))
    b_spec = spec((tn, tk), lambda i, j, k: (j, k)) if mode == "nt" else spec((tk, tn), lambda i, j, k: (k, j))
    in_specs = [a_spec, b_spec]
    in_specs += [spec((tm, r.shape[1]), lambda i, j, k: (i, 0)) for r in row_x]
    in_specs += [spec((tm, tn), lambda i, j, k: (i, j)) for _ in tile_x]
    in_specs += [spec(v.shape, lambda i, j, k: (0, 0)) for v in vec_x]
    out_specs, out_shape = [], []
    for w, dt in outs:
        if tn == N:
            out_specs.append(spec((tm, w), lambda i, j, k: (i, 0)))
        else:
            assert w == N, (name, w, N)
            out_specs.append(spec((tm, tn), lambda i, j, k: (i, j)))
        out_shape.append(jax.ShapeDtypeStruct((M, w), dt))
    nx = len(row_x) + len(tile_x) + len(vec_x)
    dims = _DIMS[mode]

    def body(a_ref, b_ref, *rest):
        x_refs, out_refs, acc_ref = rest[:nx], rest[nx:nx + len(outs)], rest[-1]
        av, bv = a_ref[...], b_ref[...]
        if av.dtype != BF16:
            av = av.astype(BF16)
        if bv.dtype != BF16:
            bv = bv.astype(BF16)
        prod = lax.dot_general(av, bv, dims, preferred_element_type=F32)
        if nk == 1:
            acc_ref[...] = prod
            epi(acc_ref, x_refs, out_refs)
        else:
            k = pl.program_id(2)

            @pl.when(k == 0)
            def _():
                acc_ref[...] = prod

            @pl.when(k > 0)
            def _():
                acc_ref[...] += prod

            @pl.when(k == nk - 1)
            def _():
                epi(acc_ref, x_refs, out_refs)

    res = pl.pallas_call(
        body, name=name, grid=grid, in_specs=in_specs, out_specs=out_specs, out_shape=out_shape,
        scratch_shapes=[pltpu.VMEM((tm, tn), F32)],
        compiler_params=_params(("parallel", "parallel", "arbitrary")),
    )(a, b, *row_x, *tile_x, *vec_x)
    return res


def _epi_store(acc_ref, x_refs, out_refs):
    for o in out_refs:
        o[...] = acc_ref[...].astype(o.dtype)


def _mm_plain(name, a, b, mode, dtype=F32, **kw):
    n = b.shape[0] if mode == "nt" else b.shape[1]
    return _mm(name, a, b, mode, [(n, dtype)], _epi_store, **kw)[0]


def _rows(name, body, row_ins, vec_ins, row_outs, acc_outs=(), tile=512):
    S = row_ins[0].shape[0]
    t = _pick(S, (tile, 256, 128, 64, 32, 16, 8))
    in_specs = [pl.BlockSpec((t, r.shape[1]), lambda i: (i, 0)) for r in row_ins]
    in_specs += [pl.BlockSpec(v.shape, lambda i: (0, 0)) for v in vec_ins]
    out_specs = [pl.BlockSpec((t, w), lambda i: (i, 0)) for w, _ in row_outs]
    out_specs += [pl.BlockSpec(shp, lambda i: (0, 0)) for shp in acc_outs]
    out_shape = [jax.ShapeDtypeStruct((S, w), dt) for w, dt in row_outs]
    out_shape += [jax.ShapeDtypeStruct(shp, F32) for shp in acc_outs]
    sem = ("arbitrary",) if acc_outs else ("parallel",)
    return pl.pallas_call(
        functools.partial(body), name=name, grid=(S // t,), in_specs=in_specs, out_specs=out_specs,
        out_shape=out_shape, compiler_params=_params(sem),
    )(*row_ins, *vec_ins)


def _accum(ref, val):
    @pl.when(pl.program_id(0) == 0)
    def _():
        ref[...] = jnp.zeros_like(ref)

    ref[...] += val


def _colsum(v):
    return jnp.sum(v, axis=0, keepdims=True)


def _rms_fwd(x, g):
    r = lax.rsqrt(jnp.mean(x * x, axis=-1, keepdims=True) + EPS)
    return x * r * g


def _rms_bwd(x, g, du):
    r = lax.rsqrt(jnp.mean(x * x, axis=-1, keepdims=True) + EPS)
    xn = x * r
    gdu = du * g
    dx = r * (gdu - xn * jnp.mean(xn * gdu, axis=-1, keepdims=True))
    return dx, _colsum(du * xn)


def _sigmoid(v):
    return 1.0 / (1.0 + jnp.exp(-v))


def _rope(v, c, sa, sb, sign):
    return v * c + sign * (pltpu.roll(v, HEAD_PAD - QK_ROPE // 2, 1) * sa + pltpu.roll(v, QK_ROPE // 2, 1) * sb)


def _split3(v):
    hi = v.astype(BF16)
    r1 = v - hi.astype(F32)
    mid = r1.astype(BF16)
    lo = (r1 - mid.astype(F32)).astype(BF16)
    return hi, mid, lo


def _put_stats(base, stat, col):
    hi, mid, lo = _split3(stat)
    lane = lax.broadcasted_iota(jnp.int32, base.shape, 1)
    out = jnp.where(lane == col, hi, base)
    out = jnp.where(lane == col + 1, mid, out)
    return jnp.where(lane == col + 2, lo, out)


def _neg_ones(shape, col):
    lane = lax.broadcasted_iota(jnp.int32, shape, 1)
    return jnp.where((lane >= col) & (lane < col + 3), -1.0, 0.0).astype(F32)


def _shifted(ext, t):
    p = ext.shape[0]
    for b in range(8):
        rb = ext if b == 0 else pltpu.roll(ext, p - b, 0)
        for a in range(HALO // 8 + 1):
            if 8 * a + b <= HALO:
                yield 8 * a + b, rb[8 * a:8 * a + t]


def _conv_fwd(z0, conv_w, conv_b, ln_g, ln_b):
    S, C = z0.shape
    t = _pick(S, (512, 256, 128, 64, 32))
    per = t // HALO

    def body(cur_ref, prev_ref, w_ref, b_ref, g_ref, beta_ref, z1_ref, z3_ref, ext_ref):
        i = pl.program_id(0)
        ext_ref[0:HALO, :] = jnp.where(i > 0, prev_ref[...], 0.0)
        ext_ref[HALO:, :] = cur_ref[...]
        ext = ext_ref[...]
        acc = jnp.zeros((t, C), F32)
        for d, win in _shifted(ext, t):
            k = d - (HALO - CONV_WIDTH + 1)
            if 0 <= k < CONV_WIDTH:
                acc = acc + win * w_ref[k:k + 1, :]
        z1 = acc + b_ref[...]
        z1_ref[...] = z1
        mu = jnp.mean(z1, axis=-1, keepdims=True)
        zc = z1 - mu
        rs = lax.rsqrt(jnp.mean(zc * zc, axis=-1, keepdims=True) + EPS)
        z2 = zc * rs * g_ref[...] + beta_ref[...]
        z3_ref[...] = (z2 * _sigmoid(z2)).astype(BF16)

    vec = lambda v: pl.BlockSpec(v.shape, lambda i: (0, 0))
    return pl.pallas_call(
        body, name="conv_fwd", grid=(S // t,),
        in_specs=[pl.BlockSpec((t, C), lambda i: (i, 0)),
                  pl.BlockSpec((HALO, C), lambda i: (jnp.maximum(i * per - 1, 0), 0)),
                  vec(conv_w), vec(conv_b), vec(ln_g), vec(ln_b)],
        out_specs=[pl.BlockSpec((t, C), lambda i: (i, 0)), pl.BlockSpec((t, C), lambda i: (i, 0))],
        out_shape=[jax.ShapeDtypeStruct((S, C), F32), jax.ShapeDtypeStruct((S, C), BF16)],
        scratch_shapes=[pltpu.VMEM((t + HALO, C), F32)],
        compiler_params=_params(("parallel",)),
    )(z0, z0, conv_w, conv_b, ln_g, ln_b)


def _conv_bwd_norm(dz3, z1, ln_g, ln_b):
    C = z1.shape[1]

    def body(dz3_ref, z1_ref, g_ref, beta_ref, dz1_ref, dg_ref, dbeta_ref, dbias_ref):
        z1 = z1_ref[...]
        mu = jnp.mean(z1, axis=-1, keepdims=True)
        zc = z1 - mu
        rs = lax.rsqrt(jnp.mean(zc * zc, axis=-1, keepdims=True) + EPS)
        xh = zc * rs
        z2 = xh * g_ref[...] + beta_ref[...]
        sg = _sigmoid(z2)
        dz2 = dz3_ref[...] * (sg * (1.0 + z2 * (1.0 - sg)))
        dxh = dz2 * g_ref[...]
        dz1 = rs * (dxh - jnp.mean(dxh, axis=-1, keepdims=True) - xh * jnp.mean(dxh * xh, axis=-1, keepdims=True))
        dz1_ref[...] = dz1
        _accum(dg_ref, _colsum(dz2 * xh))
        _accum(dbeta_ref, _colsum(dz2))
        _accum(dbias_ref, _colsum(dz1))

    return _rows("conv_bwd_norm", body, [dz3, z1], [ln_g, ln_b], [(C, F32)], [(1, C)] * 3)


def _conv_bwd_taps(dz1, z0, conv_in, conv_w):
    S, C = z0.shape
    t = _pick(S, (512, 256, 128, 64, 32))
    per = t // HALO
    last = S // HALO - 1
    nt = S // t

    def body(dcur_ref, dnext_ref, zcur_ref, zprev_ref, cin_ref, w_ref, dcin_ref, dw_ref, dext_ref, zext_ref):
        i = pl.program_id(0)
        dcur = dcur_ref[...]
        dext_ref[0:t, :] = dcur
        dext_ref[t:, :] = jnp.where(i < nt - 1, dnext_ref[...], 0.0)
        zext_ref[0:HALO, :] = jnp.where(i > 0, zprev_ref[...], 0.0)
        zext_ref[HALO:, :] = zcur_ref[...]

        @pl.when(i == 0)
        def _():
            dw_ref[...] = jnp.zeros_like(dw_ref)

        dz0 = jnp.zeros((t, C), F32)
        for d, win in _shifted(dext_ref[...], t):
            k = CONV_WIDTH - 1 - d
            if 0 <= k < CONV_WIDTH:
                dz0 = dz0 + win * w_ref[k:k + 1, :]
        for d, win in _shifted(zext_ref[...], t):
            k = d - (HALO - CONV_WIDTH + 1)
            if 0 <= k < CONV_WIDTH:
                dw_ref[k:k + 1, :] += _colsum(dcur * win)
        a = cin_ref[:, 0:C]
        sg = _sigmoid(cin_ref[:, C:2 * C])
        dcin_ref[:, 0:C] = (dz0 * sg).astype(BF16)
        dcin_ref[:, C:2 * C] = (dz0 * a * sg * (1.0 - sg)).astype(BF16)

    return pl.pallas_call(
        body, name="conv_bwd_taps", grid=(nt,),
        in_specs=[pl.BlockSpec((t, C), lambda i: (i, 0)),
                  pl.BlockSpec((HALO, C), lambda i: (jnp.minimum((i + 1) * per, last), 0)),
                  pl.BlockSpec((t, C), lambda i: (i, 0)),
                  pl.BlockSpec((HALO, C), lambda i: (jnp.maximum(i * per - 1, 0), 0)),
                  pl.BlockSpec((t, 2 * C), lambda i: (i, 0)),
                  pl.BlockSpec(conv_w.shape, lambda i: (0, 0))],
        out_specs=[pl.BlockSpec((t, 2 * C), lambda i: (i, 0)), pl.BlockSpec((HALO, C), lambda i: (0, 0))],
        out_shape=[jax.ShapeDtypeStruct((S, 2 * C), BF16), jax.ShapeDtypeStruct((HALO, C), F32)],
        scratch_shapes=[pltpu.VMEM((t + HALO, C), F32), pltpu.VMEM((t + HALO, C), F32)],
        compiler_params=_params(("arbitrary",)),
    )(dz1, dz1, z0, z0, conv_in, conv_w)


def _lower_tri(shape, rows_are_queries):
    row = lax.broadcasted_iota(jnp.int32, shape, 0)
    col = lax.broadcasted_iota(jnp.int32, shape, 1)
    return (col <= row) if rows_are_queries else (row <= col)


HEADS_PER_STEP = 2
FWD_KEY_TILES = 4


def _flash_specs(S, t):
    w = HEADS_PER_STEP * HEAD_PAD
    blk = pl.BlockSpec((t, w), lambda h, i: (i, h))
    head = pl.BlockSpec((S, w), lambda h, i: (0, h))
    return blk, head


def _head_lanes(g):
    return slice(g * HEAD_PAD, (g + 1) * HEAD_PAD)


def _dot_nt(a, b):
    return lax.dot_general(a, b, _DIMS["nt"], preferred_element_type=F32)


def _dot_nn(a, b):
    return lax.dot_general(a, b, _DIMS["nn"], preferred_element_type=F32)


def _dot_tn(a, b):
    return lax.dot_general(a, b, _DIMS["tn"], preferred_element_type=F32)


def _flash_fwd(q, k, v):
    S = q.shape[0]
    t = _pick(S, (512, 256, 128))

    def body(q_ref, k_ref, v_ref, o_ref, qa_ref, m_ref, acc_ref):
        qi = pl.program_id(1)
        m_ref[...] = jnp.full_like(m_ref, NEG)
        acc_ref[...] = jnp.zeros_like(acc_ref)

        def step(first, width, diag):
            rows = pl.ds(pl.multiple_of(first, width), width)
            for g in range(HEADS_PER_STEP):
                hl = _head_lanes(g)
                s = _dot_nt(q_ref[:, hl], k_ref[rows, hl])
                if diag:
                    s = jnp.where(_lower_tri(s.shape, True), s, NEG)
                m_old = m_ref[g]
                m_new = jnp.maximum(m_old, jnp.max(s, axis=-1, keepdims=True))
                p = jnp.exp(s - m_new).astype(BF16)
                acc_ref[g] = jnp.exp(m_old - m_new) * acc_ref[g] + _dot_nn(p, v_ref[rows, hl])
                m_ref[g] = m_new

        def wide(kb, carry):
            step(kb * (FWD_KEY_TILES * t), FWD_KEY_TILES * t, False)
            return carry

        def single(ki, carry):
            step(ki * t, t, False)
            return carry

        lax.fori_loop(0, qi // FWD_KEY_TILES, wide, 0)
        lax.fori_loop((qi // FWD_KEY_TILES) * FWD_KEY_TILES, qi, single, 0)
        step(qi * t, t, True)
        for g in range(HEADS_PER_STEP):
            hl = _head_lanes(g)
            acc = acc_ref[g]
            l = -acc[:, STAT_COL_V:STAT_COL_V + 1]
            o_ref[:, hl] = (acc / l).astype(BF16)
            qa_ref[:, hl] = _put_stats(q_ref[:, hl], m_ref[g] + jnp.log(l), STAT_COL_QK)

    blk, head = _flash_specs(S, t)
    return pl.pallas_call(
        body, name="mla_flash_fwd", grid=(MLA_HEADS // HEADS_PER_STEP, S // t),
        in_specs=[blk, head, head], out_specs=[blk, blk],
        out_shape=[jax.ShapeDtypeStruct(q.shape, BF16), jax.ShapeDtypeStruct(q.shape, BF16)],
        scratch_shapes=[pltpu.VMEM((HEADS_PER_STEP, t, 1), F32), pltpu.VMEM((HEADS_PER_STEP, t, HEAD_PAD), F32)],
        compiler_params=_params(("parallel", "arbitrary")),
    )(q, k, v)


def _flash_dq(qa, k, v, doa):
    S = qa.shape[0]
    t = _pick(S, (512, 256, 128))

    def body(qa_ref, k_ref, v_ref, do_ref, dq_ref, acc_ref):
        qi = pl.program_id(1)
        acc_ref[...] = jnp.zeros_like(acc_ref)

        def step(ki, diag):
            rows = pl.ds(pl.multiple_of(ki * t, t), t)
            for g in range(HEADS_PER_STEP):
                hl = _head_lanes(g)
                kk = k_ref[rows, hl]
                s = _dot_nt(qa_ref[:, hl], kk)
                if diag:
                    s = jnp.where(_lower_tri(s.shape, True), s, NEG)
                ds = (jnp.exp(s) * _dot_nt(do_ref[:, hl], v_ref[rows, hl])).astype(BF16)
                acc_ref[:, hl] += _dot_nn(ds, kk)

        def loop(ki, carry):
            step(ki, False)
            return carry

        lax.fori_loop(0, qi, loop, 0)
        step(qi, True)
        dq_ref[...] = acc_ref[...]

    blk, head = _flash_specs(S, t)
    return pl.pallas_call(
        body, name="mla_flash_dq", grid=(MLA_HEADS // HEADS_PER_STEP, S // t),
        in_specs=[blk, head, head, blk], out_specs=blk,
        out_shape=jax.ShapeDtypeStruct(qa.shape, F32),
        scratch_shapes=[pltpu.VMEM((t, HEADS_PER_STEP * HEAD_PAD), F32)],
        compiler_params=_params(("parallel", "arbitrary")),
    )(qa, k, v, doa)


def _flash_dkv(qa, k, v, doa):
    S = qa.shape[0]
    t = _pick(S, (512, 256, 128))
    n = S // t

    def body(qa_ref, k_ref, v_ref, do_ref, dk_ref, dv_ref, dk_acc, dv_acc):
        kj = pl.program_id(1)
        dk_acc[...] = jnp.zeros_like(dk_acc)
        dv_acc[...] = jnp.zeros_like(dv_acc)

        def step(qi, diag):
            rows = pl.ds(pl.multiple_of(qi * t, t), t)
            for g in range(HEADS_PER_STEP):
                hl = _head_lanes(g)
                qa, do = qa_ref[rows, hl], do_ref[rows, hl]
                st = _dot_nt(k_ref[:, hl], qa)
                if diag:
                    st = jnp.where(_lower_tri(st.shape, False), st, NEG)
                pt = jnp.exp(st)
                dst = (pt * _dot_nt(v_ref[:, hl], do)).astype(BF16)
                dv_acc[:, hl] += _dot_nn(pt.astype(BF16), do)
                dk_acc[:, hl] += _dot_nn(dst, qa)

        def loop(qi, carry):
            step(qi, False)
            return carry

        step(kj, True)
        lax.fori_loop(kj + 1, n, loop, 0)
        dk_ref[...] = dk_acc[...]
        dv_ref[...] = dv_acc[...].astype(BF16)

    blk, head = _flash_specs(S, t)
    w = HEADS_PER_STEP * HEAD_PAD
    return pl.pallas_call(
        body, name="mla_flash_dkv", grid=(MLA_HEADS // HEADS_PER_STEP, n),
        in_specs=[head, blk, blk, head], out_specs=[blk, blk],
        out_shape=[jax.ShapeDtypeStruct(qa.shape, F32), jax.ShapeDtypeStruct(qa.shape, BF16)],
        scratch_shapes=[pltpu.VMEM((t, w), F32), pltpu.VMEM((t, w), F32)],
        compiler_params=_params(("parallel", "arbitrary")),
    )(qa, k, v, doa)


def _xattn_fwd(xq, kvx):
    W = X_HEADS * X_HEAD_DIM

    def body(q_ref, kv_ref, o_ref):
        for h in range(X_HEADS):
            lo = h * X_HEAD_DIM
            s = _dot_nt(q_ref[:, lo:lo + X_HEAD_DIM], kv_ref[:, lo:lo + X_HEAD_DIM])
            p = jnp.exp(s - jnp.max(s, axis=-1, keepdims=True))
            p = p / jnp.sum(p, axis=-1, keepdims=True)
            o_ref[:, lo:lo + X_HEAD_DIM] = _dot_nn(p.astype(BF16), kv_ref[:, W + lo:W + lo + X_HEAD_DIM]).astype(BF16)

    return _rows("xattn_fwd", body, [xq], [kvx], [(W, BF16)])[0]


def _xattn_bwd(xq, kvx, dox):
    W = X_HEADS * X_HEAD_DIM
    scale = X_HEAD_DIM ** -0.5

    def body(q_ref, do_ref, kv_ref, dq_ref, dkv_ref):
        @pl.when(pl.program_id(0) == 0)
        def _():
            dkv_ref[...] = jnp.zeros_like(dkv_ref)

        for h in range(X_HEADS):
            lo = h * X_HEAD_DIM
            q, k = q_ref[:, lo:lo + X_HEAD_DIM], kv_ref[:, lo:lo + X_HEAD_DIM]
            v, do = kv_ref[:, W + lo:W + lo + X_HEAD_DIM], do_ref[:, lo:lo + X_HEAD_DIM]
            s = _dot_nt(q, k)
            p = jnp.exp(s - jnp.max(s, axis=-1, keepdims=True))
            p = p / jnp.sum(p, axis=-1, keepdims=True)
            dp = _dot_nt(do, v)
            ds = (p * (dp - jnp.sum(dp * p, axis=-1, keepdims=True))).astype(BF16)
            dq_ref[:, lo:lo + X_HEAD_DIM] = (_dot_nn(ds, k) * scale).astype(BF16)
            dkv_ref[:, lo:lo + X_HEAD_DIM] += _dot_tn(ds, q)
            dkv_ref[:, W + lo:W + lo + X_HEAD_DIM] += _dot_tn(p.astype(BF16), do)

    return _rows("xattn_bwd", body, [xq, dox], [kvx], [(W, BF16)], [kvx.shape])


def _adamw(name, w, g, m, v):
    shape = w.shape
    w2, g2, m2, v2 = [a.reshape(-1, shape[-1]) for a in (w, g, m, v)]
    R, C = w2.shape
    t = _pick(R, (256, 128, 64, 32, 16, 8))
    c1 = 1.0 / (1.0 - ADAM_B1 ** ADAM_STEP)
    c2 = 1.0 / (1.0 - ADAM_B2 ** ADAM_STEP)

    def body(w_ref, g_ref, m_ref, v_ref, d_ref, nm_ref, nv_ref):
        g = g_ref[...]
        nm = ADAM_B1 * m_ref[...] + (1.0 - ADAM_B1) * g
        nv = ADAM_B2 * v_ref[...] + (1.0 - ADAM_B2) * (g * g)
        d_ref[...] = -ADAM_LR * ((nm * c1) / (jnp.sqrt(nv * c2) + ADAM_EPS) + ADAM_WD * w_ref[...])
        nm_ref[...] = nm
        nv_ref[...] = nv

    spec = pl.BlockSpec((t, C), lambda i: (i, 0))
    outs = pl.pallas_call(
        body, name=name, grid=(R // t,), in_specs=[spec] * 4, out_specs=[spec] * 3,
        out_shape=[jax.ShapeDtypeStruct((R, C), F32)] * 3, compiler_params=_params(("parallel",)),
    )(w2, g2, m2, v2)
    return [o.reshape(shape) for o in outs]


def _place():
    x, y, c = lax.axis_index("x"), lax.axis_index("y"), lax.axis_index("c")
    return x, y, c, [(1 - x, y), (x, 1 - y), (1 - x, 1 - y)]


_ANY = pl.BlockSpec(memory_space=pl.ANY)


_HBM = pl.BlockSpec(memory_space=pltpu.HBM)
_SEM = pl.BlockSpec(memory_space=pltpu.SEMAPHORE)
_SIDE_EFFECT = pltpu.SideEffectType.DATAFLOW_SIDE_EFFECTING


def _gather_copy(src, land, slot, send, recv, k, chip, c):
    return pltpu.make_async_remote_copy(src_ref=src, dst_ref=land.at[slot], send_sem=send.at[k], recv_sem=recv.at[k],
                                        device_id=(chip[0], chip[1], c), device_id_type=MESH)


def _gather_start(shards, groups):
    n, ng = len(shards), len(groups)
    lands = [jnp.broadcast_to(s[None], (N_CHIPS,) + s.shape) for s in shards]

    def body(*refs):
        ins, lnd = refs[:n], refs[n:2 * n]
        sends, recvs = refs[2 * n:2 * n + ng], refs[2 * n + ng:2 * n + 2 * ng]
        token = refs[-1]
        x, y, c, chips = _place()
        for gi, group in enumerate(groups):
            for pos, w in enumerate(group):
                for j, chip in enumerate(chips):
                    _gather_copy(ins[w], lnd[w], 2 * x + y, sends[gi], recvs[gi], 3 * pos + j, chip, c).start()
        token[...] = jnp.zeros_like(token)

    sems = [pltpu.SemaphoreType.DMA((3 * len(g),)) for g in groups]
    res = pl.pallas_call(
        body, name="gather_weights_start",
        out_shape=sems + sems + [pltpu.HBM(a.shape, a.dtype) for a in shards + lands] + [jax.ShapeDtypeStruct((8, LANES), F32)],
        in_specs=[_HBM] * (2 * n),
        out_specs=[_SEM] * (2 * ng) + [_HBM] * (2 * n) + [pl.BlockSpec(memory_space=pltpu.VMEM)],
        input_output_aliases={i: 2 * ng + i for i in range(2 * n)},
        compiler_params=pltpu.CompilerParams(has_side_effects=_SIDE_EFFECT),
    )(*[pltpu.with_memory_space_constraint(a, pltpu.HBM) for a in shards + lands])
    sem_pairs = list(zip(res[:ng], res[ng:2 * ng]))
    return sem_pairs, res[2 * ng:2 * ng + n], res[2 * ng + n:2 * ng + 2 * n], res[-1]


def _gather_wait(name, sem_pair, shards_thru, lands_thru, after):
    m = len(shards_thru)

    def body(*refs):
        ins, lnd = refs[:m], refs[m:2 * m]
        send, recv = refs[2 * m], refs[2 * m + 1]
        x, y, c, chips = _place()
        for pos in range(m):
            for j, chip in enumerate(chips):
                cp = _gather_copy(ins[pos], lnd[pos], 2 * chip[0] + chip[1], send, recv, 3 * pos + j, chip, c)
                cp.wait_send()
                cp.wait_recv()

    res = pl.pallas_call(
        body, name=name,
        out_shape=[pltpu.HBM(a.shape, a.dtype) for a in list(shards_thru) + list(lands_thru)],
        in_specs=[_HBM] * (2 * m) + [_SEM, _SEM, _ANY], out_specs=[_HBM] * (2 * m),
        input_output_aliases={i: i for i in range(2 * m)},
        compiler_params=pltpu.CompilerParams(has_side_effects=_SIDE_EFFECT),
    )(*shards_thru, *lands_thru, *sem_pair, after)
    return res[m:]


def _pair_exchange(pack):
    def body(p_ref, o_ref, send, recv):
        x, y, c, _ = _place()
        cp = pltpu.make_async_remote_copy(src_ref=p_ref.at[1 - c], dst_ref=o_ref, send_sem=send, recv_sem=recv,
                                          device_id=(x, y, 1 - c), device_id_type=MESH)
        cp.start()
        cp.wait()

    return pl.pallas_call(
        body, name="grad_pair_exchange", in_specs=[_ANY], out_specs=_ANY,
        out_shape=jax.ShapeDtypeStruct(pack.shape[1:], pack.dtype),
        scratch_shapes=[pltpu.SemaphoreType.DMA, pltpu.SemaphoreType.DMA],
    )(pack)


def _chip_exchange(part):
    def body(p_ref, o_ref, send, recv, loc):
        x, y, c, chips = _place()
        mine = 2 * x + y
        lc = pltpu.make_async_copy(p_ref.at[mine], o_ref.at[mine], loc)
        lc.start()
        cps = []
        for j, (px, py) in enumerate(chips):
            cp = pltpu.make_async_remote_copy(src_ref=p_ref.at[2 * px + py], dst_ref=o_ref.at[mine], send_sem=send.at[j],
                                              recv_sem=recv.at[j], device_id=(px, py, c), device_id_type=MESH)
            cp.start()
            cps.append(cp)
        for j, (px, py) in enumerate(chips):
            pltpu.make_async_remote_copy(src_ref=p_ref.at[mine], dst_ref=o_ref.at[2 * px + py], send_sem=send.at[j],
                                         recv_sem=recv.at[j], device_id=(px, py, c), device_id_type=MESH).wait_recv()
        for cp in cps:
            cp.wait_send()
        lc.wait()

    return pl.pallas_call(
        body, name="grad_chip_exchange", in_specs=[_ANY], out_specs=_ANY,
        out_shape=jax.ShapeDtypeStruct(part.shape, part.dtype),
        scratch_shapes=[pltpu.SemaphoreType.DMA((3,)), pltpu.SemaphoreType.DMA((3,)), pltpu.SemaphoreType.DMA],
    )(part)


def _pair_share(half):
    def body(h_ref, o_ref, send, recv):
        x, y, c, _ = _place()
        cp = pltpu.make_async_remote_copy(src_ref=h_ref, dst_ref=o_ref, send_sem=send, recv_sem=recv,
                                          device_id=(x, y, 1 - c), device_id_type=MESH)
        cp.start()
        cp.wait()

    got = pl.pallas_call(
        body, name="grad_pair_share", in_specs=[_ANY], out_specs=_ANY,
        out_shape=jax.ShapeDtypeStruct(half.shape, half.dtype),
        scratch_shapes=[pltpu.SemaphoreType.DMA, pltpu.SemaphoreType.DMA],
    )(half)
    south = lax.axis_index("c") == 0
    return jnp.stack([jnp.where(south, half, got), jnp.where(south, got, half)])


def _sum_over_devices(block):
    R, L = block.shape

    def gather(b_ref, o_ref, send, recv, loc):
        x, y, c, _ = _place()
        lc = pltpu.make_async_copy(b_ref, o_ref.at[4 * x + 2 * y + c], loc)
        lc.start()
        peers = [(1 - x if dx else x, 1 - y if dy else y, 1 - c if dc else c)
                 for dx in (0, 1) for dy in (0, 1) for dc in (0, 1) if dx or dy or dc]
        cps = []
        for j, peer in enumerate(peers):
            cp = pltpu.make_async_remote_copy(src_ref=b_ref, dst_ref=o_ref.at[4 * x + 2 * y + c], send_sem=send.at[j],
                                              recv_sem=recv.at[j], device_id=peer, device_id_type=MESH)
            cp.start()
            cps.append(cp)
        for j, (px, py, pc) in enumerate(peers):
            pltpu.make_async_remote_copy(src_ref=b_ref, dst_ref=o_ref.at[4 * px + 2 * py + pc], send_sem=send.at[j],
                                         recv_sem=recv.at[j], device_id=(px, py, pc), device_id_type=MESH).wait_recv()
        for cp in cps:
            cp.wait_send()
        lc.wait()

    blocks = pl.pallas_call(
        gather, name="small_grads_gather", in_specs=[_ANY], out_specs=_ANY,
        out_shape=jax.ShapeDtypeStruct((N_DEV, R, L), F32),
        scratch_shapes=[pltpu.SemaphoreType.DMA((N_DEV - 1,)), pltpu.SemaphoreType.DMA((N_DEV - 1,)), pltpu.SemaphoreType.DMA],
    )(block)

    def add(b_ref, o_ref):
        total = b_ref[0]
        for d in range(1, N_DEV):
            total = total + b_ref[d]
        o_ref[...] = total

    return pl.pallas_call(add, name="small_grads_add", out_shape=jax.ShapeDtypeStruct((R, L), F32))(blocks)


def _pair_add(pack, got):
    _, nchip, R, L = pack.shape
    t = _pick(R, PACK_TILES)
    c = lax.axis_index("c").astype(jnp.int32).reshape(1)

    def body(c_ref, p_ref, g_ref, o_ref):
        o_ref[...] = (p_ref[...] + g_ref[...]).astype(BF16)

    return pl.pallas_call(
        body, name="grad_pair_add",
        grid_spec=pltpu.PrefetchScalarGridSpec(
            num_scalar_prefetch=1, grid=(nchip, R // t),
            in_specs=[pl.BlockSpec((None, None, t, L), lambda k, i, c_ref: (c_ref[0], k, i, 0)),
                      pl.BlockSpec((None, t, L), lambda k, i, c_ref: (k, i, 0))],
            out_specs=pl.BlockSpec((None, t, L), lambda k, i, c_ref: (k, i, 0))),
        out_shape=jax.ShapeDtypeStruct(got.shape, BF16), compiler_params=_params(("parallel", "parallel")),
    )(c, pack, got)


def _chip_add(parts):
    _, R, L = parts.shape
    t = _pick(R, PACK_TILES)

    def body(p_ref, o_ref):
        p = [p_ref[k].astype(F32) for k in range(N_CHIPS)]
        o_ref[...] = ((p[0] + p[1]) + p[2]) + p[3]

    return pl.pallas_call(
        body, name="grad_chip_add", grid=(R // t,),
        in_specs=[pl.BlockSpec((N_CHIPS, t, L), lambda i: (0, i, 0))], out_specs=pl.BlockSpec((t, L), lambda i: (i, 0)),
        out_shape=jax.ShapeDtypeStruct((R, L), F32), compiler_params=_params(("parallel",)),
    )(parts)


_CUT = (2 * CONV_CH, 2 * CONV_CH + Q_LORA, 2 * CONV_CH + Q_LORA + KV_LORA, 2 * CONV_CH + Q_LORA + KV_LORA + QK_ROPE)
_KR_AT = _CUT[2] + QK_NOPE


def _pad_last(a, n):
    return jnp.pad(a, [(0, 0)] * (a.ndim - 1) + [(0, n - a.shape[-1])])


def _layout_w_in(w_in):
    kr = jnp.pad(w_in[:, _CUT[2]:_CUT[3]], ((0, 0), (QK_NOPE, HEAD_PAD - QK_NOPE - QK_ROPE)))
    return jnp.concatenate([w_in[:, :_CUT[2]], kr, w_in[:, _CUT[3]:]], axis=1)


def _layout_weights(w):
    uq = _pad_last(w["w_uq"].reshape(Q_LORA, MLA_HEADS, QK_NOPE + QK_ROPE), HEAD_PAD).reshape(Q_LORA, MLA_HEADS * HEAD_PAD)
    ukv = w["w_ukv"].reshape(KV_LORA, MLA_HEADS, QK_NOPE + V_DIM)
    uk = _pad_last(ukv[:, :, :QK_NOPE], HEAD_PAD).reshape(KV_LORA, MLA_HEADS * HEAD_PAD)
    uv = _pad_last(ukv[:, :, QK_NOPE:], HEAD_PAD).reshape(KV_LORA, MLA_HEADS * HEAD_PAD)
    mo = jnp.pad(w["w_mla_out"].reshape(MLA_HEADS, V_DIM, D_MODEL), ((0, 0), (0, HEAD_PAD - V_DIM), (0, 0)))
    return dict(
        w_uq=uq, w_ukv=jnp.concatenate([uk, uv], axis=1), w_mla_out=mo.reshape(MLA_HEADS * HEAD_PAD, D_MODEL),
        w_conv_out=w["w_conv_out"], w_out=w["w_out"], w_xq=w["w_xq"], w_xkv=w["w_xkv"], w_xo=w["w_xo"],
        w_mlp1=w["w_mlp1"], w_mlp2=w["w_mlp2"])


def _unlayout_grads(g):
    gi = g["w_in"]
    w_in = jnp.concatenate([gi[:, :_CUT[2]], gi[:, _KR_AT:_KR_AT + QK_ROPE], gi[:, _CUT[2] + HEAD_PAD:]], axis=1)
    uq = g["w_uq"].reshape(Q_LORA, MLA_HEADS, HEAD_PAD)[:, :, :QK_NOPE + QK_ROPE].reshape(Q_LORA, -1)
    gk = g["w_ukv"][:, :MLA_HEADS * HEAD_PAD].reshape(KV_LORA, MLA_HEADS, HEAD_PAD)[:, :, :QK_NOPE]
    gv = g["w_ukv"][:, MLA_HEADS * HEAD_PAD:].reshape(KV_LORA, MLA_HEADS, HEAD_PAD)[:, :, :V_DIM]
    ukv = jnp.concatenate([gk, gv], axis=2).reshape(KV_LORA, -1)
    mo = g["w_mla_out"].reshape(MLA_HEADS, HEAD_PAD, D_MODEL)[:, :V_DIM].reshape(MLA_HEADS * V_DIM, D_MODEL)
    out = dict(g)
    out.update(w_in=w_in, w_uq=uq, w_ukv=ukv, w_mla_out=mo)
    return out


def _rope_tables(positions):
    half = QK_ROPE // 2
    inv_freq = ROPE_THETA ** (-jnp.arange(half, dtype=F32) / half)
    ang = positions.astype(F32).reshape(-1, 1) * inv_freq
    cos, sin = jnp.cos(ang), jnp.sin(ang)
    S = cos.shape[0]
    z16, z32, z64 = jnp.zeros((S, half), F32), jnp.zeros((S, QK_ROPE), F32), jnp.zeros((S, QK_NOPE), F32)
    c = jnp.concatenate([jnp.ones((S, QK_NOPE), F32), cos, cos, z32], axis=1)
    sa = jnp.concatenate([z64, -sin, z16, z32], axis=1)
    sb = jnp.concatenate([z64, z16, sin, z32], axis=1)
    return c, sa, sb


def _local_step(x, mem, positions, target, w_in_fn, rest_fn, sm):
    S = x.shape[0]
    HW = MLA_HEADS * HEAD_PAD
    rope_c, rope_sa, rope_sb = _rope_tables(positions)
    qk_scale = (QK_NOPE + QK_ROPE) ** -0.5
    c0, c1, c2, c3 = _CUT[0], _CUT[1], _CUT[2], _CUT[2] + HEAD_PAD

    def k_rms1(x_ref, g_ref, u_ref):
        u_ref[...] = _rms_fwd(x_ref[...], g_ref[...]).astype(BF16)

    u1, = _rows("rms_mix", k_rms1, [x], [sm["norm_mix_g"]], [(D_MODEL, BF16)])
    w_in = w_in_fn(u1)

    def epi_glu(acc, xs, outs):
        a, gt = acc[:, 0:CONV_CH], acc[:, CONV_CH:2 * CONV_CH]
        outs[0][...] = acc[...]
        outs[1][...] = a * _sigmoid(gt)

    conv_in, z0 = _mm("proj_conv", u1, w_in[:, :c0], "nn", [(2 * CONV_CH, F32), (CONV_CH, F32)], epi_glu)
    c_q = _mm_plain("proj_cq", u1, w_in[:, c0:c1], "nn")
    c_kv = _mm_plain("proj_ckv", u1, w_in[:, c1:c2], "nn")
    kr_raw = _mm_plain("proj_krope", u1, w_in[:, c2:c3], "nn")

    def epi_sigmoid(acc, xs, outs):
        outs[0][...] = _sigmoid(acc[...])

    gates, = _mm("proj_gates", u1, w_in[:, c3:], "nn", [(2 * D_MODEL, F32)], epi_sigmoid)

    wl, conv_w = rest_fn(gates)
    z1, z3 = _conv_fwd(z0, conv_w, sm["conv_b"], sm["conv_ln_g"], sm["conv_ln_b"])
    conv_out = _mm_plain("conv_out", z3, wl["w_conv_out"], "nn")

    def k_lora_norm(cq_ref, ckv_ref, gq_ref, gkv_ref, qn_ref, kvn_ref):
        qn_ref[...] = _rms_fwd(cq_ref[...], gq_ref[...]).astype(BF16)
        kvn_ref[...] = _rms_fwd(ckv_ref[...], gkv_ref[...]).astype(BF16)

    qn, kvn = _rows("lora_norm", k_lora_norm, [c_q, c_kv], [sm["q_norm_g"], sm["kv_norm_g"]],
                    [(Q_LORA, BF16), (KV_LORA, BF16)])

    def epi_q(acc, xs, outs):
        c, sa, sb = xs[0][...], xs[1][...], xs[2][...]
        for h in range(MLA_HEADS):
            lo = h * HEAD_PAD
            outs[0][:, lo:lo + HEAD_PAD] = (_rope(acc[:, lo:lo + HEAD_PAD], c, sa, sb, 1.0) * qk_scale).astype(BF16)

    q_att, = _mm("q_up", qn, wl["w_uq"], "nn", [(HW, BF16)], epi_q, row_x=[rope_c, rope_sa, rope_sb], tn=HW)

    def epi_kv(acc, xs, outs):
        kr = _rope(xs[0][...], xs[1][...], xs[2][...], xs[3][...], 1.0)
        kr = kr + _neg_ones(kr.shape, STAT_COL_QK)
        vconst = _neg_ones(kr.shape, STAT_COL_V)
        for h in range(MLA_HEADS):
            lo = h * HEAD_PAD
            outs[0][:, lo:lo + HEAD_PAD] = (acc[:, lo:lo + HEAD_PAD] + kr).astype(BF16)
            outs[1][:, lo:lo + HEAD_PAD] = (acc[:, HW + lo:HW + lo + HEAD_PAD] + vconst).astype(BF16)

    k_att, v_att = _mm("kv_up", kvn, wl["w_ukv"], "nn", [(HW, BF16), (HW, BF16)], epi_kv,
                       row_x=[kr_raw, rope_c, rope_sa, rope_sb], tn=2 * HW)

    o_att, q_aug = _flash_fwd(q_att, k_att, v_att)
    mla_out = _mm_plain("mla_out", o_att, wl["w_mla_out"], "nn")

    def k_merge(g_ref, co_ref, mo_ref, out_ref):
        out_ref[...] = (g_ref[:, 0:D_MODEL] * co_ref[...] + g_ref[:, D_MODEL:] * mo_ref[...]).astype(BF16)

    merged, = _rows("merge", k_merge, [gates, conv_out, mla_out], [], [(D_MODEL, BF16)], tile=256)

    def epi_res_norm(acc, xs, outs):
        h = xs[0][...] + acc[...]
        outs[0][...] = h
        outs[1][...] = _rms_fwd(h, xs[1][...]).astype(BF16)

    h1, u2 = _mm("mix_out", merged, wl["w_out"], "nn", [(D_MODEL, F32), (D_MODEL, BF16)], epi_res_norm,
                 row_x=[x], vec_x=[sm["norm_xattn_g"]], tn=D_MODEL)

    xscale = X_HEAD_DIM ** -0.5

    def epi_scale(acc, xs, outs):
        outs[0][...] = (acc[...] * xscale).astype(BF16)

    xq, = _mm("xattn_q", u2, wl["w_xq"], "nn", [(X_HEADS * X_HEAD_DIM, BF16)], epi_scale)

    def k_mem_norm(m_ref, g_ref, o_ref):
        o_ref[...] = _rms_fwd(m_ref[...], g_ref[...]).astype(BF16)

    mem_n, = _rows("mem_norm", k_mem_norm, [mem], [sm["norm_mem_g"]], [(D_MODEL, BF16)])
    kvx = _mm_plain("xattn_kv", mem_n, wl["w_xkv"], "nn", dtype=BF16)
    ox = _xattn_fwd(xq, kvx)
    h2, u3 = _mm("xattn_out", ox, wl["w_xo"], "nn", [(D_MODEL, F32), (D_MODEL, BF16)], epi_res_norm,
                 row_x=[h1], vec_x=[sm["norm_mlp_g"]], tn=D_MODEL)

    def epi_relu2(acc, xs, outs):
        r = jnp.maximum(acc[...], 0.0)
        outs[0][...] = (r * r).astype(BF16)

    hid, = _mm("mlp_up", u3, wl["w_mlp1"], "nn", [(D_FF, BF16)], epi_relu2)

    def epi_res(acc, xs, outs):
        outs[0][...] = xs[0][...] + acc[...]

    h3, = _mm("mlp_down", hid, wl["w_mlp2"], "nn", [(D_MODEL, F32)], epi_res, row_x=[h2], tn=D_MODEL)

    def k_final(h_ref, t_ref, g_ref, dh_ref, dhb_ref, loss_ref, dg_ref):
        h, g = h_ref[...], g_ref[...]
        e = _rms_fwd(h, g) - t_ref[...]
        part = 0.5 * jnp.sum(jnp.mean(e * e, axis=-1, keepdims=True), axis=0, keepdims=True)
        _accum(loss_ref, jnp.broadcast_to(part, loss_ref.shape))
        dh, dg = _rms_bwd(h, g, e * (1.0 / D_MODEL))
        dh_ref[...] = dh
        dhb_ref[...] = dh.astype(BF16)
        _accum(dg_ref, dg)

    dh3, dh3b, loss, g_final = _rows("final_loss", k_final, [h3, target], [sm["final_norm_g"]],
                                     [(D_MODEL, F32), (D_MODEL, BF16)], [(1, LANES), (1, D_MODEL)])

    def epi_drelu2(acc, xs, outs):
        outs[0][...] = (acc[...] * (2.0 * jnp.sqrt(xs[0][...].astype(F32)))).astype(BF16)

    da1, = _mm("mlp_down_dx", dh3b, wl["w_mlp2"], "nt", [(D_FF, BF16)], epi_drelu2, tile_x=[hid])
    g_mlp2 = _mm_plain("mlp_down_dw", hid, dh3b, "tn")
    g_mlp1 = _mm_plain("mlp_up_dw", u3, da1, "tn")
    du3 = _mm_plain("mlp_up_dx", da1, wl["w_mlp1"], "nt")

    def k_norm_bwd(x_ref, du_ref, dres_ref, g_ref, dh_ref, dhb_ref, dg_ref):
        dx, dg = _rms_bwd(x_ref[...], g_ref[...], du_ref[...])
        dh = dres_ref[...] + dx
        dh_ref[...] = dh
        dhb_ref[...] = dh.astype(BF16)
        _accum(dg_ref, dg)

    def norm_bwd(name, xin, du, dres, g):
        return _rows(name, k_norm_bwd, [xin, du, dres], [g], [(D_MODEL, F32), (D_MODEL, BF16)], [(1, D_MODEL)], tile=256)

    dh2, dh2b, g_norm_mlp = norm_bwd("norm_mlp_bwd", h2, du3, dh3, sm["norm_mlp_g"])

    dox = _mm_plain("xattn_out_dx", dh2b, wl["w_xo"], "nt", dtype=BF16)
    g_xo = _mm_plain("xattn_out_dw", ox, dh2b, "tn")
    dxq, dkvx = _xattn_bwd(xq, kvx, dox)
    g_xq = _mm_plain("xattn_q_dw", u2, dxq, "tn")
    du2 = _mm_plain("xattn_q_dx", dxq, wl["w_xq"], "nt")
    g_xkv = _mm_plain("xattn_kv_dw", mem_n, dkvx, "tn")
    dmem_n = _mm_plain("xattn_kv_dx", dkvx, wl["w_xkv"], "nt")

    def k_mem_bwd(m_ref, d_ref, g_ref, dg_ref):
        _, dg = _rms_bwd(m_ref[...], g_ref[...], d_ref[...])
        _accum(dg_ref, dg)

    g_norm_mem, = _rows("mem_norm_bwd", k_mem_bwd, [mem, dmem_n], [sm["norm_mem_g"]], [], [(1, D_MODEL)])
    dh1, dh1b, g_norm_xattn = norm_bwd("norm_xattn_bwd", h1, du2, dh2, sm["norm_xattn_g"])

    dmerged = _mm_plain("mix_out_dx", dh1b, wl["w_out"], "nt")
    g_out = _mm_plain("mix_out_dw", merged, dh1b, "tn")

    def k_merge_bwd(dm_ref, g_ref, co_ref, mo_ref, dco_ref, dmo_ref, dgl_ref):
        dm = dm_ref[...]
        g0, g1 = g_ref[:, 0:D_MODEL], g_ref[:, D_MODEL:]
        dco_ref[...] = (dm * g0).astype(BF16)
        dmo_ref[...] = (dm * g1).astype(BF16)
        dgl_ref[:, 0:D_MODEL] = (dm * co_ref[...] * g0 * (1.0 - g0)).astype(BF16)
        dgl_ref[:, D_MODEL:] = (dm * mo_ref[...] * g1 * (1.0 - g1)).astype(BF16)

    dconv_out, dmla_out, dgl = _rows("merge_bwd", k_merge_bwd, [dmerged, gates, conv_out, mla_out], [],
                                     [(D_MODEL, BF16), (D_MODEL, BF16), (2 * D_MODEL, BF16)], tile=256)

    def epi_do(acc, xs, outs):
        for h in range(MLA_HEADS):
            lo = h * HEAD_PAD
            do = acc[:, lo:lo + HEAD_PAD]
            delta = jnp.sum(do * xs[0][:, lo:lo + HEAD_PAD].astype(F32), axis=-1, keepdims=True)
            outs[0][:, lo:lo + HEAD_PAD] = _put_stats(do.astype(BF16), delta, STAT_COL_V)

    do_aug, = _mm("mla_out_dx", dmla_out, wl["w_mla_out"], "nt", [(HW, BF16)], epi_do, row_x=[o_att], tn=HW)
    g_mla_out = _mm_plain("mla_out_dw", o_att, dmla_out, "tn")
    dq_att = _flash_dq(q_aug, k_att, v_att, do_aug)
    dk_att, dv_att = _flash_dkv(q_aug, k_att, v_att, do_aug)

    def k_rope_bwd(dq_ref, dk_ref, dv_ref, c_ref, sa_ref, sb_ref, dqr_ref, dkv_ref, dkr_ref):
        c, sa, sb = c_ref[...], sa_ref[...], sb_ref[...]
        lane = lax.broadcasted_iota(jnp.int32, c.shape, 1)
        nope = (lane < QK_NOPE).astype(F32)
        ropem = ((lane >= QK_NOPE) & (lane < QK_NOPE + QK_ROPE)).astype(F32)
        dkr = jnp.zeros(c.shape, F32)
        for h in range(MLA_HEADS):
            lo = h * HEAD_PAD
            dqr_ref[:, lo:lo + HEAD_PAD] = (_rope(dq_ref[:, lo:lo + HEAD_PAD], c, sa, sb, -1.0) * qk_scale).astype(BF16)
            dk = dk_ref[:, lo:lo + HEAD_PAD]
            dkv_ref[:, lo:lo + HEAD_PAD] = (dk * nope).astype(BF16)
            dkr = dkr + dk
        dkv_ref[:, HW:] = dv_ref[...]
        dkr_ref[...] = (_rope(dkr * ropem, c, sa, sb, -1.0) * ropem).astype(BF16)

    dq_raw, dkv_cat, dkr = _rows("rope_bwd", k_rope_bwd, [dq_att, dk_att, dv_att, rope_c, rope_sa, rope_sb], [],
                                 [(HW, BF16), (2 * HW, BF16), (HEAD_PAD, BF16)], tile=256)
    g_uq = _mm_plain("q_up_dw", qn, dq_raw, "tn")
    dqn = _mm_plain("q_up_dx", dq_raw, wl["w_uq"], "nt")
    g_ukv = _mm_plain("kv_up_dw", kvn, dkv_cat, "tn")
    dkvn = _mm_plain("kv_up_dx", dkv_cat, wl["w_ukv"], "nt")

    def k_lora_bwd(cq_ref, ckv_ref, dqn_ref, dkvn_ref, gq_ref, gkv_ref, dcq_ref, dckv_ref, dgq_ref, dgkv_ref):
        dcq, dgq = _rms_bwd(cq_ref[...], gq_ref[...], dqn_ref[...])
        dckv, dgkv = _rms_bwd(ckv_ref[...], gkv_ref[...], dkvn_ref[...])
        dcq_ref[...] = dcq.astype(BF16)
        dckv_ref[...] = dckv.astype(BF16)
        _accum(dgq_ref, dgq)
        _accum(dgkv_ref, dgkv)

    dc_q, dc_kv, g_q_norm, g_kv_norm = _rows("lora_norm_bwd", k_lora_bwd, [c_q, c_kv, dqn, dkvn],
                                              [sm["q_norm_g"], sm["kv_norm_g"]], [(Q_LORA, BF16), (KV_LORA, BF16)],
                                              [(1, Q_LORA), (1, KV_LORA)])

    dz3 = _mm_plain("conv_out_dx", dconv_out, wl["w_conv_out"], "nt")
    g_conv_out = _mm_plain("conv_out_dw", z3, dconv_out, "tn")
    dz1, g_ln_g, g_ln_b, g_conv_b = _conv_bwd_norm(dz3, z1, sm["conv_ln_g"], sm["conv_ln_b"])
    dconv_in, g_conv_w = _conv_bwd_taps(dz1, z0, conv_in, conv_w)

    dproj = jnp.concatenate([dconv_in, dc_q, dc_kv, dkr, dgl], axis=1)
    g_in = _mm_plain("proj_dw", u1, dproj, "tn")
    du1 = _mm_plain("proj_dx", dproj, w_in, "nt")

    def k_norm1_bwd(x_ref, du_ref, dres_ref, g_ref, dx_ref, dg_ref):
        dx, dg = _rms_bwd(x_ref[...], g_ref[...], du_ref[...])
        dx_ref[...] = dres_ref[...] + dx
        _accum(dg_ref, dg)

    grad_x, g_norm_mix = _rows("norm_mix_bwd", k_norm1_bwd, [x, du1, dh1], [sm["norm_mix_g"]], [(D_MODEL, F32)],
                               [(1, D_MODEL)], tile=256)

    grads = dict(norm_mix_g=g_norm_mix, w_in=g_in, conv_w=g_conv_w[:CONV_WIDTH], conv_b=g_conv_b, conv_ln_g=g_ln_g,
                 conv_ln_b=g_ln_b, w_conv_out=g_conv_out, q_norm_g=g_q_norm, w_uq=g_uq, kv_norm_g=g_kv_norm, w_ukv=g_ukv,
                 w_mla_out=g_mla_out, w_out=g_out, norm_xattn_g=g_norm_xattn, norm_mem_g=g_norm_mem, w_xq=g_xq,
                 w_xkv=g_xkv, w_xo=g_xo, norm_mlp_g=g_norm_mlp, w_mlp1=g_mlp1, w_mlp2=g_mlp2, final_norm_g=g_final)
    return loss, grad_x, grads


def _shard(a, k, axis):
    n = a.shape[axis] // N_CHIPS
    return lax.slice_in_dim(a, k * n, (k + 1) * n, axis=axis)


def _pack_rows(n_elems):
    half = -(-n_elems // (2 * LANES))
    return 2 * (-(-half // PACK_TILES[0]) * PACK_TILES[0])


def _pack_small(grads, loss):
    flat = jnp.concatenate([grads[n].reshape(-1) for n in SMALL] + [loss.reshape(-1)[:1]])
    rows = -(-flat.shape[0] // (8 * LANES)) * 8
    return jnp.pad(flat, (0, rows * LANES - flat.shape[0])).reshape(rows, LANES)


def _pack_grads(grads):
    packs = []
    for k in range(N_CHIPS):
        flat = jnp.concatenate([_shard(grads[n], k, SHARD_AXIS[n]).reshape(-1) for n in BIG])
        rows = _pack_rows(flat.shape[0])
        packs.append(jnp.pad(flat, (0, rows * LANES - flat.shape[0])).reshape(2, rows // 2, LANES))
    return jnp.stack(packs, axis=1)


def _unpack(flat, names, shapes):
    out, at = {}, 0
    for n in names:
        size = math.prod(shapes[n])
        out[n] = flat[at:at + size].reshape(shapes[n])
        at += size
    return out, at


def kernel(x, mem, positions, norm_mix_g, w_in, conv_w, conv_b, conv_ln_g, conv_ln_b, w_conv_out, q_norm_g, w_uq, kv_norm_g, w_ukv, w_mla_out, w_out, norm_xattn_g, norm_mem_g, w_xq, w_xkv, w_xo, norm_mlp_g, w_mlp1, w_mlp2, final_norm_g, loss_target, m_norm_mix_g, m_w_in, m_conv_w, m_conv_b, m_conv_ln_g, m_conv_ln_b, m_w_conv_out, m_q_norm_g, m_w_uq, m_kv_norm_g, m_w_ukv, m_w_mla_out, m_w_out, m_norm_xattn_g, m_norm_mem_g, m_w_xq, m_w_xkv, m_w_xo, m_norm_mlp_g, m_w_mlp1, m_w_mlp2, m_final_norm_g, v_norm_mix_g, v_w_in, v_conv_w, v_conv_b, v_conv_ln_g, v_conv_ln_b, v_w_conv_out, v_q_norm_g, v_w_uq, v_kv_norm_g, v_w_ukv, v_w_mla_out, v_w_out, v_norm_xattn_g, v_norm_mem_g, v_w_xq, v_w_xkv, v_w_xo, v_norm_mlp_g, v_w_mlp1, v_w_mlp2, v_final_norm_g):
    args = dict(locals())
    w = {n: args[n] for n in WEIGHTS}
    m = {n: args["m_" + n] for n in WEIGHTS}
    v = {n: args["v_" + n] for n in WEIGHTS}

    shards = [w[n][0].astype(F32 if n == "conv_w" else BF16) for n in BIG]
    sem_pairs, shards_thru, lands_thru, token = _gather_start(shards, [[0], list(range(1, len(BIG)))])

    def unshard(n, g):
        ax = SHARD_AXIS[n]
        return jnp.moveaxis(g, 0, ax).reshape(g.shape[1:1 + ax] + (N_CHIPS * g.shape[1 + ax],) + g.shape[2 + ax:])

    def w_in_fn(after):
        land, = _gather_wait("gather_weights_wait_w_in", sem_pairs[0], shards_thru[:1], lands_thru[:1], after)
        return _layout_w_in(unshard("w_in", land))

    def rest_fn(after):
        lands = _gather_wait("gather_weights_wait_rest", sem_pairs[1], shards_thru[1:], lands_thru[1:], after)
        full = {n: unshard(n, g) for n, g in zip(BIG[1:], lands)}
        return _layout_weights(full), full["conv_w"]

    sm = {n: w[n].reshape(1, -1) for n in SMALL}
    sm["norm_mix_g"] = sm["norm_mix_g"] + token[0, 0]

    loss, grad_x, grads = _local_step(x[0], mem[0], positions, loss_target[0], w_in_fn, rest_fn, sm)
    grads = _unlayout_grads(grads)

    small_flat = _sum_over_devices(_pack_small(grads, loss)).reshape(-1)
    pack = _pack_grads(grads)
    pair = _pair_add(pack, _pair_exchange(pack))
    half = _chip_add(_chip_exchange(pair))
    flat = _pair_share(half).reshape(-1)
    shapes = {n: w[n].shape[1:] if n in BIG else w[n].shape for n in WEIGHTS}
    g_sum, _ = _unpack(flat, BIG, shapes)
    g_small, at = _unpack(small_flat, SMALL, shapes)
    g_sum.update(g_small)
    loss_sum = small_flat[at]

    out_g, out_d, out_m, out_v = [], [], [], []
    for n in WEIGHTS:
        g = g_sum[n].reshape(w[n].shape)
        d, nm, nv = _adamw("adamw_" + n, w[n], g, m[n], v[n])
        out_g.append(g)
        out_d.append(d)
        out_m.append(nm)
        out_v.append(nv)
    return (loss_sum, grad_x[None], *out_g, *out_d, *out_m, *out_v)
```

```python
import functools
import math

import jax
import jax.numpy as jnp
from jax import lax
from jax.experimental import pallas as pl
from jax.experimental.pallas import tpu as pltpu

F32 = jnp.float32
BF16 = jnp.bfloat16
MESH = pl.DeviceIdType.MESH

D_MODEL = 1024
CONV_CH = 512
CONV_WIDTH = 31
MLA_HEADS = 8
QK_NOPE = 64
QK_ROPE = 32
V_DIM = 64
Q_LORA = 384
KV_LORA = 256
MEM_LEN = 256
X_HEADS = 4
X_HEAD_DIM = 128
D_FF = 4096
ROPE_THETA = 10000.0
EPS = 1e-6
HEAD_PAD = 128
STAT_COL_QK = QK_NOPE + QK_ROPE
STAT_COL_V = V_DIM
HALO = 32
N_CHIPS = 4
LANES = 128

ADAM_LR = 0.001
ADAM_B1 = 0.9
ADAM_B2 = 0.999
ADAM_EPS = 1e-08
ADAM_WD = 0.01
ADAM_STEP = 10

VMEM_LIMIT = 52 * 1024 * 1024
PACK_TILES = (1536, 1024, 512, 256, 128, 64, 32, 16, 8)
N_DEV = 8
NEG = -1e30

BIG = ["w_in", "conv_w", "w_conv_out", "w_uq", "w_ukv", "w_mla_out", "w_out", "w_xq", "w_xkv", "w_xo", "w_mlp1", "w_mlp2"]
FIRST_NEEDED = 2
SHARD_AXIS = {"w_in": 1, "w_conv_out": 1, "w_uq": 1, "w_ukv": 1, "w_mla_out": 1, "w_out": 0, "w_xq": 0, "w_xkv": 0,
              "w_xo": 1, "w_mlp1": 1, "w_mlp2": 0, "conv_w": 1}
SMALL = ["norm_mix_g", "conv_b", "conv_ln_g", "conv_ln_b", "q_norm_g", "kv_norm_g", "norm_xattn_g", "norm_mem_g",
         "norm_mlp_g", "final_norm_g"]
WEIGHTS = ["norm_mix_g", "w_in", "conv_w", "conv_b", "conv_ln_g", "conv_ln_b", "w_conv_out", "q_norm_g", "w_uq",
           "kv_norm_g", "w_ukv", "w_mla_out", "w_out", "norm_xattn_g", "norm_mem_g", "w_xq", "w_xkv", "w_xo",
           "norm_mlp_g", "w_mlp1", "w_mlp2", "final_norm_g"]


def _pick(n, prefs):
    for p in prefs:
        if n % p == 0:
            return p
    return n


def _params(sem):
    return pltpu.CompilerParams(dimension_semantics=sem, vmem_limit_bytes=VMEM_LIMIT)


_DIMS = {"nn": (((1,), (0,)), ((), ())), "nt": (((1,), (1,)), ((), ())), "tn": (((0,), (0,)), ((), ()))}


def _mm(name, a, b, mode, outs, epi, row_x=(), tile_x=(), vec_x=(), tm=None, tn=None, tk=None):
    if mode == "nn":
        (M, K), (_, N) = a.shape, b.shape
    elif mode == "nt":
        (M, K), (N, _) = a.shape, b.shape
    else:
        (K, M), (_, N) = a.shape, b.shape
    tm = tm or _pick(M, (1024, 512, 384, 256, 128))
    tn = tn or _pick(N, (1024, 768, 512, 384, 256, 128))
    tk = tk or _pick(K, (1024, 768, 512, 384, 256, 128))
    nk = K // tk
    rows_inner = nk == 1 and N // tn > 1
    grid = (N // tn, M // tm, nk) if rows_inner else (M // tm, N // tn, nk)

    def spec(shape, f):
        return pl.BlockSpec(shape, (lambda j, i, k: f(i, j, k)) if rows_inner else f)

    a_spec = spec((tk, tm), lambda i, j, k: (k, i)) if mode == "tn" else spec((tm, tk), lambda i, j, k: (i, k))
    b_spec = spec((tn, tk), lambda i, j, k: (j, k)) if mode == "nt" else spec((tk, tn), lambda i, j, k: (k, j))
    in_specs = [a_spec, b_spec]
    in_specs += [spec((tm, r.shape[1]), lambda i, j, k: (i, 0)) for r in row_x]
    in_specs += [spec((tm, tn), lambda i, j, k: (i, j)) for _ in tile_x]
    in_specs += [spec(v.shape, lambda i, j, k: (0, 0)) for v in vec_x]
    out_specs, out_shape = [], []
    for w, dt in outs:
        if tn == N:
            out_specs.append(spec((tm, w), lambda i, j, k: (i, 0)))
        else:
            assert w == N, (name, w, N)
            out_specs.append(spec((tm, tn), lambda i, j, k: (i, j)))
        out_shape.append(jax.ShapeDtypeStruct((M, w), dt))
    nx = len(row_x) + len(tile_x) + len(vec_x)
    dims = _DIMS[mode]

    def body(a_ref, b_ref, *rest):
        x_refs, out_refs, acc_ref = rest[:nx], rest[nx:nx + len(outs)], rest[-1]
        av, bv = a_ref[...], b_ref[...]
        if av.dtype != BF16:
            av = av.astype(BF16)
        if bv.dtype != BF16:
            bv = bv.astype(BF16)
        prod = lax.dot_general(av, bv, dims, preferred_element_type=F32)
        if nk == 1:
            acc_ref[...] = prod
            epi(acc_ref, x_refs, out_refs)
        else:
            k = pl.program_id(2)

            @pl.when(k == 0)
            def _():
                acc_ref[...] = prod

            @pl.when(k > 0)
            def _():
                acc_ref[...] += prod

            @pl.when(k == nk - 1)
            def _():
                epi(acc_ref, x_refs, out_refs)

    res = pl.pallas_call(
        body, name=name, grid=grid, in_specs=in_specs, out_specs=out_specs, out_shape=out_shape,
        scratch_shapes=[pltpu.VMEM((tm, tn), F32)],
        compiler_params=_params(("parallel", "parallel", "arbitrary")),
    )(a, b, *row_x, *tile_x, *vec_x)
    return res


def _epi_store(acc_ref, x_refs, out_refs):
    for o in out_refs:
        o[...] = acc_ref[...].astype(o.dtype)


def _mm_plain(name, a, b, mode, dtype=F32, **kw):
    n = b.shape[0] if mode == "nt" else b.shape[1]
    return _mm(name, a, b, mode, [(n, dtype)], _epi_store, **kw)[0]


def _rows(name, body, row_ins, vec_ins, row_outs, acc_outs=(), tile=512):
    S = row_ins[0].shape[0]
    t = _pick(S, (tile, 256, 128, 64, 32, 16, 8))
    in_specs = [pl.BlockSpec((t, r.shape[1]), lambda i: (i, 0)) for r in row_ins]
    in_specs += [pl.BlockSpec(v.shape, lambda i: (0, 0)) for v in vec_ins]
    out_specs = [pl.BlockSpec((t, w), lambda i: (i, 0)) for w, _ in row_outs]
    out_specs += [pl.BlockSpec(shp, lambda i: (0, 0)) for shp in acc_outs]
    out_shape = [jax.ShapeDtypeStruct((S, w), dt) for w, dt in row_outs]
    out_shape += [jax.ShapeDtypeStruct(shp, F32) for shp in acc_outs]
    sem = ("arbitrary",) if acc_outs else ("parallel",)
    return pl.pallas_call(
        functools.partial(body), name=name, grid=(S // t,), in_specs=in_specs, out_specs=out_specs,
        out_shape=out_shape, compiler_params=_params(sem),
    )(*row_ins, *vec_ins)


def _accum(ref, val):
    @pl.when(pl.program_id(0) == 0)
    def _():
        ref[...] = jnp.zeros_like(ref)

    ref[...] += val


def _colsum(v):
    return jnp.sum(v, axis=0, keepdims=True)


def _rms_fwd(x, g):
    r = lax.rsqrt(jnp.mean(x * x, axis=-1, keepdims=True) + EPS)
    return x * r * g


def _rms_bwd(x, g, du):
    r = lax.rsqrt(jnp.mean(x * x, axis=-1, keepdims=True) + EPS)
    xn = x * r
    gdu = du * g
    dx = r * (gdu - xn * jnp.mean(xn * gdu, axis=-1, keepdims=True))
    return dx, _colsum(du * xn)


def _sigmoid(v):
    return 1.0 / (1.0 + jnp.exp(-v))


def _rope(v, c, sa, sb, sign):
    return v * c + sign * (pltpu.roll(v, HEAD_PAD - QK_ROPE // 2, 1) * sa + pltpu.roll(v, QK_ROPE // 2, 1) * sb)


def _split3(v):
    hi = v.astype(BF16)
    r1 = v - hi.astype(F32)
    mid = r1.astype(BF16)
    lo = (r1 - mid.astype(F32)).astype(BF16)
    return hi, mid, lo


def _put_stats(base, stat, col):
    hi, mid, lo = _split3(stat)
    lane = lax.broadcasted_iota(jnp.int32, base.shape, 1)
    out = jnp.where(lane == col, hi, base)
    out = jnp.where(lane == col + 1, mid, out)
    return jnp.where(lane == col + 2, lo, out)


def _neg_ones(shape, col):
    lane = lax.broadcasted_iota(jnp.int32, shape, 1)
    return jnp.where((lane >= col) & (lane < col + 3), -1.0, 0.0).astype(F32)


def _shifted(ext, t):
    p = ext.shape[0]
    for b in range(8):
        rb = ext if b == 0 else pltpu.roll(ext, p - b, 0)
        for a in range(HALO // 8 + 1):
            if 8 * a + b <= HALO:
                yield 8 * a + b, rb[8 * a:8 * a + t]


def _conv_fwd(z0, conv_w, conv_b, ln_g, ln_b):
    S, C = z0.shape
    t = _pick(S, (512, 256, 128, 64, 32))
    per = t // HALO

    def body(cur_ref, prev_ref, w_ref, b_ref, g_ref, beta_ref, z1_ref, z3_ref, ext_ref):
        i = pl.program_id(0)
        ext_ref[0:HALO, :] = jnp.where(i > 0, prev_ref[...], 0.0)
        ext_ref[HALO:, :] = cur_ref[...]
        ext = ext_ref[...]
        acc = jnp.zeros((t, C), F32)
        for d, win in _shifted(ext, t):
            k = d - (HALO - CONV_WIDTH + 1)
            if 0 <= k < CONV_WIDTH:
                acc = acc + win * w_ref[k:k + 1, :]
        z1 = acc + b_ref[...]
        z1_ref[...] = z1
        mu = jnp.mean(z1, axis=-1, keepdims=True)
        zc = z1 - mu
        rs = lax.rsqrt(jnp.mean(zc * zc, axis=-1, keepdims=True) + EPS)
        z2 = zc * rs * g_ref[...] + beta_ref[...]
        z3_ref[...] = (z2 * _sigmoid(z2)).astype(BF16)

    vec = lambda v: pl.BlockSpec(v.shape, lambda i: (0, 0))
    return pl.pallas_call(
        body, name="conv_fwd", grid=(S // t,),
        in_specs=[pl.BlockSpec((t, C), lambda i: (i, 0)),
                  pl.BlockSpec((HALO, C), lambda i: (jnp.maximum(i * per - 1, 0), 0)),
                  vec(conv_w), vec(conv_b), vec(ln_g), vec(ln_b)],
        out_specs=[pl.BlockSpec((t, C), lambda i: (i, 0)), pl.BlockSpec((t, C), lambda i: (i, 0))],
        out_shape=[jax.ShapeDtypeStruct((S, C), F32), jax.ShapeDtypeStruct((S, C), BF16)],
        scratch_shapes=[pltpu.VMEM((t + HALO, C), F32)],
        compiler_params=_params(("parallel",)),
    )(z0, z0, conv_w, conv_b, ln_g, ln_b)


def _conv_bwd_norm(dz3, z1, ln_g, ln_b):
    C = z1.shape[1]

    def body(dz3_ref, z1_ref, g_ref, beta_ref, dz1_ref, dg_ref, dbeta_ref, dbias_ref):
        z1 = z1_ref[...]
        mu = jnp.mean(z1, axis=-1, keepdims=True)
        zc = z1 - mu
        rs = lax.rsqrt(jnp.mean(zc * zc, axis=-1, keepdims=True) + EPS)
        xh = zc * rs
        z2 = xh * g_ref[...] + beta_ref[...]
        sg = _sigmoid(z2)
        dz2 = dz3_ref[...] * (sg * (1.0 + z2 * (1.0 - sg)))
        dxh = dz2 * g_ref[...]
        dz1 = rs * (dxh - jnp.mean(dxh, axis=-1, keepdims=True) - xh * jnp.mean(dxh * xh, axis=-1, keepdims=True))
        dz1_ref[...] = dz1
        _accum(dg_ref, _colsum(dz2 * xh))
        _accum(dbeta_ref, _colsum(dz2))
        _accum(dbias_ref, _colsum(dz1))

    return _rows("conv_bwd_norm", body, [dz3, z1], [ln_g, ln_b], [(C, F32)], [(1, C)] * 3)


def _conv_bwd_taps(dz1, z0, conv_in, conv_w):
    S, C = z0.shape
    t = _pick(S, (512, 256, 128, 64, 32))
    per = t // HALO
    last = S // HALO - 1
    nt = S // t

    def body(dcur_ref, dnext_ref, zcur_ref, zprev_ref, cin_ref, w_ref, dcin_ref, dw_ref, dext_ref, zext_ref):
        i = pl.program_id(0)
        dcur = dcur_ref[...]
        dext_ref[0:t, :] = dcur
        dext_ref[t:, :] = jnp.where(i < nt - 1, dnext_ref[...], 0.0)
        zext_ref[0:HALO, :] = jnp.where(i > 0, zprev_ref[...], 0.0)
        zext_ref[HALO:, :] = zcur_ref[...]

        @pl.when(i == 0)
        def _():
            dw_ref[...] = jnp.zeros_like(dw_ref)

        dz0 = jnp.zeros((t, C), F32)
        for d, win in _shifted(dext_ref[...], t):
            k = CONV_WIDTH - 1 - d
            if 0 <= k < CONV_WIDTH:
                dz0 = dz0 + win * w_ref[k:k + 1, :]
        for d, win in _shifted(zext_ref[...], t):
            k = d - (HALO - CONV_WIDTH + 1)
            if 0 <= k < CONV_WIDTH:
                dw_ref[k:k + 1, :] += _colsum(dcur * win)
        a = cin_ref[:, 0:C]
        sg = _sigmoid(cin_ref[:, C:2 * C])
        dcin_ref[:, 0:C] = (dz0 * sg).astype(BF16)
        dcin_ref[:, C:2 * C] = (dz0 * a * sg * (1.0 - sg)).astype(BF16)

    return pl.pallas_call(
        body, name="conv_bwd_taps", grid=(nt,),
        in_specs=[pl.BlockSpec((t, C), lambda i: (i, 0)),
                  pl.BlockSpec((HALO, C), lambda i: (jnp.minimum((i + 1) * per, last), 0)),
                  pl.BlockSpec((t, C), lambda i: (i, 0)),
                  pl.BlockSpec((HALO, C), lambda i: (jnp.maximum(i * per - 1, 0), 0)),
                  pl.BlockSpec((t, 2 * C), lambda i: (i, 0)),
                  pl.BlockSpec(conv_w.shape, lambda i: (0, 0))],
        out_specs=[pl.BlockSpec((t, 2 * C), lambda i: (i, 0)), pl.BlockSpec((HALO, C), lambda i: (0, 0))],
        out_shape=[jax.ShapeDtypeStruct((S, 2 * C), BF16), jax.ShapeDtypeStruct((HALO, C), F32)],
        scratch_shapes=[pltpu.VMEM((t + HALO, C), F32), pltpu.VMEM((t + HALO, C), F32)],
        compiler_params=_params(("arbitrary",)),
    )(dz1, dz1, z0, z0, conv_in, conv_w)


def _lower_tri(shape, rows_are_queries):
    row = lax.broadcasted_iota(jnp.int32, shape, 0)
    col = lax.broadcasted_iota(jnp.int32, shape, 1)
    return (col <= row) if rows_are_queries else (row <= col)


HEADS_PER_STEP = 2
FWD_KEY_TILES = 4


def _flash_specs(S, t):
    w = HEADS_PER_STEP * HEAD_PAD
    blk = pl.BlockSpec((t, w), lambda h, i: (i, h))
    head = pl.BlockSpec((S, w), lambda h, i: (0, h))
    return blk, head


def _head_lanes(g):
    return slice(g * HEAD_PAD, (g + 1) * HEAD_PAD)


def _dot_nt(a, b):
    return lax.dot_general(a, b, _DIMS["nt"], preferred_element_type=F32)


def _dot_nn(a, b):
    return lax.dot_general(a, b, _DIMS["nn"], preferred_element_type=F32)


def _dot_tn(a, b):
    return lax.dot_general(a, b, _DIMS["tn"], preferred_element_type=F32)


def _flash_fwd(q, k, v):
    S = q.shape[0]
    t = _pick(S, (512, 256, 128))

    def body(q_ref, k_ref, v_ref, o_ref, qa_ref, m_ref, acc_ref):
        qi = pl.program_id(1)
        m_ref[...] = jnp.full_like(m_ref, NEG)
        acc_ref[...] = jnp.zeros_like(acc_ref)

        def step(first, width, diag):
            rows = pl.ds(pl.multiple_of(first, width), width)
            for g in range(HEADS_PER_STEP):
                hl = _head_lanes(g)
                s = _dot_nt(q_ref[:, hl], k_ref[rows, hl])
                if diag:
                    s = jnp.where(_lower_tri(s.shape, True), s, NEG)
                m_old = m_ref[g]
                m_new = jnp.maximum(m_old, jnp.max(s, axis=-1, keepdims=True))
                p = jnp.exp(s - m_new).astype(BF16)
                acc_ref[g] = jnp.exp(m_old - m_new) * acc_ref[g] + _dot_nn(p, v_ref[rows, hl])
                m_ref[g] = m_new

        def wide(kb, carry):
            step(kb * (FWD_KEY_TILES * t), FWD_KEY_TILES * t, False)
            return carry

        def single(ki, carry):
            step(ki * t, t, False)
            return carry

        lax.fori_loop(0, qi // FWD_KEY_TILES, wide, 0)
        lax.fori_loop((qi // FWD_KEY_TILES) * FWD_KEY_TILES, qi, single, 0)
        step(qi * t, t, True)
        for g in range(HEADS_PER_STEP):
            hl = _head_lanes(g)
            acc = acc_ref[g]
            l = -acc[:, STAT_COL_V:STAT_COL_V + 1]
            o_ref[:, hl] = (acc / l).astype(BF16)
            qa_ref[:, hl] = _put_stats(q_ref[:, hl], m_ref[g] + jnp.log(l), STAT_COL_QK)

    blk, head = _flash_specs(S, t)
    return pl.pallas_call(
        body, name="mla_flash_fwd", grid=(MLA_HEADS // HEADS_PER_STEP, S // t),
        in_specs=[blk, head, head], out_specs=[blk, blk],
        out_shape=[jax.ShapeDtypeStruct(q.shape, BF16), jax.ShapeDtypeStruct(q.shape, BF16)],
        scratch_shapes=[pltpu.VMEM((HEADS_PER_STEP, t, 1), F32), pltpu.VMEM((HEADS_PER_STEP, t, HEAD_PAD), F32)],
        compiler_params=_params(("parallel", "arbitrary")),
    )(q, k, v)


def _flash_bwd(qa, k, v, doa):
    S = qa.shape[0]
    t = _pick(S, (512, 256, 128))
    n = S // t

    def body(qa_ref, k_ref, v_ref, do_ref, dq_ref, dk_ref, dv_ref, dk_acc, dv_acc):
        kj = pl.program_id(1)

        @pl.when(kj == 0)
        def _():
            dq_ref[...] = jnp.zeros_like(dq_ref)

        dk_acc[...] = jnp.zeros_like(dk_acc)
        dv_acc[...] = jnp.zeros_like(dv_acc)

        def step(qi, diag):
            rows = pl.ds(pl.multiple_of(qi * t, t), t)
            for g in range(HEADS_PER_STEP):
                hl = _head_lanes(g)
                qa, do, kk = qa_ref[rows, hl], do_ref[rows, hl], k_ref[:, hl]
                st = _dot_nt(kk, qa)
                if diag:
                    st = jnp.where(_lower_tri(st.shape, False), st, NEG)
                pt = jnp.exp(st)
                dst = (pt * _dot_nt(v_ref[:, hl], do)).astype(BF16)
                dv_acc[:, hl] += _dot_nn(pt.astype(BF16), do)
                dk_acc[:, hl] += _dot_nn(dst, qa)
                dq_ref[rows, hl] += _dot_tn(dst, kk)

        def loop(qi, carry):
            step(qi, False)
            return carry

        step(kj, True)
        lax.fori_loop(kj + 1, n, loop, 0)
        dk_ref[...] = dk_acc[...]
        dv_ref[...] = dv_acc[...].astype(BF16)

    blk, head = _flash_specs(S, t)
    w = HEADS_PER_STEP * HEAD_PAD
    return pl.pallas_call(
        body, name="mla_flash_bwd", grid=(MLA_HEADS // HEADS_PER_STEP, n),
        in_specs=[head, blk, blk, head], out_specs=[head, blk, blk],
        out_shape=[jax.ShapeDtypeStruct(qa.shape, F32), jax.ShapeDtypeStruct(qa.shape, F32), jax.ShapeDtypeStruct(qa.shape, BF16)],
        scratch_shapes=[pltpu.VMEM((t, w), F32), pltpu.VMEM((t, w), F32)],
        compiler_params=_params(("parallel", "arbitrary")),
    )(qa, k, v, doa)


def _xattn_fwd(xq, kvx):
    W = X_HEADS * X_HEAD_DIM

    def body(q_ref, kv_ref, o_ref):
        for h in range(X_HEADS):
            lo = h * X_HEAD_DIM
            s = _dot_nt(q_ref[:, lo:lo + X_HEAD_DIM], kv_ref[:, lo:lo + X_HEAD_DIM])
            p = jnp.exp(s - jnp.max(s, axis=-1, keepdims=True))
            p = p / jnp.sum(p, axis=-1, keepdims=True)
            o_ref[:, lo:lo + X_HEAD_DIM] = _dot_nn(p.astype(BF16), kv_ref[:, W + lo:W + lo + X_HEAD_DIM]).astype(BF16)

    return _rows("xattn_fwd", body, [xq], [kvx], [(W, BF16)])[0]


def _xattn_bwd(xq, kvx, dox):
    W = X_HEADS * X_HEAD_DIM
    scale = X_HEAD_DIM ** -0.5

    def body(q_ref, do_ref, kv_ref, dq_ref, dkv_ref):
        @pl.when(pl.program_id(0) == 0)
        def _():
            dkv_ref[...] = jnp.zeros_like(dkv_ref)

        for h in range(X_HEADS):
            lo = h * X_HEAD_DIM
            q, k = q_ref[:, lo:lo + X_HEAD_DIM], kv_ref[:, lo:lo + X_HEAD_DIM]
            v, do = kv_ref[:, W + lo:W + lo + X_HEAD_DIM], do_ref[:, lo:lo + X_HEAD_DIM]
            s = _dot_nt(q, k)
            p = jnp.exp(s - jnp.max(s, axis=-1, keepdims=True))
            p = p / jnp.sum(p, axis=-1, keepdims=True)
            dp = _dot_nt(do, v)
            ds = (p * (dp - jnp.sum(dp * p, axis=-1, keepdims=True))).astype(BF16)
            dq_ref[:, lo:lo + X_HEAD_DIM] = (_dot_nn(ds, k) * scale).astype(BF16)
            dkv_ref[:, lo:lo + X_HEAD_DIM] += _dot_tn(ds, q)
            dkv_ref[:, W + lo:W + lo + X_HEAD_DIM] += _dot_tn(p.astype(BF16), do)

    return _rows("xattn_bwd", body, [xq, dox], [kvx], [(W, BF16)], [kvx.shape])


def _adamw(name, w, g, m, v):
    shape = w.shape
    w2, g2, m2, v2 = [a.reshape(-1, shape[-1]) for a in (w, g, m, v)]
    R, C = w2.shape
    t = _pick(R, (256, 128, 64, 32, 16, 8))
    c1 = 1.0 / (1.0 - ADAM_B1 ** ADAM_STEP)
    c2 = 1.0 / (1.0 - ADAM_B2 ** ADAM_STEP)

    def body(w_ref, g_ref, m_ref, v_ref, d_ref, nm_ref, nv_ref):
        g = g_ref[...]
        nm = ADAM_B1 * m_ref[...] + (1.0 - ADAM_B1) * g
        nv = ADAM_B2 * v_ref[...] + (1.0 - ADAM_B2) * (g * g)
        d_ref[...] = -ADAM_LR * ((nm * c1) / (jnp.sqrt(nv * c2) + ADAM_EPS) + ADAM_WD * w_ref[...])
        nm_ref[...] = nm
        nv_ref[...] = nv

    spec = pl.BlockSpec((t, C), lambda i: (i, 0))
    outs = pl.pallas_call(
        body, name=name, grid=(R // t,), in_specs=[spec] * 4, out_specs=[spec] * 3,
        out_shape=[jax.ShapeDtypeStruct((R, C), F32)] * 3, compiler_params=_params(("parallel",)),
    )(w2, g2, m2, v2)
    return [o.reshape(shape) for o in outs]


def _place():
    x, y, c = lax.axis_index("x"), lax.axis_index("y"), lax.axis_index("c")
    return x, y, c, [(1 - x, y), (x, 1 - y), (1 - x, 1 - y)]


_ANY = pl.BlockSpec(memory_space=pl.ANY)


_HBM = pl.BlockSpec(memory_space=pltpu.HBM)
_SEM = pl.BlockSpec(memory_space=pltpu.SEMAPHORE)
_SIDE_EFFECT = pltpu.SideEffectType.DATAFLOW_SIDE_EFFECTING


def _gather_copy(src, land, slot, send, recv, k, chip, c):
    return pltpu.make_async_remote_copy(src_ref=src, dst_ref=land.at[slot], send_sem=send.at[k], recv_sem=recv.at[k],
                                        device_id=(chip[0], chip[1], c), device_id_type=MESH)


def _gather_start(shards, groups):
    n, ng = len(shards), len(groups)
    lands = [jnp.broadcast_to(s[None], (N_CHIPS,) + s.shape) for s in shards]

    def body(*refs):
        ins, lnd = refs[:n], refs[n:2 * n]
        sends, recvs = refs[2 * n:2 * n + ng], refs[2 * n + ng:2 * n + 2 * ng]
        token = refs[-1]
        x, y, c, chips = _place()
        for gi, group in enumerate(groups):
            for pos, w in enumerate(group):
                for j, chip in enumerate(chips):
                    _gather_copy(ins[w], lnd[w], 2 * x + y, sends[gi], recvs[gi], 3 * pos + j, chip, c).start()
        token[...] = jnp.zeros_like(token)

    sems = [pltpu.SemaphoreType.DMA((3 * len(g),)) for g in groups]
    res = pl.pallas_call(
        body, name="gather_weights_start",
        out_shape=sems + sems + [pltpu.HBM(a.shape, a.dtype) for a in shards + lands] + [jax.ShapeDtypeStruct((8, LANES), F32)],
        in_specs=[_HBM] * (2 * n),
        out_specs=[_SEM] * (2 * ng) + [_HBM] * (2 * n) + [pl.BlockSpec(memory_space=pltpu.VMEM)],
        input_output_aliases={i: 2 * ng + i for i in range(2 * n)},
        compiler_params=pltpu.CompilerParams(has_side_effects=_SIDE_EFFECT),
    )(*[pltpu.with_memory_space_constraint(a, pltpu.HBM) for a in shards + lands])
    sem_pairs = list(zip(res[:ng], res[ng:2 * ng]))
    return sem_pairs, res[2 * ng:2 * ng + n], res[2 * ng + n:2 * ng + 2 * n], res[-1]


def _gather_wait(name, sem_pair, shards_thru, lands_thru, after):
    m = len(shards_thru)

    def body(*refs):
        ins, lnd = refs[:m], refs[m:2 * m]
        send, recv = refs[2 * m], refs[2 * m + 1]
        x, y, c, chips = _place()
        for pos in range(m):
            for j, chip in enumerate(chips):
                cp = _gather_copy(ins[pos], lnd[pos], 2 * chip[0] + chip[1], send, recv, 3 * pos + j, chip, c)
                cp.wait_send()
                cp.wait_recv()

    res = pl.pallas_call(
        body, name=name,
        out_shape=[pltpu.HBM(a.shape, a.dtype) for a in list(shards_thru) + list(lands_thru)],
        in_specs=[_HBM] * (2 * m) + [_SEM, _SEM, _ANY], out_specs=[_HBM] * (2 * m),
        input_output_aliases={i: i for i in range(2 * m)},
        compiler_params=pltpu.CompilerParams(has_side_effects=_SIDE_EFFECT),
    )(*shards_thru, *lands_thru, *sem_pair, after)
    return res[m:]


def _pair_exchange(pack):
    def body(p_ref, o_ref, send, recv):
        x, y, c, _ = _place()
        cp = pltpu.make_async_remote_copy(src_ref=p_ref.at[1 - c], dst_ref=o_ref, send_sem=send, recv_sem=recv,
                                          device_id=(x, y, 1 - c), device_id_type=MESH)
        cp.start()
        cp.wait()

    return pl.pallas_call(
        body, name="grad_pair_exchange", in_specs=[_ANY], out_specs=_ANY,
        out_shape=jax.ShapeDtypeStruct(pack.shape[1:], pack.dtype),
        scratch_shapes=[pltpu.SemaphoreType.DMA, pltpu.SemaphoreType.DMA],
    )(pack)


def _chip_exchange(part):
    def body(p_ref, o_ref, send, recv, loc):
        x, y, c, chips = _place()
        mine = 2 * x + y
        lc = pltpu.make_async_copy(p_ref.at[mine], o_ref.at[mine], loc)
        lc.start()
        cps = []
        for j, (px, py) in enumerate(chips):
            cp = pltpu.make_async_remote_copy(src_ref=p_ref.at[2 * px + py], dst_ref=o_ref.at[mine], send_sem=send.at[j],
                                              recv_sem=recv.at[j], device_id=(px, py, c), device_id_type=MESH)
            cp.start()
            cps.append(cp)
        for j, (px, py) in enumerate(chips):
            pltpu.make_async_remote_copy(src_ref=p_ref.at[mine], dst_ref=o_ref.at[2 * px + py], send_sem=send.at[j],
                                         recv_sem=recv.at[j], device_id=(px, py, c), device_id_type=MESH).wait_recv()
        for cp in cps:
            cp.wait_send()
        lc.wait()

    return pl.pallas_call(
        body, name="grad_chip_exchange", in_specs=[_ANY], out_specs=_ANY,
        out_shape=jax.ShapeDtypeStruct(part.shape, part.dtype),
        scratch_shapes=[pltpu.SemaphoreType.DMA((3,)), pltpu.SemaphoreType.DMA((3,)), pltpu.SemaphoreType.DMA],
    )(part)


def _pair_share(half):
    def body(h_ref, o_ref, send, recv):
        x, y, c, _ = _place()
        cp = pltpu.make_async_remote_copy(src_ref=h_ref, dst_ref=o_ref, send_sem=send, recv_sem=recv,
                                          device_id=(x, y, 1 - c), device_id_type=MESH)
        cp.start()
        cp.wait()

    got = pl.pallas_call(
        body, name="grad_pair_share", in_specs=[_ANY], out_specs=_ANY,
        out_shape=jax.ShapeDtypeStruct(half.shape, half.dtype),
        scratch_shapes=[pltpu.SemaphoreType.DMA, pltpu.SemaphoreType.DMA],
    )(half)
    south = lax.axis_index("c") == 0
    return jnp.stack([jnp.where(south, half, got), jnp.where(south, got, half)])


def _sum_over_devices(block):
    R, L = block.shape

    def gather(b_ref, o_ref, send, recv, loc):
        x, y, c, _ = _place()
        lc = pltpu.make_async_copy(b_ref, o_ref.at[4 * x + 2 * y + c], loc)
        lc.start()
        peers = [(1 - x if dx else x, 1 - y if dy else y, 1 - c if dc else c)
                 for dx in (0, 1) for dy in (0, 1) for dc in (0, 1) if dx or dy or dc]
        cps = []
        for j, peer in enumerate(peers):
            cp = pltpu.make_async_remote_copy(src_ref=b_ref, dst_ref=o_ref.at[4 * x + 2 * y + c], send_sem=send.at[j],
                                              recv_sem=recv.at[j], device_id=peer, device_id_type=MESH)
            cp.start()
            cps.append(cp)
        for j, (px, py, pc) in enumerate(peers):
            pltpu.make_async_remote_copy(src_ref=b_ref, dst_ref=o_ref.at[4 * px + 2 * py + pc], send_sem=send.at[j],
                                         recv_sem=recv.at[j], device_id=(px, py, pc), device_id_type=MESH).wait_recv()
        for cp in cps:
            cp.wait_send()
        lc.wait()

    blocks = pl.pallas_call(
        gather, name="small_grads_gather", in_specs=[_ANY], out_specs=_ANY,
        out_shape=jax.ShapeDtypeStruct((N_DEV, R, L), F32),
        scratch_shapes=[pltpu.SemaphoreType.DMA((N_DEV - 1,)), pltpu.SemaphoreType.DMA((N_DEV - 1,)), pltpu.SemaphoreType.DMA],
    )(block)

    def add(b_ref, o_ref):
        total = b_ref[0]
        for d in range(1, N_DEV):
            total = total + b_ref[d]
        o_ref[...] = total

    return pl.pallas_call(add, name="small_grads_add", out_shape=jax.ShapeDtypeStruct((R, L), F32))(blocks)


def _pair_add(pack, got):
    _, nchip, R, L = pack.shape
    t = _pick(R, PACK_TILES)
    c = lax.axis_index("c").astype(jnp.int32).reshape(1)

    def body(c_ref, p_ref, g_ref, o_ref):
        o_ref[...] = (p_ref[...].astype(F32) + g_ref[...].astype(F32)).astype(BF16)

    return pl.pallas_call(
        body, name="grad_pair_add",
        grid_spec=pltpu.PrefetchScalarGridSpec(
            num_scalar_prefetch=1, grid=(nchip, R // t),
            in_specs=[pl.BlockSpec((None, None, t, L), lambda k, i, c_ref: (c_ref[0], k, i, 0)),
                      pl.BlockSpec((None, t, L), lambda k, i, c_ref: (k, i, 0))],
            out_specs=pl.BlockSpec((None, t, L), lambda k, i, c_ref: (k, i, 0))),
        out_shape=jax.ShapeDtypeStruct(got.shape, BF16), compiler_params=_params(("parallel", "parallel")),
    )(c, pack, got)


def _chip_add(parts):
    _, R, L = parts.shape
    t = _pick(R, PACK_TILES)

    def body(p_ref, o_ref):
        p = [p_ref[k].astype(F32) for k in range(N_CHIPS)]
        o_ref[...] = ((p[0] + p[1]) + p[2]) + p[3]

    return pl.pallas_call(
        body, name="grad_chip_add", grid=(R // t,),
        in_specs=[pl.BlockSpec((N_CHIPS, t, L), lambda i: (0, i, 0))], out_specs=pl.BlockSpec((t, L), lambda i: (i, 0)),
        out_shape=jax.ShapeDtypeStruct((R, L), F32), compiler_params=_params(("parallel",)),
    )(parts)


_CUT = (2 * CONV_CH, 2 * CONV_CH + Q_LORA, 2 * CONV_CH + Q_LORA + KV_LORA, 2 * CONV_CH + Q_LORA + KV_LORA + QK_ROPE)
_KR_AT = _CUT[2] + QK_NOPE


def _pad_last(a, n):
    return jnp.pad(a, [(0, 0)] * (a.ndim - 1) + [(0, n - a.shape[-1])])


def _layout_w_in(w_in):
    kr = jnp.pad(w_in[:, _CUT[2]:_CUT[3]], ((0, 0), (QK_NOPE, HEAD_PAD - QK_NOPE - QK_ROPE)))
    return jnp.concatenate([w_in[:, :_CUT[2]], kr, w_in[:, _CUT[3]:]], axis=1)


def _layout_weights(w):
    uq = _pad_last(w["w_uq"].reshape(Q_LORA, MLA_HEADS, QK_NOPE + QK_ROPE), HEAD_PAD).reshape(Q_LORA, MLA_HEADS * HEAD_PAD)
    ukv = w["w_ukv"].reshape(KV_LORA, MLA_HEADS, QK_NOPE + V_DIM)
    uk = _pad_last(ukv[:, :, :QK_NOPE], HEAD_PAD).reshape(KV_LORA, MLA_HEADS * HEAD_PAD)
    uv = _pad_last(ukv[:, :, QK_NOPE:], HEAD_PAD).reshape(KV_LORA, MLA_HEADS * HEAD_PAD)
    mo = jnp.pad(w["w_mla_out"].reshape(MLA_HEADS, V_DIM, D_MODEL), ((0, 0), (0, HEAD_PAD - V_DIM), (0, 0)))
    return dict(
        w_uq=uq, w_ukv=jnp.concatenate([uk, uv], axis=1), w_mla_out=mo.reshape(MLA_HEADS * HEAD_PAD, D_MODEL),
        w_conv_out=w["w_conv_out"], w_out=w["w_out"], w_xq=w["w_xq"], w_xkv=w["w_xkv"], w_xo=w["w_xo"],
        w_mlp1=w["w_mlp1"], w_mlp2=w["w_mlp2"])


def _unlayout_grads(g):
    gi = g["w_in"]
    w_in = jnp.concatenate([gi[:, :_CUT[2]], gi[:, _KR_AT:_KR_AT + QK_ROPE], gi[:, _CUT[2] + HEAD_PAD:]], axis=1)
    uq = g["w_uq"].reshape(Q_LORA, MLA_HEADS, HEAD_PAD)[:, :, :QK_NOPE + QK_ROPE].reshape(Q_LORA, -1)
    gk = g["w_ukv"][:, :MLA_HEADS * HEAD_PAD].reshape(KV_LORA, MLA_HEADS, HEAD_PAD)[:, :, :QK_NOPE]
    gv = g["w_ukv"][:, MLA_HEADS * HEAD_PAD:].reshape(KV_LORA, MLA_HEADS, HEAD_PAD)[:, :, :V_DIM]
    ukv = jnp.concatenate([gk, gv], axis=2).reshape(KV_LORA, -1)
    mo = g["w_mla_out"].reshape(MLA_HEADS, HEAD_PAD, D_MODEL)[:, :V_DIM].reshape(MLA_HEADS * V_DIM, D_MODEL)
    out = dict(g)
    out.update(w_in=w_in, w_uq=uq, w_ukv=ukv, w_mla_out=mo)
    return out


def _rope_tables(positions):
    half = QK_ROPE // 2
    inv_freq = ROPE_THETA ** (-jnp.arange(half, dtype=F32) / half)
    ang = positions.astype(F32).reshape(-1, 1) * inv_freq
    cos, sin = jnp.cos(ang), jnp.sin(ang)
    S = cos.shape[0]
    z16, z32, z64 = jnp.zeros((S, half), F32), jnp.zeros((S, QK_ROPE), F32), jnp.zeros((S, QK_NOPE), F32)
    c = jnp.concatenate([jnp.ones((S, QK_NOPE), F32), cos, cos, z32], axis=1)
    sa = jnp.concatenate([z64, -sin, z16, z32], axis=1)
    sb = jnp.concatenate([z64, z16, sin, z32], axis=1)
    return c, sa, sb


def _local_step(x, mem, positions, target, first_fn, rest_fn, sm):
    S = x.shape[0]
    HW = MLA_HEADS * HEAD_PAD
    rope_c, rope_sa, rope_sb = _rope_tables(positions)
    qk_scale = (QK_NOPE + QK_ROPE) ** -0.5
    c0, c1, c2, c3 = _CUT[0], _CUT[1], _CUT[2], _CUT[2] + HEAD_PAD

    def k_rms1(x_ref, g_ref, u_ref):
        u_ref[...] = _rms_fwd(x_ref[...], g_ref[...]).astype(BF16)

    u1, = _rows("rms_mix", k_rms1, [x], [sm["norm_mix_g"]], [(D_MODEL, BF16)])
    w_in, conv_w = first_fn(u1)

    def epi_glu(acc, xs, outs):
        a, gt = acc[:, 0:CONV_CH], acc[:, CONV_CH:2 * CONV_CH]
        outs[0][...] = acc[...]
        outs[1][...] = a * _sigmoid(gt)

    conv_in, z0 = _mm("proj_conv", u1, w_in[:, :c0], "nn", [(2 * CONV_CH, F32), (CONV_CH, F32)], epi_glu)
    c_q = _mm_plain("proj_cq", u1, w_in[:, c0:c1], "nn")
    c_kv = _mm_plain("proj_ckv", u1, w_in[:, c1:c2], "nn")
    kr_raw = _mm_plain("proj_krope", u1, w_in[:, c2:c3], "nn")

    def epi_sigmoid(acc, xs, outs):
        outs[0][...] = _sigmoid(acc[...])

    gates, = _mm("proj_gates", u1, w_in[:, c3:], "nn", [(2 * D_MODEL, F32)], epi_sigmoid)

    z1, z3 = _conv_fwd(z0, conv_w, sm["conv_b"], sm["conv_ln_g"], sm["conv_ln_b"])
    wl = rest_fn(z1)
    conv_out = _mm_plain("conv_out", z3, wl["w_conv_out"], "nn")

    def k_lora_norm(cq_ref, ckv_ref, gq_ref, gkv_ref, qn_ref, kvn_ref):
        qn_ref[...] = _rms_fwd(cq_ref[...], gq_ref[...]).astype(BF16)
        kvn_ref[...] = _rms_fwd(ckv_ref[...], gkv_ref[...]).astype(BF16)

    qn, kvn = _rows("lora_norm", k_lora_norm, [c_q, c_kv], [sm["q_norm_g"], sm["kv_norm_g"]],
                    [(Q_LORA, BF16), (KV_LORA, BF16)])

    def epi_q(acc, xs, outs):
        c, sa, sb = xs[0][...], xs[1][...], xs[2][...]
        for h in range(MLA_HEADS):
            lo = h * HEAD_PAD
            outs[0][:, lo:lo + HEAD_PAD] = (_rope(acc[:, lo:lo + HEAD_PAD], c, sa, sb, 1.0) * qk_scale).astype(BF16)

    q_att, = _mm("q_up", qn, wl["w_uq"], "nn", [(HW, BF16)], epi_q, row_x=[rope_c, rope_sa, rope_sb], tn=HW)

    def epi_kv(acc, xs, outs):
        kr = _rope(xs[0][...], xs[1][...], xs[2][...], xs[3][...], 1.0)
        kr = kr + _neg_ones(kr.shape, STAT_COL_QK)
        vconst = _neg_ones(kr.shape, STAT_COL_V)
        for h in range(MLA_HEADS):
            lo = h * HEAD_PAD
            outs[0][:, lo:lo + HEAD_PAD] = (acc[:, lo:lo + HEAD_PAD] + kr).astype(BF16)
            outs[1][:, lo:lo + HEAD_PAD] = (acc[:, HW + lo:HW + lo + HEAD_PAD] + vconst).astype(BF16)

    k_att, v_att = _mm("kv_up", kvn, wl["w_ukv"], "nn", [(HW, BF16), (HW, BF16)], epi_kv,
                       row_x=[kr_raw, rope_c, rope_sa, rope_sb], tn=2 * HW)

    o_att, q_aug = _flash_fwd(q_att, k_att, v_att)
    mla_out = _mm_plain("mla_out", o_att, wl["w_mla_out"], "nn")

    def k_merge(g_ref, co_ref, mo_ref, out_ref):
        out_ref[...] = (g_ref[:, 0:D_MODEL] * co_ref[...] + g_ref[:, D_MODEL:] * mo_ref[...]).astype(BF16)

    merged, = _rows("merge", k_merge, [gates, conv_out, mla_out], [], [(D_MODEL, BF16)], tile=256)

    def epi_res_norm(acc, xs, outs):
        h = xs[0][...] + acc[...]
        outs[0][...] = h
        outs[1][...] = _rms_fwd(h, xs[1][...]).astype(BF16)

    h1, u2 = _mm("mix_out", merged, wl["w_out"], "nn", [(D_MODEL, F32), (D_MODEL, BF16)], epi_res_norm,
                 row_x=[x], vec_x=[sm["norm_xattn_g"]], tn=D_MODEL)

    xscale = X_HEAD_DIM ** -0.5

    def epi_scale(acc, xs, outs):
        outs[0][...] = (acc[...] * xscale).astype(BF16)

    xq, = _mm("xattn_q", u2, wl["w_xq"], "nn", [(X_HEADS * X_HEAD_DIM, BF16)], epi_scale)

    def k_mem_norm(m_ref, g_ref, o_ref):
        o_ref[...] = _rms_fwd(m_ref[...], g_ref[...]).astype(BF16)

    mem_n, = _rows("mem_norm", k_mem_norm, [mem], [sm["norm_mem_g"]], [(D_MODEL, BF16)])
    kvx = _mm_plain("xattn_kv", mem_n, wl["w_xkv"], "nn", dtype=BF16)
    ox = _xattn_fwd(xq, kvx)
    h2, u3 = _mm("xattn_out", ox, wl["w_xo"], "nn", [(D_MODEL, F32), (D_MODEL, BF16)], epi_res_norm,
                 row_x=[h1], vec_x=[sm["norm_mlp_g"]], tn=D_MODEL)

    def epi_relu2(acc, xs, outs):
        r = jnp.maximum(acc[...], 0.0)
        outs[0][...] = (r * r).astype(BF16)

    hid, = _mm("mlp_up", u3, wl["w_mlp1"], "nn", [(D_FF, BF16)], epi_relu2)

    def epi_res(acc, xs, outs):
        outs[0][...] = xs[0][...] + acc[...]

    h3, = _mm("mlp_down", hid, wl["w_mlp2"], "nn", [(D_MODEL, F32)], epi_res, row_x=[h2], tn=D_MODEL)

    def k_final(h_ref, t_ref, g_ref, dh_ref, dhb_ref, loss_ref, dg_ref):
        h, g = h_ref[...], g_ref[...]
        e = _rms_fwd(h, g) - t_ref[...]
        part = 0.5 * jnp.sum(jnp.mean(e * e, axis=-1, keepdims=True), axis=0, keepdims=True)
        _accum(loss_ref, jnp.broadcast_to(part, loss_ref.shape))
        dh, dg = _rms_bwd(h, g, e * (1.0 / D_MODEL))
        dh_ref[...] = dh
        dhb_ref[...] = dh.astype(BF16)
        _accum(dg_ref, dg)

    dh3, dh3b, loss, g_final = _rows("final_loss", k_final, [h3, target], [sm["final_norm_g"]],
                                     [(D_MODEL, F32), (D_MODEL, BF16)], [(1, LANES), (1, D_MODEL)])

    def epi_drelu2(acc, xs, outs):
        outs[0][...] = (acc[...] * (2.0 * jnp.sqrt(xs[0][...].astype(F32)))).astype(BF16)

    da1, = _mm("mlp_down_dx", dh3b, wl["w_mlp2"], "nt", [(D_FF, BF16)], epi_drelu2, tile_x=[hid])
    g_mlp2 = _mm_plain("mlp_down_dw", hid, dh3b, "tn")
    g_mlp1 = _mm_plain("mlp_up_dw", u3, da1, "tn")
    du3 = _mm_plain("mlp_up_dx", da1, wl["w_mlp1"], "nt")

    def k_norm_bwd(x_ref, du_ref, dres_ref, g_ref, dh_ref, dhb_ref, dg_ref):
        dx, dg = _rms_bwd(x_ref[...], g_ref[...], du_ref[...])
        dh = dres_ref[...] + dx
        dh_ref[...] = dh
        dhb_ref[...] = dh.astype(BF16)
        _accum(dg_ref, dg)

    def norm_bwd(name, xin, du, dres, g):
        return _rows(name, k_norm_bwd, [xin, du, dres], [g], [(D_MODEL, F32), (D_MODEL, BF16)], [(1, D_MODEL)], tile=256)

    dh2, dh2b, g_norm_mlp = norm_bwd("norm_mlp_bwd", h2, du3, dh3, sm["norm_mlp_g"])

    dox = _mm_plain("xattn_out_dx", dh2b, wl["w_xo"], "nt", dtype=BF16)
    g_xo = _mm_plain("xattn_out_dw", ox, dh2b, "tn")
    dxq, dkvx = _xattn_bwd(xq, kvx, dox)
    g_xq = _mm_plain("xattn_q_dw", u2, dxq, "tn")
    du2 = _mm_plain("xattn_q_dx", dxq, wl["w_xq"], "nt")
    g_xkv = _mm_plain("xattn_kv_dw", mem_n, dkvx, "tn")
    dmem_n = _mm_plain("xattn_kv_dx", dkvx, wl["w_xkv"], "nt")

    def k_mem_bwd(m_ref, d_ref, g_ref, dg_ref):
        _, dg = _rms_bwd(m_ref[...], g_ref[...], d_ref[...])
        _accum(dg_ref, dg)

    g_norm_mem, = _rows("mem_norm_bwd", k_mem_bwd, [mem, dmem_n], [sm["norm_mem_g"]], [], [(1, D_MODEL)])
    dh1, dh1b, g_norm_xattn = norm_bwd("norm_xattn_bwd", h1, du2, dh2, sm["norm_xattn_g"])

    dmerged = _mm_plain("mix_out_dx", dh1b, wl["w_out"], "nt")
    g_out = _mm_plain("mix_out_dw", merged, dh1b, "tn")

    def k_merge_bwd(dm_ref, g_ref, co_ref, mo_ref, dco_ref, dmo_ref, dgl_ref):
        dm = dm_ref[...]
        g0, g1 = g_ref[:, 0:D_MODEL], g_ref[:, D_MODEL:]
        dco_ref[...] = (dm * g0).astype(BF16)
        dmo_ref[...] = (dm * g1).astype(BF16)
        dgl_ref[:, 0:D_MODEL] = (dm * co_ref[...] * g0 * (1.0 - g0)).astype(BF16)
        dgl_ref[:, D_MODEL:] = (dm * mo_ref[...] * g1 * (1.0 - g1)).astype(BF16)

    dconv_out, dmla_out, dgl = _rows("merge_bwd", k_merge_bwd, [dmerged, gates, conv_out, mla_out], [],
                                     [(D_MODEL, BF16), (D_MODEL, BF16), (2 * D_MODEL, BF16)], tile=256)

    def epi_do(acc, xs, outs):
        for h in range(MLA_HEADS):
            lo = h * HEAD_PAD
            do = acc[:, lo:lo + HEAD_PAD]
            delta = jnp.sum(do * xs[0][:, lo:lo + HEAD_PAD].astype(F32), axis=-1, keepdims=True)
            outs[0][:, lo:lo + HEAD_PAD] = _put_stats(do.astype(BF16), delta, STAT_COL_V)

    do_aug, = _mm("mla_out_dx", dmla_out, wl["w_mla_out"], "nt", [(HW, BF16)], epi_do, row_x=[o_att], tn=HW)
    g_mla_out = _mm_plain("mla_out_dw", o_att, dmla_out, "tn")
    dq_att, dk_att, dv_att = _flash_bwd(q_aug, k_att, v_att, do_aug)

    def k_rope_bwd(dq_ref, dk_ref, dv_ref, c_ref, sa_ref, sb_ref, dqr_ref, dkv_ref, dkr_ref):
        c, sa, sb = c_ref[...], sa_ref[...], sb_ref[...]
        lane = lax.broadcasted_iota(jnp.int32, c.shape, 1)
        nope = (lane < QK_NOPE).astype(F32)
        ropem = ((lane >= QK_NOPE) & (lane < QK_NOPE + QK_ROPE)).astype(F32)
        dkr = jnp.zeros(c.shape, F32)
        for h in range(MLA_HEADS):
            lo = h * HEAD_PAD
            dqr_ref[:, lo:lo + HEAD_PAD] = (_rope(dq_ref[:, lo:lo + HEAD_PAD], c, sa, sb, -1.0) * qk_scale).astype(BF16)
            dk = dk_ref[:, lo:lo + HEAD_PAD]
            dkv_ref[:, lo:lo + HEAD_PAD] = (dk * nope).astype(BF16)
            dkr = dkr + dk
        dkv_ref[:, HW:] = dv_ref[...]
        dkr_ref[...] = (_rope(dkr * ropem, c, sa, sb, -1.0) * ropem).astype(BF16)

    dq_raw, dkv_cat, dkr = _rows("rope_bwd", k_rope_bwd, [dq_att, dk_att, dv_att, rope_c, rope_sa, rope_sb], [],
                                 [(HW, BF16), (2 * HW, BF16), (HEAD_PAD, BF16)], tile=256)
    g_uq = _mm_plain("q_up_dw", qn, dq_raw, "tn")
    dqn = _mm_plain("q_up_dx", dq_raw, wl["w_uq"], "nt")
    g_ukv = _mm_plain("kv_up_dw", kvn, dkv_cat, "tn")
    dkvn = _mm_plain("kv_up_dx", dkv_cat, wl["w_ukv"], "nt")

    def k_lora_bwd(cq_ref, ckv_ref, dqn_ref, dkvn_ref, gq_ref, gkv_ref, dcq_ref, dckv_ref, dgq_ref, dgkv_ref):
        dcq, dgq = _rms_bwd(cq_ref[...], gq_ref[...], dqn_ref[...])
        dckv, dgkv = _rms_bwd(ckv_ref[...], gkv_ref[...], dkvn_ref[...])
        dcq_ref[...] = dcq.astype(BF16)
        dckv_ref[...] = dckv.astype(BF16)
        _accum(dgq_ref, dgq)
        _accum(dgkv_ref, dgkv)

    dc_q, dc_kv, g_q_norm, g_kv_norm = _rows("lora_norm_bwd", k_lora_bwd, [c_q, c_kv, dqn, dkvn],
                                              [sm["q_norm_g"], sm["kv_norm_g"]], [(Q_LORA, BF16), (KV_LORA, BF16)],
                                              [(1, Q_LORA), (1, KV_LORA)])

    dz3 = _mm_plain("conv_out_dx", dconv_out, wl["w_conv_out"], "nt")
    g_conv_out = _mm_plain("conv_out_dw", z3, dconv_out, "tn")
    dz1, g_ln_g, g_ln_b, g_conv_b = _conv_bwd_norm(dz3, z1, sm["conv_ln_g"], sm["conv_ln_b"])
    dconv_in, g_conv_w = _conv_bwd_taps(dz1, z0, conv_in, conv_w)

    dproj = jnp.concatenate([dconv_in, dc_q, dc_kv, dkr, dgl], axis=1)
    g_in = _mm_plain("proj_dw", u1, dproj, "tn")
    du1 = _mm_plain("proj_dx", dproj, w_in, "nt")

    def k_norm1_bwd(x_ref, du_ref, dres_ref, g_ref, dx_ref, dg_ref):
        dx, dg = _rms_bwd(x_ref[...], g_ref[...], du_ref[...])
        dx_ref[...] = dres_ref[...] + dx
        _accum(dg_ref, dg)

    grad_x, g_norm_mix = _rows("norm_mix_bwd", k_norm1_bwd, [x, du1, dh1], [sm["norm_mix_g"]], [(D_MODEL, F32)],
                               [(1, D_MODEL)], tile=256)

    grads = dict(norm_mix_g=g_norm_mix, w_in=g_in, conv_w=g_conv_w[:CONV_WIDTH], conv_b=g_conv_b, conv_ln_g=g_ln_g,
                 conv_ln_b=g_ln_b, w_conv_out=g_conv_out, q_norm_g=g_q_norm, w_uq=g_uq, kv_norm_g=g_kv_norm, w_ukv=g_ukv,
                 w_mla_out=g_mla_out, w_out=g_out, norm_xattn_g=g_norm_xattn, norm_mem_g=g_norm_mem, w_xq=g_xq,
                 w_xkv=g_xkv, w_xo=g_xo, norm_mlp_g=g_norm_mlp, w_mlp1=g_mlp1, w_mlp2=g_mlp2, final_norm_g=g_final)
    return loss, grad_x, grads


def _shard(a, k, axis):
    n = a.shape[axis] // N_CHIPS
    return lax.slice_in_dim(a, k * n, (k + 1) * n, axis=axis)


def _pack_rows(n_elems):
    half = -(-n_elems // (2 * LANES))
    return 2 * (-(-half // PACK_TILES[0]) * PACK_TILES[0])


def _pack_small(grads, loss):
    flat = jnp.concatenate([grads[n].reshape(-1) for n in SMALL] + [loss.reshape(-1)[:1]])
    rows = -(-flat.shape[0] // (8 * LANES)) * 8
    return jnp.pad(flat, (0, rows * LANES - flat.shape[0])).reshape(rows, LANES)


def _pack_grads(grads):
    packs = []
    for k in range(N_CHIPS):
        flat = jnp.concatenate([_shard(grads[n], k, SHARD_AXIS[n]).reshape(-1).astype(BF16) for n in BIG])
        rows = _pack_rows(flat.shape[0])
        packs.append(jnp.pad(flat, (0, rows * LANES - flat.shape[0])).reshape(2, rows // 2, LANES))
    return jnp.stack(packs, axis=1)


def _unpack(flat, names, shapes):
    out, at = {}, 0
    for n in names:
        size = math.prod(shapes[n])
        out[n] = flat[at:at + size].reshape(shapes[n])
        at += size
    return out, at


def kernel(x, mem, positions, norm_mix_g, w_in, conv_w, conv_b, conv_ln_g, conv_ln_b, w_conv_out, q_norm_g, w_uq, kv_norm_g, w_ukv, w_mla_out, w_out, norm_xattn_g, norm_mem_g, w_xq, w_xkv, w_xo, norm_mlp_g, w_mlp1, w_mlp2, final_norm_g, loss_target, m_norm_mix_g, m_w_in, m_conv_w, m_conv_b, m_conv_ln_g, m_conv_ln_b, m_w_conv_out, m_q_norm_g, m_w_uq, m_kv_norm_g, m_w_ukv, m_w_mla_out, m_w_out, m_norm_xattn_g, m_norm_mem_g, m_w_xq, m_w_xkv, m_w_xo, m_norm_mlp_g, m_w_mlp1, m_w_mlp2, m_final_norm_g, v_norm_mix_g, v_w_in, v_conv_w, v_conv_b, v_conv_ln_g, v_conv_ln_b, v_w_conv_out, v_q_norm_g, v_w_uq, v_kv_norm_g, v_w_ukv, v_w_mla_out, v_w_out, v_norm_xattn_g, v_norm_mem_g, v_w_xq, v_w_xkv, v_w_xo, v_norm_mlp_g, v_w_mlp1, v_w_mlp2, v_final_norm_g):
    args = dict(locals())
    w = {n: args[n] for n in WEIGHTS}
    m = {n: args["m_" + n] for n in WEIGHTS}
    v = {n: args["v_" + n] for n in WEIGHTS}

    shards = [w[n][0].astype(F32 if n == "conv_w" else BF16) for n in BIG]
    first, rest = list(range(FIRST_NEEDED)), list(range(FIRST_NEEDED, len(BIG)))
    sem_pairs, shards_thru, lands_thru, token = _gather_start(shards, [first, rest])

    def unshard(n, g):
        ax = SHARD_AXIS[n]
        return jnp.moveaxis(g, 0, ax).reshape(g.shape[1:1 + ax] + (N_CHIPS * g.shape[1 + ax],) + g.shape[2 + ax:])

    def first_fn(after):
        lands = _gather_wait("gather_weights_wait_first", sem_pairs[0], shards_thru[:FIRST_NEEDED], lands_thru[:FIRST_NEEDED], after)
        return _layout_w_in(unshard("w_in", lands[0])), unshard("conv_w", lands[1])

    def rest_fn(after):
        lands = _gather_wait("gather_weights_wait_rest", sem_pairs[1], shards_thru[FIRST_NEEDED:], lands_thru[FIRST_NEEDED:], after)
        return _layout_weights({n: unshard(n, g) for n, g in zip(BIG[FIRST_NEEDED:], lands)})

    sm = {n: w[n].reshape(1, -1) for n in SMALL}
    sm["norm_mix_g"] = sm["norm_mix_g"] + token[0, 0]

    loss, grad_x, grads = _local_step(x[0], mem[0], positions, loss_target[0], first_fn, rest_fn, sm)
    grads = _unlayout_grads(grads)

    small_flat = _sum_over_devices(_pack_small(grads, loss)).reshape(-1)
    pack = _pack_grads(grads)
    pair = _pair_add(pack, _pair_exchange(pack))
    half = _chip_add(_chip_exchange(pair))
    flat = _pair_share(half).reshape(-1)
    shapes = {n: w[n].shape[1:] if n in BIG else w[n].shape for n in WEIGHTS}
    g_sum, _ = _unpack(flat, BIG, shapes)
    g_small, at = _unpack(small_flat, SMALL, shapes)
    g_sum.update(g_small)
    loss_sum = small_flat[at]

    out_g, out_d, out_m, out_v = [], [], [], []
    for n in WEIGHTS:
        g = g_sum[n].reshape(w[n].shape)
        d, nm, nv = _adamw("adamw_" + n, w[n], g, m[n], v[n])
        out_g.append(g)
        out_d.append(d)
        out_m.append(nm)
        out_v.append(nv)
    return (loss_sum, grad_x[None], *out_g, *out_d, *out_m, *out_v)
```

```python
import functools
import math

import jax
import jax.numpy as jnp
from jax import lax
from jax.experimental import pallas as pl
from jax.experimental.pallas import tpu as pltpu

F32 = jnp.float32
BF16 = jnp.bfloat16
MESH = pl.DeviceIdType.MESH

D_MODEL = 1024
CONV_CH = 512
CONV_WIDTH = 31
MLA_HEADS = 8
QK_NOPE = 64
QK_ROPE = 32
V_DIM = 64
Q_LORA = 384
KV_LORA = 256
MEM_LEN = 256
X_HEADS = 4
X_HEAD_DIM = 128
D_FF = 4096
ROPE_THETA = 10000.0
EPS = 1e-6
HEAD_PAD = 128
STAT_COL_QK = QK_NOPE + QK_ROPE
STAT_COL_V = V_DIM
HALO = 32
N_CHIPS = 4
LANES = 128

ADAM_LR = 0.001
ADAM_B1 = 0.9
ADAM_B2 = 0.999
ADAM_EPS = 1e-08
ADAM_WD = 0.01
ADAM_STEP = 10

VMEM_LIMIT = 52 * 1024 * 1024
ROW_TILES = (1024, 512, 256, 128, 64, 32, 16)
PACK_ROW_ALIGN = 32
N_DEV = 8
NEG = -1e30

BIG = ["w_in", "conv_w", "w_conv_out", "w_uq", "w_ukv", "w_mla_out", "w_out", "w_xq", "w_xkv", "w_xo", "w_mlp1", "w_mlp2"]
FIRST_NEEDED = 2
SHARD_AXIS = {"w_in": 1, "w_conv_out": 1, "w_uq": 1, "w_ukv": 1, "w_mla_out": 1, "w_out": 0, "w_xq": 0, "w_xkv": 0,
              "w_xo": 1, "w_mlp1": 1, "w_mlp2": 0, "conv_w": 1}
SMALL = ["norm_mix_g", "conv_b", "conv_ln_g", "conv_ln_b", "q_norm_g", "kv_norm_g", "norm_xattn_g", "norm_mem_g",
         "norm_mlp_g", "final_norm_g"]
WEIGHTS = ["norm_mix_g", "w_in", "conv_w", "conv_b", "conv_ln_g", "conv_ln_b", "w_conv_out", "q_norm_g", "w_uq",
           "kv_norm_g", "w_ukv", "w_mla_out", "w_out", "norm_xattn_g", "norm_mem_g", "w_xq", "w_xkv", "w_xo",
           "norm_mlp_g", "w_mlp1", "w_mlp2", "final_norm_g"]


def _pick(n, prefs):
    for p in prefs:
        if n % p == 0:
            return p
    return n


def _params(sem):
    return pltpu.CompilerParams(dimension_semantics=sem, vmem_limit_bytes=VMEM_LIMIT)


_DIMS = {"nn": (((1,), (0,)), ((), ())), "nt": (((1,), (1,)), ((), ())), "tn": (((0,), (0,)), ((), ()))}


def _mm(name, a, b, mode, outs, epi, row_x=(), tile_x=(), vec_x=(), tm=None, tn=None, tk=None):
    if mode == "nn":
        (M, K), (_, N) = a.shape, b.shape
    elif mode == "nt":
        (M, K), (N, _) = a.shape, b.shape
    else:
        (K, M), (_, N) = a.shape, b.shape
    tm = tm or _pick(M, (1024, 512, 384, 256, 128))
    tn = tn or _pick(N, (1024, 768, 512, 384, 256, 128))
    tk = tk or _pick(K, (1024, 768, 512, 384, 256, 128))
    nk = K // tk
    rows_inner = nk == 1 and N // tn > 1
    grid = (N // tn, M // tm, nk) if rows_inner else (M // tm, N // tn, nk)

    def spec(shape, f):
        return pl.BlockSpec(shape, (lambda j, i, k: f(i, j, k)) if rows_inner else f)

    a_spec = spec((tk, tm), lambda i, j, k: (k, i)) if mode == "tn" else spec((tm, tk), lambda i, j, k: (i, k))
    b_spec = spec((tn, tk), lambda i, j, k: (j, k)) if mode == "nt" else spec((tk, tn), lambda i, j, k: (k, j))
    in_specs = [a_spec, b_spec]
    in_specs += [spec((tm, r.shape[1]), lambda i, j, k: (i, 0)) for r in row_x]
    in_specs += [spec((tm, tn), lambda i, j, k: (i, j)) for _ in tile_x]
    in_specs += [spec(v.shape, lambda i, j, k: (0, 0)) for v in vec_x]
    out_specs, out_shape = [], []
    for w, dt in outs:
        if tn == N:
            out_specs.append(spec((tm, w), lambda i, j, k: (i, 0)))
        else:
            assert w == N, (name, w, N)
            out_specs.append(spec((tm, tn), lambda i, j, k: (i, j)))
        out_shape.append(jax.ShapeDtypeStruct((M, w), dt))
    nx = len(row_x) + len(tile_x) + len(vec_x)
    dims = _DIMS[mode]

    def body(a_ref, b_ref, *rest):
        x_refs, out_refs, acc_ref = rest[:nx], rest[nx:nx + len(outs)], rest[-1]
        av, bv = a_ref[...], b_ref[...]
        if av.dtype != BF16:
            av = av.astype(BF16)
        if bv.dtype != BF16:
            bv = bv.astype(BF16)
        prod = lax.dot_general(av, bv, dims, preferred_element_type=F32)
        if nk == 1:
            acc_ref[...] = prod
            epi(acc_ref, x_refs, out_refs)
        else:
            k = pl.program_id(2)

            @pl.when(k == 0)
            def _():
                acc_ref[...] = prod

            @pl.when(k > 0)
            def _():
                acc_ref[...] += prod

            @pl.when(k == nk - 1)
            def _():
                epi(acc_ref, x_refs, out_refs)

    res = pl.pallas_call(
        body, name=name, grid=grid, in_specs=in_specs, out_specs=out_specs, out_shape=out_shape,
        scratch_shapes=[pltpu.VMEM((tm, tn), F32)],
        compiler_params=_params(("parallel", "parallel", "arbitrary")),
    )(a, b, *row_x, *tile_x, *vec_x)
    return res


def _epi_store(acc_ref, x_refs, out_refs):
    for o in out_refs:
        o[...] = acc_ref[...].astype(o.dtype)


def _mm_plain(name, a, b, mode, dtype=F32, **kw):
    n = b.shape[0] if mode == "nt" else b.shape[1]
    return _mm(name, a, b, mode, [(n, dtype)], _epi_store, **kw)[0]


def _rows(name, body, row_ins, vec_ins, row_outs, acc_outs=(), tile=512):
    S = row_ins[0].shape[0]
    t = _pick(S, (tile, 256, 128, 64, 32, 16, 8))
    in_specs = [pl.BlockSpec((t, r.shape[1]), lambda i: (i, 0)) for r in row_ins]
    in_specs += [pl.BlockSpec(v.shape, lambda i: (0, 0)) for v in vec_ins]
    out_specs = [pl.BlockSpec((t, w), lambda i: (i, 0)) for w, _ in row_outs]
    out_specs += [pl.BlockSpec(shp, lambda i: (0, 0)) for shp in acc_outs]
    out_shape = [jax.ShapeDtypeStruct((S, w), dt) for w, dt in row_outs]
    out_shape += [jax.ShapeDtypeStruct(shp, F32) for shp in acc_outs]
    sem = ("arbitrary",) if acc_outs else ("parallel",)
    return pl.pallas_call(
        functools.partial(body), name=name, grid=(S // t,), in_specs=in_specs, out_specs=out_specs,
        out_shape=out_shape, compiler_params=_params(sem),
    )(*row_ins, *vec_ins)


def _accum(ref, val):
    @pl.when(pl.program_id(0) == 0)
    def _():
        ref[...] = jnp.zeros_like(ref)

    ref[...] += val


def _colsum(v):
    return jnp.sum(v, axis=0, keepdims=True)


def _rms_fwd(x, g):
    r = lax.rsqrt(jnp.mean(x * x, axis=-1, keepdims=True) + EPS)
    return x * r * g


def _rms_bwd(x, g, du):
    r = lax.rsqrt(jnp.mean(x * x, axis=-1, keepdims=True) + EPS)
    xn = x * r
    gdu = du * g
    dx = r * (gdu - xn * jnp.mean(xn * gdu, axis=-1, keepdims=True))
    return dx, _colsum(du * xn)


def _sigmoid(v):
    return 1.0 / (1.0 + jnp.exp(-v))


def _rope(v, c, sa, sb, sign):
    return v * c + sign * (pltpu.roll(v, HEAD_PAD - QK_ROPE // 2, 1) * sa + pltpu.roll(v, QK_ROPE // 2, 1) * sb)


def _split3(v):
    hi = v.astype(BF16)
    r1 = v - hi.astype(F32)
    mid = r1.astype(BF16)
    lo = (r1 - mid.astype(F32)).astype(BF16)
    return hi, mid, lo


def _put_stats(base, stat, col):
    hi, mid, lo = _split3(stat)
    lane = lax.broadcasted_iota(jnp.int32, base.shape, 1)
    out = jnp.where(lane == col, hi, base)
    out = jnp.where(lane == col + 1, mid, out)
    return jnp.where(lane == col + 2, lo, out)


def _neg_ones(shape, col):
    lane = lax.broadcasted_iota(jnp.int32, shape, 1)
    return jnp.where((lane >= col) & (lane < col + 3), -1.0, 0.0).astype(F32)


def _shifted(ext, t):
    p = ext.shape[0]
    for b in range(8):
        rb = ext if b == 0 else pltpu.roll(ext, p - b, 0)
        for a in range(HALO // 8 + 1):
            if 8 * a + b <= HALO:
                yield 8 * a + b, rb[8 * a:8 * a + t]


def _conv_fwd(z0, conv_w, conv_b, ln_g, ln_b):
    S, C = z0.shape
    t = _pick(S, (512, 256, 128, 64, 32))
    per = t // HALO

    def body(cur_ref, prev_ref, w_ref, b_ref, g_ref, beta_ref, z1_ref, z3_ref, ext_ref):
        i = pl.program_id(0)
        ext_ref[0:HALO, :] = jnp.where(i > 0, prev_ref[...], 0.0)
        ext_ref[HALO:, :] = cur_ref[...]
        ext = ext_ref[...]
        acc = jnp.zeros((t, C), F32)
        for d, win in _shifted(ext, t):
            k = d - (HALO - CONV_WIDTH + 1)
            if 0 <= k < CONV_WIDTH:
                acc = acc + win * w_ref[k:k + 1, :]
        z1 = acc + b_ref[...]
        z1_ref[...] = z1
        mu = jnp.mean(z1, axis=-1, keepdims=True)
        zc = z1 - mu
        rs = lax.rsqrt(jnp.mean(zc * zc, axis=-1, keepdims=True) + EPS)
        z2 = zc * rs * g_ref[...] + beta_ref[...]
        z3_ref[...] = (z2 * _sigmoid(z2)).astype(BF16)

    vec = lambda v: pl.BlockSpec(v.shape, lambda i: (0, 0))
    return pl.pallas_call(
        body, name="conv_fwd", grid=(S // t,),
        in_specs=[pl.BlockSpec((t, C), lambda i: (i, 0)),
                  pl.BlockSpec((HALO, C), lambda i: (jnp.maximum(i * per - 1, 0), 0)),
                  vec(conv_w), vec(conv_b), vec(ln_g), vec(ln_b)],
        out_specs=[pl.BlockSpec((t, C), lambda i: (i, 0)), pl.BlockSpec((t, C), lambda i: (i, 0))],
        out_shape=[jax.ShapeDtypeStruct((S, C), F32), jax.ShapeDtypeStruct((S, C), BF16)],
        scratch_shapes=[pltpu.VMEM((t + HALO, C), F32)],
        compiler_params=_params(("parallel",)),
    )(z0, z0, conv_w, conv_b, ln_g, ln_b)


def _conv_bwd_norm(dz3, z1, ln_g, ln_b):
    C = z1.shape[1]

    def body(dz3_ref, z1_ref, g_ref, beta_ref, dz1_ref, dg_ref, dbeta_ref, dbias_ref):
        z1 = z1_ref[...]
        mu = jnp.mean(z1, axis=-1, keepdims=True)
        zc = z1 - mu
        rs = lax.rsqrt(jnp.mean(zc * zc, axis=-1, keepdims=True) + EPS)
        xh = zc * rs
        z2 = xh * g_ref[...] + beta_ref[...]
        sg = _sigmoid(z2)
        dz2 = dz3_ref[...] * (sg * (1.0 + z2 * (1.0 - sg)))
        dxh = dz2 * g_ref[...]
        dz1 = rs * (dxh - jnp.mean(dxh, axis=-1, keepdims=True) - xh * jnp.mean(dxh * xh, axis=-1, keepdims=True))
        dz1_ref[...] = dz1
        _accum(dg_ref, _colsum(dz2 * xh))
        _accum(dbeta_ref, _colsum(dz2))
        _accum(dbias_ref, _colsum(dz1))

    return _rows("conv_bwd_norm", body, [dz3, z1], [ln_g, ln_b], [(C, F32)], [(1, C)] * 3)


def _conv_bwd_taps(dz1, z0, conv_in, conv_w):
    S, C = z0.shape
    t = _pick(S, (512, 256, 128, 64, 32))
    per = t // HALO
    last = S // HALO - 1
    nt = S // t

    def body(dcur_ref, dnext_ref, zcur_ref, zprev_ref, cin_ref, w_ref, dcin_ref, dw_ref, dext_ref, zext_ref):
        i = pl.program_id(0)
        dcur = dcur_ref[...]
        dext_ref[0:t, :] = dcur
        dext_ref[t:, :] = jnp.where(i < nt - 1, dnext_ref[...], 0.0)
        zext_ref[0:HALO, :] = jnp.where(i > 0, zprev_ref[...], 0.0)
        zext_ref[HALO:, :] = zcur_ref[...]

        @pl.when(i == 0)
        def _():
            dw_ref[...] = jnp.zeros_like(dw_ref)

        dz0 = jnp.zeros((t, C), F32)
        for d, win in _shifted(dext_ref[...], t):
            k = CONV_WIDTH - 1 - d
            if 0 <= k < CONV_WIDTH:
                dz0 = dz0 + win * w_ref[k:k + 1, :]
        for d, win in _shifted(zext_ref[...], t):
            k = d - (HALO - CONV_WIDTH + 1)
            if 0 <= k < CONV_WIDTH:
                dw_ref[k:k + 1, :] += _colsum(dcur * win)
        a = cin_ref[:, 0:C]
        sg = _sigmoid(cin_ref[:, C:2 * C])
        dcin_ref[:, 0:C] = (dz0 * sg).astype(BF16)
        dcin_ref[:, C:2 * C] = (dz0 * a * sg * (1.0 - sg)).astype(BF16)

    return pl.pallas_call(
        body, name="conv_bwd_taps", grid=(nt,),
        in_specs=[pl.BlockSpec((t, C), lambda i: (i, 0)),
                  pl.BlockSpec((HALO, C), lambda i: (jnp.minimum((i + 1) * per, last), 0)),
                  pl.BlockSpec((t, C), lambda i: (i, 0)),
                  pl.BlockSpec((HALO, C), lambda i: (jnp.maximum(i * per - 1, 0), 0)),
                  pl.BlockSpec((t, 2 * C), lambda i: (i, 0)),
                  pl.BlockSpec(conv_w.shape, lambda i: (0, 0))],
        out_specs=[pl.BlockSpec((t, 2 * C), lambda i: (i, 0)), pl.BlockSpec((HALO, C), lambda i: (0, 0))],
        out_shape=[jax.ShapeDtypeStruct((S, 2 * C), BF16), jax.ShapeDtypeStruct((HALO, C), F32)],
        scratch_shapes=[pltpu.VMEM((t + HALO, C), F32), pltpu.VMEM((t + HALO, C), F32)],
        compiler_params=_params(("arbitrary",)),
    )(dz1, dz1, z0, z0, conv_in, conv_w)


def _lower_tri(shape, rows_are_queries):
    row = lax.broadcasted_iota(jnp.int32, shape, 0)
    col = lax.broadcasted_iota(jnp.int32, shape, 1)
    return (col <= row) if rows_are_queries else (row <= col)


HEADS_PER_STEP = 2
FWD_KEY_TILES = 4


def _flash_specs(S, t):
    w = HEADS_PER_STEP * HEAD_PAD
    blk = pl.BlockSpec((t, w), lambda h, i: (i, h))
    head = pl.BlockSpec((S, w), lambda h, i: (0, h))
    return blk, head


def _head_lanes(g):
    return slice(g * HEAD_PAD, (g + 1) * HEAD_PAD)


def _dot_nt(a, b):
    return lax.dot_general(a, b, _DIMS["nt"], preferred_element_type=F32)


def _dot_nn(a, b):
    return lax.dot_general(a, b, _DIMS["nn"], preferred_element_type=F32)


def _dot_tn(a, b):
    return lax.dot_general(a, b, _DIMS["tn"], preferred_element_type=F32)


def _flash_fwd(q, k, v):
    S = q.shape[0]
    t = _pick(S, (512, 256, 128))

    def body(q_ref, k_ref, v_ref, o_ref, qa_ref, m_ref, acc_ref):
        qi = pl.program_id(1)
        m_ref[...] = jnp.full_like(m_ref, NEG)
        acc_ref[...] = jnp.zeros_like(acc_ref)

        def step(first, width, diag):
            rows = pl.ds(pl.multiple_of(first, width), width)
            for g in range(HEADS_PER_STEP):
                hl = _head_lanes(g)
                s = _dot_nt(q_ref[:, hl], k_ref[rows, hl])
                if diag:
                    s = jnp.where(_lower_tri(s.shape, True), s, NEG)
                m_old = m_ref[g]
                m_new = jnp.maximum(m_old, jnp.max(s, axis=-1, keepdims=True))
                p = jnp.exp(s - m_new).astype(BF16)
                acc_ref[g] = jnp.exp(m_old - m_new) * acc_ref[g] + _dot_nn(p, v_ref[rows, hl])
                m_ref[g] = m_new

        def wide(kb, carry):
            step(kb * (FWD_KEY_TILES * t), FWD_KEY_TILES * t, False)
            return carry

        def single(ki, carry):
            step(ki * t, t, False)
            return carry

        lax.fori_loop(0, qi // FWD_KEY_TILES, wide, 0)
        lax.fori_loop((qi // FWD_KEY_TILES) * FWD_KEY_TILES, qi, single, 0)
        step(qi * t, t, True)
        for g in range(HEADS_PER_STEP):
            hl = _head_lanes(g)
            acc = acc_ref[g]
            l = -acc[:, STAT_COL_V:STAT_COL_V + 1]
            o_ref[:, hl] = (acc / l).astype(BF16)
            qa_ref[:, hl] = _put_stats(q_ref[:, hl], m_ref[g] + jnp.log(l), STAT_COL_QK)

    blk, head = _flash_specs(S, t)
    return pl.pallas_call(
        body, name="mla_flash_fwd", grid=(MLA_HEADS // HEADS_PER_STEP, S // t),
        in_specs=[blk, head, head], out_specs=[blk, blk],
        out_shape=[jax.ShapeDtypeStruct(q.shape, BF16), jax.ShapeDtypeStruct(q.shape, BF16)],
        scratch_shapes=[pltpu.VMEM((HEADS_PER_STEP, t, 1), F32), pltpu.VMEM((HEADS_PER_STEP, t, HEAD_PAD), F32)],
        compiler_params=_params(("parallel", "arbitrary")),
    )(q, k, v)


def _flash_bwd(qa, k, v, doa):
    S = qa.shape[0]
    t = _pick(S, (512, 256, 128))
    n = S // t

    def body(qa_ref, k_ref, v_ref, do_ref, dq_ref, dk_ref, dv_ref, dk_acc, dv_acc):
        kj = pl.program_id(1)

        @pl.when(kj == 0)
        def _():
            dq_ref[...] = jnp.zeros_like(dq_ref)

        dk_acc[...] = jnp.zeros_like(dk_acc)
        dv_acc[...] = jnp.zeros_like(dv_acc)

        def step(qi, diag):
            rows = pl.ds(pl.multiple_of(qi * t, t), t)
            for g in range(HEADS_PER_STEP):
                hl = _head_lanes(g)
                qa, do, kk = qa_ref[rows, hl], do_ref[rows, hl], k_ref[:, hl]
                st = _dot_nt(kk, qa)
                if diag:
                    st = jnp.where(_lower_tri(st.shape, False), st, NEG)
                pt = jnp.exp(st)
                dst = (pt * _dot_nt(v_ref[:, hl], do)).astype(BF16)
                dv_acc[:, hl] += _dot_nn(pt.astype(BF16), do)
                dk_acc[:, hl] += _dot_nn(dst, qa)
                dq_ref[rows, hl] += _dot_tn(dst, kk)

        def loop(qi, carry):
            step(qi, False)
            return carry

        step(kj, True)
        lax.fori_loop(kj + 1, n, loop, 0)
        dk_ref[...] = dk_acc[...]
        dv_ref[...] = dv_acc[...].astype(BF16)

    blk, head = _flash_specs(S, t)
    w = HEADS_PER_STEP * HEAD_PAD
    return pl.pallas_call(
        body, name="mla_flash_bwd", grid=(MLA_HEADS // HEADS_PER_STEP, n),
        in_specs=[head, blk, blk, head], out_specs=[head, blk, blk],
        out_shape=[jax.ShapeDtypeStruct(qa.shape, F32), jax.ShapeDtypeStruct(qa.shape, F32), jax.ShapeDtypeStruct(qa.shape, BF16)],
        scratch_shapes=[pltpu.VMEM((t, w), F32), pltpu.VMEM((t, w), F32)],
        compiler_params=_params(("parallel", "arbitrary")),
    )(qa, k, v, doa)


def _xattn_fwd(xq, kvx):
    W = X_HEADS * X_HEAD_DIM

    def body(q_ref, kv_ref, o_ref):
        for h in range(X_HEADS):
            lo = h * X_HEAD_DIM
            s = _dot_nt(q_ref[:, lo:lo + X_HEAD_DIM], kv_ref[:, lo:lo + X_HEAD_DIM])
            p = jnp.exp(s - jnp.max(s, axis=-1, keepdims=True))
            p = p / jnp.sum(p, axis=-1, keepdims=True)
            o_ref[:, lo:lo + X_HEAD_DIM] = _dot_nn(p.astype(BF16), kv_ref[:, W + lo:W + lo + X_HEAD_DIM]).astype(BF16)

    return _rows("xattn_fwd", body, [xq], [kvx], [(W, BF16)])[0]


def _xattn_bwd(xq, kvx, dox):
    W = X_HEADS * X_HEAD_DIM
    scale = X_HEAD_DIM ** -0.5

    def body(q_ref, do_ref, kv_ref, dq_ref, dkv_ref):
        @pl.when(pl.program_id(0) == 0)
        def _():
            dkv_ref[...] = jnp.zeros_like(dkv_ref)

        for h in range(X_HEADS):
            lo = h * X_HEAD_DIM
            q, k = q_ref[:, lo:lo + X_HEAD_DIM], kv_ref[:, lo:lo + X_HEAD_DIM]
            v, do = kv_ref[:, W + lo:W + lo + X_HEAD_DIM], do_ref[:, lo:lo + X_HEAD_DIM]
            s = _dot_nt(q, k)
            p = jnp.exp(s - jnp.max(s, axis=-1, keepdims=True))
            p = p / jnp.sum(p, axis=-1, keepdims=True)
            dp = _dot_nt(do, v)
            ds = (p * (dp - jnp.sum(dp * p, axis=-1, keepdims=True))).astype(BF16)
            dq_ref[:, lo:lo + X_HEAD_DIM] = (_dot_nn(ds, k) * scale).astype(BF16)
            dkv_ref[:, lo:lo + X_HEAD_DIM] += _dot_tn(ds, q)
            dkv_ref[:, W + lo:W + lo + X_HEAD_DIM] += _dot_tn(p.astype(BF16), do)

    return _rows("xattn_bwd", body, [xq, dox], [kvx], [(W, BF16)], [kvx.shape])


def _adamw(name, w, g, m, v):
    shape = w.shape
    w2, g2, m2, v2 = [a.reshape(-1, shape[-1]) for a in (w, g, m, v)]
    R, C = w2.shape
    t = _pick(R, (256, 128, 64, 32, 16, 8))
    c1 = 1.0 / (1.0 - ADAM_B1 ** ADAM_STEP)
    c2 = 1.0 / (1.0 - ADAM_B2 ** ADAM_STEP)

    def body(w_ref, g_ref, m_ref, v_ref, d_ref, nm_ref, nv_ref):
        g = g_ref[...]
        nm = ADAM_B1 * m_ref[...] + (1.0 - ADAM_B1) * g
        nv = ADAM_B2 * v_ref[...] + (1.0 - ADAM_B2) * (g * g)
        d_ref[...] = -ADAM_LR * ((nm * c1) / (jnp.sqrt(nv * c2) + ADAM_EPS) + ADAM_WD * w_ref[...])
        nm_ref[...] = nm
        nv_ref[...] = nv

    spec = pl.BlockSpec((t, C), lambda i: (i, 0))
    outs = pl.pallas_call(
        body, name=name, grid=(R // t,), in_specs=[spec] * 4, out_specs=[spec] * 3,
        out_shape=[jax.ShapeDtypeStruct((R, C), F32)] * 3, compiler_params=_params(("parallel",)),
    )(w2, g2, m2, v2)
    return [o.reshape(shape) for o in outs]


def _place():
    x, y, c = lax.axis_index("x"), lax.axis_index("y"), lax.axis_index("c")
    return x, y, c, [(1 - x, y), (x, 1 - y), (1 - x, 1 - y)]


_ANY = pl.BlockSpec(memory_space=pl.ANY)


_HBM = pl.BlockSpec(memory_space=pltpu.HBM)
_SEM = pl.BlockSpec(memory_space=pltpu.SEMAPHORE)
_SIDE_EFFECT = pltpu.SideEffectType.DATAFLOW_SIDE_EFFECTING


def _gather_copy(src, land, slot, send, recv, k, chip, c):
    return pltpu.make_async_remote_copy(src_ref=src, dst_ref=land.at[slot], send_sem=send.at[k], recv_sem=recv.at[k],
                                        device_id=(chip[0], chip[1], c), device_id_type=MESH)


def _gather_start(shards, groups):
    n, ng = len(shards), len(groups)
    lands = [jnp.broadcast_to(s[None], (N_CHIPS,) + s.shape) for s in shards]

    def body(*refs):
        ins, lnd = refs[:n], refs[n:2 * n]
        sends, recvs = refs[2 * n:2 * n + ng], refs[2 * n + ng:2 * n + 2 * ng]
        token = refs[-1]
        x, y, c, chips = _place()
        for gi, group in enumerate(groups):
            for pos, w in enumerate(group):
                for j, chip in enumerate(chips):
                    _gather_copy(ins[w], lnd[w], 2 * x + y, sends[gi], recvs[gi], 3 * pos + j, chip, c).start()
        token[...] = jnp.zeros_like(token)

    sems = [pltpu.SemaphoreType.DMA((3 * len(g),)) for g in groups]
    res = pl.pallas_call(
        body, name="gather_weights_start",
        out_shape=sems + sems + [pltpu.HBM(a.shape, a.dtype) for a in shards + lands] + [jax.ShapeDtypeStruct((8, LANES), F32)],
        in_specs=[_HBM] * (2 * n),
        out_specs=[_SEM] * (2 * ng) + [_HBM] * (2 * n) + [pl.BlockSpec(memory_space=pltpu.VMEM)],
        input_output_aliases={i: 2 * ng + i for i in range(2 * n)},
        compiler_params=pltpu.CompilerParams(has_side_effects=_SIDE_EFFECT),
    )(*[pltpu.with_memory_space_constraint(a, pltpu.HBM) for a in shards + lands])
    sem_pairs = list(zip(res[:ng], res[ng:2 * ng]))
    return sem_pairs, res[2 * ng:2 * ng + n], res[2 * ng + n:2 * ng + 2 * n], res[-1]


def _gather_wait(name, sem_pair, shards_thru, lands_thru, after):
    m = len(shards_thru)

    def body(*refs):
        ins, lnd = refs[:m], refs[m:2 * m]
        send, recv = refs[2 * m], refs[2 * m + 1]
        x, y, c, chips = _place()
        for pos in range(m):
            for j, chip in enumerate(chips):
                cp = _gather_copy(ins[pos], lnd[pos], 2 * chip[0] + chip[1], send, recv, 3 * pos + j, chip, c)
                cp.wait_send()
                cp.wait_recv()

    res = pl.pallas_call(
        body, name=name,
        out_shape=[pltpu.HBM(a.shape, a.dtype) for a in list(shards_thru) + list(lands_thru)],
        in_specs=[_HBM] * (2 * m) + [_SEM, _SEM, _ANY], out_specs=[_HBM] * (2 * m),
        input_output_aliases={i: i for i in range(2 * m)},
        compiler_params=pltpu.CompilerParams(has_side_effects=_SIDE_EFFECT),
    )(*shards_thru, *lands_thru, *sem_pair, after)
    return res[m:]


def _pair_exchange(packs):
    n = len(packs)

    def body(*refs):
        ins, outs, send, recv = refs[:n], refs[n:2 * n], refs[2 * n], refs[2 * n + 1]
        x, y, c, _ = _place()
        cps = []
        for g in range(n):
            cp = pltpu.make_async_remote_copy(src_ref=ins[g].at[:, pl.ds(1 - c, 1)], dst_ref=outs[g], send_sem=send.at[g],
                                              recv_sem=recv.at[g], device_id=(x, y, 1 - c), device_id_type=MESH)
            cp.start()
            cps.append(cp)
        for cp in cps:
            cp.wait()

    return pl.pallas_call(
        body, name="grad_pair_exchange", in_specs=[_ANY] * n, out_specs=[_ANY] * n,
        out_shape=[jax.ShapeDtypeStruct((N_CHIPS, 1) + p.shape[2:], p.dtype) for p in packs],
        scratch_shapes=[pltpu.SemaphoreType.DMA((n,)), pltpu.SemaphoreType.DMA((n,))],
    )(*packs)


def _chip_exchange(parts):
    n = len(parts)

    def body(*refs):
        ins, outs, send, recv = refs[:n], refs[n:2 * n], refs[2 * n], refs[2 * n + 1]
        x, y, c, chips = _place()
        cps = []
        for g in range(n):
            for j, (px, py) in enumerate(chips):
                cp = pltpu.make_async_remote_copy(src_ref=ins[g].at[2 * px + py], dst_ref=outs[g].at[2 * x + y],
                                                  send_sem=send.at[3 * g + j], recv_sem=recv.at[3 * g + j],
                                                  device_id=(px, py, c), device_id_type=MESH)
                cp.start()
                cps.append(cp)
        for g in range(n):
            for j, (px, py) in enumerate(chips):
                pltpu.make_async_remote_copy(src_ref=ins[g].at[2 * x + y], dst_ref=outs[g].at[2 * px + py],
                                             send_sem=send.at[3 * g + j], recv_sem=recv.at[3 * g + j],
                                             device_id=(px, py, c), device_id_type=MESH).wait_recv()
        for cp in cps:
            cp.wait_send()

    return pl.pallas_call(
        body, name="grad_chip_exchange", in_specs=[_ANY] * n, out_specs=[_ANY] * n,
        out_shape=[jax.ShapeDtypeStruct(p.shape, p.dtype) for p in parts],
        scratch_shapes=[pltpu.SemaphoreType.DMA((3 * n,)), pltpu.SemaphoreType.DMA((3 * n,))],
    )(*parts)


def _pair_share(halves):
    n = len(halves)

    def body(*refs):
        outs, send, recv = refs[n:2 * n], refs[2 * n], refs[2 * n + 1]
        x, y, c, _ = _place()
        cps = []
        for g in range(n):
            cp = pltpu.make_async_remote_copy(src_ref=outs[g].at[c], dst_ref=outs[g].at[c], send_sem=send.at[g],
                                              recv_sem=recv.at[g], device_id=(x, y, 1 - c), device_id_type=MESH)
            cp.start()
            cps.append(cp)
        for g in range(n):
            pltpu.make_async_remote_copy(src_ref=outs[g].at[c], dst_ref=outs[g].at[1 - c], send_sem=send.at[g],
                                         recv_sem=recv.at[g], device_id=(x, y, 1 - c), device_id_type=MESH).wait_recv()
        for cp in cps:
            cp.wait_send()

    return pl.pallas_call(
        body, name="grad_pair_share", in_specs=[_ANY] * n, out_specs=[_ANY] * n,
        out_shape=[jax.ShapeDtypeStruct(h.shape, h.dtype) for h in halves],
        input_output_aliases={g: g for g in range(n)},
        scratch_shapes=[pltpu.SemaphoreType.DMA((n,)), pltpu.SemaphoreType.DMA((n,))],
    )(*halves)


def _sum_over_devices(block):
    R, L = block.shape

    def gather(b_ref, o_ref, send, recv, loc):
        x, y, c, _ = _place()
        lc = pltpu.make_async_copy(b_ref, o_ref.at[4 * x + 2 * y + c], loc)
        lc.start()
        peers = [(1 - x if dx else x, 1 - y if dy else y, 1 - c if dc else c)
                 for dx in (0, 1) for dy in (0, 1) for dc in (0, 1) if dx or dy or dc]
        cps = []
        for j, peer in enumerate(peers):
            cp = pltpu.make_async_remote_copy(src_ref=b_ref, dst_ref=o_ref.at[4 * x + 2 * y + c], send_sem=send.at[j],
                                              recv_sem=recv.at[j], device_id=peer, device_id_type=MESH)
            cp.start()
            cps.append(cp)
        for j, (px, py, pc) in enumerate(peers):
            pltpu.make_async_remote_copy(src_ref=b_ref, dst_ref=o_ref.at[4 * px + 2 * py + pc], send_sem=send.at[j],
                                         recv_sem=recv.at[j], device_id=(px, py, pc), device_id_type=MESH).wait_recv()
        for cp in cps:
            cp.wait_send()
        lc.wait()

    blocks = pl.pallas_call(
        gather, name="small_grads_gather", in_specs=[_ANY], out_specs=_ANY,
        out_shape=jax.ShapeDtypeStruct((N_DEV, R, L), F32),
        scratch_shapes=[pltpu.SemaphoreType.DMA((N_DEV - 1,)), pltpu.SemaphoreType.DMA((N_DEV - 1,)), pltpu.SemaphoreType.DMA],
    )(block)

    def add(b_ref, o_ref):
        total = b_ref[0]
        for d in range(1, N_DEV):
            total = total + b_ref[d]
        o_ref[...] = total

    return pl.pallas_call(add, name="small_grads_add", out_shape=jax.ShapeDtypeStruct((R, L), F32))(blocks)


def _pair_add(name, pack, got):
    _, _, R, C = pack.shape
    t = _pick(R, ROW_TILES)
    c = lax.axis_index("c").astype(jnp.int32).reshape(1)

    def body(c_ref, p_ref, g_ref, o_ref):
        o_ref[...] = (p_ref[...].astype(F32) + g_ref[...].astype(F32)).astype(BF16)

    return pl.pallas_call(
        body, name=name,
        grid_spec=pltpu.PrefetchScalarGridSpec(
            num_scalar_prefetch=1, grid=(N_CHIPS, R // t),
            in_specs=[pl.BlockSpec((None, None, t, C), lambda k, i, c_ref: (k, c_ref[0], i, 0)),
                      pl.BlockSpec((None, None, t, C), lambda k, i, c_ref: (k, 0, i, 0))],
            out_specs=pl.BlockSpec((None, t, C), lambda k, i, c_ref: (k, i, 0))),
        out_shape=jax.ShapeDtypeStruct((N_CHIPS, R, C), BF16), compiler_params=_params(("parallel", "parallel")),
    )(c, pack, got)


def _chip_add(name, own, got):
    _, R, C = own.shape
    t = _pick(R, ROW_TILES)
    x, y, c, _ = _place()
    place = jnp.stack([c, 2 * x + y]).astype(jnp.int32)

    def body(place_ref, own_ref, g1_ref, g2_ref, g3_ref, o_ref):
        o_ref[...] = ((own_ref[...].astype(F32) + g1_ref[...].astype(F32)) + g2_ref[...].astype(F32)) + g3_ref[...].astype(F32)

    def other(d):
        return pl.BlockSpec((None, t, C), lambda i, place_ref: ((place_ref[1] + d) % N_CHIPS, i, 0))

    return pl.pallas_call(
        body, name=name,
        grid_spec=pltpu.PrefetchScalarGridSpec(
            num_scalar_prefetch=1, grid=(R // t,),
            in_specs=[pl.BlockSpec((None, t, C), lambda i, place_ref: (place_ref[1], i, 0)), other(1), other(2), other(3)],
            out_specs=pl.BlockSpec((None, t, C), lambda i, place_ref: (place_ref[0], i, 0))),
        out_shape=jax.ShapeDtypeStruct((2, R, C), F32), compiler_params=_params(("parallel",)),
    )(place, own, got, got, got)


_CUT = (2 * CONV_CH, 2 * CONV_CH + Q_LORA, 2 * CONV_CH + Q_LORA + KV_LORA, 2 * CONV_CH + Q_LORA + KV_LORA + QK_ROPE)
_KR_AT = _CUT[2] + QK_NOPE


def _pad_last(a, n):
    return jnp.pad(a, [(0, 0)] * (a.ndim - 1) + [(0, n - a.shape[-1])])


def _layout_w_in(w_in):
    kr = jnp.pad(w_in[:, _CUT[2]:_CUT[3]], ((0, 0), (QK_NOPE, HEAD_PAD - QK_NOPE - QK_ROPE)))
    return jnp.concatenate([w_in[:, :_CUT[2]], kr, w_in[:, _CUT[3]:]], axis=1)


def _layout_weights(w):
    uq = _pad_last(w["w_uq"].reshape(Q_LORA, MLA_HEADS, QK_NOPE + QK_ROPE), HEAD_PAD).reshape(Q_LORA, MLA_HEADS * HEAD_PAD)
    ukv = w["w_ukv"].reshape(KV_LORA, MLA_HEADS, QK_NOPE + V_DIM)
    uk = _pad_last(ukv[:, :, :QK_NOPE], HEAD_PAD).reshape(KV_LORA, MLA_HEADS * HEAD_PAD)
    uv = _pad_last(ukv[:, :, QK_NOPE:], HEAD_PAD).reshape(KV_LORA, MLA_HEADS * HEAD_PAD)
    mo = jnp.pad(w["w_mla_out"].reshape(MLA_HEADS, V_DIM, D_MODEL), ((0, 0), (0, HEAD_PAD - V_DIM), (0, 0)))
    return dict(
        w_uq=uq, w_ukv=jnp.concatenate([uk, uv], axis=1), w_mla_out=mo.reshape(MLA_HEADS * HEAD_PAD, D_MODEL),
        w_conv_out=w["w_conv_out"], w_out=w["w_out"], w_xq=w["w_xq"], w_xkv=w["w_xkv"], w_xo=w["w_xo"],
        w_mlp1=w["w_mlp1"], w_mlp2=w["w_mlp2"])


def _unlayout_grads(g):
    gi = g["w_in"]
    w_in = jnp.concatenate([gi[:, :_CUT[2]], gi[:, _KR_AT:_KR_AT + QK_ROPE], gi[:, _CUT[2] + HEAD_PAD:]], axis=1)
    uq = g["w_uq"].reshape(Q_LORA, MLA_HEADS, HEAD_PAD)[:, :, :QK_NOPE + QK_ROPE].reshape(Q_LORA, -1)
    gk = g["w_ukv"][:, :MLA_HEADS * HEAD_PAD].reshape(KV_LORA, MLA_HEADS, HEAD_PAD)[:, :, :QK_NOPE]
    gv = g["w_ukv"][:, MLA_HEADS * HEAD_PAD:].reshape(KV_LORA, MLA_HEADS, HEAD_PAD)[:, :, :V_DIM]
    ukv = jnp.concatenate([gk, gv], axis=2).reshape(KV_LORA, -1)
    mo = g["w_mla_out"].reshape(MLA_HEADS, HEAD_PAD, D_MODEL)[:, :V_DIM].reshape(MLA_HEADS * V_DIM, D_MODEL)
    out = dict(g)
    out.update(w_in=w_in, w_uq=uq, w_ukv=ukv, w_mla_out=mo)
    return out


def _rope_tables(positions):
    half = QK_ROPE // 2
    inv_freq = ROPE_THETA ** (-jnp.arange(half, dtype=F32) / half)
    ang = positions.astype(F32).reshape(-1, 1) * inv_freq
    cos, sin = jnp.cos(ang), jnp.sin(ang)
    S = cos.shape[0]
    z16, z32, z64 = jnp.zeros((S, half), F32), jnp.zeros((S, QK_ROPE), F32), jnp.zeros((S, QK_NOPE), F32)
    c = jnp.concatenate([jnp.ones((S, QK_NOPE), F32), cos, cos, z32], axis=1)
    sa = jnp.concatenate([z64, -sin, z16, z32], axis=1)
    sb = jnp.concatenate([z64, z16, sin, z32], axis=1)
    return c, sa, sb


def _local_step(x, mem, positions, target, first_fn, rest_fn, sm):
    S = x.shape[0]
    HW = MLA_HEADS * HEAD_PAD
    rope_c, rope_sa, rope_sb = _rope_tables(positions)
    qk_scale = (QK_NOPE + QK_ROPE) ** -0.5
    c0, c1, c2, c3 = _CUT[0], _CUT[1], _CUT[2], _CUT[2] + HEAD_PAD

    def k_rms1(x_ref, g_ref, u_ref):
        u_ref[...] = _rms_fwd(x_ref[...], g_ref[...]).astype(BF16)

    u1, = _rows("rms_mix", k_rms1, [x], [sm["norm_mix_g"]], [(D_MODEL, BF16)])
    w_in, conv_w = first_fn(u1)

    def epi_glu(acc, xs, outs):
        a, gt = acc[:, 0:CONV_CH], acc[:, CONV_CH:2 * CONV_CH]
        outs[0][...] = acc[...]
        outs[1][...] = a * _sigmoid(gt)

    conv_in, z0 = _mm("proj_conv", u1, w_in[:, :c0], "nn", [(2 * CONV_CH, F32), (CONV_CH, F32)], epi_glu)
    c_q = _mm_plain("proj_cq", u1, w_in[:, c0:c1], "nn")
    c_kv = _mm_plain("proj_ckv", u1, w_in[:, c1:c2], "nn")
    kr_raw = _mm_plain("proj_krope", u1, w_in[:, c2:c3], "nn")

    def epi_sigmoid(acc, xs, outs):
        outs[0][...] = _sigmoid(acc[...])

    gates, = _mm("proj_gates", u1, w_in[:, c3:], "nn", [(2 * D_MODEL, F32)], epi_sigmoid)

    z1, z3 = _conv_fwd(z0, conv_w, sm["conv_b"], sm["conv_ln_g"], sm["conv_ln_b"])
    wl = rest_fn(z1)
    conv_out = _mm_plain("conv_out", z3, wl["w_conv_out"], "nn")

    def k_lora_norm(cq_ref, ckv_ref, gq_ref, gkv_ref, qn_ref, kvn_ref):
        qn_ref[...] = _rms_fwd(cq_ref[...], gq_ref[...]).astype(BF16)
        kvn_ref[...] = _rms_fwd(ckv_ref[...], gkv_ref[...]).astype(BF16)

    qn, kvn = _rows("lora_norm", k_lora_norm, [c_q, c_kv], [sm["q_norm_g"], sm["kv_norm_g"]],
                    [(Q_LORA, BF16), (KV_LORA, BF16)])

    def epi_q(acc, xs, outs):
        c, sa, sb = xs[0][...], xs[1][...], xs[2][...]
        for h in range(MLA_HEADS):
            lo = h * HEAD_PAD
            outs[0][:, lo:lo + HEAD_PAD] = (_rope(acc[:, lo:lo + HEAD_PAD], c, sa, sb, 1.0) * qk_scale).astype(BF16)

    q_att, = _mm("q_up", qn, wl["w_uq"], "nn", [(HW, BF16)], epi_q, row_x=[rope_c, rope_sa, rope_sb], tn=HW)

    def epi_kv(acc, xs, outs):
        kr = _rope(xs[0][...], xs[1][...], xs[2][...], xs[3][...], 1.0)
        kr = kr + _neg_ones(kr.shape, STAT_COL_QK)
        vconst = _neg_ones(kr.shape, STAT_COL_V)
        for h in range(MLA_HEADS):
            lo = h * HEAD_PAD
            outs[0][:, lo:lo + HEAD_PAD] = (acc[:, lo:lo + HEAD_PAD] + kr).astype(BF16)
            outs[1][:, lo:lo + HEAD_PAD] = (acc[:, HW + lo:HW + lo + HEAD_PAD] + vconst).astype(BF16)

    k_att, v_att = _mm("kv_up", kvn, wl["w_ukv"], "nn", [(HW, BF16), (HW, BF16)], epi_kv,
                       row_x=[kr_raw, rope_c, rope_sa, rope_sb], tn=2 * HW)

    o_att, q_aug = _flash_fwd(q_att, k_att, v_att)
    mla_out = _mm_plain("mla_out", o_att, wl["w_mla_out"], "nn")

    def k_merge(g_ref, co_ref, mo_ref, out_ref):
        out_ref[...] = (g_ref[:, 0:D_MODEL] * co_ref[...] + g_ref[:, D_MODEL:] * mo_ref[...]).astype(BF16)

    merged, = _rows("merge", k_merge, [gates, conv_out, mla_out], [], [(D_MODEL, BF16)], tile=256)

    def epi_res_norm(acc, xs, outs):
        h = xs[0][...] + acc[...]
        outs[0][...] = h
        outs[1][...] = _rms_fwd(h, xs[1][...]).astype(BF16)

    h1, u2 = _mm("mix_out", merged, wl["w_out"], "nn", [(D_MODEL, F32), (D_MODEL, BF16)], epi_res_norm,
                 row_x=[x], vec_x=[sm["norm_xattn_g"]], tn=D_MODEL)

    xscale = X_HEAD_DIM ** -0.5

    def epi_scale(acc, xs, outs):
        outs[0][...] = (acc[...] * xscale).astype(BF16)

    xq, = _mm("xattn_q", u2, wl["w_xq"], "nn", [(X_HEADS * X_HEAD_DIM, BF16)], epi_scale)

    def k_mem_norm(m_ref, g_ref, o_ref):
        o_ref[...] = _rms_fwd(m_ref[...], g_ref[...]).astype(BF16)

    mem_n, = _rows("mem_norm", k_mem_norm, [mem], [sm["norm_mem_g"]], [(D_MODEL, BF16)])
    kvx = _mm_plain("xattn_kv", mem_n, wl["w_xkv"], "nn", dtype=BF16)
    ox = _xattn_fwd(xq, kvx)
    h2, u3 = _mm("xattn_out", ox, wl["w_xo"], "nn", [(D_MODEL, F32), (D_MODEL, BF16)], epi_res_norm,
                 row_x=[h1], vec_x=[sm["norm_mlp_g"]], tn=D_MODEL)

    def epi_relu2(acc, xs, outs):
        r = jnp.maximum(acc[...], 0.0)
        outs[0][...] = (r * r).astype(BF16)

    hid, = _mm("mlp_up", u3, wl["w_mlp1"], "nn", [(D_FF, BF16)], epi_relu2)

    def epi_res(acc, xs, outs):
        outs[0][...] = xs[0][...] + acc[...]

    h3, = _mm("mlp_down", hid, wl["w_mlp2"], "nn", [(D_MODEL, F32)], epi_res, row_x=[h2], tn=D_MODEL)

    def k_final(h_ref, t_ref, g_ref, dh_ref, dhb_ref, loss_ref, dg_ref):
        h, g = h_ref[...], g_ref[...]
        e = _rms_fwd(h, g) - t_ref[...]
        part = 0.5 * jnp.sum(jnp.mean(e * e, axis=-1, keepdims=True), axis=0, keepdims=True)
        _accum(loss_ref, jnp.broadcast_to(part, loss_ref.shape))
        dh, dg = _rms_bwd(h, g, e * (1.0 / D_MODEL))
        dh_ref[...] = dh
        dhb_ref[...] = dh.astype(BF16)
        _accum(dg_ref, dg)

    dh3, dh3b, loss, g_final = _rows("final_loss", k_final, [h3, target], [sm["final_norm_g"]],
                                     [(D_MODEL, F32), (D_MODEL, BF16)], [(1, LANES), (1, D_MODEL)])

    def epi_drelu2(acc, xs, outs):
        outs[0][...] = (acc[...] * (2.0 * jnp.sqrt(xs[0][...].astype(F32)))).astype(BF16)

    da1, = _mm("mlp_down_dx", dh3b, wl["w_mlp2"], "nt", [(D_FF, BF16)], epi_drelu2, tile_x=[hid])
    g_mlp2 = _mm_plain("mlp_down_dw", hid, dh3b, "tn")
    g_mlp1 = _mm_plain("mlp_up_dw", u3, da1, "tn")
    du3 = _mm_plain("mlp_up_dx", da1, wl["w_mlp1"], "nt")

    def k_norm_bwd(x_ref, du_ref, dres_ref, g_ref, dh_ref, dhb_ref, dg_ref):
        dx, dg = _rms_bwd(x_ref[...], g_ref[...], du_ref[...])
        dh = dres_ref[...] + dx
        dh_ref[...] = dh
        dhb_ref[...] = dh.astype(BF16)
        _accum(dg_ref, dg)

    def norm_bwd(name, xin, du, dres, g):
        return _rows(name, k_norm_bwd, [xin, du, dres], [g], [(D_MODEL, F32), (D_MODEL, BF16)], [(1, D_MODEL)], tile=256)

    dh2, dh2b, g_norm_mlp = norm_bwd("norm_mlp_bwd", h2, du3, dh3, sm["norm_mlp_g"])

    dox = _mm_plain("xattn_out_dx", dh2b, wl["w_xo"], "nt", dtype=BF16)
    g_xo = _mm_plain("xattn_out_dw", ox, dh2b, "tn")
    dxq, dkvx = _xattn_bwd(xq, kvx, dox)
    g_xq = _mm_plain("xattn_q_dw", u2, dxq, "tn")
    du2 = _mm_plain("xattn_q_dx", dxq, wl["w_xq"], "nt")
    g_xkv = _mm_plain("xattn_kv_dw", mem_n, dkvx, "tn")
    dmem_n = _mm_plain("xattn_kv_dx", dkvx, wl["w_xkv"], "nt")

    def k_mem_bwd(m_ref, d_ref, g_ref, dg_ref):
        _, dg = _rms_bwd(m_ref[...], g_ref[...], d_ref[...])
        _accum(dg_ref, dg)

    g_norm_mem, = _rows("mem_norm_bwd", k_mem_bwd, [mem, dmem_n], [sm["norm_mem_g"]], [], [(1, D_MODEL)])
    dh1, dh1b, g_norm_xattn = norm_bwd("norm_xattn_bwd", h1, du2, dh2, sm["norm_xattn_g"])

    dmerged = _mm_plain("mix_out_dx", dh1b, wl["w_out"], "nt")
    g_out = _mm_plain("mix_out_dw", merged, dh1b, "tn")

    def k_merge_bwd(dm_ref, g_ref, co_ref, mo_ref, dco_ref, dmo_ref, dgl_ref):
        dm = dm_ref[...]
        g0, g1 = g_ref[:, 0:D_MODEL], g_ref[:, D_MODEL:]
        dco_ref[...] = (dm * g0).astype(BF16)
        dmo_ref[...] = (dm * g1).astype(BF16)
        dgl_ref[:, 0:D_MODEL] = (dm * co_ref[...] * g0 * (1.0 - g0)).astype(BF16)
        dgl_ref[:, D_MODEL:] = (dm * mo_ref[...] * g1 * (1.0 - g1)).astype(BF16)

    dconv_out, dmla_out, dgl = _rows("merge_bwd", k_merge_bwd, [dmerged, gates, conv_out, mla_out], [],
                                     [(D_MODEL, BF16), (D_MODEL, BF16), (2 * D_MODEL, BF16)], tile=256)

    def epi_do(acc, xs, outs):
        for h in range(MLA_HEADS):
            lo = h * HEAD_PAD
            do = acc[:, lo:lo + HEAD_PAD]
            delta = jnp.sum(do * xs[0][:, lo:lo + HEAD_PAD].astype(F32), axis=-1, keepdims=True)
            outs[0][:, lo:lo + HEAD_PAD] = _put_stats(do.astype(BF16), delta, STAT_COL_V)

    do_aug, = _mm("mla_out_dx", dmla_out, wl["w_mla_out"], "nt", [(HW, BF16)], epi_do, row_x=[o_att], tn=HW)
    g_mla_out = _mm_plain("mla_out_dw", o_att, dmla_out, "tn")
    dq_att, dk_att, dv_att = _flash_bwd(q_aug, k_att, v_att, do_aug)

    def k_rope_bwd(dq_ref, dk_ref, dv_ref, c_ref, sa_ref, sb_ref, dqr_ref, dkv_ref, dkr_ref):
        c, sa, sb = c_ref[...], sa_ref[...], sb_ref[...]
        lane = lax.broadcasted_iota(jnp.int32, c.shape, 1)
        nope = (lane < QK_NOPE).astype(F32)
        ropem = ((lane >= QK_NOPE) & (lane < QK_NOPE + QK_ROPE)).astype(F32)
        dkr = jnp.zeros(c.shape, F32)
        for h in range(MLA_HEADS):
            lo = h * HEAD_PAD
            dqr_ref[:, lo:lo + HEAD_PAD] = (_rope(dq_ref[:, lo:lo + HEAD_PAD], c, sa, sb, -1.0) * qk_scale).astype(BF16)
            dk = dk_ref[:, lo:lo + HEAD_PAD]
            dkv_ref[:, lo:lo + HEAD_PAD] = (dk * nope).astype(BF16)
            dkr = dkr + dk
        dkv_ref[:, HW:] = dv_ref[...]
        dkr_ref[...] = (_rope(dkr * ropem, c, sa, sb, -1.0) * ropem).astype(BF16)

    dq_raw, dkv_cat, dkr = _rows("rope_bwd", k_rope_bwd, [dq_att, dk_att, dv_att, rope_c, rope_sa, rope_sb], [],
                                 [(HW, BF16), (2 * HW, BF16), (HEAD_PAD, BF16)], tile=256)
    g_uq = _mm_plain("q_up_dw", qn, dq_raw, "tn")
    dqn = _mm_plain("q_up_dx", dq_raw, wl["w_uq"], "nt")
    g_ukv = _mm_plain("kv_up_dw", kvn, dkv_cat, "tn")
    dkvn = _mm_plain("kv_up_dx", dkv_cat, wl["w_ukv"], "nt")

    def k_lora_bwd(cq_ref, ckv_ref, dqn_ref, dkvn_ref, gq_ref, gkv_ref, dcq_ref, dckv_ref, dgq_ref, dgkv_ref):
        dcq, dgq = _rms_bwd(cq_ref[...], gq_ref[...], dqn_ref[...])
        dckv, dgkv = _rms_bwd(ckv_ref[...], gkv_ref[...], dkvn_ref[...])
        dcq_ref[...] = dcq.astype(BF16)
        dckv_ref[...] = dckv.astype(BF16)
        _accum(dgq_ref, dgq)
        _accum(dgkv_ref, dgkv)

    dc_q, dc_kv, g_q_norm, g_kv_norm = _rows("lora_norm_bwd", k_lora_bwd, [c_q, c_kv, dqn, dkvn],
                                              [sm["q_norm_g"], sm["kv_norm_g"]], [(Q_LORA, BF16), (KV_LORA, BF16)],
                                              [(1, Q_LORA), (1, KV_LORA)])

    dz3 = _mm_plain("conv_out_dx", dconv_out, wl["w_conv_out"], "nt")
    g_conv_out = _mm_plain("conv_out_dw", z3, dconv_out, "tn")
    dz1, g_ln_g, g_ln_b, g_conv_b = _conv_bwd_norm(dz3, z1, sm["conv_ln_g"], sm["conv_ln_b"])
    dconv_in, g_conv_w = _conv_bwd_taps(dz1, z0, conv_in, conv_w)

    dproj = jnp.concatenate([dconv_in, dc_q, dc_kv, dkr, dgl], axis=1)
    g_in = _mm_plain("proj_dw", u1, dproj, "tn")
    du1 = _mm_plain("proj_dx", dproj, w_in, "nt")

    def k_norm1_bwd(x_ref, du_ref, dres_ref, g_ref, dx_ref, dg_ref):
        dx, dg = _rms_bwd(x_ref[...], g_ref[...], du_ref[...])
        dx_ref[...] = dres_ref[...] + dx
        _accum(dg_ref, dg)

    grad_x, g_norm_mix = _rows("norm_mix_bwd", k_norm1_bwd, [x, du1, dh1], [sm["norm_mix_g"]], [(D_MODEL, F32)],
                               [(1, D_MODEL)], tile=256)

    grads = dict(norm_mix_g=g_norm_mix, w_in=g_in, conv_w=g_conv_w[:CONV_WIDTH], conv_b=g_conv_b, conv_ln_g=g_ln_g,
                 conv_ln_b=g_ln_b, w_conv_out=g_conv_out, q_norm_g=g_q_norm, w_uq=g_uq, kv_norm_g=g_kv_norm, w_ukv=g_ukv,
                 w_mla_out=g_mla_out, w_out=g_out, norm_xattn_g=g_norm_xattn, norm_mem_g=g_norm_mem, w_xq=g_xq,
                 w_xkv=g_xkv, w_xo=g_xo, norm_mlp_g=g_norm_mlp, w_mlp1=g_mlp1, w_mlp2=g_mlp2, final_norm_g=g_final)
    return loss, grad_x, grads


def _shard(a, k, axis):
    n = a.shape[axis] // N_CHIPS
    return lax.slice_in_dim(a, k * n, (k + 1) * n, axis=axis)


def _pack_small(grads, loss):
    flat = jnp.concatenate([grads[n].reshape(-1) for n in SMALL] + [loss.reshape(-1)[:1]])
    rows = -(-flat.shape[0] // (8 * LANES)) * 8
    return jnp.pad(flat, (0, rows * LANES - flat.shape[0])).reshape(rows, LANES)


def _pack_groups(shapes):
    groups = {}
    for n in BIG:
        groups.setdefault(shapes[n][1], []).append(n)
    return groups


def _pad_rows(a, mult):
    return jnp.pad(a, ((0, -a.shape[0] % mult), (0, 0)))


def _pack_grads(grads, shapes):
    packs = []
    for width, names in _pack_groups(shapes).items():
        per_chip = [jnp.concatenate([_pad_rows(_shard(grads[n], k, SHARD_AXIS[n]).astype(BF16), PACK_ROW_ALIGN) for n in names])
                    for k in range(N_CHIPS)]
        rows = per_chip[0].shape[0]
        packs.append(jnp.stack(per_chip).reshape(N_CHIPS, 2, rows // 2, width))
    return packs


def _unpack_grads(fulls, shapes):
    out = {}
    for full, names in zip(fulls, _pack_groups(shapes).values()):
        flat, at = full.reshape(-1, full.shape[-1]), 0
        for n in names:
            rows = shapes[n][0]
            out[n] = flat[at:at + rows]
            at += rows + (-rows % PACK_ROW_ALIGN)
    return out


def _unpack(flat, names, shapes):
    out, at = {}, 0
    for n in names:
        size = math.prod(shapes[n])
        out[n] = flat[at:at + size].reshape(shapes[n])
        at += size
    return out, at


def kernel(x, mem, positions, norm_mix_g, w_in, conv_w, conv_b, conv_ln_g, conv_ln_b, w_conv_out, q_norm_g, w_uq, kv_norm_g, w_ukv, w_mla_out, w_out, norm_xattn_g, norm_mem_g, w_xq, w_xkv, w_xo, norm_mlp_g, w_mlp1, w_mlp2, final_norm_g, loss_target, m_norm_mix_g, m_w_in, m_conv_w, m_conv_b, m_conv_ln_g, m_conv_ln_b, m_w_conv_out, m_q_norm_g, m_w_uq, m_kv_norm_g, m_w_ukv, m_w_mla_out, m_w_out, m_norm_xattn_g, m_norm_mem_g, m_w_xq, m_w_xkv, m_w_xo, m_norm_mlp_g, m_w_mlp1, m_w_mlp2, m_final_norm_g, v_norm_mix_g, v_w_in, v_conv_w, v_conv_b, v_conv_ln_g, v_conv_ln_b, v_w_conv_out, v_q_norm_g, v_w_uq, v_kv_norm_g, v_w_ukv, v_w_mla_out, v_w_out, v_norm_xattn_g, v_norm_mem_g, v_w_xq, v_w_xkv, v_w_xo, v_norm_mlp_g, v_w_mlp1, v_w_mlp2, v_final_norm_g):
    args = dict(locals())
    w = {n: args[n] for n in WEIGHTS}
    m = {n: args["m_" + n] for n in WEIGHTS}
    v = {n: args["v_" + n] for n in WEIGHTS}

    shards = [w[n][0].astype(F32 if n == "conv_w" else BF16) for n in BIG]
    first, rest = list(range(FIRST_NEEDED)), list(range(FIRST_NEEDED, len(BIG)))
    sem_pairs, shards_thru, lands_thru, token = _gather_start(shards, [first, rest])

    def unshard(n, g):
        ax = SHARD_AXIS[n]
        return jnp.moveaxis(g, 0, ax).reshape(g.shape[1:1 + ax] + (N_CHIPS * g.shape[1 + ax],) + g.shape[2 + ax:])

    def first_fn(after):
        lands = _gather_wait("gather_weights_wait_first", sem_pairs[0], shards_thru[:FIRST_NEEDED], lands_thru[:FIRST_NEEDED], after)
        return _layout_w_in(unshard("w_in", lands[0])), unshard("conv_w", lands[1])

    def rest_fn(after):
        lands = _gather_wait("gather_weights_wait_rest", sem_pairs[1], shards_thru[FIRST_NEEDED:], lands_thru[FIRST_NEEDED:], after)
        return _layout_weights({n: unshard(n, g) for n, g in zip(BIG[FIRST_NEEDED:], lands)})

    sm = {n: w[n].reshape(1, -1) for n in SMALL}
    sm["norm_mix_g"] = sm["norm_mix_g"] + token[0, 0]

    loss, grad_x, grads = _local_step(x[0], mem[0], positions, loss_target[0], first_fn, rest_fn, sm)
    grads = _unlayout_grads(grads)

    shapes = {n: w[n].shape[1:] if n in BIG else w[n].shape for n in WEIGHTS}
    small_flat = _sum_over_devices(_pack_small(grads, loss)).reshape(-1)
    packs = _pack_grads(grads, shapes)
    pairs = [_pair_add(f"grad_pair_add_{i}", p, g) for i, (p, g) in enumerate(zip(packs, _pair_exchange(packs)))]
    halves = [_chip_add(f"grad_chip_add_{i}", p, g) for i, (p, g) in enumerate(zip(pairs, _chip_exchange(pairs)))]
    g_sum = _unpack_grads(_pair_share(halves), shapes)
    g_small, at = _unpack(small_flat, SMALL, shapes)
    g_sum.update(g_small)
    loss_sum = small_flat[at]

    out_g, out_d, out_m, out_v = [], [], [], []
    for n in WEIGHTS:
        g = g_sum[n].reshape(w[n].shape)
        d, nm, nv = _adamw("adamw_" + n, w[n], g, m[n], v[n])
        out_g.append(g)
        out_d.append(d)
        out_m.append(nm)
        out_v.append(nv)
    return (loss_sum, grad_x[None], *out_g, *out_d, *out_m, *out_v)
```

```python
import functools
import math

import jax
import jax.numpy as jnp
from jax import lax
from jax.experimental import pallas as pl
from jax.experimental.pallas import tpu as pltpu

F32 = jnp.float32
BF16 = jnp.bfloat16
MESH = pl.DeviceIdType.MESH

D_MODEL = 1024
CONV_CH = 512
CONV_WIDTH = 31
MLA_HEADS = 8
QK_NOPE = 64
QK_ROPE = 32
V_DIM = 64
Q_LORA = 384
KV_LORA = 256
MEM_LEN = 256
X_HEADS = 4
X_HEAD_DIM = 128
D_FF = 4096
ROPE_THETA = 10000.0
EPS = 1e-6
HEAD_PAD = 128
STAT_COL_QK = QK_NOPE + QK_ROPE
STAT_COL_V = V_DIM
HALO = 32
N_CHIPS = 4
LANES = 128

ADAM_LR = 0.001
ADAM_B1 = 0.9
ADAM_B2 = 0.999
ADAM_EPS = 1e-08
ADAM_WD = 0.01
ADAM_STEP = 10

VMEM_LIMIT = 52 * 1024 * 1024
ROW_TILES = (1024, 512, 256, 128, 64, 32, 16)
PACK_ROW_ALIGN = 32
N_DEV = 8
NEG = -1e30

BIG = ["w_in", "conv_w", "w_conv_out", "w_uq", "w_ukv", "w_mla_out", "w_out", "w_xq", "w_xkv", "w_xo", "w_mlp1", "w_mlp2"]
WEIGHT_WAITS = (2, 5)
SHARD_AXIS = {"w_in": 1, "w_conv_out": 1, "w_uq": 1, "w_ukv": 1, "w_mla_out": 1, "w_out": 0, "w_xq": 0, "w_xkv": 0,
              "w_xo": 1, "w_mlp1": 1, "w_mlp2": 0, "conv_w": 1}
EARLY_GRADS = ["w_mlp1", "w_mlp2", "w_xkv", "w_xq", "w_xo"]
SMALL = ["norm_mix_g", "conv_b", "conv_ln_g", "conv_ln_b", "q_norm_g", "kv_norm_g", "norm_xattn_g", "norm_mem_g",
         "norm_mlp_g", "final_norm_g"]
WEIGHTS = ["norm_mix_g", "w_in", "conv_w", "conv_b", "conv_ln_g", "conv_ln_b", "w_conv_out", "q_norm_g", "w_uq",
           "kv_norm_g", "w_ukv", "w_mla_out", "w_out", "norm_xattn_g", "norm_mem_g", "w_xq", "w_xkv", "w_xo",
           "norm_mlp_g", "w_mlp1", "w_mlp2", "final_norm_g"]


def _pick(n, prefs):
    for p in prefs:
        if n % p == 0:
            return p
    return n


def _params(sem):
    return pltpu.CompilerParams(dimension_semantics=sem, vmem_limit_bytes=VMEM_LIMIT)


_DIMS = {"nn": (((1,), (0,)), ((), ())), "nt": (((1,), (1,)), ((), ())), "tn": (((0,), (0,)), ((), ()))}


def _mm(name, a, b, mode, outs, epi, row_x=(), tile_x=(), vec_x=(), tm=None, tn=None, tk=None):
    if mode == "nn":
        (M, K), (_, N) = a.shape, b.shape
    elif mode == "nt":
        (M, K), (N, _) = a.shape, b.shape
    else:
        (K, M), (_, N) = a.shape, b.shape
    tm = tm or _pick(M, (1024, 512, 384, 256, 128))
    tn = tn or _pick(N, (1024, 768, 512, 384, 256, 128))
    tk = tk or _pick(K, (2048, 1920, 1024, 768, 512, 384, 256, 128))
    nk = K // tk
    rows_inner = nk == 1 and N // tn > 1
    grid = (N // tn, M // tm, nk) if rows_inner else (M // tm, N // tn, nk)

    def spec(shape, f):
        return pl.BlockSpec(shape, (lambda j, i, k: f(i, j, k)) if rows_inner else f)

    a_spec = spec((tk, tm), lambda i, j, k: (k, i)) if mode == "tn" else spec((tm, tk), lambda i, j, k: (i, k))
    b_spec = spec((tn, tk), lambda i, j, k: (j, k)) if mode == "nt" else spec((tk, tn), lambda i, j, k: (k, j))
    in_specs = [a_spec, b_spec]
    in_specs += [spec((tm, r.shape[1]), lambda i, j, k: (i, 0)) for r in row_x]
    in_specs += [spec((tm, tn), lambda i, j, k: (i, j)) for _ in tile_x]
    in_specs += [spec(v.shape, lambda i, j, k: (0, 0)) for v in vec_x]
    out_specs, out_shape = [], []
    for w, dt in outs:
        if tn == N:
            out_specs.append(spec((tm, w), lambda i, j, k: (i, 0)))
        else:
            assert w == N, (name, w, N)
            out_specs.append(spec((tm, tn), lambda i, j, k: (i, j)))
        out_shape.append(jax.ShapeDtypeStruct((M, w), dt))
    nx = len(row_x) + len(tile_x) + len(vec_x)
    dims = _DIMS[mode]

    def body(a_ref, b_ref, *rest):
        x_refs, out_refs, acc_ref = rest[:nx], rest[nx:nx + len(outs)], rest[-1]
        av, bv = a_ref[...], b_ref[...]
        if av.dtype != BF16:
            av = av.astype(BF16)
        if bv.dtype != BF16:
            bv = bv.astype(BF16)
        prod = lax.dot_general(av, bv, dims, preferred_element_type=F32)
        if nk == 1:
            acc_ref[...] = prod
            epi(acc_ref, x_refs, out_refs)
        else:
            k = pl.program_id(2)

            @pl.when(k == 0)
            def _():
                acc_ref[...] = prod

            @pl.when(k > 0)
            def _():
                acc_ref[...] += prod

            @pl.when(k == nk - 1)
            def _():
                epi(acc_ref, x_refs, out_refs)

    res = pl.pallas_call(
        body, name=name, grid=grid, in_specs=in_specs, out_specs=out_specs, out_shape=out_shape,
        scratch_shapes=[pltpu.VMEM((tm, tn), F32)],
        compiler_params=_params(("parallel", "parallel", "arbitrary")),
    )(a, b, *row_x, *tile_x, *vec_x)
    return res


def _epi_store(acc_ref, x_refs, out_refs):
    for o in out_refs:
        o[...] = acc_ref[...].astype(o.dtype)


def _mm_plain(name, a, b, mode, dtype=F32, **kw):
    n = b.shape[0] if mode == "nt" else b.shape[1]
    return _mm(name, a, b, mode, [(n, dtype)], _epi_store, **kw)[0]


def _rows(name, body, row_ins, vec_ins, row_outs, acc_outs=(), tile=512):
    S = row_ins[0].shape[0]
    t = _pick(S, (tile, 256, 128, 64, 32, 16, 8))
    in_specs = [pl.BlockSpec((t, r.shape[1]), lambda i: (i, 0)) for r in row_ins]
    in_specs += [pl.BlockSpec(v.shape, lambda i: (0, 0)) for v in vec_ins]
    out_specs = [pl.BlockSpec((t, w), lambda i: (i, 0)) for w, _ in row_outs]
    out_specs += [pl.BlockSpec(shp, lambda i: (0, 0)) for shp in acc_outs]
    out_shape = [jax.ShapeDtypeStruct((S, w), dt) for w, dt in row_outs]
    out_shape += [jax.ShapeDtypeStruct(shp, F32) for shp in acc_outs]
    sem = ("arbitrary",) if acc_outs else ("parallel",)
    return pl.pallas_call(
        functools.partial(body), name=name, grid=(S // t,), in_specs=in_specs, out_specs=out_specs,
        out_shape=out_shape, compiler_params=_params(sem),
    )(*row_ins, *vec_ins)


def _accum(ref, val):
    @pl.when(pl.program_id(0) == 0)
    def _():
        ref[...] = jnp.zeros_like(ref)

    ref[...] += val


def _colsum(v):
    return jnp.sum(v, axis=0, keepdims=True)


def _rms_fwd(x, g):
    r = lax.rsqrt(jnp.mean(x * x, axis=-1, keepdims=True) + EPS)
    return x * r * g


def _rms_bwd(x, g, du):
    r = lax.rsqrt(jnp.mean(x * x, axis=-1, keepdims=True) + EPS)
    xn = x * r
    gdu = du * g
    dx = r * (gdu - xn * jnp.mean(xn * gdu, axis=-1, keepdims=True))
    return dx, _colsum(du * xn)


def _sigmoid(v):
    return 1.0 / (1.0 + jnp.exp(-v))


def _rope(v, c, sa, sb, sign):
    return v * c + sign * (pltpu.roll(v, HEAD_PAD - QK_ROPE // 2, 1) * sa + pltpu.roll(v, QK_ROPE // 2, 1) * sb)


def _split3(v):
    hi = v.astype(BF16)
    r1 = v - hi.astype(F32)
    mid = r1.astype(BF16)
    lo = (r1 - mid.astype(F32)).astype(BF16)
    return hi, mid, lo


def _put_stats(base, stat, col):
    hi, mid, lo = _split3(stat)
    lane = lax.broadcasted_iota(jnp.int32, base.shape, 1)
    out = jnp.where(lane == col, hi, base)
    out = jnp.where(lane == col + 1, mid, out)
    return jnp.where(lane == col + 2, lo, out)


def _neg_ones(shape, col):
    lane = lax.broadcasted_iota(jnp.int32, shape, 1)
    return jnp.where((lane >= col) & (lane < col + 3), -1.0, 0.0).astype(F32)


def _shifted(ext, t):
    p = ext.shape[0]
    for b in range(8):
        rb = ext if b == 0 else pltpu.roll(ext, p - b, 0)
        for a in range(HALO // 8 + 1):
            if 8 * a + b <= HALO:
                yield 8 * a + b, rb[8 * a:8 * a + t]


def _conv_fwd(z0, conv_w, conv_b, ln_g, ln_b):
    S, C = z0.shape
    t = _pick(S, (512, 256, 128, 64, 32))
    per = t // HALO

    def body(cur_ref, prev_ref, w_ref, b_ref, g_ref, beta_ref, z1_ref, z3_ref, ext_ref):
        i = pl.program_id(0)
        ext_ref[0:HALO, :] = jnp.where(i > 0, prev_ref[...], 0.0)
        ext_ref[HALO:, :] = cur_ref[...]
        ext = ext_ref[...]
        acc = jnp.zeros((t, C), F32)
        for d, win in _shifted(ext, t):
            k = d - (HALO - CONV_WIDTH + 1)
            if 0 <= k < CONV_WIDTH:
                acc = acc + win * w_ref[k:k + 1, :]
        z1 = acc + b_ref[...]
        z1_ref[...] = z1
        mu = jnp.mean(z1, axis=-1, keepdims=True)
        zc = z1 - mu
        rs = lax.rsqrt(jnp.mean(zc * zc, axis=-1, keepdims=True) + EPS)
        z2 = zc * rs * g_ref[...] + beta_ref[...]
        z3_ref[...] = (z2 * _sigmoid(z2)).astype(BF16)

    vec = lambda v: pl.BlockSpec(v.shape, lambda i: (0, 0))
    return pl.pallas_call(
        body, name="conv_fwd", grid=(S // t,),
        in_specs=[pl.BlockSpec((t, C), lambda i: (i, 0)),
                  pl.BlockSpec((HALO, C), lambda i: (jnp.maximum(i * per - 1, 0), 0)),
                  vec(conv_w), vec(conv_b), vec(ln_g), vec(ln_b)],
        out_specs=[pl.BlockSpec((t, C), lambda i: (i, 0)), pl.BlockSpec((t, C), lambda i: (i, 0))],
        out_shape=[jax.ShapeDtypeStruct((S, C), F32), jax.ShapeDtypeStruct((S, C), BF16)],
        scratch_shapes=[pltpu.VMEM((t + HALO, C), F32)],
        compiler_params=_params(("parallel",)),
    )(z0, z0, conv_w, conv_b, ln_g, ln_b)


def _conv_bwd_norm(dz3, z1, ln_g, ln_b):
    C = z1.shape[1]

    def body(dz3_ref, z1_ref, g_ref, beta_ref, dz1_ref, dg_ref, dbeta_ref, dbias_ref):
        z1 = z1_ref[...]
        mu = jnp.mean(z1, axis=-1, keepdims=True)
        zc = z1 - mu
        rs = lax.rsqrt(jnp.mean(zc * zc, axis=-1, keepdims=True) + EPS)
        xh = zc * rs
        z2 = xh * g_ref[...] + beta_ref[...]
        sg = _sigmoid(z2)
        dz2 = dz3_ref[...] * (sg * (1.0 + z2 * (1.0 - sg)))
        dxh = dz2 * g_ref[...]
        dz1 = rs * (dxh - jnp.mean(dxh, axis=-1, keepdims=True) - xh * jnp.mean(dxh * xh, axis=-1, keepdims=True))
        dz1_ref[...] = dz1
        _accum(dg_ref, _colsum(dz2 * xh))
        _accum(dbeta_ref, _colsum(dz2))
        _accum(dbias_ref, _colsum(dz1))

    return _rows("conv_bwd_norm", body, [dz3, z1], [ln_g, ln_b], [(C, F32)], [(1, C)] * 3)


def _conv_bwd_taps(dz1, z0, conv_in, conv_w):
    S, C = z0.shape
    t = _pick(S, (512, 256, 128, 64, 32))
    per = t // HALO
    last = S // HALO - 1
    nt = S // t

    def body(dcur_ref, dnext_ref, zcur_ref, zprev_ref, cin_ref, w_ref, dcin_ref, dw_ref, dext_ref, zext_ref):
        i = pl.program_id(0)
        dcur = dcur_ref[...]
        dext_ref[0:t, :] = dcur
        dext_ref[t:, :] = jnp.where(i < nt - 1, dnext_ref[...], 0.0)
        zext_ref[0:HALO, :] = jnp.where(i > 0, zprev_ref[...], 0.0)
        zext_ref[HALO:, :] = zcur_ref[...]

        @pl.when(i == 0)
        def _():
            dw_ref[...] = jnp.zeros_like(dw_ref)

        dz0 = jnp.zeros((t, C), F32)
        for d, win in _shifted(dext_ref[...], t):
            k = CONV_WIDTH - 1 - d
            if 0 <= k < CONV_WIDTH:
                dz0 = dz0 + win * w_ref[k:k + 1, :]
        for d, win in _shifted(zext_ref[...], t):
            k = d - (HALO - CONV_WIDTH + 1)
            if 0 <= k < CONV_WIDTH:
                dw_ref[k:k + 1, :] += _colsum(dcur * win)
        a = cin_ref[:, 0:C]
        sg = _sigmoid(cin_ref[:, C:2 * C])
        dcin_ref[:, 0:C] = (dz0 * sg).astype(BF16)
        dcin_ref[:, C:2 * C] = (dz0 * a * sg * (1.0 - sg)).astype(BF16)

    return pl.pallas_call(
        body, name="conv_bwd_taps", grid=(nt,),
        in_specs=[pl.BlockSpec((t, C), lambda i: (i, 0)),
                  pl.BlockSpec((HALO, C), lambda i: (jnp.minimum((i + 1) * per, last), 0)),
                  pl.BlockSpec((t, C), lambda i: (i, 0)),
                  pl.BlockSpec((HALO, C), lambda i: (jnp.maximum(i * per - 1, 0), 0)),
                  pl.BlockSpec((t, 2 * C), lambda i: (i, 0)),
                  pl.BlockSpec(conv_w.shape, lambda i: (0, 0))],
        out_specs=[pl.BlockSpec((t, 2 * C), lambda i: (i, 0)), pl.BlockSpec((HALO, C), lambda i: (0, 0))],
        out_shape=[jax.ShapeDtypeStruct((S, 2 * C), BF16), jax.ShapeDtypeStruct((HALO, C), F32)],
        scratch_shapes=[pltpu.VMEM((t + HALO, C), F32), pltpu.VMEM((t + HALO, C), F32)],
        compiler_params=_params(("arbitrary",)),
    )(dz1, dz1, z0, z0, conv_in, conv_w)


def _lower_tri(shape, rows_are_queries):
    row = lax.broadcasted_iota(jnp.int32, shape, 0)
    col = lax.broadcasted_iota(jnp.int32, shape, 1)
    return (col <= row) if rows_are_queries else (row <= col)


HEADS_PER_STEP = 2
FWD_KEY_TILES = 4


def _flash_specs(S, t):
    w = HEADS_PER_STEP * HEAD_PAD
    blk = pl.BlockSpec((t, w), lambda h, i: (i, h))
    head = pl.BlockSpec((S, w), lambda h, i: (0, h))
    return blk, head


def _head_lanes(g):
    return slice(g * HEAD_PAD, (g + 1) * HEAD_PAD)


def _dot_nt(a, b):
    return lax.dot_general(a, b, _DIMS["nt"], preferred_element_type=F32)


def _dot_nn(a, b):
    return lax.dot_general(a, b, _DIMS["nn"], preferred_element_type=F32)


def _dot_tn(a, b):
    return lax.dot_general(a, b, _DIMS["tn"], preferred_element_type=F32)


def _flash_fwd(q, k, v):
    S = q.shape[0]
    t = _pick(S, (512, 256, 128))

    def body(q_ref, k_ref, v_ref, o_ref, qa_ref, m_ref, acc_ref):
        qi = pl.program_id(1)
        m_ref[...] = jnp.full_like(m_ref, NEG)
        acc_ref[...] = jnp.zeros_like(acc_ref)

        def step(first, width, diag):
            rows = pl.ds(pl.multiple_of(first, width), width)
            for g in range(HEADS_PER_STEP):
                hl = _head_lanes(g)
                s = _dot_nt(q_ref[:, hl], k_ref[rows, hl])
                if diag:
                    s = jnp.where(_lower_tri(s.shape, True), s, NEG)
                m_old = m_ref[g]
                m_new = jnp.maximum(m_old, jnp.max(s, axis=-1, keepdims=True))
                p = jnp.exp(s - m_new).astype(BF16)
                acc_ref[g] = jnp.exp(m_old - m_new) * acc_ref[g] + _dot_nn(p, v_ref[rows, hl])
                m_ref[g] = m_new

        def wide(kb, carry):
            step(kb * (FWD_KEY_TILES * t), FWD_KEY_TILES * t, False)
            return carry

        def single(ki, carry):
            step(ki * t, t, False)
            return carry

        lax.fori_loop(0, qi // FWD_KEY_TILES, wide, 0)
        lax.fori_loop((qi // FWD_KEY_TILES) * FWD_KEY_TILES, qi, single, 0)
        step(qi * t, t, True)
        for g in range(HEADS_PER_STEP):
            hl = _head_lanes(g)
            acc = acc_ref[g]
            l = -acc[:, STAT_COL_V:STAT_COL_V + 1]
            o_ref[:, hl] = (acc / l).astype(BF16)
            qa_ref[:, hl] = _put_stats(q_ref[:, hl], m_ref[g] + jnp.log(l), STAT_COL_QK)

    blk, head = _flash_specs(S, t)
    return pl.pallas_call(
        body, name="mla_flash_fwd", grid=(MLA_HEADS // HEADS_PER_STEP, S // t),
        in_specs=[blk, head, head], out_specs=[blk, blk],
        out_shape=[jax.ShapeDtypeStruct(q.shape, BF16), jax.ShapeDtypeStruct(q.shape, BF16)],
        scratch_shapes=[pltpu.VMEM((HEADS_PER_STEP, t, 1), F32), pltpu.VMEM((HEADS_PER_STEP, t, HEAD_PAD), F32)],
        compiler_params=_params(("parallel", "arbitrary")),
    )(q, k, v)


def _flash_bwd(qa, k, v, doa):
    S = qa.shape[0]
    t = _pick(S, (512, 256, 128))
    n = S // t

    def body(qa_ref, k_ref, v_ref, do_ref, dq_ref, dk_ref, dv_ref, dk_acc, dv_acc):
        kj = pl.program_id(1)

        @pl.when(kj == 0)
        def _():
            dq_ref[...] = jnp.zeros_like(dq_ref)

        dk_acc[...] = jnp.zeros_like(dk_acc)
        dv_acc[...] = jnp.zeros_like(dv_acc)

        def step(qi, diag):
            rows = pl.ds(pl.multiple_of(qi * t, t), t)
            for g in range(HEADS_PER_STEP):
                hl = _head_lanes(g)
                qa, do, kk = qa_ref[rows, hl], do_ref[rows, hl], k_ref[:, hl]
                st = _dot_nt(kk, qa)
                if diag:
                    st = jnp.where(_lower_tri(st.shape, False), st, NEG)
                pt = jnp.exp(st)
                dst = (pt * _dot_nt(v_ref[:, hl], do)).astype(BF16)
                dv_acc[:, hl] += _dot_nn(pt.astype(BF16), do)
                dk_acc[:, hl] += _dot_nn(dst, qa)
                dq_ref[rows, hl] += _dot_tn(dst, kk)

        def loop(qi, carry):
            step(qi, False)
            return carry

        step(kj, True)
        lax.fori_loop(kj + 1, n, loop, 0)
        dk_ref[...] = dk_acc[...]
        dv_ref[...] = dv_acc[...].astype(BF16)

    blk, head = _flash_specs(S, t)
    w = HEADS_PER_STEP * HEAD_PAD
    return pl.pallas_call(
        body, name="mla_flash_bwd", grid=(MLA_HEADS // HEADS_PER_STEP, n),
        in_specs=[head, blk, blk, head], out_specs=[head, blk, blk],
        out_shape=[jax.ShapeDtypeStruct(qa.shape, F32), jax.ShapeDtypeStruct(qa.shape, F32), jax.ShapeDtypeStruct(qa.shape, BF16)],
        scratch_shapes=[pltpu.VMEM((t, w), F32), pltpu.VMEM((t, w), F32)],
        compiler_params=_params(("parallel", "arbitrary")),
    )(qa, k, v, doa)


def _xattn_fwd(xq, kvx):
    W = X_HEADS * X_HEAD_DIM

    def body(q_ref, kv_ref, o_ref):
        for h in range(X_HEADS):
            lo = h * X_HEAD_DIM
            s = _dot_nt(q_ref[:, lo:lo + X_HEAD_DIM], kv_ref[:, lo:lo + X_HEAD_DIM])
            p = jnp.exp(s - jnp.max(s, axis=-1, keepdims=True))
            p = p / jnp.sum(p, axis=-1, keepdims=True)
            o_ref[:, lo:lo + X_HEAD_DIM] = _dot_nn(p.astype(BF16), kv_ref[:, W + lo:W + lo + X_HEAD_DIM]).astype(BF16)

    return _rows("xattn_fwd", body, [xq], [kvx], [(W, BF16)])[0]


def _xattn_bwd(xq, kvx, dox):
    W = X_HEADS * X_HEAD_DIM
    scale = X_HEAD_DIM ** -0.5

    def body(q_ref, do_ref, kv_ref, dq_ref, dkv_ref):
        @pl.when(pl.program_id(0) == 0)
        def _():
            dkv_ref[...] = jnp.zeros_like(dkv_ref)

        for h in range(X_HEADS):
            lo = h * X_HEAD_DIM
            q, k = q_ref[:, lo:lo + X_HEAD_DIM], kv_ref[:, lo:lo + X_HEAD_DIM]
            v, do = kv_ref[:, W + lo:W + lo + X_HEAD_DIM], do_ref[:, lo:lo + X_HEAD_DIM]
            s = _dot_nt(q, k)
            p = jnp.exp(s - jnp.max(s, axis=-1, keepdims=True))
            p = p / jnp.sum(p, axis=-1, keepdims=True)
            dp = _dot_nt(do, v)
            ds = (p * (dp - jnp.sum(dp * p, axis=-1, keepdims=True))).astype(BF16)
            dq_ref[:, lo:lo + X_HEAD_DIM] = (_dot_nn(ds, k) * scale).astype(BF16)
            dkv_ref[:, lo:lo + X_HEAD_DIM] += _dot_tn(ds, q)
            dkv_ref[:, W + lo:W + lo + X_HEAD_DIM] += _dot_tn(p.astype(BF16), do)

    return _rows("xattn_bwd", body, [xq, dox], [kvx], [(W, BF16)], [kvx.shape])


def _adamw(name, w, g, m, v):
    shape = w.shape
    w2, g2, m2, v2 = [a.reshape(-1, shape[-1]) for a in (w, g, m, v)]
    R, C = w2.shape
    t = _pick(R, (256, 128, 64, 32, 16, 8))
    c1 = 1.0 / (1.0 - ADAM_B1 ** ADAM_STEP)
    c2 = 1.0 / (1.0 - ADAM_B2 ** ADAM_STEP)

    def body(w_ref, g_ref, m_ref, v_ref, d_ref, nm_ref, nv_ref):
        g = g_ref[...]
        nm = ADAM_B1 * m_ref[...] + (1.0 - ADAM_B1) * g
        nv = ADAM_B2 * v_ref[...] + (1.0 - ADAM_B2) * (g * g)
        d_ref[...] = -ADAM_LR * ((nm * c1) / (jnp.sqrt(nv * c2) + ADAM_EPS) + ADAM_WD * w_ref[...])
        nm_ref[...] = nm
        nv_ref[...] = nv

    spec = pl.BlockSpec((t, C), lambda i: (i, 0))
    outs = pl.pallas_call(
        body, name=name, grid=(R // t,), in_specs=[spec] * 4, out_specs=[spec] * 3,
        out_shape=[jax.ShapeDtypeStruct((R, C), F32)] * 3, compiler_params=_params(("parallel",)),
    )(w2, g2, m2, v2)
    return [o.reshape(shape) for o in outs]


def _place():
    x, y, c = lax.axis_index("x"), lax.axis_index("y"), lax.axis_index("c")
    return x, y, c, [(1 - x, y), (x, 1 - y), (1 - x, 1 - y)]


_ANY = pl.BlockSpec(memory_space=pl.ANY)


_HBM = pl.BlockSpec(memory_space=pltpu.HBM)
_SEM = pl.BlockSpec(memory_space=pltpu.SEMAPHORE)
_SIDE_EFFECT = pltpu.SideEffectType.DATAFLOW_SIDE_EFFECTING


def _gather_copy(src, land, slot, send, recv, k, chip, c):
    return pltpu.make_async_remote_copy(src_ref=src, dst_ref=land.at[slot], send_sem=send.at[k], recv_sem=recv.at[k],
                                        device_id=(chip[0], chip[1], c), device_id_type=MESH)


def _gather_start(shards, groups):
    n, ng = len(shards), len(groups)
    lands = [jnp.broadcast_to(s[None], (N_CHIPS,) + s.shape) for s in shards]

    def body(*refs):
        ins, lnd = refs[:n], refs[n:2 * n]
        sends, recvs = refs[2 * n:2 * n + ng], refs[2 * n + ng:2 * n + 2 * ng]
        token = refs[-1]
        x, y, c, chips = _place()
        for gi, group in enumerate(groups):
            for pos, w in enumerate(group):
                for j, chip in enumerate(chips):
                    _gather_copy(ins[w], lnd[w], 2 * x + y, sends[gi], recvs[gi], 3 * pos + j, chip, c).start()
        token[...] = jnp.zeros_like(token)

    sems = [pltpu.SemaphoreType.DMA((3 * len(g),)) for g in groups]
    res = pl.pallas_call(
        body, name="gather_weights_start",
        out_shape=sems + sems + [pltpu.HBM(a.shape, a.dtype) for a in shards + lands] + [jax.ShapeDtypeStruct((8, LANES), F32)],
        in_specs=[_HBM] * (2 * n),
        out_specs=[_SEM] * (2 * ng) + [_HBM] * (2 * n) + [pl.BlockSpec(memory_space=pltpu.VMEM)],
        input_output_aliases={i: 2 * ng + i for i in range(2 * n)},
        compiler_params=pltpu.CompilerParams(has_side_effects=_SIDE_EFFECT),
    )(*[pltpu.with_memory_space_constraint(a, pltpu.HBM) for a in shards + lands])
    sem_pairs = list(zip(res[:ng], res[ng:2 * ng]))
    return sem_pairs, res[2 * ng:2 * ng + n], res[2 * ng + n:2 * ng + 2 * n], res[-1]


def _gather_wait(name, sem_pair, shards_thru, lands_thru, after):
    m = len(shards_thru)

    def body(*refs):
        ins, lnd = refs[:m], refs[m:2 * m]
        send, recv = refs[2 * m], refs[2 * m + 1]
        x, y, c, chips = _place()
        for pos in range(m):
            for j, chip in enumerate(chips):
                cp = _gather_copy(ins[pos], lnd[pos], 2 * chip[0] + chip[1], send, recv, 3 * pos + j, chip, c)
                cp.wait_send()
                cp.wait_recv()

    res = pl.pallas_call(
        body, name=name,
        out_shape=[pltpu.HBM(a.shape, a.dtype) for a in list(shards_thru) + list(lands_thru)],
        in_specs=[_HBM] * (2 * m) + [_SEM, _SEM, _ANY], out_specs=[_HBM] * (2 * m),
        input_output_aliases={i: i for i in range(2 * m)},
        compiler_params=pltpu.CompilerParams(has_side_effects=_SIDE_EFFECT),
    )(*shards_thru, *lands_thru, *sem_pair, after)
    return res[m:]


def _pair_exchange(name, packs):
    n = len(packs)

    def body(*refs):
        ins, outs, send, recv = refs[:n], refs[n:2 * n], refs[2 * n], refs[2 * n + 1]
        x, y, c, _ = _place()
        cps = []
        for g in range(n):
            cp = pltpu.make_async_remote_copy(src_ref=ins[g].at[:, pl.ds(1 - c, 1)], dst_ref=outs[g], send_sem=send.at[g],
                                              recv_sem=recv.at[g], device_id=(x, y, 1 - c), device_id_type=MESH)
            cp.start()
            cps.append(cp)
        for cp in cps:
            cp.wait()

    return pl.pallas_call(
        body, name=name, in_specs=[_ANY] * n, out_specs=[_ANY] * n,
        out_shape=[jax.ShapeDtypeStruct((N_CHIPS, 1) + p.shape[2:], p.dtype) for p in packs],
        scratch_shapes=[pltpu.SemaphoreType.DMA((n,)), pltpu.SemaphoreType.DMA((n,))],
    )(*packs)


def _chip_exchange(parts):
    n = len(parts)

    def body(*refs):
        ins, outs, send, recv = refs[:n], refs[n:2 * n], refs[2 * n], refs[2 * n + 1]
        x, y, c, chips = _place()
        cps = []
        for g in range(n):
            for j, (px, py) in enumerate(chips):
                cp = pltpu.make_async_remote_copy(src_ref=ins[g].at[2 * px + py], dst_ref=outs[g].at[2 * x + y],
                                                  send_sem=send.at[3 * g + j], recv_sem=recv.at[3 * g + j],
                                                  device_id=(px, py, c), device_id_type=MESH)
                cp.start()
                cps.append(cp)
        for g in range(n):
            for j, (px, py) in enumerate(chips):
                pltpu.make_async_remote_copy(src_ref=ins[g].at[2 * x + y], dst_ref=outs[g].at[2 * px + py],
                                             send_sem=send.at[3 * g + j], recv_sem=recv.at[3 * g + j],
                                             device_id=(px, py, c), device_id_type=MESH).wait_recv()
        for cp in cps:
            cp.wait_send()

    return pl.pallas_call(
        body, name="grad_chip_exchange", in_specs=[_ANY] * n, out_specs=[_ANY] * n,
        out_shape=[jax.ShapeDtypeStruct(p.shape, p.dtype) for p in parts],
        scratch_shapes=[pltpu.SemaphoreType.DMA((3 * n,)), pltpu.SemaphoreType.DMA((3 * n,))],
    )(*parts)


def _chip_copy(src, land, src_slot, dst_slot, send, recv, k, chip, c):
    return pltpu.make_async_remote_copy(src_ref=src.at[src_slot], dst_ref=land.at[dst_slot], send_sem=send.at[k],
                                        recv_sem=recv.at[k], device_id=(chip[0], chip[1], c), device_id_type=MESH)


def _chip_exchange_start(parts):
    n = len(parts)
    lands = [lax.empty(p.shape, p.dtype) for p in parts]

    def body(*refs):
        ins, lnd, send, recv, token = refs[:n], refs[n:2 * n], refs[2 * n], refs[2 * n + 1], refs[-1]
        x, y, c, chips = _place()
        for g in range(n):
            for j, chip in enumerate(chips):
                _chip_copy(ins[g], lnd[g], 2 * chip[0] + chip[1], 2 * x + y, send, recv, 3 * g + j, chip, c).start()
        token[...] = jnp.zeros_like(token)

    sems = [pltpu.SemaphoreType.DMA((3 * n,))] * 2
    res = pl.pallas_call(
        body, name="grad_chip_exchange_early_start",
        out_shape=sems + [pltpu.HBM(a.shape, a.dtype) for a in list(parts) + lands] + [jax.ShapeDtypeStruct((8, LANES), F32)],
        in_specs=[_HBM] * (2 * n),
        out_specs=[_SEM] * 2 + [_HBM] * (2 * n) + [pl.BlockSpec(memory_space=pltpu.VMEM)],
        input_output_aliases={i: 2 + i for i in range(2 * n)},
        compiler_params=pltpu.CompilerParams(has_side_effects=_SIDE_EFFECT),
    )(*[pltpu.with_memory_space_constraint(a, pltpu.HBM) for a in list(parts) + lands])
    return res[:2], res[2:2 + n], res[2 + n:2 + 2 * n], res[-1]


def _chip_exchange_wait(sems, parts_thru, lands_thru, after):
    n = len(parts_thru)

    def body(*refs):
        ins, lnd, send, recv = refs[:n], refs[n:2 * n], refs[2 * n], refs[2 * n + 1]
        x, y, c, chips = _place()
        for g in range(n):
            for j, chip in enumerate(chips):
                cp = _chip_copy(ins[g], lnd[g], 2 * x + y, 2 * chip[0] + chip[1], send, recv, 3 * g + j, chip, c)
                cp.wait_send()
                cp.wait_recv()

    res = pl.pallas_call(
        body, name="grad_chip_exchange_early_wait",
        out_shape=[pltpu.HBM(a.shape, a.dtype) for a in list(parts_thru) + list(lands_thru)],
        in_specs=[_HBM] * (2 * n) + [_SEM, _SEM, _ANY], out_specs=[_HBM] * (2 * n),
        input_output_aliases={i: i for i in range(2 * n)},
        compiler_params=pltpu.CompilerParams(has_side_effects=_SIDE_EFFECT),
    )(*parts_thru, *lands_thru, *sems, after)
    return res[:n], res[n:]


def _pair_share(halves):
    n = len(halves)

    def body(*refs):
        outs, send, recv = refs[n:2 * n], refs[2 * n], refs[2 * n + 1]
        x, y, c, _ = _place()
        cps = []
        for g in range(n):
            cp = pltpu.make_async_remote_copy(src_ref=outs[g].at[c], dst_ref=outs[g].at[c], send_sem=send.at[g],
                                              recv_sem=recv.at[g], device_id=(x, y, 1 - c), device_id_type=MESH)
            cp.start()
            cps.append(cp)
        for g in range(n):
            pltpu.make_async_remote_copy(src_ref=outs[g].at[c], dst_ref=outs[g].at[1 - c], send_sem=send.at[g],
                                         recv_sem=recv.at[g], device_id=(x, y, 1 - c), device_id_type=MESH).wait_recv()
        for cp in cps:
            cp.wait_send()

    return pl.pallas_call(
        body, name="grad_pair_share", in_specs=[_ANY] * n, out_specs=[_ANY] * n,
        out_shape=[jax.ShapeDtypeStruct(h.shape, h.dtype) for h in halves],
        input_output_aliases={g: g for g in range(n)},
        scratch_shapes=[pltpu.SemaphoreType.DMA((n,)), pltpu.SemaphoreType.DMA((n,))],
    )(*halves)


def _sum_over_devices(block):
    R, L = block.shape

    def gather(b_ref, o_ref, send, recv, loc):
        x, y, c, _ = _place()
        lc = pltpu.make_async_copy(b_ref, o_ref.at[4 * x + 2 * y + c], loc)
        lc.start()
        peers = [(1 - x if dx else x, 1 - y if dy else y, 1 - c if dc else c)
                 for dx in (0, 1) for dy in (0, 1) for dc in (0, 1) if dx or dy or dc]
        cps = []
        for j, peer in enumerate(peers):
            cp = pltpu.make_async_remote_copy(src_ref=b_ref, dst_ref=o_ref.at[4 * x + 2 * y + c], send_sem=send.at[j],
                                              recv_sem=recv.at[j], device_id=peer, device_id_type=MESH)
            cp.start()
            cps.append(cp)
        for j, (px, py, pc) in enumerate(peers):
            pltpu.make_async_remote_copy(src_ref=b_ref, dst_ref=o_ref.at[4 * px + 2 * py + pc], send_sem=send.at[j],
                                         recv_sem=recv.at[j], device_id=(px, py, pc), device_id_type=MESH).wait_recv()
        for cp in cps:
            cp.wait_send()
        lc.wait()

    blocks = pl.pallas_call(
        gather, name="small_grads_gather", in_specs=[_ANY], out_specs=_ANY,
        out_shape=jax.ShapeDtypeStruct((N_DEV, R, L), F32),
        scratch_shapes=[pltpu.SemaphoreType.DMA((N_DEV - 1,)), pltpu.SemaphoreType.DMA((N_DEV - 1,)), pltpu.SemaphoreType.DMA],
    )(block)

    def add(b_ref, o_ref):
        total = b_ref[0]
        for d in range(1, N_DEV):
            total = total + b_ref[d]
        o_ref[...] = total

    return pl.pallas_call(add, name="small_grads_add", out_shape=jax.ShapeDtypeStruct((R, L), F32))(blocks)


def _pair_add(name, pack, got):
    _, _, R, C = pack.shape
    t = _pick(R, ROW_TILES)
    c = lax.axis_index("c").astype(jnp.int32).reshape(1)

    def body(c_ref, p_ref, g_ref, o_ref):
        o_ref[...] = (p_ref[...].astype(F32) + g_ref[...].astype(F32)).astype(BF16)

    return pl.pallas_call(
        body, name=name,
        grid_spec=pltpu.PrefetchScalarGridSpec(
            num_scalar_prefetch=1, grid=(N_CHIPS, R // t),
            in_specs=[pl.BlockSpec((None, None, t, C), lambda k, i, c_ref: (k, c_ref[0], i, 0)),
                      pl.BlockSpec((None, None, t, C), lambda k, i, c_ref: (k, 0, i, 0))],
            out_specs=pl.BlockSpec((None, t, C), lambda k, i, c_ref: (k, i, 0))),
        out_shape=jax.ShapeDtypeStruct((N_CHIPS, R, C), BF16), compiler_params=_params(("parallel", "parallel")),
    )(c, pack, got)


def _chip_add(name, own, got):
    _, R, C = own.shape
    t = _pick(R, ROW_TILES)
    x, y, c, _ = _place()
    place = jnp.stack([c, 2 * x + y]).astype(jnp.int32)

    def body(place_ref, own_ref, g1_ref, g2_ref, g3_ref, o_ref):
        o_ref[...] = ((own_ref[...].astype(F32) + g1_ref[...].astype(F32)) + g2_ref[...].astype(F32)) + g3_ref[...].astype(F32)

    def other(d):
        return pl.BlockSpec((None, t, C), lambda i, place_ref: ((place_ref[1] + d) % N_CHIPS, i, 0))

    return pl.pallas_call(
        body, name=name,
        grid_spec=pltpu.PrefetchScalarGridSpec(
            num_scalar_prefetch=1, grid=(R // t,),
            in_specs=[pl.BlockSpec((None, t, C), lambda i, place_ref: (place_ref[1], i, 0)), other(1), other(2), other(3)],
            out_specs=pl.BlockSpec((None, t, C), lambda i, place_ref: (place_ref[0], i, 0))),
        out_shape=jax.ShapeDtypeStruct((2, R, C), F32), compiler_params=_params(("parallel",)),
    )(place, own, got, got, got)


_CUT = (2 * CONV_CH, 2 * CONV_CH + Q_LORA, 2 * CONV_CH + Q_LORA + KV_LORA, 2 * CONV_CH + Q_LORA + KV_LORA + QK_ROPE)
_KR_AT = _CUT[2] + QK_NOPE


def _pad_last(a, n):
    return jnp.pad(a, [(0, 0)] * (a.ndim - 1) + [(0, n - a.shape[-1])])


def _layout_w_in(w_in):
    kr = jnp.pad(w_in[:, _CUT[2]:_CUT[3]], ((0, 0), (QK_NOPE, HEAD_PAD - QK_NOPE - QK_ROPE)))
    return jnp.concatenate([w_in[:, :_CUT[2]], kr, w_in[:, _CUT[3]:]], axis=1)


def _layout_weights(w):
    out = dict(w)
    if "w_uq" in w:
        out["w_uq"] = _pad_last(w["w_uq"].reshape(Q_LORA, MLA_HEADS, QK_NOPE + QK_ROPE), HEAD_PAD).reshape(Q_LORA, MLA_HEADS * HEAD_PAD)
    if "w_ukv" in w:
        ukv = w["w_ukv"].reshape(KV_LORA, MLA_HEADS, QK_NOPE + V_DIM)
        uk = _pad_last(ukv[:, :, :QK_NOPE], HEAD_PAD).reshape(KV_LORA, MLA_HEADS * HEAD_PAD)
        uv = _pad_last(ukv[:, :, QK_NOPE:], HEAD_PAD).reshape(KV_LORA, MLA_HEADS * HEAD_PAD)
        out["w_ukv"] = jnp.concatenate([uk, uv], axis=1)
    if "w_mla_out" in w:
        mo = jnp.pad(w["w_mla_out"].reshape(MLA_HEADS, V_DIM, D_MODEL), ((0, 0), (0, HEAD_PAD - V_DIM), (0, 0)))
        out["w_mla_out"] = mo.reshape(MLA_HEADS * HEAD_PAD, D_MODEL)
    return out


def _unlayout_grads(g):
    gi = g["w_in"]
    w_in = jnp.concatenate([gi[:, :_CUT[2]], gi[:, _KR_AT:_KR_AT + QK_ROPE], gi[:, _CUT[2] + HEAD_PAD:]], axis=1)
    uq = g["w_uq"].reshape(Q_LORA, MLA_HEADS, HEAD_PAD)[:, :, :QK_NOPE + QK_ROPE].reshape(Q_LORA, -1)
    gk = g["w_ukv"][:, :MLA_HEADS * HEAD_PAD].reshape(KV_LORA, MLA_HEADS, HEAD_PAD)[:, :, :QK_NOPE]
    gv = g["w_ukv"][:, MLA_HEADS * HEAD_PAD:].reshape(KV_LORA, MLA_HEADS, HEAD_PAD)[:, :, :V_DIM]
    ukv = jnp.concatenate([gk, gv], axis=2).reshape(KV_LORA, -1)
    mo = g["w_mla_out"].reshape(MLA_HEADS, HEAD_PAD, D_MODEL)[:, :V_DIM].reshape(MLA_HEADS * V_DIM, D_MODEL)
    out = dict(g)
    out.update(w_in=w_in, w_uq=uq, w_ukv=ukv, w_mla_out=mo)
    return out


def _rope_tables(positions):
    half = QK_ROPE // 2
    inv_freq = ROPE_THETA ** (-jnp.arange(half, dtype=F32) / half)
    ang = positions.astype(F32).reshape(-1, 1) * inv_freq
    cos, sin = jnp.cos(ang), jnp.sin(ang)
    S = cos.shape[0]
    z16, z32, z64 = jnp.zeros((S, half), F32), jnp.zeros((S, QK_ROPE), F32), jnp.zeros((S, QK_NOPE), F32)
    c = jnp.concatenate([jnp.ones((S, QK_NOPE), F32), cos, cos, z32], axis=1)
    sa = jnp.concatenate([z64, -sin, z16, z32], axis=1)
    sb = jnp.concatenate([z64, z16, sin, z32], axis=1)
    return c, sa, sb


def _local_step(x, mem, positions, target, weight_fns, early_grads_fn, sm):
    S = x.shape[0]
    HW = MLA_HEADS * HEAD_PAD
    rope_c, rope_sa, rope_sb = _rope_tables(positions)
    qk_scale = (QK_NOPE + QK_ROPE) ** -0.5
    c0, c1, c2, c3 = _CUT[0], _CUT[1], _CUT[2], _CUT[2] + HEAD_PAD

    def k_rms1(x_ref, g_ref, u_ref):
        u_ref[...] = _rms_fwd(x_ref[...], g_ref[...]).astype(BF16)

    u1, = _rows("rms_mix", k_rms1, [x], [sm["norm_mix_g"]], [(D_MODEL, BF16)])
    w_in, conv_w = weight_fns[0](u1)

    def epi_glu(acc, xs, outs):
        a, gt = acc[:, 0:CONV_CH], acc[:, CONV_CH:2 * CONV_CH]
        outs[0][...] = acc[...]
        outs[1][...] = a * _sigmoid(gt)

    conv_in, z0 = _mm("proj_conv", u1, w_in[:, :c0], "nn", [(2 * CONV_CH, F32), (CONV_CH, F32)], epi_glu)
    c_q = _mm_plain("proj_cq", u1, w_in[:, c0:c1], "nn")
    c_kv = _mm_plain("proj_ckv", u1, w_in[:, c1:c2], "nn")
    kr_raw = _mm_plain("proj_krope", u1, w_in[:, c2:c3], "nn")

    def epi_sigmoid(acc, xs, outs):
        outs[0][...] = _sigmoid(acc[...])

    gates, = _mm("proj_gates", u1, w_in[:, c3:], "nn", [(2 * D_MODEL, F32)], epi_sigmoid)

    z1, z3 = _conv_fwd(z0, conv_w, sm["conv_b"], sm["conv_ln_g"], sm["conv_ln_b"])
    wl = weight_fns[1](z1)
    conv_out = _mm_plain("conv_out", z3, wl["w_conv_out"], "nn")

    def k_lora_norm(cq_ref, ckv_ref, gq_ref, gkv_ref, qn_ref, kvn_ref):
        qn_ref[...] = _rms_fwd(cq_ref[...], gq_ref[...]).astype(BF16)
        kvn_ref[...] = _rms_fwd(ckv_ref[...], gkv_ref[...]).astype(BF16)

    qn, kvn = _rows("lora_norm", k_lora_norm, [c_q, c_kv], [sm["q_norm_g"], sm["kv_norm_g"]],
                    [(Q_LORA, BF16), (KV_LORA, BF16)])

    def epi_q(acc, xs, outs):
        c, sa, sb = xs[0][...], xs[1][...], xs[2][...]
        for h in range(MLA_HEADS):
            lo = h * HEAD_PAD
            outs[0][:, lo:lo + HEAD_PAD] = (_rope(acc[:, lo:lo + HEAD_PAD], c, sa, sb, 1.0) * qk_scale).astype(BF16)

    q_att, = _mm("q_up", qn, wl["w_uq"], "nn", [(HW, BF16)], epi_q, row_x=[rope_c, rope_sa, rope_sb], tn=HW)

    def epi_kv(acc, xs, outs):
        kr = _rope(xs[0][...], xs[1][...], xs[2][...], xs[3][...], 1.0)
        kr = kr + _neg_ones(kr.shape, STAT_COL_QK)
        vconst = _neg_ones(kr.shape, STAT_COL_V)
        for h in range(MLA_HEADS):
            lo = h * HEAD_PAD
            outs[0][:, lo:lo + HEAD_PAD] = (acc[:, lo:lo + HEAD_PAD] + kr).astype(BF16)
            outs[1][:, lo:lo + HEAD_PAD] = (acc[:, HW + lo:HW + lo + HEAD_PAD] + vconst).astype(BF16)

    k_att, v_att = _mm("kv_up", kvn, wl["w_ukv"], "nn", [(HW, BF16), (HW, BF16)], epi_kv,
                       row_x=[kr_raw, rope_c, rope_sa, rope_sb], tn=2 * HW)

    o_att, q_aug = _flash_fwd(q_att, k_att, v_att)
    wl.update(weight_fns[2](o_att))
    mla_out = _mm_plain("mla_out", o_att, wl["w_mla_out"], "nn")

    def k_merge(g_ref, co_ref, mo_ref, out_ref):
        out_ref[...] = (g_ref[:, 0:D_MODEL] * co_ref[...] + g_ref[:, D_MODEL:] * mo_ref[...]).astype(BF16)

    merged, = _rows("merge", k_merge, [gates, conv_out, mla_out], [], [(D_MODEL, BF16)], tile=256)

    def epi_res_norm(acc, xs, outs):
        h = xs[0][...] + acc[...]
        outs[0][...] = h
        outs[1][...] = _rms_fwd(h, xs[1][...]).astype(BF16)

    h1, u2 = _mm("mix_out", merged, wl["w_out"], "nn", [(D_MODEL, F32), (D_MODEL, BF16)], epi_res_norm,
                 row_x=[x], vec_x=[sm["norm_xattn_g"]], tn=D_MODEL)

    xscale = X_HEAD_DIM ** -0.5

    def epi_scale(acc, xs, outs):
        outs[0][...] = (acc[...] * xscale).astype(BF16)

    xq, = _mm("xattn_q", u2, wl["w_xq"], "nn", [(X_HEADS * X_HEAD_DIM, BF16)], epi_scale)

    def k_mem_norm(m_ref, g_ref, o_ref):
        o_ref[...] = _rms_fwd(m_ref[...], g_ref[...]).astype(BF16)

    mem_n, = _rows("mem_norm", k_mem_norm, [mem], [sm["norm_mem_g"]], [(D_MODEL, BF16)])
    kvx = _mm_plain("xattn_kv", mem_n, wl["w_xkv"], "nn", dtype=BF16)
    ox = _xattn_fwd(xq, kvx)
    h2, u3 = _mm("xattn_out", ox, wl["w_xo"], "nn", [(D_MODEL, F32), (D_MODEL, BF16)], epi_res_norm,
                 row_x=[h1], vec_x=[sm["norm_mlp_g"]], tn=D_MODEL)

    def epi_relu2(acc, xs, outs):
        r = jnp.maximum(acc[...], 0.0)
        outs[0][...] = (r * r).astype(BF16)

    hid, = _mm("mlp_up", u3, wl["w_mlp1"], "nn", [(D_FF, BF16)], epi_relu2)

    def epi_res(acc, xs, outs):
        outs[0][...] = xs[0][...] + acc[...]

    h3, = _mm("mlp_down", hid, wl["w_mlp2"], "nn", [(D_MODEL, F32)], epi_res, row_x=[h2], tn=D_MODEL)

    def k_final(h_ref, t_ref, g_ref, dh_ref, dhb_ref, loss_ref, dg_ref):
        h, g = h_ref[...], g_ref[...]
        e = _rms_fwd(h, g) - t_ref[...]
        part = 0.5 * jnp.sum(jnp.mean(e * e, axis=-1, keepdims=True), axis=0, keepdims=True)
        _accum(loss_ref, jnp.broadcast_to(part, loss_ref.shape))
        dh, dg = _rms_bwd(h, g, e * (1.0 / D_MODEL))
        dh_ref[...] = dh
        dhb_ref[...] = dh.astype(BF16)
        _accum(dg_ref, dg)

    dh3, dh3b, loss, g_final = _rows("final_loss", k_final, [h3, target], [sm["final_norm_g"]],
                                     [(D_MODEL, F32), (D_MODEL, BF16)], [(1, LANES), (1, D_MODEL)])

    def epi_drelu2(acc, xs, outs):
        outs[0][...] = (acc[...] * (2.0 * jnp.sqrt(xs[0][...].astype(F32)))).astype(BF16)

    da1, = _mm("mlp_down_dx", dh3b, wl["w_mlp2"], "nt", [(D_FF, BF16)], epi_drelu2, tile_x=[hid])
    g_mlp2 = _mm_plain("mlp_down_dw", hid, dh3b, "tn")
    g_mlp1 = _mm_plain("mlp_up_dw", u3, da1, "tn")
    du3 = _mm_plain("mlp_up_dx", da1, wl["w_mlp1"], "nt")

    def k_norm_bwd(x_ref, du_ref, dres_ref, g_ref, dh_ref, dhb_ref, dg_ref):
        dx, dg = _rms_bwd(x_ref[...], g_ref[...], du_ref[...])
        dh = dres_ref[...] + dx
        dh_ref[...] = dh
        dhb_ref[...] = dh.astype(BF16)
        _accum(dg_ref, dg)

    def norm_bwd(name, xin, du, dres, g):
        return _rows(name, k_norm_bwd, [xin, du, dres], [g], [(D_MODEL, F32), (D_MODEL, BF16)], [(1, D_MODEL)], tile=256)

    dh2, dh2b, g_norm_mlp = norm_bwd("norm_mlp_bwd", h2, du3, dh3, sm["norm_mlp_g"])

    dox = _mm_plain("xattn_out_dx", dh2b, wl["w_xo"], "nt", dtype=BF16)
    g_xo = _mm_plain("xattn_out_dw", ox, dh2b, "tn")
    dxq, dkvx = _xattn_bwd(xq, kvx, dox)
    g_xq = _mm_plain("xattn_q_dw", u2, dxq, "tn")
    du2 = _mm_plain("xattn_q_dx", dxq, wl["w_xq"], "nt")
    g_xkv = _mm_plain("xattn_kv_dw", mem_n, dkvx, "tn")
    dmem_n = _mm_plain("xattn_kv_dx", dkvx, wl["w_xkv"], "nt")

    def k_mem_bwd(m_ref, d_ref, g_ref, dg_ref):
        _, dg = _rms_bwd(m_ref[...], g_ref[...], d_ref[...])
        _accum(dg_ref, dg)

    g_norm_mem, = _rows("mem_norm_bwd", k_mem_bwd, [mem, dmem_n], [sm["norm_mem_g"]], [], [(1, D_MODEL)])
    token = early_grads_fn(dict(w_mlp1=g_mlp1, w_mlp2=g_mlp2, w_xo=g_xo, w_xq=g_xq, w_xkv=g_xkv))
    dh1, dh1b, g_norm_xattn = norm_bwd("norm_xattn_bwd", h1, du2, dh2, sm["norm_xattn_g"] + token[0, 0])

    dmerged = _mm_plain("mix_out_dx", dh1b, wl["w_out"], "nt")
    g_out = _mm_plain("mix_out_dw", merged, dh1b, "tn")

    def k_merge_bwd(dm_ref, g_ref, co_ref, mo_ref, dco_ref, dmo_ref, dgl_ref):
        dm = dm_ref[...]
        g0, g1 = g_ref[:, 0:D_MODEL], g_ref[:, D_MODEL:]
        dco_ref[...] = (dm * g0).astype(BF16)
        dmo_ref[...] = (dm * g1).astype(BF16)
        dgl_ref[:, 0:D_MODEL] = (dm * co_ref[...] * g0 * (1.0 - g0)).astype(BF16)
        dgl_ref[:, D_MODEL:] = (dm * mo_ref[...] * g1 * (1.0 - g1)).astype(BF16)

    dconv_out, dmla_out, dgl = _rows("merge_bwd", k_merge_bwd, [dmerged, gates, conv_out, mla_out], [],
                                     [(D_MODEL, BF16), (D_MODEL, BF16), (2 * D_MODEL, BF16)], tile=256)

    def epi_do(acc, xs, outs):
        for h in range(MLA_HEADS):
            lo = h * HEAD_PAD
            do = acc[:, lo:lo + HEAD_PAD]
            delta = jnp.sum(do * xs[0][:, lo:lo + HEAD_PAD].astype(F32), axis=-1, keepdims=True)
            outs[0][:, lo:lo + HEAD_PAD] = _put_stats(do.astype(BF16), delta, STAT_COL_V)

    do_aug, = _mm("mla_out_dx", dmla_out, wl["w_mla_out"], "nt", [(HW, BF16)], epi_do, row_x=[o_att], tn=HW)
    g_mla_out = _mm_plain("mla_out_dw", o_att, dmla_out, "tn")
    dq_att, dk_att, dv_att = _flash_bwd(q_aug, k_att, v_att, do_aug)

    def k_rope_bwd(dq_ref, dk_ref, dv_ref, c_ref, sa_ref, sb_ref, dqr_ref, dkv_ref, dkr_ref):
        c, sa, sb = c_ref[...], sa_ref[...], sb_ref[...]
        lane = lax.broadcasted_iota(jnp.int32, c.shape, 1)
        nope = (lane < QK_NOPE).astype(F32)
        ropem = ((lane >= QK_NOPE) & (lane < QK_NOPE + QK_ROPE)).astype(F32)
        dkr = jnp.zeros(c.shape, F32)
        for h in range(MLA_HEADS):
            lo = h * HEAD_PAD
            dqr_ref[:, lo:lo + HEAD_PAD] = (_rope(dq_ref[:, lo:lo + HEAD_PAD], c, sa, sb, -1.0) * qk_scale).astype(BF16)
            dk = dk_ref[:, lo:lo + HEAD_PAD]
            dkv_ref[:, lo:lo + HEAD_PAD] = (dk * nope).astype(BF16)
            dkr = dkr + dk
        dkv_ref[:, HW:] = dv_ref[...]
        dkr_ref[...] = (_rope(dkr * ropem, c, sa, sb, -1.0) * ropem).astype(BF16)

    dq_raw, dkv_cat, dkr = _rows("rope_bwd", k_rope_bwd, [dq_att, dk_att, dv_att, rope_c, rope_sa, rope_sb], [],
                                 [(HW, BF16), (2 * HW, BF16), (HEAD_PAD, BF16)], tile=256)
    g_uq = _mm_plain("q_up_dw", qn, dq_raw, "tn")
    dqn = _mm_plain("q_up_dx", dq_raw, wl["w_uq"], "nt")
    g_ukv = _mm_plain("kv_up_dw", kvn, dkv_cat, "tn")
    dkvn = _mm_plain("kv_up_dx", dkv_cat, wl["w_ukv"], "nt")

    def k_lora_bwd(cq_ref, ckv_ref, dqn_ref, dkvn_ref, gq_ref, gkv_ref, dcq_ref, dckv_ref, dgq_ref, dgkv_ref):
        dcq, dgq = _rms_bwd(cq_ref[...], gq_ref[...], dqn_ref[...])
        dckv, dgkv = _rms_bwd(ckv_ref[...], gkv_ref[...], dkvn_ref[...])
        dcq_ref[...] = dcq.astype(BF16)
        dckv_ref[...] = dckv.astype(BF16)
        _accum(dgq_ref, dgq)
        _accum(dgkv_ref, dgkv)

    dc_q, dc_kv, g_q_norm, g_kv_norm = _rows("lora_norm_bwd", k_lora_bwd, [c_q, c_kv, dqn, dkvn],
                                              [sm["q_norm_g"], sm["kv_norm_g"]], [(Q_LORA, BF16), (KV_LORA, BF16)],
                                              [(1, Q_LORA), (1, KV_LORA)])

    dz3 = _mm_plain("conv_out_dx", dconv_out, wl["w_conv_out"], "nt")
    g_conv_out = _mm_plain("conv_out_dw", z3, dconv_out, "tn")
    dz1, g_ln_g, g_ln_b, g_conv_b = _conv_bwd_norm(dz3, z1, sm["conv_ln_g"], sm["conv_ln_b"])
    dconv_in, g_conv_w = _conv_bwd_taps(dz1, z0, conv_in, conv_w)

    dproj = jnp.concatenate([dconv_in, dc_q, dc_kv, dkr, dgl], axis=1)
    g_in = _mm_plain("proj_dw", u1, dproj, "tn")
    du1 = _mm_plain("proj_dx", dproj, w_in, "nt")

    def k_norm1_bwd(x_ref, du_ref, dres_ref, g_ref, dx_ref, dg_ref):
        dx, dg = _rms_bwd(x_ref[...], g_ref[...], du_ref[...])
        dx_ref[...] = dres_ref[...] + dx
        _accum(dg_ref, dg)

    grad_x, g_norm_mix = _rows("norm_mix_bwd", k_norm1_bwd, [x, du1, dh1], [sm["norm_mix_g"]], [(D_MODEL, F32)],
                               [(1, D_MODEL)], tile=256)

    grads = dict(norm_mix_g=g_norm_mix, w_in=g_in, conv_w=g_conv_w[:CONV_WIDTH], conv_b=g_conv_b, conv_ln_g=g_ln_g,
                 conv_ln_b=g_ln_b, w_conv_out=g_conv_out, q_norm_g=g_q_norm, w_uq=g_uq, kv_norm_g=g_kv_norm, w_ukv=g_ukv,
                 w_mla_out=g_mla_out, w_out=g_out, norm_xattn_g=g_norm_xattn, norm_mem_g=g_norm_mem,
                 norm_mlp_g=g_norm_mlp, final_norm_g=g_final)
    return loss, grad_x, grads


def _shard(a, k, axis):
    n = a.shape[axis] // N_CHIPS
    return lax.slice_in_dim(a, k * n, (k + 1) * n, axis=axis)


def _pack_small(grads, loss):
    flat = jnp.concatenate([grads[n].reshape(-1) for n in SMALL] + [loss.reshape(-1)[:1]])
    rows = -(-flat.shape[0] // (8 * LANES)) * 8
    return jnp.pad(flat, (0, rows * LANES - flat.shape[0])).reshape(rows, LANES)


def _pack_groups(shapes, names):
    groups = {}
    for n in names:
        groups.setdefault(shapes[n][1], []).append(n)
    return groups


def _pad_rows(a, mult):
    return jnp.pad(a, ((0, -a.shape[0] % mult), (0, 0)))


def _pack_grads(grads, shapes, names):
    packs = []
    for width, group in _pack_groups(shapes, names).items():
        per_chip = [jnp.concatenate([_pad_rows(_shard(grads[n], k, SHARD_AXIS[n]).astype(BF16), PACK_ROW_ALIGN) for n in group])
                    for k in range(N_CHIPS)]
        rows = per_chip[0].shape[0]
        packs.append(jnp.stack(per_chip).reshape(N_CHIPS, 2, rows // 2, width))
    return packs


def _unpack_grads(fulls, shapes, names):
    out = {}
    for full, group in zip(fulls, _pack_groups(shapes, names).values()):
        flat, at = full.reshape(-1, full.shape[-1]), 0
        for n in group:
            rows = shapes[n][0]
            out[n] = flat[at:at + rows]
            at += rows + (-rows % PACK_ROW_ALIGN)
    return out


def _unpack(flat, names, shapes):
    out, at = {}, 0
    for n in names:
        size = math.prod(shapes[n])
        out[n] = flat[at:at + size].reshape(shapes[n])
        at += size
    return out, at


def kernel(x, mem, positions, norm_mix_g, w_in, conv_w, conv_b, conv_ln_g, conv_ln_b, w_conv_out, q_norm_g, w_uq, kv_norm_g, w_ukv, w_mla_out, w_out, norm_xattn_g, norm_mem_g, w_xq, w_xkv, w_xo, norm_mlp_g, w_mlp1, w_mlp2, final_norm_g, loss_target, m_norm_mix_g, m_w_in, m_conv_w, m_conv_b, m_conv_ln_g, m_conv_ln_b, m_w_conv_out, m_q_norm_g, m_w_uq, m_kv_norm_g, m_w_ukv, m_w_mla_out, m_w_out, m_norm_xattn_g, m_norm_mem_g, m_w_xq, m_w_xkv, m_w_xo, m_norm_mlp_g, m_w_mlp1, m_w_mlp2, m_final_norm_g, v_norm_mix_g, v_w_in, v_conv_w, v_conv_b, v_conv_ln_g, v_conv_ln_b, v_w_conv_out, v_q_norm_g, v_w_uq, v_kv_norm_g, v_w_ukv, v_w_mla_out, v_w_out, v_norm_xattn_g, v_norm_mem_g, v_w_xq, v_w_xkv, v_w_xo, v_norm_mlp_g, v_w_mlp1, v_w_mlp2, v_final_norm_g):
    args = dict(locals())
    w = {n: args[n] for n in WEIGHTS}
    m = {n: args["m_" + n] for n in WEIGHTS}
    v = {n: args["v_" + n] for n in WEIGHTS}

    shards = [w[n][0].astype(F32 if n == "conv_w" else BF16) for n in BIG]
    bounds = (0,) + WEIGHT_WAITS + (len(BIG),)
    spans = [slice(lo, hi) for lo, hi in zip(bounds[:-1], bounds[1:])]
    sem_pairs, shards_thru, lands_thru, token = _gather_start(shards, [list(range(len(BIG)))[sp] for sp in spans])

    def unshard(n, g):
        ax = SHARD_AXIS[n]
        return jnp.moveaxis(g, 0, ax).reshape(g.shape[1:1 + ax] + (N_CHIPS * g.shape[1 + ax],) + g.shape[2 + ax:])

    def wait_fn(i):
        def fn(after):
            lands = _gather_wait(f"gather_weights_wait_{i}", sem_pairs[i], shards_thru[spans[i]], lands_thru[spans[i]], after)
            full = {n: unshard(n, g) for n, g in zip(BIG[spans[i]], lands)}
            return (_layout_w_in(full["w_in"]), full["conv_w"]) if i == 0 else _layout_weights(full)
        return fn

    sm = {n: w[n].reshape(1, -1) for n in SMALL}
    sm["norm_mix_g"] = sm["norm_mix_g"] + token[0, 0]

    shapes = {n: w[n].shape[1:] if n in BIG else w[n].shape for n in WEIGHTS}
    early = {}

    def early_grads_fn(g):
        packs = _pack_grads(g, shapes, EARLY_GRADS)
        pairs = [_pair_add(f"grad_pair_add_early_{i}", p, r)
                 for i, (p, r) in enumerate(zip(packs, _pair_exchange("grad_pair_exchange_early", packs)))]
        early["sems"], early["parts"], early["lands"], tok = _chip_exchange_start(pairs)
        return tok

    loss, grad_x, grads = _local_step(x[0], mem[0], positions, loss_target[0], [wait_fn(i) for i in range(3)], early_grads_fn, sm)
    grads = _unlayout_grads(grads)

    small_flat = _sum_over_devices(_pack_small(grads, loss)).reshape(-1)
    late_names = [n for n in BIG if n not in EARLY_GRADS]
    packs = _pack_grads(grads, shapes, late_names)
    pairs = [_pair_add(f"grad_pair_add_{i}", p, g) for i, (p, g) in enumerate(zip(packs, _pair_exchange("grad_pair_exchange", packs)))]
    halves = [_chip_add(f"grad_chip_add_{i}", p, g) for i, (p, g) in enumerate(zip(pairs, _chip_exchange(pairs)))]
    early_own, early_got = _chip_exchange_wait(early["sems"], early["parts"], early["lands"], halves[0])
    halves += [_chip_add(f"grad_chip_add_early_{i}", p, g) for i, (p, g) in enumerate(zip(early_own, early_got))]
    fulls = _pair_share(halves)
    g_sum = _unpack_grads(fulls[:len(packs)], shapes, late_names)
    g_sum.update(_unpack_grads(fulls[len(packs):], shapes, EARLY_GRADS))
    g_small, at = _unpack(small_flat, SMALL, shapes)
    g_sum.update(g_small)
    loss_sum = small_flat[at]

    out_g, out_d, out_m, out_v = [], [], [], []
    for n in WEIGHTS:
        g = g_sum[n].reshape(w[n].shape)
        d, nm, nv = _adamw("adamw_" + n, w[n], g, m[n], v[n])
        out_g.append(g)
        out_d.append(d)
        out_m.append(nm)
        out_v.append(nv)
    return (loss_sum, grad_x[None], *out_g, *out_d, *out_m, *out_v)
```

```python
import functools
import math

import jax
import jax.numpy as jnp
from jax import lax
from jax.experimental import pallas as pl
from jax.experimental.pallas import tpu as pltpu

F32 = jnp.float32
BF16 = jnp.bfloat16
MESH = pl.DeviceIdType.MESH

D_MODEL = 1024
CONV_CH = 512
CONV_WIDTH = 31
MLA_HEADS = 8
QK_NOPE = 64
QK_ROPE = 32
V_DIM = 64
Q_LORA = 384
KV_LORA = 256
MEM_LEN = 256
X_HEADS = 4
X_HEAD_DIM = 128
D_FF = 4096
ROPE_THETA = 10000.0
EPS = 1e-6
HEAD_PAD = 128
STAT_COL_QK = QK_NOPE + QK_ROPE
STAT_COL_V = V_DIM
HALO = 32
N_CHIPS = 4
LANES = 128

ADAM_LR = 0.001
ADAM_B1 = 0.9
ADAM_B2 = 0.999
ADAM_EPS = 1e-08
ADAM_WD = 0.01
ADAM_STEP = 10

VMEM_LIMIT = 52 * 1024 * 1024
ROW_TILES = (1024, 512, 256, 128, 64, 32, 16)
PACK_ROW_ALIGN = 32
N_DEV = 8
NEG = -1e30

BIG = ["w_in", "conv_w", "w_conv_out", "w_uq", "w_ukv", "w_mla_out", "w_out", "w_xq", "w_xkv", "w_xo", "w_mlp1", "w_mlp2"]
WEIGHT_WAITS = (2, 5)
SHARD_AXIS = {"w_in": 1, "w_conv_out": 1, "w_uq": 1, "w_ukv": 1, "w_mla_out": 1, "w_out": 0, "w_xq": 0, "w_xkv": 0,
              "w_xo": 1, "w_mlp1": 1, "w_mlp2": 0, "conv_w": 1}
EARLY_GRADS = ["w_mlp1", "w_mlp2", "w_xkv", "w_xq", "w_xo"]
SMALL = ["norm_mix_g", "conv_b", "conv_ln_g", "conv_ln_b", "q_norm_g", "kv_norm_g", "norm_xattn_g", "norm_mem_g",
         "norm_mlp_g", "final_norm_g"]
WEIGHTS = ["norm_mix_g", "w_in", "conv_w", "conv_b", "conv_ln_g", "conv_ln_b", "w_conv_out", "q_norm_g", "w_uq",
           "kv_norm_g", "w_ukv", "w_mla_out", "w_out", "norm_xattn_g", "norm_mem_g", "w_xq", "w_xkv", "w_xo",
           "norm_mlp_g", "w_mlp1", "w_mlp2", "final_norm_g"]


def _pick(n, prefs):
    for p in prefs:
        if n % p == 0:
            return p
    return n


def _params(sem):
    return pltpu.CompilerParams(dimension_semantics=sem, vmem_limit_bytes=VMEM_LIMIT)


_DIMS = {"nn": (((1,), (0,)), ((), ())), "nt": (((1,), (1,)), ((), ())), "tn": (((0,), (0,)), ((), ()))}


def _mm(name, a, b, mode, outs, epi, row_x=(), tile_x=(), vec_x=(), tm=None, tn=None, tk=None):
    if mode == "nn":
        (M, K), (_, N) = a.shape, b.shape
    elif mode == "nt":
        (M, K), (N, _) = a.shape, b.shape
    else:
        (K, M), (_, N) = a.shape, b.shape
    tm = tm or _pick(M, (1024, 512, 384, 256, 128))
    tn = tn or _pick(N, (1024, 768, 512, 384, 256, 128))
    tk = tk or _pick(K, (2048, 1920, 1024, 768, 512, 384, 256, 128))
    nk = K // tk
    rows_inner = nk == 1 and N // tn > 1
    grid = (N // tn, M // tm, nk) if rows_inner else (M // tm, N // tn, nk)

    def spec(shape, f):
        return pl.BlockSpec(shape, (lambda j, i, k: f(i, j, k)) if rows_inner else f)

    a_spec = spec((tk, tm), lambda i, j, k: (k, i)) if mode == "tn" else spec((tm, tk), lambda i, j, k: (i, k))
    b_spec = spec((tn, tk), lambda i, j, k: (j, k)) if mode == "nt" else spec((tk, tn), lambda i, j, k: (k, j))
    in_specs = [a_spec, b_spec]
    in_specs += [spec((tm, r.shape[1]), lambda i, j, k: (i, 0)) for r in row_x]
    in_specs += [spec((tm, tn), lambda i, j, k: (i, j)) for _ in tile_x]
    in_specs += [spec(v.shape, lambda i, j, k: (0, 0)) for v in vec_x]
    out_specs, out_shape = [], []
    for w, dt in outs:
        if tn == N:
            out_specs.append(spec((tm, w), lambda i, j, k: (i, 0)))
        else:
            assert w == N, (name, w, N)
            out_specs.append(spec((tm, tn), lambda i, j, k: (i, j)))
        out_shape.append(jax.ShapeDtypeStruct((M, w), dt))
    nx = len(row_x) + len(tile_x) + len(vec_x)
    dims = _DIMS[mode]

    def body(a_ref, b_ref, *rest):
        x_refs, out_refs, acc_ref = rest[:nx], rest[nx:nx + len(outs)], rest[-1]
        av, bv = a_ref[...], b_ref[...]
        if av.dtype != BF16:
            av = av.astype(BF16)
        if bv.dtype != BF16:
            bv = bv.astype(BF16)
        prod = lax.dot_general(av, bv, dims, preferred_element_type=F32)
        if nk == 1:
            acc_ref[...] = prod
            epi(acc_ref, x_refs, out_refs)
        else:
            k = pl.program_id(2)

            @pl.when(k == 0)
            def _():
                acc_ref[...] = prod

            @pl.when(k > 0)
            def _():
                acc_ref[...] += prod

            @pl.when(k == nk - 1)
            def _():
                epi(acc_ref, x_refs, out_refs)

    res = pl.pallas_call(
        body, name=name, grid=grid, in_specs=in_specs, out_specs=out_specs, out_shape=out_shape,
        scratch_shapes=[pltpu.VMEM((tm, tn), F32)],
        compiler_params=_params(("parallel", "parallel", "arbitrary")),
    )(a, b, *row_x, *tile_x, *vec_x)
    return res


def _epi_store(acc_ref, x_refs, out_refs):
    for o in out_refs:
        o[...] = acc_ref[...].astype(o.dtype)


def _mm_plain(name, a, b, mode, dtype=F32, **kw):
    n = b.shape[0] if mode == "nt" else b.shape[1]
    return _mm(name, a, b, mode, [(n, dtype)], _epi_store, **kw)[0]


def _rows(name, body, row_ins, vec_ins, row_outs, acc_outs=(), tile=512):
    S = row_ins[0].shape[0]
    t = _pick(S, (tile, 256, 128, 64, 32, 16, 8))
    in_specs = [pl.BlockSpec((t, r.shape[1]), lambda i: (i, 0)) for r in row_ins]
    in_specs += [pl.BlockSpec(v.shape, lambda i: (0, 0)) for v in vec_ins]
    out_specs = [pl.BlockSpec((t, w), lambda i: (i, 0)) for w, _ in row_outs]
    out_specs += [pl.BlockSpec(shp, lambda i: (0, 0)) for shp in acc_outs]
    out_shape = [jax.ShapeDtypeStruct((S, w), dt) for w, dt in row_outs]
    out_shape += [jax.ShapeDtypeStruct(shp, F32) for shp in acc_outs]
    sem = ("arbitrary",) if acc_outs else ("parallel",)
    return pl.pallas_call(
        functools.partial(body), name=name, grid=(S // t,), in_specs=in_specs, out_specs=out_specs,
        out_shape=out_shape, compiler_params=_params(sem),
    )(*row_ins, *vec_ins)


def _accum(ref, val):
    @pl.when(pl.program_id(0) == 0)
    def _():
        ref[...] = jnp.zeros_like(ref)

    ref[...] += val


def _colsum(v):
    return jnp.sum(v, axis=0, keepdims=True)


def _rms_fwd(x, g):
    r = lax.rsqrt(jnp.mean(x * x, axis=-1, keepdims=True) + EPS)
    return x * r * g


def _rms_bwd(x, g, du):
    r = lax.rsqrt(jnp.mean(x * x, axis=-1, keepdims=True) + EPS)
    xn = x * r
    gdu = du * g
    dx = r * (gdu - xn * jnp.mean(xn * gdu, axis=-1, keepdims=True))
    return dx, _colsum(du * xn)


def _sigmoid(v):
    return 1.0 / (1.0 + jnp.exp(-v))


def _rope(v, c, sa, sb, sign):
    return v * c + sign * (pltpu.roll(v, HEAD_PAD - QK_ROPE // 2, 1) * sa + pltpu.roll(v, QK_ROPE // 2, 1) * sb)


def _split3(v):
    hi = v.astype(BF16)
    r1 = v - hi.astype(F32)
    mid = r1.astype(BF16)
    lo = (r1 - mid.astype(F32)).astype(BF16)
    return hi, mid, lo


def _put_stats(base, stat, col):
    hi, mid, lo = _split3(stat)
    lane = lax.broadcasted_iota(jnp.int32, base.shape, 1)
    out = jnp.where(lane == col, hi, base)
    out = jnp.where(lane == col + 1, mid, out)
    return jnp.where(lane == col + 2, lo, out)


def _neg_ones(shape, col):
    lane = lax.broadcasted_iota(jnp.int32, shape, 1)
    return jnp.where((lane >= col) & (lane < col + 3), -1.0, 0.0).astype(F32)


def _shifted(ext, t):
    p = ext.shape[0]
    for b in range(8):
        rb = ext if b == 0 else pltpu.roll(ext, p - b, 0)
        for a in range(HALO // 8 + 1):
            if 8 * a + b <= HALO:
                yield 8 * a + b, rb[8 * a:8 * a + t]


def _conv_fwd(z0, conv_w, conv_b, ln_g, ln_b):
    S, C = z0.shape
    t = _pick(S, (512, 256, 128, 64, 32))
    per = t // HALO

    def body(cur_ref, prev_ref, w_ref, b_ref, g_ref, beta_ref, z1_ref, z3_ref, ext_ref):
        i = pl.program_id(0)
        ext_ref[0:HALO, :] = jnp.where(i > 0, prev_ref[...], 0.0)
        ext_ref[HALO:, :] = cur_ref[...]
        ext = ext_ref[...]
        acc = jnp.zeros((t, C), F32)
        for d, win in _shifted(ext, t):
            k = d - (HALO - CONV_WIDTH + 1)
            if 0 <= k < CONV_WIDTH:
                acc = acc + win * w_ref[k:k + 1, :]
        z1 = acc + b_ref[...]
        z1_ref[...] = z1
        mu = jnp.mean(z1, axis=-1, keepdims=True)
        zc = z1 - mu
        rs = lax.rsqrt(jnp.mean(zc * zc, axis=-1, keepdims=True) + EPS)
        z2 = zc * rs * g_ref[...] + beta_ref[...]
        z3_ref[...] = (z2 * _sigmoid(z2)).astype(BF16)

    vec = lambda v: pl.BlockSpec(v.shape, lambda i: (0, 0))
    return pl.pallas_call(
        body, name="conv_fwd", grid=(S // t,),
        in_specs=[pl.BlockSpec((t, C), lambda i: (i, 0)),
                  pl.BlockSpec((HALO, C), lambda i: (jnp.maximum(i * per - 1, 0), 0)),
                  vec(conv_w), vec(conv_b), vec(ln_g), vec(ln_b)],
        out_specs=[pl.BlockSpec((t, C), lambda i: (i, 0)), pl.BlockSpec((t, C), lambda i: (i, 0))],
        out_shape=[jax.ShapeDtypeStruct((S, C), F32), jax.ShapeDtypeStruct((S, C), BF16)],
        scratch_shapes=[pltpu.VMEM((t + HALO, C), F32)],
        compiler_params=_params(("parallel",)),
    )(z0, z0, conv_w, conv_b, ln_g, ln_b)


def _conv_bwd_norm(dz3, z1, ln_g, ln_b):
    C = z1.shape[1]

    def body(dz3_ref, z1_ref, g_ref, beta_ref, dz1_ref, dg_ref, dbeta_ref, dbias_ref):
        z1 = z1_ref[...]
        mu = jnp.mean(z1, axis=-1, keepdims=True)
        zc = z1 - mu
        rs = lax.rsqrt(jnp.mean(zc * zc, axis=-1, keepdims=True) + EPS)
        xh = zc * rs
        z2 = xh * g_ref[...] + beta_ref[...]
        sg = _sigmoid(z2)
        dz2 = dz3_ref[...] * (sg * (1.0 + z2 * (1.0 - sg)))
        dxh = dz2 * g_ref[...]
        dz1 = rs * (dxh - jnp.mean(dxh, axis=-1, keepdims=True) - xh * jnp.mean(dxh * xh, axis=-1, keepdims=True))
        dz1_ref[...] = dz1
        _accum(dg_ref, _colsum(dz2 * xh))
        _accum(dbeta_ref, _colsum(dz2))
        _accum(dbias_ref, _colsum(dz1))

    return _rows("conv_bwd_norm", body, [dz3, z1], [ln_g, ln_b], [(C, F32)], [(1, C)] * 3)


def _conv_bwd_taps(dz1, z0, conv_in, conv_w):
    S, C = z0.shape
    t = _pick(S, (512, 256, 128, 64, 32))
    per = t // HALO
    last = S // HALO - 1
    nt = S // t

    def body(dcur_ref, dnext_ref, zcur_ref, zprev_ref, cin_ref, w_ref, dcin_ref, dw_ref, dext_ref, zext_ref):
        i = pl.program_id(0)
        dcur = dcur_ref[...]
        dext_ref[0:t, :] = dcur
        dext_ref[t:, :] = jnp.where(i < nt - 1, dnext_ref[...], 0.0)
        zext_ref[0:HALO, :] = jnp.where(i > 0, zprev_ref[...], 0.0)
        zext_ref[HALO:, :] = zcur_ref[...]

        @pl.when(i == 0)
        def _():
            dw_ref[...] = jnp.zeros_like(dw_ref)

        dz0 = jnp.zeros((t, C), F32)
        for d, win in _shifted(dext_ref[...], t):
            k = CONV_WIDTH - 1 - d
            if 0 <= k < CONV_WIDTH:
                dz0 = dz0 + win * w_ref[k:k + 1, :]
        for d, win in _shifted(zext_ref[...], t):
            k = d - (HALO - CONV_WIDTH + 1)
            if 0 <= k < CONV_WIDTH:
                dw_ref[k:k + 1, :] += _colsum(dcur * win)
        a = cin_ref[:, 0:C]
        sg = _sigmoid(cin_ref[:, C:2 * C])
        dcin_ref[:, 0:C] = (dz0 * sg).astype(BF16)
        dcin_ref[:, C:2 * C] = (dz0 * a * sg * (1.0 - sg)).astype(BF16)

    return pl.pallas_call(
        body, name="conv_bwd_taps", grid=(nt,),
        in_specs=[pl.BlockSpec((t, C), lambda i: (i, 0)),
                  pl.BlockSpec((HALO, C), lambda i: (jnp.minimum((i + 1) * per, last), 0)),
                  pl.BlockSpec((t, C), lambda i: (i, 0)),
                  pl.BlockSpec((HALO, C), lambda i: (jnp.maximum(i * per - 1, 0), 0)),
                  pl.BlockSpec((t, 2 * C), lambda i: (i, 0)),
                  pl.BlockSpec(conv_w.shape, lambda i: (0, 0))],
        out_specs=[pl.BlockSpec((t, 2 * C), lambda i: (i, 0)), pl.BlockSpec((HALO, C), lambda i: (0, 0))],
        out_shape=[jax.ShapeDtypeStruct((S, 2 * C), BF16), jax.ShapeDtypeStruct((HALO, C), F32)],
        scratch_shapes=[pltpu.VMEM((t + HALO, C), F32), pltpu.VMEM((t + HALO, C), F32)],
        compiler_params=_params(("arbitrary",)),
    )(dz1, dz1, z0, z0, conv_in, conv_w)


def _lower_tri(shape, rows_are_queries):
    row = lax.broadcasted_iota(jnp.int32, shape, 0)
    col = lax.broadcasted_iota(jnp.int32, shape, 1)
    return (col <= row) if rows_are_queries else (row <= col)


HEADS_PER_STEP = 2
FWD_KEY_TILES = 4


def _flash_specs(S, t):
    w = HEADS_PER_STEP * HEAD_PAD
    blk = pl.BlockSpec((t, w), lambda h, i: (i, h))
    head = pl.BlockSpec((S, w), lambda h, i: (0, h))
    return blk, head


def _head_lanes(g):
    return slice(g * HEAD_PAD, (g + 1) * HEAD_PAD)


def _dot_nt(a, b):
    return lax.dot_general(a, b, _DIMS["nt"], preferred_element_type=F32)


def _dot_nn(a, b):
    return lax.dot_general(a, b, _DIMS["nn"], preferred_element_type=F32)


def _dot_tn(a, b):
    return lax.dot_general(a, b, _DIMS["tn"], preferred_element_type=F32)


def _flash_fwd(q, k, v):
    S = q.shape[0]
    t = _pick(S, (512, 256, 128))

    def body(q_ref, k_ref, v_ref, o_ref, qa_ref, m_ref, acc_ref):
        qi = pl.program_id(1)
        m_ref[...] = jnp.full_like(m_ref, NEG)
        acc_ref[...] = jnp.zeros_like(acc_ref)

        def step(first, tiles, diag):
            width = tiles * t
            rows = pl.ds(pl.multiple_of(first, t), width)
            for g in range(HEADS_PER_STEP):
                hl = _head_lanes(g)
                s = _dot_nt(q_ref[:, hl], k_ref[rows, hl])
                if diag:
                    row = lax.broadcasted_iota(jnp.int32, s.shape, 0)
                    col = lax.broadcasted_iota(jnp.int32, s.shape, 1)
                    s = jnp.where(col <= row + (tiles - 1) * t, s, NEG)
                m_old = m_ref[g]
                m_new = jnp.maximum(m_old, jnp.max(s, axis=-1, keepdims=True))
                p = jnp.exp(s - m_new).astype(BF16)
                acc_ref[g] = jnp.exp(m_old - m_new) * acc_ref[g] + _dot_nn(p, v_ref[rows, hl])
                m_ref[g] = m_new

        def wide(kb, carry):
            step(kb * (FWD_KEY_TILES * t), FWD_KEY_TILES, False)
            return carry

        full_groups = qi // FWD_KEY_TILES
        lax.fori_loop(0, full_groups, wide, 0)
        for tiles in range(1, min(FWD_KEY_TILES, S // t) + 1):
            @pl.when(qi - full_groups * FWD_KEY_TILES == tiles - 1)
            def _():
                step(full_groups * (FWD_KEY_TILES * t), tiles, True)

        for g in range(HEADS_PER_STEP):
            hl = _head_lanes(g)
            acc = acc_ref[g]
            l = -acc[:, STAT_COL_V:STAT_COL_V + 1]
            o_ref[:, hl] = (acc / l).astype(BF16)
            qa_ref[:, hl] = _put_stats(q_ref[:, hl], m_ref[g] + jnp.log(l), STAT_COL_QK)

    blk, head = _flash_specs(S, t)
    return pl.pallas_call(
        body, name="mla_flash_fwd", grid=(MLA_HEADS // HEADS_PER_STEP, S // t),
        in_specs=[blk, head, head], out_specs=[blk, blk],
        out_shape=[jax.ShapeDtypeStruct(q.shape, BF16), jax.ShapeDtypeStruct(q.shape, BF16)],
        scratch_shapes=[pltpu.VMEM((HEADS_PER_STEP, t, 1), F32), pltpu.VMEM((HEADS_PER_STEP, t, HEAD_PAD), F32)],
        compiler_params=_params(("parallel", "arbitrary")),
    )(q, k, v)


def _flash_bwd(qa, k, v, doa):
    S = qa.shape[0]
    t = _pick(S, (512, 256, 128))
    n = S // t

    def body(qa_ref, k_ref, v_ref, do_ref, dq_ref, dk_ref, dv_ref, dk_acc, dv_acc):
        kj = pl.program_id(1)

        @pl.when(kj == 0)
        def _():
            dq_ref[...] = jnp.zeros_like(dq_ref)

        dk_acc[...] = jnp.zeros_like(dk_acc)
        dv_acc[...] = jnp.zeros_like(dv_acc)

        def step(qi, diag):
            rows = pl.ds(pl.multiple_of(qi * t, t), t)
            for g in range(HEADS_PER_STEP):
                hl = _head_lanes(g)
                qa, do, kk = qa_ref[rows, hl], do_ref[rows, hl], k_ref[:, hl]
                st = _dot_nt(kk, qa)
                if diag:
                    st = jnp.where(_lower_tri(st.shape, False), st, NEG)
                pt = jnp.exp(st)
                dst = (pt * _dot_nt(v_ref[:, hl], do)).astype(BF16)
                dv_acc[:, hl] += _dot_nn(pt.astype(BF16), do)
                dk_acc[:, hl] += _dot_nn(dst, qa)
                dq_ref[rows, hl] += _dot_tn(dst, kk)

        def loop(qi, carry):
            step(qi, False)
            return carry

        step(kj, True)
        lax.fori_loop(kj + 1, n, loop, 0)
        dk_ref[...] = dk_acc[...]
        dv_ref[...] = dv_acc[...].astype(BF16)

    blk, head = _flash_specs(S, t)
    w = HEADS_PER_STEP * HEAD_PAD
    return pl.pallas_call(
        body, name="mla_flash_bwd", grid=(MLA_HEADS // HEADS_PER_STEP, n),
        in_specs=[head, blk, blk, head], out_specs=[head, blk, blk],
        out_shape=[jax.ShapeDtypeStruct(qa.shape, F32), jax.ShapeDtypeStruct(qa.shape, F32), jax.ShapeDtypeStruct(qa.shape, BF16)],
        scratch_shapes=[pltpu.VMEM((t, w), F32), pltpu.VMEM((t, w), F32)],
        compiler_params=_params(("parallel", "arbitrary")),
    )(qa, k, v, doa)


def _xattn_fwd(xq, kvx):
    W = X_HEADS * X_HEAD_DIM

    def body(q_ref, kv_ref, o_ref):
        for h in range(X_HEADS):
            lo = h * X_HEAD_DIM
            s = _dot_nt(q_ref[:, lo:lo + X_HEAD_DIM], kv_ref[:, lo:lo + X_HEAD_DIM])
            p = jnp.exp(s - jnp.max(s, axis=-1, keepdims=True))
            p = p / jnp.sum(p, axis=-1, keepdims=True)
            o_ref[:, lo:lo + X_HEAD_DIM] = _dot_nn(p.astype(BF16), kv_ref[:, W + lo:W + lo + X_HEAD_DIM]).astype(BF16)

    return _rows("xattn_fwd", body, [xq], [kvx], [(W, BF16)])[0]


def _xattn_bwd(xq, kvx, dox):
    W = X_HEADS * X_HEAD_DIM
    scale = X_HEAD_DIM ** -0.5

    def body(q_ref, do_ref, kv_ref, dq_ref, dkv_ref):
        @pl.when(pl.program_id(0) == 0)
        def _():
            dkv_ref[...] = jnp.zeros_like(dkv_ref)

        for h in range(X_HEADS):
            lo = h * X_HEAD_DIM
            q, k = q_ref[:, lo:lo + X_HEAD_DIM], kv_ref[:, lo:lo + X_HEAD_DIM]
            v, do = kv_ref[:, W + lo:W + lo + X_HEAD_DIM], do_ref[:, lo:lo + X_HEAD_DIM]
            s = _dot_nt(q, k)
            p = jnp.exp(s - jnp.max(s, axis=-1, keepdims=True))
            p = p / jnp.sum(p, axis=-1, keepdims=True)
            dp = _dot_nt(do, v)
            ds = (p * (dp - jnp.sum(dp * p, axis=-1, keepdims=True))).astype(BF16)
            dq_ref[:, lo:lo + X_HEAD_DIM] = (_dot_nn(ds, k) * scale).astype(BF16)
            dkv_ref[:, lo:lo + X_HEAD_DIM] += _dot_tn(ds, q)
            dkv_ref[:, W + lo:W + lo + X_HEAD_DIM] += _dot_tn(p.astype(BF16), do)

    return _rows("xattn_bwd", body, [xq, dox], [kvx], [(W, BF16)], [kvx.shape])


def _adamw(name, w, g, m, v):
    shape = w.shape
    w2, g2, m2, v2 = [a.reshape(-1, shape[-1]) for a in (w, g, m, v)]
    R, C = w2.shape
    t = _pick(R, (256, 128, 64, 32, 16, 8))
    c1 = 1.0 / (1.0 - ADAM_B1 ** ADAM_STEP)
    c2 = 1.0 / (1.0 - ADAM_B2 ** ADAM_STEP)

    def body(w_ref, g_ref, m_ref, v_ref, d_ref, nm_ref, nv_ref):
        g = g_ref[...]
        nm = ADAM_B1 * m_ref[...] + (1.0 - ADAM_B1) * g
        nv = ADAM_B2 * v_ref[...] + (1.0 - ADAM_B2) * (g * g)
        d_ref[...] = -ADAM_LR * ((nm * c1) / (jnp.sqrt(nv * c2) + ADAM_EPS) + ADAM_WD * w_ref[...])
        nm_ref[...] = nm
        nv_ref[...] = nv

    spec = pl.BlockSpec((t, C), lambda i: (i, 0))
    outs = pl.pallas_call(
        body, name=name, grid=(R // t,), in_specs=[spec] * 4, out_specs=[spec] * 3,
        out_shape=[jax.ShapeDtypeStruct((R, C), F32)] * 3, compiler_params=_params(("parallel",)),
    )(w2, g2, m2, v2)
    return [o.reshape(shape) for o in outs]


def _place():
    x, y, c = lax.axis_index("x"), lax.axis_index("y"), lax.axis_index("c")
    return x, y, c, [(1 - x, y), (x, 1 - y), (1 - x, 1 - y)]


_ANY = pl.BlockSpec(memory_space=pl.ANY)


_HBM = pl.BlockSpec(memory_space=pltpu.HBM)
_SEM = pl.BlockSpec(memory_space=pltpu.SEMAPHORE)
_SIDE_EFFECT = pltpu.SideEffectType.DATAFLOW_SIDE_EFFECTING


def _gather_copy(src, land, slot, send, recv, k, chip, c):
    return pltpu.make_async_remote_copy(src_ref=src, dst_ref=land.at[slot], send_sem=send.at[k], recv_sem=recv.at[k],
                                        device_id=(chip[0], chip[1], c), device_id_type=MESH)


def _gather_start(shards, groups):
    n, ng = len(shards), len(groups)
    lands = [jnp.broadcast_to(s[None], (N_CHIPS,) + s.shape) for s in shards]

    def body(*refs):
        ins, lnd = refs[:n], refs[n:2 * n]
        sends, recvs = refs[2 * n:2 * n + ng], refs[2 * n + ng:2 * n + 2 * ng]
        token = refs[-1]
        x, y, c, chips = _place()
        for gi, group in enumerate(groups):
            for pos, w in enumerate(group):
                for j, chip in enumerate(chips):
                    _gather_copy(ins[w], lnd[w], 2 * x + y, sends[gi], recvs[gi], 3 * pos + j, chip, c).start()
        token[...] = jnp.zeros_like(token)

    sems = [pltpu.SemaphoreType.DMA((3 * len(g),)) for g in groups]
    res = pl.pallas_call(
        body, name="gather_weights_start",
        out_shape=sems + sems + [pltpu.HBM(a.shape, a.dtype) for a in shards + lands] + [jax.ShapeDtypeStruct((8, LANES), F32)],
        in_specs=[_HBM] * (2 * n),
        out_specs=[_SEM] * (2 * ng) + [_HBM] * (2 * n) + [pl.BlockSpec(memory_space=pltpu.VMEM)],
        input_output_aliases={i: 2 * ng + i for i in range(2 * n)},
        compiler_params=pltpu.CompilerParams(has_side_effects=_SIDE_EFFECT),
    )(*[pltpu.with_memory_space_constraint(a, pltpu.HBM) for a in shards + lands])
    sem_pairs = list(zip(res[:ng], res[ng:2 * ng]))
    return sem_pairs, res[2 * ng:2 * ng + n], res[2 * ng + n:2 * ng + 2 * n], res[-1]


def _gather_wait(name, sem_pair, shards_thru, lands_thru, after):
    m = len(shards_thru)

    def body(*refs):
        ins, lnd = refs[:m], refs[m:2 * m]
        send, recv = refs[2 * m], refs[2 * m + 1]
        x, y, c, chips = _place()
        for pos in range(m):
            for j, chip in enumerate(chips):
                cp = _gather_copy(ins[pos], lnd[pos], 2 * chip[0] + chip[1], send, recv, 3 * pos + j, chip, c)
                cp.wait_send()
                cp.wait_recv()

    res = pl.pallas_call(
        body, name=name,
        out_shape=[pltpu.HBM(a.shape, a.dtype) for a in list(shards_thru) + list(lands_thru)],
        in_specs=[_HBM] * (2 * m) + [_SEM, _SEM, _ANY], out_specs=[_HBM] * (2 * m),
        input_output_aliases={i: i for i in range(2 * m)},
        compiler_params=pltpu.CompilerParams(has_side_effects=_SIDE_EFFECT),
    )(*shards_thru, *lands_thru, *sem_pair, after)
    return res[m:]


def _pair_exchange(name, packs):
    n = len(packs)

    def body(*refs):
        ins, outs, send, recv = refs[:n], refs[n:2 * n], refs[2 * n], refs[2 * n + 1]
        x, y, c, _ = _place()
        cps = []
        for g in range(n):
            cp = pltpu.make_async_remote_copy(src_ref=ins[g].at[:, pl.ds(1 - c, 1)], dst_ref=outs[g], send_sem=send.at[g],
                                              recv_sem=recv.at[g], device_id=(x, y, 1 - c), device_id_type=MESH)
            cp.start()
            cps.append(cp)
        for cp in cps:
            cp.wait()

    return pl.pallas_call(
        body, name=name, in_specs=[_ANY] * n, out_specs=[_ANY] * n,
        out_shape=[jax.ShapeDtypeStruct((N_CHIPS, 1) + p.shape[2:], p.dtype) for p in packs],
        scratch_shapes=[pltpu.SemaphoreType.DMA((n,)), pltpu.SemaphoreType.DMA((n,))],
    )(*packs)


def _chip_copy(src, land, src_slot, dst_slot, send, recv, k, chip, c):
    return pltpu.make_async_remote_copy(src_ref=src.at[src_slot], dst_ref=land.at[dst_slot], send_sem=send.at[k],
                                        recv_sem=recv.at[k], device_id=(chip[0], chip[1], c), device_id_type=MESH)


def _chip_exchange_start(name, parts):
    n = len(parts)
    lands = [lax.empty(p.shape, p.dtype) for p in parts]

    def body(*refs):
        ins, lnd, send, recv, token = refs[:n], refs[n:2 * n], refs[2 * n], refs[2 * n + 1], refs[-1]
        x, y, c, chips = _place()
        for g in range(n):
            for j, chip in enumerate(chips):
                _chip_copy(ins[g], lnd[g], 2 * chip[0] + chip[1], 2 * x + y, send, recv, 3 * g + j, chip, c).start()
        token[...] = jnp.zeros_like(token)

    sems = [pltpu.SemaphoreType.DMA((3 * n,))] * 2
    res = pl.pallas_call(
        body, name=name,
        out_shape=sems + [pltpu.HBM(a.shape, a.dtype) for a in list(parts) + lands] + [jax.ShapeDtypeStruct((8, LANES), F32)],
        in_specs=[_HBM] * (2 * n),
        out_specs=[_SEM] * 2 + [_HBM] * (2 * n) + [pl.BlockSpec(memory_space=pltpu.VMEM)],
        input_output_aliases={i: 2 + i for i in range(2 * n)},
        compiler_params=pltpu.CompilerParams(has_side_effects=_SIDE_EFFECT),
    )(*[pltpu.with_memory_space_constraint(a, pltpu.HBM) for a in list(parts) + lands])
    return res[:2], res[2:2 + n], res[2 + n:2 + 2 * n], res[-1]


def _chip_exchange_wait(name, sems, parts_thru, lands_thru, after):
    n = len(parts_thru)

    def body(*refs):
        ins, lnd, send, recv = refs[:n], refs[n:2 * n], refs[2 * n], refs[2 * n + 1]
        x, y, c, chips = _place()
        for g in range(n):
            for j, chip in enumerate(chips):
                cp = _chip_copy(ins[g], lnd[g], 2 * x + y, 2 * chip[0] + chip[1], send, recv, 3 * g + j, chip, c)
                cp.wait_send()
                cp.wait_recv()

    res = pl.pallas_call(
        body, name=name,
        out_shape=[pltpu.HBM(a.shape, a.dtype) for a in list(parts_thru) + list(lands_thru)],
        in_specs=[_HBM] * (2 * n) + [_SEM, _SEM, _ANY], out_specs=[_HBM] * (2 * n),
        input_output_aliases={i: i for i in range(2 * n)},
        compiler_params=pltpu.CompilerParams(has_side_effects=_SIDE_EFFECT),
    )(*parts_thru, *lands_thru, *sems, after)
    return res[:n], res[n:]


def _pair_share(halves):
    n = len(halves)

    def body(*refs):
        outs, send, recv = refs[n:2 * n], refs[2 * n], refs[2 * n + 1]
        x, y, c, _ = _place()
        cps = []
        for g in range(n):
            cp = pltpu.make_async_remote_copy(src_ref=outs[g].at[c], dst_ref=outs[g].at[c], send_sem=send.at[g],
                                              recv_sem=recv.at[g], device_id=(x, y, 1 - c), device_id_type=MESH)
            cp.start()
            cps.append(cp)
        for g in range(n):
            pltpu.make_async_remote_copy(src_ref=outs[g].at[c], dst_ref=outs[g].at[1 - c], send_sem=send.at[g],
                                         recv_sem=recv.at[g], device_id=(x, y, 1 - c), device_id_type=MESH).wait_recv()
        for cp in cps:
            cp.wait_send()

    return pl.pallas_call(
        body, name="grad_pair_share", in_specs=[_ANY] * n, out_specs=[_ANY] * n,
        out_shape=[jax.ShapeDtypeStruct(h.shape, h.dtype) for h in halves],
        input_output_aliases={g: g for g in range(n)},
        scratch_shapes=[pltpu.SemaphoreType.DMA((n,)), pltpu.SemaphoreType.DMA((n,))],
    )(*halves)


def _sum_over_devices(block):
    R, L = block.shape

    def gather(b_ref, o_ref, send, recv, loc):
        x, y, c, _ = _place()
        lc = pltpu.make_async_copy(b_ref, o_ref.at[4 * x + 2 * y + c], loc)
        lc.start()
        peers = [(1 - x if dx else x, 1 - y if dy else y, 1 - c if dc else c)
                 for dx in (0, 1) for dy in (0, 1) for dc in (0, 1) if dx or dy or dc]
        cps = []
        for j, peer in enumerate(peers):
            cp = pltpu.make_async_remote_copy(src_ref=b_ref, dst_ref=o_ref.at[4 * x + 2 * y + c], send_sem=send.at[j],
                                              recv_sem=recv.at[j], device_id=peer, device_id_type=MESH)
            cp.start()
            cps.append(cp)
        for j, (px, py, pc) in enumerate(peers):
            pltpu.make_async_remote_copy(src_ref=b_ref, dst_ref=o_ref.at[4 * px + 2 * py + pc], send_sem=send.at[j],
                                         recv_sem=recv.at[j], device_id=(px, py, pc), device_id_type=MESH).wait_recv()
        for cp in cps:
            cp.wait_send()
        lc.wait()

    blocks = pl.pallas_call(
        gather, name="small_grads_gather", in_specs=[_ANY], out_specs=_ANY,
        out_shape=jax.ShapeDtypeStruct((N_DEV, R, L), F32),
        scratch_shapes=[pltpu.SemaphoreType.DMA((N_DEV - 1,)), pltpu.SemaphoreType.DMA((N_DEV - 1,)), pltpu.SemaphoreType.DMA],
    )(block)

    def add(b_ref, o_ref):
        total = b_ref[0]
        for d in range(1, N_DEV):
            total = total + b_ref[d]
        o_ref[...] = total

    return pl.pallas_call(add, name="small_grads_add", out_shape=jax.ShapeDtypeStruct((R, L), F32))(blocks)


def _pair_add(name, pack, got):
    _, _, R, C = pack.shape
    t = _pick(R, ROW_TILES)
    c = lax.axis_index("c").astype(jnp.int32).reshape(1)

    def body(c_ref, p_ref, g_ref, o_ref):
        o_ref[...] = (p_ref[...].astype(F32) + g_ref[...].astype(F32)).astype(BF16)

    return pl.pallas_call(
        body, name=name,
        grid_spec=pltpu.PrefetchScalarGridSpec(
            num_scalar_prefetch=1, grid=(N_CHIPS, R // t),
            in_specs=[pl.BlockSpec((None, None, t, C), lambda k, i, c_ref: (k, c_ref[0], i, 0)),
                      pl.BlockSpec((None, None, t, C), lambda k, i, c_ref: (k, 0, i, 0))],
            out_specs=pl.BlockSpec((None, t, C), lambda k, i, c_ref: (k, i, 0))),
        out_shape=jax.ShapeDtypeStruct((N_CHIPS, R, C), BF16), compiler_params=_params(("parallel", "parallel")),
    )(c, pack, got)


def _chip_add(name, own, got):
    _, R, C = own.shape
    t = _pick(R, ROW_TILES)
    x, y, c, _ = _place()
    place = jnp.stack([c, 2 * x + y]).astype(jnp.int32)

    def body(place_ref, own_ref, g1_ref, g2_ref, g3_ref, o_ref):
        o_ref[...] = ((own_ref[...].astype(F32) + g1_ref[...].astype(F32)) + g2_ref[...].astype(F32)) + g3_ref[...].astype(F32)

    def other(d):
        return pl.BlockSpec((None, t, C), lambda i, place_ref: ((place_ref[1] + d) % N_CHIPS, i, 0))

    return pl.pallas_call(
        body, name=name,
        grid_spec=pltpu.PrefetchScalarGridSpec(
            num_scalar_prefetch=1, grid=(R // t,),
            in_specs=[pl.BlockSpec((None, t, C), lambda i, place_ref: (place_ref[1], i, 0)), other(1), other(2), other(3)],
            out_specs=pl.BlockSpec((None, t, C), lambda i, place_ref: (place_ref[0], i, 0))),
        out_shape=jax.ShapeDtypeStruct((2, R, C), F32), compiler_params=_params(("parallel",)),
    )(place, own, got, got, got)


_CUT = (2 * CONV_CH, 2 * CONV_CH + Q_LORA, 2 * CONV_CH + Q_LORA + KV_LORA, 2 * CONV_CH + Q_LORA + KV_LORA + QK_ROPE)
_KR_AT = _CUT[2] + QK_NOPE


def _pad_last(a, n):
    return jnp.pad(a, [(0, 0)] * (a.ndim - 1) + [(0, n - a.shape[-1])])


def _layout_w_in(w_in):
    kr = jnp.pad(w_in[:, _CUT[2]:_CUT[3]], ((0, 0), (QK_NOPE, HEAD_PAD - QK_NOPE - QK_ROPE)))
    return jnp.concatenate([w_in[:, :_CUT[2]], kr, w_in[:, _CUT[3]:]], axis=1)


def _layout_weights(w):
    out = dict(w)
    if "w_uq" in w:
        out["w_uq"] = _pad_last(w["w_uq"].reshape(Q_LORA, MLA_HEADS, QK_NOPE + QK_ROPE), HEAD_PAD).reshape(Q_LORA, MLA_HEADS * HEAD_PAD)
    if "w_ukv" in w:
        ukv = w["w_ukv"].reshape(KV_LORA, MLA_HEADS, QK_NOPE + V_DIM)
        uk = _pad_last(ukv[:, :, :QK_NOPE], HEAD_PAD).reshape(KV_LORA, MLA_HEADS * HEAD_PAD)
        uv = _pad_last(ukv[:, :, QK_NOPE:], HEAD_PAD).reshape(KV_LORA, MLA_HEADS * HEAD_PAD)
        out["w_ukv"] = jnp.concatenate([uk, uv], axis=1)
    if "w_mla_out" in w:
        mo = jnp.pad(w["w_mla_out"].reshape(MLA_HEADS, V_DIM, D_MODEL), ((0, 0), (0, HEAD_PAD - V_DIM), (0, 0)))
        out["w_mla_out"] = mo.reshape(MLA_HEADS * HEAD_PAD, D_MODEL)
    return out


def _unlayout_grads(g):
    out = dict(g)
    if "w_in" in g:
        gi = g["w_in"]
        out["w_in"] = jnp.concatenate([gi[:, :_CUT[2]], gi[:, _KR_AT:_KR_AT + QK_ROPE], gi[:, _CUT[2] + HEAD_PAD:]], axis=1)
    if "w_uq" in g:
        out["w_uq"] = g["w_uq"].reshape(Q_LORA, MLA_HEADS, HEAD_PAD)[:, :, :QK_NOPE + QK_ROPE].reshape(Q_LORA, -1)
    if "w_ukv" in g:
        gk = g["w_ukv"][:, :MLA_HEADS * HEAD_PAD].reshape(KV_LORA, MLA_HEADS, HEAD_PAD)[:, :, :QK_NOPE]
        gv = g["w_ukv"][:, MLA_HEADS * HEAD_PAD:].reshape(KV_LORA, MLA_HEADS, HEAD_PAD)[:, :, :V_DIM]
        out["w_ukv"] = jnp.concatenate([gk, gv], axis=2).reshape(KV_LORA, -1)
    if "w_mla_out" in g:
        out["w_mla_out"] = g["w_mla_out"].reshape(MLA_HEADS, HEAD_PAD, D_MODEL)[:, :V_DIM].reshape(MLA_HEADS * V_DIM, D_MODEL)
    return out


def _rope_tables(positions):
    half = QK_ROPE // 2
    inv_freq = ROPE_THETA ** (-jnp.arange(half, dtype=F32) / half)
    ang = positions.astype(F32).reshape(-1, 1) * inv_freq
    cos, sin = jnp.cos(ang), jnp.sin(ang)
    S = cos.shape[0]
    z16, z32, z64 = jnp.zeros((S, half), F32), jnp.zeros((S, QK_ROPE), F32), jnp.zeros((S, QK_NOPE), F32)
    c = jnp.concatenate([jnp.ones((S, QK_NOPE), F32), cos, cos, z32], axis=1)
    sa = jnp.concatenate([z64, -sin, z16, z32], axis=1)
    sb = jnp.concatenate([z64, z16, sin, z32], axis=1)
    return c, sa, sb


def _local_step(x, mem, positions, target, weight_fns, early_grads_fn, late_grads_fn, sm):
    S = x.shape[0]
    HW = MLA_HEADS * HEAD_PAD
    rope_c, rope_sa, rope_sb = _rope_tables(positions)
    qk_scale = (QK_NOPE + QK_ROPE) ** -0.5
    c0, c1, c2, c3 = _CUT[0], _CUT[1], _CUT[2], _CUT[2] + HEAD_PAD

    def k_rms1(x_ref, g_ref, u_ref):
        u_ref[...] = _rms_fwd(x_ref[...], g_ref[...]).astype(BF16)

    u1, = _rows("rms_mix", k_rms1, [x], [sm["norm_mix_g"]], [(D_MODEL, BF16)])
    w_in, conv_w = weight_fns[0](u1)

    def epi_glu(acc, xs, outs):
        a, gt = acc[:, 0:CONV_CH], acc[:, CONV_CH:2 * CONV_CH]
        outs[0][...] = acc[...]
        outs[1][...] = a * _sigmoid(gt)

    conv_in, z0 = _mm("proj_conv", u1, w_in[:, :c0], "nn", [(2 * CONV_CH, F32), (CONV_CH, F32)], epi_glu)
    c_q = _mm_plain("proj_cq", u1, w_in[:, c0:c1], "nn")
    c_kv = _mm_plain("proj_ckv", u1, w_in[:, c1:c2], "nn")
    kr_raw = _mm_plain("proj_krope", u1, w_in[:, c2:c3], "nn")

    def epi_sigmoid(acc, xs, outs):
        outs[0][...] = _sigmoid(acc[...])

    gates, = _mm("proj_gates", u1, w_in[:, c3:], "nn", [(2 * D_MODEL, F32)], epi_sigmoid)

    z1, z3 = _conv_fwd(z0, conv_w, sm["conv_b"], sm["conv_ln_g"], sm["conv_ln_b"])
    wl = weight_fns[1](z1)
    conv_out = _mm_plain("conv_out", z3, wl["w_conv_out"], "nn")

    def k_lora_norm(cq_ref, ckv_ref, gq_ref, gkv_ref, qn_ref, kvn_ref):
        qn_ref[...] = _rms_fwd(cq_ref[...], gq_ref[...]).astype(BF16)
        kvn_ref[...] = _rms_fwd(ckv_ref[...], gkv_ref[...]).astype(BF16)

    qn, kvn = _rows("lora_norm", k_lora_norm, [c_q, c_kv], [sm["q_norm_g"], sm["kv_norm_g"]],
                    [(Q_LORA, BF16), (KV_LORA, BF16)])

    def epi_q(acc, xs, outs):
        c, sa, sb = xs[0][...], xs[1][...], xs[2][...]
        for h in range(MLA_HEADS):
            lo = h * HEAD_PAD
            outs[0][:, lo:lo + HEAD_PAD] = (_rope(acc[:, lo:lo + HEAD_PAD], c, sa, sb, 1.0) * qk_scale).astype(BF16)

    q_att, = _mm("q_up", qn, wl["w_uq"], "nn", [(HW, BF16)], epi_q, row_x=[rope_c, rope_sa, rope_sb], tn=HW)

    def epi_kv(acc, xs, outs):
        kr = _rope(xs[0][...], xs[1][...], xs[2][...], xs[3][...], 1.0)
        kr = kr + _neg_ones(kr.shape, STAT_COL_QK)
        vconst = _neg_ones(kr.shape, STAT_COL_V)
        for h in range(MLA_HEADS):
            lo = h * HEAD_PAD
            outs[0][:, lo:lo + HEAD_PAD] = (acc[:, lo:lo + HEAD_PAD] + kr).astype(BF16)
            outs[1][:, lo:lo + HEAD_PAD] = (acc[:, HW + lo:HW + lo + HEAD_PAD] + vconst).astype(BF16)

    k_att, v_att = _mm("kv_up", kvn, wl["w_ukv"], "nn", [(HW, BF16), (HW, BF16)], epi_kv,
                       row_x=[kr_raw, rope_c, rope_sa, rope_sb], tn=2 * HW)

    o_att, q_aug = _flash_fwd(q_att, k_att, v_att)
    wl.update(weight_fns[2](o_att))
    mla_out = _mm_plain("mla_out", o_att, wl["w_mla_out"], "nn")

    def k_merge(g_ref, co_ref, mo_ref, out_ref):
        out_ref[...] = (g_ref[:, 0:D_MODEL] * co_ref[...] + g_ref[:, D_MODEL:] * mo_ref[...]).astype(BF16)

    merged, = _rows("merge", k_merge, [gates, conv_out, mla_out], [], [(D_MODEL, BF16)], tile=256)

    def epi_res_norm(acc, xs, outs):
        h = xs[0][...] + acc[...]
        outs[0][...] = h
        outs[1][...] = _rms_fwd(h, xs[1][...]).astype(BF16)

    h1, u2 = _mm("mix_out", merged, wl["w_out"], "nn", [(D_MODEL, F32), (D_MODEL, BF16)], epi_res_norm,
                 row_x=[x], vec_x=[sm["norm_xattn_g"]], tn=D_MODEL)

    xscale = X_HEAD_DIM ** -0.5

    def epi_scale(acc, xs, outs):
        outs[0][...] = (acc[...] * xscale).astype(BF16)

    xq, = _mm("xattn_q", u2, wl["w_xq"], "nn", [(X_HEADS * X_HEAD_DIM, BF16)], epi_scale)

    def k_mem_norm(m_ref, g_ref, o_ref):
        o_ref[...] = _rms_fwd(m_ref[...], g_ref[...]).astype(BF16)

    mem_n, = _rows("mem_norm", k_mem_norm, [mem], [sm["norm_mem_g"]], [(D_MODEL, BF16)])
    kvx = _mm_plain("xattn_kv", mem_n, wl["w_xkv"], "nn", dtype=BF16)
    ox = _xattn_fwd(xq, kvx)
    h2, u3 = _mm("xattn_out", ox, wl["w_xo"], "nn", [(D_MODEL, F32), (D_MODEL, BF16)], epi_res_norm,
                 row_x=[h1], vec_x=[sm["norm_mlp_g"]], tn=D_MODEL)

    def epi_relu2(acc, xs, outs):
        r = jnp.maximum(acc[...], 0.0)
        outs[0][...] = (r * r).astype(BF16)

    hid, = _mm("mlp_up", u3, wl["w_mlp1"], "nn", [(D_FF, BF16)], epi_relu2)

    def epi_res(acc, xs, outs):
        outs[0][...] = xs[0][...] + acc[...]

    h3, = _mm("mlp_down", hid, wl["w_mlp2"], "nn", [(D_MODEL, F32)], epi_res, row_x=[h2], tn=D_MODEL)

    def k_final(h_ref, t_ref, g_ref, dh_ref, dhb_ref, loss_ref, dg_ref):
        h, g = h_ref[...], g_ref[...]
        e = _rms_fwd(h, g) - t_ref[...]
        part = 0.5 * jnp.sum(jnp.mean(e * e, axis=-1, keepdims=True), axis=0, keepdims=True)
        _accum(loss_ref, jnp.broadcast_to(part, loss_ref.shape))
        dh, dg = _rms_bwd(h, g, e * (1.0 / D_MODEL))
        dh_ref[...] = dh
        dhb_ref[...] = dh.astype(BF16)
        _accum(dg_ref, dg)

    dh3, dh3b, loss, g_final = _rows("final_loss", k_final, [h3, target], [sm["final_norm_g"]],
                                     [(D_MODEL, F32), (D_MODEL, BF16)], [(1, LANES), (1, D_MODEL)])

    def epi_drelu2(acc, xs, outs):
        outs[0][...] = (acc[...] * (2.0 * jnp.sqrt(xs[0][...].astype(F32)))).astype(BF16)

    da1, = _mm("mlp_down_dx", dh3b, wl["w_mlp2"], "nt", [(D_FF, BF16)], epi_drelu2, tile_x=[hid])
    g_mlp2 = _mm_plain("mlp_down_dw", hid, dh3b, "tn")
    g_mlp1 = _mm_plain("mlp_up_dw", u3, da1, "tn")
    du3 = _mm_plain("mlp_up_dx", da1, wl["w_mlp1"], "nt")

    def k_norm_bwd(x_ref, du_ref, dres_ref, g_ref, dh_ref, dhb_ref, dg_ref):
        dx, dg = _rms_bwd(x_ref[...], g_ref[...], du_ref[...])
        dh = dres_ref[...] + dx
        dh_ref[...] = dh
        dhb_ref[...] = dh.astype(BF16)
        _accum(dg_ref, dg)

    def norm_bwd(name, xin, du, dres, g):
        return _rows(name, k_norm_bwd, [xin, du, dres], [g], [(D_MODEL, F32), (D_MODEL, BF16)], [(1, D_MODEL)], tile=256)

    dh2, dh2b, g_norm_mlp = norm_bwd("norm_mlp_bwd", h2, du3, dh3, sm["norm_mlp_g"])

    dox = _mm_plain("xattn_out_dx", dh2b, wl["w_xo"], "nt", dtype=BF16)
    g_xo = _mm_plain("xattn_out_dw", ox, dh2b, "tn")
    dxq, dkvx = _xattn_bwd(xq, kvx, dox)
    g_xq = _mm_plain("xattn_q_dw", u2, dxq, "tn")
    du2 = _mm_plain("xattn_q_dx", dxq, wl["w_xq"], "nt")
    g_xkv = _mm_plain("xattn_kv_dw", mem_n, dkvx, "tn")
    dmem_n = _mm_plain("xattn_kv_dx", dkvx, wl["w_xkv"], "nt")

    def k_mem_bwd(m_ref, d_ref, g_ref, dg_ref):
        _, dg = _rms_bwd(m_ref[...], g_ref[...], d_ref[...])
        _accum(dg_ref, dg)

    g_norm_mem, = _rows("mem_norm_bwd", k_mem_bwd, [mem, dmem_n], [sm["norm_mem_g"]], [], [(1, D_MODEL)])
    token = early_grads_fn(dict(w_mlp1=g_mlp1, w_mlp2=g_mlp2, w_xo=g_xo, w_xq=g_xq, w_xkv=g_xkv))
    dh1, dh1b, g_norm_xattn = norm_bwd("norm_xattn_bwd", h1, du2, dh2, sm["norm_xattn_g"] + token[0, 0])

    dmerged = _mm_plain("mix_out_dx", dh1b, wl["w_out"], "nt")
    g_out = _mm_plain("mix_out_dw", merged, dh1b, "tn")

    def k_merge_bwd(dm_ref, g_ref, co_ref, mo_ref, dco_ref, dmo_ref, dgl_ref):
        dm = dm_ref[...]
        g0, g1 = g_ref[:, 0:D_MODEL], g_ref[:, D_MODEL:]
        dco_ref[...] = (dm * g0).astype(BF16)
        dmo_ref[...] = (dm * g1).astype(BF16)
        dgl_ref[:, 0:D_MODEL] = (dm * co_ref[...] * g0 * (1.0 - g0)).astype(BF16)
        dgl_ref[:, D_MODEL:] = (dm * mo_ref[...] * g1 * (1.0 - g1)).astype(BF16)

    dconv_out, dmla_out, dgl = _rows("merge_bwd", k_merge_bwd, [dmerged, gates, conv_out, mla_out], [],
                                     [(D_MODEL, BF16), (D_MODEL, BF16), (2 * D_MODEL, BF16)], tile=256)

    def epi_do(acc, xs, outs):
        for h in range(MLA_HEADS):
            lo = h * HEAD_PAD
            do = acc[:, lo:lo + HEAD_PAD]
            delta = jnp.sum(do * xs[0][:, lo:lo + HEAD_PAD].astype(F32), axis=-1, keepdims=True)
            outs[0][:, lo:lo + HEAD_PAD] = _put_stats(do.astype(BF16), delta, STAT_COL_V)

    do_aug, = _mm("mla_out_dx", dmla_out, wl["w_mla_out"], "nt", [(HW, BF16)], epi_do, row_x=[o_att], tn=HW)
    g_mla_out = _mm_plain("mla_out_dw", o_att, dmla_out, "tn")
    dq_att, dk_att, dv_att = _flash_bwd(q_aug, k_att, v_att, do_aug)

    def k_rope_bwd(dq_ref, dk_ref, dv_ref, c_ref, sa_ref, sb_ref, dqr_ref, dkv_ref, dkr_ref):
        c, sa, sb = c_ref[...], sa_ref[...], sb_ref[...]
        lane = lax.broadcasted_iota(jnp.int32, c.shape, 1)
        nope = (lane < QK_NOPE).astype(F32)
        ropem = ((lane >= QK_NOPE) & (lane < QK_NOPE + QK_ROPE)).astype(F32)
        dkr = jnp.zeros(c.shape, F32)
        for h in range(MLA_HEADS):
            lo = h * HEAD_PAD
            dqr_ref[:, lo:lo + HEAD_PAD] = (_rope(dq_ref[:, lo:lo + HEAD_PAD], c, sa, sb, -1.0) * qk_scale).astype(BF16)
            dk = dk_ref[:, lo:lo + HEAD_PAD]
            dkv_ref[:, lo:lo + HEAD_PAD] = (dk * nope).astype(BF16)
            dkr = dkr + dk
        dkv_ref[:, HW:] = dv_ref[...]
        dkr_ref[...] = (_rope(dkr * ropem, c, sa, sb, -1.0) * ropem).astype(BF16)

    dq_raw, dkv_cat, dkr = _rows("rope_bwd", k_rope_bwd, [dq_att, dk_att, dv_att, rope_c, rope_sa, rope_sb], [],
                                 [(HW, BF16), (2 * HW, BF16), (HEAD_PAD, BF16)], tile=256)
    g_uq = _mm_plain("q_up_dw", qn, dq_raw, "tn")
    dqn = _mm_plain("q_up_dx", dq_raw, wl["w_uq"], "nt")
    g_ukv = _mm_plain("kv_up_dw", kvn, dkv_cat, "tn")
    dkvn = _mm_plain("kv_up_dx", dkv_cat, wl["w_ukv"], "nt")

    def k_lora_bwd(cq_ref, ckv_ref, dqn_ref, dkvn_ref, gq_ref, gkv_ref, dcq_ref, dckv_ref, dgq_ref, dgkv_ref):
        dcq, dgq = _rms_bwd(cq_ref[...], gq_ref[...], dqn_ref[...])
        dckv, dgkv = _rms_bwd(ckv_ref[...], gkv_ref[...], dkvn_ref[...])
        dcq_ref[...] = dcq.astype(BF16)
        dckv_ref[...] = dckv.astype(BF16)
        _accum(dgq_ref, dgq)
        _accum(dgkv_ref, dgkv)

    dc_q, dc_kv, g_q_norm, g_kv_norm = _rows("lora_norm_bwd", k_lora_bwd, [c_q, c_kv, dqn, dkvn],
                                              [sm["q_norm_g"], sm["kv_norm_g"]], [(Q_LORA, BF16), (KV_LORA, BF16)],
                                              [(1, Q_LORA), (1, KV_LORA)])

    dz3 = _mm_plain("conv_out_dx", dconv_out, wl["w_conv_out"], "nt")
    g_conv_out = _mm_plain("conv_out_dw", z3, dconv_out, "tn")
    dz1, g_ln_g, g_ln_b, g_conv_b = _conv_bwd_norm(dz3, z1, sm["conv_ln_g"], sm["conv_ln_b"])
    dconv_in, g_conv_w = _conv_bwd_taps(dz1, z0, conv_in, conv_w)

    dproj = jnp.concatenate([dconv_in, dc_q, dc_kv, dkr, dgl], axis=1)
    g_in = _mm_plain("proj_dw", u1, dproj, "tn")
    token = late_grads_fn(dict(w_in=g_in, conv_w=g_conv_w[:CONV_WIDTH], w_conv_out=g_conv_out, w_uq=g_uq, w_ukv=g_ukv,
                               w_mla_out=g_mla_out, w_out=g_out))
    du1, = _mm("proj_dx", dproj, w_in, "nt", [(D_MODEL, F32)], _epi_store, vec_x=[token])

    def k_norm1_bwd(x_ref, du_ref, dres_ref, g_ref, dx_ref, dg_ref):
        dx, dg = _rms_bwd(x_ref[...], g_ref[...], du_ref[...])
        dx_ref[...] = dres_ref[...] + dx
        _accum(dg_ref, dg)

    grad_x, g_norm_mix = _rows("norm_mix_bwd", k_norm1_bwd, [x, du1, dh1], [sm["norm_mix_g"]], [(D_MODEL, F32)],
                               [(1, D_MODEL)], tile=256)

    small = dict(norm_mix_g=g_norm_mix, conv_b=g_conv_b, conv_ln_g=g_ln_g, conv_ln_b=g_ln_b, q_norm_g=g_q_norm,
                 kv_norm_g=g_kv_norm, norm_xattn_g=g_norm_xattn, norm_mem_g=g_norm_mem, norm_mlp_g=g_norm_mlp,
                 final_norm_g=g_final)
    return loss, grad_x, small


def _shard(a, k, axis):
    n = a.shape[axis] // N_CHIPS
    return lax.slice_in_dim(a, k * n, (k + 1) * n, axis=axis)


def _pack_small(grads, loss):
    flat = jnp.concatenate([grads[n].reshape(-1) for n in SMALL] + [loss.reshape(-1)[:1]])
    rows = -(-flat.shape[0] // (8 * LANES)) * 8
    return jnp.pad(flat, (0, rows * LANES - flat.shape[0])).reshape(rows, LANES)


def _pack_groups(shapes, names):
    groups = {}
    for n in names:
        groups.setdefault(shapes[n][1], []).append(n)
    return groups


def _pad_rows(a, mult):
    return jnp.pad(a, ((0, -a.shape[0] % mult), (0, 0)))


def _pack_grads(grads, shapes, names):
    packs = []
    for width, group in _pack_groups(shapes, names).items():
        per_chip = [jnp.concatenate([_pad_rows(_shard(grads[n], k, SHARD_AXIS[n]).astype(BF16), PACK_ROW_ALIGN) for n in group])
                    for k in range(N_CHIPS)]
        rows = per_chip[0].shape[0]
        packs.append(jnp.stack(per_chip).reshape(N_CHIPS, 2, rows // 2, width))
    return packs


def _unpack_grads(fulls, shapes, names):
    out = {}
    for full, group in zip(fulls, _pack_groups(shapes, names).values()):
        flat, at = full.reshape(-1, full.shape[-1]), 0
        for n in group:
            rows = shapes[n][0]
            out[n] = flat[at:at + rows]
            at += rows + (-rows % PACK_ROW_ALIGN)
    return out


def _unpack(flat, names, shapes):
    out, at = {}, 0
    for n in names:
        size = math.prod(shapes[n])
        out[n] = flat[at:at + size].reshape(shapes[n])
        at += size
    return out, at


def kernel(x, mem, positions, norm_mix_g, w_in, conv_w, conv_b, conv_ln_g, conv_ln_b, w_conv_out, q_norm_g, w_uq, kv_norm_g, w_ukv, w_mla_out, w_out, norm_xattn_g, norm_mem_g, w_xq, w_xkv, w_xo, norm_mlp_g, w_mlp1, w_mlp2, final_norm_g, loss_target, m_norm_mix_g, m_w_in, m_conv_w, m_conv_b, m_conv_ln_g, m_conv_ln_b, m_w_conv_out, m_q_norm_g, m_w_uq, m_kv_norm_g, m_w_ukv, m_w_mla_out, m_w_out, m_norm_xattn_g, m_norm_mem_g, m_w_xq, m_w_xkv, m_w_xo, m_norm_mlp_g, m_w_mlp1, m_w_mlp2, m_final_norm_g, v_norm_mix_g, v_w_in, v_conv_w, v_conv_b, v_conv_ln_g, v_conv_ln_b, v_w_conv_out, v_q_norm_g, v_w_uq, v_kv_norm_g, v_w_ukv, v_w_mla_out, v_w_out, v_norm_xattn_g, v_norm_mem_g, v_w_xq, v_w_xkv, v_w_xo, v_norm_mlp_g, v_w_mlp1, v_w_mlp2, v_final_norm_g):
    args = dict(locals())
    w = {n: args[n] for n in WEIGHTS}
    m = {n: args["m_" + n] for n in WEIGHTS}
    v = {n: args["v_" + n] for n in WEIGHTS}

    shards = [w[n][0].astype(F32 if n == "conv_w" else BF16) for n in BIG]
    bounds = (0,) + WEIGHT_WAITS + (len(BIG),)
    spans = [slice(lo, hi) for lo, hi in zip(bounds[:-1], bounds[1:])]
    sem_pairs, shards_thru, lands_thru, token = _gather_start(shards, [list(range(len(BIG)))[sp] for sp in spans])

    def unshard(n, g):
        ax = SHARD_AXIS[n]
        return jnp.moveaxis(g, 0, ax).reshape(g.shape[1:1 + ax] + (N_CHIPS * g.shape[1 + ax],) + g.shape[2 + ax:])

    def wait_fn(i):
        def fn(after):
            lands = _gather_wait(f"gather_weights_wait_{i}", sem_pairs[i], shards_thru[spans[i]], lands_thru[spans[i]], after)
            full = {n: unshard(n, g) for n, g in zip(BIG[spans[i]], lands)}
            return (_layout_w_in(full["w_in"]), full["conv_w"]) if i == 0 else _layout_weights(full)
        return fn

    sm = {n: w[n].reshape(1, -1) for n in SMALL}
    sm["norm_mix_g"] = sm["norm_mix_g"] + token[0, 0]

    shapes = {n: w[n].shape[1:] if n in BIG else w[n].shape for n in WEIGHTS}
    late_names = [n for n in BIG if n not in EARLY_GRADS]
    inflight = {}

    def send_grads(tag, names):
        def fn(g):
            packs = _pack_grads(_unlayout_grads(g), shapes, names)
            got = _pair_exchange(f"grad_pair_exchange_{tag}", packs)
            pairs = [_pair_add(f"grad_pair_add_{tag}_{i}", p, r) for i, (p, r) in enumerate(zip(packs, got))]
            *inflight[tag], token = _chip_exchange_start(f"grad_chip_exchange_{tag}_start", pairs)
            return token
        return fn

    loss, grad_x, g_small = _local_step(x[0], mem[0], positions, loss_target[0], [wait_fn(i) for i in range(3)],
                                        send_grads("early", EARLY_GRADS), send_grads("late", late_names), sm)

    halves, counts = [], {}
    for tag in ("late", "early"):
        own, got = _chip_exchange_wait(f"grad_chip_exchange_{tag}_wait", *inflight[tag], grad_x)
        halves += [_chip_add(f"grad_chip_add_{tag}_{i}", p, g) for i, (p, g) in enumerate(zip(own, got))]
        counts[tag] = len(own)
    fulls = _pair_share(halves)
    g_sum = _unpack_grads(fulls[:counts["late"]], shapes, late_names)
    g_sum.update(_unpack_grads(fulls[counts["late"]:], shapes, EARLY_GRADS))
    small_flat = _sum_over_devices(_pack_small(g_small, loss)).reshape(-1)
    g_small, at = _unpack(small_flat, SMALL, shapes)
    g_sum.update(g_small)
    loss_sum = small_flat[at]

    out_g, out_d, out_m, out_v = [], [], [], []
    for n in WEIGHTS:
        g = g_sum[n].reshape(w[n].shape)
        d, nm, nv = _adamw("adamw_" + n, w[n], g, m[n], v[n])
        out_g.append(g)
        out_d.append(d)
        out_m.append(nm)
        out_v.append(nv)
    return (loss_sum, grad_x[None], *out_g, *out_d, *out_m, *out_v)
```

```python
import functools
import math

import jax
import jax.numpy as jnp
from jax import lax
from jax.experimental import pallas as pl
from jax.experimental.pallas import tpu as pltpu

F32 = jnp.float32
BF16 = jnp.bfloat16
MESH = pl.DeviceIdType.MESH

D_MODEL = 1024
CONV_CH = 512
CONV_WIDTH = 31
MLA_HEADS = 8
QK_NOPE = 64
QK_ROPE = 32
V_DIM = 64
Q_LORA = 384
KV_LORA = 256
MEM_LEN = 256
X_HEADS = 4
X_HEAD_DIM = 128
D_FF = 4096
ROPE_THETA = 10000.0
EPS = 1e-6
HEAD_PAD = 128
STAT_COL_QK = QK_NOPE + QK_ROPE
STAT_COL_V = V_DIM
HALO = 32
N_CHIPS = 4
LANES = 128

ADAM_LR = 0.001
ADAM_B1 = 0.9
ADAM_B2 = 0.999
ADAM_EPS = 1e-08
ADAM_WD = 0.01
ADAM_STEP = 10

VMEM_LIMIT = 52 * 1024 * 1024
ROW_TILES = (1024, 512, 256, 128, 64, 32, 16)
PACK_ROW_ALIGN = 32
N_DEV = 8
NEG = -1e30

BIG = ["w_in", "conv_w", "w_conv_out", "w_uq", "w_ukv", "w_mla_out", "w_out", "w_xq", "w_xkv", "w_xo", "w_mlp1", "w_mlp2"]
WEIGHT_WAITS = (2, 5)
SHARD_AXIS = {"w_in": 1, "w_conv_out": 1, "w_uq": 1, "w_ukv": 1, "w_mla_out": 1, "w_out": 0, "w_xq": 0, "w_xkv": 0,
              "w_xo": 1, "w_mlp1": 1, "w_mlp2": 0, "conv_w": 1}
EARLY_GRADS = ["w_mlp1", "w_mlp2", "w_xkv", "w_xq", "w_xo"]
SMALL = ["norm_mix_g", "conv_b", "conv_ln_g", "conv_ln_b", "q_norm_g", "kv_norm_g", "norm_xattn_g", "norm_mem_g",
         "norm_mlp_g", "final_norm_g"]
WEIGHTS = ["norm_mix_g", "w_in", "conv_w", "conv_b", "conv_ln_g", "conv_ln_b", "w_conv_out", "q_norm_g", "w_uq",
           "kv_norm_g", "w_ukv", "w_mla_out", "w_out", "norm_xattn_g", "norm_mem_g", "w_xq", "w_xkv", "w_xo",
           "norm_mlp_g", "w_mlp1", "w_mlp2", "final_norm_g"]


def _pick(n, prefs):
    for p in prefs:
        if n % p == 0:
            return p
    return n


def _params(sem):
    return pltpu.CompilerParams(dimension_semantics=sem, vmem_limit_bytes=VMEM_LIMIT)


_DIMS = {"nn": (((1,), (0,)), ((), ())), "nt": (((1,), (1,)), ((), ())), "tn": (((0,), (0,)), ((), ()))}


def _mm(name, a, b, mode, outs, epi, row_x=(), tile_x=(), vec_x=(), sums=(), tm=None, tn=None, tk=None):
    if mode == "nn":
        (M, K), (_, N) = a.shape, b.shape
    elif mode == "nt":
        (M, K), (N, _) = a.shape, b.shape
    else:
        (K, M), (_, N) = a.shape, b.shape
    tm = tm or _pick(M, (1024, 512, 384, 256, 128))
    tn = tn or _pick(N, (1024, 768, 512, 384, 256, 128))
    tk = tk or _pick(K, (2048, 1920, 1024, 768, 512, 384, 256, 128))
    nk = K // tk
    rows_inner = nk == 1 and N // tn > 1
    grid = (N // tn, M // tm, nk) if rows_inner else (M // tm, N // tn, nk)

    def spec(shape, f):
        return pl.BlockSpec(shape, (lambda j, i, k: f(i, j, k)) if rows_inner else f)

    a_spec = spec((tk, tm), lambda i, j, k: (k, i)) if mode == "tn" else spec((tm, tk), lambda i, j, k: (i, k))
    b_spec = spec((tn, tk), lambda i, j, k: (j, k)) if mode == "nt" else spec((tk, tn), lambda i, j, k: (k, j))
    in_specs = [a_spec, b_spec]
    in_specs += [spec((tm, r.shape[1]), lambda i, j, k: (i, 0)) for r in row_x]
    in_specs += [spec((tm, tn), lambda i, j, k: (i, j)) for _ in tile_x]
    in_specs += [spec(v.shape, lambda i, j, k: (0, 0)) for v in vec_x]
    out_specs, out_shape = [], []
    for w, dt in outs:
        if tn == N:
            out_specs.append(spec((tm, w), lambda i, j, k: (i, 0)))
        else:
            assert w == N, (name, w, N)
            out_specs.append(spec((tm, tn), lambda i, j, k: (i, j)))
        out_shape.append(jax.ShapeDtypeStruct((M, w), dt))
    for shp in sums:
        assert tn == N and not rows_inner, name
        out_specs.append(spec(shp, lambda i, j, k: (0, 0)))
        out_shape.append(jax.ShapeDtypeStruct(shp, F32))
    nx = len(row_x) + len(tile_x) + len(vec_x)
    dims = _DIMS[mode]

    def body(a_ref, b_ref, *rest):
        x_refs, out_refs, acc_ref = rest[:nx], rest[nx:nx + len(outs) + len(sums)], rest[-1]
        av, bv = a_ref[...], b_ref[...]
        if av.dtype != BF16:
            av = av.astype(BF16)
        if bv.dtype != BF16:
            bv = bv.astype(BF16)
        prod = lax.dot_general(av, bv, dims, preferred_element_type=F32)
        if nk == 1:
            acc_ref[...] = prod
            epi(acc_ref, x_refs, out_refs)
        else:
            k = pl.program_id(2)

            @pl.when(k == 0)
            def _():
                acc_ref[...] = prod

            @pl.when(k > 0)
            def _():
                acc_ref[...] += prod

            @pl.when(k == nk - 1)
            def _():
                epi(acc_ref, x_refs, out_refs)

    res = pl.pallas_call(
        body, name=name, grid=grid, in_specs=in_specs, out_specs=out_specs, out_shape=out_shape,
        scratch_shapes=[pltpu.VMEM((tm, tn), F32)],
        compiler_params=_params(("arbitrary",) * 3 if sums else ("parallel", "parallel", "arbitrary")),
    )(a, b, *row_x, *tile_x, *vec_x)
    return res


def _epi_store(acc_ref, x_refs, out_refs):
    for o in out_refs:
        o[...] = acc_ref[...].astype(o.dtype)


def _mm_plain(name, a, b, mode, dtype=F32, **kw):
    n = b.shape[0] if mode == "nt" else b.shape[1]
    return _mm(name, a, b, mode, [(n, dtype)], _epi_store, **kw)[0]


def _rows(name, body, row_ins, vec_ins, row_outs, acc_outs=(), tile=512):
    S = row_ins[0].shape[0]
    t = _pick(S, (tile, 256, 128, 64, 32, 16, 8))
    in_specs = [pl.BlockSpec((t, r.shape[1]), lambda i: (i, 0)) for r in row_ins]
    in_specs += [pl.BlockSpec(v.shape, lambda i: (0, 0)) for v in vec_ins]
    out_specs = [pl.BlockSpec((t, w), lambda i: (i, 0)) for w, _ in row_outs]
    out_specs += [pl.BlockSpec(shp, lambda i: (0, 0)) for shp in acc_outs]
    out_shape = [jax.ShapeDtypeStruct((S, w), dt) for w, dt in row_outs]
    out_shape += [jax.ShapeDtypeStruct(shp, F32) for shp in acc_outs]
    sem = ("arbitrary",) if acc_outs else ("parallel",)
    return pl.pallas_call(
        functools.partial(body), name=name, grid=(S // t,), in_specs=in_specs, out_specs=out_specs,
        out_shape=out_shape, compiler_params=_params(sem),
    )(*row_ins, *vec_ins)


def _accum(ref, val, first=True):
    if first:
        @pl.when(pl.program_id(0) == 0)
        def _():
            ref[...] = jnp.zeros_like(ref)

    ref[...] += val


EPILOGUE_ROWS = 256


def _row_chunks(n):
    step = min(EPILOGUE_ROWS, n)
    return [slice(r, r + step) for r in range(0, n, step)]


def _colsum(v):
    return jnp.sum(v, axis=0, keepdims=True)


def _rms_fwd(x, g):
    r = lax.rsqrt(jnp.mean(x * x, axis=-1, keepdims=True) + EPS)
    return x * r * g


def _rms_bwd(x, g, du):
    r = lax.rsqrt(jnp.mean(x * x, axis=-1, keepdims=True) + EPS)
    xn = x * r
    gdu = du * g
    dx = r * (gdu - xn * jnp.mean(xn * gdu, axis=-1, keepdims=True))
    return dx, _colsum(du * xn)


def _sigmoid(v):
    return 1.0 / (1.0 + jnp.exp(-v))


def _rope(v, c, sa, sb, sign):
    return v * c + sign * (pltpu.roll(v, HEAD_PAD - QK_ROPE // 2, 1) * sa + pltpu.roll(v, QK_ROPE // 2, 1) * sb)


def _split3(v):
    hi = v.astype(BF16)
    r1 = v - hi.astype(F32)
    mid = r1.astype(BF16)
    lo = (r1 - mid.astype(F32)).astype(BF16)
    return hi, mid, lo


def _put_stats(base, stat, col):
    hi, mid, lo = _split3(stat)
    lane = lax.broadcasted_iota(jnp.int32, base.shape, 1)
    out = jnp.where(lane == col, hi, base)
    out = jnp.where(lane == col + 1, mid, out)
    return jnp.where(lane == col + 2, lo, out)


def _neg_ones(shape, col):
    lane = lax.broadcasted_iota(jnp.int32, shape, 1)
    return jnp.where((lane >= col) & (lane < col + 3), -1.0, 0.0).astype(F32)


def _shifted(ext, t):
    p = ext.shape[0]
    for b in range(8):
        rb = ext if b == 0 else pltpu.roll(ext, p - b, 0)
        for a in range(HALO // 8 + 1):
            if 8 * a + b <= HALO:
                yield 8 * a + b, rb[8 * a:8 * a + t]


def _conv_fwd(z0, conv_w, conv_b, ln_g, ln_b):
    S, C = z0.shape
    t = _pick(S, (512, 256, 128, 64, 32))
    per = t // HALO

    def body(cur_ref, prev_ref, w_ref, b_ref, g_ref, beta_ref, z1_ref, z3_ref, ext_ref):
        i = pl.program_id(0)
        ext_ref[0:HALO, :] = jnp.where(i > 0, prev_ref[...], 0.0)
        ext_ref[HALO:, :] = cur_ref[...]
        ext = ext_ref[...]
        acc = jnp.zeros((t, C), F32)
        for d, win in _shifted(ext, t):
            k = d - (HALO - CONV_WIDTH + 1)
            if 0 <= k < CONV_WIDTH:
                acc = acc + win * w_ref[k:k + 1, :]
        z1 = acc + b_ref[...]
        z1_ref[...] = z1
        mu = jnp.mean(z1, axis=-1, keepdims=True)
        zc = z1 - mu
        rs = lax.rsqrt(jnp.mean(zc * zc, axis=-1, keepdims=True) + EPS)
        z2 = zc * rs * g_ref[...] + beta_ref[...]
        z3_ref[...] = (z2 * _sigmoid(z2)).astype(BF16)

    vec = lambda v: pl.BlockSpec(v.shape, lambda i: (0, 0))
    return pl.pallas_call(
        body, name="conv_fwd", grid=(S // t,),
        in_specs=[pl.BlockSpec((t, C), lambda i: (i, 0)),
                  pl.BlockSpec((HALO, C), lambda i: (jnp.maximum(i * per - 1, 0), 0)),
                  vec(conv_w), vec(conv_b), vec(ln_g), vec(ln_b)],
        out_specs=[pl.BlockSpec((t, C), lambda i: (i, 0)), pl.BlockSpec((t, C), lambda i: (i, 0))],
        out_shape=[jax.ShapeDtypeStruct((S, C), F32), jax.ShapeDtypeStruct((S, C), BF16)],
        scratch_shapes=[pltpu.VMEM((t + HALO, C), F32)],
        compiler_params=_params(("parallel",)),
    )(z0, z0, conv_w, conv_b, ln_g, ln_b)


def _conv_bwd_norm(dz3, z1, ln_g, ln_b):
    C = z1.shape[1]

    def body(dz3_ref, z1_ref, g_ref, beta_ref, dz1_ref, dg_ref, dbeta_ref, dbias_ref):
        z1 = z1_ref[...]
        mu = jnp.mean(z1, axis=-1, keepdims=True)
        zc = z1 - mu
        rs = lax.rsqrt(jnp.mean(zc * zc, axis=-1, keepdims=True) + EPS)
        xh = zc * rs
        z2 = xh * g_ref[...] + beta_ref[...]
        sg = _sigmoid(z2)
        dz2 = dz3_ref[...] * (sg * (1.0 + z2 * (1.0 - sg)))
        dxh = dz2 * g_ref[...]
        dz1 = rs * (dxh - jnp.mean(dxh, axis=-1, keepdims=True) - xh * jnp.mean(dxh * xh, axis=-1, keepdims=True))
        dz1_ref[...] = dz1
        _accum(dg_ref, _colsum(dz2 * xh))
        _accum(dbeta_ref, _colsum(dz2))
        _accum(dbias_ref, _colsum(dz1))

    return _rows("conv_bwd_norm", body, [dz3, z1], [ln_g, ln_b], [(C, F32)], [(1, C)] * 3)


def _conv_bwd_taps(dz1, z0, conv_in, conv_w):
    S, C = z0.shape
    t = _pick(S, (512, 256, 128, 64, 32))
    per = t // HALO
    last = S // HALO - 1
    nt = S // t

    def body(dcur_ref, dnext_ref, zcur_ref, zprev_ref, cin_ref, w_ref, dcin_ref, dw_ref, dext_ref, zext_ref):
        i = pl.program_id(0)
        dcur = dcur_ref[...]
        dext_ref[0:t, :] = dcur
        dext_ref[t:, :] = jnp.where(i < nt - 1, dnext_ref[...], 0.0)
        zext_ref[0:HALO, :] = jnp.where(i > 0, zprev_ref[...], 0.0)
        zext_ref[HALO:, :] = zcur_ref[...]

        @pl.when(i == 0)
        def _():
            dw_ref[...] = jnp.zeros_like(dw_ref)

        dz0 = jnp.zeros((t, C), F32)
        for d, win in _shifted(dext_ref[...], t):
            k = CONV_WIDTH - 1 - d
            if 0 <= k < CONV_WIDTH:
                dz0 = dz0 + win * w_ref[k:k + 1, :]
        for d, win in _shifted(zext_ref[...], t):
            k = d - (HALO - CONV_WIDTH + 1)
            if 0 <= k < CONV_WIDTH:
                dw_ref[k:k + 1, :] += _colsum(dcur * win)
        a = cin_ref[:, 0:C]
        sg = _sigmoid(cin_ref[:, C:2 * C])
        dcin_ref[:, 0:C] = (dz0 * sg).astype(BF16)
        dcin_ref[:, C:2 * C] = (dz0 * a * sg * (1.0 - sg)).astype(BF16)

    return pl.pallas_call(
        body, name="conv_bwd_taps", grid=(nt,),
        in_specs=[pl.BlockSpec((t, C), lambda i: (i, 0)),
                  pl.BlockSpec((HALO, C), lambda i: (jnp.minimum((i + 1) * per, last), 0)),
                  pl.BlockSpec((t, C), lambda i: (i, 0)),
                  pl.BlockSpec((HALO, C), lambda i: (jnp.maximum(i * per - 1, 0), 0)),
                  pl.BlockSpec((t, 2 * C), lambda i: (i, 0)),
                  pl.BlockSpec(conv_w.shape, lambda i: (0, 0))],
        out_specs=[pl.BlockSpec((t, 2 * C), lambda i: (i, 0)), pl.BlockSpec((HALO, C), lambda i: (0, 0))],
        out_shape=[jax.ShapeDtypeStruct((S, 2 * C), BF16), jax.ShapeDtypeStruct((HALO, C), F32)],
        scratch_shapes=[pltpu.VMEM((t + HALO, C), F32), pltpu.VMEM((t + HALO, C), F32)],
        compiler_params=_params(("arbitrary",)),
    )(dz1, dz1, z0, z0, conv_in, conv_w)


def _lower_tri(shape, rows_are_queries):
    row = lax.broadcasted_iota(jnp.int32, shape, 0)
    col = lax.broadcasted_iota(jnp.int32, shape, 1)
    return (col <= row) if rows_are_queries else (row <= col)


HEADS_PER_STEP = 2
FWD_KEY_TILES = 4


def _flash_specs(S, t):
    w = HEADS_PER_STEP * HEAD_PAD
    blk = pl.BlockSpec((t, w), lambda h, i: (i, h))
    head = pl.BlockSpec((S, w), lambda h, i: (0, h))
    return blk, head


def _head_lanes(g):
    return slice(g * HEAD_PAD, (g + 1) * HEAD_PAD)


def _dot_nt(a, b):
    return lax.dot_general(a, b, _DIMS["nt"], preferred_element_type=F32)


def _dot_nn(a, b):
    return lax.dot_general(a, b, _DIMS["nn"], preferred_element_type=F32)


def _dot_tn(a, b):
    return lax.dot_general(a, b, _DIMS["tn"], preferred_element_type=F32)


def _flash_fwd(q, k, v):
    S = q.shape[0]
    t = _pick(S, (512, 256, 128))

    def body(q_ref, k_ref, v_ref, o_ref, qa_ref, m_ref, acc_ref):
        qi = pl.program_id(1)
        m_ref[...] = jnp.full_like(m_ref, NEG)
        acc_ref[...] = jnp.zeros_like(acc_ref)

        def step(first, tiles, diag):
            width = tiles * t
            rows = pl.ds(pl.multiple_of(first, t), width)
            for g in range(HEADS_PER_STEP):
                hl = _head_lanes(g)
                s = _dot_nt(q_ref[:, hl], k_ref[rows, hl])
                if diag:
                    row = lax.broadcasted_iota(jnp.int32, s.shape, 0)
                    col = lax.broadcasted_iota(jnp.int32, s.shape, 1)
                    s = jnp.where(col <= row + (tiles - 1) * t, s, NEG)
                m_old = m_ref[g]
                m_new = jnp.maximum(m_old, jnp.max(s, axis=-1, keepdims=True))
                p = jnp.exp(s - m_new).astype(BF16)
                acc_ref[g] = jnp.exp(m_old - m_new) * acc_ref[g] + _dot_nn(p, v_ref[rows, hl])
                m_ref[g] = m_new

        def wide(kb, carry):
            step(kb * (FWD_KEY_TILES * t), FWD_KEY_TILES, False)
            return carry

        full_groups = qi // FWD_KEY_TILES
        lax.fori_loop(0, full_groups, wide, 0)
        for tiles in range(1, min(FWD_KEY_TILES, S // t) + 1):
            @pl.when(qi - full_groups * FWD_KEY_TILES == tiles - 1)
            def _():
                step(full_groups * (FWD_KEY_TILES * t), tiles, True)

        for g in range(HEADS_PER_STEP):
            hl = _head_lanes(g)
            acc = acc_ref[g]
            l = -acc[:, STAT_COL_V:STAT_COL_V + 1]
            o_ref[:, hl] = (acc / l).astype(BF16)
            qa_ref[:, hl] = _put_stats(q_ref[:, hl], m_ref[g] + jnp.log(l), STAT_COL_QK)

    blk, head = _flash_specs(S, t)
    return pl.pallas_call(
        body, name="mla_flash_fwd", grid=(MLA_HEADS // HEADS_PER_STEP, S // t),
        in_specs=[blk, head, head], out_specs=[blk, blk],
        out_shape=[jax.ShapeDtypeStruct(q.shape, BF16), jax.ShapeDtypeStruct(q.shape, BF16)],
        scratch_shapes=[pltpu.VMEM((HEADS_PER_STEP, t, 1), F32), pltpu.VMEM((HEADS_PER_STEP, t, HEAD_PAD), F32)],
        compiler_params=_params(("parallel", "arbitrary")),
    )(q, k, v)


def _flash_bwd(qa, k, v, doa):
    S = qa.shape[0]
    t = _pick(S, (512, 256, 128))
    n = S // t

    def body(qa_ref, k_ref, v_ref, do_ref, dq_ref, dk_ref, dv_ref, dk_acc, dv_acc):
        kj = pl.program_id(1)

        @pl.when(kj == 0)
        def _():
            dq_ref[...] = jnp.zeros_like(dq_ref)

        dk_acc[...] = jnp.zeros_like(dk_acc)
        dv_acc[...] = jnp.zeros_like(dv_acc)

        def step(qi, diag):
            rows = pl.ds(pl.multiple_of(qi * t, t), t)
            for g in range(HEADS_PER_STEP):
                hl = _head_lanes(g)
                qa, do, kk = qa_ref[rows, hl], do_ref[rows, hl], k_ref[:, hl]
                st = _dot_nt(kk, qa)
                if diag:
                    st = jnp.where(_lower_tri(st.shape, False), st, NEG)
                pt = jnp.exp(st)
                dst = (pt * _dot_nt(v_ref[:, hl], do)).astype(BF16)
                dv_acc[:, hl] += _dot_nn(pt.astype(BF16), do)
                dk_acc[:, hl] += _dot_nn(dst, qa)
                dq_ref[rows, hl] += _dot_tn(dst, kk)

        def loop(qi, carry):
            step(qi, False)
            return carry

        step(kj, True)
        lax.fori_loop(kj + 1, n, loop, 0)
        dk_ref[...] = dk_acc[...]
        dv_ref[...] = dv_acc[...].astype(BF16)

    blk, head = _flash_specs(S, t)
    w = HEADS_PER_STEP * HEAD_PAD
    return pl.pallas_call(
        body, name="mla_flash_bwd", grid=(MLA_HEADS // HEADS_PER_STEP, n),
        in_specs=[head, blk, blk, head], out_specs=[head, blk, blk],
        out_shape=[jax.ShapeDtypeStruct(qa.shape, F32), jax.ShapeDtypeStruct(qa.shape, F32), jax.ShapeDtypeStruct(qa.shape, BF16)],
        scratch_shapes=[pltpu.VMEM((t, w), F32), pltpu.VMEM((t, w), F32)],
        compiler_params=_params(("parallel", "arbitrary")),
    )(qa, k, v, doa)


def _xattn_fwd(xq, kvx):
    W = X_HEADS * X_HEAD_DIM

    def body(q_ref, kv_ref, o_ref):
        for h in range(X_HEADS):
            lo = h * X_HEAD_DIM
            s = _dot_nt(q_ref[:, lo:lo + X_HEAD_DIM], kv_ref[:, lo:lo + X_HEAD_DIM])
            p = jnp.exp(s - jnp.max(s, axis=-1, keepdims=True))
            p = p / jnp.sum(p, axis=-1, keepdims=True)
            o_ref[:, lo:lo + X_HEAD_DIM] = _dot_nn(p.astype(BF16), kv_ref[:, W + lo:W + lo + X_HEAD_DIM]).astype(BF16)

    return _rows("xattn_fwd", body, [xq], [kvx], [(W, BF16)])[0]


def _xattn_bwd(xq, kvx, dox):
    W = X_HEADS * X_HEAD_DIM
    scale = X_HEAD_DIM ** -0.5

    def body(q_ref, do_ref, kv_ref, dq_ref, dkv_ref):
        @pl.when(pl.program_id(0) == 0)
        def _():
            dkv_ref[...] = jnp.zeros_like(dkv_ref)

        for h in range(X_HEADS):
            lo = h * X_HEAD_DIM
            q, k = q_ref[:, lo:lo + X_HEAD_DIM], kv_ref[:, lo:lo + X_HEAD_DIM]
            v, do = kv_ref[:, W + lo:W + lo + X_HEAD_DIM], do_ref[:, lo:lo + X_HEAD_DIM]
            s = _dot_nt(q, k)
            p = jnp.exp(s - jnp.max(s, axis=-1, keepdims=True))
            p = p / jnp.sum(p, axis=-1, keepdims=True)
            dp = _dot_nt(do, v)
            ds = (p * (dp - jnp.sum(dp * p, axis=-1, keepdims=True))).astype(BF16)
            dq_ref[:, lo:lo + X_HEAD_DIM] = (_dot_nn(ds, k) * scale).astype(BF16)
            dkv_ref[:, lo:lo + X_HEAD_DIM] += _dot_tn(ds, q)
            dkv_ref[:, W + lo:W + lo + X_HEAD_DIM] += _dot_tn(p.astype(BF16), do)

    return _rows("xattn_bwd", body, [xq, dox], [kvx], [(W, BF16)], [kvx.shape])


def _adamw(name, w, g, m, v):
    c1 = 1.0 / (1.0 - ADAM_B1 ** ADAM_STEP)
    c2 = 1.0 / (1.0 - ADAM_B2 ** ADAM_STEP)
    lead = (0,) * (w.ndim - 2)
    w2 = w.reshape((1,) * (2 - w.ndim) + w.shape) if w.ndim < 2 else w
    m2, v2 = m.reshape(w2.shape), v.reshape(w2.shape)
    g2 = g.reshape(w2.shape[-2:])
    R, C = g2.shape
    t = _pick(R, (256, 128, 64, 32, 16, 8))

    def body(w_ref, g_ref, m_ref, v_ref, go_ref, d_ref, nm_ref, nv_ref):
        g = g_ref[...]
        nm = ADAM_B1 * m_ref[lead] + (1.0 - ADAM_B1) * g
        nv = ADAM_B2 * v_ref[lead] + (1.0 - ADAM_B2) * (g * g)
        go_ref[lead] = g
        d_ref[lead] = -ADAM_LR * ((nm * c1) / (jnp.sqrt(nv * c2) + ADAM_EPS) + ADAM_WD * w_ref[lead])
        nm_ref[lead] = nm
        nv_ref[lead] = nv

    full = pl.BlockSpec((1,) * len(lead) + (t, C), lambda i: lead + (i, 0))
    outs = pl.pallas_call(
        body, name=name, grid=(R // t,), in_specs=[full, pl.BlockSpec((t, C), lambda i: (i, 0)), full, full],
        out_specs=[full] * 4, out_shape=[jax.ShapeDtypeStruct(w2.shape, F32)] * 4, compiler_params=_params(("parallel",)),
    )(w2, g2, m2, v2)
    return [o.reshape(w.shape) for o in outs]


def _place():
    x, y, c = lax.axis_index("x"), lax.axis_index("y"), lax.axis_index("c")
    return x, y, c, [(1 - x, y), (x, 1 - y), (1 - x, 1 - y)]


_ANY = pl.BlockSpec(memory_space=pl.ANY)


_HBM = pl.BlockSpec(memory_space=pltpu.HBM)
_SEM = pl.BlockSpec(memory_space=pltpu.SEMAPHORE)
_SIDE_EFFECT = pltpu.SideEffectType.DATAFLOW_SIDE_EFFECTING


def _gather_copy(src, land, slot, send, recv, k, chip, c):
    return pltpu.make_async_remote_copy(src_ref=src, dst_ref=land.at[slot], send_sem=send.at[k], recv_sem=recv.at[k],
                                        device_id=(chip[0], chip[1], c), device_id_type=MESH)


def _gather_start(shards, groups):
    n, ng = len(shards), len(groups)
    lands = [jnp.broadcast_to(s[None], (N_CHIPS,) + s.shape) for s in shards]

    def body(*refs):
        ins, lnd = refs[:n], refs[n:2 * n]
        sends, recvs = refs[2 * n:2 * n + ng], refs[2 * n + ng:2 * n + 2 * ng]
        token = refs[-1]
        x, y, c, chips = _place()
        for gi, group in enumerate(groups):
            for pos, w in enumerate(group):
                for j, chip in enumerate(chips):
                    _gather_copy(ins[w], lnd[w], 2 * x + y, sends[gi], recvs[gi], 3 * pos + j, chip, c).start()
        token[...] = jnp.zeros_like(token)

    sems = [pltpu.SemaphoreType.DMA((3 * len(g),)) for g in groups]
    res = pl.pallas_call(
        body, name="gather_weights_start",
        out_shape=sems + sems + [pltpu.HBM(a.shape, a.dtype) for a in shards + lands] + [jax.ShapeDtypeStruct((8, LANES), F32)],
        in_specs=[_HBM] * (2 * n),
        out_specs=[_SEM] * (2 * ng) + [_HBM] * (2 * n) + [pl.BlockSpec(memory_space=pltpu.VMEM)],
        input_output_aliases={i: 2 * ng + i for i in range(2 * n)},
        compiler_params=pltpu.CompilerParams(has_side_effects=_SIDE_EFFECT),
    )(*[pltpu.with_memory_space_constraint(a, pltpu.HBM) for a in shards + lands])
    sem_pairs = list(zip(res[:ng], res[ng:2 * ng]))
    return sem_pairs, res[2 * ng:2 * ng + n], res[2 * ng + n:2 * ng + 2 * n], res[-1]


def _gather_wait(name, sem_pair, shards_thru, lands_thru, after):
    m = len(shards_thru)

    def body(*refs):
        ins, lnd = refs[:m], refs[m:2 * m]
        send, recv = refs[2 * m], refs[2 * m + 1]
        x, y, c, chips = _place()
        for pos in range(m):
            for j, chip in enumerate(chips):
                cp = _gather_copy(ins[pos], lnd[pos], 2 * chip[0] + chip[1], send, recv, 3 * pos + j, chip, c)
                cp.wait_send()
                cp.wait_recv()

    res = pl.pallas_call(
        body, name=name,
        out_shape=[pltpu.HBM(a.shape, a.dtype) for a in list(shards_thru) + list(lands_thru)],
        in_specs=[_HBM] * (2 * m) + [_SEM, _SEM, _ANY], out_specs=[_HBM] * (2 * m),
        input_output_aliases={i: i for i in range(2 * m)},
        compiler_params=pltpu.CompilerParams(has_side_effects=_SIDE_EFFECT),
    )(*shards_thru, *lands_thru, *sem_pair, after)
    return res[m:]


def _pair_exchange(name, packs):
    n = len(packs)

    def body(*refs):
        ins, outs, send, recv = refs[:n], refs[n:2 * n], refs[2 * n], refs[2 * n + 1]
        x, y, c, _ = _place()
        cps = []
        for g in range(n):
            cp = pltpu.make_async_remote_copy(src_ref=ins[g].at[:, pl.ds(1 - c, 1)], dst_ref=outs[g], send_sem=send.at[g],
                                              recv_sem=recv.at[g], device_id=(x, y, 1 - c), device_id_type=MESH)
            cp.start()
            cps.append(cp)
        for cp in cps:
            cp.wait()

    return pl.pallas_call(
        body, name=name, in_specs=[_ANY] * n, out_specs=[_ANY] * n,
        out_shape=[jax.ShapeDtypeStruct((N_CHIPS, 1) + p.shape[2:], p.dtype) for p in packs],
        scratch_shapes=[pltpu.SemaphoreType.DMA((n,)), pltpu.SemaphoreType.DMA((n,))],
    )(*packs)


def _chip_copy(src, land, src_slot, dst_slot, send, recv, k, chip, c):
    return pltpu.make_async_remote_copy(src_ref=src.at[src_slot], dst_ref=land.at[dst_slot], send_sem=send.at[k],
                                        recv_sem=recv.at[k], device_id=(chip[0], chip[1], c), device_id_type=MESH)


def _chip_exchange_start(name, parts):
    n = len(parts)
    lands = [lax.empty(p.shape, p.dtype) for p in parts]

    def body(*refs):
        ins, lnd, send, recv, token = refs[:n], refs[n:2 * n], refs[2 * n], refs[2 * n + 1], refs[-1]
        x, y, c, chips = _place()
        for g in range(n):
            for j, chip in enumerate(chips):
                _chip_copy(ins[g], lnd[g], 2 * chip[0] + chip[1], 2 * x + y, send, recv, 3 * g + j, chip, c).start()
        token[...] = jnp.zeros_like(token)

    sems = [pltpu.SemaphoreType.DMA((3 * n,))] * 2
    res = pl.pallas_call(
        body, name=name,
        out_shape=sems + [pltpu.HBM(a.shape, a.dtype) for a in list(parts) + lands] + [jax.ShapeDtypeStruct((8, LANES), F32)],
        in_specs=[_HBM] * (2 * n),
        out_specs=[_SEM] * 2 + [_HBM] * (2 * n) + [pl.BlockSpec(memory_space=pltpu.VMEM)],
        input_output_aliases={i: 2 + i for i in range(2 * n)},
        compiler_params=pltpu.CompilerParams(has_side_effects=_SIDE_EFFECT),
    )(*[pltpu.with_memory_space_constraint(a, pltpu.HBM) for a in list(parts) + lands])
    return res[:2], res[2:2 + n], res[2 + n:2 + 2 * n], res[-1]


def _chip_exchange_wait(name, sems, parts_thru, lands_thru, after):
    n = len(parts_thru)

    def body(*refs):
        ins, lnd, send, recv = refs[:n], refs[n:2 * n], refs[2 * n], refs[2 * n + 1]
        x, y, c, chips = _place()
        for g in range(n):
            for j, chip in enumerate(chips):
                cp = _chip_copy(ins[g], lnd[g], 2 * x + y, 2 * chip[0] + chip[1], send, recv, 3 * g + j, chip, c)
                cp.wait_send()
                cp.wait_recv()

    res = pl.pallas_call(
        body, name=name,
        out_shape=[pltpu.HBM(a.shape, a.dtype) for a in list(parts_thru) + list(lands_thru)],
        in_specs=[_HBM] * (2 * n) + [_SEM, _SEM, _ANY], out_specs=[_HBM] * (2 * n),
        input_output_aliases={i: i for i in range(2 * n)},
        compiler_params=pltpu.CompilerParams(has_side_effects=_SIDE_EFFECT),
    )(*parts_thru, *lands_thru, *sems, after)
    return res[:n], res[n:]


def _pair_share(halves):
    n = len(halves)

    def body(*refs):
        outs, send, recv = refs[n:2 * n], refs[2 * n], refs[2 * n + 1]
        x, y, c, _ = _place()
        cps = []
        for g in range(n):
            cp = pltpu.make_async_remote_copy(src_ref=outs[g].at[c], dst_ref=outs[g].at[c], send_sem=send.at[g],
                                              recv_sem=recv.at[g], device_id=(x, y, 1 - c), device_id_type=MESH)
            cp.start()
            cps.append(cp)
        for g in range(n):
            pltpu.make_async_remote_copy(src_ref=outs[g].at[c], dst_ref=outs[g].at[1 - c], send_sem=send.at[g],
                                         recv_sem=recv.at[g], device_id=(x, y, 1 - c), device_id_type=MESH).wait_recv()
        for cp in cps:
            cp.wait_send()

    return pl.pallas_call(
        body, name="grad_pair_share", in_specs=[_ANY] * n, out_specs=[_ANY] * n,
        out_shape=[jax.ShapeDtypeStruct(h.shape, h.dtype) for h in halves],
        input_output_aliases={g: g for g in range(n)},
        scratch_shapes=[pltpu.SemaphoreType.DMA((n,)), pltpu.SemaphoreType.DMA((n,))],
    )(*halves)


def _sum_over_devices(block):
    R, L = block.shape

    def gather(b_ref, o_ref, send, recv, loc):
        x, y, c, _ = _place()
        lc = pltpu.make_async_copy(b_ref, o_ref.at[4 * x + 2 * y + c], loc)
        lc.start()
        peers = [(1 - x if dx else x, 1 - y if dy else y, 1 - c if dc else c)
                 for dx in (0, 1) for dy in (0, 1) for dc in (0, 1) if dx or dy or dc]
        cps = []
        for j, peer in enumerate(peers):
            cp = pltpu.make_async_remote_copy(src_ref=b_ref, dst_ref=o_ref.at[4 * x + 2 * y + c], send_sem=send.at[j],
                                              recv_sem=recv.at[j], device_id=peer, device_id_type=MESH)
            cp.start()
            cps.append(cp)
        for j, (px, py, pc) in enumerate(peers):
            pltpu.make_async_remote_copy(src_ref=b_ref, dst_ref=o_ref.at[4 * px + 2 * py + pc], send_sem=send.at[j],
                                         recv_sem=recv.at[j], device_id=(px, py, pc), device_id_type=MESH).wait_recv()
        for cp in cps:
            cp.wait_send()
        lc.wait()

    blocks = pl.pallas_call(
        gather, name="small_grads_gather", in_specs=[_ANY], out_specs=_ANY,
        out_shape=jax.ShapeDtypeStruct((N_DEV, R, L), F32),
        scratch_shapes=[pltpu.SemaphoreType.DMA((N_DEV - 1,)), pltpu.SemaphoreType.DMA((N_DEV - 1,)), pltpu.SemaphoreType.DMA],
    )(block)

    def add(b_ref, o_ref):
        total = b_ref[0]
        for d in range(1, N_DEV):
            total = total + b_ref[d]
        o_ref[...] = total

    return pl.pallas_call(add, name="small_grads_add", out_shape=jax.ShapeDtypeStruct((R, L), F32))(blocks)


def _pair_add(name, pack, got):
    _, _, R, C = pack.shape
    t = _pick(R, ROW_TILES)
    c = lax.axis_index("c").astype(jnp.int32).reshape(1)

    def body(c_ref, p_ref, g_ref, o_ref):
        o_ref[...] = (p_ref[...].astype(F32) + g_ref[...].astype(F32)).astype(BF16)

    return pl.pallas_call(
        body, name=name,
        grid_spec=pltpu.PrefetchScalarGridSpec(
            num_scalar_prefetch=1, grid=(N_CHIPS, R // t),
            in_specs=[pl.BlockSpec((None, None, t, C), lambda k, i, c_ref: (k, c_ref[0], i, 0)),
                      pl.BlockSpec((None, None, t, C), lambda k, i, c_ref: (k, 0, i, 0))],
            out_specs=pl.BlockSpec((None, t, C), lambda k, i, c_ref: (k, i, 0))),
        out_shape=jax.ShapeDtypeStruct((N_CHIPS, R, C), BF16), compiler_params=_params(("parallel", "parallel")),
    )(c, pack, got)


def _chip_add(name, own, got):
    _, R, C = own.shape
    t = _pick(R, ROW_TILES)
    x, y, c, _ = _place()
    place = jnp.stack([c, 2 * x + y]).astype(jnp.int32)

    def body(place_ref, own_ref, g1_ref, g2_ref, g3_ref, o_ref):
        o_ref[...] = ((own_ref[...].astype(F32) + g1_ref[...].astype(F32)) + g2_ref[...].astype(F32)) + g3_ref[...].astype(F32)

    def other(d):
        return pl.BlockSpec((None, t, C), lambda i, place_ref: ((place_ref[1] + d) % N_CHIPS, i, 0))

    return pl.pallas_call(
        body, name=name,
        grid_spec=pltpu.PrefetchScalarGridSpec(
            num_scalar_prefetch=1, grid=(R // t,),
            in_specs=[pl.BlockSpec((None, t, C), lambda i, place_ref: (place_ref[1], i, 0)), other(1), other(2), other(3)],
            out_specs=pl.BlockSpec((None, t, C), lambda i, place_ref: (place_ref[0], i, 0))),
        out_shape=jax.ShapeDtypeStruct((2, R, C), F32), compiler_params=_params(("parallel",)),
    )(place, own, got, got, got)


_CUT = (2 * CONV_CH, 2 * CONV_CH + Q_LORA, 2 * CONV_CH + Q_LORA + KV_LORA, 2 * CONV_CH + Q_LORA + KV_LORA + QK_ROPE)
_KR_AT = _CUT[2] + QK_NOPE


def _pad_last(a, n):
    return jnp.pad(a, [(0, 0)] * (a.ndim - 1) + [(0, n - a.shape[-1])])


def _layout_w_in(w_in):
    kr = jnp.pad(w_in[:, _CUT[2]:_CUT[3]], ((0, 0), (QK_NOPE, HEAD_PAD - QK_NOPE - QK_ROPE)))
    return jnp.concatenate([w_in[:, :_CUT[2]], kr, w_in[:, _CUT[3]:]], axis=1)


def _layout_weights(w):
    out = dict(w)
    if "w_uq" in w:
        out["w_uq"] = _pad_last(w["w_uq"].reshape(Q_LORA, MLA_HEADS, QK_NOPE + QK_ROPE), HEAD_PAD).reshape(Q_LORA, MLA_HEADS * HEAD_PAD)
    if "w_ukv" in w:
        ukv = w["w_ukv"].reshape(KV_LORA, MLA_HEADS, QK_NOPE + V_DIM)
        uk = _pad_last(ukv[:, :, :QK_NOPE], HEAD_PAD).reshape(KV_LORA, MLA_HEADS * HEAD_PAD)
        uv = _pad_last(ukv[:, :, QK_NOPE:], HEAD_PAD).reshape(KV_LORA, MLA_HEADS * HEAD_PAD)
        out["w_ukv"] = jnp.concatenate([uk, uv], axis=1)
    if "w_mla_out" in w:
        mo = jnp.pad(w["w_mla_out"].reshape(MLA_HEADS, V_DIM, D_MODEL), ((0, 0), (0, HEAD_PAD - V_DIM), (0, 0)))
        out["w_mla_out"] = mo.reshape(MLA_HEADS * HEAD_PAD, D_MODEL)
    return out


def _unlayout_grads(g):
    out = dict(g)
    if "w_in" in g:
        gi = g["w_in"]
        out["w_in"] = jnp.concatenate([gi[:, :_CUT[2]], gi[:, _KR_AT:_KR_AT + QK_ROPE], gi[:, _CUT[2] + HEAD_PAD:]], axis=1)
    if "w_uq" in g:
        out["w_uq"] = g["w_uq"].reshape(Q_LORA, MLA_HEADS, HEAD_PAD)[:, :, :QK_NOPE + QK_ROPE].reshape(Q_LORA, -1)
    if "w_ukv" in g:
        gk = g["w_ukv"][:, :MLA_HEADS * HEAD_PAD].reshape(KV_LORA, MLA_HEADS, HEAD_PAD)[:, :, :QK_NOPE]
        gv = g["w_ukv"][:, MLA_HEADS * HEAD_PAD:].reshape(KV_LORA, MLA_HEADS, HEAD_PAD)[:, :, :V_DIM]
        out["w_ukv"] = jnp.concatenate([gk, gv], axis=2).reshape(KV_LORA, -1)
    if "w_mla_out" in g:
        out["w_mla_out"] = g["w_mla_out"].reshape(MLA_HEADS, HEAD_PAD, D_MODEL)[:, :V_DIM].reshape(MLA_HEADS * V_DIM, D_MODEL)
    return out


def _rope_tables(positions):
    half = QK_ROPE // 2
    inv_freq = ROPE_THETA ** (-jnp.arange(half, dtype=F32) / half)
    ang = positions.astype(F32).reshape(-1, 1) * inv_freq
    cos, sin = jnp.cos(ang), jnp.sin(ang)
    S = cos.shape[0]
    z16, z32, z64 = jnp.zeros((S, half), F32), jnp.zeros((S, QK_ROPE), F32), jnp.zeros((S, QK_NOPE), F32)
    c = jnp.concatenate([jnp.ones((S, QK_NOPE), F32), cos, cos, z32], axis=1)
    sa = jnp.concatenate([z64, -sin, z16, z32], axis=1)
    sb = jnp.concatenate([z64, z16, sin, z32], axis=1)
    return c, sa, sb


def _local_step(x, mem, positions, target, weight_fns, early_grads_fn, late_grads_fn, sm):
    S = x.shape[0]
    HW = MLA_HEADS * HEAD_PAD
    rope_c, rope_sa, rope_sb = _rope_tables(positions)
    qk_scale = (QK_NOPE + QK_ROPE) ** -0.5
    c0, c1, c2, c3 = _CUT[0], _CUT[1], _CUT[2], _CUT[2] + HEAD_PAD

    def k_rms1(x_ref, g_ref, u_ref):
        u_ref[...] = _rms_fwd(x_ref[...], g_ref[...]).astype(BF16)

    u1, = _rows("rms_mix", k_rms1, [x], [sm["norm_mix_g"]], [(D_MODEL, BF16)])
    w_in, conv_w = weight_fns[0](u1)

    def epi_glu(acc, xs, outs):
        a, gt = acc[:, 0:CONV_CH], acc[:, CONV_CH:2 * CONV_CH]
        outs[0][...] = acc[...]
        outs[1][...] = a * _sigmoid(gt)

    conv_in, z0 = _mm("proj_conv", u1, w_in[:, :c0], "nn", [(2 * CONV_CH, F32), (CONV_CH, F32)], epi_glu)
    c_q = _mm_plain("proj_cq", u1, w_in[:, c0:c1], "nn")
    c_kv = _mm_plain("proj_ckv", u1, w_in[:, c1:c2], "nn")
    kr_raw = _mm_plain("proj_krope", u1, w_in[:, c2:c3], "nn")

    def epi_sigmoid(acc, xs, outs):
        outs[0][...] = _sigmoid(acc[...]).astype(BF16)

    gates, = _mm("proj_gates", u1, w_in[:, c3:], "nn", [(2 * D_MODEL, BF16)], epi_sigmoid)

    z1, z3 = _conv_fwd(z0, conv_w, sm["conv_b"], sm["conv_ln_g"], sm["conv_ln_b"])
    wl = weight_fns[1](z1)
    conv_out = _mm_plain("conv_out", z3, wl["w_conv_out"], "nn", dtype=BF16)

    def k_lora_norm(cq_ref, ckv_ref, gq_ref, gkv_ref, qn_ref, kvn_ref):
        qn_ref[...] = _rms_fwd(cq_ref[...], gq_ref[...]).astype(BF16)
        kvn_ref[...] = _rms_fwd(ckv_ref[...], gkv_ref[...]).astype(BF16)

    qn, kvn = _rows("lora_norm", k_lora_norm, [c_q, c_kv], [sm["q_norm_g"], sm["kv_norm_g"]],
                    [(Q_LORA, BF16), (KV_LORA, BF16)])

    def epi_q(acc, xs, outs):
        c, sa, sb = xs[0][...], xs[1][...], xs[2][...]
        for h in range(MLA_HEADS):
            lo = h * HEAD_PAD
            outs[0][:, lo:lo + HEAD_PAD] = (_rope(acc[:, lo:lo + HEAD_PAD], c, sa, sb, 1.0) * qk_scale).astype(BF16)

    q_att, = _mm("q_up", qn, wl["w_uq"], "nn", [(HW, BF16)], epi_q, row_x=[rope_c, rope_sa, rope_sb], tn=HW)

    def epi_kv(acc, xs, outs):
        kr = _rope(xs[0][...], xs[1][...], xs[2][...], xs[3][...], 1.0)
        kr = kr + _neg_ones(kr.shape, STAT_COL_QK)
        vconst = _neg_ones(kr.shape, STAT_COL_V)
        for h in range(MLA_HEADS):
            lo = h * HEAD_PAD
            outs[0][:, lo:lo + HEAD_PAD] = (acc[:, lo:lo + HEAD_PAD] + kr).astype(BF16)
            outs[1][:, lo:lo + HEAD_PAD] = (acc[:, HW + lo:HW + lo + HEAD_PAD] + vconst).astype(BF16)

    k_att, v_att = _mm("kv_up", kvn, wl["w_ukv"], "nn", [(HW, BF16), (HW, BF16)], epi_kv,
                       row_x=[kr_raw, rope_c, rope_sa, rope_sb], tn=2 * HW)

    o_att, q_aug = _flash_fwd(q_att, k_att, v_att)
    wl.update(weight_fns[2](o_att))
    mla_out = _mm_plain("mla_out", o_att, wl["w_mla_out"], "nn", dtype=BF16)

    def k_merge(g_ref, co_ref, mo_ref, out_ref):
        g0, g1 = g_ref[:, 0:D_MODEL].astype(F32), g_ref[:, D_MODEL:].astype(F32)
        out_ref[...] = (g0 * co_ref[...].astype(F32) + g1 * mo_ref[...].astype(F32)).astype(BF16)

    merged, = _rows("merge", k_merge, [gates, conv_out, mla_out], [], [(D_MODEL, BF16)], tile=256)

    def epi_res_norm(acc, xs, outs):
        h = xs[0][...] + acc[...]
        outs[0][...] = h
        outs[1][...] = _rms_fwd(h, xs[1][...]).astype(BF16)

    h1, u2 = _mm("mix_out", merged, wl["w_out"], "nn", [(D_MODEL, F32), (D_MODEL, BF16)], epi_res_norm,
                 row_x=[x], vec_x=[sm["norm_xattn_g"]], tn=D_MODEL)

    xscale = X_HEAD_DIM ** -0.5

    def epi_scale(acc, xs, outs):
        outs[0][...] = (acc[...] * xscale).astype(BF16)

    xq, = _mm("xattn_q", u2, wl["w_xq"], "nn", [(X_HEADS * X_HEAD_DIM, BF16)], epi_scale)

    def k_mem_norm(m_ref, g_ref, o_ref):
        o_ref[...] = _rms_fwd(m_ref[...], g_ref[...]).astype(BF16)

    mem_n, = _rows("mem_norm", k_mem_norm, [mem], [sm["norm_mem_g"]], [(D_MODEL, BF16)])
    kvx = _mm_plain("xattn_kv", mem_n, wl["w_xkv"], "nn", dtype=BF16)
    ox = _xattn_fwd(xq, kvx)
    h2, u3 = _mm("xattn_out", ox, wl["w_xo"], "nn", [(D_MODEL, F32), (D_MODEL, BF16)], epi_res_norm,
                 row_x=[h1], vec_x=[sm["norm_mlp_g"]], tn=D_MODEL)

    def epi_relu2(acc, xs, outs):
        r = jnp.maximum(acc[...], 0.0)
        outs[0][...] = (r * r).astype(BF16)

    hid, = _mm("mlp_up", u3, wl["w_mlp1"], "nn", [(D_FF, BF16)], epi_relu2)

    def epi_final(acc, xs, outs):
        g = xs[2][...]
        for rows in _row_chunks(acc.shape[0]):
            h = xs[0][rows, :] + acc[rows, :]
            e = _rms_fwd(h, g) - xs[1][rows, :]
            part = 0.5 * jnp.sum(jnp.mean(e * e, axis=-1, keepdims=True), axis=0, keepdims=True)
            dh, dg = _rms_bwd(h, g, e * (1.0 / D_MODEL))
            outs[0][rows, :] = dh
            outs[1][rows, :] = dh.astype(BF16)
            _accum(outs[2], jnp.broadcast_to(part, outs[2].shape), first=rows.start == 0)
            _accum(outs[3], dg, first=rows.start == 0)

    dh3, dh3b, loss, g_final = _mm("mlp_down_loss", hid, wl["w_mlp2"], "nn", [(D_MODEL, F32), (D_MODEL, BF16)], epi_final,
                                   row_x=[h2, target], vec_x=[sm["final_norm_g"]], sums=[(1, LANES), (1, D_MODEL)],
                                   tn=D_MODEL, tk=1024)

    def epi_drelu2(acc, xs, outs):
        outs[0][...] = (acc[...] * (2.0 * jnp.sqrt(xs[0][...].astype(F32)))).astype(BF16)

    da1, = _mm("mlp_down_dx", dh3b, wl["w_mlp2"], "nt", [(D_FF, BF16)], epi_drelu2, tile_x=[hid])
    g_mlp2 = _mm_plain("mlp_down_dw", hid, dh3b, "tn")
    g_mlp1 = _mm_plain("mlp_up_dw", u3, da1, "tn")

    def epi_norm_bwd(acc, xs, outs):
        for rows in _row_chunks(acc.shape[0]):
            dx, dg = _rms_bwd(xs[0][rows, :], xs[2][...], acc[rows, :])
            dh = xs[1][rows, :] + dx
            outs[0][rows, :] = dh
            if len(outs) == 3:
                outs[1][rows, :] = dh.astype(BF16)
            _accum(outs[-1], dg, first=rows.start == 0)

    def dx_norm_bwd(name, dy, w, xin, dres, vecs, with_bf16=True):
        outs = [(D_MODEL, F32), (D_MODEL, BF16)] if with_bf16 else [(D_MODEL, F32)]
        return _mm(name, dy, w, "nt", outs, epi_norm_bwd, row_x=[xin, dres], vec_x=vecs, sums=[(1, D_MODEL)],
                   tn=D_MODEL, tk=_pick(dy.shape[1], (1024, 768, 512)))

    dh2, dh2b, g_norm_mlp = dx_norm_bwd("mlp_up_dx_norm", da1, wl["w_mlp1"], h2, dh3, [sm["norm_mlp_g"]])

    dox = _mm_plain("xattn_out_dx", dh2b, wl["w_xo"], "nt", dtype=BF16)
    g_xo = _mm_plain("xattn_out_dw", ox, dh2b, "tn")
    dxq, dkvx = _xattn_bwd(xq, kvx, dox)
    g_xq = _mm_plain("xattn_q_dw", u2, dxq, "tn")
    g_xkv = _mm_plain("xattn_kv_dw", mem_n, dkvx, "tn")
    dmem_n = _mm_plain("xattn_kv_dx", dkvx, wl["w_xkv"], "nt")

    def k_mem_bwd(m_ref, d_ref, g_ref, dg_ref):
        _, dg = _rms_bwd(m_ref[...], g_ref[...], d_ref[...])
        _accum(dg_ref, dg)

    g_norm_mem, = _rows("mem_norm_bwd", k_mem_bwd, [mem, dmem_n], [sm["norm_mem_g"]], [], [(1, D_MODEL)])
    token = early_grads_fn(dict(w_mlp1=g_mlp1, w_mlp2=g_mlp2, w_xo=g_xo, w_xq=g_xq, w_xkv=g_xkv))
    dh1, dh1b, g_norm_xattn = dx_norm_bwd("xattn_q_dx_norm", dxq, wl["w_xq"], h1, dh2, [sm["norm_xattn_g"], token])

    dmerged = _mm_plain("mix_out_dx", dh1b, wl["w_out"], "nt")
    g_out = _mm_plain("mix_out_dw", merged, dh1b, "tn")

    def k_merge_bwd(dm_ref, g_ref, co_ref, mo_ref, dco_ref, dmo_ref, dgl_ref):
        dm = dm_ref[...]
        g0, g1 = g_ref[:, 0:D_MODEL].astype(F32), g_ref[:, D_MODEL:].astype(F32)
        dco_ref[...] = (dm * g0).astype(BF16)
        dmo_ref[...] = (dm * g1).astype(BF16)
        dgl_ref[:, 0:D_MODEL] = (dm * co_ref[...].astype(F32) * g0 * (1.0 - g0)).astype(BF16)
        dgl_ref[:, D_MODEL:] = (dm * mo_ref[...].astype(F32) * g1 * (1.0 - g1)).astype(BF16)

    dconv_out, dmla_out, dgl = _rows("merge_bwd", k_merge_bwd, [dmerged, gates, conv_out, mla_out], [],
                                     [(D_MODEL, BF16), (D_MODEL, BF16), (2 * D_MODEL, BF16)], tile=256)

    def epi_do(acc, xs, outs):
        for h in range(MLA_HEADS):
            lo = h * HEAD_PAD
            do = acc[:, lo:lo + HEAD_PAD]
            delta = jnp.sum(do * xs[0][:, lo:lo + HEAD_PAD].astype(F32), axis=-1, keepdims=True)
            outs[0][:, lo:lo + HEAD_PAD] = _put_stats(do.astype(BF16), delta, STAT_COL_V)

    do_aug, = _mm("mla_out_dx", dmla_out, wl["w_mla_out"], "nt", [(HW, BF16)], epi_do, row_x=[o_att], tn=HW)
    g_mla_out = _mm_plain("mla_out_dw", o_att, dmla_out, "tn")
    dq_att, dk_att, dv_att = _flash_bwd(q_aug, k_att, v_att, do_aug)

    def k_rope_bwd(dq_ref, dk_ref, dv_ref, c_ref, sa_ref, sb_ref, dqr_ref, dkv_ref, dkr_ref):
        c, sa, sb = c_ref[...], sa_ref[...], sb_ref[...]
        lane = lax.broadcasted_iota(jnp.int32, c.shape, 1)
        nope = (lane < QK_NOPE).astype(F32)
        ropem = ((lane >= QK_NOPE) & (lane < QK_NOPE + QK_ROPE)).astype(F32)
        dkr = jnp.zeros(c.shape, F32)
        for h in range(MLA_HEADS):
            lo = h * HEAD_PAD
            dqr_ref[:, lo:lo + HEAD_PAD] = (_rope(dq_ref[:, lo:lo + HEAD_PAD], c, sa, sb, -1.0) * qk_scale).astype(BF16)
            dk = dk_ref[:, lo:lo + HEAD_PAD]
            dkv_ref[:, lo:lo + HEAD_PAD] = (dk * nope).astype(BF16)
            dkr = dkr + dk
        dkv_ref[:, HW:] = dv_ref[...]
        dkr_ref[...] = (_rope(dkr * ropem, c, sa, sb, -1.0) * ropem).astype(BF16)

    dq_raw, dkv_cat, dkr = _rows("rope_bwd", k_rope_bwd, [dq_att, dk_att, dv_att, rope_c, rope_sa, rope_sb], [],
                                 [(HW, BF16), (2 * HW, BF16), (HEAD_PAD, BF16)], tile=256)
    g_uq = _mm_plain("q_up_dw", qn, dq_raw, "tn")
    dqn = _mm_plain("q_up_dx", dq_raw, wl["w_uq"], "nt")
    g_ukv = _mm_plain("kv_up_dw", kvn, dkv_cat, "tn")
    dkvn = _mm_plain("kv_up_dx", dkv_cat, wl["w_ukv"], "nt")

    def k_lora_bwd(cq_ref, ckv_ref, dqn_ref, dkvn_ref, gq_ref, gkv_ref, dcq_ref, dckv_ref, dgq_ref, dgkv_ref):
        dcq, dgq = _rms_bwd(cq_ref[...], gq_ref[...], dqn_ref[...])
        dckv, dgkv = _rms_bwd(ckv_ref[...], gkv_ref[...], dkvn_ref[...])
        dcq_ref[...] = dcq.astype(BF16)
        dckv_ref[...] = dckv.astype(BF16)
        _accum(dgq_ref, dgq)
        _accum(dgkv_ref, dgkv)

    dc_q, dc_kv, g_q_norm, g_kv_norm = _rows("lora_norm_bwd", k_lora_bwd, [c_q, c_kv, dqn, dkvn],
                                              [sm["q_norm_g"], sm["kv_norm_g"]], [(Q_LORA, BF16), (KV_LORA, BF16)],
                                              [(1, Q_LORA), (1, KV_LORA)])

    dz3 = _mm_plain("conv_out_dx", dconv_out, wl["w_conv_out"], "nt")
    g_conv_out = _mm_plain("conv_out_dw", z3, dconv_out, "tn")
    dz1, g_ln_g, g_ln_b, g_conv_b = _conv_bwd_norm(dz3, z1, sm["conv_ln_g"], sm["conv_ln_b"])
    dconv_in, g_conv_w = _conv_bwd_taps(dz1, z0, conv_in, conv_w)

    dproj = jnp.concatenate([dconv_in, dc_q, dc_kv, dkr, dgl], axis=1)
    g_in = _mm_plain("proj_dw", u1, dproj, "tn")
    token = late_grads_fn(dict(w_in=g_in, conv_w=g_conv_w[:CONV_WIDTH], w_conv_out=g_conv_out, w_uq=g_uq, w_ukv=g_ukv,
                               w_mla_out=g_mla_out, w_out=g_out))
    grad_x, g_norm_mix = dx_norm_bwd("proj_dx_norm", dproj, w_in, x, dh1, [sm["norm_mix_g"], token], with_bf16=False)

    small = dict(norm_mix_g=g_norm_mix, conv_b=g_conv_b, conv_ln_g=g_ln_g, conv_ln_b=g_ln_b, q_norm_g=g_q_norm,
                 kv_norm_g=g_kv_norm, norm_xattn_g=g_norm_xattn, norm_mem_g=g_norm_mem, norm_mlp_g=g_norm_mlp,
                 final_norm_g=g_final)
    return loss, grad_x, small


def _shard(a, k, axis):
    n = a.shape[axis] // N_CHIPS
    return lax.slice_in_dim(a, k * n, (k + 1) * n, axis=axis)


def _pack_small(grads, loss):
    flat = jnp.concatenate([grads[n].reshape(-1) for n in SMALL] + [loss.reshape(-1)[:1]])
    rows = -(-flat.shape[0] // (8 * LANES)) * 8
    return jnp.pad(flat, (0, rows * LANES - flat.shape[0])).reshape(rows, LANES)


def _pack_groups(shapes, names):
    groups = {}
    for n in names:
        groups.setdefault(shapes[n][1], []).append(n)
    return groups


def _pad_rows(a, mult):
    return jnp.pad(a, ((0, -a.shape[0] % mult), (0, 0)))


def _pack_grads(grads, shapes, names):
    packs = []
    for width, group in _pack_groups(shapes, names).items():
        per_chip = [jnp.concatenate([_pad_rows(_shard(grads[n], k, SHARD_AXIS[n]).astype(BF16), PACK_ROW_ALIGN) for n in group])
                    for k in range(N_CHIPS)]
        rows = per_chip[0].shape[0]
        packs.append(jnp.stack(per_chip).reshape(N_CHIPS, 2, rows // 2, width))
    return packs


def _unpack_grads(fulls, shapes, names):
    out = {}
    for full, group in zip(fulls, _pack_groups(shapes, names).values()):
        flat, at = full.reshape(-1, full.shape[-1]), 0
        for n in group:
            rows = shapes[n][0]
            out[n] = flat[at:at + rows]
            at += rows + (-rows % PACK_ROW_ALIGN)
    return out


def _unpack(flat, names, shapes):
    out, at = {}, 0
    for n in names:
        size = math.prod(shapes[n])
        out[n] = flat[at:at + size].reshape(shapes[n])
        at += size
    return out, at


def kernel(x, mem, positions, norm_mix_g, w_in, conv_w, conv_b, conv_ln_g, conv_ln_b, w_conv_out, q_norm_g, w_uq, kv_norm_g, w_ukv, w_mla_out, w_out, norm_xattn_g, norm_mem_g, w_xq, w_xkv, w_xo, norm_mlp_g, w_mlp1, w_mlp2, final_norm_g, loss_target, m_norm_mix_g, m_w_in, m_conv_w, m_conv_b, m_conv_ln_g, m_conv_ln_b, m_w_conv_out, m_q_norm_g, m_w_uq, m_kv_norm_g, m_w_ukv, m_w_mla_out, m_w_out, m_norm_xattn_g, m_norm_mem_g, m_w_xq, m_w_xkv, m_w_xo, m_norm_mlp_g, m_w_mlp1, m_w_mlp2, m_final_norm_g, v_norm_mix_g, v_w_in, v_conv_w, v_conv_b, v_conv_ln_g, v_conv_ln_b, v_w_conv_out, v_q_norm_g, v_w_uq, v_kv_norm_g, v_w_ukv, v_w_mla_out, v_w_out, v_norm_xattn_g, v_norm_mem_g, v_w_xq, v_w_xkv, v_w_xo, v_norm_mlp_g, v_w_mlp1, v_w_mlp2, v_final_norm_g):
    args = dict(locals())
    w = {n: args[n] for n in WEIGHTS}
    m = {n: args["m_" + n] for n in WEIGHTS}
    v = {n: args["v_" + n] for n in WEIGHTS}

    shards = [w[n][0].astype(F32 if n == "conv_w" else BF16) for n in BIG]
    bounds = (0,) + WEIGHT_WAITS + (len(BIG),)
    spans = [slice(lo, hi) for lo, hi in zip(bounds[:-1], bounds[1:])]
    sem_pairs, shards_thru, lands_thru, token = _gather_start(shards, [list(range(len(BIG)))[sp] for sp in spans])

    def unshard(n, g):
        ax = SHARD_AXIS[n]
        return jnp.moveaxis(g, 0, ax).reshape(g.shape[1:1 + ax] + (N_CHIPS * g.shape[1 + ax],) + g.shape[2 + ax:])

    def wait_fn(i):
        def fn(after):
            lands = _gather_wait(f"gather_weights_wait_{i}", sem_pairs[i], shards_thru[spans[i]], lands_thru[spans[i]], after)
            full = {n: unshard(n, g) for n, g in zip(BIG[spans[i]], lands)}
            return (_layout_w_in(full["w_in"]), full["conv_w"]) if i == 0 else _layout_weights(full)
        return fn

    sm = {n: w[n].reshape(1, -1) for n in SMALL}
    sm["norm_mix_g"] = sm["norm_mix_g"] + token[0, 0]

    shapes = {n: w[n].shape[1:] if n in BIG else w[n].shape for n in WEIGHTS}
    late_names = [n for n in BIG if n not in EARLY_GRADS]
    inflight = {}

    def send_grads(tag, names):
        def fn(g):
            packs = _pack_grads(_unlayout_grads(g), shapes, names)
            got = _pair_exchange(f"grad_pair_exchange_{tag}", packs)
            pairs = [_pair_add(f"grad_pair_add_{tag}_{i}", p, r) for i, (p, r) in enumerate(zip(packs, got))]
            *inflight[tag], token = _chip_exchange_start(f"grad_chip_exchange_{tag}_start", pairs)
            return token
        return fn

    loss, grad_x, g_small = _local_step(x[0], mem[0], positions, loss_target[0], [wait_fn(i) for i in range(3)],
                                        send_grads("early", EARLY_GRADS), send_grads("late", late_names), sm)

    halves, counts = [], {}
    for tag in ("late", "early"):
        own, got = _chip_exchange_wait(f"grad_chip_exchange_{tag}_wait", *inflight[tag], grad_x)
        halves += [_chip_add(f"grad_chip_add_{tag}_{i}", p, g) for i, (p, g) in enumerate(zip(own, got))]
        counts[tag] = len(own)
    fulls = _pair_share(halves)
    g_sum = _unpack_grads(fulls[:counts["late"]], shapes, late_names)
    g_sum.update(_unpack_grads(fulls[counts["late"]:], shapes, EARLY_GRADS))
    small_flat = _sum_over_devices(_pack_small(g_small, loss)).reshape(-1)
    g_small, at = _unpack(small_flat, SMALL, shapes)
    g_sum.update(g_small)
    loss_sum = small_flat[at]

    out_g, out_d, out_m, out_v = [], [], [], []
    for n in WEIGHTS:
        g, d, nm, nv = _adamw("adamw_" + n, w[n], g_sum[n], m[n], v[n])
        out_g.append(g)
        out_d.append(d)
        out_m.append(nm)
        out_v.append(nv)
    return (loss_sum, grad_x[None], *out_g, *out_d, *out_m, *out_v)
```

```python
import functools
import math

import jax
import jax.numpy as jnp
from jax import lax
from jax.experimental import pallas as pl
from jax.experimental.pallas import tpu as pltpu

F32 = jnp.float32
BF16 = jnp.bfloat16
MESH = pl.DeviceIdType.MESH

D_MODEL = 1024
CONV_CH = 512
CONV_WIDTH = 31
MLA_HEADS = 8
QK_NOPE = 64
QK_ROPE = 32
V_DIM = 64
Q_LORA = 384
KV_LORA = 256
MEM_LEN = 256
X_HEADS = 4
X_HEAD_DIM = 128
D_FF = 4096
ROPE_THETA = 10000.0
EPS = 1e-6
HEAD_PAD = 128
STAT_COL_QK = QK_NOPE + QK_ROPE
STAT_COL_V = V_DIM
HALO = 32
N_CHIPS = 4
LANES = 128

ADAM_LR = 0.001
ADAM_B1 = 0.9
ADAM_B2 = 0.999
ADAM_EPS = 1e-08
ADAM_WD = 0.01
ADAM_STEP = 10

VMEM_LIMIT = 52 * 1024 * 1024
ROW_TILES = (1024, 512, 256, 128, 64, 32, 16)
PACK_ROW_ALIGN = 32
N_DEV = 8
NEG = -1e30

BIG = ["w_in", "conv_w", "w_conv_out", "w_uq", "w_ukv", "w_mla_out", "w_out", "w_xq", "w_xkv", "w_xo", "w_mlp1", "w_mlp2"]
WEIGHT_WAITS = (2, 5)
SHARD_AXIS = {"w_in": 1, "w_conv_out": 1, "w_uq": 1, "w_ukv": 1, "w_mla_out": 1, "w_out": 0, "w_xq": 0, "w_xkv": 0,
              "w_xo": 1, "w_mlp1": 1, "w_mlp2": 0, "conv_w": 1}
EARLY_GRADS = ["w_mlp1", "w_mlp2", "w_xkv", "w_xq", "w_xo"]
SMALL = ["norm_mix_g", "conv_b", "conv_ln_g", "conv_ln_b", "q_norm_g", "kv_norm_g", "norm_xattn_g", "norm_mem_g",
         "norm_mlp_g", "final_norm_g"]
WEIGHTS = ["norm_mix_g", "w_in", "conv_w", "conv_b", "conv_ln_g", "conv_ln_b", "w_conv_out", "q_norm_g", "w_uq",
           "kv_norm_g", "w_ukv", "w_mla_out", "w_out", "norm_xattn_g", "norm_mem_g", "w_xq", "w_xkv", "w_xo",
           "norm_mlp_g", "w_mlp1", "w_mlp2", "final_norm_g"]


def _pick(n, prefs):
    for p in prefs:
        if n % p == 0:
            return p
    return n


def _params(sem):
    return pltpu.CompilerParams(dimension_semantics=sem, vmem_limit_bytes=VMEM_LIMIT)


_DIMS = {"nn": (((1,), (0,)), ((), ())), "nt": (((1,), (1,)), ((), ())), "tn": (((0,), (0,)), ((), ()))}


def _mm(name, a, b, mode, outs, epi, row_x=(), tile_x=(), vec_x=(), sums=(), tm=None, tn=None, tk=None, b_cols=None):
    if mode == "nn":
        (M, K), (_, N) = a.shape, b.shape
        if b_cols is not None:
            N = b_cols[1]
    elif mode == "nt":
        (M, K), (N, _) = a.shape, b.shape
    else:
        (K, M), (_, N) = a.shape, b.shape
    tm = tm or _pick(M, (1024, 512, 384, 256, 128))
    tn = tn or _pick(N, (1024, 768, 512, 384, 256, 128))
    tk = tk or _pick(K, (2048, 1920, 1024, 768, 512, 384, 256, 128))
    nk = K // tk
    rows_inner = nk == 1 and N // tn > 1
    grid = (N // tn, M // tm, nk) if rows_inner else (M // tm, N // tn, nk)

    def spec(shape, f):
        return pl.BlockSpec(shape, (lambda j, i, k: f(i, j, k)) if rows_inner else f)

    b_off = 0
    if b_cols is not None:
        assert mode == "nn" and b_cols[0] % tn == 0, (name, b_cols, tn)
        b_off = b_cols[0] // tn
    a_spec = spec((tk, tm), lambda i, j, k: (k, i)) if mode == "tn" else spec((tm, tk), lambda i, j, k: (i, k))
    b_spec = spec((tn, tk), lambda i, j, k: (j, k)) if mode == "nt" else spec((tk, tn), lambda i, j, k: (k, j + b_off))
    in_specs = [a_spec, b_spec]
    in_specs += [spec((tm, r.shape[1]), lambda i, j, k: (i, 0)) for r in row_x]
    in_specs += [spec((tm, tn), lambda i, j, k: (i, j)) for _ in tile_x]
    in_specs += [spec(v.shape, lambda i, j, k: (0, 0)) for v in vec_x]
    out_specs, out_shape = [], []
    for w, dt in outs:
        if tn == N:
            out_specs.append(spec((tm, w), lambda i, j, k: (i, 0)))
        else:
            assert w == N, (name, w, N)
            out_specs.append(spec((tm, tn), lambda i, j, k: (i, j)))
        out_shape.append(jax.ShapeDtypeStruct((M, w), dt))
    for shp in sums:
        assert tn == N and not rows_inner, name
        out_specs.append(spec(shp, lambda i, j, k: (0, 0)))
        out_shape.append(jax.ShapeDtypeStruct(shp, F32))
    nx = len(row_x) + len(tile_x) + len(vec_x)
    dims = _DIMS[mode]

    def body(a_ref, b_ref, *rest):
        x_refs, out_refs, acc_ref = rest[:nx], rest[nx:nx + len(outs) + len(sums)], rest[-1]
        av, bv = a_ref[...], b_ref[...]
        if av.dtype != BF16:
            av = av.astype(BF16)
        if bv.dtype != BF16:
            bv = bv.astype(BF16)
        prod = lax.dot_general(av, bv, dims, preferred_element_type=F32)
        if nk == 1:
            acc_ref[...] = prod
            epi(acc_ref, x_refs, out_refs)
        else:
            k = pl.program_id(2)

            @pl.when(k == 0)
            def _():
                acc_ref[...] = prod

            @pl.when(k > 0)
            def _():
                acc_ref[...] += prod

            @pl.when(k == nk - 1)
            def _():
                epi(acc_ref, x_refs, out_refs)

    res = pl.pallas_call(
        body, name=name, grid=grid, in_specs=in_specs, out_specs=out_specs, out_shape=out_shape,
        scratch_shapes=[pltpu.VMEM((tm, tn), F32)],
        compiler_params=_params(("arbitrary",) * 3 if sums else ("parallel", "parallel", "arbitrary")),
    )(a, b, *row_x, *tile_x, *vec_x)
    return res


def _epi_store(acc_ref, x_refs, out_refs):
    for o in out_refs:
        o[...] = acc_ref[...].astype(o.dtype)


def _mm_plain(name, a, b, mode, dtype=F32, **kw):
    n = kw["b_cols"][1] if kw.get("b_cols") else (b.shape[0] if mode == "nt" else b.shape[1])
    return _mm(name, a, b, mode, [(n, dtype)], _epi_store, **kw)[0]


def _rows(name, body, row_ins, vec_ins, row_outs, acc_outs=(), tile=512):
    S = row_ins[0].shape[0]
    t = _pick(S, (tile, 256, 128, 64, 32, 16, 8))
    in_specs = [pl.BlockSpec((t, r.shape[1]), lambda i: (i, 0)) for r in row_ins]
    in_specs += [pl.BlockSpec(v.shape, lambda i: (0, 0)) for v in vec_ins]
    out_specs = [pl.BlockSpec((t, w), lambda i: (i, 0)) for w, _ in row_outs]
    out_specs += [pl.BlockSpec(shp, lambda i: (0, 0)) for shp in acc_outs]
    out_shape = [jax.ShapeDtypeStruct((S, w), dt) for w, dt in row_outs]
    out_shape += [jax.ShapeDtypeStruct(shp, F32) for shp in acc_outs]
    sem = ("arbitrary",) if acc_outs else ("parallel",)
    return pl.pallas_call(
        functools.partial(body), name=name, grid=(S // t,), in_specs=in_specs, out_specs=out_specs,
        out_shape=out_shape, compiler_params=_params(sem),
    )(*row_ins, *vec_ins)


def _accum(ref, val, first=True):
    if first:
        @pl.when(pl.program_id(0) == 0)
        def _():
            ref[...] = jnp.zeros_like(ref)

    ref[...] += val


EPILOGUE_ROWS = 256


def _row_chunks(n):
    step = min(EPILOGUE_ROWS, n)
    return [slice(r, r + step) for r in range(0, n, step)]


def _colsum(v):
    return jnp.sum(v, axis=0, keepdims=True)


def _rms_fwd(x, g):
    r = lax.rsqrt(jnp.mean(x * x, axis=-1, keepdims=True) + EPS)
    return x * r * g


def _rms_bwd(x, g, du):
    r = lax.rsqrt(jnp.mean(x * x, axis=-1, keepdims=True) + EPS)
    xn = x * r
    gdu = du * g
    dx = r * (gdu - xn * jnp.mean(xn * gdu, axis=-1, keepdims=True))
    return dx, _colsum(du * xn)


def _sigmoid(v):
    return 1.0 / (1.0 + jnp.exp(-v))


def _rope(v, c, sa, sb, sign):
    return v * c + sign * (pltpu.roll(v, HEAD_PAD - QK_ROPE // 2, 1) * sa + pltpu.roll(v, QK_ROPE // 2, 1) * sb)


def _split3(v):
    hi = v.astype(BF16)
    r1 = v - hi.astype(F32)
    mid = r1.astype(BF16)
    lo = (r1 - mid.astype(F32)).astype(BF16)
    return hi, mid, lo


def _put_stats(base, stat, col):
    hi, mid, lo = _split3(stat)
    lane = lax.broadcasted_iota(jnp.int32, base.shape, 1)
    out = jnp.where(lane == col, hi, base)
    out = jnp.where(lane == col + 1, mid, out)
    return jnp.where(lane == col + 2, lo, out)


def _neg_ones(shape, col):
    lane = lax.broadcasted_iota(jnp.int32, shape, 1)
    return jnp.where((lane >= col) & (lane < col + 3), -1.0, 0.0).astype(F32)


def _shifted(ext, t):
    p = ext.shape[0]
    for b in range(8):
        rb = ext if b == 0 else pltpu.roll(ext, p - b, 0)
        for a in range(HALO // 8 + 1):
            if 8 * a + b <= HALO:
                yield 8 * a + b, rb[8 * a:8 * a + t]


def _conv_fwd(z0, conv_w, conv_b, ln_g, ln_b):
    S, C = z0.shape
    t = _pick(S, (512, 256, 128, 64, 32))
    per = t // HALO

    def body(cur_ref, prev_ref, w_ref, b_ref, g_ref, beta_ref, z1_ref, z3_ref, ext_ref):
        i = pl.program_id(0)
        ext_ref[0:HALO, :] = jnp.where(i > 0, prev_ref[...], 0.0)
        ext_ref[HALO:, :] = cur_ref[...]
        ext = ext_ref[...]
        acc = jnp.zeros((t, C), F32)
        for d, win in _shifted(ext, t):
            k = d - (HALO - CONV_WIDTH + 1)
            if 0 <= k < CONV_WIDTH:
                acc = acc + win * w_ref[k:k + 1, :]
        z1 = acc + b_ref[...]
        z1_ref[...] = z1
        mu = jnp.mean(z1, axis=-1, keepdims=True)
        zc = z1 - mu
        rs = lax.rsqrt(jnp.mean(zc * zc, axis=-1, keepdims=True) + EPS)
        z2 = zc * rs * g_ref[...] + beta_ref[...]
        z3_ref[...] = (z2 * _sigmoid(z2)).astype(BF16)

    vec = lambda v: pl.BlockSpec(v.shape, lambda i: (0, 0))
    return pl.pallas_call(
        body, name="conv_fwd", grid=(S // t,),
        in_specs=[pl.BlockSpec((t, C), lambda i: (i, 0)),
                  pl.BlockSpec((HALO, C), lambda i: (jnp.maximum(i * per - 1, 0), 0)),
                  vec(conv_w), vec(conv_b), vec(ln_g), vec(ln_b)],
        out_specs=[pl.BlockSpec((t, C), lambda i: (i, 0)), pl.BlockSpec((t, C), lambda i: (i, 0))],
        out_shape=[jax.ShapeDtypeStruct((S, C), F32), jax.ShapeDtypeStruct((S, C), BF16)],
        scratch_shapes=[pltpu.VMEM((t + HALO, C), F32)],
        compiler_params=_params(("parallel",)),
    )(z0, z0, conv_w, conv_b, ln_g, ln_b)


def _conv_bwd_norm(dz3, z1, ln_g, ln_b):
    C = z1.shape[1]

    def body(dz3_ref, z1_ref, g_ref, beta_ref, dz1_ref, dg_ref, dbeta_ref, dbias_ref):
        z1 = z1_ref[...]
        mu = jnp.mean(z1, axis=-1, keepdims=True)
        zc = z1 - mu
        rs = lax.rsqrt(jnp.mean(zc * zc, axis=-1, keepdims=True) + EPS)
        xh = zc * rs
        z2 = xh * g_ref[...] + beta_ref[...]
        sg = _sigmoid(z2)
        dz2 = dz3_ref[...] * (sg * (1.0 + z2 * (1.0 - sg)))
        dxh = dz2 * g_ref[...]
        dz1 = rs * (dxh - jnp.mean(dxh, axis=-1, keepdims=True) - xh * jnp.mean(dxh * xh, axis=-1, keepdims=True))
        dz1_ref[...] = dz1
        _accum(dg_ref, _colsum(dz2 * xh))
        _accum(dbeta_ref, _colsum(dz2))
        _accum(dbias_ref, _colsum(dz1))

    return _rows("conv_bwd_norm", body, [dz3, z1], [ln_g, ln_b], [(C, F32)], [(1, C)] * 3)


def _conv_bwd_taps(dz1, z0, conv_in, conv_w):
    S, C = z0.shape
    t = _pick(S, (512, 256, 128, 64, 32))
    per = t // HALO
    last = S // HALO - 1
    nt = S // t

    def body(dcur_ref, dnext_ref, zcur_ref, zprev_ref, cin_ref, w_ref, dcin_ref, dw_ref, dext_ref, zext_ref):
        i = pl.program_id(0)
        dcur = dcur_ref[...]
        dext_ref[0:t, :] = dcur
        dext_ref[t:, :] = jnp.where(i < nt - 1, dnext_ref[...], 0.0)
        zext_ref[0:HALO, :] = jnp.where(i > 0, zprev_ref[...], 0.0)
        zext_ref[HALO:, :] = zcur_ref[...]

        @pl.when(i == 0)
        def _():
            dw_ref[...] = jnp.zeros_like(dw_ref)

        dz0 = jnp.zeros((t, C), F32)
        for d, win in _shifted(dext_ref[...], t):
            k = CONV_WIDTH - 1 - d
            if 0 <= k < CONV_WIDTH:
                dz0 = dz0 + win * w_ref[k:k + 1, :]
        for d, win in _shifted(zext_ref[...], t):
            k = d - (HALO - CONV_WIDTH + 1)
            if 0 <= k < CONV_WIDTH:
                dw_ref[k:k + 1, :] += _colsum(dcur * win)
        a = cin_ref[:, 0:C].astype(F32)
        sg = _sigmoid(cin_ref[:, C:2 * C].astype(F32))
        dcin_ref[:, 0:C] = (dz0 * sg).astype(BF16)
        dcin_ref[:, C:2 * C] = (dz0 * a * sg * (1.0 - sg)).astype(BF16)

    return pl.pallas_call(
        body, name="conv_bwd_taps", grid=(nt,),
        in_specs=[pl.BlockSpec((t, C), lambda i: (i, 0)),
                  pl.BlockSpec((HALO, C), lambda i: (jnp.minimum((i + 1) * per, last), 0)),
                  pl.BlockSpec((t, C), lambda i: (i, 0)),
                  pl.BlockSpec((HALO, C), lambda i: (jnp.maximum(i * per - 1, 0), 0)),
                  pl.BlockSpec((t, 2 * C), lambda i: (i, 0)),
                  pl.BlockSpec(conv_w.shape, lambda i: (0, 0))],
        out_specs=[pl.BlockSpec((t, 2 * C), lambda i: (i, 0)), pl.BlockSpec((HALO, C), lambda i: (0, 0))],
        out_shape=[jax.ShapeDtypeStruct((S, 2 * C), BF16), jax.ShapeDtypeStruct((HALO, C), F32)],
        scratch_shapes=[pltpu.VMEM((t + HALO, C), F32), pltpu.VMEM((t + HALO, C), F32)],
        compiler_params=_params(("arbitrary",)),
    )(dz1, dz1, z0, z0, conv_in, conv_w)


def _lower_tri(shape, rows_are_queries):
    row = lax.broadcasted_iota(jnp.int32, shape, 0)
    col = lax.broadcasted_iota(jnp.int32, shape, 1)
    return (col <= row) if rows_are_queries else (row <= col)


HEADS_PER_STEP = 2
FWD_HEADS_PER_STEP = 2
FWD_KEY_TILES = 4


def _flash_specs(S, t, heads):
    w = heads * HEAD_PAD
    blk = pl.BlockSpec((t, w), lambda h, i: (i, h))
    head = pl.BlockSpec((S, w), lambda h, i: (0, h))
    return blk, head


def _head_lanes(g):
    return slice(g * HEAD_PAD, (g + 1) * HEAD_PAD)


def _dot_nt(a, b):
    return lax.dot_general(a, b, _DIMS["nt"], preferred_element_type=F32)


def _dot_nn(a, b):
    return lax.dot_general(a, b, _DIMS["nn"], preferred_element_type=F32)


def _dot_tn(a, b):
    return lax.dot_general(a, b, _DIMS["tn"], preferred_element_type=F32)


def _flash_fwd(q, k, v):
    S = q.shape[0]
    t = _pick(S, (512, 256, 128))

    def body(q_ref, k_ref, v_ref, o_ref, qa_ref, m_ref, acc_ref):
        qi = pl.program_id(1)
        m_ref[...] = jnp.full_like(m_ref, NEG)
        acc_ref[...] = jnp.zeros_like(acc_ref)

        def step(first, tiles, diag):
            width = tiles * t
            rows = pl.ds(pl.multiple_of(first, t), width)
            for g in range(FWD_HEADS_PER_STEP):
                hl = _head_lanes(g)
                s = _dot_nt(q_ref[:, hl], k_ref[rows, hl])
                if diag:
                    row = lax.broadcasted_iota(jnp.int32, s.shape, 0)
                    col = lax.broadcasted_iota(jnp.int32, s.shape, 1)
                    s = jnp.where(col <= row + (tiles - 1) * t, s, NEG)
                m_old = m_ref[g]
                m_new = jnp.maximum(m_old, jnp.max(s, axis=-1, keepdims=True))
                p = jnp.exp(s - m_new).astype(BF16)
                acc_ref[g] = jnp.exp(m_old - m_new) * acc_ref[g] + _dot_nn(p, v_ref[rows, hl])
                m_ref[g] = m_new

        def wide(kb, carry):
            step(kb * (FWD_KEY_TILES * t), FWD_KEY_TILES, False)
            return carry

        full_groups = qi // FWD_KEY_TILES
        lax.fori_loop(0, full_groups, wide, 0)
        for tiles in range(1, min(FWD_KEY_TILES, S // t) + 1):
            @pl.when(qi - full_groups * FWD_KEY_TILES == tiles - 1)
            def _():
                step(full_groups * (FWD_KEY_TILES * t), tiles, True)

        for g in range(FWD_HEADS_PER_STEP):
            hl = _head_lanes(g)
            acc = acc_ref[g]
            l = -acc[:, STAT_COL_V:STAT_COL_V + 1]
            o_ref[:, hl] = (acc / l).astype(BF16)
            qa_ref[:, hl] = _put_stats(q_ref[:, hl], m_ref[g] + jnp.log(l), STAT_COL_QK)

    blk, head = _flash_specs(S, t, FWD_HEADS_PER_STEP)
    return pl.pallas_call(
        body, name="mla_flash_fwd", grid=(MLA_HEADS // FWD_HEADS_PER_STEP, S // t),
        in_specs=[blk, head, head], out_specs=[blk, blk],
        out_shape=[jax.ShapeDtypeStruct(q.shape, BF16), jax.ShapeDtypeStruct(q.shape, BF16)],
        scratch_shapes=[pltpu.VMEM((FWD_HEADS_PER_STEP, t, 1), F32), pltpu.VMEM((FWD_HEADS_PER_STEP, t, HEAD_PAD), F32)],
        compiler_params=_params(("parallel", "arbitrary")),
    )(q, k, v)


def _flash_bwd(qa, k, v, doa):
    S = qa.shape[0]
    t = _pick(S, (512, 256, 128))
    n = S // t

    def body(qa_ref, k_ref, v_ref, do_ref, dq_ref, dk_ref, dv_ref, dk_acc, dv_acc):
        kj = pl.program_id(1)

        @pl.when(kj == 0)
        def _():
            dq_ref[...] = jnp.zeros_like(dq_ref)

        dk_acc[...] = jnp.zeros_like(dk_acc)
        dv_acc[...] = jnp.zeros_like(dv_acc)

        def step(qi, diag):
            rows = pl.ds(pl.multiple_of(qi * t, t), t)
            for g in range(HEADS_PER_STEP):
                hl = _head_lanes(g)
                qa, do, kk = qa_ref[rows, hl], do_ref[rows, hl], k_ref[:, hl]
                st = _dot_nt(kk, qa)
                if diag:
                    st = jnp.where(_lower_tri(st.shape, False), st, NEG)
                pt = jnp.exp(st)
                dst = (pt * _dot_nt(v_ref[:, hl], do)).astype(BF16)
                dv_acc[:, hl] += _dot_nn(pt.astype(BF16), do)
                dk_acc[:, hl] += _dot_nn(dst, qa)
                dq_ref[rows, hl] += _dot_tn(dst, kk)

        def loop(qi, carry):
            step(qi, False)
            return carry

        step(kj, True)
        lax.fori_loop(kj + 1, n, loop, 0)
        dk_ref[...] = dk_acc[...]
        dv_ref[...] = dv_acc[...].astype(BF16)

    blk, head = _flash_specs(S, t, HEADS_PER_STEP)
    w = HEADS_PER_STEP * HEAD_PAD
    return pl.pallas_call(
        body, name="mla_flash_bwd", grid=(MLA_HEADS // HEADS_PER_STEP, n),
        in_specs=[head, blk, blk, head], out_specs=[head, blk, blk],
        out_shape=[jax.ShapeDtypeStruct(qa.shape, F32), jax.ShapeDtypeStruct(qa.shape, F32), jax.ShapeDtypeStruct(qa.shape, BF16)],
        scratch_shapes=[pltpu.VMEM((t, w), F32), pltpu.VMEM((t, w), F32)],
        compiler_params=_params(("parallel", "arbitrary")),
    )(qa, k, v, doa)


def _xattn_fwd(xq, kvx):
    W = X_HEADS * X_HEAD_DIM

    def body(q_ref, kv_ref, o_ref):
        for h in range(X_HEADS):
            lo = h * X_HEAD_DIM
            s = _dot_nt(q_ref[:, lo:lo + X_HEAD_DIM], kv_ref[:, lo:lo + X_HEAD_DIM])
            p = jnp.exp(s - jnp.max(s, axis=-1, keepdims=True))
            p = p / jnp.sum(p, axis=-1, keepdims=True)
            o_ref[:, lo:lo + X_HEAD_DIM] = _dot_nn(p.astype(BF16), kv_ref[:, W + lo:W + lo + X_HEAD_DIM]).astype(BF16)

    return _rows("xattn_fwd", body, [xq], [kvx], [(W, BF16)])[0]


def _xattn_bwd(xq, kvx, dox):
    W = X_HEADS * X_HEAD_DIM
    scale = X_HEAD_DIM ** -0.5

    def body(q_ref, do_ref, kv_ref, dq_ref, dkv_ref):
        @pl.when(pl.program_id(0) == 0)
        def _():
            dkv_ref[...] = jnp.zeros_like(dkv_ref)

        for h in range(X_HEADS):
            lo = h * X_HEAD_DIM
            q, k = q_ref[:, lo:lo + X_HEAD_DIM], kv_ref[:, lo:lo + X_HEAD_DIM]
            v, do = kv_ref[:, W + lo:W + lo + X_HEAD_DIM], do_ref[:, lo:lo + X_HEAD_DIM]
            s = _dot_nt(q, k)
            p = jnp.exp(s - jnp.max(s, axis=-1, keepdims=True))
            p = p / jnp.sum(p, axis=-1, keepdims=True)
            dp = _dot_nt(do, v)
            ds = (p * (dp - jnp.sum(dp * p, axis=-1, keepdims=True))).astype(BF16)
            dq_ref[:, lo:lo + X_HEAD_DIM] = (_dot_nn(ds, k) * scale).astype(BF16)
            dkv_ref[:, lo:lo + X_HEAD_DIM] += _dot_tn(ds, q)
            dkv_ref[:, W + lo:W + lo + X_HEAD_DIM] += _dot_tn(p.astype(BF16), do)

    return _rows("xattn_bwd", body, [xq, dox], [kvx], [(W, BF16)], [kvx.shape])


def _adamw(name, w, g, m, v):
    c1 = 1.0 / (1.0 - ADAM_B1 ** ADAM_STEP)
    c2 = 1.0 / (1.0 - ADAM_B2 ** ADAM_STEP)
    lead = (0,) * (w.ndim - 2)
    w2 = w.reshape((1,) * (2 - w.ndim) + w.shape) if w.ndim < 2 else w
    m2, v2 = m.reshape(w2.shape), v.reshape(w2.shape)
    g2 = g.reshape(w2.shape[-2:])
    R, C = g2.shape
    t = _pick(R, (256, 128, 64, 32, 16, 8))

    def body(w_ref, g_ref, m_ref, v_ref, go_ref, d_ref, nm_ref, nv_ref):
        g = g_ref[...]
        nm = ADAM_B1 * m_ref[lead] + (1.0 - ADAM_B1) * g
        nv = ADAM_B2 * v_ref[lead] + (1.0 - ADAM_B2) * (g * g)
        go_ref[lead] = g
        d_ref[lead] = -ADAM_LR * ((nm * c1) / (jnp.sqrt(nv * c2) + ADAM_EPS) + ADAM_WD * w_ref[lead])
        nm_ref[lead] = nm
        nv_ref[lead] = nv

    full = pl.BlockSpec((1,) * len(lead) + (t, C), lambda i: lead + (i, 0))
    outs = pl.pallas_call(
        body, name=name, grid=(R // t,), in_specs=[full, pl.BlockSpec((t, C), lambda i: (i, 0)), full, full],
        out_specs=[full] * 4, out_shape=[jax.ShapeDtypeStruct(w2.shape, F32)] * 4, compiler_params=_params(("parallel",)),
    )(w2, g2, m2, v2)
    return [o.reshape(w.shape) for o in outs]


def _place():
    x, y, c = lax.axis_index("x"), lax.axis_index("y"), lax.axis_index("c")
    return x, y, c, [(1 - x, y), (x, 1 - y), (1 - x, 1 - y)]


_ANY = pl.BlockSpec(memory_space=pl.ANY)


_HBM = pl.BlockSpec(memory_space=pltpu.HBM)
_SEM = pl.BlockSpec(memory_space=pltpu.SEMAPHORE)
_SIDE_EFFECT = pltpu.SideEffectType.DATAFLOW_SIDE_EFFECTING


def _gather_copy(src, land, slot, send, recv, k, chip, c):
    return pltpu.make_async_remote_copy(src_ref=src, dst_ref=land.at[slot], send_sem=send.at[k], recv_sem=recv.at[k],
                                        device_id=(chip[0], chip[1], c), device_id_type=MESH)


def _gather_start(shards, groups):
    n, ng = len(shards), len(groups)
    lands = [jnp.broadcast_to(s[None], (N_CHIPS,) + s.shape) for s in shards]

    def body(*refs):
        ins, lnd = refs[:n], refs[n:2 * n]
        sends, recvs = refs[2 * n:2 * n + ng], refs[2 * n + ng:2 * n + 2 * ng]
        token = refs[-1]
        x, y, c, chips = _place()
        for gi, group in enumerate(groups):
            for pos, w in enumerate(group):
                for j, chip in enumerate(chips):
                    _gather_copy(ins[w], lnd[w], 2 * x + y, sends[gi], recvs[gi], 3 * pos + j, chip, c).start()
        token[...] = jnp.zeros_like(token)

    sems = [pltpu.SemaphoreType.DMA((3 * len(g),)) for g in groups]
    res = pl.pallas_call(
        body, name="gather_weights_start",
        out_shape=sems + sems + [pltpu.HBM(a.shape, a.dtype) for a in shards + lands] + [jax.ShapeDtypeStruct((8, LANES), F32)],
        in_specs=[_HBM] * (2 * n),
        out_specs=[_SEM] * (2 * ng) + [_HBM] * (2 * n) + [pl.BlockSpec(memory_space=pltpu.VMEM)],
        input_output_aliases={i: 2 * ng + i for i in range(2 * n)},
        compiler_params=pltpu.CompilerParams(has_side_effects=_SIDE_EFFECT),
    )(*[pltpu.with_memory_space_constraint(a, pltpu.HBM) for a in shards + lands])
    sem_pairs = list(zip(res[:ng], res[ng:2 * ng]))
    return sem_pairs, res[2 * ng:2 * ng + n], res[2 * ng + n:2 * ng + 2 * n], res[-1]


def _gather_wait(name, sem_pair, shards_thru, lands_thru, after):
    m = len(shards_thru)

    def body(*refs):
        ins, lnd = refs[:m], refs[m:2 * m]
        send, recv = refs[2 * m], refs[2 * m + 1]
        x, y, c, chips = _place()
        for pos in range(m):
            for j, chip in enumerate(chips):
                cp = _gather_copy(ins[pos], lnd[pos], 2 * chip[0] + chip[1], send, recv, 3 * pos + j, chip, c)
                cp.wait_send()
                cp.wait_recv()

    res = pl.pallas_call(
        body, name=name,
        out_shape=[pltpu.HBM(a.shape, a.dtype) for a in list(shards_thru) + list(lands_thru)],
        in_specs=[_HBM] * (2 * m) + [_SEM, _SEM, _ANY], out_specs=[_HBM] * (2 * m),
        input_output_aliases={i: i for i in range(2 * m)},
        compiler_params=pltpu.CompilerParams(has_side_effects=_SIDE_EFFECT),
    )(*shards_thru, *lands_thru, *sem_pair, after)
    return res[m:]


def _pair_exchange(name, packs):
    n = len(packs)

    def body(*refs):
        ins, outs, send, recv = refs[:n], refs[n:2 * n], refs[2 * n], refs[2 * n + 1]
        x, y, c, _ = _place()
        cps = []
        for g in range(n):
            cp = pltpu.make_async_remote_copy(src_ref=ins[g].at[:, pl.ds(1 - c, 1)], dst_ref=outs[g], send_sem=send.at[g],
                                              recv_sem=recv.at[g], device_id=(x, y, 1 - c), device_id_type=MESH)
            cp.start()
            cps.append(cp)
        for cp in cps:
            cp.wait()

    return pl.pallas_call(
        body, name=name, in_specs=[_ANY] * n, out_specs=[_ANY] * n,
        out_shape=[jax.ShapeDtypeStruct((N_CHIPS, 1) + p.shape[2:], p.dtype) for p in packs],
        scratch_shapes=[pltpu.SemaphoreType.DMA((n,)), pltpu.SemaphoreType.DMA((n,))],
    )(*packs)


def _chip_copy(src, land, src_slot, dst_slot, send, recv, k, chip, c):
    return pltpu.make_async_remote_copy(src_ref=src.at[src_slot], dst_ref=land.at[dst_slot], send_sem=send.at[k],
                                        recv_sem=recv.at[k], device_id=(chip[0], chip[1], c), device_id_type=MESH)


def _chip_exchange_start(name, parts):
    n = len(parts)
    lands = [lax.empty(p.shape, p.dtype) for p in parts]

    def body(*refs):
        ins, lnd, send, recv, token = refs[:n], refs[n:2 * n], refs[2 * n], refs[2 * n + 1], refs[-1]
        x, y, c, chips = _place()
        for g in range(n):
            for j, chip in enumerate(chips):
                _chip_copy(ins[g], lnd[g], 2 * chip[0] + chip[1], 2 * x + y, send, recv, 3 * g + j, chip, c).start()
        token[...] = jnp.zeros_like(token)

    sems = [pltpu.SemaphoreType.DMA((3 * n,))] * 2
    res = pl.pallas_call(
        body, name=name,
        out_shape=sems + [pltpu.HBM(a.shape, a.dtype) for a in list(parts) + lands] + [jax.ShapeDtypeStruct((8, LANES), F32)],
        in_specs=[_HBM] * (2 * n),
        out_specs=[_SEM] * 2 + [_HBM] * (2 * n) + [pl.BlockSpec(memory_space=pltpu.VMEM)],
        input_output_aliases={i: 2 + i for i in range(2 * n)},
        compiler_params=pltpu.CompilerParams(has_side_effects=_SIDE_EFFECT),
    )(*[pltpu.with_memory_space_constraint(a, pltpu.HBM) for a in list(parts) + lands])
    return res[:2], res[2:2 + n], res[2 + n:2 + 2 * n], res[-1]


def _chip_exchange_wait(name, sems, parts_thru, lands_thru, after):
    n = len(parts_thru)

    def body(*refs):
        ins, lnd, send, recv = refs[:n], refs[n:2 * n], refs[2 * n], refs[2 * n + 1]
        x, y, c, chips = _place()
        for g in range(n):
            for j, chip in enumerate(chips):
                cp = _chip_copy(ins[g], lnd[g], 2 * x + y, 2 * chip[0] + chip[1], send, recv, 3 * g + j, chip, c)
                cp.wait_send()
                cp.wait_recv()

    res = pl.pallas_call(
        body, name=name,
        out_shape=[pltpu.HBM(a.shape, a.dtype) for a in list(parts_thru) + list(lands_thru)],
        in_specs=[_HBM] * (2 * n) + [_SEM, _SEM, _ANY], out_specs=[_HBM] * (2 * n),
        input_output_aliases={i: i for i in range(2 * n)},
        compiler_params=pltpu.CompilerParams(has_side_effects=_SIDE_EFFECT),
    )(*parts_thru, *lands_thru, *sems, after)
    return res[:n], res[n:]


def _pair_share(halves):
    n = len(halves)

    def body(*refs):
        outs, send, recv = refs[n:2 * n], refs[2 * n], refs[2 * n + 1]
        x, y, c, _ = _place()
        cps = []
        for g in range(n):
            cp = pltpu.make_async_remote_copy(src_ref=outs[g].at[c], dst_ref=outs[g].at[c], send_sem=send.at[g],
                                              recv_sem=recv.at[g], device_id=(x, y, 1 - c), device_id_type=MESH)
            cp.start()
            cps.append(cp)
        for g in range(n):
            pltpu.make_async_remote_copy(src_ref=outs[g].at[c], dst_ref=outs[g].at[1 - c], send_sem=send.at[g],
                                         recv_sem=recv.at[g], device_id=(x, y, 1 - c), device_id_type=MESH).wait_recv()
        for cp in cps:
            cp.wait_send()

    return pl.pallas_call(
        body, name="grad_pair_share", in_specs=[_ANY] * n, out_specs=[_ANY] * n,
        out_shape=[jax.ShapeDtypeStruct(h.shape, h.dtype) for h in halves],
        input_output_aliases={g: g for g in range(n)},
        scratch_shapes=[pltpu.SemaphoreType.DMA((n,)), pltpu.SemaphoreType.DMA((n,))],
    )(*halves)


def _sum_over_devices(block):
    R, L = block.shape

    def gather(b_ref, o_ref, send, recv, loc):
        x, y, c, _ = _place()
        lc = pltpu.make_async_copy(b_ref, o_ref.at[4 * x + 2 * y + c], loc)
        lc.start()
        peers = [(1 - x if dx else x, 1 - y if dy else y, 1 - c if dc else c)
                 for dx in (0, 1) for dy in (0, 1) for dc in (0, 1) if dx or dy or dc]
        cps = []
        for j, peer in enumerate(peers):
            cp = pltpu.make_async_remote_copy(src_ref=b_ref, dst_ref=o_ref.at[4 * x + 2 * y + c], send_sem=send.at[j],
                                              recv_sem=recv.at[j], device_id=peer, device_id_type=MESH)
            cp.start()
            cps.append(cp)
        for j, (px, py, pc) in enumerate(peers):
            pltpu.make_async_remote_copy(src_ref=b_ref, dst_ref=o_ref.at[4 * px + 2 * py + pc], send_sem=send.at[j],
                                         recv_sem=recv.at[j], device_id=(px, py, pc), device_id_type=MESH).wait_recv()
        for cp in cps:
            cp.wait_send()
        lc.wait()

    blocks = pl.pallas_call(
        gather, name="small_grads_gather", in_specs=[_ANY], out_specs=_ANY,
        out_shape=jax.ShapeDtypeStruct((N_DEV, R, L), F32),
        scratch_shapes=[pltpu.SemaphoreType.DMA((N_DEV - 1,)), pltpu.SemaphoreType.DMA((N_DEV - 1,)), pltpu.SemaphoreType.DMA],
    )(block)

    def add(b_ref, o_ref):
        total = b_ref[0]
        for d in range(1, N_DEV):
            total = total + b_ref[d]
        o_ref[...] = total

    return pl.pallas_call(add, name="small_grads_add", out_shape=jax.ShapeDtypeStruct((R, L), F32))(blocks)


def _pair_add(name, pack, got):
    _, _, R, C = pack.shape
    t = _pick(R, ROW_TILES)
    c = lax.axis_index("c").astype(jnp.int32).reshape(1)

    def body(c_ref, p_ref, g_ref, o_ref):
        o_ref[...] = (p_ref[...].astype(F32) + g_ref[...].astype(F32)).astype(BF16)

    return pl.pallas_call(
        body, name=name,
        grid_spec=pltpu.PrefetchScalarGridSpec(
            num_scalar_prefetch=1, grid=(N_CHIPS, R // t),
            in_specs=[pl.BlockSpec((None, None, t, C), lambda k, i, c_ref: (k, c_ref[0], i, 0)),
                      pl.BlockSpec((None, None, t, C), lambda k, i, c_ref: (k, 0, i, 0))],
            out_specs=pl.BlockSpec((None, t, C), lambda k, i, c_ref: (k, i, 0))),
        out_shape=jax.ShapeDtypeStruct((N_CHIPS, R, C), BF16), compiler_params=_params(("parallel", "parallel")),
    )(c, pack, got)


def _chip_add(name, own, got):
    _, R, C = own.shape
    t = _pick(R, ROW_TILES)
    x, y, c, _ = _place()
    place = jnp.stack([c, 2 * x + y]).astype(jnp.int32)

    def body(place_ref, own_ref, g1_ref, g2_ref, g3_ref, o_ref):
        o_ref[...] = ((own_ref[...].astype(F32) + g1_ref[...].astype(F32)) + g2_ref[...].astype(F32)) + g3_ref[...].astype(F32)

    def other(d):
        return pl.BlockSpec((None, t, C), lambda i, place_ref: ((place_ref[1] + d) % N_CHIPS, i, 0))

    return pl.pallas_call(
        body, name=name,
        grid_spec=pltpu.PrefetchScalarGridSpec(
            num_scalar_prefetch=1, grid=(R // t,),
            in_specs=[pl.BlockSpec((None, t, C), lambda i, place_ref: (place_ref[1], i, 0)), other(1), other(2), other(3)],
            out_specs=pl.BlockSpec((None, t, C), lambda i, place_ref: (place_ref[0], i, 0))),
        out_shape=jax.ShapeDtypeStruct((2, R, C), F32), compiler_params=_params(("parallel",)),
    )(place, own, got, got, got)


_CUT = (2 * CONV_CH, 2 * CONV_CH + Q_LORA, 2 * CONV_CH + Q_LORA + KV_LORA, 2 * CONV_CH + Q_LORA + KV_LORA + QK_ROPE)
W_IN_CONV = (0, 2 * CONV_CH)
W_IN_GATES = (W_IN_CONV[1], 2 * D_MODEL)
W_IN_CQ = (W_IN_GATES[0] + W_IN_GATES[1], Q_LORA)
W_IN_KR = (W_IN_CQ[0] + W_IN_CQ[1], HEAD_PAD)
W_IN_CKV = (W_IN_KR[0] + W_IN_KR[1], KV_LORA)


def _pad_last(a, n):
    return jnp.pad(a, [(0, 0)] * (a.ndim - 1) + [(0, n - a.shape[-1])])


def _layout_w_in(w_in):
    kr = jnp.pad(w_in[:, _CUT[2]:_CUT[3]], ((0, 0), (QK_NOPE, HEAD_PAD - QK_NOPE - QK_ROPE)))
    return jnp.concatenate([w_in[:, :_CUT[0]], w_in[:, _CUT[3]:], w_in[:, _CUT[0]:_CUT[1]], kr, w_in[:, _CUT[1]:_CUT[2]]], axis=1)


def _layout_weights(w):
    out = dict(w)
    if "w_uq" in w:
        out["w_uq"] = _pad_last(w["w_uq"].reshape(Q_LORA, MLA_HEADS, QK_NOPE + QK_ROPE), HEAD_PAD).reshape(Q_LORA, MLA_HEADS * HEAD_PAD)
    if "w_ukv" in w:
        ukv = w["w_ukv"].reshape(KV_LORA, MLA_HEADS, QK_NOPE + V_DIM)
        uk = _pad_last(ukv[:, :, :QK_NOPE], HEAD_PAD).reshape(KV_LORA, MLA_HEADS * HEAD_PAD)
        uv = _pad_last(ukv[:, :, QK_NOPE:], HEAD_PAD).reshape(KV_LORA, MLA_HEADS * HEAD_PAD)
        out["w_ukv"] = jnp.concatenate([uk, uv], axis=1)
    if "w_mla_out" in w:
        mo = jnp.pad(w["w_mla_out"].reshape(MLA_HEADS, V_DIM, D_MODEL), ((0, 0), (0, HEAD_PAD - V_DIM), (0, 0)))
        out["w_mla_out"] = mo.reshape(MLA_HEADS * HEAD_PAD, D_MODEL)
    return out


def _unlayout_grads(g):
    out = dict(g)
    if "w_in" in g:
        gi = g["w_in"]
        win = lambda w: gi[:, w[0]:w[0] + w[1]]
        kr = gi[:, W_IN_KR[0] + QK_NOPE:W_IN_KR[0] + QK_NOPE + QK_ROPE]
        out["w_in"] = jnp.concatenate([win(W_IN_CONV), win(W_IN_CQ), win(W_IN_CKV), kr, win(W_IN_GATES)], axis=1)
    if "w_uq" in g:
        out["w_uq"] = g["w_uq"].reshape(Q_LORA, MLA_HEADS, HEAD_PAD)[:, :, :QK_NOPE + QK_ROPE].reshape(Q_LORA, -1)
    if "w_ukv" in g:
        gk = g["w_ukv"][:, :MLA_HEADS * HEAD_PAD].reshape(KV_LORA, MLA_HEADS, HEAD_PAD)[:, :, :QK_NOPE]
        gv = g["w_ukv"][:, MLA_HEADS * HEAD_PAD:].reshape(KV_LORA, MLA_HEADS, HEAD_PAD)[:, :, :V_DIM]
        out["w_ukv"] = jnp.concatenate([gk, gv], axis=2).reshape(KV_LORA, -1)
    if "w_mla_out" in g:
        out["w_mla_out"] = g["w_mla_out"].reshape(MLA_HEADS, HEAD_PAD, D_MODEL)[:, :V_DIM].reshape(MLA_HEADS * V_DIM, D_MODEL)
    return out


def _rope_tables(positions):
    half = QK_ROPE // 2
    inv_freq = ROPE_THETA ** (-jnp.arange(half, dtype=F32) / half)
    ang = positions.astype(F32).reshape(-1, 1) * inv_freq
    cos, sin = jnp.cos(ang), jnp.sin(ang)
    S = cos.shape[0]
    z16, z32, z64 = jnp.zeros((S, half), F32), jnp.zeros((S, QK_ROPE), F32), jnp.zeros((S, QK_NOPE), F32)
    c = jnp.concatenate([jnp.ones((S, QK_NOPE), F32), cos, cos, z32], axis=1)
    sa = jnp.concatenate([z64, -sin, z16, z32], axis=1)
    sb = jnp.concatenate([z64, z16, sin, z32], axis=1)
    return c, sa, sb


def _local_step(x, mem, positions, target, weight_fns, early_grads_fn, late_grads_fn, sm):
    S = x.shape[0]
    HW = MLA_HEADS * HEAD_PAD
    rope_c, rope_sa, rope_sb = _rope_tables(positions)
    qk_scale = (QK_NOPE + QK_ROPE) ** -0.5

    def k_rms1(x_ref, g_ref, u_ref):
        u_ref[...] = _rms_fwd(x_ref[...], g_ref[...]).astype(BF16)

    u1, = _rows("rms_mix", k_rms1, [x], [sm["norm_mix_g"]], [(D_MODEL, BF16)])
    w_in, conv_w = weight_fns[0](u1)

    def epi_glu(acc, xs, outs):
        a, gt = acc[:, 0:CONV_CH], acc[:, CONV_CH:2 * CONV_CH]
        outs[0][...] = acc[...].astype(BF16)
        outs[1][...] = a * _sigmoid(gt)

    conv_in, z0 = _mm("proj_conv", u1, w_in, "nn", [(2 * CONV_CH, BF16), (CONV_CH, F32)], epi_glu, b_cols=W_IN_CONV)
    c_q = _mm_plain("proj_cq", u1, w_in, "nn", b_cols=W_IN_CQ)
    c_kv = _mm_plain("proj_ckv", u1, w_in, "nn", b_cols=W_IN_CKV)
    kr_raw = _mm_plain("proj_krope", u1, w_in, "nn", b_cols=W_IN_KR)

    def epi_sigmoid(acc, xs, outs):
        outs[0][...] = _sigmoid(acc[...]).astype(BF16)

    gates, = _mm("proj_gates", u1, w_in, "nn", [(2 * D_MODEL, BF16)], epi_sigmoid, b_cols=W_IN_GATES)

    z1, z3 = _conv_fwd(z0, conv_w, sm["conv_b"], sm["conv_ln_g"], sm["conv_ln_b"])
    wl = weight_fns[1](z1)
    conv_out = _mm_plain("conv_out", z3, wl["w_conv_out"], "nn", dtype=BF16)

    def k_lora_norm(cq_ref, ckv_ref, gq_ref, gkv_ref, qn_ref, kvn_ref):
        qn_ref[...] = _rms_fwd(cq_ref[...], gq_ref[...]).astype(BF16)
        kvn_ref[...] = _rms_fwd(ckv_ref[...], gkv_ref[...]).astype(BF16)

    qn, kvn = _rows("lora_norm", k_lora_norm, [c_q, c_kv], [sm["q_norm_g"], sm["kv_norm_g"]],
                    [(Q_LORA, BF16), (KV_LORA, BF16)])

    def epi_q(acc, xs, outs):
        c, sa, sb = xs[0][...], xs[1][...], xs[2][...]
        for h in range(MLA_HEADS):
            lo = h * HEAD_PAD
            outs[0][:, lo:lo + HEAD_PAD] = (_rope(acc[:, lo:lo + HEAD_PAD], c, sa, sb, 1.0) * qk_scale).astype(BF16)

    q_att, = _mm("q_up", qn, wl["w_uq"], "nn", [(HW, BF16)], epi_q, row_x=[rope_c, rope_sa, rope_sb], tn=HW)

    def epi_kv(acc, xs, outs):
        kr = _rope(xs[0][...], xs[1][...], xs[2][...], xs[3][...], 1.0)
        kr = kr + _neg_ones(kr.shape, STAT_COL_QK)
        vconst = _neg_ones(kr.shape, STAT_COL_V)
        for h in range(MLA_HEADS):
            lo = h * HEAD_PAD
            outs[0][:, lo:lo + HEAD_PAD] = (acc[:, lo:lo + HEAD_PAD] + kr).astype(BF16)
            outs[1][:, lo:lo + HEAD_PAD] = (acc[:, HW + lo:HW + lo + HEAD_PAD] + vconst).astype(BF16)

    k_att, v_att = _mm("kv_up", kvn, wl["w_ukv"], "nn", [(HW, BF16), (HW, BF16)], epi_kv,
                       row_x=[kr_raw, rope_c, rope_sa, rope_sb], tn=2 * HW)

    o_att, q_aug = _flash_fwd(q_att, k_att, v_att)
    wl.update(weight_fns[2](o_att))

    def epi_merge(acc, xs, outs):
        for rows in _row_chunks(acc.shape[0]):
            mo = acc[rows, :]
            g0, g1 = xs[0][rows, 0:D_MODEL].astype(F32), xs[0][rows, D_MODEL:].astype(F32)
            outs[0][rows, :] = mo.astype(BF16)
            outs[1][rows, :] = (g0 * xs[1][rows, :].astype(F32) + g1 * mo).astype(BF16)

    mla_out, merged = _mm("mla_out_merge", o_att, wl["w_mla_out"], "nn", [(D_MODEL, BF16), (D_MODEL, BF16)], epi_merge,
                          row_x=[gates, conv_out], tn=D_MODEL)

    def epi_res_norm(acc, xs, outs):
        h = xs[0][...] + acc[...]
        outs[0][...] = h
        outs[1][...] = _rms_fwd(h, xs[1][...]).astype(BF16)

    h1, u2 = _mm("mix_out", merged, wl["w_out"], "nn", [(D_MODEL, F32), (D_MODEL, BF16)], epi_res_norm,
                 row_x=[x], vec_x=[sm["norm_xattn_g"]], tn=D_MODEL)

    xscale = X_HEAD_DIM ** -0.5

    def epi_scale(acc, xs, outs):
        outs[0][...] = (acc[...] * xscale).astype(BF16)

    xq, = _mm("xattn_q", u2, wl["w_xq"], "nn", [(X_HEADS * X_HEAD_DIM, BF16)], epi_scale)

    def k_mem_norm(m_ref, g_ref, o_ref):
        o_ref[...] = _rms_fwd(m_ref[...], g_ref[...]).astype(BF16)

    mem_n, = _rows("mem_norm", k_mem_norm, [mem], [sm["norm_mem_g"]], [(D_MODEL, BF16)])
    kvx = _mm_plain("xattn_kv", mem_n, wl["w_xkv"], "nn", dtype=BF16)
    ox = _xattn_fwd(xq, kvx)
    h2, u3 = _mm("xattn_out", ox, wl["w_xo"], "nn", [(D_MODEL, F32), (D_MODEL, BF16)], epi_res_norm,
                 row_x=[h1], vec_x=[sm["norm_mlp_g"]], tn=D_MODEL)

    def epi_relu2(acc, xs, outs):
        r = jnp.maximum(acc[...], 0.0)
        outs[0][...] = (r * r).astype(BF16)

    hid, = _mm("mlp_up", u3, wl["w_mlp1"], "nn", [(D_FF, BF16)], epi_relu2)

    def epi_final(acc, xs, outs):
        g = xs[2][...]
        for rows in _row_chunks(acc.shape[0]):
            h = xs[0][rows, :] + acc[rows, :]
            e = _rms_fwd(h, g) - xs[1][rows, :]
            part = 0.5 * jnp.sum(jnp.mean(e * e, axis=-1, keepdims=True), axis=0, keepdims=True)
            dh, dg = _rms_bwd(h, g, e * (1.0 / D_MODEL))
            outs[0][rows, :] = dh
            outs[1][rows, :] = dh.astype(BF16)
            _accum(outs[2], jnp.broadcast_to(part, outs[2].shape), first=rows.start == 0)
            _accum(outs[3], dg, first=rows.start == 0)

    dh3, dh3b, loss, g_final = _mm("mlp_down_loss", hid, wl["w_mlp2"], "nn", [(D_MODEL, F32), (D_MODEL, BF16)], epi_final,
                                   row_x=[h2, target], vec_x=[sm["final_norm_g"]], sums=[(1, LANES), (1, D_MODEL)],
                                   tn=D_MODEL, tk=1024)

    def epi_drelu2(acc, xs, outs):
        outs[0][...] = (acc[...] * (2.0 * jnp.sqrt(xs[0][...].astype(F32)))).astype(BF16)

    da1, = _mm("mlp_down_dx", dh3b, wl["w_mlp2"], "nt", [(D_FF, BF16)], epi_drelu2, tile_x=[hid])
    g_mlp2 = _mm_plain("mlp_down_dw", hid, dh3b, "tn")
    g_mlp1 = _mm_plain("mlp_up_dw", u3, da1, "tn")

    def epi_norm_bwd(acc, xs, outs):
        for rows in _row_chunks(acc.shape[0]):
            dx, dg = _rms_bwd(xs[0][rows, :], xs[2][...], acc[rows, :])
            dh = xs[1][rows, :] + dx
            outs[0][rows, :] = dh
            if len(outs) == 3:
                outs[1][rows, :] = dh.astype(BF16)
            _accum(outs[-1], dg, first=rows.start == 0)

    def dx_norm_bwd(name, dy, w, xin, dres, vecs, with_bf16=True):
        outs = [(D_MODEL, F32), (D_MODEL, BF16)] if with_bf16 else [(D_MODEL, F32)]
        return _mm(name, dy, w, "nt", outs, epi_norm_bwd, row_x=[xin, dres], vec_x=vecs, sums=[(1, D_MODEL)],
                   tn=D_MODEL, tk=_pick(dy.shape[1], (1024, 768, 512)))

    dh2, dh2b, g_norm_mlp = dx_norm_bwd("mlp_up_dx_norm", da1, wl["w_mlp1"], h2, dh3, [sm["norm_mlp_g"]])

    dox = _mm_plain("xattn_out_dx", dh2b, wl["w_xo"], "nt", dtype=BF16)
    g_xo = _mm_plain("xattn_out_dw", ox, dh2b, "tn")
    dxq, dkvx = _xattn_bwd(xq, kvx, dox)
    g_xq = _mm_plain("xattn_q_dw", u2, dxq, "tn")
    g_xkv = _mm_plain("xattn_kv_dw", mem_n, dkvx, "tn")
    dmem_n = _mm_plain("xattn_kv_dx", dkvx, wl["w_xkv"], "nt")

    def k_mem_bwd(m_ref, d_ref, g_ref, dg_ref):
        _, dg = _rms_bwd(m_ref[...], g_ref[...], d_ref[...])
        _accum(dg_ref, dg)

    g_norm_mem, = _rows("mem_norm_bwd", k_mem_bwd, [mem, dmem_n], [sm["norm_mem_g"]], [], [(1, D_MODEL)])
    token = early_grads_fn(dict(w_mlp1=g_mlp1, w_mlp2=g_mlp2, w_xo=g_xo, w_xq=g_xq, w_xkv=g_xkv))
    dh1, dh1b, g_norm_xattn = dx_norm_bwd("xattn_q_dx_norm", dxq, wl["w_xq"], h1, dh2, [sm["norm_xattn_g"], token])

    dmerged = _mm_plain("mix_out_dx", dh1b, wl["w_out"], "nt")
    g_out = _mm_plain("mix_out_dw", merged, dh1b, "tn")

    def k_merge_bwd(dm_ref, g_ref, co_ref, mo_ref, dco_ref, dmo_ref, dgl_ref):
        dm = dm_ref[...]
        g0, g1 = g_ref[:, 0:D_MODEL].astype(F32), g_ref[:, D_MODEL:].astype(F32)
        dco_ref[...] = (dm * g0).astype(BF16)
        dmo_ref[...] = (dm * g1).astype(BF16)
        dgl_ref[:, 0:D_MODEL] = (dm * co_ref[...].astype(F32) * g0 * (1.0 - g0)).astype(BF16)
        dgl_ref[:, D_MODEL:] = (dm * mo_ref[...].astype(F32) * g1 * (1.0 - g1)).astype(BF16)

    dconv_out, dmla_out, dgl = _rows("merge_bwd", k_merge_bwd, [dmerged, gates, conv_out, mla_out], [],
                                     [(D_MODEL, BF16), (D_MODEL, BF16), (2 * D_MODEL, BF16)], tile=256)

    def epi_do(acc, xs, outs):
        for h in range(MLA_HEADS):
            lo = h * HEAD_PAD
            do = acc[:, lo:lo + HEAD_PAD]
            delta = jnp.sum(do * xs[0][:, lo:lo + HEAD_PAD].astype(F32), axis=-1, keepdims=True)
            outs[0][:, lo:lo + HEAD_PAD] = _put_stats(do.astype(BF16), delta, STAT_COL_V)

    do_aug, = _mm("mla_out_dx", dmla_out, wl["w_mla_out"], "nt", [(HW, BF16)], epi_do, row_x=[o_att], tn=HW)
    g_mla_out = _mm_plain("mla_out_dw", o_att, dmla_out, "tn")
    dq_att, dk_att, dv_att = _flash_bwd(q_aug, k_att, v_att, do_aug)

    def k_rope_bwd(dq_ref, dk_ref, dv_ref, c_ref, sa_ref, sb_ref, dqr_ref, dkv_ref, dkr_ref):
        c, sa, sb = c_ref[...], sa_ref[...], sb_ref[...]
        lane = lax.broadcasted_iota(jnp.int32, c.shape, 1)
        nope = (lane < QK_NOPE).astype(F32)
        ropem = ((lane >= QK_NOPE) & (lane < QK_NOPE + QK_ROPE)).astype(F32)
        dkr = jnp.zeros(c.shape, F32)
        for h in range(MLA_HEADS):
            lo = h * HEAD_PAD
            dqr_ref[:, lo:lo + HEAD_PAD] = (_rope(dq_ref[:, lo:lo + HEAD_PAD], c, sa, sb, -1.0) * qk_scale).astype(BF16)
            dk = dk_ref[:, lo:lo + HEAD_PAD]
            dkv_ref[:, lo:lo + HEAD_PAD] = (dk * nope).astype(BF16)
            dkr = dkr + dk
        dkv_ref[:, HW:] = dv_ref[...]
        dkr_ref[...] = (_rope(dkr * ropem, c, sa, sb, -1.0) * ropem).astype(BF16)

    dq_raw, dkv_cat, dkr = _rows("rope_bwd", k_rope_bwd, [dq_att, dk_att, dv_att, rope_c, rope_sa, rope_sb], [],
                                 [(HW, BF16), (2 * HW, BF16), (HEAD_PAD, BF16)], tile=256)
    g_uq = _mm_plain("q_up_dw", qn, dq_raw, "tn")
    dqn = _mm_plain("q_up_dx", dq_raw, wl["w_uq"], "nt")
    g_ukv = _mm_plain("kv_up_dw", kvn, dkv_cat, "tn")
    dkvn = _mm_plain("kv_up_dx", dkv_cat, wl["w_ukv"], "nt")

    def k_lora_bwd(cq_ref, ckv_ref, dqn_ref, dkvn_ref, gq_ref, gkv_ref, dcq_ref, dckv_ref, dgq_ref, dgkv_ref):
        dcq, dgq = _rms_bwd(cq_ref[...], gq_ref[...], dqn_ref[...])
        dckv, dgkv = _rms_bwd(ckv_ref[...], gkv_ref[...], dkvn_ref[...])
        dcq_ref[...] = dcq.astype(BF16)
        dckv_ref[...] = dckv.astype(BF16)
        _accum(dgq_ref, dgq)
        _accum(dgkv_ref, dgkv)

    dc_q, dc_kv, g_q_norm, g_kv_norm = _rows("lora_norm_bwd", k_lora_bwd, [c_q, c_kv, dqn, dkvn],
                                              [sm["q_norm_g"], sm["kv_norm_g"]], [(Q_LORA, BF16), (KV_LORA, BF16)],
                                              [(1, Q_LORA), (1, KV_LORA)])

    dz3 = _mm_plain("conv_out_dx", dconv_out, wl["w_conv_out"], "nt")
    g_conv_out = _mm_plain("conv_out_dw", z3, dconv_out, "tn")
    dz1, g_ln_g, g_ln_b, g_conv_b = _conv_bwd_norm(dz3, z1, sm["conv_ln_g"], sm["conv_ln_b"])
    dconv_in, g_conv_w = _conv_bwd_taps(dz1, z0, conv_in, conv_w)

    dproj = jnp.concatenate([dconv_in, dgl, dc_q, dkr, dc_kv], axis=1)
    g_in = _mm_plain("proj_dw", u1, dproj, "tn")
    token = late_grads_fn(dict(w_in=g_in, conv_w=g_conv_w[:CONV_WIDTH], w_conv_out=g_conv_out, w_uq=g_uq, w_ukv=g_ukv,
                               w_mla_out=g_mla_out, w_out=g_out))
    grad_x, g_norm_mix = dx_norm_bwd("proj_dx_norm", dproj, w_in, x, dh1, [sm["norm_mix_g"], token], with_bf16=False)

    small = dict(norm_mix_g=g_norm_mix, conv_b=g_conv_b, conv_ln_g=g_ln_g, conv_ln_b=g_ln_b, q_norm_g=g_q_norm,
                 kv_norm_g=g_kv_norm, norm_xattn_g=g_norm_xattn, norm_mem_g=g_norm_mem, norm_mlp_g=g_norm_mlp,
                 final_norm_g=g_final)
    return loss, grad_x, small


def _shard(a, k, axis):
    n = a.shape[axis] // N_CHIPS
    return lax.slice_in_dim(a, k * n, (k + 1) * n, axis=axis)


def _pack_small(grads, loss):
    flat = jnp.concatenate([grads[n].reshape(-1) for n in SMALL] + [loss.reshape(-1)[:1]])
    rows = -(-flat.shape[0] // (8 * LANES)) * 8
    return jnp.pad(flat, (0, rows * LANES - flat.shape[0])).reshape(rows, LANES)


def _pack_groups(shapes, names):
    groups = {}
    for n in names:
        groups.setdefault(shapes[n][1], []).append(n)
    return groups


def _pad_rows(a, mult):
    return jnp.pad(a, ((0, -a.shape[0] % mult), (0, 0)))


def _pack_grads(grads, shapes, names):
    packs = []
    for width, group in _pack_groups(shapes, names).items():
        per_chip = [jnp.concatenate([_pad_rows(_shard(grads[n], k, SHARD_AXIS[n]).astype(BF16), PACK_ROW_ALIGN) for n in group])
                    for k in range(N_CHIPS)]
        rows = per_chip[0].shape[0]
        packs.append(jnp.stack(per_chip).reshape(N_CHIPS, 2, rows // 2, width))
    return packs


def _unpack_grads(fulls, shapes, names):
    out = {}
    for full, group in zip(fulls, _pack_groups(shapes, names).values()):
        flat, at = full.reshape(-1, full.shape[-1]), 0
        for n in group:
            rows = shapes[n][0]
            out[n] = flat[at:at + rows]
            at += rows + (-rows % PACK_ROW_ALIGN)
    return out


def _unpack(flat, names, shapes):
    out, at = {}, 0
    for n in names:
        size = math.prod(shapes[n])
        out[n] = flat[at:at + size].reshape(shapes[n])
        at += size
    return out, at


def kernel(x, mem, positions, norm_mix_g, w_in, conv_w, conv_b, conv_ln_g, conv_ln_b, w_conv_out, q_norm_g, w_uq, kv_norm_g, w_ukv, w_mla_out, w_out, norm_xattn_g, norm_mem_g, w_xq, w_xkv, w_xo, norm_mlp_g, w_mlp1, w_mlp2, final_norm_g, loss_target, m_norm_mix_g, m_w_in, m_conv_w, m_conv_b, m_conv_ln_g, m_conv_ln_b, m_w_conv_out, m_q_norm_g, m_w_uq, m_kv_norm_g, m_w_ukv, m_w_mla_out, m_w_out, m_norm_xattn_g, m_norm_mem_g, m_w_xq, m_w_xkv, m_w_xo, m_norm_mlp_g, m_w_mlp1, m_w_mlp2, m_final_norm_g, v_norm_mix_g, v_w_in, v_conv_w, v_conv_b, v_conv_ln_g, v_conv_ln_b, v_w_conv_out, v_q_norm_g, v_w_uq, v_kv_norm_g, v_w_ukv, v_w_mla_out, v_w_out, v_norm_xattn_g, v_norm_mem_g, v_w_xq, v_w_xkv, v_w_xo, v_norm_mlp_g, v_w_mlp1, v_w_mlp2, v_final_norm_g):
    args = dict(locals())
    w = {n: args[n] for n in WEIGHTS}
    m = {n: args["m_" + n] for n in WEIGHTS}
    v = {n: args["v_" + n] for n in WEIGHTS}

    shards = [w[n][0].astype(F32 if n == "conv_w" else BF16) for n in BIG]
    bounds = (0,) + WEIGHT_WAITS + (len(BIG),)
    spans = [slice(lo, hi) for lo, hi in zip(bounds[:-1], bounds[1:])]
    sem_pairs, shards_thru, lands_thru, token = _gather_start(shards, [list(range(len(BIG)))[sp] for sp in spans])

    def unshard(n, g):
        ax = SHARD_AXIS[n]
        return jnp.moveaxis(g, 0, ax).reshape(g.shape[1:1 + ax] + (N_CHIPS * g.shape[1 + ax],) + g.shape[2 + ax:])

    def wait_fn(i):
        def fn(after):
            lands = _gather_wait(f"gather_weights_wait_{i}", sem_pairs[i], shards_thru[spans[i]], lands_thru[spans[i]], after)
            full = {n: unshard(n, g) for n, g in zip(BIG[spans[i]], lands)}
            return (_layout_w_in(full["w_in"]), full["conv_w"]) if i == 0 else _layout_weights(full)
        return fn

    sm = {n: w[n].reshape(1, -1) for n in SMALL}
    sm["norm_mix_g"] = sm["norm_mix_g"] + token[0, 0]

    shapes = {n: w[n].shape[1:] if n in BIG else w[n].shape for n in WEIGHTS}
    late_names = [n for n in BIG if n not in EARLY_GRADS]
    inflight = {}

    def send_grads(tag, names):
        def fn(g):
            packs = _pack_grads(_unlayout_grads(g), shapes, names)
            got = _pair_exchange(f"grad_pair_exchange_{tag}", packs)
            pairs = [_pair_add(f"grad_pair_add_{tag}_{i}", p, r) for i, (p, r) in enumerate(zip(packs, got))]
            *inflight[tag], token = _chip_exchange_start(f"grad_chip_exchange_{tag}_start", pairs)
            return token
        return fn

    loss, grad_x, g_small = _local_step(x[0], mem[0], positions, loss_target[0], [wait_fn(i) for i in range(3)],
                                        send_grads("early", EARLY_GRADS), send_grads("late", late_names), sm)

    halves, counts = [], {}
    for tag in ("late", "early"):
        own, got = _chip_exchange_wait(f"grad_chip_exchange_{tag}_wait", *inflight[tag], grad_x)
        halves += [_chip_add(f"grad_chip_add_{tag}_{i}", p, g) for i, (p, g) in enumerate(zip(own, got))]
        counts[tag] = len(own)
    fulls = _pair_share(halves)
    g_sum = _unpack_grads(fulls[:counts["late"]], shapes, late_names)
    g_sum.update(_unpack_grads(fulls[counts["late"]:], shapes, EARLY_GRADS))
    small_flat = _sum_over_devices(_pack_small(g_small, loss)).reshape(-1)
    g_small, at = _unpack(small_flat, SMALL, shapes)
    g_sum.update(g_small)
    loss_sum = small_flat[at]

    out_g, out_d, out_m, out_v = [], [], [], []
    for n in WEIGHTS:
        g, d, nm, nv = _adamw("adamw_" + n, w[n], g_sum[n], m[n], v[n])
        out_g.append(g)
        out_d.append(d)
        out_m.append(nm)
        out_v.append(nv)
    return (loss_sum, grad_x[None], *out_g, *out_d, *out_m, *out_v)
```

```python
import functools
import math

import jax
import jax.numpy as jnp
from jax import lax
from jax.experimental import pallas as pl
from jax.experimental.pallas import tpu as pltpu

F32 = jnp.float32
BF16 = jnp.bfloat16
MESH = pl.DeviceIdType.MESH

D_MODEL = 1024
CONV_CH = 512
CONV_WIDTH = 31
MLA_HEADS = 8
QK_NOPE = 64
QK_ROPE = 32
V_DIM = 64
Q_LORA = 384
KV_LORA = 256
MEM_LEN = 256
X_HEADS = 4
X_HEAD_DIM = 128
D_FF = 4096
ROPE_THETA = 10000.0
EPS = 1e-6
HEAD_PAD = 128
STAT_COL_QK = QK_NOPE + QK_ROPE
STAT_COL_V = V_DIM
HALO = 32
N_CHIPS = 4
LANES = 128

ADAM_LR = 0.001
ADAM_B1 = 0.9
ADAM_B2 = 0.999
ADAM_EPS = 1e-08
ADAM_WD = 0.01
ADAM_STEP = 10

VMEM_LIMIT = 52 * 1024 * 1024
ROW_TILES = (1024, 512, 256, 128, 64, 32, 16)
PACK_ROW_ALIGN = 32
N_DEV = 8
NEG = -1e30

BIG = ["w_in", "conv_w", "w_conv_out", "w_uq", "w_ukv", "w_mla_out", "w_out", "w_xq", "w_xkv", "w_xo", "w_mlp1", "w_mlp2"]
WEIGHT_WAITS = (2, 5)
SHARD_AXIS = {"w_in": 1, "w_conv_out": 1, "w_uq": 1, "w_ukv": 1, "w_mla_out": 1, "w_out": 0, "w_xq": 0, "w_xkv": 0,
              "w_xo": 1, "w_mlp1": 1, "w_mlp2": 0, "conv_w": 1}
EARLY_GRADS = ["w_mlp1", "w_mlp2", "w_xkv", "w_xq", "w_xo"]
SMALL = ["norm_mix_g", "conv_b", "conv_ln_g", "conv_ln_b", "q_norm_g", "kv_norm_g", "norm_xattn_g", "norm_mem_g",
         "norm_mlp_g", "final_norm_g"]
WEIGHTS = ["norm_mix_g", "w_in", "conv_w", "conv_b", "conv_ln_g", "conv_ln_b", "w_conv_out", "q_norm_g", "w_uq",
           "kv_norm_g", "w_ukv", "w_mla_out", "w_out", "norm_xattn_g", "norm_mem_g", "w_xq", "w_xkv", "w_xo",
           "norm_mlp_g", "w_mlp1", "w_mlp2", "final_norm_g"]


def _pick(n, prefs):
    for p in prefs:
        if n % p == 0:
            return p
    return n


def _params(sem):
    return pltpu.CompilerParams(dimension_semantics=sem, vmem_limit_bytes=VMEM_LIMIT)


_DIMS = {"nn": (((1,), (0,)), ((), ())), "nt": (((1,), (1,)), ((), ())), "tn": (((0,), (0,)), ((), ()))}


def _mm(name, a, b, mode, outs, epi, row_x=(), tile_x=(), vec_x=(), sums=(), tm=None, tn=None, tk=None, b_cols=None):
    if mode == "nn":
        (M, K), (_, N) = a.shape, b.shape
        if b_cols is not None:
            N = b_cols[1]
    elif mode == "nt":
        (M, K), (N, _) = a.shape, b.shape
    else:
        (K, M), (_, N) = a.shape, b.shape
    tm = tm or _pick(M, (1024, 512, 384, 256, 128))
    tn = tn or _pick(N, (1024, 768, 512, 384, 256, 128))
    tk = tk or _pick(K, (2048, 1920, 1024, 768, 512, 384, 256, 128))
    nk = K // tk
    rows_inner = nk == 1 and N // tn > 1
    grid = (N // tn, M // tm, nk) if rows_inner else (M // tm, N // tn, nk)

    def spec(shape, f):
        return pl.BlockSpec(shape, (lambda j, i, k: f(i, j, k)) if rows_inner else f)

    b_off = 0
    if b_cols is not None:
        assert mode == "nn" and b_cols[0] % tn == 0, (name, b_cols, tn)
        b_off = b_cols[0] // tn
    a_spec = spec((tk, tm), lambda i, j, k: (k, i)) if mode == "tn" else spec((tm, tk), lambda i, j, k: (i, k))
    b_spec = spec((tn, tk), lambda i, j, k: (j, k)) if mode == "nt" else spec((tk, tn), lambda i, j, k: (k, j + b_off))
    in_specs = [a_spec, b_spec]
    in_specs += [spec((tm, r.shape[1]), lambda i, j, k: (i, 0)) for r in row_x]
    in_specs += [spec((tm, tn), lambda i, j, k: (i, j)) for _ in tile_x]
    in_specs += [spec(v.shape, lambda i, j, k: (0, 0)) for v in vec_x]
    out_specs, out_shape = [], []
    for w, dt in outs:
        if tn == N:
            out_specs.append(spec((tm, w), lambda i, j, k: (i, 0)))
        else:
            assert w == N, (name, w, N)
            out_specs.append(spec((tm, tn), lambda i, j, k: (i, j)))
        out_shape.append(jax.ShapeDtypeStruct((M, w), dt))
    for shp in sums:
        assert tn == N and not rows_inner, name
        out_specs.append(spec(shp, lambda i, j, k: (0, 0)))
        out_shape.append(jax.ShapeDtypeStruct(shp, F32))
    nx = len(row_x) + len(tile_x) + len(vec_x)
    dims = _DIMS[mode]

    def body(a_ref, b_ref, *rest):
        x_refs, out_refs, acc_ref = rest[:nx], rest[nx:nx + len(outs) + len(sums)], rest[-1]
        av, bv = a_ref[...], b_ref[...]
        if av.dtype != BF16:
            av = av.astype(BF16)
        if bv.dtype != BF16:
            bv = bv.astype(BF16)
        prod = lax.dot_general(av, bv, dims, preferred_element_type=F32)
        if nk == 1:
            acc_ref[...] = prod
            epi(acc_ref, x_refs, out_refs)
        else:
            k = pl.program_id(2)

            @pl.when(k == 0)
            def _():
                acc_ref[...] = prod

            @pl.when(k > 0)
            def _():
                acc_ref[...] += prod

            @pl.when(k == nk - 1)
            def _():
                epi(acc_ref, x_refs, out_refs)

    res = pl.pallas_call(
        body, name=name, grid=grid, in_specs=in_specs, out_specs=out_specs, out_shape=out_shape,
        scratch_shapes=[pltpu.VMEM((tm, tn), F32)],
        compiler_params=_params(("arbitrary",) * 3 if sums else ("parallel", "parallel", "arbitrary")),
    )(a, b, *row_x, *tile_x, *vec_x)
    return res


def _epi_store(acc_ref, x_refs, out_refs):
    for o in out_refs:
        o[...] = acc_ref[...].astype(o.dtype)


def _mm_plain(name, a, b, mode, dtype=F32, **kw):
    n = kw["b_cols"][1] if kw.get("b_cols") else (b.shape[0] if mode == "nt" else b.shape[1])
    return _mm(name, a, b, mode, [(n, dtype)], _epi_store, **kw)[0]


def _rows(name, body, row_ins, vec_ins, row_outs, acc_outs=(), tile=512):
    S = row_ins[0].shape[0]
    t = _pick(S, (tile, 256, 128, 64, 32, 16, 8))
    in_specs = [pl.BlockSpec((t, r.shape[1]), lambda i: (i, 0)) for r in row_ins]
    in_specs += [pl.BlockSpec(v.shape, lambda i: (0, 0)) for v in vec_ins]
    out_specs = [pl.BlockSpec((t, w), lambda i: (i, 0)) for w, _ in row_outs]
    out_specs += [pl.BlockSpec(shp, lambda i: (0, 0)) for shp in acc_outs]
    out_shape = [jax.ShapeDtypeStruct((S, w), dt) for w, dt in row_outs]
    out_shape += [jax.ShapeDtypeStruct(shp, F32) for shp in acc_outs]
    sem = ("arbitrary",) if acc_outs else ("parallel",)
    return pl.pallas_call(
        functools.partial(body), name=name, grid=(S // t,), in_specs=in_specs, out_specs=out_specs,
        out_shape=out_shape, compiler_params=_params(sem),
    )(*row_ins, *vec_ins)


def _accum(ref, val, first=True):
    if first:
        @pl.when(pl.program_id(0) == 0)
        def _():
            ref[...] = jnp.zeros_like(ref)

    ref[...] += val


EPILOGUE_ROWS = 256


def _row_chunks(n):
    step = min(EPILOGUE_ROWS, n)
    return [slice(r, r + step) for r in range(0, n, step)]


def _colsum(v):
    return jnp.sum(v, axis=0, keepdims=True)


def _rms_fwd(x, g):
    r = lax.rsqrt(jnp.mean(x * x, axis=-1, keepdims=True) + EPS)
    return x * r * g


def _rms_bwd(x, g, du):
    r = lax.rsqrt(jnp.mean(x * x, axis=-1, keepdims=True) + EPS)
    xn = x * r
    gdu = du * g
    dx = r * (gdu - xn * jnp.mean(xn * gdu, axis=-1, keepdims=True))
    return dx, _colsum(du * xn)


def _sigmoid(v):
    return 1.0 / (1.0 + jnp.exp(-v))


def _rope(v, c, sa, sb, sign):
    return v * c + sign * (pltpu.roll(v, HEAD_PAD - QK_ROPE // 2, 1) * sa + pltpu.roll(v, QK_ROPE // 2, 1) * sb)


def _split3(v):
    hi = v.astype(BF16)
    r1 = v - hi.astype(F32)
    mid = r1.astype(BF16)
    lo = (r1 - mid.astype(F32)).astype(BF16)
    return hi, mid, lo


def _put_stats(base, stat, col):
    hi, mid, lo = _split3(stat)
    lane = lax.broadcasted_iota(jnp.int32, base.shape, 1)
    out = jnp.where(lane == col, hi, base)
    out = jnp.where(lane == col + 1, mid, out)
    return jnp.where(lane == col + 2, lo, out)


def _neg_ones(shape, col):
    lane = lax.broadcasted_iota(jnp.int32, shape, 1)
    return jnp.where((lane >= col) & (lane < col + 3), -1.0, 0.0).astype(F32)


def _shifted(ext, t):
    p = ext.shape[0]
    for b in range(8):
        rb = ext if b == 0 else pltpu.roll(ext, p - b, 0)
        for a in range(HALO // 8 + 1):
            if 8 * a + b <= HALO:
                yield 8 * a + b, rb[8 * a:8 * a + t]


def _conv_fwd(z0, conv_w, conv_b, ln_g, ln_b):
    S, C = z0.shape
    t = _pick(S, (512, 256, 128, 64, 32))
    per = t // HALO

    def body(cur_ref, prev_ref, w_ref, b_ref, g_ref, beta_ref, z1_ref, z3_ref, ext_ref):
        i = pl.program_id(0)
        ext_ref[0:HALO, :] = jnp.where(i > 0, prev_ref[...], 0.0)
        ext_ref[HALO:, :] = cur_ref[...]
        ext = ext_ref[...]
        acc = jnp.zeros((t, C), F32)
        for d, win in _shifted(ext, t):
            k = d - (HALO - CONV_WIDTH + 1)
            if 0 <= k < CONV_WIDTH:
                acc = acc + win * w_ref[k:k + 1, :]
        z1 = acc + b_ref[...]
        z1_ref[...] = z1
        mu = jnp.mean(z1, axis=-1, keepdims=True)
        zc = z1 - mu
        rs = lax.rsqrt(jnp.mean(zc * zc, axis=-1, keepdims=True) + EPS)
        z2 = zc * rs * g_ref[...] + beta_ref[...]
        z3_ref[...] = (z2 * _sigmoid(z2)).astype(BF16)

    vec = lambda v: pl.BlockSpec(v.shape, lambda i: (0, 0))
    return pl.pallas_call(
        body, name="conv_fwd", grid=(S // t,),
        in_specs=[pl.BlockSpec((t, C), lambda i: (i, 0)),
                  pl.BlockSpec((HALO, C), lambda i: (jnp.maximum(i * per - 1, 0), 0)),
                  vec(conv_w), vec(conv_b), vec(ln_g), vec(ln_b)],
        out_specs=[pl.BlockSpec((t, C), lambda i: (i, 0)), pl.BlockSpec((t, C), lambda i: (i, 0))],
        out_shape=[jax.ShapeDtypeStruct((S, C), F32), jax.ShapeDtypeStruct((S, C), BF16)],
        scratch_shapes=[pltpu.VMEM((t + HALO, C), F32)],
        compiler_params=_params(("parallel",)),
    )(z0, z0, conv_w, conv_b, ln_g, ln_b)


def _conv_bwd_norm(dz3, z1, ln_g, ln_b):
    C = z1.shape[1]

    def body(dz3_ref, z1_ref, g_ref, beta_ref, dz1_ref, dg_ref, dbeta_ref, dbias_ref):
        z1 = z1_ref[...]
        mu = jnp.mean(z1, axis=-1, keepdims=True)
        zc = z1 - mu
        rs = lax.rsqrt(jnp.mean(zc * zc, axis=-1, keepdims=True) + EPS)
        xh = zc * rs
        z2 = xh * g_ref[...] + beta_ref[...]
        sg = _sigmoid(z2)
        dz2 = dz3_ref[...] * (sg * (1.0 + z2 * (1.0 - sg)))
        dxh = dz2 * g_ref[...]
        dz1 = rs * (dxh - jnp.mean(dxh, axis=-1, keepdims=True) - xh * jnp.mean(dxh * xh, axis=-1, keepdims=True))
        dz1_ref[...] = dz1
        _accum(dg_ref, _colsum(dz2 * xh))
        _accum(dbeta_ref, _colsum(dz2))
        _accum(dbias_ref, _colsum(dz1))

    return _rows("conv_bwd_norm", body, [dz3, z1], [ln_g, ln_b], [(C, F32)], [(1, C)] * 3)


def _conv_bwd_taps(dz1, z0, conv_in, conv_w):
    S, C = z0.shape
    t = _pick(S, (512, 256, 128, 64, 32))
    per = t // HALO
    last = S // HALO - 1
    nt = S // t

    def body(dcur_ref, dnext_ref, zcur_ref, zprev_ref, cin_ref, w_ref, dcin_ref, dw_ref, dext_ref, zext_ref):
        i = pl.program_id(0)
        dcur = dcur_ref[...]
        dext_ref[0:t, :] = dcur
        dext_ref[t:, :] = jnp.where(i < nt - 1, dnext_ref[...], 0.0)
        zext_ref[0:HALO, :] = jnp.where(i > 0, zprev_ref[...], 0.0)
        zext_ref[HALO:, :] = zcur_ref[...]

        @pl.when(i == 0)
        def _():
            dw_ref[...] = jnp.zeros_like(dw_ref)

        dz0 = jnp.zeros((t, C), F32)
        for d, win in _shifted(dext_ref[...], t):
            k = CONV_WIDTH - 1 - d
            if 0 <= k < CONV_WIDTH:
                dz0 = dz0 + win * w_ref[k:k + 1, :]
        for d, win in _shifted(zext_ref[...], t):
            k = d - (HALO - CONV_WIDTH + 1)
            if 0 <= k < CONV_WIDTH:
                dw_ref[k:k + 1, :] += _colsum(dcur * win)
        a = cin_ref[:, 0:C].astype(F32)
        sg = _sigmoid(cin_ref[:, C:2 * C].astype(F32))
        dcin_ref[:, 0:C] = (dz0 * sg).astype(BF16)
        dcin_ref[:, C:2 * C] = (dz0 * a * sg * (1.0 - sg)).astype(BF16)

    return pl.pallas_call(
        body, name="conv_bwd_taps", grid=(nt,),
        in_specs=[pl.BlockSpec((t, C), lambda i: (i, 0)),
                  pl.BlockSpec((HALO, C), lambda i: (jnp.minimum((i + 1) * per, last), 0)),
                  pl.BlockSpec((t, C), lambda i: (i, 0)),
                  pl.BlockSpec((HALO, C), lambda i: (jnp.maximum(i * per - 1, 0), 0)),
                  pl.BlockSpec((t, 2 * C), lambda i: (i, 0)),
                  pl.BlockSpec(conv_w.shape, lambda i: (0, 0))],
        out_specs=[pl.BlockSpec((t, 2 * C), lambda i: (i, 0)), pl.BlockSpec((HALO, C), lambda i: (0, 0))],
        out_shape=[jax.ShapeDtypeStruct((S, 2 * C), BF16), jax.ShapeDtypeStruct((HALO, C), F32)],
        scratch_shapes=[pltpu.VMEM((t + HALO, C), F32), pltpu.VMEM((t + HALO, C), F32)],
        compiler_params=_params(("arbitrary",)),
    )(dz1, dz1, z0, z0, conv_in, conv_w)


def _lower_tri(shape, rows_are_queries):
    row = lax.broadcasted_iota(jnp.int32, shape, 0)
    col = lax.broadcasted_iota(jnp.int32, shape, 1)
    return (col <= row) if rows_are_queries else (row <= col)


HEADS_PER_STEP = 2
FWD_HEADS_PER_STEP = 2
FWD_KEY_TILES = 4


def _flash_specs(S, t, heads):
    w = heads * HEAD_PAD
    blk = pl.BlockSpec((t, w), lambda h, i: (i, h))
    head = pl.BlockSpec((S, w), lambda h, i: (0, h))
    return blk, head


def _head_lanes(g):
    return slice(g * HEAD_PAD, (g + 1) * HEAD_PAD)


def _dot_nt(a, b):
    return lax.dot_general(a, b, _DIMS["nt"], preferred_element_type=F32)


def _dot_nn(a, b):
    return lax.dot_general(a, b, _DIMS["nn"], preferred_element_type=F32)


def _dot_tn(a, b):
    return lax.dot_general(a, b, _DIMS["tn"], preferred_element_type=F32)


def _flash_fwd(q, k, v):
    S = q.shape[0]
    t = _pick(S, (512, 256, 128))

    def body(q_ref, k_ref, v_ref, o_ref, qa_ref, m_ref, acc_ref):
        qi = pl.program_id(1)
        m_ref[...] = jnp.full_like(m_ref, NEG)
        acc_ref[...] = jnp.zeros_like(acc_ref)

        def step(first, tiles, diag):
            width = tiles * t
            rows = pl.ds(pl.multiple_of(first, t), width)
            for g in range(FWD_HEADS_PER_STEP):
                hl = _head_lanes(g)
                s = _dot_nt(q_ref[:, hl], k_ref[rows, hl])
                if diag:
                    row = lax.broadcasted_iota(jnp.int32, s.shape, 0)
                    col = lax.broadcasted_iota(jnp.int32, s.shape, 1)
                    s = jnp.where(col <= row + (tiles - 1) * t, s, NEG)
                m_old = m_ref[g]
                m_new = jnp.maximum(m_old, jnp.max(s, axis=-1, keepdims=True))
                p = jnp.exp(s - m_new).astype(BF16)
                acc_ref[g] = jnp.exp(m_old - m_new) * acc_ref[g] + _dot_nn(p, v_ref[rows, hl])
                m_ref[g] = m_new

        def wide(kb, carry):
            step(kb * (FWD_KEY_TILES * t), FWD_KEY_TILES, False)
            return carry

        full_groups = qi // FWD_KEY_TILES
        lax.fori_loop(0, full_groups, wide, 0)
        for tiles in range(1, min(FWD_KEY_TILES, S // t) + 1):
            @pl.when(qi - full_groups * FWD_KEY_TILES == tiles - 1)
            def _():
                step(full_groups * (FWD_KEY_TILES * t), tiles, True)

        for g in range(FWD_HEADS_PER_STEP):
            hl = _head_lanes(g)
            acc = acc_ref[g]
            l = -acc[:, STAT_COL_V:STAT_COL_V + 1]
            o_ref[:, hl] = (acc / l).astype(BF16)
            qa_ref[:, hl] = _put_stats(q_ref[:, hl], m_ref[g] + jnp.log(l), STAT_COL_QK)

    blk, head = _flash_specs(S, t, FWD_HEADS_PER_STEP)
    return pl.pallas_call(
        body, name="mla_flash_fwd", grid=(MLA_HEADS // FWD_HEADS_PER_STEP, S // t),
        in_specs=[blk, head, head], out_specs=[blk, blk],
        out_shape=[jax.ShapeDtypeStruct(q.shape, BF16), jax.ShapeDtypeStruct(q.shape, BF16)],
        scratch_shapes=[pltpu.VMEM((FWD_HEADS_PER_STEP, t, 1), F32), pltpu.VMEM((FWD_HEADS_PER_STEP, t, HEAD_PAD), F32)],
        compiler_params=_params(("parallel", "arbitrary")),
    )(q, k, v)


def _flash_bwd(qa, k, v, doa):
    S = qa.shape[0]
    t = _pick(S, (1024, 512, 256, 128))
    n = S // t
    half = t // 2

    def body(qa_ref, k_ref, v_ref, do_ref, dq_ref, dk_ref, dv_ref, dk_acc, dv_acc):
        kj = pl.program_id(1)

        @pl.when(kj == 0)
        def _():
            dq_ref[...] = jnp.zeros_like(dq_ref)

        dk_acc[...] = jnp.zeros_like(dk_acc)
        dv_acc[...] = jnp.zeros_like(dv_acc)

        def step(q_first, q_len, keys, diag):
            rows = pl.ds(pl.multiple_of(q_first, q_len), q_len)
            for g in range(HEADS_PER_STEP):
                hl = _head_lanes(g)
                qa, do, kk = qa_ref[rows, hl], do_ref[rows, hl], k_ref[keys, hl]
                st = _dot_nt(kk, qa)
                if diag:
                    st = jnp.where(_lower_tri(st.shape, False), st, NEG)
                pt = jnp.exp(st)
                dst = (pt * _dot_nt(v_ref[keys, hl], do)).astype(BF16)
                dv_acc[keys, hl] += _dot_nn(pt.astype(BF16), do)
                dk_acc[keys, hl] += _dot_nn(dst, qa)
                dq_ref[rows, hl] += _dot_tn(dst, kk)

        def loop(qi, carry):
            step(qi * t, t, slice(0, t), False)
            return carry

        lo, hi = slice(0, half), slice(half, t)
        step(kj * t, half, lo, True)
        step(kj * t + half, half, lo, False)
        step(kj * t + half, half, hi, True)
        lax.fori_loop(kj + 1, n, loop, 0)
        dk_ref[...] = dk_acc[...]
        dv_ref[...] = dv_acc[...].astype(BF16)

    blk, head = _flash_specs(S, t, HEADS_PER_STEP)
    w = HEADS_PER_STEP * HEAD_PAD
    return pl.pallas_call(
        body, name="mla_flash_bwd", grid=(MLA_HEADS // HEADS_PER_STEP, n),
        in_specs=[head, blk, blk, head], out_specs=[head, blk, blk],
        out_shape=[jax.ShapeDtypeStruct(qa.shape, F32), jax.ShapeDtypeStruct(qa.shape, F32), jax.ShapeDtypeStruct(qa.shape, BF16)],
        scratch_shapes=[pltpu.VMEM((t, w), F32), pltpu.VMEM((t, w), F32)],
        compiler_params=_params(("parallel", "arbitrary")),
    )(qa, k, v, doa)


def _xattn_fwd(xq, kvx):
    W = X_HEADS * X_HEAD_DIM

    def body(q_ref, kv_ref, o_ref):
        for h in range(X_HEADS):
            lo = h * X_HEAD_DIM
            s = _dot_nt(q_ref[:, lo:lo + X_HEAD_DIM], kv_ref[:, lo:lo + X_HEAD_DIM])
            p = jnp.exp(s - jnp.max(s, axis=-1, keepdims=True))
            p = p / jnp.sum(p, axis=-1, keepdims=True)
            o_ref[:, lo:lo + X_HEAD_DIM] = _dot_nn(p.astype(BF16), kv_ref[:, W + lo:W + lo + X_HEAD_DIM]).astype(BF16)

    return _rows("xattn_fwd", body, [xq], [kvx], [(W, BF16)])[0]


def _xattn_bwd(xq, kvx, dox):
    W = X_HEADS * X_HEAD_DIM
    scale = X_HEAD_DIM ** -0.5

    def body(q_ref, do_ref, kv_ref, dq_ref, dkv_ref):
        @pl.when(pl.program_id(0) == 0)
        def _():
            dkv_ref[...] = jnp.zeros_like(dkv_ref)

        for h in range(X_HEADS):
            lo = h * X_HEAD_DIM
            q, k = q_ref[:, lo:lo + X_HEAD_DIM], kv_ref[:, lo:lo + X_HEAD_DIM]
            v, do = kv_ref[:, W + lo:W + lo + X_HEAD_DIM], do_ref[:, lo:lo + X_HEAD_DIM]
            s = _dot_nt(q, k)
            p = jnp.exp(s - jnp.max(s, axis=-1, keepdims=True))
            p = p / jnp.sum(p, axis=-1, keepdims=True)
            dp = _dot_nt(do, v)
            ds = (p * (dp - jnp.sum(dp * p, axis=-1, keepdims=True))).astype(BF16)
            dq_ref[:, lo:lo + X_HEAD_DIM] = (_dot_nn(ds, k) * scale).astype(BF16)
            dkv_ref[:, lo:lo + X_HEAD_DIM] += _dot_tn(ds, q)
            dkv_ref[:, W + lo:W + lo + X_HEAD_DIM] += _dot_tn(p.astype(BF16), do)

    return _rows("xattn_bwd", body, [xq, dox], [kvx], [(W, BF16)], [kvx.shape])


def _adamw(name, w, g, m, v):
    c1 = 1.0 / (1.0 - ADAM_B1 ** ADAM_STEP)
    c2 = 1.0 / (1.0 - ADAM_B2 ** ADAM_STEP)
    lead = (0,) * (w.ndim - 2)
    w2 = w.reshape((1,) * (2 - w.ndim) + w.shape) if w.ndim < 2 else w
    m2, v2 = m.reshape(w2.shape), v.reshape(w2.shape)
    g2 = g.reshape(w2.shape[-2:])
    R, C = g2.shape
    t = _pick(R, (256, 128, 64, 32, 16, 8))

    def body(w_ref, g_ref, m_ref, v_ref, go_ref, d_ref, nm_ref, nv_ref):
        g = g_ref[...]
        nm = ADAM_B1 * m_ref[lead] + (1.0 - ADAM_B1) * g
        nv = ADAM_B2 * v_ref[lead] + (1.0 - ADAM_B2) * (g * g)
        go_ref[lead] = g
        d_ref[lead] = -ADAM_LR * ((nm * c1) / (jnp.sqrt(nv * c2) + ADAM_EPS) + ADAM_WD * w_ref[lead])
        nm_ref[lead] = nm
        nv_ref[lead] = nv

    full = pl.BlockSpec((1,) * len(lead) + (t, C), lambda i: lead + (i, 0))
    outs = pl.pallas_call(
        body, name=name, grid=(R // t,), in_specs=[full, pl.BlockSpec((t, C), lambda i: (i, 0)), full, full],
        out_specs=[full] * 4, out_shape=[jax.ShapeDtypeStruct(w2.shape, F32)] * 4, compiler_params=_params(("parallel",)),
    )(w2, g2, m2, v2)
    return [o.reshape(w.shape) for o in outs]


def _place():
    x, y, c = lax.axis_index("x"), lax.axis_index("y"), lax.axis_index("c")
    return x, y, c, [(1 - x, y), (x, 1 - y), (1 - x, 1 - y)]


_ANY = pl.BlockSpec(memory_space=pl.ANY)


_HBM = pl.BlockSpec(memory_space=pltpu.HBM)
_SEM = pl.BlockSpec(memory_space=pltpu.SEMAPHORE)
_SIDE_EFFECT = pltpu.SideEffectType.DATAFLOW_SIDE_EFFECTING


def _gather_copy(src, land, slot, send, recv, k, chip, c):
    return pltpu.make_async_remote_copy(src_ref=src, dst_ref=land.at[slot], send_sem=send.at[k], recv_sem=recv.at[k],
                                        device_id=(chip[0], chip[1], c), device_id_type=MESH)


def _gather_start(shards, groups):
    n, ng = len(shards), len(groups)
    lands = [jnp.broadcast_to(s[None], (N_CHIPS,) + s.shape) for s in shards]

    def body(*refs):
        ins, lnd = refs[:n], refs[n:2 * n]
        sends, recvs = refs[2 * n:2 * n + ng], refs[2 * n + ng:2 * n + 2 * ng]
        token = refs[-1]
        x, y, c, chips = _place()
        for gi, group in enumerate(groups):
            for pos, w in enumerate(group):
                for j, chip in enumerate(chips):
                    _gather_copy(ins[w], lnd[w], 2 * x + y, sends[gi], recvs[gi], 3 * pos + j, chip, c).start()
        token[...] = jnp.zeros_like(token)

    sems = [pltpu.SemaphoreType.DMA((3 * len(g),)) for g in groups]
    res = pl.pallas_call(
        body, name="gather_weights_start",
        out_shape=sems + sems + [pltpu.HBM(a.shape, a.dtype) for a in shards + lands] + [jax.ShapeDtypeStruct((8, LANES), F32)],
        in_specs=[_HBM] * (2 * n),
        out_specs=[_SEM] * (2 * ng) + [_HBM] * (2 * n) + [pl.BlockSpec(memory_space=pltpu.VMEM)],
        input_output_aliases={i: 2 * ng + i for i in range(2 * n)},
        compiler_params=pltpu.CompilerParams(has_side_effects=_SIDE_EFFECT),
    )(*[pltpu.with_memory_space_constraint(a, pltpu.HBM) for a in shards + lands])
    sem_pairs = list(zip(res[:ng], res[ng:2 * ng]))
    return sem_pairs, res[2 * ng:2 * ng + n], res[2 * ng + n:2 * ng + 2 * n], res[-1]


def _gather_wait(name, sem_pair, shards_thru, lands_thru, after):
    m = len(shards_thru)

    def body(*refs):
        ins, lnd = refs[:m], refs[m:2 * m]
        send, recv = refs[2 * m], refs[2 * m + 1]
        x, y, c, chips = _place()
        for pos in range(m):
            for j, chip in enumerate(chips):
                cp = _gather_copy(ins[pos], lnd[pos], 2 * chip[0] + chip[1], send, recv, 3 * pos + j, chip, c)
                cp.wait_send()
                cp.wait_recv()

    res = pl.pallas_call(
        body, name=name,
        out_shape=[pltpu.HBM(a.shape, a.dtype) for a in list(shards_thru) + list(lands_thru)],
        in_specs=[_HBM] * (2 * m) + [_SEM, _SEM, _ANY], out_specs=[_HBM] * (2 * m),
        input_output_aliases={i: i for i in range(2 * m)},
        compiler_params=pltpu.CompilerParams(has_side_effects=_SIDE_EFFECT),
    )(*shards_thru, *lands_thru, *sem_pair, after)
    return res[m:]


def _pair_exchange(name, packs):
    n = len(packs)

    def body(*refs):
        ins, outs, send, recv = refs[:n], refs[n:2 * n], refs[2 * n], refs[2 * n + 1]
        x, y, c, _ = _place()
        cps = []
        for g in range(n):
            cp = pltpu.make_async_remote_copy(src_ref=ins[g].at[:, pl.ds(1 - c, 1)], dst_ref=outs[g], send_sem=send.at[g],
                                              recv_sem=recv.at[g], device_id=(x, y, 1 - c), device_id_type=MESH)
            cp.start()
            cps.append(cp)
        for cp in cps:
            cp.wait()

    return pl.pallas_call(
        body, name=name, in_specs=[_ANY] * n, out_specs=[_ANY] * n,
        out_shape=[jax.ShapeDtypeStruct((N_CHIPS, 1) + p.shape[2:], p.dtype) for p in packs],
        scratch_shapes=[pltpu.SemaphoreType.DMA((n,)), pltpu.SemaphoreType.DMA((n,))],
    )(*packs)


def _chip_copy(src, land, src_slot, dst_slot, send, recv, k, chip, c):
    return pltpu.make_async_remote_copy(src_ref=src.at[src_slot], dst_ref=land.at[dst_slot], send_sem=send.at[k],
                                        recv_sem=recv.at[k], device_id=(chip[0], chip[1], c), device_id_type=MESH)


def _chip_exchange_start(name, parts):
    n = len(parts)
    lands = [lax.empty(p.shape, p.dtype) for p in parts]

    def body(*refs):
        ins, lnd, send, recv, token = refs[:n], refs[n:2 * n], refs[2 * n], refs[2 * n + 1], refs[-1]
        x, y, c, chips = _place()
        for g in range(n):
            for j, chip in enumerate(chips):
                _chip_copy(ins[g], lnd[g], 2 * chip[0] + chip[1], 2 * x + y, send, recv, 3 * g + j, chip, c).start()
        token[...] = jnp.zeros_like(token)

    sems = [pltpu.SemaphoreType.DMA((3 * n,))] * 2
    res = pl.pallas_call(
        body, name=name,
        out_shape=sems + [pltpu.HBM(a.shape, a.dtype) for a in list(parts) + lands] + [jax.ShapeDtypeStruct((8, LANES), F32)],
        in_specs=[_HBM] * (2 * n),
        out_specs=[_SEM] * 2 + [_HBM] * (2 * n) + [pl.BlockSpec(memory_space=pltpu.VMEM)],
        input_output_aliases={i: 2 + i for i in range(2 * n)},
        compiler_params=pltpu.CompilerParams(has_side_effects=_SIDE_EFFECT),
    )(*[pltpu.with_memory_space_constraint(a, pltpu.HBM) for a in list(parts) + lands])
    return res[:2], res[2:2 + n], res[2 + n:2 + 2 * n], res[-1]


def _chip_exchange_wait(name, sems, parts_thru, lands_thru, after):
    n = len(parts_thru)

    def body(*refs):
        ins, lnd, send, recv = refs[:n], refs[n:2 * n], refs[2 * n], refs[2 * n + 1]
        x, y, c, chips = _place()
        for g in range(n):
            for j, chip in enumerate(chips):
                cp = _chip_copy(ins[g], lnd[g], 2 * x + y, 2 * chip[0] + chip[1], send, recv, 3 * g + j, chip, c)
                cp.wait_send()
                cp.wait_recv()

    res = pl.pallas_call(
        body, name=name,
        out_shape=[pltpu.HBM(a.shape, a.dtype) for a in list(parts_thru) + list(lands_thru)],
        in_specs=[_HBM] * (2 * n) + [_SEM, _SEM, _ANY], out_specs=[_HBM] * (2 * n),
        input_output_aliases={i: i for i in range(2 * n)},
        compiler_params=pltpu.CompilerParams(has_side_effects=_SIDE_EFFECT),
    )(*parts_thru, *lands_thru, *sems, after)
    return res[:n], res[n:]


def _pair_share(halves):
    n = len(halves)

    def body(*refs):
        outs, send, recv = refs[n:2 * n], refs[2 * n], refs[2 * n + 1]
        x, y, c, _ = _place()
        cps = []
        for g in range(n):
            cp = pltpu.make_async_remote_copy(src_ref=outs[g].at[c], dst_ref=outs[g].at[c], send_sem=send.at[g],
                                              recv_sem=recv.at[g], device_id=(x, y, 1 - c), device_id_type=MESH)
            cp.start()
            cps.append(cp)
        for g in range(n):
            pltpu.make_async_remote_copy(src_ref=outs[g].at[c], dst_ref=outs[g].at[1 - c], send_sem=send.at[g],
                                         recv_sem=recv.at[g], device_id=(x, y, 1 - c), device_id_type=MESH).wait_recv()
        for cp in cps:
            cp.wait_send()

    return pl.pallas_call(
        body, name="grad_pair_share", in_specs=[_ANY] * n, out_specs=[_ANY] * n,
        out_shape=[jax.ShapeDtypeStruct(h.shape, h.dtype) for h in halves],
        input_output_aliases={g: g for g in range(n)},
        scratch_shapes=[pltpu.SemaphoreType.DMA((n,)), pltpu.SemaphoreType.DMA((n,))],
    )(*halves)


def _sum_over_devices(block):
    R, L = block.shape

    def gather(b_ref, o_ref, send, recv, loc):
        x, y, c, _ = _place()
        lc = pltpu.make_async_copy(b_ref, o_ref.at[4 * x + 2 * y + c], loc)
        lc.start()
        peers = [(1 - x if dx else x, 1 - y if dy else y, 1 - c if dc else c)
                 for dx in (0, 1) for dy in (0, 1) for dc in (0, 1) if dx or dy or dc]
        cps = []
        for j, peer in enumerate(peers):
            cp = pltpu.make_async_remote_copy(src_ref=b_ref, dst_ref=o_ref.at[4 * x + 2 * y + c], send_sem=send.at[j],
                                              recv_sem=recv.at[j], device_id=peer, device_id_type=MESH)
            cp.start()
            cps.append(cp)
        for j, (px, py, pc) in enumerate(peers):
            pltpu.make_async_remote_copy(src_ref=b_ref, dst_ref=o_ref.at[4 * px + 2 * py + pc], send_sem=send.at[j],
                                         recv_sem=recv.at[j], device_id=(px, py, pc), device_id_type=MESH).wait_recv()
        for cp in cps:
            cp.wait_send()
        lc.wait()

    blocks = pl.pallas_call(
        gather, name="small_grads_gather", in_specs=[_ANY], out_specs=_ANY,
        out_shape=jax.ShapeDtypeStruct((N_DEV, R, L), F32),
        scratch_shapes=[pltpu.SemaphoreType.DMA((N_DEV - 1,)), pltpu.SemaphoreType.DMA((N_DEV - 1,)), pltpu.SemaphoreType.DMA],
    )(block)

    def add(b_ref, o_ref):
        total = b_ref[0]
        for d in range(1, N_DEV):
            total = total + b_ref[d]
        o_ref[...] = total

    return pl.pallas_call(add, name="small_grads_add", out_shape=jax.ShapeDtypeStruct((R, L), F32))(blocks)


def _pair_add(name, pack, got):
    _, _, R, C = pack.shape
    t = _pick(R, ROW_TILES)
    c = lax.axis_index("c").astype(jnp.int32).reshape(1)

    def body(c_ref, p_ref, g_ref, o_ref):
        o_ref[...] = (p_ref[...].astype(F32) + g_ref[...].astype(F32)).astype(BF16)

    return pl.pallas_call(
        body, name=name,
        grid_spec=pltpu.PrefetchScalarGridSpec(
            num_scalar_prefetch=1, grid=(N_CHIPS, R // t),
            in_specs=[pl.BlockSpec((None, None, t, C), lambda k, i, c_ref: (k, c_ref[0], i, 0)),
                      pl.BlockSpec((None, None, t, C), lambda k, i, c_ref: (k, 0, i, 0))],
            out_specs=pl.BlockSpec((None, t, C), lambda k, i, c_ref: (k, i, 0))),
        out_shape=jax.ShapeDtypeStruct((N_CHIPS, R, C), BF16), compiler_params=_params(("parallel", "parallel")),
    )(c, pack, got)


def _chip_add(name, own, got):
    _, R, C = own.shape
    t = _pick(R, ROW_TILES)
    x, y, c, _ = _place()
    place = jnp.stack([c, 2 * x + y]).astype(jnp.int32)

    def body(place_ref, own_ref, g1_ref, g2_ref, g3_ref, o_ref):
        o_ref[...] = ((own_ref[...].astype(F32) + g1_ref[...].astype(F32)) + g2_ref[...].astype(F32)) + g3_ref[...].astype(F32)

    def other(d):
        return pl.BlockSpec((None, t, C), lambda i, place_ref: ((place_ref[1] + d) % N_CHIPS, i, 0))

    return pl.pallas_call(
        body, name=name,
        grid_spec=pltpu.PrefetchScalarGridSpec(
            num_scalar_prefetch=1, grid=(R // t,),
            in_specs=[pl.BlockSpec((None, t, C), lambda i, place_ref: (place_ref[1], i, 0)), other(1), other(2), other(3)],
            out_specs=pl.BlockSpec((None, t, C), lambda i, place_ref: (place_ref[0], i, 0))),
        out_shape=jax.ShapeDtypeStruct((2, R, C), F32), compiler_params=_params(("parallel",)),
    )(place, own, got, got, got)


_CUT = (2 * CONV_CH, 2 * CONV_CH + Q_LORA, 2 * CONV_CH + Q_LORA + KV_LORA, 2 * CONV_CH + Q_LORA + KV_LORA + QK_ROPE)
W_IN_CONV = (0, 2 * CONV_CH)
W_IN_GATES = (W_IN_CONV[1], 2 * D_MODEL)
W_IN_CQ = (W_IN_GATES[0] + W_IN_GATES[1], Q_LORA)
W_IN_KR = (W_IN_CQ[0] + W_IN_CQ[1], HEAD_PAD)
W_IN_CKV = (W_IN_KR[0] + W_IN_KR[1], KV_LORA)


def _pad_last(a, n):
    return jnp.pad(a, [(0, 0)] * (a.ndim - 1) + [(0, n - a.shape[-1])])


def _layout_w_in(w_in):
    kr = jnp.pad(w_in[:, _CUT[2]:_CUT[3]], ((0, 0), (QK_NOPE, HEAD_PAD - QK_NOPE - QK_ROPE)))
    return jnp.concatenate([w_in[:, :_CUT[0]], w_in[:, _CUT[3]:], w_in[:, _CUT[0]:_CUT[1]], kr, w_in[:, _CUT[1]:_CUT[2]]], axis=1)


def _layout_weights(w):
    out = dict(w)
    if "w_uq" in w:
        out["w_uq"] = _pad_last(w["w_uq"].reshape(Q_LORA, MLA_HEADS, QK_NOPE + QK_ROPE), HEAD_PAD).reshape(Q_LORA, MLA_HEADS * HEAD_PAD)
    if "w_ukv" in w:
        ukv = w["w_ukv"].reshape(KV_LORA, MLA_HEADS, QK_NOPE + V_DIM)
        uk = _pad_last(ukv[:, :, :QK_NOPE], HEAD_PAD).reshape(KV_LORA, MLA_HEADS * HEAD_PAD)
        uv = _pad_last(ukv[:, :, QK_NOPE:], HEAD_PAD).reshape(KV_LORA, MLA_HEADS * HEAD_PAD)
        out["w_ukv"] = jnp.concatenate([uk, uv], axis=1)
    if "w_mla_out" in w:
        mo = jnp.pad(w["w_mla_out"].reshape(MLA_HEADS, V_DIM, D_MODEL), ((0, 0), (0, HEAD_PAD - V_DIM), (0, 0)))
        out["w_mla_out"] = mo.reshape(MLA_HEADS * HEAD_PAD, D_MODEL)
    return out


def _unlayout_grads(g):
    out = dict(g)
    if "w_in" in g:
        gi = g["w_in"]
        win = lambda w: gi[:, w[0]:w[0] + w[1]]
        kr = gi[:, W_IN_KR[0] + QK_NOPE:W_IN_KR[0] + QK_NOPE + QK_ROPE]
        out["w_in"] = jnp.concatenate([win(W_IN_CONV), win(W_IN_CQ), win(W_IN_CKV), kr, win(W_IN_GATES)], axis=1)
    if "w_uq" in g:
        out["w_uq"] = g["w_uq"].reshape(Q_LORA, MLA_HEADS, HEAD_PAD)[:, :, :QK_NOPE + QK_ROPE].reshape(Q_LORA, -1)
    if "w_ukv" in g:
        gk = g["w_ukv"][:, :MLA_HEADS * HEAD_PAD].reshape(KV_LORA, MLA_HEADS, HEAD_PAD)[:, :, :QK_NOPE]
        gv = g["w_ukv"][:, MLA_HEADS * HEAD_PAD:].reshape(KV_LORA, MLA_HEADS, HEAD_PAD)[:, :, :V_DIM]
        out["w_ukv"] = jnp.concatenate([gk, gv], axis=2).reshape(KV_LORA, -1)
    if "w_mla_out" in g:
        out["w_mla_out"] = g["w_mla_out"].reshape(MLA_HEADS, HEAD_PAD, D_MODEL)[:, :V_DIM].reshape(MLA_HEADS * V_DIM, D_MODEL)
    return out


def _rope_tables(positions):
    half = QK_ROPE // 2
    inv_freq = ROPE_THETA ** (-jnp.arange(half, dtype=F32) / half)
    ang = positions.astype(F32).reshape(-1, 1) * inv_freq
    cos, sin = jnp.cos(ang), jnp.sin(ang)
    S = cos.shape[0]
    z16, z32, z64 = jnp.zeros((S, half), F32), jnp.zeros((S, QK_ROPE), F32), jnp.zeros((S, QK_NOPE), F32)
    c = jnp.concatenate([jnp.ones((S, QK_NOPE), F32), cos, cos, z32], axis=1)
    sa = jnp.concatenate([z64, -sin, z16, z32], axis=1)
    sb = jnp.concatenate([z64, z16, sin, z32], axis=1)
    return c, sa, sb


def _local_step(x, mem, positions, target, weight_fns, early_grads_fn, late_grads_fn, sm):
    S = x.shape[0]
    HW = MLA_HEADS * HEAD_PAD
    rope_c, rope_sa, rope_sb = _rope_tables(positions)
    qk_scale = (QK_NOPE + QK_ROPE) ** -0.5

    def k_rms1(x_ref, g_ref, u_ref):
        u_ref[...] = _rms_fwd(x_ref[...], g_ref[...]).astype(BF16)

    u1, = _rows("rms_mix", k_rms1, [x], [sm["norm_mix_g"]], [(D_MODEL, BF16)])
    w_in, conv_w = weight_fns[0](u1)

    def epi_glu(acc, xs, outs):
        a, gt = acc[:, 0:CONV_CH], acc[:, CONV_CH:2 * CONV_CH]
        outs[0][...] = acc[...].astype(BF16)
        outs[1][...] = a * _sigmoid(gt)

    conv_in, z0 = _mm("proj_conv", u1, w_in, "nn", [(2 * CONV_CH, BF16), (CONV_CH, F32)], epi_glu, b_cols=W_IN_CONV)
    c_q = _mm_plain("proj_cq", u1, w_in, "nn", b_cols=W_IN_CQ)
    c_kv = _mm_plain("proj_ckv", u1, w_in, "nn", b_cols=W_IN_CKV)
    kr_raw = _mm_plain("proj_krope", u1, w_in, "nn", b_cols=W_IN_KR)

    def epi_sigmoid(acc, xs, outs):
        outs[0][...] = _sigmoid(acc[...]).astype(BF16)

    gates, = _mm("proj_gates", u1, w_in, "nn", [(2 * D_MODEL, BF16)], epi_sigmoid, b_cols=W_IN_GATES)

    z1, z3 = _conv_fwd(z0, conv_w, sm["conv_b"], sm["conv_ln_g"], sm["conv_ln_b"])
    wl = weight_fns[1](z1)
    conv_out = _mm_plain("conv_out", z3, wl["w_conv_out"], "nn", dtype=BF16)

    def k_lora_norm(cq_ref, ckv_ref, gq_ref, gkv_ref, qn_ref, kvn_ref):
        qn_ref[...] = _rms_fwd(cq_ref[...], gq_ref[...]).astype(BF16)
        kvn_ref[...] = _rms_fwd(ckv_ref[...], gkv_ref[...]).astype(BF16)

    qn, kvn = _rows("lora_norm", k_lora_norm, [c_q, c_kv], [sm["q_norm_g"], sm["kv_norm_g"]],
                    [(Q_LORA, BF16), (KV_LORA, BF16)])

    def epi_q(acc, xs, outs):
        c, sa, sb = xs[0][...], xs[1][...], xs[2][...]
        for h in range(MLA_HEADS):
            lo = h * HEAD_PAD
            outs[0][:, lo:lo + HEAD_PAD] = (_rope(acc[:, lo:lo + HEAD_PAD], c, sa, sb, 1.0) * qk_scale).astype(BF16)

    q_att, = _mm("q_up", qn, wl["w_uq"], "nn", [(HW, BF16)], epi_q, row_x=[rope_c, rope_sa, rope_sb], tn=HW)

    def epi_kv(acc, xs, outs):
        kr = _rope(xs[0][...], xs[1][...], xs[2][...], xs[3][...], 1.0)
        kr = kr + _neg_ones(kr.shape, STAT_COL_QK)
        vconst = _neg_ones(kr.shape, STAT_COL_V)
        for h in range(MLA_HEADS):
            lo = h * HEAD_PAD
            outs[0][:, lo:lo + HEAD_PAD] = (acc[:, lo:lo + HEAD_PAD] + kr).astype(BF16)
            outs[1][:, lo:lo + HEAD_PAD] = (acc[:, HW + lo:HW + lo + HEAD_PAD] + vconst).astype(BF16)

    k_att, v_att = _mm("kv_up", kvn, wl["w_ukv"], "nn", [(HW, BF16), (HW, BF16)], epi_kv,
                       row_x=[kr_raw, rope_c, rope_sa, rope_sb], tn=2 * HW)

    o_att, q_aug = _flash_fwd(q_att, k_att, v_att)
    wl.update(weight_fns[2](o_att))

    def epi_merge(acc, xs, outs):
        for rows in _row_chunks(acc.shape[0]):
            mo = acc[rows, :]
            g0, g1 = xs[0][rows, 0:D_MODEL].astype(F32), xs[0][rows, D_MODEL:].astype(F32)
            outs[0][rows, :] = mo.astype(BF16)
            outs[1][rows, :] = (g0 * xs[1][rows, :].astype(F32) + g1 * mo).astype(BF16)

    mla_out, merged = _mm("mla_out_merge", o_att, wl["w_mla_out"], "nn", [(D_MODEL, BF16), (D_MODEL, BF16)], epi_merge,
                          row_x=[gates, conv_out], tn=D_MODEL)

    def epi_res_norm(acc, xs, outs):
        h = xs[0][...] + acc[...]
        outs[0][...] = h
        outs[1][...] = _rms_fwd(h, xs[1][...]).astype(BF16)

    h1, u2 = _mm("mix_out", merged, wl["w_out"], "nn", [(D_MODEL, F32), (D_MODEL, BF16)], epi_res_norm,
                 row_x=[x], vec_x=[sm["norm_xattn_g"]], tn=D_MODEL)

    xscale = X_HEAD_DIM ** -0.5

    def epi_scale(acc, xs, outs):
        outs[0][...] = (acc[...] * xscale).astype(BF16)

    xq, = _mm("xattn_q", u2, wl["w_xq"], "nn", [(X_HEADS * X_HEAD_DIM, BF16)], epi_scale)

    def k_mem_norm(m_ref, g_ref, o_ref):
        o_ref[...] = _rms_fwd(m_ref[...], g_ref[...]).astype(BF16)

    mem_n, = _rows("mem_norm", k_mem_norm, [mem], [sm["norm_mem_g"]], [(D_MODEL, BF16)])
    kvx = _mm_plain("xattn_kv", mem_n, wl["w_xkv"], "nn", dtype=BF16)
    ox = _xattn_fwd(xq, kvx)
    h2, u3 = _mm("xattn_out", ox, wl["w_xo"], "nn", [(D_MODEL, F32), (D_MODEL, BF16)], epi_res_norm,
                 row_x=[h1], vec_x=[sm["norm_mlp_g"]], tn=D_MODEL)

    def epi_relu2(acc, xs, outs):
        r = jnp.maximum(acc[...], 0.0)
        outs[0][...] = (r * r).astype(BF16)

    hid, = _mm("mlp_up", u3, wl["w_mlp1"], "nn", [(D_FF, BF16)], epi_relu2)

    def epi_final(acc, xs, outs):
        g = xs[2][...]
        for rows in _row_chunks(acc.shape[0]):
            h = xs[0][rows, :] + acc[rows, :]
            e = _rms_fwd(h, g) - xs[1][rows, :]
            part = 0.5 * jnp.sum(jnp.mean(e * e, axis=-1, keepdims=True), axis=0, keepdims=True)
            dh, dg = _rms_bwd(h, g, e * (1.0 / D_MODEL))
            outs[0][rows, :] = dh
            outs[1][rows, :] = dh.astype(BF16)
            _accum(outs[2], jnp.broadcast_to(part, outs[2].shape), first=rows.start == 0)
            _accum(outs[3], dg, first=rows.start == 0)

    dh3, dh3b, loss, g_final = _mm("mlp_down_loss", hid, wl["w_mlp2"], "nn", [(D_MODEL, F32), (D_MODEL, BF16)], epi_final,
                                   row_x=[h2, target], vec_x=[sm["final_norm_g"]], sums=[(1, LANES), (1, D_MODEL)],
                                   tn=D_MODEL, tk=1024)

    def epi_drelu2(acc, xs, outs):
        outs[0][...] = (acc[...] * (2.0 * jnp.sqrt(xs[0][...].astype(F32)))).astype(BF16)

    da1, = _mm("mlp_down_dx", dh3b, wl["w_mlp2"], "nt", [(D_FF, BF16)], epi_drelu2, tile_x=[hid])
    g_mlp2 = _mm_plain("mlp_down_dw", hid, dh3b, "tn")
    g_mlp1 = _mm_plain("mlp_up_dw", u3, da1, "tn")

    def epi_norm_bwd(acc, xs, outs):
        for rows in _row_chunks(acc.shape[0]):
            dx, dg = _rms_bwd(xs[0][rows, :], xs[2][...], acc[rows, :])
            dh = xs[1][rows, :] + dx
            outs[0][rows, :] = dh
            if len(outs) == 3:
                outs[1][rows, :] = dh.astype(BF16)
            _accum(outs[-1], dg, first=rows.start == 0)

    def dx_norm_bwd(name, dy, w, xin, dres, vecs, with_bf16=True):
        outs = [(D_MODEL, F32), (D_MODEL, BF16)] if with_bf16 else [(D_MODEL, F32)]
        return _mm(name, dy, w, "nt", outs, epi_norm_bwd, row_x=[xin, dres], vec_x=vecs, sums=[(1, D_MODEL)],
                   tn=D_MODEL, tk=_pick(dy.shape[1], (1024, 768, 512)))

    dh2, dh2b, g_norm_mlp = dx_norm_bwd("mlp_up_dx_norm", da1, wl["w_mlp1"], h2, dh3, [sm["norm_mlp_g"]])

    dox = _mm_plain("xattn_out_dx", dh2b, wl["w_xo"], "nt", dtype=BF16)
    g_xo = _mm_plain("xattn_out_dw", ox, dh2b, "tn")
    dxq, dkvx = _xattn_bwd(xq, kvx, dox)
    g_xq = _mm_plain("xattn_q_dw", u2, dxq, "tn")
    g_xkv = _mm_plain("xattn_kv_dw", mem_n, dkvx, "tn")
    dmem_n = _mm_plain("xattn_kv_dx", dkvx, wl["w_xkv"], "nt")

    def k_mem_bwd(m_ref, d_ref, g_ref, dg_ref):
        _, dg = _rms_bwd(m_ref[...], g_ref[...], d_ref[...])
        _accum(dg_ref, dg)

    g_norm_mem, = _rows("mem_norm_bwd", k_mem_bwd, [mem, dmem_n], [sm["norm_mem_g"]], [], [(1, D_MODEL)])
    token = early_grads_fn(dict(w_mlp1=g_mlp1, w_mlp2=g_mlp2, w_xo=g_xo, w_xq=g_xq, w_xkv=g_xkv))
    dh1, dh1b, g_norm_xattn = dx_norm_bwd("xattn_q_dx_norm", dxq, wl["w_xq"], h1, dh2, [sm["norm_xattn_g"], token])

    dmerged = _mm_plain("mix_out_dx", dh1b, wl["w_out"], "nt")
    g_out = _mm_plain("mix_out_dw", merged, dh1b, "tn")

    def k_merge_bwd(dm_ref, g_ref, co_ref, mo_ref, dco_ref, dmo_ref, dgl_ref):
        dm = dm_ref[...]
        g0, g1 = g_ref[:, 0:D_MODEL].astype(F32), g_ref[:, D_MODEL:].astype(F32)
        dco_ref[...] = (dm * g0).astype(BF16)
        dmo_ref[...] = (dm * g1).astype(BF16)
        dgl_ref[:, 0:D_MODEL] = (dm * co_ref[...].astype(F32) * g0 * (1.0 - g0)).astype(BF16)
        dgl_ref[:, D_MODEL:] = (dm * mo_ref[...].astype(F32) * g1 * (1.0 - g1)).astype(BF16)

    dconv_out, dmla_out, dgl = _rows("merge_bwd", k_merge_bwd, [dmerged, gates, conv_out, mla_out], [],
                                     [(D_MODEL, BF16), (D_MODEL, BF16), (2 * D_MODEL, BF16)], tile=256)

    def epi_do(acc, xs, outs):
        for h in range(MLA_HEADS):
            lo = h * HEAD_PAD
            do = acc[:, lo:lo + HEAD_PAD]
            delta = jnp.sum(do * xs[0][:, lo:lo + HEAD_PAD].astype(F32), axis=-1, keepdims=True)
            outs[0][:, lo:lo + HEAD_PAD] = _put_stats(do.astype(BF16), delta, STAT_COL_V)

    do_aug, = _mm("mla_out_dx", dmla_out, wl["w_mla_out"], "nt", [(HW, BF16)], epi_do, row_x=[o_att], tn=HW)
    g_mla_out = _mm_plain("mla_out_dw", o_att, dmla_out, "tn")
    dq_att, dk_att, dv_att = _flash_bwd(q_aug, k_att, v_att, do_aug)

    def k_rope_bwd(dq_ref, dk_ref, dv_ref, c_ref, sa_ref, sb_ref, dqr_ref, dkv_ref, dkr_ref):
        c, sa, sb = c_ref[...], sa_ref[...], sb_ref[...]
        lane = lax.broadcasted_iota(jnp.int32, c.shape, 1)
        nope = (lane < QK_NOPE).astype(F32)
        ropem = ((lane >= QK_NOPE) & (lane < QK_NOPE + QK_ROPE)).astype(F32)
        dkr = jnp.zeros(c.shape, F32)
        for h in range(MLA_HEADS):
            lo = h * HEAD_PAD
            dqr_ref[:, lo:lo + HEAD_PAD] = (_rope(dq_ref[:, lo:lo + HEAD_PAD], c, sa, sb, -1.0) * qk_scale).astype(BF16)
            dk = dk_ref[:, lo:lo + HEAD_PAD]
            dkv_ref[:, lo:lo + HEAD_PAD] = (dk * nope).astype(BF16)
            dkr = dkr + dk
        dkv_ref[:, HW:] = dv_ref[...]
        dkr_ref[...] = (_rope(dkr * ropem, c, sa, sb, -1.0) * ropem).astype(BF16)

    dq_raw, dkv_cat, dkr = _rows("rope_bwd", k_rope_bwd, [dq_att, dk_att, dv_att, rope_c, rope_sa, rope_sb], [],
                                 [(HW, BF16), (2 * HW, BF16), (HEAD_PAD, BF16)], tile=256)
    g_uq = _mm_plain("q_up_dw", qn, dq_raw, "tn")
    dqn = _mm_plain("q_up_dx", dq_raw, wl["w_uq"], "nt")
    g_ukv = _mm_plain("kv_up_dw", kvn, dkv_cat, "tn")
    dkvn = _mm_plain("kv_up_dx", dkv_cat, wl["w_ukv"], "nt")

    def k_lora_bwd(cq_ref, ckv_ref, dqn_ref, dkvn_ref, gq_ref, gkv_ref, dcq_ref, dckv_ref, dgq_ref, dgkv_ref):
        dcq, dgq = _rms_bwd(cq_ref[...], gq_ref[...], dqn_ref[...])
        dckv, dgkv = _rms_bwd(ckv_ref[...], gkv_ref[...], dkvn_ref[...])
        dcq_ref[...] = dcq.astype(BF16)
        dckv_ref[...] = dckv.astype(BF16)
        _accum(dgq_ref, dgq)
        _accum(dgkv_ref, dgkv)

    dc_q, dc_kv, g_q_norm, g_kv_norm = _rows("lora_norm_bwd", k_lora_bwd, [c_q, c_kv, dqn, dkvn],
                                              [sm["q_norm_g"], sm["kv_norm_g"]], [(Q_LORA, BF16), (KV_LORA, BF16)],
                                              [(1, Q_LORA), (1, KV_LORA)])

    dz3 = _mm_plain("conv_out_dx", dconv_out, wl["w_conv_out"], "nt")
    g_conv_out = _mm_plain("conv_out_dw", z3, dconv_out, "tn")
    dz1, g_ln_g, g_ln_b, g_conv_b = _conv_bwd_norm(dz3, z1, sm["conv_ln_g"], sm["conv_ln_b"])
    dconv_in, g_conv_w = _conv_bwd_taps(dz1, z0, conv_in, conv_w)

    dproj = jnp.concatenate([dconv_in, dgl, dc_q, dkr, dc_kv], axis=1)
    g_in = _mm_plain("proj_dw", u1, dproj, "tn")
    token = late_grads_fn(dict(w_in=g_in, conv_w=g_conv_w[:CONV_WIDTH], w_conv_out=g_conv_out, w_uq=g_uq, w_ukv=g_ukv,
                               w_mla_out=g_mla_out, w_out=g_out))
    grad_x, g_norm_mix = dx_norm_bwd("proj_dx_norm", dproj, w_in, x, dh1, [sm["norm_mix_g"], token], with_bf16=False)

    small = dict(norm_mix_g=g_norm_mix, conv_b=g_conv_b, conv_ln_g=g_ln_g, conv_ln_b=g_ln_b, q_norm_g=g_q_norm,
                 kv_norm_g=g_kv_norm, norm_xattn_g=g_norm_xattn, norm_mem_g=g_norm_mem, norm_mlp_g=g_norm_mlp,
                 final_norm_g=g_final)
    return loss, grad_x, small


def _shard(a, k, axis):
    n = a.shape[axis] // N_CHIPS
    return lax.slice_in_dim(a, k * n, (k + 1) * n, axis=axis)


def _pack_small(grads, loss):
    flat = jnp.concatenate([grads[n].reshape(-1) for n in SMALL] + [loss.reshape(-1)[:1]])
    rows = -(-flat.shape[0] // (8 * LANES)) * 8
    return jnp.pad(flat, (0, rows * LANES - flat.shape[0])).reshape(rows, LANES)


def _pack_groups(shapes, names):
    groups = {}
    for n in names:
        groups.setdefault(shapes[n][1], []).append(n)
    return groups


def _pad_rows(a, mult):
    return jnp.pad(a, ((0, -a.shape[0] % mult), (0, 0)))


def _pack_grads(grads, shapes, names):
    packs = []
    for width, group in _pack_groups(shapes, names).items():
        per_chip = [jnp.concatenate([_pad_rows(_shard(grads[n], k, SHARD_AXIS[n]).astype(BF16), PACK_ROW_ALIGN) for n in group])
                    for k in range(N_CHIPS)]
        rows = per_chip[0].shape[0]
        packs.append(jnp.stack(per_chip).reshape(N_CHIPS, 2, rows // 2, width))
    return packs


def _unpack_grads(fulls, shapes, names):
    out = {}
    for full, group in zip(fulls, _pack_groups(shapes, names).values()):
        flat, at = full.reshape(-1, full.shape[-1]), 0
        for n in group:
            rows = shapes[n][0]
            out[n] = flat[at:at + rows]
            at += rows + (-rows % PACK_ROW_ALIGN)
    return out


def _unpack(flat, names, shapes):
    out, at = {}, 0
    for n in names:
        size = math.prod(shapes[n])
        out[n] = flat[at:at + size].reshape(shapes[n])
        at += size
    return out, at


def kernel(x, mem, positions, norm_mix_g, w_in, conv_w, conv_b, conv_ln_g, conv_ln_b, w_conv_out, q_norm_g, w_uq, kv_norm_g, w_ukv, w_mla_out, w_out, norm_xattn_g, norm_mem_g, w_xq, w_xkv, w_xo, norm_mlp_g, w_mlp1, w_mlp2, final_norm_g, loss_target, m_norm_mix_g, m_w_in, m_conv_w, m_conv_b, m_conv_ln_g, m_conv_ln_b, m_w_conv_out, m_q_norm_g, m_w_uq, m_kv_norm_g, m_w_ukv, m_w_mla_out, m_w_out, m_norm_xattn_g, m_norm_mem_g, m_w_xq, m_w_xkv, m_w_xo, m_norm_mlp_g, m_w_mlp1, m_w_mlp2, m_final_norm_g, v_norm_mix_g, v_w_in, v_conv_w, v_conv_b, v_conv_ln_g, v_conv_ln_b, v_w_conv_out, v_q_norm_g, v_w_uq, v_kv_norm_g, v_w_ukv, v_w_mla_out, v_w_out, v_norm_xattn_g, v_norm_mem_g, v_w_xq, v_w_xkv, v_w_xo, v_norm_mlp_g, v_w_mlp1, v_w_mlp2, v_final_norm_g):
    args = dict(locals())
    w = {n: args[n] for n in WEIGHTS}
    m = {n: args["m_" + n] for n in WEIGHTS}
    v = {n: args["v_" + n] for n in WEIGHTS}

    shards = [w[n][0].astype(F32 if n == "conv_w" else BF16) for n in BIG]
    bounds = (0,) + WEIGHT_WAITS + (len(BIG),)
    spans = [slice(lo, hi) for lo, hi in zip(bounds[:-1], bounds[1:])]
    sem_pairs, shards_thru, lands_thru, token = _gather_start(shards, [list(range(len(BIG)))[sp] for sp in spans])

    def unshard(n, g):
        ax = SHARD_AXIS[n]
        return jnp.moveaxis(g, 0, ax).reshape(g.shape[1:1 + ax] + (N_CHIPS * g.shape[1 + ax],) + g.shape[2 + ax:])

    def wait_fn(i):
        def fn(after):
            lands = _gather_wait(f"gather_weights_wait_{i}", sem_pairs[i], shards_thru[spans[i]], lands_thru[spans[i]], after)
            full = {n: unshard(n, g) for n, g in zip(BIG[spans[i]], lands)}
            return (_layout_w_in(full["w_in"]), full["conv_w"]) if i == 0 else _layout_weights(full)
        return fn

    sm = {n: w[n].reshape(1, -1) for n in SMALL}
    sm["norm_mix_g"] = sm["norm_mix_g"] + token[0, 0]

    shapes = {n: w[n].shape[1:] if n in BIG else w[n].shape for n in WEIGHTS}
    late_names = [n for n in BIG if n not in EARLY_GRADS]
    inflight = {}

    def send_grads(tag, names):
        def fn(g):
            packs = _pack_grads(_unlayout_grads(g), shapes, names)
            got = _pair_exchange(f"grad_pair_exchange_{tag}", packs)
            pairs = [_pair_add(f"grad_pair_add_{tag}_{i}", p, r) for i, (p, r) in enumerate(zip(packs, got))]
            *inflight[tag], token = _chip_exchange_start(f"grad_chip_exchange_{tag}_start", pairs)
            return token
        return fn

    loss, grad_x, g_small = _local_step(x[0], mem[0], positions, loss_target[0], [wait_fn(i) for i in range(3)],
                                        send_grads("early", EARLY_GRADS), send_grads("late", late_names), sm)

    halves, counts = [], {}
    for tag in ("late", "early"):
        own, got = _chip_exchange_wait(f"grad_chip_exchange_{tag}_wait", *inflight[tag], grad_x)
        halves += [_chip_add(f"grad_chip_add_{tag}_{i}", p, g) for i, (p, g) in enumerate(zip(own, got))]
        counts[tag] = len(own)
    fulls = _pair_share(halves)
    g_sum = _unpack_grads(fulls[:counts["late"]], shapes, late_names)
    g_sum.update(_unpack_grads(fulls[counts["late"]:], shapes, EARLY_GRADS))
    small_flat = _sum_over_devices(_pack_small(g_small, loss)).reshape(-1)
    g_small, at = _unpack(small_flat, SMALL, shapes)
    g_sum.update(g_small)
    loss_sum = small_flat[at]

    out_g, out_d, out_m, out_v = [], [], [], []
    for n in WEIGHTS:
        g, d, nm, nv = _adamw("adamw_" + n, w[n], g_sum[n], m[n], v[n])
        out_g.append(g)
        out_d.append(d)
        out_m.append(nm)
        out_v.append(nv)
    return (loss_sum, grad_x[None], *out_g, *out_d, *out_m, *out_v)
```

```python
import functools
import math

import jax
import jax.numpy as jnp
from jax import lax
from jax.experimental import pallas as pl
from jax.experimental.pallas import tpu as pltpu

F32 = jnp.float32
BF16 = jnp.bfloat16
MESH = pl.DeviceIdType.MESH

D_MODEL = 1024
CONV_CH = 512
CONV_WIDTH = 31
MLA_HEADS = 8
QK_NOPE = 64
QK_ROPE = 32
V_DIM = 64
Q_LORA = 384
KV_LORA = 256
MEM_LEN = 256
X_HEADS = 4
X_HEAD_DIM = 128
D_FF = 4096
ROPE_THETA = 10000.0
EPS = 1e-6
HEAD_PAD = 128
STAT_COL_QK = QK_NOPE + QK_ROPE
STAT_COL_V = V_DIM
HALO = 32
N_CHIPS = 4
LANES = 128

ADAM_LR = 0.001
ADAM_B1 = 0.9
ADAM_B2 = 0.999
ADAM_EPS = 1e-08
ADAM_WD = 0.01
ADAM_STEP = 10

VMEM_LIMIT = 52 * 1024 * 1024
ROW_TILES = (1024, 512, 256, 128, 64, 32, 16)
PACK_ROW_ALIGN = 32
N_DEV = 8
NEG = -1e30

BIG = ["w_in", "conv_w", "w_conv_out", "w_uq", "w_ukv", "w_mla_out", "w_out", "w_xq", "w_xkv", "w_xo", "w_mlp1", "w_mlp2"]
WEIGHT_WAITS = (2, 5)
SHARD_AXIS = {"w_in": 1, "w_conv_out": 1, "w_uq": 1, "w_ukv": 1, "w_mla_out": 1, "w_out": 0, "w_xq": 0, "w_xkv": 0,
              "w_xo": 1, "w_mlp1": 1, "w_mlp2": 0, "conv_w": 1}
EARLY_GRADS = ["w_mlp1", "w_mlp2", "w_xkv", "w_xq", "w_xo"]
SMALL = ["norm_mix_g", "conv_b", "conv_ln_g", "conv_ln_b", "q_norm_g", "kv_norm_g", "norm_xattn_g", "norm_mem_g",
         "norm_mlp_g", "final_norm_g"]
WEIGHTS = ["norm_mix_g", "w_in", "conv_w", "conv_b", "conv_ln_g", "conv_ln_b", "w_conv_out", "q_norm_g", "w_uq",
           "kv_norm_g", "w_ukv", "w_mla_out", "w_out", "norm_xattn_g", "norm_mem_g", "w_xq", "w_xkv", "w_xo",
           "norm_mlp_g", "w_mlp1", "w_mlp2", "final_norm_g"]


def _pick(n, prefs):
    for p in prefs:
        if n % p == 0:
            return p
    return n


def _params(sem):
    return pltpu.CompilerParams(dimension_semantics=sem, vmem_limit_bytes=VMEM_LIMIT)


_DIMS = {"nn": (((1,), (0,)), ((), ())), "nt": (((1,), (1,)), ((), ())), "tn": (((0,), (0,)), ((), ()))}


def _mm(name, a, b, mode, outs, epi, row_x=(), tile_x=(), vec_x=(), sums=(), tm=None, tn=None, tk=None, b_cols=None):
    if mode == "nn":
        (M, K), (_, N) = a.shape, b.shape
        if b_cols is not None:
            N = b_cols[1]
    elif mode == "nt":
        (M, K), (N, _) = a.shape, b.shape
    else:
        (K, M), (_, N) = a.shape, b.shape
    tm = tm or _pick(M, (1024, 512, 384, 256, 128))
    tn = tn or _pick(N, (1024, 768, 512, 384, 256, 128))
    tk = tk or _pick(K, (2048, 1920, 1024, 768, 512, 384, 256, 128))
    nk = K // tk
    rows_inner = nk == 1 and N // tn > 1
    grid = (N // tn, M // tm, nk) if rows_inner else (M // tm, N // tn, nk)

    def spec(shape, f):
        return pl.BlockSpec(shape, (lambda j, i, k: f(i, j, k)) if rows_inner else f)

    b_off = 0
    if b_cols is not None:
        assert mode == "nn" and b_cols[0] % tn == 0, (name, b_cols, tn)
        b_off = b_cols[0] // tn
    a_spec = spec((tk, tm), lambda i, j, k: (k, i)) if mode == "tn" else spec((tm, tk), lambda i, j, k: (i, k))
    b_spec = spec((tn, tk), lambda i, j, k: (j, k)) if mode == "nt" else spec((tk, tn), lambda i, j, k: (k, j + b_off))
    in_specs = [a_spec, b_spec]
    in_specs += [spec((tm, r.shape[1]), lambda i, j, k: (i, 0)) for r in row_x]
    in_specs += [spec((tm, tn), lambda i, j, k: (i, j)) for _ in tile_x]
    in_specs += [spec(v.shape, lambda i, j, k: (0, 0)) for v in vec_x]
    out_specs, out_shape = [], []
    for w, dt in outs:
        if tn == N:
            out_specs.append(spec((tm, w), lambda i, j, k: (i, 0)))
        else:
            assert w == N, (name, w, N)
            out_specs.append(spec((tm, tn), lambda i, j, k: (i, j)))
        out_shape.append(jax.ShapeDtypeStruct((M, w), dt))
    for shp in sums:
        assert tn == N and not rows_inner, name
        out_specs.append(spec(shp, lambda i, j, k: (0, 0)))
        out_shape.append(jax.ShapeDtypeStruct(shp, F32))
    nx = len(row_x) + len(tile_x) + len(vec_x)
    dims = _DIMS[mode]

    def body(a_ref, b_ref, *rest):
        x_refs, out_refs, acc_ref = rest[:nx], rest[nx:nx + len(outs) + len(sums)], rest[-1]
        av, bv = a_ref[...], b_ref[...]
        if av.dtype != BF16:
            av = av.astype(BF16)
        if bv.dtype != BF16:
            bv = bv.astype(BF16)
        prod = lax.dot_general(av, bv, dims, preferred_element_type=F32)
        if nk == 1:
            acc_ref[...] = prod
            epi(acc_ref, x_refs, out_refs)
        else:
            k = pl.program_id(2)

            @pl.when(k == 0)
            def _():
                acc_ref[...] = prod

            @pl.when(k > 0)
            def _():
                acc_ref[...] += prod

            @pl.when(k == nk - 1)
            def _():
                epi(acc_ref, x_refs, out_refs)

    res = pl.pallas_call(
        body, name=name, grid=grid, in_specs=in_specs, out_specs=out_specs, out_shape=out_shape,
        scratch_shapes=[pltpu.VMEM((tm, tn), F32)],
        compiler_params=_params(("arbitrary",) * 3 if sums else ("parallel", "parallel", "arbitrary")),
    )(a, b, *row_x, *tile_x, *vec_x)
    return res


def _epi_store(acc_ref, x_refs, out_refs):
    for o in out_refs:
        o[...] = acc_ref[...].astype(o.dtype)


def _mm_plain(name, a, b, mode, dtype=F32, **kw):
    n = kw["b_cols"][1] if kw.get("b_cols") else (b.shape[0] if mode == "nt" else b.shape[1])
    return _mm(name, a, b, mode, [(n, dtype)], _epi_store, **kw)[0]


def _rows(name, body, row_ins, vec_ins, row_outs, acc_outs=(), tile=512):
    S = row_ins[0].shape[0]
    t = _pick(S, (tile, 256, 128, 64, 32, 16, 8))
    in_specs = [pl.BlockSpec((t, r.shape[1]), lambda i: (i, 0)) for r in row_ins]
    in_specs += [pl.BlockSpec(v.shape, lambda i: (0, 0)) for v in vec_ins]
    out_specs = [pl.BlockSpec((t, w), lambda i: (i, 0)) for w, _ in row_outs]
    out_specs += [pl.BlockSpec(shp, lambda i: (0, 0)) for shp in acc_outs]
    out_shape = [jax.ShapeDtypeStruct((S, w), dt) for w, dt in row_outs]
    out_shape += [jax.ShapeDtypeStruct(shp, F32) for shp in acc_outs]
    sem = ("arbitrary",) if acc_outs else ("parallel",)
    return pl.pallas_call(
        functools.partial(body), name=name, grid=(S // t,), in_specs=in_specs, out_specs=out_specs,
        out_shape=out_shape, compiler_params=_params(sem),
    )(*row_ins, *vec_ins)


def _accum(ref, val, first=True):
    if first:
        @pl.when(pl.program_id(0) == 0)
        def _():
            ref[...] = jnp.zeros_like(ref)

    ref[...] += val


EPILOGUE_ROWS = 256


def _row_chunks(n):
    step = min(EPILOGUE_ROWS, n)
    return [slice(r, r + step) for r in range(0, n, step)]


def _colsum(v):
    return jnp.sum(v, axis=0, keepdims=True)


def _rms_fwd(x, g):
    r = lax.rsqrt(jnp.mean(x * x, axis=-1, keepdims=True) + EPS)
    return x * r * g


def _rms_bwd(x, g, du):
    r = lax.rsqrt(jnp.mean(x * x, axis=-1, keepdims=True) + EPS)
    xn = x * r
    gdu = du * g
    dx = r * (gdu - xn * jnp.mean(xn * gdu, axis=-1, keepdims=True))
    return dx, _colsum(du * xn)


def _sigmoid(v):
    return 1.0 / (1.0 + jnp.exp(-v))


def _rope(v, c, sa, sb, sign):
    return v * c + sign * (pltpu.roll(v, HEAD_PAD - QK_ROPE // 2, 1) * sa + pltpu.roll(v, QK_ROPE // 2, 1) * sb)


def _split3(v):
    hi = v.astype(BF16)
    r1 = v - hi.astype(F32)
    mid = r1.astype(BF16)
    lo = (r1 - mid.astype(F32)).astype(BF16)
    return hi, mid, lo


def _put_stats(base, stat, col):
    hi, mid, lo = _split3(stat)
    lane = lax.broadcasted_iota(jnp.int32, base.shape, 1)
    out = jnp.where(lane == col, hi, base)
    out = jnp.where(lane == col + 1, mid, out)
    return jnp.where(lane == col + 2, lo, out)


def _neg_ones(shape, col):
    lane = lax.broadcasted_iota(jnp.int32, shape, 1)
    return jnp.where((lane >= col) & (lane < col + 3), -1.0, 0.0).astype(F32)


def _shifted(ext, t):
    p = ext.shape[0]
    for b in range(8):
        rb = ext if b == 0 else pltpu.roll(ext, p - b, 0)
        for a in range(HALO // 8 + 1):
            if 8 * a + b <= HALO:
                yield 8 * a + b, rb[8 * a:8 * a + t]


def _conv_fwd(z0, conv_w, conv_b, ln_g, ln_b):
    S, C = z0.shape
    t = _pick(S, (512, 256, 128, 64, 32))
    per = t // HALO

    def body(cur_ref, prev_ref, w_ref, b_ref, g_ref, beta_ref, z1_ref, z3_ref, ext_ref):
        i = pl.program_id(0)
        ext_ref[0:HALO, :] = jnp.where(i > 0, prev_ref[...], 0.0)
        ext_ref[HALO:, :] = cur_ref[...]
        ext = ext_ref[...]
        acc = jnp.zeros((t, C), F32)
        for d, win in _shifted(ext, t):
            k = d - (HALO - CONV_WIDTH + 1)
            if 0 <= k < CONV_WIDTH:
                acc = acc + win * w_ref[k:k + 1, :]
        z1 = acc + b_ref[...]
        z1_ref[...] = z1
        mu = jnp.mean(z1, axis=-1, keepdims=True)
        zc = z1 - mu
        rs = lax.rsqrt(jnp.mean(zc * zc, axis=-1, keepdims=True) + EPS)
        z2 = zc * rs * g_ref[...] + beta_ref[...]
        z3_ref[...] = (z2 * _sigmoid(z2)).astype(BF16)

    vec = lambda v: pl.BlockSpec(v.shape, lambda i: (0, 0))
    return pl.pallas_call(
        body, name="conv_fwd", grid=(S // t,),
        in_specs=[pl.BlockSpec((t, C), lambda i: (i, 0)),
                  pl.BlockSpec((HALO, C), lambda i: (jnp.maximum(i * per - 1, 0), 0)),
                  vec(conv_w), vec(conv_b), vec(ln_g), vec(ln_b)],
        out_specs=[pl.BlockSpec((t, C), lambda i: (i, 0)), pl.BlockSpec((t, C), lambda i: (i, 0))],
        out_shape=[jax.ShapeDtypeStruct((S, C), F32), jax.ShapeDtypeStruct((S, C), BF16)],
        scratch_shapes=[pltpu.VMEM((t + HALO, C), F32)],
        compiler_params=_params(("parallel",)),
    )(z0, z0, conv_w, conv_b, ln_g, ln_b)


def _conv_bwd_norm(dz3, z1, ln_g, ln_b):
    C = z1.shape[1]

    def body(dz3_ref, z1_ref, g_ref, beta_ref, dz1_ref, dg_ref, dbeta_ref, dbias_ref):
        z1 = z1_ref[...]
        mu = jnp.mean(z1, axis=-1, keepdims=True)
        zc = z1 - mu
        rs = lax.rsqrt(jnp.mean(zc * zc, axis=-1, keepdims=True) + EPS)
        xh = zc * rs
        z2 = xh * g_ref[...] + beta_ref[...]
        sg = _sigmoid(z2)
        dz2 = dz3_ref[...] * (sg * (1.0 + z2 * (1.0 - sg)))
        dxh = dz2 * g_ref[...]
        dz1 = rs * (dxh - jnp.mean(dxh, axis=-1, keepdims=True) - xh * jnp.mean(dxh * xh, axis=-1, keepdims=True))
        dz1_ref[...] = dz1
        _accum(dg_ref, _colsum(dz2 * xh))
        _accum(dbeta_ref, _colsum(dz2))
        _accum(dbias_ref, _colsum(dz1))

    return _rows("conv_bwd_norm", body, [dz3, z1], [ln_g, ln_b], [(C, F32)], [(1, C)] * 3)


def _conv_bwd_taps(dz1, z0, conv_in, conv_w):
    S, C = z0.shape
    t = _pick(S, (512, 256, 128, 64, 32))
    per = t // HALO
    last = S // HALO - 1
    nt = S // t

    def body(dcur_ref, dnext_ref, zcur_ref, zprev_ref, cin_ref, w_ref, dcin_ref, dw_ref, dext_ref, zext_ref):
        i = pl.program_id(0)
        dcur = dcur_ref[...]
        dext_ref[0:t, :] = dcur
        dext_ref[t:, :] = jnp.where(i < nt - 1, dnext_ref[...], 0.0)
        zext_ref[0:HALO, :] = jnp.where(i > 0, zprev_ref[...], 0.0)
        zext_ref[HALO:, :] = zcur_ref[...]

        @pl.when(i == 0)
        def _():
            dw_ref[...] = jnp.zeros_like(dw_ref)

        dz0 = jnp.zeros((t, C), F32)
        for d, win in _shifted(dext_ref[...], t):
            k = CONV_WIDTH - 1 - d
            if 0 <= k < CONV_WIDTH:
                dz0 = dz0 + win * w_ref[k:k + 1, :]
        for d, win in _shifted(zext_ref[...], t):
            k = d - (HALO - CONV_WIDTH + 1)
            if 0 <= k < CONV_WIDTH:
                dw_ref[k:k + 1, :] += _colsum(dcur * win)
        a = cin_ref[:, 0:C].astype(F32)
        sg = _sigmoid(cin_ref[:, C:2 * C].astype(F32))
        dcin_ref[:, 0:C] = (dz0 * sg).astype(BF16)
        dcin_ref[:, C:2 * C] = (dz0 * a * sg * (1.0 - sg)).astype(BF16)

    return pl.pallas_call(
        body, name="conv_bwd_taps", grid=(nt,),
        in_specs=[pl.BlockSpec((t, C), lambda i: (i, 0)),
                  pl.BlockSpec((HALO, C), lambda i: (jnp.minimum((i + 1) * per, last), 0)),
                  pl.BlockSpec((t, C), lambda i: (i, 0)),
                  pl.BlockSpec((HALO, C), lambda i: (jnp.maximum(i * per - 1, 0), 0)),
                  pl.BlockSpec((t, 2 * C), lambda i: (i, 0)),
                  pl.BlockSpec(conv_w.shape, lambda i: (0, 0))],
        out_specs=[pl.BlockSpec((t, 2 * C), lambda i: (i, 0)), pl.BlockSpec((HALO, C), lambda i: (0, 0))],
        out_shape=[jax.ShapeDtypeStruct((S, 2 * C), BF16), jax.ShapeDtypeStruct((HALO, C), F32)],
        scratch_shapes=[pltpu.VMEM((t + HALO, C), F32), pltpu.VMEM((t + HALO, C), F32)],
        compiler_params=_params(("arbitrary",)),
    )(dz1, dz1, z0, z0, conv_in, conv_w)


def _lower_tri(shape, rows_are_queries):
    row = lax.broadcasted_iota(jnp.int32, shape, 0)
    col = lax.broadcasted_iota(jnp.int32, shape, 1)
    return (col <= row) if rows_are_queries else (row <= col)


HEADS_PER_STEP = 2
FWD_HEADS_PER_STEP = 2
FWD_KEY_TILES = 8


def _flash_specs(S, t, heads):
    w = heads * HEAD_PAD
    blk = pl.BlockSpec((t, w), lambda h, i: (i, h))
    head = pl.BlockSpec((S, w), lambda h, i: (0, h))
    return blk, head


def _head_lanes(g):
    return slice(g * HEAD_PAD, (g + 1) * HEAD_PAD)


def _dot_nt(a, b):
    return lax.dot_general(a, b, _DIMS["nt"], preferred_element_type=F32)


def _dot_nn(a, b):
    return lax.dot_general(a, b, _DIMS["nn"], preferred_element_type=F32)


def _dot_tn(a, b):
    return lax.dot_general(a, b, _DIMS["tn"], preferred_element_type=F32)


def _flash_fwd(q, k, v):
    S = q.shape[0]
    t = _pick(S, (512, 256, 128))

    def body(q_ref, k_ref, v_ref, o_ref, qa_ref, m_ref, acc_ref):
        qi = pl.program_id(1)
        m_ref[...] = jnp.full_like(m_ref, NEG)
        acc_ref[...] = jnp.zeros_like(acc_ref)

        def step(first, tiles, diag):
            width = tiles * t
            rows = pl.ds(pl.multiple_of(first, t), width)
            for g in range(FWD_HEADS_PER_STEP):
                hl = _head_lanes(g)
                s = _dot_nt(q_ref[:, hl], k_ref[rows, hl])
                if diag:
                    row = lax.broadcasted_iota(jnp.int32, s.shape, 0)
                    col = lax.broadcasted_iota(jnp.int32, s.shape, 1)
                    s = jnp.where(col <= row + (tiles - 1) * t, s, NEG)
                m_old = m_ref[g]
                m_new = jnp.maximum(m_old, jnp.max(s, axis=-1, keepdims=True))
                p = jnp.exp(s - m_new).astype(BF16)
                acc_ref[g] = jnp.exp(m_old - m_new) * acc_ref[g] + _dot_nn(p, v_ref[rows, hl])
                m_ref[g] = m_new

        def wide(kb, carry):
            step(kb * (FWD_KEY_TILES * t), FWD_KEY_TILES, False)
            return carry

        full_groups = qi // FWD_KEY_TILES
        lax.fori_loop(0, full_groups, wide, 0)
        for tiles in range(1, min(FWD_KEY_TILES, S // t) + 1):
            @pl.when(qi - full_groups * FWD_KEY_TILES == tiles - 1)
            def _():
                step(full_groups * (FWD_KEY_TILES * t), tiles, True)

        for g in range(FWD_HEADS_PER_STEP):
            hl = _head_lanes(g)
            acc = acc_ref[g]
            l = -acc[:, STAT_COL_V:STAT_COL_V + 1]
            o_ref[:, hl] = (acc / l).astype(BF16)
            qa_ref[:, hl] = _put_stats(q_ref[:, hl], m_ref[g] + jnp.log(l), STAT_COL_QK)

    blk, head = _flash_specs(S, t, FWD_HEADS_PER_STEP)
    return pl.pallas_call(
        body, name="mla_flash_fwd", grid=(MLA_HEADS // FWD_HEADS_PER_STEP, S // t),
        in_specs=[blk, head, head], out_specs=[blk, blk],
        out_shape=[jax.ShapeDtypeStruct(q.shape, BF16), jax.ShapeDtypeStruct(q.shape, BF16)],
        scratch_shapes=[pltpu.VMEM((FWD_HEADS_PER_STEP, t, 1), F32), pltpu.VMEM((FWD_HEADS_PER_STEP, t, HEAD_PAD), F32)],
        compiler_params=_params(("parallel", "arbitrary")),
    )(q, k, v)


def _flash_bwd(qa, k, v, doa):
    S = qa.shape[0]
    t = _pick(S, (1024, 512, 256, 128))
    n = S // t
    half = t // 2

    def body(qa_ref, k_ref, v_ref, do_ref, dq_ref, dk_ref, dv_ref, dk_acc, dv_acc):
        kj = pl.program_id(1)

        @pl.when(kj == 0)
        def _():
            dq_ref[...] = jnp.zeros_like(dq_ref)

        dk_acc[...] = jnp.zeros_like(dk_acc)
        dv_acc[...] = jnp.zeros_like(dv_acc)

        def step(q_first, q_len, keys, diag):
            rows = pl.ds(pl.multiple_of(q_first, q_len), q_len)
            for g in range(HEADS_PER_STEP):
                hl = _head_lanes(g)
                qa, do, kk = qa_ref[rows, hl], do_ref[rows, hl], k_ref[keys, hl]
                st = _dot_nt(kk, qa)
                if diag:
                    st = jnp.where(_lower_tri(st.shape, False), st, NEG)
                pt = jnp.exp(st)
                dst = (pt * _dot_nt(v_ref[keys, hl], do)).astype(BF16)
                dv_acc[keys, hl] += _dot_nn(pt.astype(BF16), do)
                dk_acc[keys, hl] += _dot_nn(dst, qa)
                dq_ref[rows, hl] += _dot_tn(dst, kk)

        def loop(qi, carry):
            step(qi * t, t, slice(0, t), False)
            return carry

        lo, hi = slice(0, half), slice(half, t)
        step(kj * t, half, lo, True)
        step(kj * t + half, half, lo, False)
        step(kj * t + half, half, hi, True)
        lax.fori_loop(kj + 1, n, loop, 0)
        dk_ref[...] = dk_acc[...]
        dv_ref[...] = dv_acc[...].astype(BF16)

    blk, head = _flash_specs(S, t, HEADS_PER_STEP)
    w = HEADS_PER_STEP * HEAD_PAD
    return pl.pallas_call(
        body, name="mla_flash_bwd", grid=(MLA_HEADS // HEADS_PER_STEP, n),
        in_specs=[head, blk, blk, head], out_specs=[head, blk, blk],
        out_shape=[jax.ShapeDtypeStruct(qa.shape, F32), jax.ShapeDtypeStruct(qa.shape, F32), jax.ShapeDtypeStruct(qa.shape, BF16)],
        scratch_shapes=[pltpu.VMEM((t, w), F32), pltpu.VMEM((t, w), F32)],
        compiler_params=_params(("parallel", "arbitrary")),
    )(qa, k, v, doa)


def _xattn_fwd(xq, kvx):
    W = X_HEADS * X_HEAD_DIM

    def body(q_ref, kv_ref, o_ref):
        for h in range(X_HEADS):
            lo = h * X_HEAD_DIM
            s = _dot_nt(q_ref[:, lo:lo + X_HEAD_DIM], kv_ref[:, lo:lo + X_HEAD_DIM])
            p = jnp.exp(s - jnp.max(s, axis=-1, keepdims=True))
            p = p / jnp.sum(p, axis=-1, keepdims=True)
            o_ref[:, lo:lo + X_HEAD_DIM] = _dot_nn(p.astype(BF16), kv_ref[:, W + lo:W + lo + X_HEAD_DIM]).astype(BF16)

    return _rows("xattn_fwd", body, [xq], [kvx], [(W, BF16)])[0]


def _xattn_bwd(xq, kvx, dox):
    W = X_HEADS * X_HEAD_DIM
    scale = X_HEAD_DIM ** -0.5

    def body(q_ref, do_ref, kv_ref, dq_ref, dkv_ref):
        @pl.when(pl.program_id(0) == 0)
        def _():
            dkv_ref[...] = jnp.zeros_like(dkv_ref)

        for h in range(X_HEADS):
            lo = h * X_HEAD_DIM
            q, k = q_ref[:, lo:lo + X_HEAD_DIM], kv_ref[:, lo:lo + X_HEAD_DIM]
            v, do = kv_ref[:, W + lo:W + lo + X_HEAD_DIM], do_ref[:, lo:lo + X_HEAD_DIM]
            s = _dot_nt(q, k)
            p = jnp.exp(s - jnp.max(s, axis=-1, keepdims=True))
            p = p / jnp.sum(p, axis=-1, keepdims=True)
            dp = _dot_nt(do, v)
            ds = (p * (dp - jnp.sum(dp * p, axis=-1, keepdims=True))).astype(BF16)
            dq_ref[:, lo:lo + X_HEAD_DIM] = (_dot_nn(ds, k) * scale).astype(BF16)
            dkv_ref[:, lo:lo + X_HEAD_DIM] += _dot_tn(ds, q)
            dkv_ref[:, W + lo:W + lo + X_HEAD_DIM] += _dot_tn(p.astype(BF16), do)

    return _rows("xattn_bwd", body, [xq, dox], [kvx], [(W, BF16)], [kvx.shape])


def _adamw(name, w, g, m, v):
    c1 = 1.0 / (1.0 - ADAM_B1 ** ADAM_STEP)
    c2 = 1.0 / (1.0 - ADAM_B2 ** ADAM_STEP)
    lead = (0,) * (w.ndim - 2)
    w2 = w.reshape((1,) * (2 - w.ndim) + w.shape) if w.ndim < 2 else w
    m2, v2 = m.reshape(w2.shape), v.reshape(w2.shape)
    g2 = g.reshape(w2.shape[-2:])
    R, C = g2.shape
    t = _pick(R, (256, 128, 64, 32, 16, 8))

    def body(w_ref, g_ref, m_ref, v_ref, go_ref, d_ref, nm_ref, nv_ref):
        g = g_ref[...]
        nm = ADAM_B1 * m_ref[lead] + (1.0 - ADAM_B1) * g
        nv = ADAM_B2 * v_ref[lead] + (1.0 - ADAM_B2) * (g * g)
        go_ref[lead] = g
        d_ref[lead] = -ADAM_LR * ((nm * c1) / (jnp.sqrt(nv * c2) + ADAM_EPS) + ADAM_WD * w_ref[lead])
        nm_ref[lead] = nm
        nv_ref[lead] = nv

    full = pl.BlockSpec((1,) * len(lead) + (t, C), lambda i: lead + (i, 0))
    outs = pl.pallas_call(
        body, name=name, grid=(R // t,), in_specs=[full, pl.BlockSpec((t, C), lambda i: (i, 0)), full, full],
        out_specs=[full] * 4, out_shape=[jax.ShapeDtypeStruct(w2.shape, F32)] * 4, compiler_params=_params(("parallel",)),
    )(w2, g2, m2, v2)
    return [o.reshape(w.shape) for o in outs]


def _place():
    x, y, c = lax.axis_index("x"), lax.axis_index("y"), lax.axis_index("c")
    return x, y, c, [(1 - x, y), (x, 1 - y), (1 - x, 1 - y)]


_ANY = pl.BlockSpec(memory_space=pl.ANY)


_HBM = pl.BlockSpec(memory_space=pltpu.HBM)
_SEM = pl.BlockSpec(memory_space=pltpu.SEMAPHORE)
_SIDE_EFFECT = pltpu.SideEffectType.DATAFLOW_SIDE_EFFECTING


def _gather_copy(src, land, slot, send, recv, k, chip, c):
    return pltpu.make_async_remote_copy(src_ref=src, dst_ref=land.at[slot], send_sem=send.at[k], recv_sem=recv.at[k],
                                        device_id=(chip[0], chip[1], c), device_id_type=MESH)


def _gather_start(shards, groups):
    n, ng = len(shards), len(groups)
    lands = [jnp.broadcast_to(s[None], (N_CHIPS,) + s.shape) for s in shards]

    def body(*refs):
        ins, lnd = refs[:n], refs[n:2 * n]
        sends, recvs = refs[2 * n:2 * n + ng], refs[2 * n + ng:2 * n + 2 * ng]
        token = refs[-1]
        x, y, c, chips = _place()
        for gi, group in enumerate(groups):
            for pos, w in enumerate(group):
                for j, chip in enumerate(chips):
                    _gather_copy(ins[w], lnd[w], 2 * x + y, sends[gi], recvs[gi], 3 * pos + j, chip, c).start()
        token[...] = jnp.zeros_like(token)

    sems = [pltpu.SemaphoreType.DMA((3 * len(g),)) for g in groups]
    res = pl.pallas_call(
        body, name="gather_weights_start",
        out_shape=sems + sems + [pltpu.HBM(a.shape, a.dtype) for a in shards + lands] + [jax.ShapeDtypeStruct((8, LANES), F32)],
        in_specs=[_HBM] * (2 * n),
        out_specs=[_SEM] * (2 * ng) + [_HBM] * (2 * n) + [pl.BlockSpec(memory_space=pltpu.VMEM)],
        input_output_aliases={i: 2 * ng + i for i in range(2 * n)},
        compiler_params=pltpu.CompilerParams(has_side_effects=_SIDE_EFFECT),
    )(*[pltpu.with_memory_space_constraint(a, pltpu.HBM) for a in shards + lands])
    sem_pairs = list(zip(res[:ng], res[ng:2 * ng]))
    return sem_pairs, res[2 * ng:2 * ng + n], res[2 * ng + n:2 * ng + 2 * n], res[-1]


def _gather_wait(name, sem_pair, shards_thru, lands_thru, after):
    m = len(shards_thru)

    def body(*refs):
        ins, lnd = refs[:m], refs[m:2 * m]
        send, recv = refs[2 * m], refs[2 * m + 1]
        x, y, c, chips = _place()
        for pos in range(m):
            for j, chip in enumerate(chips):
                cp = _gather_copy(ins[pos], lnd[pos], 2 * chip[0] + chip[1], send, recv, 3 * pos + j, chip, c)
                cp.wait_send()
                cp.wait_recv()

    res = pl.pallas_call(
        body, name=name,
        out_shape=[pltpu.HBM(a.shape, a.dtype) for a in list(shards_thru) + list(lands_thru)],
        in_specs=[_HBM] * (2 * m) + [_SEM, _SEM, _ANY], out_specs=[_HBM] * (2 * m),
        input_output_aliases={i: i for i in range(2 * m)},
        compiler_params=pltpu.CompilerParams(has_side_effects=_SIDE_EFFECT),
    )(*shards_thru, *lands_thru, *sem_pair, after)
    return res[m:]


def _pair_exchange(name, packs):
    n = len(packs)

    def body(*refs):
        ins, outs, send, recv = refs[:n], refs[n:2 * n], refs[2 * n], refs[2 * n + 1]
        x, y, c, _ = _place()
        cps = []
        for g in range(n):
            cp = pltpu.make_async_remote_copy(src_ref=ins[g].at[:, pl.ds(1 - c, 1)], dst_ref=outs[g], send_sem=send.at[g],
                                              recv_sem=recv.at[g], device_id=(x, y, 1 - c), device_id_type=MESH)
            cp.start()
            cps.append(cp)
        for cp in cps:
            cp.wait()

    return pl.pallas_call(
        body, name=name, in_specs=[_ANY] * n, out_specs=[_ANY] * n,
        out_shape=[jax.ShapeDtypeStruct((N_CHIPS, 1) + p.shape[2:], p.dtype) for p in packs],
        scratch_shapes=[pltpu.SemaphoreType.DMA((n,)), pltpu.SemaphoreType.DMA((n,))],
    )(*packs)


def _chip_copy(src, land, src_slot, dst_slot, send, recv, k, chip, c):
    return pltpu.make_async_remote_copy(src_ref=src.at[src_slot], dst_ref=land.at[dst_slot], send_sem=send.at[k],
                                        recv_sem=recv.at[k], device_id=(chip[0], chip[1], c), device_id_type=MESH)


def _chip_exchange_start(name, parts):
    n = len(parts)
    lands = [lax.empty(p.shape, p.dtype) for p in parts]

    def body(*refs):
        ins, lnd, send, recv, token = refs[:n], refs[n:2 * n], refs[2 * n], refs[2 * n + 1], refs[-1]
        x, y, c, chips = _place()
        for g in range(n):
            for j, chip in enumerate(chips):
                _chip_copy(ins[g], lnd[g], 2 * chip[0] + chip[1], 2 * x + y, send, recv, 3 * g + j, chip, c).start()
        token[...] = jnp.zeros_like(token)

    sems = [pltpu.SemaphoreType.DMA((3 * n,))] * 2
    res = pl.pallas_call(
        body, name=name,
        out_shape=sems + [pltpu.HBM(a.shape, a.dtype) for a in list(parts) + lands] + [jax.ShapeDtypeStruct((8, LANES), F32)],
        in_specs=[_HBM] * (2 * n),
        out_specs=[_SEM] * 2 + [_HBM] * (2 * n) + [pl.BlockSpec(memory_space=pltpu.VMEM)],
        input_output_aliases={i: 2 + i for i in range(2 * n)},
        compiler_params=pltpu.CompilerParams(has_side_effects=_SIDE_EFFECT),
    )(*[pltpu.with_memory_space_constraint(a, pltpu.HBM) for a in list(parts) + lands])
    return res[:2], res[2:2 + n], res[2 + n:2 + 2 * n], res[-1]


def _chip_exchange_wait(name, sems, parts_thru, lands_thru, after):
    n = len(parts_thru)

    def body(*refs):
        ins, lnd, send, recv = refs[:n], refs[n:2 * n], refs[2 * n], refs[2 * n + 1]
        x, y, c, chips = _place()
        for g in range(n):
            for j, chip in enumerate(chips):
                cp = _chip_copy(ins[g], lnd[g], 2 * x + y, 2 * chip[0] + chip[1], send, recv, 3 * g + j, chip, c)
                cp.wait_send()
                cp.wait_recv()

    res = pl.pallas_call(
        body, name=name,
        out_shape=[pltpu.HBM(a.shape, a.dtype) for a in list(parts_thru) + list(lands_thru)],
        in_specs=[_HBM] * (2 * n) + [_SEM, _SEM, _ANY], out_specs=[_HBM] * (2 * n),
        input_output_aliases={i: i for i in range(2 * n)},
        compiler_params=pltpu.CompilerParams(has_side_effects=_SIDE_EFFECT),
    )(*parts_thru, *lands_thru, *sems, after)
    return res[:n], res[n:]


def _pair_share(halves):
    n = len(halves)

    def body(*refs):
        outs, send, recv = refs[n:2 * n], refs[2 * n], refs[2 * n + 1]
        x, y, c, _ = _place()
        cps = []
        for g in range(n):
            cp = pltpu.make_async_remote_copy(src_ref=outs[g].at[c], dst_ref=outs[g].at[c], send_sem=send.at[g],
                                              recv_sem=recv.at[g], device_id=(x, y, 1 - c), device_id_type=MESH)
            cp.start()
            cps.append(cp)
        for g in range(n):
            pltpu.make_async_remote_copy(src_ref=outs[g].at[c], dst_ref=outs[g].at[1 - c], send_sem=send.at[g],
                                         recv_sem=recv.at[g], device_id=(x, y, 1 - c), device_id_type=MESH).wait_recv()
        for cp in cps:
            cp.wait_send()

    return pl.pallas_call(
        body, name="grad_pair_share", in_specs=[_ANY] * n, out_specs=[_ANY] * n,
        out_shape=[jax.ShapeDtypeStruct(h.shape, h.dtype) for h in halves],
        input_output_aliases={g: g for g in range(n)},
        scratch_shapes=[pltpu.SemaphoreType.DMA((n,)), pltpu.SemaphoreType.DMA((n,))],
    )(*halves)


def _sum_over_devices(block):
    R, L = block.shape

    def gather(b_ref, o_ref, send, recv, loc):
        x, y, c, _ = _place()
        lc = pltpu.make_async_copy(b_ref, o_ref.at[4 * x + 2 * y + c], loc)
        lc.start()
        peers = [(1 - x if dx else x, 1 - y if dy else y, 1 - c if dc else c)
                 for dx in (0, 1) for dy in (0, 1) for dc in (0, 1) if dx or dy or dc]
        cps = []
        for j, peer in enumerate(peers):
            cp = pltpu.make_async_remote_copy(src_ref=b_ref, dst_ref=o_ref.at[4 * x + 2 * y + c], send_sem=send.at[j],
                                              recv_sem=recv.at[j], device_id=peer, device_id_type=MESH)
            cp.start()
            cps.append(cp)
        for j, (px, py, pc) in enumerate(peers):
            pltpu.make_async_remote_copy(src_ref=b_ref, dst_ref=o_ref.at[4 * px + 2 * py + pc], send_sem=send.at[j],
                                         recv_sem=recv.at[j], device_id=(px, py, pc), device_id_type=MESH).wait_recv()
        for cp in cps:
            cp.wait_send()
        lc.wait()

    blocks = pl.pallas_call(
        gather, name="small_grads_gather", in_specs=[_ANY], out_specs=_ANY,
        out_shape=jax.ShapeDtypeStruct((N_DEV, R, L), F32),
        scratch_shapes=[pltpu.SemaphoreType.DMA((N_DEV - 1,)), pltpu.SemaphoreType.DMA((N_DEV - 1,)), pltpu.SemaphoreType.DMA],
    )(block)

    def add(b_ref, o_ref):
        total = b_ref[0]
        for d in range(1, N_DEV):
            total = total + b_ref[d]
        o_ref[...] = total

    return pl.pallas_call(add, name="small_grads_add", out_shape=jax.ShapeDtypeStruct((R, L), F32))(blocks)


def _pair_add(name, pack, got):
    _, _, R, C = pack.shape
    t = _pick(R, ROW_TILES)
    c = lax.axis_index("c").astype(jnp.int32).reshape(1)

    def body(c_ref, p_ref, g_ref, o_ref):
        o_ref[...] = (p_ref[...].astype(F32) + g_ref[...].astype(F32)).astype(BF16)

    return pl.pallas_call(
        body, name=name,
        grid_spec=pltpu.PrefetchScalarGridSpec(
            num_scalar_prefetch=1, grid=(N_CHIPS, R // t),
            in_specs=[pl.BlockSpec((None, None, t, C), lambda k, i, c_ref: (k, c_ref[0], i, 0)),
                      pl.BlockSpec((None, None, t, C), lambda k, i, c_ref: (k, 0, i, 0))],
            out_specs=pl.BlockSpec((None, t, C), lambda k, i, c_ref: (k, i, 0))),
        out_shape=jax.ShapeDtypeStruct((N_CHIPS, R, C), BF16), compiler_params=_params(("parallel", "parallel")),
    )(c, pack, got)


def _chip_add(name, own, got):
    _, R, C = own.shape
    t = _pick(R, ROW_TILES)
    x, y, c, _ = _place()
    place = jnp.stack([c, 2 * x + y]).astype(jnp.int32)

    def body(place_ref, own_ref, g1_ref, g2_ref, g3_ref, o_ref):
        o_ref[...] = ((own_ref[...].astype(F32) + g1_ref[...].astype(F32)) + g2_ref[...].astype(F32)) + g3_ref[...].astype(F32)

    def other(d):
        return pl.BlockSpec((None, t, C), lambda i, place_ref: ((place_ref[1] + d) % N_CHIPS, i, 0))

    return pl.pallas_call(
        body, name=name,
        grid_spec=pltpu.PrefetchScalarGridSpec(
            num_scalar_prefetch=1, grid=(R // t,),
            in_specs=[pl.BlockSpec((None, t, C), lambda i, place_ref: (place_ref[1], i, 0)), other(1), other(2), other(3)],
            out_specs=pl.BlockSpec((None, t, C), lambda i, place_ref: (place_ref[0], i, 0))),
        out_shape=jax.ShapeDtypeStruct((2, R, C), F32), compiler_params=_params(("parallel",)),
    )(place, own, got, got, got)


_CUT = (2 * CONV_CH, 2 * CONV_CH + Q_LORA, 2 * CONV_CH + Q_LORA + KV_LORA, 2 * CONV_CH + Q_LORA + KV_LORA + QK_ROPE)
W_IN_CONV = (0, 2 * CONV_CH)
W_IN_GATES = (W_IN_CONV[1], 2 * D_MODEL)
W_IN_CQ = (W_IN_GATES[0] + W_IN_GATES[1], Q_LORA)
W_IN_KR = (W_IN_CQ[0] + W_IN_CQ[1], HEAD_PAD)
W_IN_CKV = (W_IN_KR[0] + W_IN_KR[1], KV_LORA)


def _pad_last(a, n):
    return jnp.pad(a, [(0, 0)] * (a.ndim - 1) + [(0, n - a.shape[-1])])


def _layout_w_in(w_in):
    kr = jnp.pad(w_in[:, _CUT[2]:_CUT[3]], ((0, 0), (QK_NOPE, HEAD_PAD - QK_NOPE - QK_ROPE)))
    return jnp.concatenate([w_in[:, :_CUT[0]], w_in[:, _CUT[3]:], w_in[:, _CUT[0]:_CUT[1]], kr, w_in[:, _CUT[1]:_CUT[2]]], axis=1)


def _layout_weights(w):
    out = dict(w)
    if "w_uq" in w:
        out["w_uq"] = _pad_last(w["w_uq"].reshape(Q_LORA, MLA_HEADS, QK_NOPE + QK_ROPE), HEAD_PAD).reshape(Q_LORA, MLA_HEADS * HEAD_PAD)
    if "w_ukv" in w:
        ukv = w["w_ukv"].reshape(KV_LORA, MLA_HEADS, QK_NOPE + V_DIM)
        uk = _pad_last(ukv[:, :, :QK_NOPE], HEAD_PAD).reshape(KV_LORA, MLA_HEADS * HEAD_PAD)
        uv = _pad_last(ukv[:, :, QK_NOPE:], HEAD_PAD).reshape(KV_LORA, MLA_HEADS * HEAD_PAD)
        out["w_ukv"] = jnp.concatenate([uk, uv], axis=1)
    if "w_mla_out" in w:
        mo = jnp.pad(w["w_mla_out"].reshape(MLA_HEADS, V_DIM, D_MODEL), ((0, 0), (0, HEAD_PAD - V_DIM), (0, 0)))
        out["w_mla_out"] = mo.reshape(MLA_HEADS * HEAD_PAD, D_MODEL)
    return out


def _unlayout_grads(g):
    out = dict(g)
    if "w_in" in g:
        gi = g["w_in"]
        win = lambda w: gi[:, w[0]:w[0] + w[1]]
        kr = gi[:, W_IN_KR[0] + QK_NOPE:W_IN_KR[0] + QK_NOPE + QK_ROPE]
        out["w_in"] = jnp.concatenate([win(W_IN_CONV), win(W_IN_CQ), win(W_IN_CKV), kr, win(W_IN_GATES)], axis=1)
    if "w_uq" in g:
        out["w_uq"] = g["w_uq"].reshape(Q_LORA, MLA_HEADS, HEAD_PAD)[:, :, :QK_NOPE + QK_ROPE].reshape(Q_LORA, -1)
    if "w_ukv" in g:
        gk = g["w_ukv"][:, :MLA_HEADS * HEAD_PAD].reshape(KV_LORA, MLA_HEADS, HEAD_PAD)[:, :, :QK_NOPE]
        gv = g["w_ukv"][:, MLA_HEADS * HEAD_PAD:].reshape(KV_LORA, MLA_HEADS, HEAD_PAD)[:, :, :V_DIM]
        out["w_ukv"] = jnp.concatenate([gk, gv], axis=2).reshape(KV_LORA, -1)
    if "w_mla_out" in g:
        out["w_mla_out"] = g["w_mla_out"].reshape(MLA_HEADS, HEAD_PAD, D_MODEL)[:, :V_DIM].reshape(MLA_HEADS * V_DIM, D_MODEL)
    return out


def _rope_tables(positions):
    half = QK_ROPE // 2
    inv_freq = ROPE_THETA ** (-jnp.arange(half, dtype=F32) / half)
    ang = positions.astype(F32).reshape(-1, 1) * inv_freq
    cos, sin = jnp.cos(ang), jnp.sin(ang)
    S = cos.shape[0]
    z16, z32, z64 = jnp.zeros((S, half), F32), jnp.zeros((S, QK_ROPE), F32), jnp.zeros((S, QK_NOPE), F32)
    c = jnp.concatenate([jnp.ones((S, QK_NOPE), F32), cos, cos, z32], axis=1)
    sa = jnp.concatenate([z64, -sin, z16, z32], axis=1)
    sb = jnp.concatenate([z64, z16, sin, z32], axis=1)
    return c, sa, sb


def _local_step(x, mem, positions, target, weight_fns, early_grads_fn, late_grads_fn, sm):
    S = x.shape[0]
    HW = MLA_HEADS * HEAD_PAD
    rope_c, rope_sa, rope_sb = _rope_tables(positions)
    qk_scale = (QK_NOPE + QK_ROPE) ** -0.5

    def k_rms1(x_ref, g_ref, u_ref):
        u_ref[...] = _rms_fwd(x_ref[...], g_ref[...]).astype(BF16)

    u1, = _rows("rms_mix", k_rms1, [x], [sm["norm_mix_g"]], [(D_MODEL, BF16)])
    w_in, conv_w = weight_fns[0](u1)

    def epi_glu(acc, xs, outs):
        a, gt = acc[:, 0:CONV_CH], acc[:, CONV_CH:2 * CONV_CH]
        outs[0][...] = acc[...].astype(BF16)
        outs[1][...] = a * _sigmoid(gt)

    conv_in, z0 = _mm("proj_conv", u1, w_in, "nn", [(2 * CONV_CH, BF16), (CONV_CH, F32)], epi_glu, b_cols=W_IN_CONV)
    c_q = _mm_plain("proj_cq", u1, w_in, "nn", b_cols=W_IN_CQ)
    c_kv = _mm_plain("proj_ckv", u1, w_in, "nn", b_cols=W_IN_CKV)
    kr_raw = _mm_plain("proj_krope", u1, w_in, "nn", b_cols=W_IN_KR)

    def epi_sigmoid(acc, xs, outs):
        outs[0][...] = _sigmoid(acc[...]).astype(BF16)

    gates, = _mm("proj_gates", u1, w_in, "nn", [(2 * D_MODEL, BF16)], epi_sigmoid, b_cols=W_IN_GATES)

    z1, z3 = _conv_fwd(z0, conv_w, sm["conv_b"], sm["conv_ln_g"], sm["conv_ln_b"])
    wl = weight_fns[1](z1)
    conv_out = _mm_plain("conv_out", z3, wl["w_conv_out"], "nn", dtype=BF16)

    def k_lora_norm(cq_ref, ckv_ref, gq_ref, gkv_ref, qn_ref, kvn_ref):
        qn_ref[...] = _rms_fwd(cq_ref[...], gq_ref[...]).astype(BF16)
        kvn_ref[...] = _rms_fwd(ckv_ref[...], gkv_ref[...]).astype(BF16)

    qn, kvn = _rows("lora_norm", k_lora_norm, [c_q, c_kv], [sm["q_norm_g"], sm["kv_norm_g"]],
                    [(Q_LORA, BF16), (KV_LORA, BF16)])

    def epi_q(acc, xs, outs):
        c, sa, sb = xs[0][...], xs[1][...], xs[2][...]
        for h in range(MLA_HEADS):
            lo = h * HEAD_PAD
            outs[0][:, lo:lo + HEAD_PAD] = (_rope(acc[:, lo:lo + HEAD_PAD], c, sa, sb, 1.0) * qk_scale).astype(BF16)

    q_att, = _mm("q_up", qn, wl["w_uq"], "nn", [(HW, BF16)], epi_q, row_x=[rope_c, rope_sa, rope_sb], tn=HW)

    def epi_kv(acc, xs, outs):
        kr = _rope(xs[0][...], xs[1][...], xs[2][...], xs[3][...], 1.0)
        kr = kr + _neg_ones(kr.shape, STAT_COL_QK)
        vconst = _neg_ones(kr.shape, STAT_COL_V)
        for h in range(MLA_HEADS):
            lo = h * HEAD_PAD
            outs[0][:, lo:lo + HEAD_PAD] = (acc[:, lo:lo + HEAD_PAD] + kr).astype(BF16)
            outs[1][:, lo:lo + HEAD_PAD] = (acc[:, HW + lo:HW + lo + HEAD_PAD] + vconst).astype(BF16)

    k_att, v_att = _mm("kv_up", kvn, wl["w_ukv"], "nn", [(HW, BF16), (HW, BF16)], epi_kv,
                       row_x=[kr_raw, rope_c, rope_sa, rope_sb], tn=2 * HW)

    o_att, q_aug = _flash_fwd(q_att, k_att, v_att)
    wl.update(weight_fns[2](o_att))

    def epi_merge(acc, xs, outs):
        for rows in _row_chunks(acc.shape[0]):
            mo = acc[rows, :]
            g0, g1 = xs[0][rows, 0:D_MODEL].astype(F32), xs[0][rows, D_MODEL:].astype(F32)
            outs[0][rows, :] = mo.astype(BF16)
            outs[1][rows, :] = (g0 * xs[1][rows, :].astype(F32) + g1 * mo).astype(BF16)

    mla_out, merged = _mm("mla_out_merge", o_att, wl["w_mla_out"], "nn", [(D_MODEL, BF16), (D_MODEL, BF16)], epi_merge,
                          row_x=[gates, conv_out], tn=D_MODEL)

    def epi_res_norm(acc, xs, outs):
        h = xs[0][...] + acc[...]
        outs[0][...] = h
        outs[1][...] = _rms_fwd(h, xs[1][...]).astype(BF16)

    h1, u2 = _mm("mix_out", merged, wl["w_out"], "nn", [(D_MODEL, F32), (D_MODEL, BF16)], epi_res_norm,
                 row_x=[x], vec_x=[sm["norm_xattn_g"]], tn=D_MODEL)

    xscale = X_HEAD_DIM ** -0.5

    def epi_scale(acc, xs, outs):
        outs[0][...] = (acc[...] * xscale).astype(BF16)

    xq, = _mm("xattn_q", u2, wl["w_xq"], "nn", [(X_HEADS * X_HEAD_DIM, BF16)], epi_scale)

    def k_mem_norm(m_ref, g_ref, o_ref):
        o_ref[...] = _rms_fwd(m_ref[...], g_ref[...]).astype(BF16)

    mem_n, = _rows("mem_norm", k_mem_norm, [mem], [sm["norm_mem_g"]], [(D_MODEL, BF16)])
    kvx = _mm_plain("xattn_kv", mem_n, wl["w_xkv"], "nn", dtype=BF16)
    ox = _xattn_fwd(xq, kvx)
    h2, u3 = _mm("xattn_out", ox, wl["w_xo"], "nn", [(D_MODEL, F32), (D_MODEL, BF16)], epi_res_norm,
                 row_x=[h1], vec_x=[sm["norm_mlp_g"]], tn=D_MODEL)

    def epi_relu2(acc, xs, outs):
        r = jnp.maximum(acc[...], 0.0)
        outs[0][...] = (r * r).astype(BF16)

    hid, = _mm("mlp_up", u3, wl["w_mlp1"], "nn", [(D_FF, BF16)], epi_relu2)

    def epi_final(acc, xs, outs):
        g = xs[2][...]
        for rows in _row_chunks(acc.shape[0]):
            h = xs[0][rows, :] + acc[rows, :]
            e = _rms_fwd(h, g) - xs[1][rows, :]
            part = 0.5 * jnp.sum(jnp.mean(e * e, axis=-1, keepdims=True), axis=0, keepdims=True)
            dh, dg = _rms_bwd(h, g, e * (1.0 / D_MODEL))
            outs[0][rows, :] = dh
            outs[1][rows, :] = dh.astype(BF16)
            _accum(outs[2], jnp.broadcast_to(part, outs[2].shape), first=rows.start == 0)
            _accum(outs[3], dg, first=rows.start == 0)

    dh3, dh3b, loss, g_final = _mm("mlp_down_loss", hid, wl["w_mlp2"], "nn", [(D_MODEL, F32), (D_MODEL, BF16)], epi_final,
                                   row_x=[h2, target], vec_x=[sm["final_norm_g"]], sums=[(1, LANES), (1, D_MODEL)],
                                   tn=D_MODEL, tk=1024)

    def epi_drelu2(acc, xs, outs):
        outs[0][...] = (acc[...] * (2.0 * jnp.sqrt(xs[0][...].astype(F32)))).astype(BF16)

    da1, = _mm("mlp_down_dx", dh3b, wl["w_mlp2"], "nt", [(D_FF, BF16)], epi_drelu2, tile_x=[hid])
    g_mlp2 = _mm_plain("mlp_down_dw", hid, dh3b, "tn")
    g_mlp1 = _mm_plain("mlp_up_dw", u3, da1, "tn")

    def epi_norm_bwd(acc, xs, outs):
        for rows in _row_chunks(acc.shape[0]):
            dx, dg = _rms_bwd(xs[0][rows, :], xs[2][...], acc[rows, :])
            dh = xs[1][rows, :] + dx
            outs[0][rows, :] = dh
            if len(outs) == 3:
                outs[1][rows, :] = dh.astype(BF16)
            _accum(outs[-1], dg, first=rows.start == 0)

    def dx_norm_bwd(name, dy, w, xin, dres, vecs, with_bf16=True):
        outs = [(D_MODEL, F32), (D_MODEL, BF16)] if with_bf16 else [(D_MODEL, F32)]
        return _mm(name, dy, w, "nt", outs, epi_norm_bwd, row_x=[xin, dres], vec_x=vecs, sums=[(1, D_MODEL)],
                   tn=D_MODEL, tk=_pick(dy.shape[1], (1024, 768, 512)))

    dh2, dh2b, g_norm_mlp = dx_norm_bwd("mlp_up_dx_norm", da1, wl["w_mlp1"], h2, dh3, [sm["norm_mlp_g"]])

    dox = _mm_plain("xattn_out_dx", dh2b, wl["w_xo"], "nt", dtype=BF16)
    g_xo = _mm_plain("xattn_out_dw", ox, dh2b, "tn")
    dxq, dkvx = _xattn_bwd(xq, kvx, dox)
    g_xq = _mm_plain("xattn_q_dw", u2, dxq, "tn")
    g_xkv = _mm_plain("xattn_kv_dw", mem_n, dkvx, "tn")
    dmem_n = _mm_plain("xattn_kv_dx", dkvx, wl["w_xkv"], "nt")

    def k_mem_bwd(m_ref, d_ref, g_ref, dg_ref):
        _, dg = _rms_bwd(m_ref[...], g_ref[...], d_ref[...])
        _accum(dg_ref, dg)

    g_norm_mem, = _rows("mem_norm_bwd", k_mem_bwd, [mem, dmem_n], [sm["norm_mem_g"]], [], [(1, D_MODEL)])
    token = early_grads_fn(dict(w_mlp1=g_mlp1, w_mlp2=g_mlp2, w_xo=g_xo, w_xq=g_xq, w_xkv=g_xkv))
    dh1, dh1b, g_norm_xattn = dx_norm_bwd("xattn_q_dx_norm", dxq, wl["w_xq"], h1, dh2, [sm["norm_xattn_g"], token])

    dmerged = _mm_plain("mix_out_dx", dh1b, wl["w_out"], "nt")
    g_out = _mm_plain("mix_out_dw", merged, dh1b, "tn")

    def k_merge_bwd(dm_ref, g_ref, co_ref, mo_ref, dco_ref, dmo_ref, dgl_ref):
        dm = dm_ref[...]
        g0, g1 = g_ref[:, 0:D_MODEL].astype(F32), g_ref[:, D_MODEL:].astype(F32)
        dco_ref[...] = (dm * g0).astype(BF16)
        dmo_ref[...] = (dm * g1).astype(BF16)
        dgl_ref[:, 0:D_MODEL] = (dm * co_ref[...].astype(F32) * g0 * (1.0 - g0)).astype(BF16)
        dgl_ref[:, D_MODEL:] = (dm * mo_ref[...].astype(F32) * g1 * (1.0 - g1)).astype(BF16)

    dconv_out, dmla_out, dgl = _rows("merge_bwd", k_merge_bwd, [dmerged, gates, conv_out, mla_out], [],
                                     [(D_MODEL, BF16), (D_MODEL, BF16), (2 * D_MODEL, BF16)], tile=256)

    def epi_do(acc, xs, outs):
        for h in range(MLA_HEADS):
            lo = h * HEAD_PAD
            do = acc[:, lo:lo + HEAD_PAD]
            delta = jnp.sum(do * xs[0][:, lo:lo + HEAD_PAD].astype(F32), axis=-1, keepdims=True)
            outs[0][:, lo:lo + HEAD_PAD] = _put_stats(do.astype(BF16), delta, STAT_COL_V)

    do_aug, = _mm("mla_out_dx", dmla_out, wl["w_mla_out"], "nt", [(HW, BF16)], epi_do, row_x=[o_att], tn=HW)
    g_mla_out = _mm_plain("mla_out_dw", o_att, dmla_out, "tn")
    dq_att, dk_att, dv_att = _flash_bwd(q_aug, k_att, v_att, do_aug)

    def k_rope_bwd(dq_ref, dk_ref, dv_ref, c_ref, sa_ref, sb_ref, dqr_ref, dkv_ref, dkr_ref):
        c, sa, sb = c_ref[...], sa_ref[...], sb_ref[...]
        lane = lax.broadcasted_iota(jnp.int32, c.shape, 1)
        nope = (lane < QK_NOPE).astype(F32)
        ropem = ((lane >= QK_NOPE) & (lane < QK_NOPE + QK_ROPE)).astype(F32)
        dkr = jnp.zeros(c.shape, F32)
        for h in range(MLA_HEADS):
            lo = h * HEAD_PAD
            dqr_ref[:, lo:lo + HEAD_PAD] = (_rope(dq_ref[:, lo:lo + HEAD_PAD], c, sa, sb, -1.0) * qk_scale).astype(BF16)
            dk = dk_ref[:, lo:lo + HEAD_PAD]
            dkv_ref[:, lo:lo + HEAD_PAD] = (dk * nope).astype(BF16)
            dkr = dkr + dk
        dkv_ref[:, HW:] = dv_ref[...]
        dkr_ref[...] = (_rope(dkr * ropem, c, sa, sb, -1.0) * ropem).astype(BF16)

    dq_raw, dkv_cat, dkr = _rows("rope_bwd", k_rope_bwd, [dq_att, dk_att, dv_att, rope_c, rope_sa, rope_sb], [],
                                 [(HW, BF16), (2 * HW, BF16), (HEAD_PAD, BF16)], tile=256)
    g_uq = _mm_plain("q_up_dw", qn, dq_raw, "tn")
    dqn = _mm_plain("q_up_dx", dq_raw, wl["w_uq"], "nt")
    g_ukv = _mm_plain("kv_up_dw", kvn, dkv_cat, "tn")
    dkvn = _mm_plain("kv_up_dx", dkv_cat, wl["w_ukv"], "nt")

    def k_lora_bwd(cq_ref, ckv_ref, dqn_ref, dkvn_ref, gq_ref, gkv_ref, dcq_ref, dckv_ref, dgq_ref, dgkv_ref):
        dcq, dgq = _rms_bwd(cq_ref[...], gq_ref[...], dqn_ref[...])
        dckv, dgkv = _rms_bwd(ckv_ref[...], gkv_ref[...], dkvn_ref[...])
        dcq_ref[...] = dcq.astype(BF16)
        dckv_ref[...] = dckv.astype(BF16)
        _accum(dgq_ref, dgq)
        _accum(dgkv_ref, dgkv)

    dc_q, dc_kv, g_q_norm, g_kv_norm = _rows("lora_norm_bwd", k_lora_bwd, [c_q, c_kv, dqn, dkvn],
                                              [sm["q_norm_g"], sm["kv_norm_g"]], [(Q_LORA, BF16), (KV_LORA, BF16)],
                                              [(1, Q_LORA), (1, KV_LORA)])

    dz3 = _mm_plain("conv_out_dx", dconv_out, wl["w_conv_out"], "nt")
    g_conv_out = _mm_plain("conv_out_dw", z3, dconv_out, "tn")
    dz1, g_ln_g, g_ln_b, g_conv_b = _conv_bwd_norm(dz3, z1, sm["conv_ln_g"], sm["conv_ln_b"])
    dconv_in, g_conv_w = _conv_bwd_taps(dz1, z0, conv_in, conv_w)

    dproj = jnp.concatenate([dconv_in, dgl, dc_q, dkr, dc_kv], axis=1)
    g_in = _mm_plain("proj_dw", u1, dproj, "tn")
    token = late_grads_fn(dict(w_in=g_in, conv_w=g_conv_w[:CONV_WIDTH], w_conv_out=g_conv_out, w_uq=g_uq, w_ukv=g_ukv,
                               w_mla_out=g_mla_out, w_out=g_out))
    grad_x, g_norm_mix = dx_norm_bwd("proj_dx_norm", dproj, w_in, x, dh1, [sm["norm_mix_g"], token], with_bf16=False)

    small = dict(norm_mix_g=g_norm_mix, conv_b=g_conv_b, conv_ln_g=g_ln_g, conv_ln_b=g_ln_b, q_norm_g=g_q_norm,
                 kv_norm_g=g_kv_norm, norm_xattn_g=g_norm_xattn, norm_mem_g=g_norm_mem, norm_mlp_g=g_norm_mlp,
                 final_norm_g=g_final)
    return loss, grad_x, small


def _shard(a, k, axis):
    n = a.shape[axis] // N_CHIPS
    return lax.slice_in_dim(a, k * n, (k + 1) * n, axis=axis)


def _pack_small(grads, loss):
    flat = jnp.concatenate([grads[n].reshape(-1) for n in SMALL] + [loss.reshape(-1)[:1]])
    rows = -(-flat.shape[0] // (8 * LANES)) * 8
    return jnp.pad(flat, (0, rows * LANES - flat.shape[0])).reshape(rows, LANES)


def _pack_groups(shapes, names):
    groups = {}
    for n in names:
        groups.setdefault(shapes[n][1], []).append(n)
    return groups


def _pad_rows(a, mult):
    return jnp.pad(a, ((0, -a.shape[0] % mult), (0, 0)))


def _pack_grads(grads, shapes, names):
    packs = []
    for width, group in _pack_groups(shapes, names).items():
        per_chip = [jnp.concatenate([_pad_rows(_shard(grads[n], k, SHARD_AXIS[n]).astype(BF16), PACK_ROW_ALIGN) for n in group])
                    for k in range(N_CHIPS)]
        rows = per_chip[0].shape[0]
        packs.append(jnp.stack(per_chip).reshape(N_CHIPS, 2, rows // 2, width))
    return packs


def _unpack_grads(fulls, shapes, names):
    out = {}
    for full, group in zip(fulls, _pack_groups(shapes, names).values()):
        flat, at = full.reshape(-1, full.shape[-1]), 0
        for n in group:
            rows = shapes[n][0]
            out[n] = flat[at:at + rows]
            at += rows + (-rows % PACK_ROW_ALIGN)
    return out


def _unpack(flat, names, shapes):
    out, at = {}, 0
    for n in names:
        size = math.prod(shapes[n])
        out[n] = flat[at:at + size].reshape(shapes[n])
        at += size
    return out, at


def kernel(x, mem, positions, norm_mix_g, w_in, conv_w, conv_b, conv_ln_g, conv_ln_b, w_conv_out, q_norm_g, w_uq, kv_norm_g, w_ukv, w_mla_out, w_out, norm_xattn_g, norm_mem_g, w_xq, w_xkv, w_xo, norm_mlp_g, w_mlp1, w_mlp2, final_norm_g, loss_target, m_norm_mix_g, m_w_in, m_conv_w, m_conv_b, m_conv_ln_g, m_conv_ln_b, m_w_conv_out, m_q_norm_g, m_w_uq, m_kv_norm_g, m_w_ukv, m_w_mla_out, m_w_out, m_norm_xattn_g, m_norm_mem_g, m_w_xq, m_w_xkv, m_w_xo, m_norm_mlp_g, m_w_mlp1, m_w_mlp2, m_final_norm_g, v_norm_mix_g, v_w_in, v_conv_w, v_conv_b, v_conv_ln_g, v_conv_ln_b, v_w_conv_out, v_q_norm_g, v_w_uq, v_kv_norm_g, v_w_ukv, v_w_mla_out, v_w_out, v_norm_xattn_g, v_norm_mem_g, v_w_xq, v_w_xkv, v_w_xo, v_norm_mlp_g, v_w_mlp1, v_w_mlp2, v_final_norm_g):
    args = dict(locals())
    w = {n: args[n] for n in WEIGHTS}
    m = {n: args["m_" + n] for n in WEIGHTS}
    v = {n: args["v_" + n] for n in WEIGHTS}

    shards = [w[n][0].astype(F32 if n == "conv_w" else BF16) for n in BIG]
    bounds = (0,) + WEIGHT_WAITS + (len(BIG),)
    spans = [slice(lo, hi) for lo, hi in zip(bounds[:-1], bounds[1:])]
    sem_pairs, shards_thru, lands_thru, token = _gather_start(shards, [list(range(len(BIG)))[sp] for sp in spans])

    def unshard(n, g):
        ax = SHARD_AXIS[n]
        return jnp.moveaxis(g, 0, ax).reshape(g.shape[1:1 + ax] + (N_CHIPS * g.shape[1 + ax],) + g.shape[2 + ax:])

    def wait_fn(i):
        def fn(after):
            lands = _gather_wait(f"gather_weights_wait_{i}", sem_pairs[i], shards_thru[spans[i]], lands_thru[spans[i]], after)
            full = {n: unshard(n, g) for n, g in zip(BIG[spans[i]], lands)}
            return (_layout_w_in(full["w_in"]), full["conv_w"]) if i == 0 else _layout_weights(full)
        return fn

    sm = {n: w[n].reshape(1, -1) for n in SMALL}
    sm["norm_mix_g"] = sm["norm_mix_g"] + token[0, 0]

    shapes = {n: w[n].shape[1:] if n in BIG else w[n].shape for n in WEIGHTS}
    late_names = [n for n in BIG if n not in EARLY_GRADS]
    inflight = {}

    def send_grads(tag, names):
        def fn(g):
            packs = _pack_grads(_unlayout_grads(g), shapes, names)
            got = _pair_exchange(f"grad_pair_exchange_{tag}", packs)
            pairs = [_pair_add(f"grad_pair_add_{tag}_{i}", p, r) for i, (p, r) in enumerate(zip(packs, got))]
            *inflight[tag], token = _chip_exchange_start(f"grad_chip_exchange_{tag}_start", pairs)
            return token
        return fn

    loss, grad_x, g_small = _local_step(x[0], mem[0], positions, loss_target[0], [wait_fn(i) for i in range(3)],
                                        send_grads("early", EARLY_GRADS), send_grads("late", late_names), sm)

    halves, counts = [], {}
    for tag in ("late", "early"):
        own, got = _chip_exchange_wait(f"grad_chip_exchange_{tag}_wait", *inflight[tag], grad_x)
        halves += [_chip_add(f"grad_chip_add_{tag}_{i}", p, g) for i, (p, g) in enumerate(zip(own, got))]
        counts[tag] = len(own)
    fulls = _pair_share(halves)
    g_sum = _unpack_grads(fulls[:counts["late"]], shapes, late_names)
    g_sum.update(_unpack_grads(fulls[counts["late"]:], shapes, EARLY_GRADS))
    small_flat = _sum_over_devices(_pack_small(g_small, loss)).reshape(-1)
    g_small, at = _unpack(small_flat, SMALL, shapes)
    g_sum.update(g_small)
    loss_sum = small_flat[at]

    out_g, out_d, out_m, out_v = [], [], [], []
    for n in WEIGHTS:
        g, d, nm, nv = _adamw("adamw_" + n, w[n], g_sum[n], m[n], v[n])
        out_g.append(g)
        out_d.append(d)
        out_m.append(nm)
        out_v.append(nv)
    return (loss_sum, grad_x[None], *out_g, *out_d, *out_m, *out_v)
```

```python
import math

import jax
import jax.numpy as jnp
from jax import lax
from jax.experimental import pallas as pl
from jax.experimental.pallas import tpu as pltpu

F32 = jnp.float32
BF16 = jnp.bfloat16
MESH = pl.DeviceIdType.MESH

D_MODEL = 1024
CONV_CH = 512
CONV_WIDTH = 31
MLA_HEADS = 8
QK_NOPE = 64
QK_ROPE = 32
V_DIM = 64
Q_LORA = 384
KV_LORA = 256
MEM_LEN = 256
X_HEADS = 4
X_HEAD_DIM = 128
D_FF = 4096
ROPE_THETA = 10000.0
EPS = 1e-6
HEAD_PAD = 128
STAT_COL_QK = QK_NOPE + QK_ROPE
STAT_COL_V = V_DIM
HALO = 32
N_CHIPS = 4
LANES = 128

ADAM_LR = 0.001
ADAM_B1 = 0.9
ADAM_B2 = 0.999
ADAM_EPS = 1e-08
ADAM_WD = 0.01
ADAM_STEP = 10

VMEM_LIMIT = 52 * 1024 * 1024
ROW_TILES = (1024, 512, 256, 128, 64, 32, 16)
PACK_ROW_ALIGN = 32
N_DEV = 8
NEG = -1e30

BIG = ["w_in", "conv_w", "w_conv_out", "w_uq", "w_ukv", "w_mla_out", "w_out", "w_xq", "w_xkv", "w_xo", "w_mlp1", "w_mlp2"]
WEIGHT_WAITS = (2, 5)
SHARD_AXIS = {"w_in": 1, "w_conv_out": 1, "w_uq": 1, "w_ukv": 1, "w_mla_out": 1, "w_out": 0, "w_xq": 0, "w_xkv": 0,
              "w_xo": 1, "w_mlp1": 1, "w_mlp2": 0, "conv_w": 1}
EARLY_GRADS = ["w_mlp1", "w_mlp2", "w_xkv", "w_xq", "w_xo"]
SMALL = ["norm_mix_g", "conv_b", "conv_ln_g", "conv_ln_b", "q_norm_g", "kv_norm_g", "norm_xattn_g", "norm_mem_g",
         "norm_mlp_g", "final_norm_g"]
WEIGHTS = ["norm_mix_g", "w_in", "conv_w", "conv_b", "conv_ln_g", "conv_ln_b", "w_conv_out", "q_norm_g", "w_uq",
           "kv_norm_g", "w_ukv", "w_mla_out", "w_out", "norm_xattn_g", "norm_mem_g", "w_xq", "w_xkv", "w_xo",
           "norm_mlp_g", "w_mlp1", "w_mlp2", "final_norm_g"]


def _pick(n, prefs):
    for p in prefs:
        if n % p == 0:
            return p
    return n


def _params(sem):
    return pltpu.CompilerParams(dimension_semantics=sem, vmem_limit_bytes=VMEM_LIMIT)


_DIMS = {"nn": (((1,), (0,)), ((), ())), "nt": (((1,), (1,)), ((), ())), "tn": (((0,), (0,)), ((), ()))}


def _mm(name, a, b, mode, outs, epi, row_x=(), tile_x=(), vec_x=(), sums=(), tm=None, tn=None, tk=None, b_cols=None):
    if mode == "nn":
        (M, K), (_, N) = a.shape, b.shape
        if b_cols is not None:
            N = b_cols[1]
    elif mode == "nt":
        (M, K), (N, _) = a.shape, b.shape
    else:
        (K, M), (_, N) = a.shape, b.shape
    tm = tm or _pick(M, (1024, 512, 384, 256, 128))
    tn = tn or _pick(N, (1024, 768, 512, 384, 256, 128))
    tk = tk or _pick(K, (2048, 1920, 1024, 768, 512, 384, 256, 128))
    nk = K // tk
    rows_inner = nk == 1 and N // tn > 1
    grid = (N // tn, M // tm, nk) if rows_inner else (M // tm, N // tn, nk)

    def spec(shape, f):
        return pl.BlockSpec(shape, (lambda j, i, k: f(i, j, k)) if rows_inner else f)

    b_off = 0
    if b_cols is not None:
        assert mode == "nn" and b_cols[0] % tn == 0, (name, b_cols, tn)
        b_off = b_cols[0] // tn
    a_spec = spec((tk, tm), lambda i, j, k: (k, i)) if mode == "tn" else spec((tm, tk), lambda i, j, k: (i, k))
    b_spec = spec((tn, tk), lambda i, j, k: (j, k)) if mode == "nt" else spec((tk, tn), lambda i, j, k: (k, j + b_off))
    in_specs = [a_spec, b_spec]
    in_specs += [spec((tm, r.shape[1]), lambda i, j, k: (i, 0)) for r in row_x]
    in_specs += [spec((tm, tn), lambda i, j, k: (i, j)) for _ in tile_x]
    in_specs += [spec(v.shape, lambda i, j, k: (0, 0)) for v in vec_x]
    out_specs, out_shape = [], []
    for w, dt in outs:
        if tn == N:
            out_specs.append(spec((tm, w), lambda i, j, k: (i, 0)))
        else:
            assert w == N, (name, w, N)
            out_specs.append(spec((tm, tn), lambda i, j, k: (i, j)))
        out_shape.append(jax.ShapeDtypeStruct((M, w), dt))
    for shp in sums:
        assert tn == N and not rows_inner, name
        out_specs.append(spec(shp, lambda i, j, k: (0, 0)))
        out_shape.append(jax.ShapeDtypeStruct(shp, F32))
    nx = len(row_x) + len(tile_x) + len(vec_x)
    dims = _DIMS[mode]

    def body(a_ref, b_ref, *rest):
        x_refs, out_refs, acc_ref = rest[:nx], rest[nx:nx + len(outs) + len(sums)], rest[-1]
        av, bv = a_ref[...], b_ref[...]
        if av.dtype != BF16:
            av = av.astype(BF16)
        if bv.dtype != BF16:
            bv = bv.astype(BF16)
        prod = lax.dot_general(av, bv, dims, preferred_element_type=F32)
        if nk == 1:
            acc_ref[...] = prod
            epi(acc_ref, x_refs, out_refs)
        else:
            k = pl.program_id(2)

            @pl.when(k == 0)
            def _():
                acc_ref[...] = prod

            @pl.when(k > 0)
            def _():
                acc_ref[...] += prod

            @pl.when(k == nk - 1)
            def _():
                epi(acc_ref, x_refs, out_refs)

    res = pl.pallas_call(
        body, name=name, grid=grid, in_specs=in_specs, out_specs=out_specs, out_shape=out_shape,
        scratch_shapes=[pltpu.VMEM((tm, tn), F32)],
        compiler_params=_params(("arbitrary",) * 3 if sums else ("parallel", "parallel", "arbitrary")),
    )(a, b, *row_x, *tile_x, *vec_x)
    return res


def _epi_store(acc_ref, x_refs, out_refs):
    for o in out_refs:
        o[...] = acc_ref[...].astype(o.dtype)


def _mm_plain(name, a, b, mode, dtype=F32, **kw):
    n = kw["b_cols"][1] if kw.get("b_cols") else (b.shape[0] if mode == "nt" else b.shape[1])
    return _mm(name, a, b, mode, [(n, dtype)], _epi_store, **kw)[0]


def _rows(name, body, row_ins, vec_ins, row_outs, acc_outs=(), tile=512, window=None):
    S = row_ins[0].shape[0]
    t = _pick(S, (tile, 256, 128, 64, 32, 16, 8))
    in_specs = [pl.BlockSpec((t, r.shape[1]), lambda i: (i, 0)) for r in row_ins]
    in_specs += [pl.BlockSpec(v.shape, lambda i: (0, 0)) for v in vec_ins]
    out_specs = [pl.BlockSpec((t, w), lambda i: (i, 0)) for w, _ in row_outs]
    out_specs += [pl.BlockSpec(shp, lambda i: (0, 0)) for shp in acc_outs]
    out_shape = [jax.ShapeDtypeStruct((S, w), dt) for w, dt in row_outs]
    out_shape += [jax.ShapeDtypeStruct(shp, F32) for shp in acc_outs]
    extra, aliases, n_in = [], {}, len(row_ins) + len(vec_ins)
    if window is not None:
        k, (first, width), total, buffer = window
        assert first % width == 0 and row_outs[k][0] == width, (name, window[:3])
        out_specs[k] = pl.BlockSpec((t, width), lambda i: (i, first // width))
        out_shape[k] = jax.ShapeDtypeStruct((S, total), row_outs[k][1])
        if buffer is not None:
            extra, aliases = [buffer], {n_in: k}
            in_specs.append(pl.BlockSpec(memory_space=pl.ANY))

    def call(*refs):
        body(*refs[:n_in], *refs[n_in + len(extra):])

    sem = ("arbitrary",) if acc_outs else ("parallel",)
    return pl.pallas_call(
        call, name=name, grid=(S // t,), in_specs=in_specs, out_specs=out_specs,
        out_shape=out_shape, input_output_aliases=aliases, compiler_params=_params(sem),
    )(*row_ins, *vec_ins, *extra)


def _accum(ref, val, first=True):
    if first:
        @pl.when(pl.program_id(0) == 0)
        def _():
            ref[...] = jnp.zeros_like(ref)

    ref[...] += val


EPILOGUE_ROWS = 256


def _row_chunks(n):
    step = min(EPILOGUE_ROWS, n)
    return [slice(r, r + step) for r in range(0, n, step)]


def _colsum(v):
    return jnp.sum(v, axis=0, keepdims=True)


def _rms_fwd(x, g):
    r = lax.rsqrt(jnp.mean(x * x, axis=-1, keepdims=True) + EPS)
    return x * r * g


def _rms_bwd(x, g, du):
    r = lax.rsqrt(jnp.mean(x * x, axis=-1, keepdims=True) + EPS)
    xn = x * r
    gdu = du * g
    dx = r * (gdu - xn * jnp.mean(xn * gdu, axis=-1, keepdims=True))
    return dx, _colsum(du * xn)


def _sigmoid(v):
    return 1.0 / (1.0 + jnp.exp(-v))


def _rope(v, c, sa, sb, sign):
    return v * c + sign * (pltpu.roll(v, HEAD_PAD - QK_ROPE // 2, 1) * sa + pltpu.roll(v, QK_ROPE // 2, 1) * sb)


def _split3(v):
    hi = v.astype(BF16)
    r1 = v - hi.astype(F32)
    mid = r1.astype(BF16)
    lo = (r1 - mid.astype(F32)).astype(BF16)
    return hi, mid, lo


def _put_stats(base, stat, col):
    hi, mid, lo = _split3(stat)
    lane = lax.broadcasted_iota(jnp.int32, base.shape, 1)
    out = jnp.where(lane == col, hi, base)
    out = jnp.where(lane == col + 1, mid, out)
    return jnp.where(lane == col + 2, lo, out)


def _neg_ones(shape, col):
    lane = lax.broadcasted_iota(jnp.int32, shape, 1)
    return jnp.where((lane >= col) & (lane < col + 3), -1.0, 0.0).astype(F32)


def _shifted(ext, t):
    p = ext.shape[0]
    for b in range(8):
        rb = ext if b == 0 else pltpu.roll(ext, p - b, 0)
        for a in range(HALO // 8 + 1):
            if 8 * a + b <= HALO:
                yield 8 * a + b, rb[8 * a:8 * a + t]


def _conv_fwd(z0, conv_w, conv_b, ln_g, ln_b):
    S, C = z0.shape
    t = _pick(S, (512, 256, 128, 64, 32))
    per = t // HALO

    def body(cur_ref, prev_ref, w_ref, b_ref, g_ref, beta_ref, z1_ref, z3_ref, ext_ref):
        i = pl.program_id(0)
        ext_ref[0:HALO, :] = jnp.where(i > 0, prev_ref[...], 0.0)
        ext_ref[HALO:, :] = cur_ref[...]
        ext = ext_ref[...]
        acc = jnp.zeros((t, C), F32)
        for d, win in _shifted(ext, t):
            k = d - (HALO - CONV_WIDTH + 1)
            if 0 <= k < CONV_WIDTH:
                acc = acc + win * w_ref[k:k + 1, :]
        z1 = acc + b_ref[...]
        z1_ref[...] = z1
        mu = jnp.mean(z1, axis=-1, keepdims=True)
        zc = z1 - mu
        rs = lax.rsqrt(jnp.mean(zc * zc, axis=-1, keepdims=True) + EPS)
        z2 = zc * rs * g_ref[...] + beta_ref[...]
        z3_ref[...] = (z2 * _sigmoid(z2)).astype(BF16)

    vec = lambda v: pl.BlockSpec(v.shape, lambda i: (0, 0))
    return pl.pallas_call(
        body, name="conv_fwd", grid=(S // t,),
        in_specs=[pl.BlockSpec((t, C), lambda i: (i, 0)),
                  pl.BlockSpec((HALO, C), lambda i: (jnp.maximum(i * per - 1, 0), 0)),
                  vec(conv_w), vec(conv_b), vec(ln_g), vec(ln_b)],
        out_specs=[pl.BlockSpec((t, C), lambda i: (i, 0)), pl.BlockSpec((t, C), lambda i: (i, 0))],
        out_shape=[jax.ShapeDtypeStruct((S, C), F32), jax.ShapeDtypeStruct((S, C), BF16)],
        scratch_shapes=[pltpu.VMEM((t + HALO, C), F32)],
        compiler_params=_params(("parallel",)),
    )(z0, z0, conv_w, conv_b, ln_g, ln_b)


def _conv_bwd_norm(dz3, z1, ln_g, ln_b):
    C = z1.shape[1]

    def body(dz3_ref, z1_ref, g_ref, beta_ref, dz1_ref, dg_ref, dbeta_ref, dbias_ref):
        z1 = z1_ref[...]
        mu = jnp.mean(z1, axis=-1, keepdims=True)
        zc = z1 - mu
        rs = lax.rsqrt(jnp.mean(zc * zc, axis=-1, keepdims=True) + EPS)
        xh = zc * rs
        z2 = xh * g_ref[...] + beta_ref[...]
        sg = _sigmoid(z2)
        dz2 = dz3_ref[...] * (sg * (1.0 + z2 * (1.0 - sg)))
        dxh = dz2 * g_ref[...]
        dz1 = rs * (dxh - jnp.mean(dxh, axis=-1, keepdims=True) - xh * jnp.mean(dxh * xh, axis=-1, keepdims=True))
        dz1_ref[...] = dz1
        _accum(dg_ref, _colsum(dz2 * xh))
        _accum(dbeta_ref, _colsum(dz2))
        _accum(dbias_ref, _colsum(dz1))

    return _rows("conv_bwd_norm", body, [dz3, z1], [ln_g, ln_b], [(C, F32)], [(1, C)] * 3)


def _conv_bwd_taps(dz1, z0, conv_in, conv_w, dproj, window):
    S, C = z0.shape
    t = _pick(S, (512, 256, 128, 64, 32))
    per = t // HALO
    last = S // HALO - 1
    nt = S // t
    assert window[1] == 2 * C and window[0] % window[1] == 0, window

    def body(dcur_ref, dnext_ref, zcur_ref, zprev_ref, cin_ref, w_ref, _, dcin_ref, dw_ref, dext_ref, zext_ref):
        i = pl.program_id(0)
        dcur = dcur_ref[...]
        dext_ref[0:t, :] = dcur
        dext_ref[t:, :] = jnp.where(i < nt - 1, dnext_ref[...], 0.0)
        zext_ref[0:HALO, :] = jnp.where(i > 0, zprev_ref[...], 0.0)
        zext_ref[HALO:, :] = zcur_ref[...]

        @pl.when(i == 0)
        def _():
            dw_ref[...] = jnp.zeros_like(dw_ref)

        dz0 = jnp.zeros((t, C), F32)
        for d, win in _shifted(dext_ref[...], t):
            k = CONV_WIDTH - 1 - d
            if 0 <= k < CONV_WIDTH:
                dz0 = dz0 + win * w_ref[k:k + 1, :]
        for d, win in _shifted(zext_ref[...], t):
            k = d - (HALO - CONV_WIDTH + 1)
            if 0 <= k < CONV_WIDTH:
                dw_ref[k:k + 1, :] += _colsum(dcur * win)
        a = cin_ref[:, 0:C].astype(F32)
        sg = _sigmoid(cin_ref[:, C:2 * C].astype(F32))
        dcin_ref[:, 0:C] = (dz0 * sg).astype(BF16)
        dcin_ref[:, C:2 * C] = (dz0 * a * sg * (1.0 - sg)).astype(BF16)

    return pl.pallas_call(
        body, name="conv_bwd_taps", grid=(nt,),
        in_specs=[pl.BlockSpec((t, C), lambda i: (i, 0)),
                  pl.BlockSpec((HALO, C), lambda i: (jnp.minimum((i + 1) * per, last), 0)),
                  pl.BlockSpec((t, C), lambda i: (i, 0)),
                  pl.BlockSpec((HALO, C), lambda i: (jnp.maximum(i * per - 1, 0), 0)),
                  pl.BlockSpec((t, 2 * C), lambda i: (i, 0)),
                  pl.BlockSpec(conv_w.shape, lambda i: (0, 0)),
                  pl.BlockSpec(memory_space=pl.ANY)],
        out_specs=[pl.BlockSpec((t, 2 * C), lambda i: (i, window[0] // window[1])), pl.BlockSpec((HALO, C), lambda i: (0, 0))],
        out_shape=[jax.ShapeDtypeStruct(dproj.shape, BF16), jax.ShapeDtypeStruct((HALO, C), F32)],
        input_output_aliases={6: 0},
        scratch_shapes=[pltpu.VMEM((t + HALO, C), F32), pltpu.VMEM((t + HALO, C), F32)],
        compiler_params=_params(("arbitrary",)),
    )(dz1, dz1, z0, z0, conv_in, conv_w, dproj)


def _lower_tri(shape, rows_are_queries):
    row = lax.broadcasted_iota(jnp.int32, shape, 0)
    col = lax.broadcasted_iota(jnp.int32, shape, 1)
    return (col <= row) if rows_are_queries else (row <= col)


HEADS_PER_STEP = 2
FWD_HEADS_PER_STEP = 2
FWD_KEY_TILES = 8


def _flash_specs(S, t, heads):
    w = heads * HEAD_PAD
    blk = pl.BlockSpec((t, w), lambda h, i: (i, h))
    head = pl.BlockSpec((S, w), lambda h, i: (0, h))
    return blk, head


def _head_lanes(g):
    return slice(g * HEAD_PAD, (g + 1) * HEAD_PAD)


def _dot_nt(a, b):
    return lax.dot_general(a, b, _DIMS["nt"], preferred_element_type=F32)


def _dot_nn(a, b):
    return lax.dot_general(a, b, _DIMS["nn"], preferred_element_type=F32)


def _dot_tn(a, b):
    return lax.dot_general(a, b, _DIMS["tn"], preferred_element_type=F32)


def _flash_fwd(q, k, v):
    S = q.shape[0]
    t = _pick(S, (512, 256, 128))

    def body(q_ref, k_ref, v_ref, o_ref, qa_ref, m_ref, acc_ref):
        qi = pl.program_id(1)
        m_ref[...] = jnp.full_like(m_ref, NEG)
        acc_ref[...] = jnp.zeros_like(acc_ref)

        def step(first, tiles, diag):
            width = tiles * t
            rows = pl.ds(pl.multiple_of(first, t), width)
            for g in range(FWD_HEADS_PER_STEP):
                hl = _head_lanes(g)
                s = _dot_nt(q_ref[:, hl], k_ref[rows, hl])
                if diag:
                    row = lax.broadcasted_iota(jnp.int32, s.shape, 0)
                    col = lax.broadcasted_iota(jnp.int32, s.shape, 1)
                    s = jnp.where(col <= row + (tiles - 1) * t, s, NEG)
                m_old = m_ref[g]
                m_new = jnp.maximum(m_old, jnp.max(s, axis=-1, keepdims=True))
                p = jnp.exp(s - m_new).astype(BF16)
                acc_ref[g] = jnp.exp(m_old - m_new) * acc_ref[g] + _dot_nn(p, v_ref[rows, hl])
                m_ref[g] = m_new

        def wide(kb, carry):
            step(kb * (FWD_KEY_TILES * t), FWD_KEY_TILES, False)
            return carry

        full_groups = qi // FWD_KEY_TILES
        lax.fori_loop(0, full_groups, wide, 0)
        for tiles in range(1, min(FWD_KEY_TILES, S // t) + 1):
            @pl.when(qi - full_groups * FWD_KEY_TILES == tiles - 1)
            def _():
                step(full_groups * (FWD_KEY_TILES * t), tiles, True)

        for g in range(FWD_HEADS_PER_STEP):
            hl = _head_lanes(g)
            acc = acc_ref[g]
            l = -acc[:, STAT_COL_V:STAT_COL_V + 1]
            o_ref[:, hl] = (acc / l).astype(BF16)
            qa_ref[:, hl] = _put_stats(q_ref[:, hl], m_ref[g] + jnp.log(l), STAT_COL_QK)

    blk, head = _flash_specs(S, t, FWD_HEADS_PER_STEP)
    return pl.pallas_call(
        body, name="mla_flash_fwd", grid=(MLA_HEADS // FWD_HEADS_PER_STEP, S // t),
        in_specs=[blk, head, head], out_specs=[blk, blk],
        out_shape=[jax.ShapeDtypeStruct(q.shape, BF16), jax.ShapeDtypeStruct(q.shape, BF16)],
        scratch_shapes=[pltpu.VMEM((FWD_HEADS_PER_STEP, t, 1), F32), pltpu.VMEM((FWD_HEADS_PER_STEP, t, HEAD_PAD), F32)],
        compiler_params=_params(("parallel", "arbitrary")),
    )(q, k, v)


def _flash_bwd(qa, k, v, doa):
    S = qa.shape[0]
    t = _pick(S, (1024, 512, 256, 128))
    n = S // t
    half = t // 2

    def body(qa_ref, k_ref, v_ref, do_ref, dq_ref, dk_ref, dv_ref, dk_acc, dv_acc):
        kj = pl.program_id(1)

        @pl.when(kj == 0)
        def _():
            dq_ref[...] = jnp.zeros_like(dq_ref)

        dk_acc[...] = jnp.zeros_like(dk_acc)
        dv_acc[...] = jnp.zeros_like(dv_acc)

        def step(q_first, q_len, keys, diag):
            rows = pl.ds(pl.multiple_of(q_first, q_len), q_len)
            for g in range(HEADS_PER_STEP):
                hl = _head_lanes(g)
                qa, do, kk = qa_ref[rows, hl], do_ref[rows, hl], k_ref[keys, hl]
                st = _dot_nt(kk, qa)
                if diag:
                    st = jnp.where(_lower_tri(st.shape, False), st, NEG)
                pt = jnp.exp(st)
                dst = (pt * _dot_nt(v_ref[keys, hl], do)).astype(BF16)
                dv_acc[keys, hl] += _dot_nn(pt.astype(BF16), do)
                dk_acc[keys, hl] += _dot_nn(dst, qa)
                dq_ref[rows, hl] += _dot_tn(dst, kk)

        def loop(qi, carry):
            step(qi * t, t, slice(0, t), False)
            return carry

        lo, hi = slice(0, half), slice(half, t)
        step(kj * t, half, lo, True)
        step(kj * t + half, half, lo, False)
        step(kj * t + half, half, hi, True)
        lax.fori_loop(kj + 1, n, loop, 0)
        dk_ref[...] = dk_acc[...]
        dv_ref[...] = dv_acc[...].astype(BF16)

    blk, head = _flash_specs(S, t, HEADS_PER_STEP)
    w = HEADS_PER_STEP * HEAD_PAD
    return pl.pallas_call(
        body, name="mla_flash_bwd", grid=(MLA_HEADS // HEADS_PER_STEP, n),
        in_specs=[head, blk, blk, head], out_specs=[head, blk, blk],
        out_shape=[jax.ShapeDtypeStruct(qa.shape, F32), jax.ShapeDtypeStruct(qa.shape, F32), jax.ShapeDtypeStruct(qa.shape, BF16)],
        scratch_shapes=[pltpu.VMEM((t, w), F32), pltpu.VMEM((t, w), F32)],
        compiler_params=_params(("parallel", "arbitrary")),
    )(qa, k, v, doa)


def _xattn_fwd(xq, kvx):
    W = X_HEADS * X_HEAD_DIM

    def body(q_ref, kv_ref, o_ref):
        for h in range(X_HEADS):
            lo = h * X_HEAD_DIM
            s = _dot_nt(q_ref[:, lo:lo + X_HEAD_DIM], kv_ref[:, lo:lo + X_HEAD_DIM])
            p = jnp.exp(s - jnp.max(s, axis=-1, keepdims=True))
            p = p / jnp.sum(p, axis=-1, keepdims=True)
            o_ref[:, lo:lo + X_HEAD_DIM] = _dot_nn(p.astype(BF16), kv_ref[:, W + lo:W + lo + X_HEAD_DIM]).astype(BF16)

    return _rows("xattn_fwd", body, [xq], [kvx], [(W, BF16)])[0]


def _xattn_bwd(xq, kvx, dox):
    W = X_HEADS * X_HEAD_DIM
    scale = X_HEAD_DIM ** -0.5

    def body(q_ref, do_ref, kv_ref, dq_ref, dkv_ref):
        @pl.when(pl.program_id(0) == 0)
        def _():
            dkv_ref[...] = jnp.zeros_like(dkv_ref)

        for h in range(X_HEADS):
            lo = h * X_HEAD_DIM
            q, k = q_ref[:, lo:lo + X_HEAD_DIM], kv_ref[:, lo:lo + X_HEAD_DIM]
            v, do = kv_ref[:, W + lo:W + lo + X_HEAD_DIM], do_ref[:, lo:lo + X_HEAD_DIM]
            s = _dot_nt(q, k)
            p = jnp.exp(s - jnp.max(s, axis=-1, keepdims=True))
            p = p / jnp.sum(p, axis=-1, keepdims=True)
            dp = _dot_nt(do, v)
            ds = (p * (dp - jnp.sum(dp * p, axis=-1, keepdims=True))).astype(BF16)
            dq_ref[:, lo:lo + X_HEAD_DIM] = (_dot_nn(ds, k) * scale).astype(BF16)
            dkv_ref[:, lo:lo + X_HEAD_DIM] += _dot_tn(ds, q)
            dkv_ref[:, W + lo:W + lo + X_HEAD_DIM] += _dot_tn(p.astype(BF16), do)

    return _rows("xattn_bwd", body, [xq, dox], [kvx], [(W, BF16)], [kvx.shape])


def _adamw(name, w, g, m, v):
    c1 = 1.0 / (1.0 - ADAM_B1 ** ADAM_STEP)
    c2 = 1.0 / (1.0 - ADAM_B2 ** ADAM_STEP)
    lead = (0,) * (w.ndim - 2)
    w2 = w.reshape((1,) * (2 - w.ndim) + w.shape) if w.ndim < 2 else w
    m2, v2 = m.reshape(w2.shape), v.reshape(w2.shape)
    g2 = g.reshape(w2.shape[-2:])
    R, C = g2.shape
    t = _pick(R, (256, 128, 64, 32, 16, 8))

    def body(w_ref, g_ref, m_ref, v_ref, go_ref, d_ref, nm_ref, nv_ref):
        g = g_ref[...]
        nm = ADAM_B1 * m_ref[lead] + (1.0 - ADAM_B1) * g
        nv = ADAM_B2 * v_ref[lead] + (1.0 - ADAM_B2) * (g * g)
        go_ref[lead] = g
        d_ref[lead] = -ADAM_LR * ((nm * c1) / (jnp.sqrt(nv * c2) + ADAM_EPS) + ADAM_WD * w_ref[lead])
        nm_ref[lead] = nm
        nv_ref[lead] = nv

    full = pl.BlockSpec((1,) * len(lead) + (t, C), lambda i: lead + (i, 0))
    outs = pl.pallas_call(
        body, name=name, grid=(R // t,), in_specs=[full, pl.BlockSpec((t, C), lambda i: (i, 0)), full, full],
        out_specs=[full] * 4, out_shape=[jax.ShapeDtypeStruct(w2.shape, F32)] * 4, compiler_params=_params(("parallel",)),
    )(w2, g2, m2, v2)
    return [o.reshape(w.shape) for o in outs]


def _place():
    x, y, c = lax.axis_index("x"), lax.axis_index("y"), lax.axis_index("c")
    return x, y, c, [(1 - x, y), (x, 1 - y), (1 - x, 1 - y)]


_ANY = pl.BlockSpec(memory_space=pl.ANY)


_HBM = pl.BlockSpec(memory_space=pltpu.HBM)
_SEM = pl.BlockSpec(memory_space=pltpu.SEMAPHORE)
_SIDE_EFFECT = pltpu.SideEffectType.DATAFLOW_SIDE_EFFECTING


def _gather_copy(src, land, slot, send, recv, k, chip, c):
    return pltpu.make_async_remote_copy(src_ref=src, dst_ref=land.at[slot], send_sem=send.at[k], recv_sem=recv.at[k],
                                        device_id=(chip[0], chip[1], c), device_id_type=MESH)


def _gather_start(shards, groups):
    n, ng = len(shards), len(groups)
    lands = [jnp.broadcast_to(s[None], (N_CHIPS,) + s.shape) for s in shards]

    def body(*refs):
        ins, lnd = refs[:n], refs[n:2 * n]
        sends, recvs = refs[2 * n:2 * n + ng], refs[2 * n + ng:2 * n + 2 * ng]
        token = refs[-1]
        x, y, c, chips = _place()
        for gi, group in enumerate(groups):
            for pos, w in enumerate(group):
                for j, chip in enumerate(chips):
                    _gather_copy(ins[w], lnd[w], 2 * x + y, sends[gi], recvs[gi], 3 * pos + j, chip, c).start()
        token[...] = jnp.zeros_like(token)

    sems = [pltpu.SemaphoreType.DMA((3 * len(g),)) for g in groups]
    res = pl.pallas_call(
        body, name="gather_weights_start",
        out_shape=sems + sems + [pltpu.HBM(a.shape, a.dtype) for a in shards + lands] + [jax.ShapeDtypeStruct((8, LANES), F32)],
        in_specs=[_HBM] * (2 * n),
        out_specs=[_SEM] * (2 * ng) + [_HBM] * (2 * n) + [pl.BlockSpec(memory_space=pltpu.VMEM)],
        input_output_aliases={i: 2 * ng + i for i in range(2 * n)},
        compiler_params=pltpu.CompilerParams(has_side_effects=_SIDE_EFFECT),
    )(*[pltpu.with_memory_space_constraint(a, pltpu.HBM) for a in shards + lands])
    sem_pairs = list(zip(res[:ng], res[ng:2 * ng]))
    return sem_pairs, res[2 * ng:2 * ng + n], res[2 * ng + n:2 * ng + 2 * n], res[-1]


def _gather_wait(name, sem_pair, shards_thru, lands_thru, after):
    m = len(shards_thru)

    def body(*refs):
        ins, lnd = refs[:m], refs[m:2 * m]
        send, recv = refs[2 * m], refs[2 * m + 1]
        x, y, c, chips = _place()
        for pos in range(m):
            for j, chip in enumerate(chips):
                cp = _gather_copy(ins[pos], lnd[pos], 2 * chip[0] + chip[1], send, recv, 3 * pos + j, chip, c)
                cp.wait_send()
                cp.wait_recv()

    res = pl.pallas_call(
        body, name=name,
        out_shape=[pltpu.HBM(a.shape, a.dtype) for a in list(shards_thru) + list(lands_thru)],
        in_specs=[_HBM] * (2 * m) + [_SEM, _SEM, _ANY], out_specs=[_HBM] * (2 * m),
        input_output_aliases={i: i for i in range(2 * m)},
        compiler_params=pltpu.CompilerParams(has_side_effects=_SIDE_EFFECT),
    )(*shards_thru, *lands_thru, *sem_pair, after)
    return res[m:]


def _pair_exchange(name, packs):
    n = len(packs)

    def body(*refs):
        ins, outs, send, recv = refs[:n], refs[n:2 * n], refs[2 * n], refs[2 * n + 1]
        x, y, c, _ = _place()
        cps = []
        for g in range(n):
            cp = pltpu.make_async_remote_copy(src_ref=ins[g].at[:, pl.ds(1 - c, 1)], dst_ref=outs[g], send_sem=send.at[g],
                                              recv_sem=recv.at[g], device_id=(x, y, 1 - c), device_id_type=MESH)
            cp.start()
            cps.append(cp)
        for cp in cps:
            cp.wait()

    return pl.pallas_call(
        body, name=name, in_specs=[_ANY] * n, out_specs=[_ANY] * n,
        out_shape=[jax.ShapeDtypeStruct((N_CHIPS, 1) + p.shape[2:], p.dtype) for p in packs],
        scratch_shapes=[pltpu.SemaphoreType.DMA((n,)), pltpu.SemaphoreType.DMA((n,))],
    )(*packs)


def _chip_copy(src, land, src_slot, dst_slot, send, recv, k, chip, c):
    return pltpu.make_async_remote_copy(src_ref=src.at[src_slot], dst_ref=land.at[dst_slot], send_sem=send.at[k],
                                        recv_sem=recv.at[k], device_id=(chip[0], chip[1], c), device_id_type=MESH)


def _chip_exchange_start(name, parts):
    n = len(parts)
    lands = [lax.empty(p.shape, p.dtype) for p in parts]

    def body(*refs):
        ins, lnd, send, recv, token = refs[:n], refs[n:2 * n], refs[2 * n], refs[2 * n + 1], refs[-1]
        x, y, c, chips = _place()
        for g in range(n):
            for j, chip in enumerate(chips):
                _chip_copy(ins[g], lnd[g], 2 * chip[0] + chip[1], 2 * x + y, send, recv, 3 * g + j, chip, c).start()
        token[...] = jnp.zeros_like(token)

    sems = [pltpu.SemaphoreType.DMA((3 * n,))] * 2
    res = pl.pallas_call(
        body, name=name,
        out_shape=sems + [pltpu.HBM(a.shape, a.dtype) for a in list(parts) + lands] + [jax.ShapeDtypeStruct((8, LANES), F32)],
        in_specs=[_HBM] * (2 * n),
        out_specs=[_SEM] * 2 + [_HBM] * (2 * n) + [pl.BlockSpec(memory_space=pltpu.VMEM)],
        input_output_aliases={i: 2 + i for i in range(2 * n)},
        compiler_params=pltpu.CompilerParams(has_side_effects=_SIDE_EFFECT),
    )(*[pltpu.with_memory_space_constraint(a, pltpu.HBM) for a in list(parts) + lands])
    return res[:2], res[2:2 + n], res[2 + n:2 + 2 * n], res[-1]


def _chip_exchange_wait(name, sems, parts_thru, lands_thru, after):
    n = len(parts_thru)

    def body(*refs):
        ins, lnd, send, recv = refs[:n], refs[n:2 * n], refs[2 * n], refs[2 * n + 1]
        x, y, c, chips = _place()
        for g in range(n):
            for j, chip in enumerate(chips):
                cp = _chip_copy(ins[g], lnd[g], 2 * x + y, 2 * chip[0] + chip[1], send, recv, 3 * g + j, chip, c)
                cp.wait_send()
                cp.wait_recv()

    res = pl.pallas_call(
        body, name=name,
        out_shape=[pltpu.HBM(a.shape, a.dtype) for a in list(parts_thru) + list(lands_thru)],
        in_specs=[_HBM] * (2 * n) + [_SEM, _SEM, _ANY], out_specs=[_HBM] * (2 * n),
        input_output_aliases={i: i for i in range(2 * n)},
        compiler_params=pltpu.CompilerParams(has_side_effects=_SIDE_EFFECT),
    )(*parts_thru, *lands_thru, *sems, after)
    return res[:n], res[n:]


def _pair_share(halves):
    n = len(halves)

    def body(*refs):
        outs, send, recv = refs[n:2 * n], refs[2 * n], refs[2 * n + 1]
        x, y, c, _ = _place()
        cps = []
        for g in range(n):
            cp = pltpu.make_async_remote_copy(src_ref=outs[g].at[c], dst_ref=outs[g].at[c], send_sem=send.at[g],
                                              recv_sem=recv.at[g], device_id=(x, y, 1 - c), device_id_type=MESH)
            cp.start()
            cps.append(cp)
        for g in range(n):
            pltpu.make_async_remote_copy(src_ref=outs[g].at[c], dst_ref=outs[g].at[1 - c], send_sem=send.at[g],
                                         recv_sem=recv.at[g], device_id=(x, y, 1 - c), device_id_type=MESH).wait_recv()
        for cp in cps:
            cp.wait_send()

    return pl.pallas_call(
        body, name="grad_pair_share", in_specs=[_ANY] * n, out_specs=[_ANY] * n,
        out_shape=[jax.ShapeDtypeStruct(h.shape, h.dtype) for h in halves],
        input_output_aliases={g: g for g in range(n)},
        scratch_shapes=[pltpu.SemaphoreType.DMA((n,)), pltpu.SemaphoreType.DMA((n,))],
    )(*halves)


def _sum_over_devices(block):
    R, L = block.shape

    def gather(b_ref, o_ref, send, recv, loc):
        x, y, c, _ = _place()
        lc = pltpu.make_async_copy(b_ref, o_ref.at[4 * x + 2 * y + c], loc)
        lc.start()
        peers = [(1 - x if dx else x, 1 - y if dy else y, 1 - c if dc else c)
                 for dx in (0, 1) for dy in (0, 1) for dc in (0, 1) if dx or dy or dc]
        cps = []
        for j, peer in enumerate(peers):
            cp = pltpu.make_async_remote_copy(src_ref=b_ref, dst_ref=o_ref.at[4 * x + 2 * y + c], send_sem=send.at[j],
                                              recv_sem=recv.at[j], device_id=peer, device_id_type=MESH)
            cp.start()
            cps.append(cp)
        for j, (px, py, pc) in enumerate(peers):
            pltpu.make_async_remote_copy(src_ref=b_ref, dst_ref=o_ref.at[4 * px + 2 * py + pc], send_sem=send.at[j],
                                         recv_sem=recv.at[j], device_id=(px, py, pc), device_id_type=MESH).wait_recv()
        for cp in cps:
            cp.wait_send()
        lc.wait()

    blocks = pl.pallas_call(
        gather, name="small_grads_gather", in_specs=[_ANY], out_specs=_ANY,
        out_shape=jax.ShapeDtypeStruct((N_DEV, R, L), F32),
        scratch_shapes=[pltpu.SemaphoreType.DMA((N_DEV - 1,)), pltpu.SemaphoreType.DMA((N_DEV - 1,)), pltpu.SemaphoreType.DMA],
    )(block)

    def add(b_ref, o_ref):
        total = b_ref[0]
        for d in range(1, N_DEV):
            total = total + b_ref[d]
        o_ref[...] = total

    return pl.pallas_call(add, name="small_grads_add", out_shape=jax.ShapeDtypeStruct((R, L), F32))(blocks)


def _pair_add(name, pack, got):
    _, _, R, C = pack.shape
    t = _pick(R, ROW_TILES)
    c = lax.axis_index("c").astype(jnp.int32).reshape(1)

    def body(c_ref, p_ref, g_ref, o_ref):
        o_ref[...] = (p_ref[...].astype(F32) + g_ref[...].astype(F32)).astype(BF16)

    return pl.pallas_call(
        body, name=name,
        grid_spec=pltpu.PrefetchScalarGridSpec(
            num_scalar_prefetch=1, grid=(N_CHIPS, R // t),
            in_specs=[pl.BlockSpec((None, None, t, C), lambda k, i, c_ref: (k, c_ref[0], i, 0)),
                      pl.BlockSpec((None, None, t, C), lambda k, i, c_ref: (k, 0, i, 0))],
            out_specs=pl.BlockSpec((None, t, C), lambda k, i, c_ref: (k, i, 0))),
        out_shape=jax.ShapeDtypeStruct((N_CHIPS, R, C), BF16), compiler_params=_params(("parallel", "parallel")),
    )(c, pack, got)


def _chip_add(name, own, got):
    _, R, C = own.shape
    t = _pick(R, ROW_TILES)
    x, y, c, _ = _place()
    place = jnp.stack([c, 2 * x + y]).astype(jnp.int32)

    def body(place_ref, own_ref, g1_ref, g2_ref, g3_ref, o_ref):
        o_ref[...] = ((own_ref[...].astype(F32) + g1_ref[...].astype(F32)) + g2_ref[...].astype(F32)) + g3_ref[...].astype(F32)

    def other(d):
        return pl.BlockSpec((None, t, C), lambda i, place_ref: ((place_ref[1] + d) % N_CHIPS, i, 0))

    return pl.pallas_call(
        body, name=name,
        grid_spec=pltpu.PrefetchScalarGridSpec(
            num_scalar_prefetch=1, grid=(R // t,),
            in_specs=[pl.BlockSpec((None, t, C), lambda i, place_ref: (place_ref[1], i, 0)), other(1), other(2), other(3)],
            out_specs=pl.BlockSpec((None, t, C), lambda i, place_ref: (place_ref[0], i, 0))),
        out_shape=jax.ShapeDtypeStruct((2, R, C), F32), compiler_params=_params(("parallel",)),
    )(place, own, got, got, got)


_CUT = (2 * CONV_CH, 2 * CONV_CH + Q_LORA, 2 * CONV_CH + Q_LORA + KV_LORA, 2 * CONV_CH + Q_LORA + KV_LORA + QK_ROPE)
W_IN_GATES = (0, 2 * D_MODEL)
W_IN_CONV = (W_IN_GATES[1], 2 * CONV_CH)
W_IN_CQ = (W_IN_CONV[0] + W_IN_CONV[1], Q_LORA)
W_IN_KR = (W_IN_CQ[0] + W_IN_CQ[1], HEAD_PAD)
W_IN_CKV = (W_IN_KR[0] + W_IN_KR[1], KV_LORA)
W_IN_LORA = (W_IN_CQ[0], Q_LORA + HEAD_PAD + KV_LORA)
W_IN_COLS = W_IN_CKV[0] + W_IN_CKV[1]


def _pad_last(a, n):
    return jnp.pad(a, [(0, 0)] * (a.ndim - 1) + [(0, n - a.shape[-1])])


def _layout_w_in(w_in):
    kr = jnp.pad(w_in[:, _CUT[2]:_CUT[3]], ((0, 0), (QK_NOPE, HEAD_PAD - QK_NOPE - QK_ROPE)))
    return jnp.concatenate([w_in[:, _CUT[3]:], w_in[:, :_CUT[0]], w_in[:, _CUT[0]:_CUT[1]], kr, w_in[:, _CUT[1]:_CUT[2]]], axis=1)


def _layout_weights(w):
    out = dict(w)
    if "w_uq" in w:
        out["w_uq"] = _pad_last(w["w_uq"].reshape(Q_LORA, MLA_HEADS, QK_NOPE + QK_ROPE), HEAD_PAD).reshape(Q_LORA, MLA_HEADS * HEAD_PAD)
    if "w_ukv" in w:
        ukv = w["w_ukv"].reshape(KV_LORA, MLA_HEADS, QK_NOPE + V_DIM)
        uk = _pad_last(ukv[:, :, :QK_NOPE], HEAD_PAD).reshape(KV_LORA, MLA_HEADS * HEAD_PAD)
        uv = _pad_last(ukv[:, :, QK_NOPE:], HEAD_PAD).reshape(KV_LORA, MLA_HEADS * HEAD_PAD)
        out["w_ukv"] = jnp.concatenate([uk, uv], axis=1)
    if "w_mla_out" in w:
        mo = jnp.pad(w["w_mla_out"].reshape(MLA_HEADS, V_DIM, D_MODEL), ((0, 0), (0, HEAD_PAD - V_DIM), (0, 0)))
        out["w_mla_out"] = mo.reshape(MLA_HEADS * HEAD_PAD, D_MODEL)
    return out


def _unlayout_grads(g):
    out = dict(g)
    if "w_in" in g:
        gi = g["w_in"]
        win = lambda w: gi[:, w[0]:w[0] + w[1]]
        kr = gi[:, W_IN_KR[0] + QK_NOPE:W_IN_KR[0] + QK_NOPE + QK_ROPE]
        out["w_in"] = jnp.concatenate([win(W_IN_CONV), win(W_IN_CQ), win(W_IN_CKV), kr, win(W_IN_GATES)], axis=1)
    if "w_uq" in g:
        out["w_uq"] = g["w_uq"].reshape(Q_LORA, MLA_HEADS, HEAD_PAD)[:, :, :QK_NOPE + QK_ROPE].reshape(Q_LORA, -1)
    if "w_ukv" in g:
        gk = g["w_ukv"][:, :MLA_HEADS * HEAD_PAD].reshape(KV_LORA, MLA_HEADS, HEAD_PAD)[:, :, :QK_NOPE]
        gv = g["w_ukv"][:, MLA_HEADS * HEAD_PAD:].reshape(KV_LORA, MLA_HEADS, HEAD_PAD)[:, :, :V_DIM]
        out["w_ukv"] = jnp.concatenate([gk, gv], axis=2).reshape(KV_LORA, -1)
    if "w_mla_out" in g:
        out["w_mla_out"] = g["w_mla_out"].reshape(MLA_HEADS, HEAD_PAD, D_MODEL)[:, :V_DIM].reshape(MLA_HEADS * V_DIM, D_MODEL)
    return out


def _rope_tables(positions):
    half = QK_ROPE // 2
    inv_freq = ROPE_THETA ** (-jnp.arange(half, dtype=F32) / half)
    ang = positions.astype(F32).reshape(-1, 1) * inv_freq
    cos, sin = jnp.cos(ang), jnp.sin(ang)
    S = cos.shape[0]
    z16, z32, z64 = jnp.zeros((S, half), F32), jnp.zeros((S, QK_ROPE), F32), jnp.zeros((S, QK_NOPE), F32)
    c = jnp.concatenate([jnp.ones((S, QK_NOPE), F32), cos, cos, z32], axis=1)
    sa = jnp.concatenate([z64, -sin, z16, z32], axis=1)
    sb = jnp.concatenate([z64, z16, sin, z32], axis=1)
    return c, sa, sb


def _local_step(x, mem, positions, target, weight_fns, early_grads_fn, late_grads_fn, sm):
    S = x.shape[0]
    HW = MLA_HEADS * HEAD_PAD
    rope_c, rope_sa, rope_sb = _rope_tables(positions)
    qk_scale = (QK_NOPE + QK_ROPE) ** -0.5

    def k_rms1(x_ref, g_ref, u_ref):
        u_ref[...] = _rms_fwd(x_ref[...], g_ref[...]).astype(BF16)

    u1, = _rows("rms_mix", k_rms1, [x], [sm["norm_mix_g"]], [(D_MODEL, BF16)])
    w_in, conv_w = weight_fns[0](u1)

    def epi_glu(acc, xs, outs):
        a, gt = acc[:, 0:CONV_CH], acc[:, CONV_CH:2 * CONV_CH]
        outs[0][...] = acc[...].astype(BF16)
        outs[1][...] = a * _sigmoid(gt)

    conv_in, z0 = _mm("proj_conv", u1, w_in, "nn", [(2 * CONV_CH, BF16), (CONV_CH, F32)], epi_glu, b_cols=W_IN_CONV)
    c_q = _mm_plain("proj_cq", u1, w_in, "nn", b_cols=W_IN_CQ)
    c_kv = _mm_plain("proj_ckv", u1, w_in, "nn", b_cols=W_IN_CKV)
    kr_raw = _mm_plain("proj_krope", u1, w_in, "nn", b_cols=W_IN_KR)

    def epi_sigmoid(acc, xs, outs):
        outs[0][...] = _sigmoid(acc[...]).astype(BF16)

    gates, = _mm("proj_gates", u1, w_in, "nn", [(2 * D_MODEL, BF16)], epi_sigmoid, b_cols=W_IN_GATES)

    z1, z3 = _conv_fwd(z0, conv_w, sm["conv_b"], sm["conv_ln_g"], sm["conv_ln_b"])
    wl = weight_fns[1](z1)
    conv_out = _mm_plain("conv_out", z3, wl["w_conv_out"], "nn", dtype=BF16)

    def k_lora_norm(cq_ref, ckv_ref, gq_ref, gkv_ref, qn_ref, kvn_ref):
        qn_ref[...] = _rms_fwd(cq_ref[...], gq_ref[...]).astype(BF16)
        kvn_ref[...] = _rms_fwd(ckv_ref[...], gkv_ref[...]).astype(BF16)

    qn, kvn = _rows("lora_norm", k_lora_norm, [c_q, c_kv], [sm["q_norm_g"], sm["kv_norm_g"]],
                    [(Q_LORA, BF16), (KV_LORA, BF16)])

    def epi_q(acc, xs, outs):
        c, sa, sb = xs[0][...], xs[1][...], xs[2][...]
        for h in range(MLA_HEADS):
            lo = h * HEAD_PAD
            outs[0][:, lo:lo + HEAD_PAD] = (_rope(acc[:, lo:lo + HEAD_PAD], c, sa, sb, 1.0) * qk_scale).astype(BF16)

    q_att, = _mm("q_up", qn, wl["w_uq"], "nn", [(HW, BF16)], epi_q, row_x=[rope_c, rope_sa, rope_sb], tn=HW)

    def epi_kv(acc, xs, outs):
        kr = _rope(xs[0][...], xs[1][...], xs[2][...], xs[3][...], 1.0)
        kr = kr + _neg_ones(kr.shape, STAT_COL_QK)
        vconst = _neg_ones(kr.shape, STAT_COL_V)
        for h in range(MLA_HEADS):
            lo = h * HEAD_PAD
            outs[0][:, lo:lo + HEAD_PAD] = (acc[:, lo:lo + HEAD_PAD] + kr).astype(BF16)
            outs[1][:, lo:lo + HEAD_PAD] = (acc[:, HW + lo:HW + lo + HEAD_PAD] + vconst).astype(BF16)

    k_att, v_att = _mm("kv_up", kvn, wl["w_ukv"], "nn", [(HW, BF16), (HW, BF16)], epi_kv,
                       row_x=[kr_raw, rope_c, rope_sa, rope_sb], tn=2 * HW)

    o_att, q_aug = _flash_fwd(q_att, k_att, v_att)
    wl.update(weight_fns[2](o_att))

    def epi_merge(acc, xs, outs):
        for rows in _row_chunks(acc.shape[0]):
            mo = acc[rows, :]
            g0, g1 = xs[0][rows, 0:D_MODEL].astype(F32), xs[0][rows, D_MODEL:].astype(F32)
            outs[0][rows, :] = mo.astype(BF16)
            outs[1][rows, :] = (g0 * xs[1][rows, :].astype(F32) + g1 * mo).astype(BF16)

    mla_out, merged = _mm("mla_out_merge", o_att, wl["w_mla_out"], "nn", [(D_MODEL, BF16), (D_MODEL, BF16)], epi_merge,
                          row_x=[gates, conv_out], tn=D_MODEL)

    def epi_res_norm(acc, xs, outs):
        h = xs[0][...] + acc[...]
        outs[0][...] = h
        outs[1][...] = _rms_fwd(h, xs[1][...]).astype(BF16)

    h1, u2 = _mm("mix_out", merged, wl["w_out"], "nn", [(D_MODEL, F32), (D_MODEL, BF16)], epi_res_norm,
                 row_x=[x], vec_x=[sm["norm_xattn_g"]], tn=D_MODEL)

    xscale = X_HEAD_DIM ** -0.5

    def epi_scale(acc, xs, outs):
        outs[0][...] = (acc[...] * xscale).astype(BF16)

    xq, = _mm("xattn_q", u2, wl["w_xq"], "nn", [(X_HEADS * X_HEAD_DIM, BF16)], epi_scale)

    def k_mem_norm(m_ref, g_ref, o_ref):
        o_ref[...] = _rms_fwd(m_ref[...], g_ref[...]).astype(BF16)

    mem_n, = _rows("mem_norm", k_mem_norm, [mem], [sm["norm_mem_g"]], [(D_MODEL, BF16)])
    kvx = _mm_plain("xattn_kv", mem_n, wl["w_xkv"], "nn", dtype=BF16)
    ox = _xattn_fwd(xq, kvx)
    h2, u3 = _mm("xattn_out", ox, wl["w_xo"], "nn", [(D_MODEL, F32), (D_MODEL, BF16)], epi_res_norm,
                 row_x=[h1], vec_x=[sm["norm_mlp_g"]], tn=D_MODEL)

    def epi_relu2(acc, xs, outs):
        r = jnp.maximum(acc[...], 0.0)
        outs[0][...] = (r * r).astype(BF16)

    hid, = _mm("mlp_up", u3, wl["w_mlp1"], "nn", [(D_FF, BF16)], epi_relu2)

    def epi_final(acc, xs, outs):
        g = xs[2][...]
        for rows in _row_chunks(acc.shape[0]):
            h = xs[0][rows, :] + acc[rows, :]
            e = _rms_fwd(h, g) - xs[1][rows, :]
            part = 0.5 * jnp.sum(jnp.mean(e * e, axis=-1, keepdims=True), axis=0, keepdims=True)
            dh, dg = _rms_bwd(h, g, e * (1.0 / D_MODEL))
            outs[0][rows, :] = dh
            outs[1][rows, :] = dh.astype(BF16)
            _accum(outs[2], jnp.broadcast_to(part, outs[2].shape), first=rows.start == 0)
            _accum(outs[3], dg, first=rows.start == 0)

    dh3, dh3b, loss, g_final = _mm("mlp_down_loss", hid, wl["w_mlp2"], "nn", [(D_MODEL, F32), (D_MODEL, BF16)], epi_final,
                                   row_x=[h2, target], vec_x=[sm["final_norm_g"]], sums=[(1, LANES), (1, D_MODEL)],
                                   tn=D_MODEL, tk=1024)

    def epi_drelu2(acc, xs, outs):
        outs[0][...] = (acc[...] * (2.0 * jnp.sqrt(xs[0][...].astype(F32)))).astype(BF16)

    da1, = _mm("mlp_down_dx", dh3b, wl["w_mlp2"], "nt", [(D_FF, BF16)], epi_drelu2, tile_x=[hid])
    g_mlp2 = _mm_plain("mlp_down_dw", hid, dh3b, "tn")
    g_mlp1 = _mm_plain("mlp_up_dw", u3, da1, "tn")

    def epi_norm_bwd(acc, xs, outs):
        for rows in _row_chunks(acc.shape[0]):
            dx, dg = _rms_bwd(xs[0][rows, :], xs[2][...], acc[rows, :])
            dh = xs[1][rows, :] + dx
            outs[0][rows, :] = dh
            if len(outs) == 3:
                outs[1][rows, :] = dh.astype(BF16)
            _accum(outs[-1], dg, first=rows.start == 0)

    def dx_norm_bwd(name, dy, w, xin, dres, vecs, with_bf16=True):
        outs = [(D_MODEL, F32), (D_MODEL, BF16)] if with_bf16 else [(D_MODEL, F32)]
        return _mm(name, dy, w, "nt", outs, epi_norm_bwd, row_x=[xin, dres], vec_x=vecs, sums=[(1, D_MODEL)],
                   tn=D_MODEL, tk=_pick(dy.shape[1], (1024, 768, 512)))

    dh2, dh2b, g_norm_mlp = dx_norm_bwd("mlp_up_dx_norm", da1, wl["w_mlp1"], h2, dh3, [sm["norm_mlp_g"]])

    dox = _mm_plain("xattn_out_dx", dh2b, wl["w_xo"], "nt", dtype=BF16)
    g_xo = _mm_plain("xattn_out_dw", ox, dh2b, "tn")
    dxq, dkvx = _xattn_bwd(xq, kvx, dox)
    g_xq = _mm_plain("xattn_q_dw", u2, dxq, "tn")
    g_xkv = _mm_plain("xattn_kv_dw", mem_n, dkvx, "tn")
    dmem_n = _mm_plain("xattn_kv_dx", dkvx, wl["w_xkv"], "nt")

    def k_mem_bwd(m_ref, d_ref, g_ref, dg_ref):
        _, dg = _rms_bwd(m_ref[...], g_ref[...], d_ref[...])
        _accum(dg_ref, dg)

    g_norm_mem, = _rows("mem_norm_bwd", k_mem_bwd, [mem, dmem_n], [sm["norm_mem_g"]], [], [(1, D_MODEL)])
    token = early_grads_fn(dict(w_mlp1=g_mlp1, w_mlp2=g_mlp2, w_xo=g_xo, w_xq=g_xq, w_xkv=g_xkv))
    dh1, dh1b, g_norm_xattn = dx_norm_bwd("xattn_q_dx_norm", dxq, wl["w_xq"], h1, dh2, [sm["norm_xattn_g"], token])

    dmerged = _mm_plain("mix_out_dx", dh1b, wl["w_out"], "nt")
    g_out = _mm_plain("mix_out_dw", merged, dh1b, "tn")

    def k_merge_bwd(dm_ref, g_ref, co_ref, mo_ref, dco_ref, dmo_ref, dgl_ref):
        dm = dm_ref[...]
        g0, g1 = g_ref[:, 0:D_MODEL].astype(F32), g_ref[:, D_MODEL:].astype(F32)
        dco_ref[...] = (dm * g0).astype(BF16)
        dmo_ref[...] = (dm * g1).astype(BF16)
        dgl_ref[:, 0:D_MODEL] = (dm * co_ref[...].astype(F32) * g0 * (1.0 - g0)).astype(BF16)
        dgl_ref[:, D_MODEL:] = (dm * mo_ref[...].astype(F32) * g1 * (1.0 - g1)).astype(BF16)

    dconv_out, dmla_out, dproj = _rows("merge_bwd", k_merge_bwd, [dmerged, gates, conv_out, mla_out], [],
                                       [(D_MODEL, BF16), (D_MODEL, BF16), (2 * D_MODEL, BF16)], tile=256,
                                       window=(2, W_IN_GATES, W_IN_COLS, None))

    def epi_do(acc, xs, outs):
        for h in range(MLA_HEADS):
            lo = h * HEAD_PAD
            do = acc[:, lo:lo + HEAD_PAD]
            delta = jnp.sum(do * xs[0][:, lo:lo + HEAD_PAD].astype(F32), axis=-1, keepdims=True)
            outs[0][:, lo:lo + HEAD_PAD] = _put_stats(do.astype(BF16), delta, STAT_COL_V)

    do_aug, = _mm("mla_out_dx", dmla_out, wl["w_mla_out"], "nt", [(HW, BF16)], epi_do, row_x=[o_att], tn=HW)
    g_mla_out = _mm_plain("mla_out_dw", o_att, dmla_out, "tn")
    dq_att, dk_att, dv_att = _flash_bwd(q_aug, k_att, v_att, do_aug)

    def k_rope_bwd(dq_ref, dk_ref, dv_ref, c_ref, sa_ref, sb_ref, dqr_ref, dkv_ref, dkr_ref):
        c, sa, sb = c_ref[...], sa_ref[...], sb_ref[...]
        lane = lax.broadcasted_iota(jnp.int32, c.shape, 1)
        nope = (lane < QK_NOPE).astype(F32)
        ropem = ((lane >= QK_NOPE) & (lane < QK_NOPE + QK_ROPE)).astype(F32)
        dkr = jnp.zeros(c.shape, F32)
        for h in range(MLA_HEADS):
            lo = h * HEAD_PAD
            dqr_ref[:, lo:lo + HEAD_PAD] = (_rope(dq_ref[:, lo:lo + HEAD_PAD], c, sa, sb, -1.0) * qk_scale).astype(BF16)
            dk = dk_ref[:, lo:lo + HEAD_PAD]
            dkv_ref[:, lo:lo + HEAD_PAD] = (dk * nope).astype(BF16)
            dkr = dkr + dk
        dkv_ref[:, HW:] = dv_ref[...]
        dkr_ref[...] = (_rope(dkr * ropem, c, sa, sb, -1.0) * ropem).astype(BF16)

    dq_raw, dkv_cat, dkr = _rows("rope_bwd", k_rope_bwd, [dq_att, dk_att, dv_att, rope_c, rope_sa, rope_sb], [],
                                 [(HW, BF16), (2 * HW, BF16), (HEAD_PAD, BF16)], tile=256)
    g_uq = _mm_plain("q_up_dw", qn, dq_raw, "tn")
    dqn = _mm_plain("q_up_dx", dq_raw, wl["w_uq"], "nt")
    g_ukv = _mm_plain("kv_up_dw", kvn, dkv_cat, "tn")
    dkvn = _mm_plain("kv_up_dx", dkv_cat, wl["w_ukv"], "nt")

    def k_lora_bwd(cq_ref, ckv_ref, dqn_ref, dkvn_ref, dkr_ref, gq_ref, gkv_ref, out_ref, dgq_ref, dgkv_ref):
        dcq, dgq = _rms_bwd(cq_ref[...], gq_ref[...], dqn_ref[...])
        dckv, dgkv = _rms_bwd(ckv_ref[...], gkv_ref[...], dkvn_ref[...])
        out_ref[:, 0:Q_LORA] = dcq.astype(BF16)
        out_ref[:, Q_LORA:Q_LORA + HEAD_PAD] = dkr_ref[...]
        out_ref[:, Q_LORA + HEAD_PAD:] = dckv.astype(BF16)
        _accum(dgq_ref, dgq)
        _accum(dgkv_ref, dgkv)

    dproj, g_q_norm, g_kv_norm = _rows("lora_norm_bwd", k_lora_bwd, [c_q, c_kv, dqn, dkvn, dkr],
                                       [sm["q_norm_g"], sm["kv_norm_g"]], [(W_IN_LORA[1], BF16)],
                                       [(1, Q_LORA), (1, KV_LORA)], window=(0, W_IN_LORA, W_IN_COLS, dproj))

    dz3 = _mm_plain("conv_out_dx", dconv_out, wl["w_conv_out"], "nt")
    g_conv_out = _mm_plain("conv_out_dw", z3, dconv_out, "tn")
    dz1, g_ln_g, g_ln_b, g_conv_b = _conv_bwd_norm(dz3, z1, sm["conv_ln_g"], sm["conv_ln_b"])
    dproj, g_conv_w = _conv_bwd_taps(dz1, z0, conv_in, conv_w, dproj, W_IN_CONV)

    g_in = _mm_plain("proj_dw", u1, dproj, "tn")
    token = late_grads_fn(dict(w_in=g_in, conv_w=g_conv_w[:CONV_WIDTH], w_conv_out=g_conv_out, w_uq=g_uq, w_ukv=g_ukv,
                               w_mla_out=g_mla_out, w_out=g_out))
    grad_x, g_norm_mix = dx_norm_bwd("proj_dx_norm", dproj, w_in, x, dh1, [sm["norm_mix_g"], token], with_bf16=False)

    small = dict(norm_mix_g=g_norm_mix, conv_b=g_conv_b, conv_ln_g=g_ln_g, conv_ln_b=g_ln_b, q_norm_g=g_q_norm,
                 kv_norm_g=g_kv_norm, norm_xattn_g=g_norm_xattn, norm_mem_g=g_norm_mem, norm_mlp_g=g_norm_mlp,
                 final_norm_g=g_final)
    return loss, grad_x, small


def _shard(a, k, axis):
    n = a.shape[axis] // N_CHIPS
    return lax.slice_in_dim(a, k * n, (k + 1) * n, axis=axis)


def _pack_small(grads, loss):
    flat = jnp.concatenate([grads[n].reshape(-1) for n in SMALL] + [loss.reshape(-1)[:1]])
    rows = -(-flat.shape[0] // (8 * LANES)) * 8
    return jnp.pad(flat, (0, rows * LANES - flat.shape[0])).reshape(rows, LANES)


def _pack_groups(shapes, names):
    groups = {}
    for n in names:
        groups.setdefault(shapes[n][1], []).append(n)
    return groups


def _pad_rows(a, mult):
    return jnp.pad(a, ((0, -a.shape[0] % mult), (0, 0)))


def _pack_grads(grads, shapes, names):
    packs = []
    for width, group in _pack_groups(shapes, names).items():
        per_chip = [jnp.concatenate([_pad_rows(_shard(grads[n], k, SHARD_AXIS[n]).astype(BF16), PACK_ROW_ALIGN) for n in group])
                    for k in range(N_CHIPS)]
        rows = per_chip[0].shape[0]
        packs.append(jnp.stack(per_chip).reshape(N_CHIPS, 2, rows // 2, width))
    return packs


def _unpack_grads(fulls, shapes, names):
    out = {}
    for full, group in zip(fulls, _pack_groups(shapes, names).values()):
        flat, at = full.reshape(-1, full.shape[-1]), 0
        for n in group:
            rows = shapes[n][0]
            out[n] = flat[at:at + rows]
            at += rows + (-rows % PACK_ROW_ALIGN)
    return out


def _unpack(flat, names, shapes):
    out, at = {}, 0
    for n in names:
        size = math.prod(shapes[n])
        out[n] = flat[at:at + size].reshape(shapes[n])
        at += size
    return out, at


def kernel(x, mem, positions, norm_mix_g, w_in, conv_w, conv_b, conv_ln_g, conv_ln_b, w_conv_out, q_norm_g, w_uq, kv_norm_g, w_ukv, w_mla_out, w_out, norm_xattn_g, norm_mem_g, w_xq, w_xkv, w_xo, norm_mlp_g, w_mlp1, w_mlp2, final_norm_g, loss_target, m_norm_mix_g, m_w_in, m_conv_w, m_conv_b, m_conv_ln_g, m_conv_ln_b, m_w_conv_out, m_q_norm_g, m_w_uq, m_kv_norm_g, m_w_ukv, m_w_mla_out, m_w_out, m_norm_xattn_g, m_norm_mem_g, m_w_xq, m_w_xkv, m_w_xo, m_norm_mlp_g, m_w_mlp1, m_w_mlp2, m_final_norm_g, v_norm_mix_g, v_w_in, v_conv_w, v_conv_b, v_conv_ln_g, v_conv_ln_b, v_w_conv_out, v_q_norm_g, v_w_uq, v_kv_norm_g, v_w_ukv, v_w_mla_out, v_w_out, v_norm_xattn_g, v_norm_mem_g, v_w_xq, v_w_xkv, v_w_xo, v_norm_mlp_g, v_w_mlp1, v_w_mlp2, v_final_norm_g):
    args = dict(locals())
    w = {n: args[n] for n in WEIGHTS}
    m = {n: args["m_" + n] for n in WEIGHTS}
    v = {n: args["v_" + n] for n in WEIGHTS}

    shards = [w[n][0].astype(F32 if n == "conv_w" else BF16) for n in BIG]
    bounds = (0,) + WEIGHT_WAITS + (len(BIG),)
    spans = [slice(lo, hi) for lo, hi in zip(bounds[:-1], bounds[1:])]
    sem_pairs, shards_thru, lands_thru, token = _gather_start(shards, [list(range(len(BIG)))[sp] for sp in spans])

    def unshard(n, g):
        ax = SHARD_AXIS[n]
        return jnp.moveaxis(g, 0, ax).reshape(g.shape[1:1 + ax] + (N_CHIPS * g.shape[1 + ax],) + g.shape[2 + ax:])

    def wait_fn(i):
        def fn(after):
            lands = _gather_wait(f"gather_weights_wait_{i}", sem_pairs[i], shards_thru[spans[i]], lands_thru[spans[i]], after)
            full = {n: unshard(n, g) for n, g in zip(BIG[spans[i]], lands)}
            return (_layout_w_in(full["w_in"]), full["conv_w"]) if i == 0 else _layout_weights(full)
        return fn

    sm = {n: w[n].reshape(1, -1) for n in SMALL}
    sm["norm_mix_g"] = sm["norm_mix_g"] + token[0, 0]

    shapes = {n: w[n].shape[1:] if n in BIG else w[n].shape for n in WEIGHTS}
    late_names = [n for n in BIG if n not in EARLY_GRADS]
    inflight = {}

    def send_grads(tag, names):
        def fn(g):
            packs = _pack_grads(_unlayout_grads(g), shapes, names)
            got = _pair_exchange(f"grad_pair_exchange_{tag}", packs)
            pairs = [_pair_add(f"grad_pair_add_{tag}_{i}", p, r) for i, (p, r) in enumerate(zip(packs, got))]
            *inflight[tag], token = _chip_exchange_start(f"grad_chip_exchange_{tag}_start", pairs)
            return token
        return fn

    loss, grad_x, g_small = _local_step(x[0], mem[0], positions, loss_target[0], [wait_fn(i) for i in range(3)],
                                        send_grads("early", EARLY_GRADS), send_grads("late", late_names), sm)

    halves, counts = [], {}
    for tag in ("late", "early"):
        own, got = _chip_exchange_wait(f"grad_chip_exchange_{tag}_wait", *inflight[tag], grad_x)
        halves += [_chip_add(f"grad_chip_add_{tag}_{i}", p, g) for i, (p, g) in enumerate(zip(own, got))]
        counts[tag] = len(own)
    fulls = _pair_share(halves)
    g_sum = _unpack_grads(fulls[:counts["late"]], shapes, late_names)
    g_sum.update(_unpack_grads(fulls[counts["late"]:], shapes, EARLY_GRADS))
    small_flat = _sum_over_devices(_pack_small(g_small, loss)).reshape(-1)
    g_small, at = _unpack(small_flat, SMALL, shapes)
    g_sum.update(g_small)
    loss_sum = small_flat[at]

    out_g, out_d, out_m, out_v = [], [], [], []
    for n in WEIGHTS:
        g, d, nm, nv = _adamw("adamw_" + n, w[n], g_sum[n], m[n], v[n])
        out_g.append(g)
        out_d.append(d)
        out_m.append(nm)
        out_v.append(nv)
    return (loss_sum, grad_x[None], *out_g, *out_d, *out_m, *out_v)
```

```python
import math

import jax
import jax.numpy as jnp
from jax import lax
from jax.experimental import pallas as pl
from jax.experimental.pallas import tpu as pltpu

F32 = jnp.float32
BF16 = jnp.bfloat16
MESH = pl.DeviceIdType.MESH

D_MODEL = 1024
CONV_CH = 512
CONV_WIDTH = 31
MLA_HEADS = 8
QK_NOPE = 64
QK_ROPE = 32
V_DIM = 64
Q_LORA = 384
KV_LORA = 256
MEM_LEN = 256
X_HEADS = 4
X_HEAD_DIM = 128
D_FF = 4096
ROPE_THETA = 10000.0
EPS = 1e-6
HEAD_PAD = 128
STAT_COL_QK = QK_NOPE + QK_ROPE
STAT_COL_V = V_DIM
HALO = 32
N_CHIPS = 4
LANES = 128

ADAM_LR = 0.001
ADAM_B1 = 0.9
ADAM_B2 = 0.999
ADAM_EPS = 1e-08
ADAM_WD = 0.01
ADAM_STEP = 10

VMEM_LIMIT = 52 * 1024 * 1024
ROW_TILES = (1024, 512, 256, 128, 64, 32, 16)
PACK_ROW_ALIGN = 32
N_DEV = 8
NEG = -1e30

BIG = ["w_in", "conv_w", "w_conv_out", "w_uq", "w_ukv", "w_mla_out", "w_out", "w_xq", "w_xkv", "w_xo", "w_mlp1", "w_mlp2"]
WEIGHT_WAITS = (2, 5)
SHARD_AXIS = {"w_in": 1, "w_conv_out": 1, "w_uq": 1, "w_ukv": 1, "w_mla_out": 1, "w_out": 0, "w_xq": 0, "w_xkv": 0,
              "w_xo": 1, "w_mlp1": 1, "w_mlp2": 0, "conv_w": 1}
EARLY_GRADS = ["w_mlp1", "w_mlp2", "w_xkv", "w_xq", "w_xo"]
SMALL = ["norm_mix_g", "conv_b", "conv_ln_g", "conv_ln_b", "q_norm_g", "kv_norm_g", "norm_xattn_g", "norm_mem_g",
         "norm_mlp_g", "final_norm_g"]
WEIGHTS = ["norm_mix_g", "w_in", "conv_w", "conv_b", "conv_ln_g", "conv_ln_b", "w_conv_out", "q_norm_g", "w_uq",
           "kv_norm_g", "w_ukv", "w_mla_out", "w_out", "norm_xattn_g", "norm_mem_g", "w_xq", "w_xkv", "w_xo",
           "norm_mlp_g", "w_mlp1", "w_mlp2", "final_norm_g"]


def _pick(n, prefs):
    for p in prefs:
        if n % p == 0:
            return p
    return n


def _params(sem):
    return pltpu.CompilerParams(dimension_semantics=sem, vmem_limit_bytes=VMEM_LIMIT)


_DIMS = {"nn": (((1,), (0,)), ((), ())), "nt": (((1,), (1,)), ((), ())), "tn": (((0,), (0,)), ((), ()))}


def _mm(name, a, b, mode, outs, epi, row_x=(), tile_x=(), vec_x=(), sums=(), tm=None, tn=None, tk=None, b_cols=None):
    if mode == "nn":
        (M, K), (_, N) = a.shape, b.shape
        if b_cols is not None:
            N = b_cols[1]
    elif mode == "nt":
        (M, K), (N, _) = a.shape, b.shape
    else:
        (K, M), (_, N) = a.shape, b.shape
    tm = tm or _pick(M, (1024, 512, 384, 256, 128))
    tn = tn or _pick(N, (1024, 768, 512, 384, 256, 128))
    tk = tk or _pick(K, (2048, 1920, 1024, 768, 512, 384, 256, 128))
    nk = K // tk
    rows_inner = nk == 1 and N // tn > 1
    grid = (N // tn, M // tm, nk) if rows_inner else (M // tm, N // tn, nk)

    def spec(shape, f):
        return pl.BlockSpec(shape, (lambda j, i, k: f(i, j, k)) if rows_inner else f)

    b_off = 0
    if b_cols is not None:
        assert mode == "nn" and b_cols[0] % tn == 0, (name, b_cols, tn)
        b_off = b_cols[0] // tn
    a_spec = spec((tk, tm), lambda i, j, k: (k, i)) if mode == "tn" else spec((tm, tk), lambda i, j, k: (i, k))
    b_spec = spec((tn, tk), lambda i, j, k: (j, k)) if mode == "nt" else spec((tk, tn), lambda i, j, k: (k, j + b_off))
    in_specs = [a_spec, b_spec]
    in_specs += [spec((tm, r.shape[1]), lambda i, j, k: (i, 0)) for r in row_x]
    in_specs += [spec((tm, tn), lambda i, j, k: (i, j)) for _ in tile_x]
    in_specs += [spec(v.shape, lambda i, j, k: (0, 0)) for v in vec_x]
    out_specs, out_shape = [], []
    for w, dt in outs:
        if tn == N:
            out_specs.append(spec((tm, w), lambda i, j, k: (i, 0)))
        else:
            assert w == N, (name, w, N)
            out_specs.append(spec((tm, tn), lambda i, j, k: (i, j)))
        out_shape.append(jax.ShapeDtypeStruct((M, w), dt))
    for shp in sums:
        assert tn == N and not rows_inner, name
        out_specs.append(spec(shp, lambda i, j, k: (0, 0)))
        out_shape.append(jax.ShapeDtypeStruct(shp, F32))
    nx = len(row_x) + len(tile_x) + len(vec_x)
    dims = _DIMS[mode]

    def body(a_ref, b_ref, *rest):
        x_refs, out_refs, acc_ref = rest[:nx], rest[nx:nx + len(outs) + len(sums)], rest[-1]
        av, bv = a_ref[...], b_ref[...]
        if av.dtype != BF16:
            av = av.astype(BF16)
        if bv.dtype != BF16:
            bv = bv.astype(BF16)
        prod = lax.dot_general(av, bv, dims, preferred_element_type=F32)
        if nk == 1:
            acc_ref[...] = prod
            epi(acc_ref, x_refs, out_refs)
        else:
            k = pl.program_id(2)

            @pl.when(k == 0)
            def _():
                acc_ref[...] = prod

            @pl.when(k > 0)
            def _():
                acc_ref[...] += prod

            @pl.when(k == nk - 1)
            def _():
                epi(acc_ref, x_refs, out_refs)

    res = pl.pallas_call(
        body, name=name, grid=grid, in_specs=in_specs, out_specs=out_specs, out_shape=out_shape,
        scratch_shapes=[pltpu.VMEM((tm, tn), F32)],
        compiler_params=_params(("arbitrary",) * 3 if sums else ("parallel", "parallel", "arbitrary")),
    )(a, b, *row_x, *tile_x, *vec_x)
    return res


def _epi_store(acc_ref, x_refs, out_refs):
    for o in out_refs:
        o[...] = acc_ref[...].astype(o.dtype)


def _mm_plain(name, a, b, mode, dtype=F32, **kw):
    n = kw["b_cols"][1] if kw.get("b_cols") else (b.shape[0] if mode == "nt" else b.shape[1])
    return _mm(name, a, b, mode, [(n, dtype)], _epi_store, **kw)[0]


def _rows(name, body, row_ins, vec_ins, row_outs, acc_outs=(), tile=512, window=None):
    S = row_ins[0].shape[0]
    t = _pick(S, (tile, 256, 128, 64, 32, 16, 8))
    in_specs = [pl.BlockSpec((t, r.shape[1]), lambda i: (i, 0)) for r in row_ins]
    in_specs += [pl.BlockSpec(v.shape, lambda i: (0, 0)) for v in vec_ins]
    out_specs = [pl.BlockSpec((t, w), lambda i: (i, 0)) for w, _ in row_outs]
    out_specs += [pl.BlockSpec(shp, lambda i: (0, 0)) for shp in acc_outs]
    out_shape = [jax.ShapeDtypeStruct((S, w), dt) for w, dt in row_outs]
    out_shape += [jax.ShapeDtypeStruct(shp, F32) for shp in acc_outs]
    extra, aliases, n_in = [], {}, len(row_ins) + len(vec_ins)
    if window is not None:
        k, (first, width), total, buffer = window
        assert first % width == 0 and row_outs[k][0] == width, (name, window[:3])
        out_specs[k] = pl.BlockSpec((t, width), lambda i: (i, first // width))
        out_shape[k] = jax.ShapeDtypeStruct((S, total), row_outs[k][1])
        if buffer is not None:
            extra, aliases = [buffer], {n_in: k}
            in_specs.append(pl.BlockSpec(memory_space=pl.ANY))

    def call(*refs):
        body(*refs[:n_in], *refs[n_in + len(extra):])

    sem = ("arbitrary",) if acc_outs else ("parallel",)
    return pl.pallas_call(
        call, name=name, grid=(S // t,), in_specs=in_specs, out_specs=out_specs,
        out_shape=out_shape, input_output_aliases=aliases, compiler_params=_params(sem),
    )(*row_ins, *vec_ins, *extra)


def _accum(ref, val, first=True):
    if first:
        @pl.when(pl.program_id(0) == 0)
        def _():
            ref[...] = jnp.zeros_like(ref)

    ref[...] += val


EPILOGUE_ROWS = 256


def _row_chunks(n):
    step = min(EPILOGUE_ROWS, n)
    return [slice(r, r + step) for r in range(0, n, step)]


def _colsum(v):
    return jnp.sum(v, axis=0, keepdims=True)


def _rms_fwd(x, g):
    r = lax.rsqrt(jnp.mean(x * x, axis=-1, keepdims=True) + EPS)
    return x * r * g


def _rms_bwd(x, g, du):
    r = lax.rsqrt(jnp.mean(x * x, axis=-1, keepdims=True) + EPS)
    xn = x * r
    gdu = du * g
    dx = r * (gdu - xn * jnp.mean(xn * gdu, axis=-1, keepdims=True))
    return dx, _colsum(du * xn)


def _sigmoid(v):
    return 1.0 / (1.0 + jnp.exp(-v))


def _rope(v, c, sa, sb, sign):
    return v * c + sign * (pltpu.roll(v, HEAD_PAD - QK_ROPE // 2, 1) * sa + pltpu.roll(v, QK_ROPE // 2, 1) * sb)


def _split3(v):
    hi = v.astype(BF16)
    r1 = v - hi.astype(F32)
    mid = r1.astype(BF16)
    lo = (r1 - mid.astype(F32)).astype(BF16)
    return hi, mid, lo


def _put_stats(base, stat, col):
    hi, mid, lo = _split3(stat)
    lane = lax.broadcasted_iota(jnp.int32, base.shape, 1)
    out = jnp.where(lane == col, hi, base)
    out = jnp.where(lane == col + 1, mid, out)
    return jnp.where(lane == col + 2, lo, out)


def _neg_ones(shape, col):
    lane = lax.broadcasted_iota(jnp.int32, shape, 1)
    return jnp.where((lane >= col) & (lane < col + 3), -1.0, 0.0).astype(F32)


def _shifted(ext, t):
    p = ext.shape[0]
    for b in range(8):
        rb = ext if b == 0 else pltpu.roll(ext, p - b, 0)
        for a in range(HALO // 8 + 1):
            if 8 * a + b <= HALO:
                yield 8 * a + b, rb[8 * a:8 * a + t]


def _conv_fwd(z0, conv_w, conv_b, ln_g, ln_b):
    S, C = z0.shape
    t = _pick(S, (512, 256, 128, 64, 32))
    per = t // HALO

    def body(cur_ref, prev_ref, w_ref, b_ref, g_ref, beta_ref, z1_ref, z3_ref, ext_ref):
        i = pl.program_id(0)
        ext_ref[0:HALO, :] = jnp.where(i > 0, prev_ref[...], 0.0)
        ext_ref[HALO:, :] = cur_ref[...]
        ext = ext_ref[...]
        acc = jnp.zeros((t, C), F32)
        for d, win in _shifted(ext, t):
            k = d - (HALO - CONV_WIDTH + 1)
            if 0 <= k < CONV_WIDTH:
                acc = acc + win * w_ref[k:k + 1, :]
        z1 = acc + b_ref[...]
        z1_ref[...] = z1
        mu = jnp.mean(z1, axis=-1, keepdims=True)
        zc = z1 - mu
        rs = lax.rsqrt(jnp.mean(zc * zc, axis=-1, keepdims=True) + EPS)
        z2 = zc * rs * g_ref[...] + beta_ref[...]
        z3_ref[...] = (z2 * _sigmoid(z2)).astype(BF16)

    vec = lambda v: pl.BlockSpec(v.shape, lambda i: (0, 0))
    return pl.pallas_call(
        body, name="conv_fwd", grid=(S // t,),
        in_specs=[pl.BlockSpec((t, C), lambda i: (i, 0)),
                  pl.BlockSpec((HALO, C), lambda i: (jnp.maximum(i * per - 1, 0), 0)),
                  vec(conv_w), vec(conv_b), vec(ln_g), vec(ln_b)],
        out_specs=[pl.BlockSpec((t, C), lambda i: (i, 0)), pl.BlockSpec((t, C), lambda i: (i, 0))],
        out_shape=[jax.ShapeDtypeStruct((S, C), F32), jax.ShapeDtypeStruct((S, C), BF16)],
        scratch_shapes=[pltpu.VMEM((t + HALO, C), F32)],
        compiler_params=_params(("parallel",)),
    )(z0, z0, conv_w, conv_b, ln_g, ln_b)


def _conv_bwd_norm(dz3, z1, ln_g, ln_b):
    C = z1.shape[1]

    def body(dz3_ref, z1_ref, g_ref, beta_ref, dz1_ref, dg_ref, dbeta_ref, dbias_ref):
        z1 = z1_ref[...]
        mu = jnp.mean(z1, axis=-1, keepdims=True)
        zc = z1 - mu
        rs = lax.rsqrt(jnp.mean(zc * zc, axis=-1, keepdims=True) + EPS)
        xh = zc * rs
        z2 = xh * g_ref[...] + beta_ref[...]
        sg = _sigmoid(z2)
        dz2 = dz3_ref[...].astype(F32) * (sg * (1.0 + z2 * (1.0 - sg)))
        dxh = dz2 * g_ref[...]
        dz1 = rs * (dxh - jnp.mean(dxh, axis=-1, keepdims=True) - xh * jnp.mean(dxh * xh, axis=-1, keepdims=True))
        dz1_ref[...] = dz1
        _accum(dg_ref, _colsum(dz2 * xh))
        _accum(dbeta_ref, _colsum(dz2))
        _accum(dbias_ref, _colsum(dz1))

    return _rows("conv_bwd_norm", body, [dz3, z1], [ln_g, ln_b], [(C, F32)], [(1, C)] * 3)


def _conv_bwd_taps(dz1, z0, conv_in, conv_w, dproj, window):
    S, C = z0.shape
    t = _pick(S, (512, 256, 128, 64, 32))
    per = t // HALO
    last = S // HALO - 1
    nt = S // t
    assert window[1] == 2 * C and window[0] % window[1] == 0, window

    def body(dcur_ref, dnext_ref, zcur_ref, zprev_ref, cin_ref, w_ref, _, dcin_ref, dw_ref, dext_ref, zext_ref):
        i = pl.program_id(0)
        dcur = dcur_ref[...]
        dext_ref[0:t, :] = dcur
        dext_ref[t:, :] = jnp.where(i < nt - 1, dnext_ref[...], 0.0)
        zext_ref[0:HALO, :] = jnp.where(i > 0, zprev_ref[...], 0.0)
        zext_ref[HALO:, :] = zcur_ref[...]

        @pl.when(i == 0)
        def _():
            dw_ref[...] = jnp.zeros_like(dw_ref)

        dz0 = jnp.zeros((t, C), F32)
        for d, win in _shifted(dext_ref[...], t):
            k = CONV_WIDTH - 1 - d
            if 0 <= k < CONV_WIDTH:
                dz0 = dz0 + win * w_ref[k:k + 1, :]
        for d, win in _shifted(zext_ref[...], t):
            k = d - (HALO - CONV_WIDTH + 1)
            if 0 <= k < CONV_WIDTH:
                dw_ref[k:k + 1, :] += _colsum(dcur * win)
        a = cin_ref[:, 0:C].astype(F32)
        sg = _sigmoid(cin_ref[:, C:2 * C].astype(F32))
        dcin_ref[:, 0:C] = (dz0 * sg).astype(BF16)
        dcin_ref[:, C:2 * C] = (dz0 * a * sg * (1.0 - sg)).astype(BF16)

    return pl.pallas_call(
        body, name="conv_bwd_taps", grid=(nt,),
        in_specs=[pl.BlockSpec((t, C), lambda i: (i, 0)),
                  pl.BlockSpec((HALO, C), lambda i: (jnp.minimum((i + 1) * per, last), 0)),
                  pl.BlockSpec((t, C), lambda i: (i, 0)),
                  pl.BlockSpec((HALO, C), lambda i: (jnp.maximum(i * per - 1, 0), 0)),
                  pl.BlockSpec((t, 2 * C), lambda i: (i, 0)),
                  pl.BlockSpec(conv_w.shape, lambda i: (0, 0)),
                  pl.BlockSpec(memory_space=pl.ANY)],
        out_specs=[pl.BlockSpec((t, 2 * C), lambda i: (i, window[0] // window[1])), pl.BlockSpec((HALO, C), lambda i: (0, 0))],
        out_shape=[jax.ShapeDtypeStruct(dproj.shape, BF16), jax.ShapeDtypeStruct((HALO, C), F32)],
        input_output_aliases={6: 0},
        scratch_shapes=[pltpu.VMEM((t + HALO, C), F32), pltpu.VMEM((t + HALO, C), F32)],
        compiler_params=_params(("arbitrary",)),
    )(dz1, dz1, z0, z0, conv_in, conv_w, dproj)


def _lower_tri(shape, rows_are_queries):
    row = lax.broadcasted_iota(jnp.int32, shape, 0)
    col = lax.broadcasted_iota(jnp.int32, shape, 1)
    return (col <= row) if rows_are_queries else (row <= col)


HEADS_PER_STEP = 2
FWD_HEADS_PER_STEP = 2
FWD_KEY_TILES = 8


def _flash_specs(S, t, heads):
    w = heads * HEAD_PAD
    blk = pl.BlockSpec((t, w), lambda h, i: (i, h))
    head = pl.BlockSpec((S, w), lambda h, i: (0, h))
    return blk, head


def _head_lanes(g):
    return slice(g * HEAD_PAD, (g + 1) * HEAD_PAD)


def _dot_nt(a, b):
    return lax.dot_general(a, b, _DIMS["nt"], preferred_element_type=F32)


def _dot_nn(a, b):
    return lax.dot_general(a, b, _DIMS["nn"], preferred_element_type=F32)


def _dot_tn(a, b):
    return lax.dot_general(a, b, _DIMS["tn"], preferred_element_type=F32)


def _flash_fwd(q, k, v):
    S = q.shape[0]
    t = _pick(S, (512, 256, 128))

    def body(q_ref, k_ref, v_ref, o_ref, qa_ref, m_ref, acc_ref):
        qi = pl.program_id(1)
        m_ref[...] = jnp.full_like(m_ref, NEG)
        acc_ref[...] = jnp.zeros_like(acc_ref)

        def step(first, tiles, diag):
            width = tiles * t
            rows = pl.ds(pl.multiple_of(first, t), width)
            for g in range(FWD_HEADS_PER_STEP):
                hl = _head_lanes(g)
                s = _dot_nt(q_ref[:, hl], k_ref[rows, hl])
                if diag:
                    row = lax.broadcasted_iota(jnp.int32, s.shape, 0)
                    col = lax.broadcasted_iota(jnp.int32, s.shape, 1)
                    s = jnp.where(col <= row + (tiles - 1) * t, s, NEG)
                m_old = m_ref[g]
                m_new = jnp.maximum(m_old, jnp.max(s, axis=-1, keepdims=True))
                p = jnp.exp(s - m_new).astype(BF16)
                acc_ref[g] = jnp.exp(m_old - m_new) * acc_ref[g] + _dot_nn(p, v_ref[rows, hl])
                m_ref[g] = m_new

        def wide(kb, carry):
            step(kb * (FWD_KEY_TILES * t), FWD_KEY_TILES, False)
            return carry

        full_groups = qi // FWD_KEY_TILES
        lax.fori_loop(0, full_groups, wide, 0)
        for tiles in range(1, min(FWD_KEY_TILES, S // t) + 1):
            @pl.when(qi - full_groups * FWD_KEY_TILES == tiles - 1)
            def _():
                step(full_groups * (FWD_KEY_TILES * t), tiles, True)

        for g in range(FWD_HEADS_PER_STEP):
            hl = _head_lanes(g)
            acc = acc_ref[g]
            l = -acc[:, STAT_COL_V:STAT_COL_V + 1]
            o_ref[:, hl] = (acc / l).astype(BF16)
            qa_ref[:, hl] = _put_stats(q_ref[:, hl], m_ref[g] + jnp.log(l), STAT_COL_QK)

    blk, head = _flash_specs(S, t, FWD_HEADS_PER_STEP)
    return pl.pallas_call(
        body, name="mla_flash_fwd", grid=(MLA_HEADS // FWD_HEADS_PER_STEP, S // t),
        in_specs=[blk, head, head], out_specs=[blk, blk],
        out_shape=[jax.ShapeDtypeStruct(q.shape, BF16), jax.ShapeDtypeStruct(q.shape, BF16)],
        scratch_shapes=[pltpu.VMEM((FWD_HEADS_PER_STEP, t, 1), F32), pltpu.VMEM((FWD_HEADS_PER_STEP, t, HEAD_PAD), F32)],
        compiler_params=_params(("parallel", "arbitrary")),
    )(q, k, v)


def _flash_bwd(qa, k, v, doa):
    S = qa.shape[0]
    t = _pick(S, (1024, 512, 256, 128))
    n = S // t
    half = t // 2

    def body(qa_ref, k_ref, v_ref, do_ref, dq_ref, dk_ref, dv_ref, dk_acc, dv_acc):
        kj = pl.program_id(1)

        @pl.when(kj == 0)
        def _():
            dq_ref[...] = jnp.zeros_like(dq_ref)

        dk_acc[...] = jnp.zeros_like(dk_acc)
        dv_acc[...] = jnp.zeros_like(dv_acc)

        def step(q_first, q_len, keys, diag):
            rows = pl.ds(pl.multiple_of(q_first, q_len), q_len)
            for g in range(HEADS_PER_STEP):
                hl = _head_lanes(g)
                qa, do, kk = qa_ref[rows, hl], do_ref[rows, hl], k_ref[keys, hl]
                st = _dot_nt(kk, qa)
                if diag:
                    st = jnp.where(_lower_tri(st.shape, False), st, NEG)
                pt = jnp.exp(st)
                dst = (pt * _dot_nt(v_ref[keys, hl], do)).astype(BF16)
                dv_acc[keys, hl] += _dot_nn(pt.astype(BF16), do)
                dk_acc[keys, hl] += _dot_nn(dst, qa)
                dq_ref[rows, hl] += _dot_tn(dst, kk)

        def loop(qi, carry):
            step(qi * t, t, slice(0, t), False)
            return carry

        lo, hi = slice(0, half), slice(half, t)
        step(kj * t, half, lo, True)
        step(kj * t + half, half, lo, False)
        step(kj * t + half, half, hi, True)
        lax.fori_loop(kj + 1, n, loop, 0)
        dk_ref[...] = dk_acc[...].astype(BF16)
        dv_ref[...] = dv_acc[...].astype(BF16)

    blk, head = _flash_specs(S, t, HEADS_PER_STEP)
    w = HEADS_PER_STEP * HEAD_PAD
    return pl.pallas_call(
        body, name="mla_flash_bwd", grid=(MLA_HEADS // HEADS_PER_STEP, n),
        in_specs=[head, blk, blk, head], out_specs=[head, blk, blk],
        out_shape=[jax.ShapeDtypeStruct(qa.shape, F32), jax.ShapeDtypeStruct(qa.shape, BF16), jax.ShapeDtypeStruct(qa.shape, BF16)],
        scratch_shapes=[pltpu.VMEM((t, w), F32), pltpu.VMEM((t, w), F32)],
        compiler_params=_params(("parallel", "arbitrary")),
    )(qa, k, v, doa)


def _xattn_fwd(xq, kvx):
    W = X_HEADS * X_HEAD_DIM

    def body(q_ref, kv_ref, o_ref):
        for h in range(X_HEADS):
            lo = h * X_HEAD_DIM
            s = _dot_nt(q_ref[:, lo:lo + X_HEAD_DIM], kv_ref[:, lo:lo + X_HEAD_DIM])
            p = jnp.exp(s - jnp.max(s, axis=-1, keepdims=True))
            p = p / jnp.sum(p, axis=-1, keepdims=True)
            o_ref[:, lo:lo + X_HEAD_DIM] = _dot_nn(p.astype(BF16), kv_ref[:, W + lo:W + lo + X_HEAD_DIM]).astype(BF16)

    return _rows("xattn_fwd", body, [xq], [kvx], [(W, BF16)])[0]


def _xattn_bwd(xq, kvx, dox):
    W = X_HEADS * X_HEAD_DIM
    scale = X_HEAD_DIM ** -0.5

    def body(q_ref, do_ref, kv_ref, dq_ref, dkv_ref):
        @pl.when(pl.program_id(0) == 0)
        def _():
            dkv_ref[...] = jnp.zeros_like(dkv_ref)

        for h in range(X_HEADS):
            lo = h * X_HEAD_DIM
            q, k = q_ref[:, lo:lo + X_HEAD_DIM], kv_ref[:, lo:lo + X_HEAD_DIM]
            v, do = kv_ref[:, W + lo:W + lo + X_HEAD_DIM], do_ref[:, lo:lo + X_HEAD_DIM]
            s = _dot_nt(q, k)
            p = jnp.exp(s - jnp.max(s, axis=-1, keepdims=True))
            p = p / jnp.sum(p, axis=-1, keepdims=True)
            dp = _dot_nt(do, v)
            ds = (p * (dp - jnp.sum(dp * p, axis=-1, keepdims=True))).astype(BF16)
            dq_ref[:, lo:lo + X_HEAD_DIM] = (_dot_nn(ds, k) * scale).astype(BF16)
            dkv_ref[:, lo:lo + X_HEAD_DIM] += _dot_tn(ds, q)
            dkv_ref[:, W + lo:W + lo + X_HEAD_DIM] += _dot_tn(p.astype(BF16), do)

    return _rows("xattn_bwd", body, [xq, dox], [kvx], [(W, BF16)], [kvx.shape])


def _adamw(name, w, g, m, v):
    c1 = 1.0 / (1.0 - ADAM_B1 ** ADAM_STEP)
    c2 = 1.0 / (1.0 - ADAM_B2 ** ADAM_STEP)
    lead = (0,) * (w.ndim - 2)
    w2 = w.reshape((1,) * (2 - w.ndim) + w.shape) if w.ndim < 2 else w
    m2, v2 = m.reshape(w2.shape), v.reshape(w2.shape)
    g2 = g.reshape(w2.shape[-2:])
    R, C = g2.shape
    t = _pick(R, (256, 128, 64, 32, 16, 8))

    def body(w_ref, g_ref, m_ref, v_ref, go_ref, d_ref, nm_ref, nv_ref):
        g = g_ref[...]
        nm = ADAM_B1 * m_ref[lead] + (1.0 - ADAM_B1) * g
        nv = ADAM_B2 * v_ref[lead] + (1.0 - ADAM_B2) * (g * g)
        go_ref[lead] = g
        d_ref[lead] = -ADAM_LR * ((nm * c1) / (jnp.sqrt(nv * c2) + ADAM_EPS) + ADAM_WD * w_ref[lead])
        nm_ref[lead] = nm
        nv_ref[lead] = nv

    full = pl.BlockSpec((1,) * len(lead) + (t, C), lambda i: lead + (i, 0))
    outs = pl.pallas_call(
        body, name=name, grid=(R // t,), in_specs=[full, pl.BlockSpec((t, C), lambda i: (i, 0)), full, full],
        out_specs=[full] * 4, out_shape=[jax.ShapeDtypeStruct(w2.shape, F32)] * 4, compiler_params=_params(("parallel",)),
    )(w2, g2, m2, v2)
    return [o.reshape(w.shape) for o in outs]


def _place():
    x, y, c = lax.axis_index("x"), lax.axis_index("y"), lax.axis_index("c")
    return x, y, c, [(1 - x, y), (x, 1 - y), (1 - x, 1 - y)]


_ANY = pl.BlockSpec(memory_space=pl.ANY)


_HBM = pl.BlockSpec(memory_space=pltpu.HBM)
_SEM = pl.BlockSpec(memory_space=pltpu.SEMAPHORE)
_SIDE_EFFECT = pltpu.SideEffectType.DATAFLOW_SIDE_EFFECTING


def _gather_copy(src, land, slot, send, recv, k, chip, c):
    return pltpu.make_async_remote_copy(src_ref=src, dst_ref=land.at[slot], send_sem=send.at[k], recv_sem=recv.at[k],
                                        device_id=(chip[0], chip[1], c), device_id_type=MESH)


def _gather_start(shards, groups):
    n, ng = len(shards), len(groups)
    mine = 2 * lax.axis_index("x") + lax.axis_index("y")
    lands = [lax.dynamic_update_slice(lax.empty((N_CHIPS,) + s.shape, s.dtype), s[None], (mine,) + (0,) * s.ndim)
             for s in shards]

    def body(*refs):
        ins, lnd = refs[:n], refs[n:2 * n]
        sends, recvs = refs[2 * n:2 * n + ng], refs[2 * n + ng:2 * n + 2 * ng]
        token = refs[-1]
        x, y, c, chips = _place()
        for gi, group in enumerate(groups):
            for pos, w in enumerate(group):
                for j, chip in enumerate(chips):
                    _gather_copy(ins[w], lnd[w], 2 * x + y, sends[gi], recvs[gi], 3 * pos + j, chip, c).start()
        token[...] = jnp.zeros_like(token)

    sems = [pltpu.SemaphoreType.DMA((3 * len(g),)) for g in groups]
    res = pl.pallas_call(
        body, name="gather_weights_start",
        out_shape=sems + sems + [pltpu.HBM(a.shape, a.dtype) for a in shards + lands] + [jax.ShapeDtypeStruct((8, LANES), F32)],
        in_specs=[_HBM] * (2 * n),
        out_specs=[_SEM] * (2 * ng) + [_HBM] * (2 * n) + [pl.BlockSpec(memory_space=pltpu.VMEM)],
        input_output_aliases={i: 2 * ng + i for i in range(2 * n)},
        compiler_params=pltpu.CompilerParams(has_side_effects=_SIDE_EFFECT),
    )(*[pltpu.with_memory_space_constraint(a, pltpu.HBM) for a in shards + lands])
    sem_pairs = list(zip(res[:ng], res[ng:2 * ng]))
    return sem_pairs, res[2 * ng:2 * ng + n], res[2 * ng + n:2 * ng + 2 * n], res[-1]


def _gather_wait(name, sem_pair, shards_thru, lands_thru, after):
    m = len(shards_thru)

    def body(*refs):
        ins, lnd = refs[:m], refs[m:2 * m]
        send, recv = refs[2 * m], refs[2 * m + 1]
        x, y, c, chips = _place()
        for pos in range(m):
            for j, chip in enumerate(chips):
                cp = _gather_copy(ins[pos], lnd[pos], 2 * chip[0] + chip[1], send, recv, 3 * pos + j, chip, c)
                cp.wait_send()
                cp.wait_recv()

    res = pl.pallas_call(
        body, name=name,
        out_shape=[pltpu.HBM(a.shape, a.dtype) for a in list(shards_thru) + list(lands_thru)],
        in_specs=[_HBM] * (2 * m) + [_SEM, _SEM, _ANY], out_specs=[_HBM] * (2 * m),
        input_output_aliases={i: i for i in range(2 * m)},
        compiler_params=pltpu.CompilerParams(has_side_effects=_SIDE_EFFECT),
    )(*shards_thru, *lands_thru, *sem_pair, after)
    return res[m:]


def _pair_exchange(name, packs):
    n = len(packs)

    def body(*refs):
        ins, outs, send, recv = refs[:n], refs[n:2 * n], refs[2 * n], refs[2 * n + 1]
        x, y, c, _ = _place()
        cps = []
        for g in range(n):
            cp = pltpu.make_async_remote_copy(src_ref=ins[g].at[:, pl.ds(1 - c, 1)], dst_ref=outs[g], send_sem=send.at[g],
                                              recv_sem=recv.at[g], device_id=(x, y, 1 - c), device_id_type=MESH)
            cp.start()
            cps.append(cp)
        for cp in cps:
            cp.wait()

    return pl.pallas_call(
        body, name=name, in_specs=[_ANY] * n, out_specs=[_ANY] * n,
        out_shape=[jax.ShapeDtypeStruct((N_CHIPS, 1) + p.shape[2:], p.dtype) for p in packs],
        scratch_shapes=[pltpu.SemaphoreType.DMA((n,)), pltpu.SemaphoreType.DMA((n,))],
    )(*packs)


def _chip_copy(src, land, src_slot, dst_slot, send, recv, k, chip, c):
    return pltpu.make_async_remote_copy(src_ref=src.at[src_slot], dst_ref=land.at[dst_slot], send_sem=send.at[k],
                                        recv_sem=recv.at[k], device_id=(chip[0], chip[1], c), device_id_type=MESH)


def _chip_exchange_start(name, parts):
    n = len(parts)
    lands = [lax.empty(p.shape, p.dtype) for p in parts]

    def body(*refs):
        ins, lnd, send, recv, token = refs[:n], refs[n:2 * n], refs[2 * n], refs[2 * n + 1], refs[-1]
        x, y, c, chips = _place()
        for g in range(n):
            for j, chip in enumerate(chips):
                _chip_copy(ins[g], lnd[g], 2 * chip[0] + chip[1], 2 * x + y, send, recv, 3 * g + j, chip, c).start()
        token[...] = jnp.zeros_like(token)

    sems = [pltpu.SemaphoreType.DMA((3 * n,))] * 2
    res = pl.pallas_call(
        body, name=name,
        out_shape=sems + [pltpu.HBM(a.shape, a.dtype) for a in list(parts) + lands] + [jax.ShapeDtypeStruct((8, LANES), F32)],
        in_specs=[_HBM] * (2 * n),
        out_specs=[_SEM] * 2 + [_HBM] * (2 * n) + [pl.BlockSpec(memory_space=pltpu.VMEM)],
        input_output_aliases={i: 2 + i for i in range(2 * n)},
        compiler_params=pltpu.CompilerParams(has_side_effects=_SIDE_EFFECT),
    )(*[pltpu.with_memory_space_constraint(a, pltpu.HBM) for a in list(parts) + lands])
    return res[:2], res[2:2 + n], res[2 + n:2 + 2 * n], res[-1]


def _chip_exchange_wait(name, sems, parts_thru, lands_thru, after):
    n = len(parts_thru)

    def body(*refs):
        ins, lnd, send, recv = refs[:n], refs[n:2 * n], refs[2 * n], refs[2 * n + 1]
        x, y, c, chips = _place()
        for g in range(n):
            for j, chip in enumerate(chips):
                cp = _chip_copy(ins[g], lnd[g], 2 * x + y, 2 * chip[0] + chip[1], send, recv, 3 * g + j, chip, c)
                cp.wait_send()
                cp.wait_recv()

    res = pl.pallas_call(
        body, name=name,
        out_shape=[pltpu.HBM(a.shape, a.dtype) for a in list(parts_thru) + list(lands_thru)],
        in_specs=[_HBM] * (2 * n) + [_SEM, _SEM, _ANY], out_specs=[_HBM] * (2 * n),
        input_output_aliases={i: i for i in range(2 * n)},
        compiler_params=pltpu.CompilerParams(has_side_effects=_SIDE_EFFECT),
    )(*parts_thru, *lands_thru, *sems, after)
    return res[:n], res[n:]


def _pair_share(halves):
    n = len(halves)

    def body(*refs):
        outs, send, recv = refs[n:2 * n], refs[2 * n], refs[2 * n + 1]
        x, y, c, _ = _place()
        cps = []
        for g in range(n):
            cp = pltpu.make_async_remote_copy(src_ref=outs[g].at[c], dst_ref=outs[g].at[c], send_sem=send.at[g],
                                              recv_sem=recv.at[g], device_id=(x, y, 1 - c), device_id_type=MESH)
            cp.start()
            cps.append(cp)
        for g in range(n):
            pltpu.make_async_remote_copy(src_ref=outs[g].at[c], dst_ref=outs[g].at[1 - c], send_sem=send.at[g],
                                         recv_sem=recv.at[g], device_id=(x, y, 1 - c), device_id_type=MESH).wait_recv()
        for cp in cps:
            cp.wait_send()

    return pl.pallas_call(
        body, name="grad_pair_share", in_specs=[_ANY] * n, out_specs=[_ANY] * n,
        out_shape=[jax.ShapeDtypeStruct(h.shape, h.dtype) for h in halves],
        input_output_aliases={g: g for g in range(n)},
        scratch_shapes=[pltpu.SemaphoreType.DMA((n,)), pltpu.SemaphoreType.DMA((n,))],
    )(*halves)


def _sum_over_devices(block):
    R, L = block.shape

    def gather(b_ref, o_ref, send, recv, loc):
        x, y, c, _ = _place()
        lc = pltpu.make_async_copy(b_ref, o_ref.at[4 * x + 2 * y + c], loc)
        lc.start()
        peers = [(1 - x if dx else x, 1 - y if dy else y, 1 - c if dc else c)
                 for dx in (0, 1) for dy in (0, 1) for dc in (0, 1) if dx or dy or dc]
        cps = []
        for j, peer in enumerate(peers):
            cp = pltpu.make_async_remote_copy(src_ref=b_ref, dst_ref=o_ref.at[4 * x + 2 * y + c], send_sem=send.at[j],
                                              recv_sem=recv.at[j], device_id=peer, device_id_type=MESH)
            cp.start()
            cps.append(cp)
        for j, (px, py, pc) in enumerate(peers):
            pltpu.make_async_remote_copy(src_ref=b_ref, dst_ref=o_ref.at[4 * px + 2 * py + pc], send_sem=send.at[j],
                                         recv_sem=recv.at[j], device_id=(px, py, pc), device_id_type=MESH).wait_recv()
        for cp in cps:
            cp.wait_send()
        lc.wait()

    blocks = pl.pallas_call(
        gather, name="small_grads_gather", in_specs=[_ANY], out_specs=_ANY,
        out_shape=jax.ShapeDtypeStruct((N_DEV, R, L), F32),
        scratch_shapes=[pltpu.SemaphoreType.DMA((N_DEV - 1,)), pltpu.SemaphoreType.DMA((N_DEV - 1,)), pltpu.SemaphoreType.DMA],
    )(block)

    def add(b_ref, o_ref):
        total = b_ref[0]
        for d in range(1, N_DEV):
            total = total + b_ref[d]
        o_ref[...] = total

    return pl.pallas_call(add, name="small_grads_add", out_shape=jax.ShapeDtypeStruct((R, L), F32))(blocks)


def _pair_add(name, pack, got):
    _, _, R, C = pack.shape
    t = _pick(R, ROW_TILES)
    c = lax.axis_index("c").astype(jnp.int32).reshape(1)

    def body(c_ref, p_ref, g_ref, o_ref):
        o_ref[...] = (p_ref[...].astype(F32) + g_ref[...].astype(F32)).astype(BF16)

    return pl.pallas_call(
        body, name=name,
        grid_spec=pltpu.PrefetchScalarGridSpec(
            num_scalar_prefetch=1, grid=(N_CHIPS, R // t),
            in_specs=[pl.BlockSpec((None, None, t, C), lambda k, i, c_ref: (k, c_ref[0], i, 0)),
                      pl.BlockSpec((None, None, t, C), lambda k, i, c_ref: (k, 0, i, 0))],
            out_specs=pl.BlockSpec((None, t, C), lambda k, i, c_ref: (k, i, 0))),
        out_shape=jax.ShapeDtypeStruct((N_CHIPS, R, C), BF16), compiler_params=_params(("parallel", "parallel")),
    )(c, pack, got)


def _chip_add(name, own, got):
    _, R, C = own.shape
    t = _pick(R, ROW_TILES)
    x, y, c, _ = _place()
    place = jnp.stack([c, 2 * x + y]).astype(jnp.int32)

    def body(place_ref, own_ref, g1_ref, g2_ref, g3_ref, o_ref):
        o_ref[...] = ((own_ref[...].astype(F32) + g1_ref[...].astype(F32)) + g2_ref[...].astype(F32)) + g3_ref[...].astype(F32)

    def other(d):
        return pl.BlockSpec((None, t, C), lambda i, place_ref: ((place_ref[1] + d) % N_CHIPS, i, 0))

    return pl.pallas_call(
        body, name=name,
        grid_spec=pltpu.PrefetchScalarGridSpec(
            num_scalar_prefetch=1, grid=(R // t,),
            in_specs=[pl.BlockSpec((None, t, C), lambda i, place_ref: (place_ref[1], i, 0)), other(1), other(2), other(3)],
            out_specs=pl.BlockSpec((None, t, C), lambda i, place_ref: (place_ref[0], i, 0))),
        out_shape=jax.ShapeDtypeStruct((2, R, C), F32), compiler_params=_params(("parallel",)),
    )(place, own, got, got, got)


_CUT = (2 * CONV_CH, 2 * CONV_CH + Q_LORA, 2 * CONV_CH + Q_LORA + KV_LORA, 2 * CONV_CH + Q_LORA + KV_LORA + QK_ROPE)
W_IN_GATES = (0, 2 * D_MODEL)
W_IN_CONV = (W_IN_GATES[1], 2 * CONV_CH)
W_IN_CQ = (W_IN_CONV[0] + W_IN_CONV[1], Q_LORA)
W_IN_KR = (W_IN_CQ[0] + W_IN_CQ[1], HEAD_PAD)
W_IN_CKV = (W_IN_KR[0] + W_IN_KR[1], KV_LORA)
W_IN_LORA = (W_IN_CQ[0], Q_LORA + HEAD_PAD + KV_LORA)
W_IN_COLS = W_IN_CKV[0] + W_IN_CKV[1]


def _pad_last(a, n):
    return jnp.pad(a, [(0, 0)] * (a.ndim - 1) + [(0, n - a.shape[-1])])


def _layout_w_in(w_in):
    kr = jnp.pad(w_in[:, _CUT[2]:_CUT[3]], ((0, 0), (QK_NOPE, HEAD_PAD - QK_NOPE - QK_ROPE)))
    return jnp.concatenate([w_in[:, _CUT[3]:], w_in[:, :_CUT[0]], w_in[:, _CUT[0]:_CUT[1]], kr, w_in[:, _CUT[1]:_CUT[2]]], axis=1)


def _layout_weights(w):
    out = dict(w)
    if "w_uq" in w:
        out["w_uq"] = _pad_last(w["w_uq"].reshape(Q_LORA, MLA_HEADS, QK_NOPE + QK_ROPE), HEAD_PAD).reshape(Q_LORA, MLA_HEADS * HEAD_PAD)
    if "w_ukv" in w:
        ukv = w["w_ukv"].reshape(KV_LORA, MLA_HEADS, QK_NOPE + V_DIM)
        uk = _pad_last(ukv[:, :, :QK_NOPE], HEAD_PAD).reshape(KV_LORA, MLA_HEADS * HEAD_PAD)
        uv = _pad_last(ukv[:, :, QK_NOPE:], HEAD_PAD).reshape(KV_LORA, MLA_HEADS * HEAD_PAD)
        out["w_ukv"] = jnp.concatenate([uk, uv], axis=1)
    if "w_mla_out" in w:
        mo = jnp.pad(w["w_mla_out"].reshape(MLA_HEADS, V_DIM, D_MODEL), ((0, 0), (0, HEAD_PAD - V_DIM), (0, 0)))
        out["w_mla_out"] = mo.reshape(MLA_HEADS * HEAD_PAD, D_MODEL)
    return out


def _unlayout_grads(g):
    out = dict(g)
    if "w_in" in g:
        gi = g["w_in"]
        win = lambda w: gi[:, w[0]:w[0] + w[1]]
        kr = gi[:, W_IN_KR[0] + QK_NOPE:W_IN_KR[0] + QK_NOPE + QK_ROPE]
        out["w_in"] = jnp.concatenate([win(W_IN_CONV), win(W_IN_CQ), win(W_IN_CKV), kr, win(W_IN_GATES)], axis=1)
    if "w_uq" in g:
        out["w_uq"] = g["w_uq"].reshape(Q_LORA, MLA_HEADS, HEAD_PAD)[:, :, :QK_NOPE + QK_ROPE].reshape(Q_LORA, -1)
    if "w_ukv" in g:
        gk = g["w_ukv"][:, :MLA_HEADS * HEAD_PAD].reshape(KV_LORA, MLA_HEADS, HEAD_PAD)[:, :, :QK_NOPE]
        gv = g["w_ukv"][:, MLA_HEADS * HEAD_PAD:].reshape(KV_LORA, MLA_HEADS, HEAD_PAD)[:, :, :V_DIM]
        out["w_ukv"] = jnp.concatenate([gk, gv], axis=2).reshape(KV_LORA, -1)
    if "w_mla_out" in g:
        out["w_mla_out"] = g["w_mla_out"].reshape(MLA_HEADS, HEAD_PAD, D_MODEL)[:, :V_DIM].reshape(MLA_HEADS * V_DIM, D_MODEL)
    return out


def _rope_tables(positions):
    half = QK_ROPE // 2
    inv_freq = ROPE_THETA ** (-jnp.arange(half, dtype=F32) / half)
    ang = positions.astype(F32).reshape(-1, 1) * inv_freq
    cos, sin = jnp.cos(ang), jnp.sin(ang)
    S = cos.shape[0]
    z16, z32, z64 = jnp.zeros((S, half), F32), jnp.zeros((S, QK_ROPE), F32), jnp.zeros((S, QK_NOPE), F32)
    c = jnp.concatenate([jnp.ones((S, QK_NOPE), F32), cos, cos, z32], axis=1)
    sa = jnp.concatenate([z64, -sin, z16, z32], axis=1)
    sb = jnp.concatenate([z64, z16, sin, z32], axis=1)
    return c, sa, sb


def _local_step(x, mem, positions, target, weight_fns, early_grads_fn, late_grads_fn, sm):
    S = x.shape[0]
    HW = MLA_HEADS * HEAD_PAD
    rope_c, rope_sa, rope_sb = _rope_tables(positions)
    qk_scale = (QK_NOPE + QK_ROPE) ** -0.5

    def k_rms1(x_ref, g_ref, u_ref):
        u_ref[...] = _rms_fwd(x_ref[...], g_ref[...]).astype(BF16)

    u1, = _rows("rms_mix", k_rms1, [x], [sm["norm_mix_g"]], [(D_MODEL, BF16)])
    w_in, conv_w = weight_fns[0](u1)

    def epi_glu(acc, xs, outs):
        a, gt = acc[:, 0:CONV_CH], acc[:, CONV_CH:2 * CONV_CH]
        outs[0][...] = acc[...].astype(BF16)
        outs[1][...] = a * _sigmoid(gt)

    conv_in, z0 = _mm("proj_conv", u1, w_in, "nn", [(2 * CONV_CH, BF16), (CONV_CH, F32)], epi_glu, b_cols=W_IN_CONV)
    c_q = _mm_plain("proj_cq", u1, w_in, "nn", b_cols=W_IN_CQ)
    c_kv = _mm_plain("proj_ckv", u1, w_in, "nn", b_cols=W_IN_CKV)
    kr_raw = _mm_plain("proj_krope", u1, w_in, "nn", b_cols=W_IN_KR)

    def epi_sigmoid(acc, xs, outs):
        outs[0][...] = _sigmoid(acc[...]).astype(BF16)

    gates, = _mm("proj_gates", u1, w_in, "nn", [(2 * D_MODEL, BF16)], epi_sigmoid, b_cols=W_IN_GATES)

    z1, z3 = _conv_fwd(z0, conv_w, sm["conv_b"], sm["conv_ln_g"], sm["conv_ln_b"])
    wl = weight_fns[1](z1)
    conv_out = _mm_plain("conv_out", z3, wl["w_conv_out"], "nn", dtype=BF16)

    def k_lora_norm(cq_ref, ckv_ref, gq_ref, gkv_ref, qn_ref, kvn_ref):
        qn_ref[...] = _rms_fwd(cq_ref[...], gq_ref[...]).astype(BF16)
        kvn_ref[...] = _rms_fwd(ckv_ref[...], gkv_ref[...]).astype(BF16)

    qn, kvn = _rows("lora_norm", k_lora_norm, [c_q, c_kv], [sm["q_norm_g"], sm["kv_norm_g"]],
                    [(Q_LORA, BF16), (KV_LORA, BF16)])

    def epi_q(acc, xs, outs):
        c, sa, sb = xs[0][...], xs[1][...], xs[2][...]
        for h in range(MLA_HEADS):
            lo = h * HEAD_PAD
            outs[0][:, lo:lo + HEAD_PAD] = (_rope(acc[:, lo:lo + HEAD_PAD], c, sa, sb, 1.0) * qk_scale).astype(BF16)

    q_att, = _mm("q_up", qn, wl["w_uq"], "nn", [(HW, BF16)], epi_q, row_x=[rope_c, rope_sa, rope_sb], tn=HW)

    def epi_kv(acc, xs, outs):
        kr = _rope(xs[0][...], xs[1][...], xs[2][...], xs[3][...], 1.0)
        kr = kr + _neg_ones(kr.shape, STAT_COL_QK)
        vconst = _neg_ones(kr.shape, STAT_COL_V)
        for h in range(MLA_HEADS):
            lo = h * HEAD_PAD
            outs[0][:, lo:lo + HEAD_PAD] = (acc[:, lo:lo + HEAD_PAD] + kr).astype(BF16)
            outs[1][:, lo:lo + HEAD_PAD] = (acc[:, HW + lo:HW + lo + HEAD_PAD] + vconst).astype(BF16)

    k_att, v_att = _mm("kv_up", kvn, wl["w_ukv"], "nn", [(HW, BF16), (HW, BF16)], epi_kv,
                       row_x=[kr_raw, rope_c, rope_sa, rope_sb], tn=2 * HW)

    o_att, q_aug = _flash_fwd(q_att, k_att, v_att)
    wl.update(weight_fns[2](o_att))

    def epi_merge(acc, xs, outs):
        for rows in _row_chunks(acc.shape[0]):
            mo = acc[rows, :]
            g0, g1 = xs[0][rows, 0:D_MODEL].astype(F32), xs[0][rows, D_MODEL:].astype(F32)
            outs[0][rows, :] = mo.astype(BF16)
            outs[1][rows, :] = (g0 * xs[1][rows, :].astype(F32) + g1 * mo).astype(BF16)

    mla_out, merged = _mm("mla_out_merge", o_att, wl["w_mla_out"], "nn", [(D_MODEL, BF16), (D_MODEL, BF16)], epi_merge,
                          row_x=[gates, conv_out], tn=D_MODEL)

    def epi_res_norm(acc, xs, outs):
        h = xs[0][...] + acc[...]
        outs[0][...] = h
        outs[1][...] = _rms_fwd(h, xs[1][...]).astype(BF16)

    h1, u2 = _mm("mix_out", merged, wl["w_out"], "nn", [(D_MODEL, F32), (D_MODEL, BF16)], epi_res_norm,
                 row_x=[x], vec_x=[sm["norm_xattn_g"]], tn=D_MODEL)

    xscale = X_HEAD_DIM ** -0.5

    def epi_scale(acc, xs, outs):
        outs[0][...] = (acc[...] * xscale).astype(BF16)

    xq, = _mm("xattn_q", u2, wl["w_xq"], "nn", [(X_HEADS * X_HEAD_DIM, BF16)], epi_scale)

    def k_mem_norm(m_ref, g_ref, o_ref):
        o_ref[...] = _rms_fwd(m_ref[...], g_ref[...]).astype(BF16)

    mem_n, = _rows("mem_norm", k_mem_norm, [mem], [sm["norm_mem_g"]], [(D_MODEL, BF16)])
    kvx = _mm_plain("xattn_kv", mem_n, wl["w_xkv"], "nn", dtype=BF16)
    ox = _xattn_fwd(xq, kvx)
    h2, u3 = _mm("xattn_out", ox, wl["w_xo"], "nn", [(D_MODEL, F32), (D_MODEL, BF16)], epi_res_norm,
                 row_x=[h1], vec_x=[sm["norm_mlp_g"]], tn=D_MODEL)

    def epi_relu2(acc, xs, outs):
        r = jnp.maximum(acc[...], 0.0)
        outs[0][...] = (r * r).astype(BF16)

    hid, = _mm("mlp_up", u3, wl["w_mlp1"], "nn", [(D_FF, BF16)], epi_relu2)

    def epi_final(acc, xs, outs):
        g = xs[2][...]
        for rows in _row_chunks(acc.shape[0]):
            h = xs[0][rows, :] + acc[rows, :]
            e = _rms_fwd(h, g) - xs[1][rows, :]
            part = 0.5 * jnp.sum(jnp.mean(e * e, axis=-1, keepdims=True), axis=0, keepdims=True)
            dh, dg = _rms_bwd(h, g, e * (1.0 / D_MODEL))
            outs[0][rows, :] = dh
            outs[1][rows, :] = dh.astype(BF16)
            _accum(outs[2], jnp.broadcast_to(part, outs[2].shape), first=rows.start == 0)
            _accum(outs[3], dg, first=rows.start == 0)

    dh3, dh3b, loss, g_final = _mm("mlp_down_loss", hid, wl["w_mlp2"], "nn", [(D_MODEL, F32), (D_MODEL, BF16)], epi_final,
                                   row_x=[h2, target], vec_x=[sm["final_norm_g"]], sums=[(1, LANES), (1, D_MODEL)],
                                   tn=D_MODEL, tk=1024)

    def epi_drelu2(acc, xs, outs):
        outs[0][...] = (acc[...] * (2.0 * jnp.sqrt(xs[0][...].astype(F32)))).astype(BF16)

    da1, = _mm("mlp_down_dx", dh3b, wl["w_mlp2"], "nt", [(D_FF, BF16)], epi_drelu2, tile_x=[hid])
    g_mlp2 = _mm_plain("mlp_down_dw", hid, dh3b, "tn")
    g_mlp1 = _mm_plain("mlp_up_dw", u3, da1, "tn")

    def epi_norm_bwd(acc, xs, outs):
        for rows in _row_chunks(acc.shape[0]):
            dx, dg = _rms_bwd(xs[0][rows, :], xs[2][...], acc[rows, :])
            dh = xs[1][rows, :] + dx
            outs[0][rows, :] = dh
            if len(outs) == 3:
                outs[1][rows, :] = dh.astype(BF16)
            _accum(outs[-1], dg, first=rows.start == 0)

    def dx_norm_bwd(name, dy, w, xin, dres, vecs, with_bf16=True):
        outs = [(D_MODEL, F32), (D_MODEL, BF16)] if with_bf16 else [(D_MODEL, F32)]
        return _mm(name, dy, w, "nt", outs, epi_norm_bwd, row_x=[xin, dres], vec_x=vecs, sums=[(1, D_MODEL)],
                   tn=D_MODEL, tk=_pick(dy.shape[1], (1024, 768, 512)))

    dh2, dh2b, g_norm_mlp = dx_norm_bwd("mlp_up_dx_norm", da1, wl["w_mlp1"], h2, dh3, [sm["norm_mlp_g"]])

    dox = _mm_plain("xattn_out_dx", dh2b, wl["w_xo"], "nt", dtype=BF16)
    g_xo = _mm_plain("xattn_out_dw", ox, dh2b, "tn")
    dxq, dkvx = _xattn_bwd(xq, kvx, dox)
    g_xq = _mm_plain("xattn_q_dw", u2, dxq, "tn")
    g_xkv = _mm_plain("xattn_kv_dw", mem_n, dkvx, "tn")
    dmem_n = _mm_plain("xattn_kv_dx", dkvx, wl["w_xkv"], "nt")

    def k_mem_bwd(m_ref, d_ref, g_ref, dg_ref):
        _, dg = _rms_bwd(m_ref[...], g_ref[...], d_ref[...])
        _accum(dg_ref, dg)

    g_norm_mem, = _rows("mem_norm_bwd", k_mem_bwd, [mem, dmem_n], [sm["norm_mem_g"]], [], [(1, D_MODEL)])
    token = early_grads_fn(dict(w_mlp1=g_mlp1, w_mlp2=g_mlp2, w_xo=g_xo, w_xq=g_xq, w_xkv=g_xkv))
    dh1, dh1b, g_norm_xattn = dx_norm_bwd("xattn_q_dx_norm", dxq, wl["w_xq"], h1, dh2, [sm["norm_xattn_g"], token])

    dmerged = _mm_plain("mix_out_dx", dh1b, wl["w_out"], "nt", dtype=BF16)
    g_out = _mm_plain("mix_out_dw", merged, dh1b, "tn")

    def k_merge_bwd(dm_ref, g_ref, co_ref, mo_ref, dco_ref, dmo_ref, dgl_ref):
        dm = dm_ref[...].astype(F32)
        g0, g1 = g_ref[:, 0:D_MODEL].astype(F32), g_ref[:, D_MODEL:].astype(F32)
        dco_ref[...] = (dm * g0).astype(BF16)
        dmo_ref[...] = (dm * g1).astype(BF16)
        dgl_ref[:, 0:D_MODEL] = (dm * co_ref[...].astype(F32) * g0 * (1.0 - g0)).astype(BF16)
        dgl_ref[:, D_MODEL:] = (dm * mo_ref[...].astype(F32) * g1 * (1.0 - g1)).astype(BF16)

    dconv_out, dmla_out, dproj = _rows("merge_bwd", k_merge_bwd, [dmerged, gates, conv_out, mla_out], [],
                                       [(D_MODEL, BF16), (D_MODEL, BF16), (2 * D_MODEL, BF16)], tile=256,
                                       window=(2, W_IN_GATES, W_IN_COLS, None))

    def epi_do(acc, xs, outs):
        for h in range(MLA_HEADS):
            lo = h * HEAD_PAD
            do = acc[:, lo:lo + HEAD_PAD]
            delta = jnp.sum(do * xs[0][:, lo:lo + HEAD_PAD].astype(F32), axis=-1, keepdims=True)
            outs[0][:, lo:lo + HEAD_PAD] = _put_stats(do.astype(BF16), delta, STAT_COL_V)

    do_aug, = _mm("mla_out_dx", dmla_out, wl["w_mla_out"], "nt", [(HW, BF16)], epi_do, row_x=[o_att], tn=HW)
    g_mla_out = _mm_plain("mla_out_dw", o_att, dmla_out, "tn")
    dq_att, dk_att, dv_att = _flash_bwd(q_aug, k_att, v_att, do_aug)

    def k_rope_bwd(dq_ref, dk_ref, dv_ref, c_ref, sa_ref, sb_ref, dqr_ref, dkv_ref, dkr_ref):
        c, sa, sb = c_ref[...], sa_ref[...], sb_ref[...]
        lane = lax.broadcasted_iota(jnp.int32, c.shape, 1)
        nope = (lane < QK_NOPE).astype(F32)
        ropem = ((lane >= QK_NOPE) & (lane < QK_NOPE + QK_ROPE)).astype(F32)
        dkr = jnp.zeros(c.shape, F32)
        for h in range(MLA_HEADS):
            lo = h * HEAD_PAD
            dqr_ref[:, lo:lo + HEAD_PAD] = (_rope(dq_ref[:, lo:lo + HEAD_PAD], c, sa, sb, -1.0) * qk_scale).astype(BF16)
            dk = dk_ref[:, lo:lo + HEAD_PAD].astype(F32)
            dkv_ref[:, lo:lo + HEAD_PAD] = (dk * nope).astype(BF16)
            dkr = dkr + dk
        dkv_ref[:, HW:] = dv_ref[...]
        dkr_ref[...] = (_rope(dkr * ropem, c, sa, sb, -1.0) * ropem).astype(BF16)

    dq_raw, dkv_cat, dkr = _rows("rope_bwd", k_rope_bwd, [dq_att, dk_att, dv_att, rope_c, rope_sa, rope_sb], [],
                                 [(HW, BF16), (2 * HW, BF16), (HEAD_PAD, BF16)], tile=256)
    g_uq = _mm_plain("q_up_dw", qn, dq_raw, "tn")
    dqn = _mm_plain("q_up_dx", dq_raw, wl["w_uq"], "nt")
    g_ukv = _mm_plain("kv_up_dw", kvn, dkv_cat, "tn")
    dkvn = _mm_plain("kv_up_dx", dkv_cat, wl["w_ukv"], "nt")

    def k_lora_bwd(cq_ref, ckv_ref, dqn_ref, dkvn_ref, dkr_ref, gq_ref, gkv_ref, out_ref, dgq_ref, dgkv_ref):
        dcq, dgq = _rms_bwd(cq_ref[...], gq_ref[...], dqn_ref[...])
        dckv, dgkv = _rms_bwd(ckv_ref[...], gkv_ref[...], dkvn_ref[...])
        out_ref[:, 0:Q_LORA] = dcq.astype(BF16)
        out_ref[:, Q_LORA:Q_LORA + HEAD_PAD] = dkr_ref[...]
        out_ref[:, Q_LORA + HEAD_PAD:] = dckv.astype(BF16)
        _accum(dgq_ref, dgq)
        _accum(dgkv_ref, dgkv)

    dproj, g_q_norm, g_kv_norm = _rows("lora_norm_bwd", k_lora_bwd, [c_q, c_kv, dqn, dkvn, dkr],
                                       [sm["q_norm_g"], sm["kv_norm_g"]], [(W_IN_LORA[1], BF16)],
                                       [(1, Q_LORA), (1, KV_LORA)], window=(0, W_IN_LORA, W_IN_COLS, dproj))

    dz3 = _mm_plain("conv_out_dx", dconv_out, wl["w_conv_out"], "nt", dtype=BF16)
    g_conv_out = _mm_plain("conv_out_dw", z3, dconv_out, "tn")
    dz1, g_ln_g, g_ln_b, g_conv_b = _conv_bwd_norm(dz3, z1, sm["conv_ln_g"], sm["conv_ln_b"])
    dproj, g_conv_w = _conv_bwd_taps(dz1, z0, conv_in, conv_w, dproj, W_IN_CONV)

    g_in = _mm_plain("proj_dw", u1, dproj, "tn")
    token = late_grads_fn(dict(w_in=g_in, conv_w=g_conv_w[:CONV_WIDTH], w_conv_out=g_conv_out, w_uq=g_uq, w_ukv=g_ukv,
                               w_mla_out=g_mla_out, w_out=g_out))
    grad_x, g_norm_mix = dx_norm_bwd("proj_dx_norm", dproj, w_in, x, dh1, [sm["norm_mix_g"], token], with_bf16=False)

    small = dict(norm_mix_g=g_norm_mix, conv_b=g_conv_b, conv_ln_g=g_ln_g, conv_ln_b=g_ln_b, q_norm_g=g_q_norm,
                 kv_norm_g=g_kv_norm, norm_xattn_g=g_norm_xattn, norm_mem_g=g_norm_mem, norm_mlp_g=g_norm_mlp,
                 final_norm_g=g_final)
    return loss, grad_x, small


def _shard(a, k, axis):
    n = a.shape[axis] // N_CHIPS
    return lax.slice_in_dim(a, k * n, (k + 1) * n, axis=axis)


def _pack_small(grads, loss):
    flat = jnp.concatenate([grads[n].reshape(-1) for n in SMALL] + [loss.reshape(-1)[:1]])
    rows = -(-flat.shape[0] // (8 * LANES)) * 8
    return jnp.pad(flat, (0, rows * LANES - flat.shape[0])).reshape(rows, LANES)


def _pack_groups(shapes, names):
    groups = {}
    for n in names:
        groups.setdefault(shapes[n][1], []).append(n)
    return groups


def _pad_rows(a, mult):
    return jnp.pad(a, ((0, -a.shape[0] % mult), (0, 0)))


def _pack_grads(grads, shapes, names):
    packs = []
    for width, group in _pack_groups(shapes, names).items():
        per_chip = [jnp.concatenate([_pad_rows(_shard(grads[n], k, SHARD_AXIS[n]).astype(BF16), PACK_ROW_ALIGN) for n in group])
                    for k in range(N_CHIPS)]
        rows = per_chip[0].shape[0]
        packs.append(jnp.stack(per_chip).reshape(N_CHIPS, 2, rows // 2, width))
    return packs


def _unpack_grads(fulls, shapes, names):
    out = {}
    for full, group in zip(fulls, _pack_groups(shapes, names).values()):
        flat, at = full.reshape(-1, full.shape[-1]), 0
        for n in group:
            rows = shapes[n][0]
            out[n] = flat[at:at + rows]
            at += rows + (-rows % PACK_ROW_ALIGN)
    return out


def _unpack(flat, names, shapes):
    out, at = {}, 0
    for n in names:
        size = math.prod(shapes[n])
        out[n] = flat[at:at + size].reshape(shapes[n])
        at += size
    return out, at


def kernel(x, mem, positions, norm_mix_g, w_in, conv_w, conv_b, conv_ln_g, conv_ln_b, w_conv_out, q_norm_g, w_uq, kv_norm_g, w_ukv, w_mla_out, w_out, norm_xattn_g, norm_mem_g, w_xq, w_xkv, w_xo, norm_mlp_g, w_mlp1, w_mlp2, final_norm_g, loss_target, m_norm_mix_g, m_w_in, m_conv_w, m_conv_b, m_conv_ln_g, m_conv_ln_b, m_w_conv_out, m_q_norm_g, m_w_uq, m_kv_norm_g, m_w_ukv, m_w_mla_out, m_w_out, m_norm_xattn_g, m_norm_mem_g, m_w_xq, m_w_xkv, m_w_xo, m_norm_mlp_g, m_w_mlp1, m_w_mlp2, m_final_norm_g, v_norm_mix_g, v_w_in, v_conv_w, v_conv_b, v_conv_ln_g, v_conv_ln_b, v_w_conv_out, v_q_norm_g, v_w_uq, v_kv_norm_g, v_w_ukv, v_w_mla_out, v_w_out, v_norm_xattn_g, v_norm_mem_g, v_w_xq, v_w_xkv, v_w_xo, v_norm_mlp_g, v_w_mlp1, v_w_mlp2, v_final_norm_g):
    args = dict(locals())
    w = {n: args[n] for n in WEIGHTS}
    m = {n: args["m_" + n] for n in WEIGHTS}
    v = {n: args["v_" + n] for n in WEIGHTS}

    shards = [w[n][0].astype(F32 if n == "conv_w" else BF16) for n in BIG]
    bounds = (0,) + WEIGHT_WAITS + (len(BIG),)
    spans = [slice(lo, hi) for lo, hi in zip(bounds[:-1], bounds[1:])]
    sem_pairs, shards_thru, lands_thru, token = _gather_start(shards, [list(range(len(BIG)))[sp] for sp in spans])

    def unshard(n, g):
        ax = SHARD_AXIS[n]
        return jnp.moveaxis(g, 0, ax).reshape(g.shape[1:1 + ax] + (N_CHIPS * g.shape[1 + ax],) + g.shape[2 + ax:])

    def wait_fn(i):
        def fn(after):
            lands = _gather_wait(f"gather_weights_wait_{i}", sem_pairs[i], shards_thru[spans[i]], lands_thru[spans[i]], after)
            full = {n: unshard(n, g) for n, g in zip(BIG[spans[i]], lands)}
            return (_layout_w_in(full["w_in"]), full["conv_w"]) if i == 0 else _layout_weights(full)
        return fn

    sm = {n: w[n].reshape(1, -1) for n in SMALL}
    sm["norm_mix_g"] = sm["norm_mix_g"] + token[0, 0]

    shapes = {n: w[n].shape[1:] if n in BIG else w[n].shape for n in WEIGHTS}
    late_names = [n for n in BIG if n not in EARLY_GRADS]
    inflight = {}

    def send_grads(tag, names):
        def fn(g):
            packs = _pack_grads(_unlayout_grads(g), shapes, names)
            got = _pair_exchange(f"grad_pair_exchange_{tag}", packs)
            pairs = [_pair_add(f"grad_pair_add_{tag}_{i}", p, r) for i, (p, r) in enumerate(zip(packs, got))]
            *inflight[tag], token = _chip_exchange_start(f"grad_chip_exchange_{tag}_start", pairs)
            return token
        return fn

    loss, grad_x, g_small = _local_step(x[0], mem[0], positions, loss_target[0], [wait_fn(i) for i in range(3)],
                                        send_grads("early", EARLY_GRADS), send_grads("late", late_names), sm)

    halves, counts = [], {}
    for tag in ("late", "early"):
        own, got = _chip_exchange_wait(f"grad_chip_exchange_{tag}_wait", *inflight[tag], grad_x)
        halves += [_chip_add(f"grad_chip_add_{tag}_{i}", p, g) for i, (p, g) in enumerate(zip(own, got))]
        counts[tag] = len(own)
    fulls = _pair_share(halves)
    g_sum = _unpack_grads(fulls[:counts["late"]], shapes, late_names)
    g_sum.update(_unpack_grads(fulls[counts["late"]:], shapes, EARLY_GRADS))
    small_flat = _sum_over_devices(_pack_small(g_small, loss)).reshape(-1)
    g_small, at = _unpack(small_flat, SMALL, shapes)
    g_sum.update(g_small)
    loss_sum = small_flat[at]

    out_g, out_d, out_m, out_v = [], [], [], []
    for n in WEIGHTS:
        g, d, nm, nv = _adamw("adamw_" + n, w[n], g_sum[n], m[n], v[n])
        out_g.append(g)
        out_d.append(d)
        out_m.append(nm)
        out_v.append(nv)
    return (loss_sum, grad_x[None], *out_g, *out_d, *out_m, *out_v)
```

```python
import math

import jax
import jax.numpy as jnp
from jax import lax
from jax.experimental import pallas as pl
from jax.experimental.pallas import tpu as pltpu

F32 = jnp.float32
BF16 = jnp.bfloat16
MESH = pl.DeviceIdType.MESH

D_MODEL = 1024
CONV_CH = 512
CONV_WIDTH = 31
MLA_HEADS = 8
QK_NOPE = 64
QK_ROPE = 32
V_DIM = 64
Q_LORA = 384
KV_LORA = 256
MEM_LEN = 256
X_HEADS = 4
X_HEAD_DIM = 128
D_FF = 4096
ROPE_THETA = 10000.0
EPS = 1e-6
HEAD_PAD = 128
STAT_COL_QK = QK_NOPE + QK_ROPE
STAT_COL_V = V_DIM
HALO = 32
N_CHIPS = 4
LANES = 128

ADAM_LR = 0.001
ADAM_B1 = 0.9
ADAM_B2 = 0.999
ADAM_EPS = 1e-08
ADAM_WD = 0.01
ADAM_STEP = 10

VMEM_LIMIT = 52 * 1024 * 1024
ROW_TILES = (1024, 512, 256, 128, 64, 32, 16)
PACK_ROW_ALIGN = 32
N_DEV = 8
NEG = -1e30

BIG = ["w_in", "conv_w", "w_conv_out", "w_uq", "w_ukv", "w_mla_out", "w_out", "w_xq", "w_xkv", "w_xo", "w_mlp1", "w_mlp2"]
WEIGHT_WAITS = (2, 5)
SHARD_AXIS = {"w_in": 1, "w_conv_out": 1, "w_uq": 1, "w_ukv": 1, "w_mla_out": 1, "w_out": 0, "w_xq": 0, "w_xkv": 0,
              "w_xo": 1, "w_mlp1": 1, "w_mlp2": 0, "conv_w": 1}
EARLY_GRADS = ["w_mlp1", "w_mlp2", "w_xkv", "w_xq", "w_xo"]
SMALL = ["norm_mix_g", "conv_b", "conv_ln_g", "conv_ln_b", "q_norm_g", "kv_norm_g", "norm_xattn_g", "norm_mem_g",
         "norm_mlp_g", "final_norm_g"]
WEIGHTS = ["norm_mix_g", "w_in", "conv_w", "conv_b", "conv_ln_g", "conv_ln_b", "w_conv_out", "q_norm_g", "w_uq",
           "kv_norm_g", "w_ukv", "w_mla_out", "w_out", "norm_xattn_g", "norm_mem_g", "w_xq", "w_xkv", "w_xo",
           "norm_mlp_g", "w_mlp1", "w_mlp2", "final_norm_g"]


def _pick(n, prefs):
    for p in prefs:
        if n % p == 0:
            return p
    return n


def _params(sem):
    return pltpu.CompilerParams(dimension_semantics=sem, vmem_limit_bytes=VMEM_LIMIT)


_DIMS = {"nn": (((1,), (0,)), ((), ())), "nt": (((1,), (1,)), ((), ())), "tn": (((0,), (0,)), ((), ()))}


def _mm(name, a, b, mode, outs, epi, row_x=(), tile_x=(), vec_x=(), sums=(), tm=None, tn=None, tk=None, b_cols=None):
    if mode == "nn":
        (M, K), (_, N) = a.shape, b.shape
        if b_cols is not None:
            N = b_cols[1]
    elif mode == "nt":
        (M, K), (N, _) = a.shape, b.shape
    else:
        (K, M), (_, N) = a.shape, b.shape
    tm = tm or _pick(M, (1024, 512, 384, 256, 128))
    tn = tn or _pick(N, (1024, 768, 512, 384, 256, 128))
    tk = tk or _pick(K, (2048, 1920, 1024, 768, 512, 384, 256, 128))
    nk = K // tk
    rows_inner = nk == 1 and N // tn > 1
    grid = (N // tn, M // tm, nk) if rows_inner else (M // tm, N // tn, nk)

    def spec(shape, f):
        return pl.BlockSpec(shape, (lambda j, i, k: f(i, j, k)) if rows_inner else f)

    b_off = 0
    if b_cols is not None:
        assert mode == "nn" and b_cols[0] % tn == 0, (name, b_cols, tn)
        b_off = b_cols[0] // tn
    a_spec = spec((tk, tm), lambda i, j, k: (k, i)) if mode == "tn" else spec((tm, tk), lambda i, j, k: (i, k))
    b_spec = spec((tn, tk), lambda i, j, k: (j, k)) if mode == "nt" else spec((tk, tn), lambda i, j, k: (k, j + b_off))
    in_specs = [a_spec, b_spec]
    in_specs += [spec((tm, r.shape[1]), lambda i, j, k: (i, 0)) for r in row_x]
    in_specs += [spec((tm, tn), lambda i, j, k: (i, j)) for _ in tile_x]
    in_specs += [spec(v.shape, lambda i, j, k: (0, 0)) for v in vec_x]
    out_specs, out_shape = [], []
    for w, dt in outs:
        if tn == N:
            out_specs.append(spec((tm, w), lambda i, j, k: (i, 0)))
        else:
            assert w == N, (name, w, N)
            out_specs.append(spec((tm, tn), lambda i, j, k: (i, j)))
        out_shape.append(jax.ShapeDtypeStruct((M, w), dt))
    for shp in sums:
        assert tn == N and not rows_inner, name
        out_specs.append(spec(shp, lambda i, j, k: (0, 0)))
        out_shape.append(jax.ShapeDtypeStruct(shp, F32))
    nx = len(row_x) + len(tile_x) + len(vec_x)
    dims = _DIMS[mode]

    def body(a_ref, b_ref, *rest):
        x_refs, out_refs, acc_ref = rest[:nx], rest[nx:nx + len(outs) + len(sums)], rest[-1]
        av, bv = a_ref[...], b_ref[...]
        if av.dtype != BF16:
            av = av.astype(BF16)
        if bv.dtype != BF16:
            bv = bv.astype(BF16)
        prod = lax.dot_general(av, bv, dims, preferred_element_type=F32)
        if nk == 1:
            acc_ref[...] = prod
            epi(acc_ref, x_refs, out_refs)
        else:
            k = pl.program_id(2)

            @pl.when(k == 0)
            def _():
                acc_ref[...] = prod

            @pl.when(k > 0)
            def _():
                acc_ref[...] += prod

            @pl.when(k == nk - 1)
            def _():
                epi(acc_ref, x_refs, out_refs)

    res = pl.pallas_call(
        body, name=name, grid=grid, in_specs=in_specs, out_specs=out_specs, out_shape=out_shape,
        scratch_shapes=[pltpu.VMEM((tm, tn), F32)],
        compiler_params=_params(("arbitrary",) * 3 if sums else ("parallel", "parallel", "arbitrary")),
    )(a, b, *row_x, *tile_x, *vec_x)
    return res


def _epi_store(acc_ref, x_refs, out_refs):
    for o in out_refs:
        o[...] = acc_ref[...].astype(o.dtype)


def _mm_plain(name, a, b, mode, dtype=F32, **kw):
    n = kw["b_cols"][1] if kw.get("b_cols") else (b.shape[0] if mode == "nt" else b.shape[1])
    return _mm(name, a, b, mode, [(n, dtype)], _epi_store, **kw)[0]


def _rows(name, body, row_ins, vec_ins, row_outs, acc_outs=(), tile=512, window=None):
    S = row_ins[0].shape[0]
    t = _pick(S, (tile, 256, 128, 64, 32, 16, 8))
    in_specs = [pl.BlockSpec((t, r.shape[1]), lambda i: (i, 0)) for r in row_ins]
    in_specs += [pl.BlockSpec(v.shape, lambda i: (0, 0)) for v in vec_ins]
    out_specs = [pl.BlockSpec((t, w), lambda i: (i, 0)) for w, _ in row_outs]
    out_specs += [pl.BlockSpec(shp, lambda i: (0, 0)) for shp in acc_outs]
    out_shape = [jax.ShapeDtypeStruct((S, w), dt) for w, dt in row_outs]
    out_shape += [jax.ShapeDtypeStruct(shp, F32) for shp in acc_outs]
    extra, aliases, n_in = [], {}, len(row_ins) + len(vec_ins)
    if window is not None:
        k, (first, width), total, buffer = window
        assert first % width == 0 and row_outs[k][0] == width, (name, window[:3])
        out_specs[k] = pl.BlockSpec((t, width), lambda i: (i, first // width))
        out_shape[k] = jax.ShapeDtypeStruct((S, total), row_outs[k][1])
        if buffer is not None:
            extra, aliases = [buffer], {n_in: k}
            in_specs.append(pl.BlockSpec(memory_space=pl.ANY))

    def call(*refs):
        body(*refs[:n_in], *refs[n_in + len(extra):])

    sem = ("arbitrary",) if acc_outs else ("parallel",)
    return pl.pallas_call(
        call, name=name, grid=(S // t,), in_specs=in_specs, out_specs=out_specs,
        out_shape=out_shape, input_output_aliases=aliases, compiler_params=_params(sem),
    )(*row_ins, *vec_ins, *extra)


def _accum(ref, val, first=True):
    if first:
        @pl.when(pl.program_id(0) == 0)
        def _():
            ref[...] = jnp.zeros_like(ref)

    ref[...] += val


EPILOGUE_ROWS = 256


def _row_chunks(n):
    step = min(EPILOGUE_ROWS, n)
    return [slice(r, r + step) for r in range(0, n, step)]


def _colsum(v):
    return jnp.sum(v, axis=0, keepdims=True)


def _rms_fwd(x, g):
    r = lax.rsqrt(jnp.mean(x * x, axis=-1, keepdims=True) + EPS)
    return x * r * g


def _rms_bwd(x, g, du):
    r = lax.rsqrt(jnp.mean(x * x, axis=-1, keepdims=True) + EPS)
    xn = x * r
    gdu = du * g
    dx = r * (gdu - xn * jnp.mean(xn * gdu, axis=-1, keepdims=True))
    return dx, _colsum(du * xn)


def _sigmoid(v):
    return 1.0 / (1.0 + jnp.exp(-v))


def _rope(v, c, sa, sb, sign):
    return v * c + sign * (pltpu.roll(v, HEAD_PAD - QK_ROPE // 2, 1) * sa + pltpu.roll(v, QK_ROPE // 2, 1) * sb)


def _split3(v):
    hi = v.astype(BF16)
    r1 = v - hi.astype(F32)
    mid = r1.astype(BF16)
    lo = (r1 - mid.astype(F32)).astype(BF16)
    return hi, mid, lo


def _put_stats(base, stat, col):
    hi, mid, lo = _split3(stat)
    lane = lax.broadcasted_iota(jnp.int32, base.shape, 1)
    out = jnp.where(lane == col, hi, base)
    out = jnp.where(lane == col + 1, mid, out)
    return jnp.where(lane == col + 2, lo, out)


def _neg_ones(shape, col):
    lane = lax.broadcasted_iota(jnp.int32, shape, 1)
    return jnp.where((lane >= col) & (lane < col + 3), -1.0, 0.0).astype(F32)


def _shifted(ext, t):
    p = ext.shape[0]
    for b in range(8):
        rb = ext if b == 0 else pltpu.roll(ext, p - b, 0)
        for a in range(HALO // 8 + 1):
            if 8 * a + b <= HALO:
                yield 8 * a + b, rb[8 * a:8 * a + t]


def _conv_fwd(z0, conv_w, conv_b, ln_g, ln_b):
    S, C = z0.shape
    t = _pick(S, (512, 256, 128, 64, 32))
    per = t // HALO

    def body(cur_ref, prev_ref, w_ref, b_ref, g_ref, beta_ref, z1_ref, z3_ref, ext_ref):
        i = pl.program_id(0)
        ext_ref[0:HALO, :] = jnp.where(i > 0, prev_ref[...], 0.0)
        ext_ref[HALO:, :] = cur_ref[...]
        ext = ext_ref[...]
        acc = jnp.zeros((t, C), F32)
        for d, win in _shifted(ext, t):
            k = d - (HALO - CONV_WIDTH + 1)
            if 0 <= k < CONV_WIDTH:
                acc = acc + win * w_ref[k:k + 1, :]
        z1 = acc + b_ref[...]
        z1_ref[...] = z1
        mu = jnp.mean(z1, axis=-1, keepdims=True)
        zc = z1 - mu
        rs = lax.rsqrt(jnp.mean(zc * zc, axis=-1, keepdims=True) + EPS)
        z2 = zc * rs * g_ref[...] + beta_ref[...]
        z3_ref[...] = (z2 * _sigmoid(z2)).astype(BF16)

    vec = lambda v: pl.BlockSpec(v.shape, lambda i: (0, 0))
    return pl.pallas_call(
        body, name="conv_fwd", grid=(S // t,),
        in_specs=[pl.BlockSpec((t, C), lambda i: (i, 0)),
                  pl.BlockSpec((HALO, C), lambda i: (jnp.maximum(i * per - 1, 0), 0)),
                  vec(conv_w), vec(conv_b), vec(ln_g), vec(ln_b)],
        out_specs=[pl.BlockSpec((t, C), lambda i: (i, 0)), pl.BlockSpec((t, C), lambda i: (i, 0))],
        out_shape=[jax.ShapeDtypeStruct((S, C), F32), jax.ShapeDtypeStruct((S, C), BF16)],
        scratch_shapes=[pltpu.VMEM((t + HALO, C), F32)],
        compiler_params=_params(("parallel",)),
    )(z0, z0, conv_w, conv_b, ln_g, ln_b)


def _conv_bwd_norm(dz3, z1, ln_g, ln_b):
    C = z1.shape[1]

    def body(dz3_ref, z1_ref, g_ref, beta_ref, dz1_ref, dg_ref, dbeta_ref, dbias_ref):
        z1 = z1_ref[...]
        mu = jnp.mean(z1, axis=-1, keepdims=True)
        zc = z1 - mu
        rs = lax.rsqrt(jnp.mean(zc * zc, axis=-1, keepdims=True) + EPS)
        xh = zc * rs
        z2 = xh * g_ref[...] + beta_ref[...]
        sg = _sigmoid(z2)
        dz2 = dz3_ref[...].astype(F32) * (sg * (1.0 + z2 * (1.0 - sg)))
        dxh = dz2 * g_ref[...]
        dz1 = rs * (dxh - jnp.mean(dxh, axis=-1, keepdims=True) - xh * jnp.mean(dxh * xh, axis=-1, keepdims=True))
        dz1_ref[...] = dz1
        _accum(dg_ref, _colsum(dz2 * xh))
        _accum(dbeta_ref, _colsum(dz2))
        _accum(dbias_ref, _colsum(dz1))

    return _rows("conv_bwd_norm", body, [dz3, z1], [ln_g, ln_b], [(C, F32)], [(1, C)] * 3)


def _conv_bwd_taps(dz1, z0, conv_in, conv_w, dproj, window):
    S, C = z0.shape
    t = _pick(S, (512, 256, 128, 64, 32))
    per = t // HALO
    last = S // HALO - 1
    nt = S // t
    assert window[1] == 2 * C and window[0] % window[1] == 0, window

    def body(dcur_ref, dnext_ref, zcur_ref, zprev_ref, cin_ref, w_ref, _, dcin_ref, dw_ref, dext_ref, zext_ref):
        i = pl.program_id(0)
        dcur = dcur_ref[...]
        dext_ref[0:t, :] = dcur
        dext_ref[t:, :] = jnp.where(i < nt - 1, dnext_ref[...], 0.0)
        zext_ref[0:HALO, :] = jnp.where(i > 0, zprev_ref[...], 0.0)
        zext_ref[HALO:, :] = zcur_ref[...]

        @pl.when(i == 0)
        def _():
            dw_ref[...] = jnp.zeros_like(dw_ref)

        dz0 = jnp.zeros((t, C), F32)
        for d, win in _shifted(dext_ref[...], t):
            k = CONV_WIDTH - 1 - d
            if 0 <= k < CONV_WIDTH:
                dz0 = dz0 + win * w_ref[k:k + 1, :]
        for d, win in _shifted(zext_ref[...], t):
            k = d - (HALO - CONV_WIDTH + 1)
            if 0 <= k < CONV_WIDTH:
                dw_ref[k:k + 1, :] += _colsum(dcur * win)
        a = cin_ref[:, 0:C].astype(F32)
        sg = _sigmoid(cin_ref[:, C:2 * C].astype(F32))
        dcin_ref[:, 0:C] = (dz0 * sg).astype(BF16)
        dcin_ref[:, C:2 * C] = (dz0 * a * sg * (1.0 - sg)).astype(BF16)

    return pl.pallas_call(
        body, name="conv_bwd_taps", grid=(nt,),
        in_specs=[pl.BlockSpec((t, C), lambda i: (i, 0)),
                  pl.BlockSpec((HALO, C), lambda i: (jnp.minimum((i + 1) * per, last), 0)),
                  pl.BlockSpec((t, C), lambda i: (i, 0)),
                  pl.BlockSpec((HALO, C), lambda i: (jnp.maximum(i * per - 1, 0), 0)),
                  pl.BlockSpec((t, 2 * C), lambda i: (i, 0)),
                  pl.BlockSpec(conv_w.shape, lambda i: (0, 0)),
                  pl.BlockSpec(memory_space=pl.ANY)],
        out_specs=[pl.BlockSpec((t, 2 * C), lambda i: (i, window[0] // window[1])), pl.BlockSpec((HALO, C), lambda i: (0, 0))],
        out_shape=[jax.ShapeDtypeStruct(dproj.shape, BF16), jax.ShapeDtypeStruct((HALO, C), F32)],
        input_output_aliases={6: 0},
        scratch_shapes=[pltpu.VMEM((t + HALO, C), F32), pltpu.VMEM((t + HALO, C), F32)],
        compiler_params=_params(("arbitrary",)),
    )(dz1, dz1, z0, z0, conv_in, conv_w, dproj)


def _lower_tri(shape, rows_are_queries):
    row = lax.broadcasted_iota(jnp.int32, shape, 0)
    col = lax.broadcasted_iota(jnp.int32, shape, 1)
    return (col <= row) if rows_are_queries else (row <= col)


HEADS_PER_STEP = 2
FWD_HEADS_PER_STEP = 2
FWD_KEY_TILES = 8


def _flash_specs(S, t, heads):
    w = heads * HEAD_PAD
    blk = pl.BlockSpec((t, w), lambda h, i: (i, h))
    head = pl.BlockSpec((S, w), lambda h, i: (0, h))
    return blk, head


def _head_lanes(g):
    return slice(g * HEAD_PAD, (g + 1) * HEAD_PAD)


def _dot_nt(a, b):
    return lax.dot_general(a, b, _DIMS["nt"], preferred_element_type=F32)


def _dot_nn(a, b):
    return lax.dot_general(a, b, _DIMS["nn"], preferred_element_type=F32)


def _dot_tn(a, b):
    return lax.dot_general(a, b, _DIMS["tn"], preferred_element_type=F32)


def _flash_fwd(q, k, v):
    S = q.shape[0]
    t = _pick(S, (512, 256, 128))

    def body(q_ref, k_ref, v_ref, o_ref, qa_ref, m_ref, acc_ref):
        qi = pl.program_id(1)
        m_ref[...] = jnp.full_like(m_ref, NEG)
        acc_ref[...] = jnp.zeros_like(acc_ref)

        def step(first, tiles, diag):
            width = tiles * t
            rows = pl.ds(pl.multiple_of(first, t), width)
            for g in range(FWD_HEADS_PER_STEP):
                hl = _head_lanes(g)
                s = _dot_nt(q_ref[:, hl], k_ref[rows, hl])
                if diag:
                    row = lax.broadcasted_iota(jnp.int32, s.shape, 0)
                    col = lax.broadcasted_iota(jnp.int32, s.shape, 1)
                    s = jnp.where(col <= row + (tiles - 1) * t, s, NEG)
                m_old = m_ref[g]
                m_new = jnp.maximum(m_old, jnp.max(s, axis=-1, keepdims=True))
                p = jnp.exp(s - m_new).astype(BF16)
                acc_ref[g] = jnp.exp(m_old - m_new) * acc_ref[g] + _dot_nn(p, v_ref[rows, hl])
                m_ref[g] = m_new

        def wide(kb, carry):
            step(kb * (FWD_KEY_TILES * t), FWD_KEY_TILES, False)
            return carry

        full_groups = qi // FWD_KEY_TILES
        lax.fori_loop(0, full_groups, wide, 0)
        for tiles in range(1, min(FWD_KEY_TILES, S // t) + 1):
            @pl.when(qi - full_groups * FWD_KEY_TILES == tiles - 1)
            def _():
                step(full_groups * (FWD_KEY_TILES * t), tiles, True)

        for g in range(FWD_HEADS_PER_STEP):
            hl = _head_lanes(g)
            acc = acc_ref[g]
            l = -acc[:, STAT_COL_V:STAT_COL_V + 1]
            o_ref[:, hl] = (acc / l).astype(BF16)
            qa_ref[:, hl] = _put_stats(q_ref[:, hl], m_ref[g] + jnp.log(l), STAT_COL_QK)

    blk, head = _flash_specs(S, t, FWD_HEADS_PER_STEP)
    return pl.pallas_call(
        body, name="mla_flash_fwd", grid=(MLA_HEADS // FWD_HEADS_PER_STEP, S // t),
        in_specs=[blk, head, head], out_specs=[blk, blk],
        out_shape=[jax.ShapeDtypeStruct(q.shape, BF16), jax.ShapeDtypeStruct(q.shape, BF16)],
        scratch_shapes=[pltpu.VMEM((FWD_HEADS_PER_STEP, t, 1), F32), pltpu.VMEM((FWD_HEADS_PER_STEP, t, HEAD_PAD), F32)],
        compiler_params=_params(("parallel", "arbitrary")),
    )(q, k, v)


def _flash_bwd(qa, k, v, doa):
    S = qa.shape[0]
    t = _pick(S, (1024, 512, 256, 128))
    n = S // t
    half = t // 2

    def body(qa_ref, k_ref, v_ref, do_ref, dq_ref, dk_ref, dv_ref, dk_acc, dv_acc):
        kj = pl.program_id(1)

        @pl.when(kj == 0)
        def _():
            dq_ref[...] = jnp.zeros_like(dq_ref)

        dk_acc[...] = jnp.zeros_like(dk_acc)
        dv_acc[...] = jnp.zeros_like(dv_acc)

        def step(q_first, q_len, keys, diag):
            rows = pl.ds(pl.multiple_of(q_first, q_len), q_len)
            for g in range(HEADS_PER_STEP):
                hl = _head_lanes(g)
                qa, do, kk = qa_ref[rows, hl], do_ref[rows, hl], k_ref[keys, hl]
                st = _dot_nt(kk, qa)
                if diag:
                    st = jnp.where(_lower_tri(st.shape, False), st, NEG)
                pt = jnp.exp(st)
                dst = (pt * _dot_nt(v_ref[keys, hl], do)).astype(BF16)
                dv_acc[keys, hl] += _dot_nn(pt.astype(BF16), do)
                dk_acc[keys, hl] += _dot_nn(dst, qa)
                dq_ref[rows, hl] += _dot_tn(dst, kk)

        def loop(qi, carry):
            step(qi * t, t, slice(0, t), False)
            return carry

        lo, hi = slice(0, half), slice(half, t)
        step(kj * t, half, lo, True)
        step(kj * t + half, half, lo, False)
        step(kj * t + half, half, hi, True)
        lax.fori_loop(kj + 1, n, loop, 0)
        dk_ref[...] = dk_acc[...].astype(BF16)
        dv_ref[...] = dv_acc[...].astype(BF16)

    blk, head = _flash_specs(S, t, HEADS_PER_STEP)
    w = HEADS_PER_STEP * HEAD_PAD
    return pl.pallas_call(
        body, name="mla_flash_bwd", grid=(MLA_HEADS // HEADS_PER_STEP, n),
        in_specs=[head, blk, blk, head], out_specs=[head, blk, blk],
        out_shape=[jax.ShapeDtypeStruct(qa.shape, F32), jax.ShapeDtypeStruct(qa.shape, BF16), jax.ShapeDtypeStruct(qa.shape, BF16)],
        scratch_shapes=[pltpu.VMEM((t, w), F32), pltpu.VMEM((t, w), F32)],
        compiler_params=_params(("parallel", "arbitrary")),
    )(qa, k, v, doa)


def _xattn_fwd(xq, kvx):
    W = X_HEADS * X_HEAD_DIM

    def body(q_ref, kv_ref, o_ref):
        for h in range(X_HEADS):
            lo = h * X_HEAD_DIM
            s = _dot_nt(q_ref[:, lo:lo + X_HEAD_DIM], kv_ref[:, lo:lo + X_HEAD_DIM])
            p = jnp.exp(s - jnp.max(s, axis=-1, keepdims=True))
            p = p / jnp.sum(p, axis=-1, keepdims=True)
            o_ref[:, lo:lo + X_HEAD_DIM] = _dot_nn(p.astype(BF16), kv_ref[:, W + lo:W + lo + X_HEAD_DIM]).astype(BF16)

    return _rows("xattn_fwd", body, [xq], [kvx], [(W, BF16)])[0]


def _xattn_bwd(xq, kvx, dox):
    W = X_HEADS * X_HEAD_DIM
    scale = X_HEAD_DIM ** -0.5

    def body(q_ref, do_ref, kv_ref, dq_ref, dkv_ref):
        @pl.when(pl.program_id(0) == 0)
        def _():
            dkv_ref[...] = jnp.zeros_like(dkv_ref)

        for h in range(X_HEADS):
            lo = h * X_HEAD_DIM
            q, k = q_ref[:, lo:lo + X_HEAD_DIM], kv_ref[:, lo:lo + X_HEAD_DIM]
            v, do = kv_ref[:, W + lo:W + lo + X_HEAD_DIM], do_ref[:, lo:lo + X_HEAD_DIM]
            s = _dot_nt(q, k)
            p = jnp.exp(s - jnp.max(s, axis=-1, keepdims=True))
            p = p / jnp.sum(p, axis=-1, keepdims=True)
            dp = _dot_nt(do, v)
            ds = (p * (dp - jnp.sum(dp * p, axis=-1, keepdims=True))).astype(BF16)
            dq_ref[:, lo:lo + X_HEAD_DIM] = (_dot_nn(ds, k) * scale).astype(BF16)
            dkv_ref[:, lo:lo + X_HEAD_DIM] += _dot_tn(ds, q)
            dkv_ref[:, W + lo:W + lo + X_HEAD_DIM] += _dot_tn(p.astype(BF16), do)

    return _rows("xattn_bwd", body, [xq, dox], [kvx], [(W, BF16)], [kvx.shape])


def _adamw(name, w, g, m, v):
    c1 = 1.0 / (1.0 - ADAM_B1 ** ADAM_STEP)
    c2 = 1.0 / (1.0 - ADAM_B2 ** ADAM_STEP)
    lead = (0,) * (w.ndim - 2)
    w2 = w.reshape((1,) * (2 - w.ndim) + w.shape) if w.ndim < 2 else w
    m2, v2 = m.reshape(w2.shape), v.reshape(w2.shape)
    g2 = g.reshape(w2.shape[-2:])
    R, C = g2.shape
    t = _pick(R, (256, 128, 64, 32, 16, 8))

    def body(w_ref, g_ref, m_ref, v_ref, go_ref, d_ref, nm_ref, nv_ref):
        g = g_ref[...]
        nm = ADAM_B1 * m_ref[lead] + (1.0 - ADAM_B1) * g
        nv = ADAM_B2 * v_ref[lead] + (1.0 - ADAM_B2) * (g * g)
        go_ref[lead] = g
        d_ref[lead] = -ADAM_LR * ((nm * c1) / (jnp.sqrt(nv * c2) + ADAM_EPS) + ADAM_WD * w_ref[lead])
        nm_ref[lead] = nm
        nv_ref[lead] = nv

    full = pl.BlockSpec((1,) * len(lead) + (t, C), lambda i: lead + (i, 0))
    outs = pl.pallas_call(
        body, name=name, grid=(R // t,), in_specs=[full, pl.BlockSpec((t, C), lambda i: (i, 0)), full, full],
        out_specs=[full] * 4, out_shape=[jax.ShapeDtypeStruct(w2.shape, F32)] * 4, compiler_params=_params(("parallel",)),
    )(w2, g2, m2, v2)
    return [o.reshape(w.shape) for o in outs]


def _place():
    x, y, c = lax.axis_index("x"), lax.axis_index("y"), lax.axis_index("c")
    return x, y, c, [(1 - x, y), (x, 1 - y), (1 - x, 1 - y)]


_ANY = pl.BlockSpec(memory_space=pl.ANY)


_HBM = pl.BlockSpec(memory_space=pltpu.HBM)
_SEM = pl.BlockSpec(memory_space=pltpu.SEMAPHORE)
_SIDE_EFFECT = pltpu.SideEffectType.DATAFLOW_SIDE_EFFECTING


def _gather_copy(src, land, slot, send, recv, k, chip, c):
    return pltpu.make_async_remote_copy(src_ref=src, dst_ref=land.at[slot], send_sem=send.at[k], recv_sem=recv.at[k],
                                        device_id=(chip[0], chip[1], c), device_id_type=MESH)


def _gather_start(shards, groups):
    n, ng = len(shards), len(groups)
    mine = 2 * lax.axis_index("x") + lax.axis_index("y")
    lands = [lax.dynamic_update_slice(lax.empty((N_CHIPS,) + s.shape, s.dtype), s[None], (mine,) + (0,) * s.ndim)
             for s in shards]

    def body(*refs):
        ins, lnd = refs[:n], refs[n:2 * n]
        sends, recvs = refs[2 * n:2 * n + ng], refs[2 * n + ng:2 * n + 2 * ng]
        token = refs[-1]
        x, y, c, chips = _place()
        for gi, group in enumerate(groups):
            for pos, w in enumerate(group):
                for j, chip in enumerate(chips):
                    _gather_copy(ins[w], lnd[w], 2 * x + y, sends[gi], recvs[gi], 3 * pos + j, chip, c).start()
        token[...] = jnp.zeros_like(token)

    sems = [pltpu.SemaphoreType.DMA((3 * len(g),)) for g in groups]
    res = pl.pallas_call(
        body, name="gather_weights_start",
        out_shape=sems + sems + [pltpu.HBM(a.shape, a.dtype) for a in shards + lands] + [jax.ShapeDtypeStruct((8, LANES), F32)],
        in_specs=[_HBM] * (2 * n),
        out_specs=[_SEM] * (2 * ng) + [_HBM] * (2 * n) + [pl.BlockSpec(memory_space=pltpu.VMEM)],
        input_output_aliases={i: 2 * ng + i for i in range(2 * n)},
        compiler_params=pltpu.CompilerParams(has_side_effects=_SIDE_EFFECT),
    )(*[pltpu.with_memory_space_constraint(a, pltpu.HBM) for a in shards + lands])
    sem_pairs = list(zip(res[:ng], res[ng:2 * ng]))
    return sem_pairs, res[2 * ng:2 * ng + n], res[2 * ng + n:2 * ng + 2 * n], res[-1]


def _gather_wait(name, sem_pair, shards_thru, lands_thru, after):
    m = len(shards_thru)
    after = after if isinstance(after, (tuple, list)) else (after,)

    def body(*refs):
        ins, lnd = refs[:m], refs[m:2 * m]
        send, recv = refs[2 * m], refs[2 * m + 1]
        x, y, c, chips = _place()
        for pos in range(m):
            for j, chip in enumerate(chips):
                cp = _gather_copy(ins[pos], lnd[pos], 2 * chip[0] + chip[1], send, recv, 3 * pos + j, chip, c)
                cp.wait_send()
                cp.wait_recv()

    res = pl.pallas_call(
        body, name=name,
        out_shape=[pltpu.HBM(a.shape, a.dtype) for a in list(shards_thru) + list(lands_thru)],
        in_specs=[_HBM] * (2 * m) + [_SEM, _SEM] + [_ANY] * len(after), out_specs=[_HBM] * (2 * m),
        input_output_aliases={i: i for i in range(2 * m)},
        compiler_params=pltpu.CompilerParams(has_side_effects=_SIDE_EFFECT),
    )(*shards_thru, *lands_thru, *sem_pair, *after)
    return res[m:]


def _pair_exchange(name, packs):
    n = len(packs)

    def body(*refs):
        ins, outs, send, recv = refs[:n], refs[n:2 * n], refs[2 * n], refs[2 * n + 1]
        x, y, c, _ = _place()
        cps = []
        for g in range(n):
            cp = pltpu.make_async_remote_copy(src_ref=ins[g].at[:, pl.ds(1 - c, 1)], dst_ref=outs[g], send_sem=send.at[g],
                                              recv_sem=recv.at[g], device_id=(x, y, 1 - c), device_id_type=MESH)
            cp.start()
            cps.append(cp)
        for cp in cps:
            cp.wait()

    return pl.pallas_call(
        body, name=name, in_specs=[_ANY] * n, out_specs=[_ANY] * n,
        out_shape=[jax.ShapeDtypeStruct((N_CHIPS, 1) + p.shape[2:], p.dtype) for p in packs],
        scratch_shapes=[pltpu.SemaphoreType.DMA((n,)), pltpu.SemaphoreType.DMA((n,))],
    )(*packs)


def _chip_copy(src, land, src_slot, dst_slot, send, recv, k, chip, c):
    return pltpu.make_async_remote_copy(src_ref=src.at[src_slot], dst_ref=land.at[dst_slot], send_sem=send.at[k],
                                        recv_sem=recv.at[k], device_id=(chip[0], chip[1], c), device_id_type=MESH)


def _chip_exchange_start(name, parts):
    n = len(parts)
    lands = [lax.empty(p.shape, p.dtype) for p in parts]

    def body(*refs):
        ins, lnd, send, recv, token = refs[:n], refs[n:2 * n], refs[2 * n], refs[2 * n + 1], refs[-1]
        x, y, c, chips = _place()
        for g in range(n):
            for j, chip in enumerate(chips):
                _chip_copy(ins[g], lnd[g], 2 * chip[0] + chip[1], 2 * x + y, send, recv, 3 * g + j, chip, c).start()
        token[...] = jnp.zeros_like(token)

    sems = [pltpu.SemaphoreType.DMA((3 * n,))] * 2
    res = pl.pallas_call(
        body, name=name,
        out_shape=sems + [pltpu.HBM(a.shape, a.dtype) for a in list(parts) + lands] + [jax.ShapeDtypeStruct((8, LANES), F32)],
        in_specs=[_HBM] * (2 * n),
        out_specs=[_SEM] * 2 + [_HBM] * (2 * n) + [pl.BlockSpec(memory_space=pltpu.VMEM)],
        input_output_aliases={i: 2 + i for i in range(2 * n)},
        compiler_params=pltpu.CompilerParams(has_side_effects=_SIDE_EFFECT),
    )(*[pltpu.with_memory_space_constraint(a, pltpu.HBM) for a in list(parts) + lands])
    return res[:2], res[2:2 + n], res[2 + n:2 + 2 * n], res[-1]


def _chip_exchange_wait(name, sems, parts_thru, lands_thru, after):
    n = len(parts_thru)

    def body(*refs):
        ins, lnd, send, recv = refs[:n], refs[n:2 * n], refs[2 * n], refs[2 * n + 1]
        x, y, c, chips = _place()
        for g in range(n):
            for j, chip in enumerate(chips):
                cp = _chip_copy(ins[g], lnd[g], 2 * x + y, 2 * chip[0] + chip[1], send, recv, 3 * g + j, chip, c)
                cp.wait_send()
                cp.wait_recv()

    res = pl.pallas_call(
        body, name=name,
        out_shape=[pltpu.HBM(a.shape, a.dtype) for a in list(parts_thru) + list(lands_thru)],
        in_specs=[_HBM] * (2 * n) + [_SEM, _SEM, _ANY], out_specs=[_HBM] * (2 * n),
        input_output_aliases={i: i for i in range(2 * n)},
        compiler_params=pltpu.CompilerParams(has_side_effects=_SIDE_EFFECT),
    )(*parts_thru, *lands_thru, *sems, after)
    return res[:n], res[n:]


def _pair_share(halves):
    n = len(halves)

    def body(*refs):
        outs, send, recv = refs[n:2 * n], refs[2 * n], refs[2 * n + 1]
        x, y, c, _ = _place()
        cps = []
        for g in range(n):
            cp = pltpu.make_async_remote_copy(src_ref=outs[g].at[c], dst_ref=outs[g].at[c], send_sem=send.at[g],
                                              recv_sem=recv.at[g], device_id=(x, y, 1 - c), device_id_type=MESH)
            cp.start()
            cps.append(cp)
        for g in range(n):
            pltpu.make_async_remote_copy(src_ref=outs[g].at[c], dst_ref=outs[g].at[1 - c], send_sem=send.at[g],
                                         recv_sem=recv.at[g], device_id=(x, y, 1 - c), device_id_type=MESH).wait_recv()
        for cp in cps:
            cp.wait_send()

    return pl.pallas_call(
        body, name="grad_pair_share", in_specs=[_ANY] * n, out_specs=[_ANY] * n,
        out_shape=[jax.ShapeDtypeStruct(h.shape, h.dtype) for h in halves],
        input_output_aliases={g: g for g in range(n)},
        scratch_shapes=[pltpu.SemaphoreType.DMA((n,)), pltpu.SemaphoreType.DMA((n,))],
    )(*halves)


def _sum_over_devices(block):
    R, L = block.shape

    def gather(b_ref, o_ref, send, recv, loc):
        x, y, c, _ = _place()
        lc = pltpu.make_async_copy(b_ref, o_ref.at[4 * x + 2 * y + c], loc)
        lc.start()
        peers = [(1 - x if dx else x, 1 - y if dy else y, 1 - c if dc else c)
                 for dx in (0, 1) for dy in (0, 1) for dc in (0, 1) if dx or dy or dc]
        cps = []
        for j, peer in enumerate(peers):
            cp = pltpu.make_async_remote_copy(src_ref=b_ref, dst_ref=o_ref.at[4 * x + 2 * y + c], send_sem=send.at[j],
                                              recv_sem=recv.at[j], device_id=peer, device_id_type=MESH)
            cp.start()
            cps.append(cp)
        for j, (px, py, pc) in enumerate(peers):
            pltpu.make_async_remote_copy(src_ref=b_ref, dst_ref=o_ref.at[4 * px + 2 * py + pc], send_sem=send.at[j],
                                         recv_sem=recv.at[j], device_id=(px, py, pc), device_id_type=MESH).wait_recv()
        for cp in cps:
            cp.wait_send()
        lc.wait()

    blocks = pl.pallas_call(
        gather, name="small_grads_gather", in_specs=[_ANY], out_specs=_ANY,
        out_shape=jax.ShapeDtypeStruct((N_DEV, R, L), F32),
        scratch_shapes=[pltpu.SemaphoreType.DMA((N_DEV - 1,)), pltpu.SemaphoreType.DMA((N_DEV - 1,)), pltpu.SemaphoreType.DMA],
    )(block)

    def add(b_ref, o_ref):
        total = b_ref[0]
        for d in range(1, N_DEV):
            total = total + b_ref[d]
        o_ref[...] = total

    return pl.pallas_call(add, name="small_grads_add", out_shape=jax.ShapeDtypeStruct((R, L), F32))(blocks)


def _pair_add(name, pack, got):
    _, _, R, C = pack.shape
    t = _pick(R, ROW_TILES)
    c = lax.axis_index("c").astype(jnp.int32).reshape(1)

    def body(c_ref, p_ref, g_ref, o_ref):
        o_ref[...] = (p_ref[...].astype(F32) + g_ref[...].astype(F32)).astype(BF16)

    return pl.pallas_call(
        body, name=name,
        grid_spec=pltpu.PrefetchScalarGridSpec(
            num_scalar_prefetch=1, grid=(N_CHIPS, R // t),
            in_specs=[pl.BlockSpec((None, None, t, C), lambda k, i, c_ref: (k, c_ref[0], i, 0)),
                      pl.BlockSpec((None, None, t, C), lambda k, i, c_ref: (k, 0, i, 0))],
            out_specs=pl.BlockSpec((None, t, C), lambda k, i, c_ref: (k, i, 0))),
        out_shape=jax.ShapeDtypeStruct((N_CHIPS, R, C), BF16), compiler_params=_params(("parallel", "parallel")),
    )(c, pack, got)


def _chip_add(name, own, got):
    _, R, C = own.shape
    t = _pick(R, ROW_TILES)
    x, y, c, _ = _place()
    place = jnp.stack([c, 2 * x + y]).astype(jnp.int32)

    def body(place_ref, own_ref, g1_ref, g2_ref, g3_ref, o_ref):
        o_ref[...] = ((own_ref[...].astype(F32) + g1_ref[...].astype(F32)) + g2_ref[...].astype(F32)) + g3_ref[...].astype(F32)

    def other(d):
        return pl.BlockSpec((None, t, C), lambda i, place_ref: ((place_ref[1] + d) % N_CHIPS, i, 0))

    return pl.pallas_call(
        body, name=name,
        grid_spec=pltpu.PrefetchScalarGridSpec(
            num_scalar_prefetch=1, grid=(R // t,),
            in_specs=[pl.BlockSpec((None, t, C), lambda i, place_ref: (place_ref[1], i, 0)), other(1), other(2), other(3)],
            out_specs=pl.BlockSpec((None, t, C), lambda i, place_ref: (place_ref[0], i, 0))),
        out_shape=jax.ShapeDtypeStruct((2, R, C), F32), compiler_params=_params(("parallel",)),
    )(place, own, got, got, got)


_CUT = (2 * CONV_CH, 2 * CONV_CH + Q_LORA, 2 * CONV_CH + Q_LORA + KV_LORA, 2 * CONV_CH + Q_LORA + KV_LORA + QK_ROPE)
W_IN_GATES = (0, 2 * D_MODEL)
W_IN_CONV = (W_IN_GATES[1], 2 * CONV_CH)
W_IN_CQ = (W_IN_CONV[0] + W_IN_CONV[1], Q_LORA)
W_IN_KR = (W_IN_CQ[0] + W_IN_CQ[1], HEAD_PAD)
W_IN_CKV = (W_IN_KR[0] + W_IN_KR[1], KV_LORA)
W_IN_LORA = (W_IN_CQ[0], Q_LORA + HEAD_PAD + KV_LORA)
W_IN_COLS = W_IN_CKV[0] + W_IN_CKV[1]


def _pad_last(a, n):
    return jnp.pad(a, [(0, 0)] * (a.ndim - 1) + [(0, n - a.shape[-1])])


def _layout_w_in(w_in):
    kr = jnp.pad(w_in[:, _CUT[2]:_CUT[3]], ((0, 0), (QK_NOPE, HEAD_PAD - QK_NOPE - QK_ROPE)))
    return jnp.concatenate([w_in[:, _CUT[3]:], w_in[:, :_CUT[0]], w_in[:, _CUT[0]:_CUT[1]], kr, w_in[:, _CUT[1]:_CUT[2]]], axis=1)


def _layout_weights(w):
    out = dict(w)
    if "w_uq" in w:
        out["w_uq"] = _pad_last(w["w_uq"].reshape(Q_LORA, MLA_HEADS, QK_NOPE + QK_ROPE), HEAD_PAD).reshape(Q_LORA, MLA_HEADS * HEAD_PAD)
    if "w_ukv" in w:
        ukv = w["w_ukv"].reshape(KV_LORA, MLA_HEADS, QK_NOPE + V_DIM)
        uk = _pad_last(ukv[:, :, :QK_NOPE], HEAD_PAD).reshape(KV_LORA, MLA_HEADS * HEAD_PAD)
        uv = _pad_last(ukv[:, :, QK_NOPE:], HEAD_PAD).reshape(KV_LORA, MLA_HEADS * HEAD_PAD)
        out["w_ukv"] = jnp.concatenate([uk, uv], axis=1)
    if "w_mla_out" in w:
        mo = jnp.pad(w["w_mla_out"].reshape(MLA_HEADS, V_DIM, D_MODEL), ((0, 0), (0, HEAD_PAD - V_DIM), (0, 0)))
        out["w_mla_out"] = mo.reshape(MLA_HEADS * HEAD_PAD, D_MODEL)
    return out


def _unlayout_grads(g):
    out = dict(g)
    if "w_in" in g:
        gi = g["w_in"]
        win = lambda w: gi[:, w[0]:w[0] + w[1]]
        kr = gi[:, W_IN_KR[0] + QK_NOPE:W_IN_KR[0] + QK_NOPE + QK_ROPE]
        out["w_in"] = jnp.concatenate([win(W_IN_CONV), win(W_IN_CQ), win(W_IN_CKV), kr, win(W_IN_GATES)], axis=1)
    if "w_uq" in g:
        out["w_uq"] = g["w_uq"].reshape(Q_LORA, MLA_HEADS, HEAD_PAD)[:, :, :QK_NOPE + QK_ROPE].reshape(Q_LORA, -1)
    if "w_ukv" in g:
        gk = g["w_ukv"][:, :MLA_HEADS * HEAD_PAD].reshape(KV_LORA, MLA_HEADS, HEAD_PAD)[:, :, :QK_NOPE]
        gv = g["w_ukv"][:, MLA_HEADS * HEAD_PAD:].reshape(KV_LORA, MLA_HEADS, HEAD_PAD)[:, :, :V_DIM]
        out["w_ukv"] = jnp.concatenate([gk, gv], axis=2).reshape(KV_LORA, -1)
    if "w_mla_out" in g:
        out["w_mla_out"] = g["w_mla_out"].reshape(MLA_HEADS, HEAD_PAD, D_MODEL)[:, :V_DIM].reshape(MLA_HEADS * V_DIM, D_MODEL)
    return out


def _rope_tables(positions):
    half = QK_ROPE // 2
    inv_freq = ROPE_THETA ** (-jnp.arange(half, dtype=F32) / half)
    ang = positions.astype(F32).reshape(-1, 1) * inv_freq
    cos, sin = jnp.cos(ang), jnp.sin(ang)
    S = cos.shape[0]
    z16, z32, z64 = jnp.zeros((S, half), F32), jnp.zeros((S, QK_ROPE), F32), jnp.zeros((S, QK_NOPE), F32)
    c = jnp.concatenate([jnp.ones((S, QK_NOPE), F32), cos, cos, z32], axis=1)
    sa = jnp.concatenate([z64, -sin, z16, z32], axis=1)
    sb = jnp.concatenate([z64, z16, sin, z32], axis=1)
    return c, sa, sb


def _local_step(x, mem, positions, target, weight_fns, early_grads_fn, late_grads_fn, sm):
    S = x.shape[0]
    HW = MLA_HEADS * HEAD_PAD
    rope_c, rope_sa, rope_sb = _rope_tables(positions)
    qk_scale = (QK_NOPE + QK_ROPE) ** -0.5

    def k_rms1(x_ref, g_ref, u_ref):
        u_ref[...] = _rms_fwd(x_ref[...], g_ref[...]).astype(BF16)

    u1, = _rows("rms_mix", k_rms1, [x], [sm["norm_mix_g"]], [(D_MODEL, BF16)])
    w_in, conv_w = weight_fns[0]((u1, rope_c, rope_sa, rope_sb))

    def epi_glu(acc, xs, outs):
        a, gt = acc[:, 0:CONV_CH], acc[:, CONV_CH:2 * CONV_CH]
        outs[0][...] = acc[...].astype(BF16)
        outs[1][...] = a * _sigmoid(gt)

    conv_in, z0 = _mm("proj_conv", u1, w_in, "nn", [(2 * CONV_CH, BF16), (CONV_CH, F32)], epi_glu, b_cols=W_IN_CONV)
    c_q = _mm_plain("proj_cq", u1, w_in, "nn", b_cols=W_IN_CQ)
    c_kv = _mm_plain("proj_ckv", u1, w_in, "nn", b_cols=W_IN_CKV)
    kr_raw = _mm_plain("proj_krope", u1, w_in, "nn", b_cols=W_IN_KR)

    def epi_sigmoid(acc, xs, outs):
        outs[0][...] = _sigmoid(acc[...]).astype(BF16)

    gates, = _mm("proj_gates", u1, w_in, "nn", [(2 * D_MODEL, BF16)], epi_sigmoid, b_cols=W_IN_GATES)

    z1, z3 = _conv_fwd(z0, conv_w, sm["conv_b"], sm["conv_ln_g"], sm["conv_ln_b"])
    wl = weight_fns[1](z1)
    conv_out = _mm_plain("conv_out", z3, wl["w_conv_out"], "nn", dtype=BF16)

    def k_lora_norm(cq_ref, ckv_ref, gq_ref, gkv_ref, qn_ref, kvn_ref):
        qn_ref[...] = _rms_fwd(cq_ref[...], gq_ref[...]).astype(BF16)
        kvn_ref[...] = _rms_fwd(ckv_ref[...], gkv_ref[...]).astype(BF16)

    qn, kvn = _rows("lora_norm", k_lora_norm, [c_q, c_kv], [sm["q_norm_g"], sm["kv_norm_g"]],
                    [(Q_LORA, BF16), (KV_LORA, BF16)])

    def epi_q(acc, xs, outs):
        c, sa, sb = xs[0][...], xs[1][...], xs[2][...]
        for h in range(MLA_HEADS):
            lo = h * HEAD_PAD
            outs[0][:, lo:lo + HEAD_PAD] = (_rope(acc[:, lo:lo + HEAD_PAD], c, sa, sb, 1.0) * qk_scale).astype(BF16)

    q_att, = _mm("q_up", qn, wl["w_uq"], "nn", [(HW, BF16)], epi_q, row_x=[rope_c, rope_sa, rope_sb], tn=HW)

    def epi_kv(acc, xs, outs):
        kr = _rope(xs[0][...], xs[1][...], xs[2][...], xs[3][...], 1.0)
        kr = kr + _neg_ones(kr.shape, STAT_COL_QK)
        vconst = _neg_ones(kr.shape, STAT_COL_V)
        for h in range(MLA_HEADS):
            lo = h * HEAD_PAD
            outs[0][:, lo:lo + HEAD_PAD] = (acc[:, lo:lo + HEAD_PAD] + kr).astype(BF16)
            outs[1][:, lo:lo + HEAD_PAD] = (acc[:, HW + lo:HW + lo + HEAD_PAD] + vconst).astype(BF16)

    k_att, v_att = _mm("kv_up", kvn, wl["w_ukv"], "nn", [(HW, BF16), (HW, BF16)], epi_kv,
                       row_x=[kr_raw, rope_c, rope_sa, rope_sb], tn=2 * HW)

    o_att, q_aug = _flash_fwd(q_att, k_att, v_att)
    wl.update(weight_fns[2](o_att))

    def epi_merge(acc, xs, outs):
        for rows in _row_chunks(acc.shape[0]):
            mo = acc[rows, :]
            g0, g1 = xs[0][rows, 0:D_MODEL].astype(F32), xs[0][rows, D_MODEL:].astype(F32)
            outs[0][rows, :] = mo.astype(BF16)
            outs[1][rows, :] = (g0 * xs[1][rows, :].astype(F32) + g1 * mo).astype(BF16)

    mla_out, merged = _mm("mla_out_merge", o_att, wl["w_mla_out"], "nn", [(D_MODEL, BF16), (D_MODEL, BF16)], epi_merge,
                          row_x=[gates, conv_out], tn=D_MODEL)

    def epi_res_norm(acc, xs, outs):
        h = xs[0][...] + acc[...]
        outs[0][...] = h
        outs[1][...] = _rms_fwd(h, xs[1][...]).astype(BF16)

    h1, u2 = _mm("mix_out", merged, wl["w_out"], "nn", [(D_MODEL, F32), (D_MODEL, BF16)], epi_res_norm,
                 row_x=[x], vec_x=[sm["norm_xattn_g"]], tn=D_MODEL)

    xscale = X_HEAD_DIM ** -0.5

    def epi_scale(acc, xs, outs):
        outs[0][...] = (acc[...] * xscale).astype(BF16)

    xq, = _mm("xattn_q", u2, wl["w_xq"], "nn", [(X_HEADS * X_HEAD_DIM, BF16)], epi_scale)

    def k_mem_norm(m_ref, g_ref, o_ref):
        o_ref[...] = _rms_fwd(m_ref[...], g_ref[...]).astype(BF16)

    mem_n, = _rows("mem_norm", k_mem_norm, [mem], [sm["norm_mem_g"]], [(D_MODEL, BF16)])
    kvx = _mm_plain("xattn_kv", mem_n, wl["w_xkv"], "nn", dtype=BF16)
    ox = _xattn_fwd(xq, kvx)
    h2, u3 = _mm("xattn_out", ox, wl["w_xo"], "nn", [(D_MODEL, F32), (D_MODEL, BF16)], epi_res_norm,
                 row_x=[h1], vec_x=[sm["norm_mlp_g"]], tn=D_MODEL)

    def epi_relu2(acc, xs, outs):
        r = jnp.maximum(acc[...], 0.0)
        outs[0][...] = (r * r).astype(BF16)

    hid, = _mm("mlp_up", u3, wl["w_mlp1"], "nn", [(D_FF, BF16)], epi_relu2)

    def epi_final(acc, xs, outs):
        g = xs[2][...]
        for rows in _row_chunks(acc.shape[0]):
            h = xs[0][rows, :] + acc[rows, :]
            e = _rms_fwd(h, g) - xs[1][rows, :]
            part = 0.5 * jnp.sum(jnp.mean(e * e, axis=-1, keepdims=True), axis=0, keepdims=True)
            dh, dg = _rms_bwd(h, g, e * (1.0 / D_MODEL))
            outs[0][rows, :] = dh
            outs[1][rows, :] = dh.astype(BF16)
            _accum(outs[2], jnp.broadcast_to(part, outs[2].shape), first=rows.start == 0)
            _accum(outs[3], dg, first=rows.start == 0)

    dh3, dh3b, loss, g_final = _mm("mlp_down_loss", hid, wl["w_mlp2"], "nn", [(D_MODEL, F32), (D_MODEL, BF16)], epi_final,
                                   row_x=[h2, target], vec_x=[sm["final_norm_g"]], sums=[(1, LANES), (1, D_MODEL)],
                                   tn=D_MODEL, tk=1024)

    def epi_drelu2(acc, xs, outs):
        outs[0][...] = (acc[...] * (2.0 * jnp.sqrt(xs[0][...].astype(F32)))).astype(BF16)

    da1, = _mm("mlp_down_dx", dh3b, wl["w_mlp2"], "nt", [(D_FF, BF16)], epi_drelu2, tile_x=[hid])
    g_mlp2 = _mm_plain("mlp_down_dw", hid, dh3b, "tn")
    g_mlp1 = _mm_plain("mlp_up_dw", u3, da1, "tn")

    def epi_norm_bwd(acc, xs, outs):
        for rows in _row_chunks(acc.shape[0]):
            dx, dg = _rms_bwd(xs[0][rows, :], xs[2][...], acc[rows, :])
            dh = xs[1][rows, :] + dx
            outs[0][rows, :] = dh
            if len(outs) == 3:
                outs[1][rows, :] = dh.astype(BF16)
            _accum(outs[-1], dg, first=rows.start == 0)

    def dx_norm_bwd(name, dy, w, xin, dres, vecs, with_bf16=True):
        outs = [(D_MODEL, F32), (D_MODEL, BF16)] if with_bf16 else [(D_MODEL, F32)]
        return _mm(name, dy, w, "nt", outs, epi_norm_bwd, row_x=[xin, dres], vec_x=vecs, sums=[(1, D_MODEL)],
                   tn=D_MODEL, tk=_pick(dy.shape[1], (1024, 768, 512)))

    dh2, dh2b, g_norm_mlp = dx_norm_bwd("mlp_up_dx_norm", da1, wl["w_mlp1"], h2, dh3, [sm["norm_mlp_g"]])

    dox = _mm_plain("xattn_out_dx", dh2b, wl["w_xo"], "nt", dtype=BF16)
    g_xo = _mm_plain("xattn_out_dw", ox, dh2b, "tn")
    dxq, dkvx = _xattn_bwd(xq, kvx, dox)
    g_xq = _mm_plain("xattn_q_dw", u2, dxq, "tn")
    g_xkv = _mm_plain("xattn_kv_dw", mem_n, dkvx, "tn")
    dmem_n = _mm_plain("xattn_kv_dx", dkvx, wl["w_xkv"], "nt")

    def k_mem_bwd(m_ref, d_ref, g_ref, dg_ref):
        _, dg = _rms_bwd(m_ref[...], g_ref[...], d_ref[...])
        _accum(dg_ref, dg)

    g_norm_mem, = _rows("mem_norm_bwd", k_mem_bwd, [mem, dmem_n], [sm["norm_mem_g"]], [], [(1, D_MODEL)])
    token = early_grads_fn(dict(w_mlp1=g_mlp1, w_mlp2=g_mlp2, w_xo=g_xo, w_xq=g_xq, w_xkv=g_xkv))
    dh1, dh1b, g_norm_xattn = dx_norm_bwd("xattn_q_dx_norm", dxq, wl["w_xq"], h1, dh2, [sm["norm_xattn_g"], token])

    dmerged = _mm_plain("mix_out_dx", dh1b, wl["w_out"], "nt", dtype=BF16)
    g_out = _mm_plain("mix_out_dw", merged, dh1b, "tn")

    def k_merge_bwd(dm_ref, g_ref, co_ref, mo_ref, dco_ref, dmo_ref, dgl_ref):
        dm = dm_ref[...].astype(F32)
        g0, g1 = g_ref[:, 0:D_MODEL].astype(F32), g_ref[:, D_MODEL:].astype(F32)
        dco_ref[...] = (dm * g0).astype(BF16)
        dmo_ref[...] = (dm * g1).astype(BF16)
        dgl_ref[:, 0:D_MODEL] = (dm * co_ref[...].astype(F32) * g0 * (1.0 - g0)).astype(BF16)
        dgl_ref[:, D_MODEL:] = (dm * mo_ref[...].astype(F32) * g1 * (1.0 - g1)).astype(BF16)

    dconv_out, dmla_out, dproj = _rows("merge_bwd", k_merge_bwd, [dmerged, gates, conv_out, mla_out], [],
                                       [(D_MODEL, BF16), (D_MODEL, BF16), (2 * D_MODEL, BF16)], tile=256,
                                       window=(2, W_IN_GATES, W_IN_COLS, None))

    def epi_do(acc, xs, outs):
        for h in range(MLA_HEADS):
            lo = h * HEAD_PAD
            do = acc[:, lo:lo + HEAD_PAD]
            delta = jnp.sum(do * xs[0][:, lo:lo + HEAD_PAD].astype(F32), axis=-1, keepdims=True)
            outs[0][:, lo:lo + HEAD_PAD] = _put_stats(do.astype(BF16), delta, STAT_COL_V)

    do_aug, = _mm("mla_out_dx", dmla_out, wl["w_mla_out"], "nt", [(HW, BF16)], epi_do, row_x=[o_att], tn=HW)
    g_mla_out = _mm_plain("mla_out_dw", o_att, dmla_out, "tn")
    dq_att, dk_att, dv_att = _flash_bwd(q_aug, k_att, v_att, do_aug)

    def k_rope_bwd(dq_ref, dk_ref, dv_ref, c_ref, sa_ref, sb_ref, dqr_ref, dkv_ref, dkr_ref):
        c, sa, sb = c_ref[...], sa_ref[...], sb_ref[...]
        lane = lax.broadcasted_iota(jnp.int32, c.shape, 1)
        nope = (lane < QK_NOPE).astype(F32)
        ropem = ((lane >= QK_NOPE) & (lane < QK_NOPE + QK_ROPE)).astype(F32)
        dkr = jnp.zeros(c.shape, F32)
        for h in range(MLA_HEADS):
            lo = h * HEAD_PAD
            dqr_ref[:, lo:lo + HEAD_PAD] = (_rope(dq_ref[:, lo:lo + HEAD_PAD], c, sa, sb, -1.0) * qk_scale).astype(BF16)
            dk = dk_ref[:, lo:lo + HEAD_PAD].astype(F32)
            dkv_ref[:, lo:lo + HEAD_PAD] = (dk * nope).astype(BF16)
            dkr = dkr + dk
        dkv_ref[:, HW:] = dv_ref[...]
        dkr_ref[...] = (_rope(dkr * ropem, c, sa, sb, -1.0) * ropem).astype(BF16)

    dq_raw, dkv_cat, dkr = _rows("rope_bwd", k_rope_bwd, [dq_att, dk_att, dv_att, rope_c, rope_sa, rope_sb], [],
                                 [(HW, BF16), (2 * HW, BF16), (HEAD_PAD, BF16)], tile=256)
    g_uq = _mm_plain("q_up_dw", qn, dq_raw, "tn")
    dqn = _mm_plain("q_up_dx", dq_raw, wl["w_uq"], "nt")
    g_ukv = _mm_plain("kv_up_dw", kvn, dkv_cat, "tn")
    dkvn = _mm_plain("kv_up_dx", dkv_cat, wl["w_ukv"], "nt")

    def k_lora_bwd(cq_ref, ckv_ref, dqn_ref, dkvn_ref, dkr_ref, gq_ref, gkv_ref, out_ref, dgq_ref, dgkv_ref):
        dcq, dgq = _rms_bwd(cq_ref[...], gq_ref[...], dqn_ref[...])
        dckv, dgkv = _rms_bwd(ckv_ref[...], gkv_ref[...], dkvn_ref[...])
        out_ref[:, 0:Q_LORA] = dcq.astype(BF16)
        out_ref[:, Q_LORA:Q_LORA + HEAD_PAD] = dkr_ref[...]
        out_ref[:, Q_LORA + HEAD_PAD:] = dckv.astype(BF16)
        _accum(dgq_ref, dgq)
        _accum(dgkv_ref, dgkv)

    dproj, g_q_norm, g_kv_norm = _rows("lora_norm_bwd", k_lora_bwd, [c_q, c_kv, dqn, dkvn, dkr],
                                       [sm["q_norm_g"], sm["kv_norm_g"]], [(W_IN_LORA[1], BF16)],
                                       [(1, Q_LORA), (1, KV_LORA)], window=(0, W_IN_LORA, W_IN_COLS, dproj))

    dz3 = _mm_plain("conv_out_dx", dconv_out, wl["w_conv_out"], "nt", dtype=BF16)
    g_conv_out = _mm_plain("conv_out_dw", z3, dconv_out, "tn")
    dz1, g_ln_g, g_ln_b, g_conv_b = _conv_bwd_norm(dz3, z1, sm["conv_ln_g"], sm["conv_ln_b"])
    dproj, g_conv_w = _conv_bwd_taps(dz1, z0, conv_in, conv_w, dproj, W_IN_CONV)

    g_in = _mm_plain("proj_dw", u1, dproj, "tn")
    token = late_grads_fn(dict(w_in=g_in, conv_w=g_conv_w[:CONV_WIDTH], w_conv_out=g_conv_out, w_uq=g_uq, w_ukv=g_ukv,
                               w_mla_out=g_mla_out, w_out=g_out))
    grad_x, g_norm_mix = dx_norm_bwd("proj_dx_norm", dproj, w_in, x, dh1, [sm["norm_mix_g"], token], with_bf16=False)

    small = dict(norm_mix_g=g_norm_mix, conv_b=g_conv_b, conv_ln_g=g_ln_g, conv_ln_b=g_ln_b, q_norm_g=g_q_norm,
                 kv_norm_g=g_kv_norm, norm_xattn_g=g_norm_xattn, norm_mem_g=g_norm_mem, norm_mlp_g=g_norm_mlp,
                 final_norm_g=g_final)
    return loss, grad_x, small


def _shard(a, k, axis):
    n = a.shape[axis] // N_CHIPS
    return lax.slice_in_dim(a, k * n, (k + 1) * n, axis=axis)


def _pack_small(grads, loss):
    flat = jnp.concatenate([grads[n].reshape(-1) for n in SMALL] + [loss.reshape(-1)[:1]])
    rows = -(-flat.shape[0] // (8 * LANES)) * 8
    return jnp.pad(flat, (0, rows * LANES - flat.shape[0])).reshape(rows, LANES)


def _pack_groups(shapes, names):
    groups = {}
    for n in names:
        groups.setdefault(shapes[n][1], []).append(n)
    return groups


def _pad_rows(a, mult):
    return jnp.pad(a, ((0, -a.shape[0] % mult), (0, 0)))


def _pack_grads(grads, shapes, names):
    packs = []
    for width, group in _pack_groups(shapes, names).items():
        per_chip = [jnp.concatenate([_pad_rows(_shard(grads[n], k, SHARD_AXIS[n]).astype(BF16), PACK_ROW_ALIGN) for n in group])
                    for k in range(N_CHIPS)]
        rows = per_chip[0].shape[0]
        packs.append(jnp.stack(per_chip).reshape(N_CHIPS, 2, rows // 2, width))
    return packs


def _unpack_grads(fulls, shapes, names):
    out = {}
    for full, group in zip(fulls, _pack_groups(shapes, names).values()):
        flat, at = full.reshape(-1, full.shape[-1]), 0
        for n in group:
            rows = shapes[n][0]
            out[n] = flat[at:at + rows]
            at += rows + (-rows % PACK_ROW_ALIGN)
    return out


def _unpack(flat, names, shapes):
    out, at = {}, 0
    for n in names:
        size = math.prod(shapes[n])
        out[n] = flat[at:at + size].reshape(shapes[n])
        at += size
    return out, at


def kernel(x, mem, positions, norm_mix_g, w_in, conv_w, conv_b, conv_ln_g, conv_ln_b, w_conv_out, q_norm_g, w_uq, kv_norm_g, w_ukv, w_mla_out, w_out, norm_xattn_g, norm_mem_g, w_xq, w_xkv, w_xo, norm_mlp_g, w_mlp1, w_mlp2, final_norm_g, loss_target, m_norm_mix_g, m_w_in, m_conv_w, m_conv_b, m_conv_ln_g, m_conv_ln_b, m_w_conv_out, m_q_norm_g, m_w_uq, m_kv_norm_g, m_w_ukv, m_w_mla_out, m_w_out, m_norm_xattn_g, m_norm_mem_g, m_w_xq, m_w_xkv, m_w_xo, m_norm_mlp_g, m_w_mlp1, m_w_mlp2, m_final_norm_g, v_norm_mix_g, v_w_in, v_conv_w, v_conv_b, v_conv_ln_g, v_conv_ln_b, v_w_conv_out, v_q_norm_g, v_w_uq, v_kv_norm_g, v_w_ukv, v_w_mla_out, v_w_out, v_norm_xattn_g, v_norm_mem_g, v_w_xq, v_w_xkv, v_w_xo, v_norm_mlp_g, v_w_mlp1, v_w_mlp2, v_final_norm_g):
    args = dict(locals())
    w = {n: args[n] for n in WEIGHTS}
    m = {n: args["m_" + n] for n in WEIGHTS}
    v = {n: args["v_" + n] for n in WEIGHTS}

    shards = [w[n][0].astype(F32 if n == "conv_w" else BF16) for n in BIG]
    bounds = (0,) + WEIGHT_WAITS + (len(BIG),)
    spans = [slice(lo, hi) for lo, hi in zip(bounds[:-1], bounds[1:])]
    sem_pairs, shards_thru, lands_thru, token = _gather_start(shards, [list(range(len(BIG)))[sp] for sp in spans])

    def unshard(n, g):
        ax = SHARD_AXIS[n]
        return jnp.moveaxis(g, 0, ax).reshape(g.shape[1:1 + ax] + (N_CHIPS * g.shape[1 + ax],) + g.shape[2 + ax:])

    def wait_fn(i):
        def fn(after):
            lands = _gather_wait(f"gather_weights_wait_{i}", sem_pairs[i], shards_thru[spans[i]], lands_thru[spans[i]], after)
            full = {n: unshard(n, g) for n, g in zip(BIG[spans[i]], lands)}
            return (_layout_w_in(full["w_in"]), full["conv_w"]) if i == 0 else _layout_weights(full)
        return fn

    sm = {n: w[n].reshape(1, -1) for n in SMALL}
    sm["norm_mix_g"] = sm["norm_mix_g"] + token[0, 0]

    shapes = {n: w[n].shape[1:] if n in BIG else w[n].shape for n in WEIGHTS}
    late_names = [n for n in BIG if n not in EARLY_GRADS]
    inflight = {}

    def send_grads(tag, names):
        def fn(g):
            packs = _pack_grads(_unlayout_grads(g), shapes, names)
            got = _pair_exchange(f"grad_pair_exchange_{tag}", packs)
            pairs = [_pair_add(f"grad_pair_add_{tag}_{i}", p, r) for i, (p, r) in enumerate(zip(packs, got))]
            *inflight[tag], token = _chip_exchange_start(f"grad_chip_exchange_{tag}_start", pairs)
            return token
        return fn

    loss, grad_x, g_small = _local_step(x[0], mem[0], positions, loss_target[0], [wait_fn(i) for i in range(3)],
                                        send_grads("early", EARLY_GRADS), send_grads("late", late_names), sm)

    halves, counts = [], {}
    for tag in ("late", "early"):
        own, got = _chip_exchange_wait(f"grad_chip_exchange_{tag}_wait", *inflight[tag], grad_x)
        halves += [_chip_add(f"grad_chip_add_{tag}_{i}", p, g) for i, (p, g) in enumerate(zip(own, got))]
        counts[tag] = len(own)
    fulls = _pair_share(halves)
    g_sum = _unpack_grads(fulls[:counts["late"]], shapes, late_names)
    g_sum.update(_unpack_grads(fulls[counts["late"]:], shapes, EARLY_GRADS))
    small_flat = _sum_over_devices(_pack_small(g_small, loss)).reshape(-1)
    g_small, at = _unpack(small_flat, SMALL, shapes)
    g_sum.update(g_small)
    loss_sum = small_flat[at]

    out_g, out_d, out_m, out_v = [], [], [], []
    for n in WEIGHTS:
        g, d, nm, nv = _adamw("adamw_" + n, w[n], g_sum[n], m[n], v[n])
        out_g.append(g)
        out_d.append(d)
        out_m.append(nm)
        out_v.append(nv)
    return (loss_sum, grad_x[None], *out_g, *out_d, *out_m, *out_v)
```

```python
import math

import jax
import jax.numpy as jnp
from jax import lax
from jax.experimental import pallas as pl
from jax.experimental.pallas import tpu as pltpu

F32 = jnp.float32
BF16 = jnp.bfloat16
MESH = pl.DeviceIdType.MESH

D_MODEL = 1024
CONV_CH = 512
CONV_WIDTH = 31
MLA_HEADS = 8
QK_NOPE = 64
QK_ROPE = 32
V_DIM = 64
Q_LORA = 384
KV_LORA = 256
MEM_LEN = 256
X_HEADS = 4
X_HEAD_DIM = 128
D_FF = 4096
ROPE_THETA = 10000.0
EPS = 1e-6
HEAD_PAD = 128
STAT_COL_QK = QK_NOPE + QK_ROPE
STAT_COL_V = V_DIM
HALO = 32
N_CHIPS = 4
LANES = 128

ADAM_LR = 0.001
ADAM_B1 = 0.9
ADAM_B2 = 0.999
ADAM_EPS = 1e-08
ADAM_WD = 0.01
ADAM_STEP = 10

VMEM_LIMIT = 52 * 1024 * 1024
ROW_TILES = (1024, 512, 256, 128, 64, 32, 16)
PACK_ROW_ALIGN = 32
N_DEV = 8
NEG = -1e30

BIG = ["w_in", "conv_w", "w_conv_out", "w_uq", "w_ukv", "w_mla_out", "w_out", "w_xq", "w_xkv", "w_xo", "w_mlp1", "w_mlp2"]
WEIGHT_WAITS = (2, 5)
SHARD_AXIS = {"w_in": 1, "w_conv_out": 1, "w_uq": 1, "w_ukv": 1, "w_mla_out": 1, "w_out": 0, "w_xq": 0, "w_xkv": 0,
              "w_xo": 1, "w_mlp1": 1, "w_mlp2": 0, "conv_w": 1}
EARLY_GRADS = ["w_mlp1", "w_mlp2", "w_xkv", "w_xq", "w_xo"]
SMALL = ["norm_mix_g", "conv_b", "conv_ln_g", "conv_ln_b", "q_norm_g", "kv_norm_g", "norm_xattn_g", "norm_mem_g",
         "norm_mlp_g", "final_norm_g"]
WEIGHTS = ["norm_mix_g", "w_in", "conv_w", "conv_b", "conv_ln_g", "conv_ln_b", "w_conv_out", "q_norm_g", "w_uq",
           "kv_norm_g", "w_ukv", "w_mla_out", "w_out", "norm_xattn_g", "norm_mem_g", "w_xq", "w_xkv", "w_xo",
           "norm_mlp_g", "w_mlp1", "w_mlp2", "final_norm_g"]


def _pick(n, prefs):
    for p in prefs:
        if n % p == 0:
            return p
    return n


def _params(sem):
    return pltpu.CompilerParams(dimension_semantics=sem, vmem_limit_bytes=VMEM_LIMIT)


_DIMS = {"nn": (((1,), (0,)), ((), ())), "nt": (((1,), (1,)), ((), ())), "tn": (((0,), (0,)), ((), ()))}


def _mm(name, a, b, mode, outs, epi, row_x=(), tile_x=(), vec_x=(), sums=(), tm=None, tn=None, tk=None, b_cols=None):
    if mode == "nn":
        (M, K), (_, N) = a.shape, b.shape
        if b_cols is not None:
            N = b_cols[1]
    elif mode == "nt":
        (M, K), (N, _) = a.shape, b.shape
    else:
        (K, M), (_, N) = a.shape, b.shape
    tm = tm or _pick(M, (1024, 512, 384, 256, 128))
    tn = tn or _pick(N, (1024, 768, 512, 384, 256, 128))
    tk = tk or _pick(K, (2048, 1920, 1024, 768, 512, 384, 256, 128))
    nk = K // tk
    rows_inner = nk == 1 and N // tn > 1
    grid = (N // tn, M // tm, nk) if rows_inner else (M // tm, N // tn, nk)

    def spec(shape, f):
        return pl.BlockSpec(shape, (lambda j, i, k: f(i, j, k)) if rows_inner else f)

    b_off = 0
    if b_cols is not None:
        assert mode == "nn" and b_cols[0] % tn == 0, (name, b_cols, tn)
        b_off = b_cols[0] // tn
    a_spec = spec((tk, tm), lambda i, j, k: (k, i)) if mode == "tn" else spec((tm, tk), lambda i, j, k: (i, k))
    b_spec = spec((tn, tk), lambda i, j, k: (j, k)) if mode == "nt" else spec((tk, tn), lambda i, j, k: (k, j + b_off))
    in_specs = [a_spec, b_spec]
    in_specs += [spec((tm, r.shape[1]), lambda i, j, k: (i, 0)) for r in row_x]
    in_specs += [spec((tm, tn), lambda i, j, k: (i, j)) for _ in tile_x]
    in_specs += [spec(v.shape, lambda i, j, k: (0, 0)) for v in vec_x]
    out_specs, out_shape = [], []
    for w, dt in outs:
        if tn == N:
            out_specs.append(spec((tm, w), lambda i, j, k: (i, 0)))
        else:
            assert w == N, (name, w, N)
            out_specs.append(spec((tm, tn), lambda i, j, k: (i, j)))
        out_shape.append(jax.ShapeDtypeStruct((M, w), dt))
    for shp in sums:
        assert tn == N and not rows_inner, name
        out_specs.append(spec(shp, lambda i, j, k: (0, 0)))
        out_shape.append(jax.ShapeDtypeStruct(shp, F32))
    nx = len(row_x) + len(tile_x) + len(vec_x)
    dims = _DIMS[mode]

    def body(a_ref, b_ref, *rest):
        x_refs, out_refs, acc_ref = rest[:nx], rest[nx:nx + len(outs) + len(sums)], rest[-1]
        av, bv = a_ref[...], b_ref[...]
        if av.dtype != BF16:
            av = av.astype(BF16)
        if bv.dtype != BF16:
            bv = bv.astype(BF16)
        prod = lax.dot_general(av, bv, dims, preferred_element_type=F32)
        if nk == 1:
            acc_ref[...] = prod
            epi(acc_ref, x_refs, out_refs)
        else:
            k = pl.program_id(2)

            @pl.when(k == 0)
            def _():
                acc_ref[...] = prod

            @pl.when(k > 0)
            def _():
                acc_ref[...] += prod

            @pl.when(k == nk - 1)
            def _():
                epi(acc_ref, x_refs, out_refs)

    res = pl.pallas_call(
        body, name=name, grid=grid, in_specs=in_specs, out_specs=out_specs, out_shape=out_shape,
        scratch_shapes=[pltpu.VMEM((tm, tn), F32)],
        compiler_params=_params(("arbitrary",) * 3 if sums else ("parallel", "parallel", "arbitrary")),
    )(a, b, *row_x, *tile_x, *vec_x)
    return res


def _epi_store(acc_ref, x_refs, out_refs):
    for o in out_refs:
        o[...] = acc_ref[...].astype(o.dtype)


def _mm_plain(name, a, b, mode, dtype=F32, **kw):
    n = kw["b_cols"][1] if kw.get("b_cols") else (b.shape[0] if mode == "nt" else b.shape[1])
    return _mm(name, a, b, mode, [(n, dtype)], _epi_store, **kw)[0]


def _rows(name, body, row_ins, vec_ins, row_outs, acc_outs=(), tile=512, window=None):
    S = row_ins[0].shape[0]
    t = _pick(S, (tile, 256, 128, 64, 32, 16, 8))
    in_specs = [pl.BlockSpec((t, r.shape[1]), lambda i: (i, 0)) for r in row_ins]
    in_specs += [pl.BlockSpec(v.shape, lambda i: (0, 0)) for v in vec_ins]
    out_specs = [pl.BlockSpec((t, w), lambda i: (i, 0)) for w, _ in row_outs]
    out_specs += [pl.BlockSpec(shp, lambda i: (0, 0)) for shp in acc_outs]
    out_shape = [jax.ShapeDtypeStruct((S, w), dt) for w, dt in row_outs]
    out_shape += [jax.ShapeDtypeStruct(shp, F32) for shp in acc_outs]
    extra, aliases, n_in = [], {}, len(row_ins) + len(vec_ins)
    if window is not None:
        k, (first, width), total, buffer = window
        assert first % width == 0 and row_outs[k][0] == width, (name, window[:3])
        out_specs[k] = pl.BlockSpec((t, width), lambda i: (i, first // width))
        out_shape[k] = jax.ShapeDtypeStruct((S, total), row_outs[k][1])
        if buffer is not None:
            extra, aliases = [buffer], {n_in: k}
            in_specs.append(pl.BlockSpec(memory_space=pl.ANY))

    def call(*refs):
        body(*refs[:n_in], *refs[n_in + len(extra):])

    sem = ("arbitrary",) if acc_outs else ("parallel",)
    return pl.pallas_call(
        call, name=name, grid=(S // t,), in_specs=in_specs, out_specs=out_specs,
        out_shape=out_shape, input_output_aliases=aliases, compiler_params=_params(sem),
    )(*row_ins, *vec_ins, *extra)


def _accum(ref, val, first=True):
    if first:
        @pl.when(pl.program_id(0) == 0)
        def _():
            ref[...] = jnp.zeros_like(ref)

    ref[...] += val


EPILOGUE_ROWS = 256


def _row_chunks(n):
    step = min(EPILOGUE_ROWS, n)
    return [slice(r, r + step) for r in range(0, n, step)]


def _colsum(v):
    return jnp.sum(v, axis=0, keepdims=True)


def _rms_fwd(x, g):
    r = lax.rsqrt(jnp.mean(x * x, axis=-1, keepdims=True) + EPS)
    return x * r * g


def _rms_bwd(x, g, du):
    r = lax.rsqrt(jnp.mean(x * x, axis=-1, keepdims=True) + EPS)
    xn = x * r
    gdu = du * g
    dx = r * (gdu - xn * jnp.mean(xn * gdu, axis=-1, keepdims=True))
    return dx, _colsum(du * xn)


def _sigmoid(v):
    return 1.0 / (1.0 + jnp.exp(-v))


def _rope(v, c, sa, sb, sign):
    return v * c + sign * (pltpu.roll(v, HEAD_PAD - QK_ROPE // 2, 1) * sa + pltpu.roll(v, QK_ROPE // 2, 1) * sb)


def _split3(v):
    hi = v.astype(BF16)
    r1 = v - hi.astype(F32)
    mid = r1.astype(BF16)
    lo = (r1 - mid.astype(F32)).astype(BF16)
    return hi, mid, lo


def _put_stats(base, stat, col):
    hi, mid, lo = _split3(stat)
    lane = lax.broadcasted_iota(jnp.int32, base.shape, 1)
    out = jnp.where(lane == col, hi, base)
    out = jnp.where(lane == col + 1, mid, out)
    return jnp.where(lane == col + 2, lo, out)


def _neg_ones(shape, col):
    lane = lax.broadcasted_iota(jnp.int32, shape, 1)
    return jnp.where((lane >= col) & (lane < col + 3), -1.0, 0.0).astype(F32)


def _shifted(ext, t):
    p = ext.shape[0]
    for b in range(8):
        rb = ext if b == 0 else pltpu.roll(ext, p - b, 0)
        for a in range(HALO // 8 + 1):
            if 8 * a + b <= HALO:
                yield 8 * a + b, rb[8 * a:8 * a + t]


def _conv_fwd(z0, conv_w, conv_b, ln_g, ln_b):
    S, C = z0.shape
    t = _pick(S, (512, 256, 128, 64, 32))
    per = t // HALO

    def body(cur_ref, prev_ref, w_ref, b_ref, g_ref, beta_ref, z1_ref, z3_ref, ext_ref):
        i = pl.program_id(0)
        ext_ref[0:HALO, :] = jnp.where(i > 0, prev_ref[...], 0.0)
        ext_ref[HALO:, :] = cur_ref[...]
        ext = ext_ref[...]
        acc = jnp.zeros((t, C), F32)
        for d, win in _shifted(ext, t):
            k = d - (HALO - CONV_WIDTH + 1)
            if 0 <= k < CONV_WIDTH:
                acc = acc + win * w_ref[k:k + 1, :]
        z1 = acc + b_ref[...]
        z1_ref[...] = z1
        mu = jnp.mean(z1, axis=-1, keepdims=True)
        zc = z1 - mu
        rs = lax.rsqrt(jnp.mean(zc * zc, axis=-1, keepdims=True) + EPS)
        z2 = zc * rs * g_ref[...] + beta_ref[...]
        z3_ref[...] = (z2 * _sigmoid(z2)).astype(BF16)

    vec = lambda v: pl.BlockSpec(v.shape, lambda i: (0, 0))
    return pl.pallas_call(
        body, name="conv_fwd", grid=(S // t,),
        in_specs=[pl.BlockSpec((t, C), lambda i: (i, 0)),
                  pl.BlockSpec((HALO, C), lambda i: (jnp.maximum(i * per - 1, 0), 0)),
                  vec(conv_w), vec(conv_b), vec(ln_g), vec(ln_b)],
        out_specs=[pl.BlockSpec((t, C), lambda i: (i, 0)), pl.BlockSpec((t, C), lambda i: (i, 0))],
        out_shape=[jax.ShapeDtypeStruct((S, C), F32), jax.ShapeDtypeStruct((S, C), BF16)],
        scratch_shapes=[pltpu.VMEM((t + HALO, C), F32)],
        compiler_params=_params(("parallel",)),
    )(z0, z0, conv_w, conv_b, ln_g, ln_b)


def _conv_bwd_norm(dz3, z1, ln_g, ln_b):
    C = z1.shape[1]

    def body(dz3_ref, z1_ref, g_ref, beta_ref, dz1_ref, dg_ref, dbeta_ref, dbias_ref):
        z1 = z1_ref[...]
        mu = jnp.mean(z1, axis=-1, keepdims=True)
        zc = z1 - mu
        rs = lax.rsqrt(jnp.mean(zc * zc, axis=-1, keepdims=True) + EPS)
        xh = zc * rs
        z2 = xh * g_ref[...] + beta_ref[...]
        sg = _sigmoid(z2)
        dz2 = dz3_ref[...].astype(F32) * (sg * (1.0 + z2 * (1.0 - sg)))
        dxh = dz2 * g_ref[...]
        dz1 = rs * (dxh - jnp.mean(dxh, axis=-1, keepdims=True) - xh * jnp.mean(dxh * xh, axis=-1, keepdims=True))
        dz1_ref[...] = dz1
        _accum(dg_ref, _colsum(dz2 * xh))
        _accum(dbeta_ref, _colsum(dz2))
        _accum(dbias_ref, _colsum(dz1))

    return _rows("conv_bwd_norm", body, [dz3, z1], [ln_g, ln_b], [(C, F32)], [(1, C)] * 3)


def _conv_bwd_taps(dz1, z0, conv_in, conv_w, dproj, window):
    S, C = z0.shape
    t = _pick(S, (512, 256, 128, 64, 32))
    per = t // HALO
    last = S // HALO - 1
    nt = S // t
    assert window[1] == 2 * C and window[0] % window[1] == 0, window

    def body(dcur_ref, dnext_ref, zcur_ref, zprev_ref, cin_ref, w_ref, _, dcin_ref, dw_ref, dext_ref, zext_ref):
        i = pl.program_id(0)
        dcur = dcur_ref[...]
        dext_ref[0:t, :] = dcur
        dext_ref[t:, :] = jnp.where(i < nt - 1, dnext_ref[...], 0.0)
        zext_ref[0:HALO, :] = jnp.where(i > 0, zprev_ref[...], 0.0)
        zext_ref[HALO:, :] = zcur_ref[...]

        @pl.when(i == 0)
        def _():
            dw_ref[...] = jnp.zeros_like(dw_ref)

        dz0 = jnp.zeros((t, C), F32)
        for d, win in _shifted(dext_ref[...], t):
            k = CONV_WIDTH - 1 - d
            if 0 <= k < CONV_WIDTH:
                dz0 = dz0 + win * w_ref[k:k + 1, :]
        for d, win in _shifted(zext_ref[...], t):
            k = d - (HALO - CONV_WIDTH + 1)
            if 0 <= k < CONV_WIDTH:
                dw_ref[k:k + 1, :] += _colsum(dcur * win)
        a = cin_ref[:, 0:C].astype(F32)
        sg = _sigmoid(cin_ref[:, C:2 * C].astype(F32))
        dcin_ref[:, 0:C] = (dz0 * sg).astype(BF16)
        dcin_ref[:, C:2 * C] = (dz0 * a * sg * (1.0 - sg)).astype(BF16)

    return pl.pallas_call(
        body, name="conv_bwd_taps", grid=(nt,),
        in_specs=[pl.BlockSpec((t, C), lambda i: (i, 0)),
                  pl.BlockSpec((HALO, C), lambda i: (jnp.minimum((i + 1) * per, last), 0)),
                  pl.BlockSpec((t, C), lambda i: (i, 0)),
                  pl.BlockSpec((HALO, C), lambda i: (jnp.maximum(i * per - 1, 0), 0)),
                  pl.BlockSpec((t, 2 * C), lambda i: (i, 0)),
                  pl.BlockSpec(conv_w.shape, lambda i: (0, 0)),
                  pl.BlockSpec(memory_space=pl.ANY)],
        out_specs=[pl.BlockSpec((t, 2 * C), lambda i: (i, window[0] // window[1])), pl.BlockSpec((HALO, C), lambda i: (0, 0))],
        out_shape=[jax.ShapeDtypeStruct(dproj.shape, BF16), jax.ShapeDtypeStruct((HALO, C), F32)],
        input_output_aliases={6: 0},
        scratch_shapes=[pltpu.VMEM((t + HALO, C), F32), pltpu.VMEM((t + HALO, C), F32)],
        compiler_params=_params(("arbitrary",)),
    )(dz1, dz1, z0, z0, conv_in, conv_w, dproj)


def _lower_tri(shape, rows_are_queries):
    row = lax.broadcasted_iota(jnp.int32, shape, 0)
    col = lax.broadcasted_iota(jnp.int32, shape, 1)
    return (col <= row) if rows_are_queries else (row <= col)


HEADS_PER_STEP = 2
FWD_HEADS_PER_STEP = 2
FWD_KEY_TILES = 8


def _flash_specs(S, t, heads):
    w = heads * HEAD_PAD
    blk = pl.BlockSpec((t, w), lambda h, i: (i, h))
    head = pl.BlockSpec((S, w), lambda h, i: (0, h))
    return blk, head


def _head_lanes(g):
    return slice(g * HEAD_PAD, (g + 1) * HEAD_PAD)


def _dot_nt(a, b):
    return lax.dot_general(a, b, _DIMS["nt"], preferred_element_type=F32)


def _dot_nn(a, b):
    return lax.dot_general(a, b, _DIMS["nn"], preferred_element_type=F32)


def _dot_tn(a, b):
    return lax.dot_general(a, b, _DIMS["tn"], preferred_element_type=F32)


def _flash_fwd(q, k, v):
    S = q.shape[0]
    t = _pick(S, (512, 256, 128))

    def body(q_ref, k_ref, v_ref, o_ref, qa_ref, m_ref, acc_ref):
        qi = pl.program_id(1)
        m_ref[...] = jnp.full_like(m_ref, NEG)
        acc_ref[...] = jnp.zeros_like(acc_ref)

        def step(first, tiles, diag):
            width = tiles * t
            rows = pl.ds(pl.multiple_of(first, t), width)
            for g in range(FWD_HEADS_PER_STEP):
                hl = _head_lanes(g)
                s = _dot_nt(q_ref[:, hl], k_ref[rows, hl])
                if diag:
                    row = lax.broadcasted_iota(jnp.int32, s.shape, 0)
                    col = lax.broadcasted_iota(jnp.int32, s.shape, 1)
                    s = jnp.where(col <= row + (tiles - 1) * t, s, NEG)
                m_old = m_ref[g]
                m_new = jnp.maximum(m_old, jnp.max(s, axis=-1, keepdims=True))
                p = jnp.exp(s - m_new).astype(BF16)
                acc_ref[g] = jnp.exp(m_old - m_new) * acc_ref[g] + _dot_nn(p, v_ref[rows, hl])
                m_ref[g] = m_new

        def wide(kb, carry):
            step(kb * (FWD_KEY_TILES * t), FWD_KEY_TILES, False)
            return carry

        full_groups = qi // FWD_KEY_TILES
        lax.fori_loop(0, full_groups, wide, 0)
        for tiles in range(1, min(FWD_KEY_TILES, S // t) + 1):
            @pl.when(qi - full_groups * FWD_KEY_TILES == tiles - 1)
            def _():
                step(full_groups * (FWD_KEY_TILES * t), tiles, True)

        for g in range(FWD_HEADS_PER_STEP):
            hl = _head_lanes(g)
            acc = acc_ref[g]
            l = -acc[:, STAT_COL_V:STAT_COL_V + 1]
            o_ref[:, hl] = (acc / l).astype(BF16)
            qa_ref[:, hl] = _put_stats(q_ref[:, hl], m_ref[g] + jnp.log(l), STAT_COL_QK)

    blk, head = _flash_specs(S, t, FWD_HEADS_PER_STEP)
    return pl.pallas_call(
        body, name="mla_flash_fwd", grid=(MLA_HEADS // FWD_HEADS_PER_STEP, S // t),
        in_specs=[blk, head, head], out_specs=[blk, blk],
        out_shape=[jax.ShapeDtypeStruct(q.shape, BF16), jax.ShapeDtypeStruct(q.shape, BF16)],
        scratch_shapes=[pltpu.VMEM((FWD_HEADS_PER_STEP, t, 1), F32), pltpu.VMEM((FWD_HEADS_PER_STEP, t, HEAD_PAD), F32)],
        compiler_params=_params(("parallel", "arbitrary")),
    )(q, k, v)


def _flash_bwd(qa, k, v, doa):
    S = qa.shape[0]
    t = _pick(S, (1024, 512, 256, 128))
    n = S // t
    half = t // 2

    def body(qa_ref, k_ref, v_ref, do_ref, dq_ref, dk_ref, dv_ref, dk_acc, dv_acc):
        kj = pl.program_id(1)

        @pl.when(kj == 0)
        def _():
            dq_ref[...] = jnp.zeros_like(dq_ref)

        dk_acc[...] = jnp.zeros_like(dk_acc)
        dv_acc[...] = jnp.zeros_like(dv_acc)

        def step(q_first, q_len, keys, diag):
            rows = pl.ds(pl.multiple_of(q_first, q_len), q_len)
            for g in range(HEADS_PER_STEP):
                hl = _head_lanes(g)
                qa, do, kk = qa_ref[rows, hl], do_ref[rows, hl], k_ref[keys, hl]
                st = _dot_nt(kk, qa)
                if diag:
                    st = jnp.where(_lower_tri(st.shape, False), st, NEG)
                pt = jnp.exp(st)
                dst = (pt * _dot_nt(v_ref[keys, hl], do)).astype(BF16)
                dv_acc[keys, hl] += _dot_nn(pt.astype(BF16), do)
                dk_acc[keys, hl] += _dot_nn(dst, qa)
                dq_ref[rows, hl] += _dot_tn(dst, kk)

        def loop(qi, carry):
            step(qi * t, t, slice(0, t), False)
            return carry

        lo, hi = slice(0, half), slice(half, t)
        step(kj * t, half, lo, True)
        step(kj * t + half, half, lo, False)
        step(kj * t + half, half, hi, True)
        lax.fori_loop(kj + 1, n, loop, 0)
        dk_ref[...] = dk_acc[...].astype(BF16)
        dv_ref[...] = dv_acc[...].astype(BF16)

    blk, head = _flash_specs(S, t, HEADS_PER_STEP)
    w = HEADS_PER_STEP * HEAD_PAD
    return pl.pallas_call(
        body, name="mla_flash_bwd", grid=(MLA_HEADS // HEADS_PER_STEP, n),
        in_specs=[head, blk, blk, head],
        out_specs=[head, blk, pl.BlockSpec((t, w), lambda h, i: (i, h + qa.shape[1] // w))],
        out_shape=[jax.ShapeDtypeStruct(qa.shape, F32), jax.ShapeDtypeStruct(qa.shape, BF16),
                   jax.ShapeDtypeStruct((S, 2 * qa.shape[1]), BF16)],
        scratch_shapes=[pltpu.VMEM((t, w), F32), pltpu.VMEM((t, w), F32)],
        compiler_params=_params(("parallel", "arbitrary")),
    )(qa, k, v, doa)


def _xattn_fwd(xq, kvx):
    W = X_HEADS * X_HEAD_DIM

    def body(q_ref, kv_ref, o_ref):
        for h in range(X_HEADS):
            lo = h * X_HEAD_DIM
            s = _dot_nt(q_ref[:, lo:lo + X_HEAD_DIM], kv_ref[:, lo:lo + X_HEAD_DIM])
            p = jnp.exp(s - jnp.max(s, axis=-1, keepdims=True))
            p = p / jnp.sum(p, axis=-1, keepdims=True)
            o_ref[:, lo:lo + X_HEAD_DIM] = _dot_nn(p.astype(BF16), kv_ref[:, W + lo:W + lo + X_HEAD_DIM]).astype(BF16)

    return _rows("xattn_fwd", body, [xq], [kvx], [(W, BF16)])[0]


def _xattn_bwd(xq, kvx, dox):
    W = X_HEADS * X_HEAD_DIM
    scale = X_HEAD_DIM ** -0.5

    def body(q_ref, do_ref, kv_ref, dq_ref, dkv_ref):
        @pl.when(pl.program_id(0) == 0)
        def _():
            dkv_ref[...] = jnp.zeros_like(dkv_ref)

        for h in range(X_HEADS):
            lo = h * X_HEAD_DIM
            q, k = q_ref[:, lo:lo + X_HEAD_DIM], kv_ref[:, lo:lo + X_HEAD_DIM]
            v, do = kv_ref[:, W + lo:W + lo + X_HEAD_DIM], do_ref[:, lo:lo + X_HEAD_DIM]
            s = _dot_nt(q, k)
            p = jnp.exp(s - jnp.max(s, axis=-1, keepdims=True))
            p = p / jnp.sum(p, axis=-1, keepdims=True)
            dp = _dot_nt(do, v)
            ds = (p * (dp - jnp.sum(dp * p, axis=-1, keepdims=True))).astype(BF16)
            dq_ref[:, lo:lo + X_HEAD_DIM] = (_dot_nn(ds, k) * scale).astype(BF16)
            dkv_ref[:, lo:lo + X_HEAD_DIM] += _dot_tn(ds, q)
            dkv_ref[:, W + lo:W + lo + X_HEAD_DIM] += _dot_tn(p.astype(BF16), do)

    return _rows("xattn_bwd", body, [xq, dox], [kvx], [(W, BF16)], [kvx.shape])


def _adamw(name, w, g, m, v):
    c1 = 1.0 / (1.0 - ADAM_B1 ** ADAM_STEP)
    c2 = 1.0 / (1.0 - ADAM_B2 ** ADAM_STEP)
    lead = (0,) * (w.ndim - 2)
    w2 = w.reshape((1,) * (2 - w.ndim) + w.shape) if w.ndim < 2 else w
    m2, v2 = m.reshape(w2.shape), v.reshape(w2.shape)
    g2 = g.reshape(w2.shape[-2:])
    R, C = g2.shape
    t = _pick(R, (256, 128, 64, 32, 16, 8))

    def body(w_ref, g_ref, m_ref, v_ref, go_ref, d_ref, nm_ref, nv_ref):
        g = g_ref[...]
        nm = ADAM_B1 * m_ref[lead] + (1.0 - ADAM_B1) * g
        nv = ADAM_B2 * v_ref[lead] + (1.0 - ADAM_B2) * (g * g)
        go_ref[lead] = g
        d_ref[lead] = -ADAM_LR * ((nm * c1) / (jnp.sqrt(nv * c2) + ADAM_EPS) + ADAM_WD * w_ref[lead])
        nm_ref[lead] = nm
        nv_ref[lead] = nv

    full = pl.BlockSpec((1,) * len(lead) + (t, C), lambda i: lead + (i, 0))
    outs = pl.pallas_call(
        body, name=name, grid=(R // t,), in_specs=[full, pl.BlockSpec((t, C), lambda i: (i, 0)), full, full],
        out_specs=[full] * 4, out_shape=[jax.ShapeDtypeStruct(w2.shape, F32)] * 4, compiler_params=_params(("parallel",)),
    )(w2, g2, m2, v2)
    return [o.reshape(w.shape) for o in outs]


def _place():
    x, y, c = lax.axis_index("x"), lax.axis_index("y"), lax.axis_index("c")
    return x, y, c, [(1 - x, y), (x, 1 - y), (1 - x, 1 - y)]


_ANY = pl.BlockSpec(memory_space=pl.ANY)


_HBM = pl.BlockSpec(memory_space=pltpu.HBM)
_SEM = pl.BlockSpec(memory_space=pltpu.SEMAPHORE)
_SIDE_EFFECT = pltpu.SideEffectType.DATAFLOW_SIDE_EFFECTING


def _gather_copy(src, land, slot, send, recv, k, chip, c):
    return pltpu.make_async_remote_copy(src_ref=src, dst_ref=land.at[slot], send_sem=send.at[k], recv_sem=recv.at[k],
                                        device_id=(chip[0], chip[1], c), device_id_type=MESH)


def _gather_start(shards, groups):
    n, ng = len(shards), len(groups)
    mine = 2 * lax.axis_index("x") + lax.axis_index("y")
    lands = [lax.dynamic_update_slice(lax.empty((N_CHIPS,) + s.shape, s.dtype), s[None], (mine,) + (0,) * s.ndim)
             for s in shards]

    def body(*refs):
        ins, lnd = refs[:n], refs[n:2 * n]
        sends, recvs = refs[2 * n:2 * n + ng], refs[2 * n + ng:2 * n + 2 * ng]
        token = refs[-1]
        x, y, c, chips = _place()
        for gi, group in enumerate(groups):
            for pos, w in enumerate(group):
                for j, chip in enumerate(chips):
                    _gather_copy(ins[w], lnd[w], 2 * x + y, sends[gi], recvs[gi], 3 * pos + j, chip, c).start()
        token[...] = jnp.zeros_like(token)

    sems = [pltpu.SemaphoreType.DMA((3 * len(g),)) for g in groups]
    res = pl.pallas_call(
        body, name="gather_weights_start",
        out_shape=sems + sems + [pltpu.HBM(a.shape, a.dtype) for a in shards + lands] + [jax.ShapeDtypeStruct((8, LANES), F32)],
        in_specs=[_HBM] * (2 * n),
        out_specs=[_SEM] * (2 * ng) + [_HBM] * (2 * n) + [pl.BlockSpec(memory_space=pltpu.VMEM)],
        input_output_aliases={i: 2 * ng + i for i in range(2 * n)},
        compiler_params=pltpu.CompilerParams(has_side_effects=_SIDE_EFFECT),
    )(*[pltpu.with_memory_space_constraint(a, pltpu.HBM) for a in shards + lands])
    sem_pairs = list(zip(res[:ng], res[ng:2 * ng]))
    return sem_pairs, res[2 * ng:2 * ng + n], res[2 * ng + n:2 * ng + 2 * n], res[-1]


def _gather_wait(name, sem_pair, shards_thru, lands_thru, after):
    m = len(shards_thru)
    after = after if isinstance(after, (tuple, list)) else (after,)

    def body(*refs):
        ins, lnd = refs[:m], refs[m:2 * m]
        send, recv = refs[2 * m], refs[2 * m + 1]
        x, y, c, chips = _place()
        for pos in range(m):
            for j, chip in enumerate(chips):
                cp = _gather_copy(ins[pos], lnd[pos], 2 * chip[0] + chip[1], send, recv, 3 * pos + j, chip, c)
                cp.wait_send()
                cp.wait_recv()

    res = pl.pallas_call(
        body, name=name,
        out_shape=[pltpu.HBM(a.shape, a.dtype) for a in list(shards_thru) + list(lands_thru)],
        in_specs=[_HBM] * (2 * m) + [_SEM, _SEM] + [_ANY] * len(after), out_specs=[_HBM] * (2 * m),
        input_output_aliases={i: i for i in range(2 * m)},
        compiler_params=pltpu.CompilerParams(has_side_effects=_SIDE_EFFECT),
    )(*shards_thru, *lands_thru, *sem_pair, *after)
    return res[m:]


def _pair_exchange(name, packs):
    n = len(packs)

    def body(*refs):
        ins, outs, send, recv = refs[:n], refs[n:2 * n], refs[2 * n], refs[2 * n + 1]
        x, y, c, _ = _place()
        cps = []
        for g in range(n):
            cp = pltpu.make_async_remote_copy(src_ref=ins[g].at[:, pl.ds(1 - c, 1)], dst_ref=outs[g], send_sem=send.at[g],
                                              recv_sem=recv.at[g], device_id=(x, y, 1 - c), device_id_type=MESH)
            cp.start()
            cps.append(cp)
        for cp in cps:
            cp.wait()

    return pl.pallas_call(
        body, name=name, in_specs=[_ANY] * n, out_specs=[_ANY] * n,
        out_shape=[jax.ShapeDtypeStruct((N_CHIPS, 1) + p.shape[2:], p.dtype) for p in packs],
        scratch_shapes=[pltpu.SemaphoreType.DMA((n,)), pltpu.SemaphoreType.DMA((n,))],
    )(*packs)


def _chip_copy(src, land, src_slot, dst_slot, send, recv, k, chip, c):
    return pltpu.make_async_remote_copy(src_ref=src.at[src_slot], dst_ref=land.at[dst_slot], send_sem=send.at[k],
                                        recv_sem=recv.at[k], device_id=(chip[0], chip[1], c), device_id_type=MESH)


def _chip_exchange_start(name, parts):
    n = len(parts)
    lands = [lax.empty(p.shape, p.dtype) for p in parts]

    def body(*refs):
        ins, lnd, send, recv, token = refs[:n], refs[n:2 * n], refs[2 * n], refs[2 * n + 1], refs[-1]
        x, y, c, chips = _place()
        for g in range(n):
            for j, chip in enumerate(chips):
                _chip_copy(ins[g], lnd[g], 2 * chip[0] + chip[1], 2 * x + y, send, recv, 3 * g + j, chip, c).start()
        token[...] = jnp.zeros_like(token)

    sems = [pltpu.SemaphoreType.DMA((3 * n,))] * 2
    res = pl.pallas_call(
        body, name=name,
        out_shape=sems + [pltpu.HBM(a.shape, a.dtype) for a in list(parts) + lands] + [jax.ShapeDtypeStruct((8, LANES), F32)],
        in_specs=[_HBM] * (2 * n),
        out_specs=[_SEM] * 2 + [_HBM] * (2 * n) + [pl.BlockSpec(memory_space=pltpu.VMEM)],
        input_output_aliases={i: 2 + i for i in range(2 * n)},
        compiler_params=pltpu.CompilerParams(has_side_effects=_SIDE_EFFECT),
    )(*[pltpu.with_memory_space_constraint(a, pltpu.HBM) for a in list(parts) + lands])
    return res[:2], res[2:2 + n], res[2 + n:2 + 2 * n], res[-1]


def _chip_exchange_wait(name, sems, parts_thru, lands_thru, after):
    n = len(parts_thru)

    def body(*refs):
        ins, lnd, send, recv = refs[:n], refs[n:2 * n], refs[2 * n], refs[2 * n + 1]
        x, y, c, chips = _place()
        for g in range(n):
            for j, chip in enumerate(chips):
                cp = _chip_copy(ins[g], lnd[g], 2 * x + y, 2 * chip[0] + chip[1], send, recv, 3 * g + j, chip, c)
                cp.wait_send()
                cp.wait_recv()

    res = pl.pallas_call(
        body, name=name,
        out_shape=[pltpu.HBM(a.shape, a.dtype) for a in list(parts_thru) + list(lands_thru)],
        in_specs=[_HBM] * (2 * n) + [_SEM, _SEM, _ANY], out_specs=[_HBM] * (2 * n),
        input_output_aliases={i: i for i in range(2 * n)},
        compiler_params=pltpu.CompilerParams(has_side_effects=_SIDE_EFFECT),
    )(*parts_thru, *lands_thru, *sems, after)
    return res[:n], res[n:]


def _pair_share(halves):
    n = len(halves)

    def body(*refs):
        outs, send, recv = refs[n:2 * n], refs[2 * n], refs[2 * n + 1]
        x, y, c, _ = _place()
        cps = []
        for g in range(n):
            cp = pltpu.make_async_remote_copy(src_ref=outs[g].at[c], dst_ref=outs[g].at[c], send_sem=send.at[g],
                                              recv_sem=recv.at[g], device_id=(x, y, 1 - c), device_id_type=MESH)
            cp.start()
            cps.append(cp)
        for g in range(n):
            pltpu.make_async_remote_copy(src_ref=outs[g].at[c], dst_ref=outs[g].at[1 - c], send_sem=send.at[g],
                                         recv_sem=recv.at[g], device_id=(x, y, 1 - c), device_id_type=MESH).wait_recv()
        for cp in cps:
            cp.wait_send()

    return pl.pallas_call(
        body, name="grad_pair_share", in_specs=[_ANY] * n, out_specs=[_ANY] * n,
        out_shape=[jax.ShapeDtypeStruct(h.shape, h.dtype) for h in halves],
        input_output_aliases={g: g for g in range(n)},
        scratch_shapes=[pltpu.SemaphoreType.DMA((n,)), pltpu.SemaphoreType.DMA((n,))],
    )(*halves)


def _sum_over_devices(block):
    R, L = block.shape

    def gather(b_ref, o_ref, send, recv, loc):
        x, y, c, _ = _place()
        lc = pltpu.make_async_copy(b_ref, o_ref.at[4 * x + 2 * y + c], loc)
        lc.start()
        peers = [(1 - x if dx else x, 1 - y if dy else y, 1 - c if dc else c)
                 for dx in (0, 1) for dy in (0, 1) for dc in (0, 1) if dx or dy or dc]
        cps = []
        for j, peer in enumerate(peers):
            cp = pltpu.make_async_remote_copy(src_ref=b_ref, dst_ref=o_ref.at[4 * x + 2 * y + c], send_sem=send.at[j],
                                              recv_sem=recv.at[j], device_id=peer, device_id_type=MESH)
            cp.start()
            cps.append(cp)
        for j, (px, py, pc) in enumerate(peers):
            pltpu.make_async_remote_copy(src_ref=b_ref, dst_ref=o_ref.at[4 * px + 2 * py + pc], send_sem=send.at[j],
                                         recv_sem=recv.at[j], device_id=(px, py, pc), device_id_type=MESH).wait_recv()
        for cp in cps:
            cp.wait_send()
        lc.wait()

    blocks = pl.pallas_call(
        gather, name="small_grads_gather", in_specs=[_ANY], out_specs=_ANY,
        out_shape=jax.ShapeDtypeStruct((N_DEV, R, L), F32),
        scratch_shapes=[pltpu.SemaphoreType.DMA((N_DEV - 1,)), pltpu.SemaphoreType.DMA((N_DEV - 1,)), pltpu.SemaphoreType.DMA],
    )(block)

    def add(b_ref, o_ref):
        total = b_ref[0]
        for d in range(1, N_DEV):
            total = total + b_ref[d]
        o_ref[...] = total

    return pl.pallas_call(add, name="small_grads_add", out_shape=jax.ShapeDtypeStruct((R, L), F32))(blocks)


def _pair_add(name, pack, got):
    _, _, R, C = pack.shape
    t = _pick(R, ROW_TILES)
    c = lax.axis_index("c").astype(jnp.int32).reshape(1)

    def body(c_ref, p_ref, g_ref, o_ref):
        o_ref[...] = (p_ref[...].astype(F32) + g_ref[...].astype(F32)).astype(BF16)

    return pl.pallas_call(
        body, name=name,
        grid_spec=pltpu.PrefetchScalarGridSpec(
            num_scalar_prefetch=1, grid=(N_CHIPS, R // t),
            in_specs=[pl.BlockSpec((None, None, t, C), lambda k, i, c_ref: (k, c_ref[0], i, 0)),
                      pl.BlockSpec((None, None, t, C), lambda k, i, c_ref: (k, 0, i, 0))],
            out_specs=pl.BlockSpec((None, t, C), lambda k, i, c_ref: (k, i, 0))),
        out_shape=jax.ShapeDtypeStruct((N_CHIPS, R, C), BF16), compiler_params=_params(("parallel", "parallel")),
    )(c, pack, got)


def _chip_add(name, own, got):
    _, R, C = own.shape
    t = _pick(R, ROW_TILES)
    x, y, c, _ = _place()
    place = jnp.stack([c, 2 * x + y]).astype(jnp.int32)

    def body(place_ref, own_ref, g1_ref, g2_ref, g3_ref, o_ref):
        o_ref[...] = ((own_ref[...].astype(F32) + g1_ref[...].astype(F32)) + g2_ref[...].astype(F32)) + g3_ref[...].astype(F32)

    def other(d):
        return pl.BlockSpec((None, t, C), lambda i, place_ref: ((place_ref[1] + d) % N_CHIPS, i, 0))

    return pl.pallas_call(
        body, name=name,
        grid_spec=pltpu.PrefetchScalarGridSpec(
            num_scalar_prefetch=1, grid=(R // t,),
            in_specs=[pl.BlockSpec((None, t, C), lambda i, place_ref: (place_ref[1], i, 0)), other(1), other(2), other(3)],
            out_specs=pl.BlockSpec((None, t, C), lambda i, place_ref: (place_ref[0], i, 0))),
        out_shape=jax.ShapeDtypeStruct((2, R, C), F32), compiler_params=_params(("parallel",)),
    )(place, own, got, got, got)


_CUT = (2 * CONV_CH, 2 * CONV_CH + Q_LORA, 2 * CONV_CH + Q_LORA + KV_LORA, 2 * CONV_CH + Q_LORA + KV_LORA + QK_ROPE)
W_IN_GATES = (0, 2 * D_MODEL)
W_IN_CONV = (W_IN_GATES[1], 2 * CONV_CH)
W_IN_CQ = (W_IN_CONV[0] + W_IN_CONV[1], Q_LORA)
W_IN_KR = (W_IN_CQ[0] + W_IN_CQ[1], HEAD_PAD)
W_IN_CKV = (W_IN_KR[0] + W_IN_KR[1], KV_LORA)
W_IN_LORA = (W_IN_CQ[0], Q_LORA + HEAD_PAD + KV_LORA)
W_IN_COLS = W_IN_CKV[0] + W_IN_CKV[1]


def _pad_last(a, n):
    return jnp.pad(a, [(0, 0)] * (a.ndim - 1) + [(0, n - a.shape[-1])])


def _layout_w_in(w_in):
    kr = jnp.pad(w_in[:, _CUT[2]:_CUT[3]], ((0, 0), (QK_NOPE, HEAD_PAD - QK_NOPE - QK_ROPE)))
    return jnp.concatenate([w_in[:, _CUT[3]:], w_in[:, :_CUT[0]], w_in[:, _CUT[0]:_CUT[1]], kr, w_in[:, _CUT[1]:_CUT[2]]], axis=1)


def _layout_weights(w):
    out = dict(w)
    if "w_uq" in w:
        out["w_uq"] = _pad_last(w["w_uq"].reshape(Q_LORA, MLA_HEADS, QK_NOPE + QK_ROPE), HEAD_PAD).reshape(Q_LORA, MLA_HEADS * HEAD_PAD)
    if "w_ukv" in w:
        ukv = w["w_ukv"].reshape(KV_LORA, MLA_HEADS, QK_NOPE + V_DIM)
        uk = _pad_last(ukv[:, :, :QK_NOPE], HEAD_PAD).reshape(KV_LORA, MLA_HEADS * HEAD_PAD)
        uv = _pad_last(ukv[:, :, QK_NOPE:], HEAD_PAD).reshape(KV_LORA, MLA_HEADS * HEAD_PAD)
        out["w_ukv"] = jnp.concatenate([uk, uv], axis=1)
    if "w_mla_out" in w:
        mo = jnp.pad(w["w_mla_out"].reshape(MLA_HEADS, V_DIM, D_MODEL), ((0, 0), (0, HEAD_PAD - V_DIM), (0, 0)))
        out["w_mla_out"] = mo.reshape(MLA_HEADS * HEAD_PAD, D_MODEL)
    return out


def _unlayout_grads(g):
    out = dict(g)
    if "w_in" in g:
        gi = g["w_in"]
        win = lambda w: gi[:, w[0]:w[0] + w[1]]
        kr = gi[:, W_IN_KR[0] + QK_NOPE:W_IN_KR[0] + QK_NOPE + QK_ROPE]
        out["w_in"] = jnp.concatenate([win(W_IN_CONV), win(W_IN_CQ), win(W_IN_CKV), kr, win(W_IN_GATES)], axis=1)
    if "w_uq" in g:
        out["w_uq"] = g["w_uq"].reshape(Q_LORA, MLA_HEADS, HEAD_PAD)[:, :, :QK_NOPE + QK_ROPE].reshape(Q_LORA, -1)
    if "w_ukv" in g:
        gk = g["w_ukv"][:, :MLA_HEADS * HEAD_PAD].reshape(KV_LORA, MLA_HEADS, HEAD_PAD)[:, :, :QK_NOPE]
        gv = g["w_ukv"][:, MLA_HEADS * HEAD_PAD:].reshape(KV_LORA, MLA_HEADS, HEAD_PAD)[:, :, :V_DIM]
        out["w_ukv"] = jnp.concatenate([gk, gv], axis=2).reshape(KV_LORA, -1)
    if "w_mla_out" in g:
        out["w_mla_out"] = g["w_mla_out"].reshape(MLA_HEADS, HEAD_PAD, D_MODEL)[:, :V_DIM].reshape(MLA_HEADS * V_DIM, D_MODEL)
    return out


def _rope_tables(positions):
    half = QK_ROPE // 2
    inv_freq = ROPE_THETA ** (-jnp.arange(half, dtype=F32) / half)
    ang = positions.astype(F32).reshape(-1, 1) * inv_freq
    cos, sin = jnp.cos(ang), jnp.sin(ang)
    S = cos.shape[0]
    z16, z32, z64 = jnp.zeros((S, half), F32), jnp.zeros((S, QK_ROPE), F32), jnp.zeros((S, QK_NOPE), F32)
    c = jnp.concatenate([jnp.ones((S, QK_NOPE), F32), cos, cos, z32], axis=1)
    sa = jnp.concatenate([z64, -sin, z16, z32], axis=1)
    sb = jnp.concatenate([z64, z16, sin, z32], axis=1)
    return c, sa, sb


def _local_step(x, mem, positions, target, weight_fns, early_grads_fn, late_grads_fn, sm):
    S = x.shape[0]
    HW = MLA_HEADS * HEAD_PAD
    rope_c, rope_sa, rope_sb = _rope_tables(positions)
    qk_scale = (QK_NOPE + QK_ROPE) ** -0.5

    def k_rms1(x_ref, g_ref, u_ref):
        u_ref[...] = _rms_fwd(x_ref[...], g_ref[...]).astype(BF16)

    u1, = _rows("rms_mix", k_rms1, [x], [sm["norm_mix_g"]], [(D_MODEL, BF16)])
    w_in, conv_w = weight_fns[0]((u1, rope_c, rope_sa, rope_sb))

    def epi_glu(acc, xs, outs):
        a, gt = acc[:, 0:CONV_CH], acc[:, CONV_CH:2 * CONV_CH]
        outs[0][...] = acc[...].astype(BF16)
        outs[1][...] = a * _sigmoid(gt)

    conv_in, z0 = _mm("proj_conv", u1, w_in, "nn", [(2 * CONV_CH, BF16), (CONV_CH, F32)], epi_glu, b_cols=W_IN_CONV)
    c_q = _mm_plain("proj_cq", u1, w_in, "nn", b_cols=W_IN_CQ)
    c_kv = _mm_plain("proj_ckv", u1, w_in, "nn", b_cols=W_IN_CKV)
    kr_raw = _mm_plain("proj_krope", u1, w_in, "nn", b_cols=W_IN_KR)

    def epi_sigmoid(acc, xs, outs):
        outs[0][...] = _sigmoid(acc[...]).astype(BF16)

    gates, = _mm("proj_gates", u1, w_in, "nn", [(2 * D_MODEL, BF16)], epi_sigmoid, b_cols=W_IN_GATES)

    z1, z3 = _conv_fwd(z0, conv_w, sm["conv_b"], sm["conv_ln_g"], sm["conv_ln_b"])
    wl = weight_fns[1](z1)
    conv_out = _mm_plain("conv_out", z3, wl["w_conv_out"], "nn", dtype=BF16)

    def k_lora_norm(cq_ref, ckv_ref, gq_ref, gkv_ref, qn_ref, kvn_ref):
        qn_ref[...] = _rms_fwd(cq_ref[...], gq_ref[...]).astype(BF16)
        kvn_ref[...] = _rms_fwd(ckv_ref[...], gkv_ref[...]).astype(BF16)

    qn, kvn = _rows("lora_norm", k_lora_norm, [c_q, c_kv], [sm["q_norm_g"], sm["kv_norm_g"]],
                    [(Q_LORA, BF16), (KV_LORA, BF16)])

    def epi_q(acc, xs, outs):
        c, sa, sb = xs[0][...], xs[1][...], xs[2][...]
        for h in range(MLA_HEADS):
            lo = h * HEAD_PAD
            outs[0][:, lo:lo + HEAD_PAD] = (_rope(acc[:, lo:lo + HEAD_PAD], c, sa, sb, 1.0) * qk_scale).astype(BF16)

    q_att, = _mm("q_up", qn, wl["w_uq"], "nn", [(HW, BF16)], epi_q, row_x=[rope_c, rope_sa, rope_sb], tn=HW)

    def epi_kv(acc, xs, outs):
        kr = _rope(xs[0][...], xs[1][...], xs[2][...], xs[3][...], 1.0)
        kr = kr + _neg_ones(kr.shape, STAT_COL_QK)
        vconst = _neg_ones(kr.shape, STAT_COL_V)
        for h in range(MLA_HEADS):
            lo = h * HEAD_PAD
            outs[0][:, lo:lo + HEAD_PAD] = (acc[:, lo:lo + HEAD_PAD] + kr).astype(BF16)
            outs[1][:, lo:lo + HEAD_PAD] = (acc[:, HW + lo:HW + lo + HEAD_PAD] + vconst).astype(BF16)

    k_att, v_att = _mm("kv_up", kvn, wl["w_ukv"], "nn", [(HW, BF16), (HW, BF16)], epi_kv,
                       row_x=[kr_raw, rope_c, rope_sa, rope_sb], tn=2 * HW)

    o_att, q_aug = _flash_fwd(q_att, k_att, v_att)
    wl.update(weight_fns[2](o_att))

    def epi_merge(acc, xs, outs):
        for rows in _row_chunks(acc.shape[0]):
            mo = acc[rows, :]
            g0, g1 = xs[0][rows, 0:D_MODEL].astype(F32), xs[0][rows, D_MODEL:].astype(F32)
            outs[0][rows, :] = mo.astype(BF16)
            outs[1][rows, :] = (g0 * xs[1][rows, :].astype(F32) + g1 * mo).astype(BF16)

    mla_out, merged = _mm("mla_out_merge", o_att, wl["w_mla_out"], "nn", [(D_MODEL, BF16), (D_MODEL, BF16)], epi_merge,
                          row_x=[gates, conv_out], tn=D_MODEL)

    def epi_res_norm(acc, xs, outs):
        h = xs[0][...] + acc[...]
        outs[0][...] = h
        outs[1][...] = _rms_fwd(h, xs[1][...]).astype(BF16)

    h1, u2 = _mm("mix_out", merged, wl["w_out"], "nn", [(D_MODEL, F32), (D_MODEL, BF16)], epi_res_norm,
                 row_x=[x], vec_x=[sm["norm_xattn_g"]], tn=D_MODEL)

    xscale = X_HEAD_DIM ** -0.5

    def epi_scale(acc, xs, outs):
        outs[0][...] = (acc[...] * xscale).astype(BF16)

    xq, = _mm("xattn_q", u2, wl["w_xq"], "nn", [(X_HEADS * X_HEAD_DIM, BF16)], epi_scale)

    def k_mem_norm(m_ref, g_ref, o_ref):
        o_ref[...] = _rms_fwd(m_ref[...], g_ref[...]).astype(BF16)

    mem_n, = _rows("mem_norm", k_mem_norm, [mem], [sm["norm_mem_g"]], [(D_MODEL, BF16)])
    kvx = _mm_plain("xattn_kv", mem_n, wl["w_xkv"], "nn", dtype=BF16)
    ox = _xattn_fwd(xq, kvx)
    h2, u3 = _mm("xattn_out", ox, wl["w_xo"], "nn", [(D_MODEL, F32), (D_MODEL, BF16)], epi_res_norm,
                 row_x=[h1], vec_x=[sm["norm_mlp_g"]], tn=D_MODEL)

    def epi_relu2(acc, xs, outs):
        r = jnp.maximum(acc[...], 0.0)
        outs[0][...] = (r * r).astype(BF16)

    hid, = _mm("mlp_up", u3, wl["w_mlp1"], "nn", [(D_FF, BF16)], epi_relu2)

    def epi_final(acc, xs, outs):
        g = xs[2][...]
        for rows in _row_chunks(acc.shape[0]):
            h = xs[0][rows, :] + acc[rows, :]
            e = _rms_fwd(h, g) - xs[1][rows, :]
            part = 0.5 * jnp.sum(jnp.mean(e * e, axis=-1, keepdims=True), axis=0, keepdims=True)
            dh, dg = _rms_bwd(h, g, e * (1.0 / D_MODEL))
            outs[0][rows, :] = dh
            outs[1][rows, :] = dh.astype(BF16)
            _accum(outs[2], jnp.broadcast_to(part, outs[2].shape), first=rows.start == 0)
            _accum(outs[3], dg, first=rows.start == 0)

    dh3, dh3b, loss, g_final = _mm("mlp_down_loss", hid, wl["w_mlp2"], "nn", [(D_MODEL, F32), (D_MODEL, BF16)], epi_final,
                                   row_x=[h2, target], vec_x=[sm["final_norm_g"]], sums=[(1, LANES), (1, D_MODEL)],
                                   tn=D_MODEL, tk=1024)

    def epi_drelu2(acc, xs, outs):
        outs[0][...] = (acc[...] * (2.0 * jnp.sqrt(xs[0][...].astype(F32)))).astype(BF16)

    da1, = _mm("mlp_down_dx", dh3b, wl["w_mlp2"], "nt", [(D_FF, BF16)], epi_drelu2, tile_x=[hid])
    g_mlp2 = _mm_plain("mlp_down_dw", hid, dh3b, "tn")
    g_mlp1 = _mm_plain("mlp_up_dw", u3, da1, "tn")

    def epi_norm_bwd(acc, xs, outs):
        for rows in _row_chunks(acc.shape[0]):
            dx, dg = _rms_bwd(xs[0][rows, :], xs[2][...], acc[rows, :])
            dh = xs[1][rows, :] + dx
            outs[0][rows, :] = dh
            if len(outs) == 3:
                outs[1][rows, :] = dh.astype(BF16)
            _accum(outs[-1], dg, first=rows.start == 0)

    def dx_norm_bwd(name, dy, w, xin, dres, vecs, with_bf16=True):
        outs = [(D_MODEL, F32), (D_MODEL, BF16)] if with_bf16 else [(D_MODEL, F32)]
        return _mm(name, dy, w, "nt", outs, epi_norm_bwd, row_x=[xin, dres], vec_x=vecs, sums=[(1, D_MODEL)],
                   tn=D_MODEL, tk=_pick(dy.shape[1], (1024, 768, 512)))

    dh2, dh2b, g_norm_mlp = dx_norm_bwd("mlp_up_dx_norm", da1, wl["w_mlp1"], h2, dh3, [sm["norm_mlp_g"]])

    dox = _mm_plain("xattn_out_dx", dh2b, wl["w_xo"], "nt", dtype=BF16)
    g_xo = _mm_plain("xattn_out_dw", ox, dh2b, "tn")
    dxq, dkvx = _xattn_bwd(xq, kvx, dox)
    g_xq = _mm_plain("xattn_q_dw", u2, dxq, "tn")
    g_xkv = _mm_plain("xattn_kv_dw", mem_n, dkvx, "tn")
    dmem_n = _mm_plain("xattn_kv_dx", dkvx, wl["w_xkv"], "nt")

    def k_mem_bwd(m_ref, d_ref, g_ref, dg_ref):
        _, dg = _rms_bwd(m_ref[...], g_ref[...], d_ref[...])
        _accum(dg_ref, dg)

    g_norm_mem, = _rows("mem_norm_bwd", k_mem_bwd, [mem, dmem_n], [sm["norm_mem_g"]], [], [(1, D_MODEL)])
    token = early_grads_fn(dict(w_mlp1=g_mlp1, w_mlp2=g_mlp2, w_xo=g_xo, w_xq=g_xq, w_xkv=g_xkv))
    dh1, dh1b, g_norm_xattn = dx_norm_bwd("xattn_q_dx_norm", dxq, wl["w_xq"], h1, dh2, [sm["norm_xattn_g"], token])

    dmerged = _mm_plain("mix_out_dx", dh1b, wl["w_out"], "nt", dtype=BF16)
    g_out = _mm_plain("mix_out_dw", merged, dh1b, "tn")

    def k_merge_bwd(dm_ref, g_ref, co_ref, mo_ref, dco_ref, dmo_ref, dgl_ref):
        dm = dm_ref[...].astype(F32)
        g0, g1 = g_ref[:, 0:D_MODEL].astype(F32), g_ref[:, D_MODEL:].astype(F32)
        dco_ref[...] = (dm * g0).astype(BF16)
        dmo_ref[...] = (dm * g1).astype(BF16)
        dgl_ref[:, 0:D_MODEL] = (dm * co_ref[...].astype(F32) * g0 * (1.0 - g0)).astype(BF16)
        dgl_ref[:, D_MODEL:] = (dm * mo_ref[...].astype(F32) * g1 * (1.0 - g1)).astype(BF16)

    dconv_out, dmla_out, dproj = _rows("merge_bwd", k_merge_bwd, [dmerged, gates, conv_out, mla_out], [],
                                       [(D_MODEL, BF16), (D_MODEL, BF16), (2 * D_MODEL, BF16)], tile=256,
                                       window=(2, W_IN_GATES, W_IN_COLS, None))

    def epi_do(acc, xs, outs):
        for h in range(MLA_HEADS):
            lo = h * HEAD_PAD
            do = acc[:, lo:lo + HEAD_PAD]
            delta = jnp.sum(do * xs[0][:, lo:lo + HEAD_PAD].astype(F32), axis=-1, keepdims=True)
            outs[0][:, lo:lo + HEAD_PAD] = _put_stats(do.astype(BF16), delta, STAT_COL_V)

    do_aug, = _mm("mla_out_dx", dmla_out, wl["w_mla_out"], "nt", [(HW, BF16)], epi_do, row_x=[o_att], tn=HW)
    g_mla_out = _mm_plain("mla_out_dw", o_att, dmla_out, "tn")
    dq_att, dk_att, dkv_cat = _flash_bwd(q_aug, k_att, v_att, do_aug)

    def k_rope_bwd(dq_ref, dk_ref, c_ref, sa_ref, sb_ref, dqr_ref, dkv_ref, dkr_ref):
        c, sa, sb = c_ref[...], sa_ref[...], sb_ref[...]
        lane = lax.broadcasted_iota(jnp.int32, c.shape, 1)
        nope = (lane < QK_NOPE).astype(F32)
        ropem = ((lane >= QK_NOPE) & (lane < QK_NOPE + QK_ROPE)).astype(F32)
        dkr = jnp.zeros(c.shape, F32)
        for h in range(MLA_HEADS):
            lo = h * HEAD_PAD
            dqr_ref[:, lo:lo + HEAD_PAD] = (_rope(dq_ref[:, lo:lo + HEAD_PAD], c, sa, sb, -1.0) * qk_scale).astype(BF16)
            dk = dk_ref[:, lo:lo + HEAD_PAD].astype(F32)
            dkv_ref[:, lo:lo + HEAD_PAD] = (dk * nope).astype(BF16)
            dkr = dkr + dk
        dkr_ref[...] = (_rope(dkr * ropem, c, sa, sb, -1.0) * ropem).astype(BF16)

    dq_raw, dkv_cat, dkr = _rows("rope_bwd", k_rope_bwd, [dq_att, dk_att, rope_c, rope_sa, rope_sb], [],
                                 [(HW, BF16), (HW, BF16), (HEAD_PAD, BF16)], tile=256,
                                 window=(1, (0, HW), 2 * HW, dkv_cat))
    g_uq = _mm_plain("q_up_dw", qn, dq_raw, "tn")
    dqn = _mm_plain("q_up_dx", dq_raw, wl["w_uq"], "nt")
    g_ukv = _mm_plain("kv_up_dw", kvn, dkv_cat, "tn")
    dkvn = _mm_plain("kv_up_dx", dkv_cat, wl["w_ukv"], "nt")

    def k_lora_bwd(cq_ref, ckv_ref, dqn_ref, dkvn_ref, dkr_ref, gq_ref, gkv_ref, out_ref, dgq_ref, dgkv_ref):
        dcq, dgq = _rms_bwd(cq_ref[...], gq_ref[...], dqn_ref[...])
        dckv, dgkv = _rms_bwd(ckv_ref[...], gkv_ref[...], dkvn_ref[...])
        out_ref[:, 0:Q_LORA] = dcq.astype(BF16)
        out_ref[:, Q_LORA:Q_LORA + HEAD_PAD] = dkr_ref[...]
        out_ref[:, Q_LORA + HEAD_PAD:] = dckv.astype(BF16)
        _accum(dgq_ref, dgq)
        _accum(dgkv_ref, dgkv)

    dproj, g_q_norm, g_kv_norm = _rows("lora_norm_bwd", k_lora_bwd, [c_q, c_kv, dqn, dkvn, dkr],
                                       [sm["q_norm_g"], sm["kv_norm_g"]], [(W_IN_LORA[1], BF16)],
                                       [(1, Q_LORA), (1, KV_LORA)], window=(0, W_IN_LORA, W_IN_COLS, dproj))

    dz3 = _mm_plain("conv_out_dx", dconv_out, wl["w_conv_out"], "nt", dtype=BF16)
    g_conv_out = _mm_plain("conv_out_dw", z3, dconv_out, "tn")
    dz1, g_ln_g, g_ln_b, g_conv_b = _conv_bwd_norm(dz3, z1, sm["conv_ln_g"], sm["conv_ln_b"])
    dproj, g_conv_w = _conv_bwd_taps(dz1, z0, conv_in, conv_w, dproj, W_IN_CONV)

    g_in = _mm_plain("proj_dw", u1, dproj, "tn")
    token = late_grads_fn(dict(w_in=g_in, conv_w=g_conv_w[:CONV_WIDTH], w_conv_out=g_conv_out, w_uq=g_uq, w_ukv=g_ukv,
                               w_mla_out=g_mla_out, w_out=g_out))
    grad_x, g_norm_mix = dx_norm_bwd("proj_dx_norm", dproj, w_in, x, dh1, [sm["norm_mix_g"], token], with_bf16=False)

    small = dict(norm_mix_g=g_norm_mix, conv_b=g_conv_b, conv_ln_g=g_ln_g, conv_ln_b=g_ln_b, q_norm_g=g_q_norm,
                 kv_norm_g=g_kv_norm, norm_xattn_g=g_norm_xattn, norm_mem_g=g_norm_mem, norm_mlp_g=g_norm_mlp,
                 final_norm_g=g_final)
    return loss, grad_x, small


def _shard(a, k, axis):
    n = a.shape[axis] // N_CHIPS
    return lax.slice_in_dim(a, k * n, (k + 1) * n, axis=axis)


def _pack_small(grads, loss):
    flat = jnp.concatenate([grads[n].reshape(-1) for n in SMALL] + [loss.reshape(-1)[:1]])
    rows = -(-flat.shape[0] // (8 * LANES)) * 8
    return jnp.pad(flat, (0, rows * LANES - flat.shape[0])).reshape(rows, LANES)


def _pack_groups(shapes, names):
    groups = {}
    for n in names:
        groups.setdefault(shapes[n][1], []).append(n)
    return groups


def _pad_rows(a, mult):
    return jnp.pad(a, ((0, -a.shape[0] % mult), (0, 0)))


def _pack_grads(grads, shapes, names):
    packs = []
    for width, group in _pack_groups(shapes, names).items():
        per_chip = [jnp.concatenate([_pad_rows(_shard(grads[n], k, SHARD_AXIS[n]).astype(BF16), PACK_ROW_ALIGN) for n in group])
                    for k in range(N_CHIPS)]
        rows = per_chip[0].shape[0]
        packs.append(jnp.stack(per_chip).reshape(N_CHIPS, 2, rows // 2, width))
    return packs


def _unpack_grads(fulls, shapes, names):
    out = {}
    for full, group in zip(fulls, _pack_groups(shapes, names).values()):
        flat, at = full.reshape(-1, full.shape[-1]), 0
        for n in group:
            rows = shapes[n][0]
            out[n] = flat[at:at + rows]
            at += rows + (-rows % PACK_ROW_ALIGN)
    return out


def _unpack(flat, names, shapes):
    out, at = {}, 0
    for n in names:
        size = math.prod(shapes[n])
        out[n] = flat[at:at + size].reshape(shapes[n])
        at += size
    return out, at


def kernel(x, mem, positions, norm_mix_g, w_in, conv_w, conv_b, conv_ln_g, conv_ln_b, w_conv_out, q_norm_g, w_uq, kv_norm_g, w_ukv, w_mla_out, w_out, norm_xattn_g, norm_mem_g, w_xq, w_xkv, w_xo, norm_mlp_g, w_mlp1, w_mlp2, final_norm_g, loss_target, m_norm_mix_g, m_w_in, m_conv_w, m_conv_b, m_conv_ln_g, m_conv_ln_b, m_w_conv_out, m_q_norm_g, m_w_uq, m_kv_norm_g, m_w_ukv, m_w_mla_out, m_w_out, m_norm_xattn_g, m_norm_mem_g, m_w_xq, m_w_xkv, m_w_xo, m_norm_mlp_g, m_w_mlp1, m_w_mlp2, m_final_norm_g, v_norm_mix_g, v_w_in, v_conv_w, v_conv_b, v_conv_ln_g, v_conv_ln_b, v_w_conv_out, v_q_norm_g, v_w_uq, v_kv_norm_g, v_w_ukv, v_w_mla_out, v_w_out, v_norm_xattn_g, v_norm_mem_g, v_w_xq, v_w_xkv, v_w_xo, v_norm_mlp_g, v_w_mlp1, v_w_mlp2, v_final_norm_g):
    args = dict(locals())
    w = {n: args[n] for n in WEIGHTS}
    m = {n: args["m_" + n] for n in WEIGHTS}
    v = {n: args["v_" + n] for n in WEIGHTS}

    shards = [w[n][0].astype(F32 if n == "conv_w" else BF16) for n in BIG]
    bounds = (0,) + WEIGHT_WAITS + (len(BIG),)
    spans = [slice(lo, hi) for lo, hi in zip(bounds[:-1], bounds[1:])]
    sem_pairs, shards_thru, lands_thru, token = _gather_start(shards, [list(range(len(BIG)))[sp] for sp in spans])

    def unshard(n, g):
        ax = SHARD_AXIS[n]
        return jnp.moveaxis(g, 0, ax).reshape(g.shape[1:1 + ax] + (N_CHIPS * g.shape[1 + ax],) + g.shape[2 + ax:])

    def wait_fn(i):
        def fn(after):
            lands = _gather_wait(f"gather_weights_wait_{i}", sem_pairs[i], shards_thru[spans[i]], lands_thru[spans[i]], after)
            full = {n: unshard(n, g) for n, g in zip(BIG[spans[i]], lands)}
            return (_layout_w_in(full["w_in"]), full["conv_w"]) if i == 0 else _layout_weights(full)
        return fn

    sm = {n: w[n].reshape(1, -1) for n in SMALL}
    sm["norm_mix_g"] = sm["norm_mix_g"] + token[0, 0]

    shapes = {n: w[n].shape[1:] if n in BIG else w[n].shape for n in WEIGHTS}
    late_names = [n for n in BIG if n not in EARLY_GRADS]
    inflight = {}

    def send_grads(tag, names):
        def fn(g):
            packs = _pack_grads(_unlayout_grads(g), shapes, names)
            got = _pair_exchange(f"grad_pair_exchange_{tag}", packs)
            pairs = [_pair_add(f"grad_pair_add_{tag}_{i}", p, r) for i, (p, r) in enumerate(zip(packs, got))]
            *inflight[tag], token = _chip_exchange_start(f"grad_chip_exchange_{tag}_start", pairs)
            return token
        return fn

    loss, grad_x, g_small = _local_step(x[0], mem[0], positions, loss_target[0], [wait_fn(i) for i in range(3)],
                                        send_grads("early", EARLY_GRADS), send_grads("late", late_names), sm)

    halves, counts = [], {}
    for tag in ("late", "early"):
        own, got = _chip_exchange_wait(f"grad_chip_exchange_{tag}_wait", *inflight[tag], grad_x)
        halves += [_chip_add(f"grad_chip_add_{tag}_{i}", p, g) for i, (p, g) in enumerate(zip(own, got))]
        counts[tag] = len(own)
    fulls = _pair_share(halves)
    g_sum = _unpack_grads(fulls[:counts["late"]], shapes, late_names)
    g_sum.update(_unpack_grads(fulls[counts["late"]:], shapes, EARLY_GRADS))
    small_flat = _sum_over_devices(_pack_small(g_small, loss)).reshape(-1)
    g_small, at = _unpack(small_flat, SMALL, shapes)
    g_sum.update(g_small)
    loss_sum = small_flat[at]

    out_g, out_d, out_m, out_v = [], [], [], []
    for n in WEIGHTS:
        g, d, nm, nv = _adamw("adamw_" + n, w[n], g_sum[n], m[n], v[n])
        out_g.append(g)
        out_d.append(d)
        out_m.append(nm)
        out_v.append(nv)
    return (loss_sum, grad_x[None], *out_g, *out_d, *out_m, *out_v)
```

```python
import math

import jax
import jax.numpy as jnp
from jax import lax
from jax.experimental import pallas as pl
from jax.experimental.pallas import tpu as pltpu

F32 = jnp.float32
BF16 = jnp.bfloat16
MESH = pl.DeviceIdType.MESH

D_MODEL = 1024
CONV_CH = 512
CONV_WIDTH = 31
MLA_HEADS = 8
QK_NOPE = 64
QK_ROPE = 32
V_DIM = 64
Q_LORA = 384
KV_LORA = 256
MEM_LEN = 256
X_HEADS = 4
X_HEAD_DIM = 128
D_FF = 4096
ROPE_THETA = 10000.0
EPS = 1e-6
HEAD_PAD = 128
STAT_COL_QK = QK_NOPE + QK_ROPE
STAT_COL_V = V_DIM
HALO = 32
N_CHIPS = 4
LANES = 128

ADAM_LR = 0.001
ADAM_B1 = 0.9
ADAM_B2 = 0.999
ADAM_EPS = 1e-08
ADAM_WD = 0.01
ADAM_STEP = 10

VMEM_LIMIT = 52 * 1024 * 1024
ROW_TILES = (1024, 512, 256, 128, 64, 32, 16)
PACK_ROW_ALIGN = 32
N_DEV = 8
NEG = -1e30

BIG = ["w_in", "conv_w", "w_conv_out", "w_uq", "w_ukv", "w_mla_out", "w_out", "w_xq", "w_xkv", "w_xo", "w_mlp1", "w_mlp2"]
WEIGHT_WAITS = (2, 5)
SHARD_AXIS = {"w_in": 1, "w_conv_out": 1, "w_uq": 1, "w_ukv": 1, "w_mla_out": 1, "w_out": 0, "w_xq": 0, "w_xkv": 0,
              "w_xo": 1, "w_mlp1": 1, "w_mlp2": 0, "conv_w": 1}
EARLY_GRADS = ["w_mlp1", "w_mlp2", "w_xkv", "w_xq", "w_xo"]
SMALL = ["norm_mix_g", "conv_b", "conv_ln_g", "conv_ln_b", "q_norm_g", "kv_norm_g", "norm_xattn_g", "norm_mem_g",
         "norm_mlp_g", "final_norm_g"]
WEIGHTS = ["norm_mix_g", "w_in", "conv_w", "conv_b", "conv_ln_g", "conv_ln_b", "w_conv_out", "q_norm_g", "w_uq",
           "kv_norm_g", "w_ukv", "w_mla_out", "w_out", "norm_xattn_g", "norm_mem_g", "w_xq", "w_xkv", "w_xo",
           "norm_mlp_g", "w_mlp1", "w_mlp2", "final_norm_g"]


def _pick(n, prefs):
    for p in prefs:
        if n % p == 0:
            return p
    return n


def _params(sem):
    return pltpu.CompilerParams(dimension_semantics=sem, vmem_limit_bytes=VMEM_LIMIT)


_DIMS = {"nn": (((1,), (0,)), ((), ())), "nt": (((1,), (1,)), ((), ())), "tn": (((0,), (0,)), ((), ()))}


def _mm(name, a, b, mode, outs, epi, row_x=(), tile_x=(), vec_x=(), sums=(), tm=None, tn=None, tk=None, b_cols=None):
    if mode == "nn":
        (M, K), (_, N) = a.shape, b.shape
        if b_cols is not None:
            N = b_cols[1]
    elif mode == "nt":
        (M, K), (N, _) = a.shape, b.shape
    else:
        (K, M), (_, N) = a.shape, b.shape
    tm = tm or _pick(M, (1024, 512, 384, 256, 128))
    tn = tn or _pick(N, (1024, 768, 512, 384, 256, 128))
    tk = tk or _pick(K, (2048, 1920, 1024, 768, 512, 384, 256, 128))
    nk = K // tk
    rows_inner = nk == 1 and N // tn > 1
    grid = (N // tn, M // tm, nk) if rows_inner else (M // tm, N // tn, nk)

    def spec(shape, f):
        return pl.BlockSpec(shape, (lambda j, i, k: f(i, j, k)) if rows_inner else f)

    b_off = 0
    if b_cols is not None:
        assert mode == "nn" and b_cols[0] % tn == 0, (name, b_cols, tn)
        b_off = b_cols[0] // tn
    a_spec = spec((tk, tm), lambda i, j, k: (k, i)) if mode == "tn" else spec((tm, tk), lambda i, j, k: (i, k))
    b_spec = spec((tn, tk), lambda i, j, k: (j, k)) if mode == "nt" else spec((tk, tn), lambda i, j, k: (k, j + b_off))
    in_specs = [a_spec, b_spec]
    in_specs += [spec((tm, r.shape[1]), lambda i, j, k: (i, 0)) for r in row_x]
    in_specs += [spec((tm, tn), lambda i, j, k: (i, j)) for _ in tile_x]
    in_specs += [spec(v.shape, lambda i, j, k: (0, 0)) for v in vec_x]
    out_specs, out_shape = [], []
    for w, dt in outs:
        if tn == N:
            out_specs.append(spec((tm, w), lambda i, j, k: (i, 0)))
        else:
            assert w == N, (name, w, N)
            out_specs.append(spec((tm, tn), lambda i, j, k: (i, j)))
        out_shape.append(jax.ShapeDtypeStruct((M, w), dt))
    for shp in sums:
        assert tn == N and not rows_inner, name
        out_specs.append(spec(shp, lambda i, j, k: (0, 0)))
        out_shape.append(jax.ShapeDtypeStruct(shp, F32))
    nx = len(row_x) + len(tile_x) + len(vec_x)
    dims = _DIMS[mode]

    def body(a_ref, b_ref, *rest):
        x_refs, out_refs, acc_ref = rest[:nx], rest[nx:nx + len(outs) + len(sums)], rest[-1]
        av, bv = a_ref[...], b_ref[...]
        if av.dtype != BF16:
            av = av.astype(BF16)
        if bv.dtype != BF16:
            bv = bv.astype(BF16)
        prod = lax.dot_general(av, bv, dims, preferred_element_type=F32)
        if nk == 1:
            acc_ref[...] = prod
            epi(acc_ref, x_refs, out_refs)
        else:
            k = pl.program_id(2)

            @pl.when(k == 0)
            def _():
                acc_ref[...] = prod

            @pl.when(k > 0)
            def _():
                acc_ref[...] += prod

            @pl.when(k == nk - 1)
            def _():
                epi(acc_ref, x_refs, out_refs)

    res = pl.pallas_call(
        body, name=name, grid=grid, in_specs=in_specs, out_specs=out_specs, out_shape=out_shape,
        scratch_shapes=[pltpu.VMEM((tm, tn), F32)],
        compiler_params=_params(("arbitrary",) * 3 if sums else ("parallel", "parallel", "arbitrary")),
    )(a, b, *row_x, *tile_x, *vec_x)
    return res


def _epi_store(acc_ref, x_refs, out_refs):
    for o in out_refs:
        o[...] = acc_ref[...].astype(o.dtype)


def _mm_plain(name, a, b, mode, dtype=F32, **kw):
    n = kw["b_cols"][1] if kw.get("b_cols") else (b.shape[0] if mode == "nt" else b.shape[1])
    return _mm(name, a, b, mode, [(n, dtype)], _epi_store, **kw)[0]


def _rows(name, body, row_ins, vec_ins, row_outs, acc_outs=(), tile=512, window=None):
    S = row_ins[0].shape[0]
    t = _pick(S, (tile, 256, 128, 64, 32, 16, 8))
    in_specs = [pl.BlockSpec((t, r.shape[1]), lambda i: (i, 0)) for r in row_ins]
    in_specs += [pl.BlockSpec(v.shape, lambda i: (0, 0)) for v in vec_ins]
    out_specs = [pl.BlockSpec((t, w), lambda i: (i, 0)) for w, _ in row_outs]
    out_specs += [pl.BlockSpec(shp, lambda i: (0, 0)) for shp in acc_outs]
    out_shape = [jax.ShapeDtypeStruct((S, w), dt) for w, dt in row_outs]
    out_shape += [jax.ShapeDtypeStruct(shp, F32) for shp in acc_outs]
    extra, aliases, n_in = [], {}, len(row_ins) + len(vec_ins)
    if window is not None:
        k, (first, width), total, buffer = window
        assert first % width == 0 and row_outs[k][0] == width, (name, window[:3])
        out_specs[k] = pl.BlockSpec((t, width), lambda i: (i, first // width))
        out_shape[k] = jax.ShapeDtypeStruct((S, total), row_outs[k][1])
        if buffer is not None:
            extra, aliases = [buffer], {n_in: k}
            in_specs.append(pl.BlockSpec(memory_space=pl.ANY))

    def call(*refs):
        body(*refs[:n_in], *refs[n_in + len(extra):])

    sem = ("arbitrary",) if acc_outs else ("parallel",)
    return pl.pallas_call(
        call, name=name, grid=(S // t,), in_specs=in_specs, out_specs=out_specs,
        out_shape=out_shape, input_output_aliases=aliases, compiler_params=_params(sem),
    )(*row_ins, *vec_ins, *extra)


def _accum(ref, val, first=True):
    if first:
        @pl.when(pl.program_id(0) == 0)
        def _():
            ref[...] = jnp.zeros_like(ref)

    ref[...] += val


EPILOGUE_ROWS = 256


def _row_chunks(n):
    step = min(EPILOGUE_ROWS, n)
    return [slice(r, r + step) for r in range(0, n, step)]


def _colsum(v):
    return jnp.sum(v, axis=0, keepdims=True)


def _rms_fwd(x, g):
    r = lax.rsqrt(jnp.mean(x * x, axis=-1, keepdims=True) + EPS)
    return x * r * g


def _rms_bwd(x, g, du):
    r = lax.rsqrt(jnp.mean(x * x, axis=-1, keepdims=True) + EPS)
    xn = x * r
    gdu = du * g
    dx = r * (gdu - xn * jnp.mean(xn * gdu, axis=-1, keepdims=True))
    return dx, _colsum(du * xn)


def _sigmoid(v):
    return 1.0 / (1.0 + jnp.exp(-v))


def _rope(v, c, sa, sb, sign):
    return v * c + sign * (pltpu.roll(v, HEAD_PAD - QK_ROPE // 2, 1) * sa + pltpu.roll(v, QK_ROPE // 2, 1) * sb)


def _split3(v):
    hi = v.astype(BF16)
    r1 = v - hi.astype(F32)
    mid = r1.astype(BF16)
    lo = (r1 - mid.astype(F32)).astype(BF16)
    return hi, mid, lo


def _put_stats(base, stat, col):
    hi, mid, lo = _split3(stat)
    lane = lax.broadcasted_iota(jnp.int32, base.shape, 1)
    out = jnp.where(lane == col, hi, base)
    out = jnp.where(lane == col + 1, mid, out)
    return jnp.where(lane == col + 2, lo, out)


def _neg_ones(shape, col):
    lane = lax.broadcasted_iota(jnp.int32, shape, 1)
    return jnp.where((lane >= col) & (lane < col + 3), -1.0, 0.0).astype(F32)


def _shifted(ext, t):
    p = ext.shape[0]
    for b in range(8):
        rb = ext if b == 0 else pltpu.roll(ext, p - b, 0)
        for a in range(HALO // 8 + 1):
            if 8 * a + b <= HALO:
                yield 8 * a + b, rb[8 * a:8 * a + t]


def _conv_fwd(z0, conv_w, conv_b, ln_g, ln_b):
    S, C = z0.shape
    t = _pick(S, (512, 256, 128, 64, 32))
    per = t // HALO

    def body(cur_ref, prev_ref, w_ref, b_ref, g_ref, beta_ref, z1_ref, z3_ref, ext_ref):
        i = pl.program_id(0)
        ext_ref[0:HALO, :] = jnp.where(i > 0, prev_ref[...].astype(F32), 0.0)
        ext_ref[HALO:, :] = cur_ref[...].astype(F32)
        ext = ext_ref[...]
        acc = jnp.zeros((t, C), F32)
        for d, win in _shifted(ext, t):
            k = d - (HALO - CONV_WIDTH + 1)
            if 0 <= k < CONV_WIDTH:
                acc = acc + win * w_ref[k:k + 1, :]
        z1 = acc + b_ref[...]
        z1_ref[...] = z1
        mu = jnp.mean(z1, axis=-1, keepdims=True)
        zc = z1 - mu
        rs = lax.rsqrt(jnp.mean(zc * zc, axis=-1, keepdims=True) + EPS)
        z2 = zc * rs * g_ref[...] + beta_ref[...]
        z3_ref[...] = (z2 * _sigmoid(z2)).astype(BF16)

    vec = lambda v: pl.BlockSpec(v.shape, lambda i: (0, 0))
    return pl.pallas_call(
        body, name="conv_fwd", grid=(S // t,),
        in_specs=[pl.BlockSpec((t, C), lambda i: (i, 0)),
                  pl.BlockSpec((HALO, C), lambda i: (jnp.maximum(i * per - 1, 0), 0)),
                  vec(conv_w), vec(conv_b), vec(ln_g), vec(ln_b)],
        out_specs=[pl.BlockSpec((t, C), lambda i: (i, 0)), pl.BlockSpec((t, C), lambda i: (i, 0))],
        out_shape=[jax.ShapeDtypeStruct((S, C), F32), jax.ShapeDtypeStruct((S, C), BF16)],
        scratch_shapes=[pltpu.VMEM((t + HALO, C), F32)],
        compiler_params=_params(("parallel",)),
    )(z0, z0, conv_w, conv_b, ln_g, ln_b)


def _conv_bwd_norm(dz3, z1, ln_g, ln_b):
    C = z1.shape[1]

    def body(dz3_ref, z1_ref, g_ref, beta_ref, dz1_ref, dg_ref, dbeta_ref, dbias_ref):
        z1 = z1_ref[...]
        mu = jnp.mean(z1, axis=-1, keepdims=True)
        zc = z1 - mu
        rs = lax.rsqrt(jnp.mean(zc * zc, axis=-1, keepdims=True) + EPS)
        xh = zc * rs
        z2 = xh * g_ref[...] + beta_ref[...]
        sg = _sigmoid(z2)
        dz2 = dz3_ref[...].astype(F32) * (sg * (1.0 + z2 * (1.0 - sg)))
        dxh = dz2 * g_ref[...]
        dz1 = rs * (dxh - jnp.mean(dxh, axis=-1, keepdims=True) - xh * jnp.mean(dxh * xh, axis=-1, keepdims=True))
        dz1_ref[...] = dz1
        _accum(dg_ref, _colsum(dz2 * xh))
        _accum(dbeta_ref, _colsum(dz2))
        _accum(dbias_ref, _colsum(dz1))

    return _rows("conv_bwd_norm", body, [dz3, z1], [ln_g, ln_b], [(C, F32)], [(1, C)] * 3)


def _conv_bwd_taps(dz1, z0, conv_in, conv_w, dproj, window):
    S, C = z0.shape
    t = _pick(S, (512, 256, 128, 64, 32))
    per = t // HALO
    last = S // HALO - 1
    nt = S // t
    assert window[1] == 2 * C and window[0] % window[1] == 0, window

    def body(dcur_ref, dnext_ref, zcur_ref, zprev_ref, cin_ref, w_ref, _, dcin_ref, dw_ref, dext_ref, zext_ref):
        i = pl.program_id(0)
        dcur = dcur_ref[...]
        dext_ref[0:t, :] = dcur
        dext_ref[t:, :] = jnp.where(i < nt - 1, dnext_ref[...], 0.0)
        zext_ref[0:HALO, :] = jnp.where(i > 0, zprev_ref[...].astype(F32), 0.0)
        zext_ref[HALO:, :] = zcur_ref[...].astype(F32)

        @pl.when(i == 0)
        def _():
            dw_ref[...] = jnp.zeros_like(dw_ref)

        dz0 = jnp.zeros((t, C), F32)
        for d, win in _shifted(dext_ref[...], t):
            k = CONV_WIDTH - 1 - d
            if 0 <= k < CONV_WIDTH:
                dz0 = dz0 + win * w_ref[k:k + 1, :]
        for d, win in _shifted(zext_ref[...], t):
            k = d - (HALO - CONV_WIDTH + 1)
            if 0 <= k < CONV_WIDTH:
                dw_ref[k:k + 1, :] += _colsum(dcur * win)
        a = cin_ref[:, 0:C].astype(F32)
        sg = _sigmoid(cin_ref[:, C:2 * C].astype(F32))
        dcin_ref[:, 0:C] = (dz0 * sg).astype(BF16)
        dcin_ref[:, C:2 * C] = (dz0 * a * sg * (1.0 - sg)).astype(BF16)

    return pl.pallas_call(
        body, name="conv_bwd_taps", grid=(nt,),
        in_specs=[pl.BlockSpec((t, C), lambda i: (i, 0)),
                  pl.BlockSpec((HALO, C), lambda i: (jnp.minimum((i + 1) * per, last), 0)),
                  pl.BlockSpec((t, C), lambda i: (i, 0)),
                  pl.BlockSpec((HALO, C), lambda i: (jnp.maximum(i * per - 1, 0), 0)),
                  pl.BlockSpec((t, 2 * C), lambda i: (i, 0)),
                  pl.BlockSpec(conv_w.shape, lambda i: (0, 0)),
                  pl.BlockSpec(memory_space=pl.ANY)],
        out_specs=[pl.BlockSpec((t, 2 * C), lambda i: (i, window[0] // window[1])), pl.BlockSpec((HALO, C), lambda i: (0, 0))],
        out_shape=[jax.ShapeDtypeStruct(dproj.shape, BF16), jax.ShapeDtypeStruct((HALO, C), F32)],
        input_output_aliases={6: 0},
        scratch_shapes=[pltpu.VMEM((t + HALO, C), F32), pltpu.VMEM((t + HALO, C), F32)],
        compiler_params=_params(("arbitrary",)),
    )(dz1, dz1, z0, z0, conv_in, conv_w, dproj)


def _lower_tri(shape, rows_are_queries):
    row = lax.broadcasted_iota(jnp.int32, shape, 0)
    col = lax.broadcasted_iota(jnp.int32, shape, 1)
    return (col <= row) if rows_are_queries else (row <= col)


HEADS_PER_STEP = 2
FWD_HEADS_PER_STEP = 2
FWD_KEY_TILES = 8


def _flash_specs(S, t, heads):
    w = heads * HEAD_PAD
    blk = pl.BlockSpec((t, w), lambda h, i: (i, h))
    head = pl.BlockSpec((S, w), lambda h, i: (0, h))
    return blk, head


def _head_lanes(g):
    return slice(g * HEAD_PAD, (g + 1) * HEAD_PAD)


def _dot_nt(a, b):
    return lax.dot_general(a, b, _DIMS["nt"], preferred_element_type=F32)


def _dot_nn(a, b):
    return lax.dot_general(a, b, _DIMS["nn"], preferred_element_type=F32)


def _dot_tn(a, b):
    return lax.dot_general(a, b, _DIMS["tn"], preferred_element_type=F32)


def _flash_fwd(q, k, v):
    S = q.shape[0]
    t = _pick(S, (512, 256, 128))

    def body(q_ref, k_ref, v_ref, o_ref, qa_ref, m_ref, acc_ref):
        qi = pl.program_id(1)
        m_ref[...] = jnp.full_like(m_ref, NEG)
        acc_ref[...] = jnp.zeros_like(acc_ref)

        def step(first, tiles, diag):
            width = tiles * t
            rows = pl.ds(pl.multiple_of(first, t), width)
            for g in range(FWD_HEADS_PER_STEP):
                hl = _head_lanes(g)
                s = _dot_nt(q_ref[:, hl], k_ref[rows, hl])
                if diag:
                    row = lax.broadcasted_iota(jnp.int32, s.shape, 0)
                    col = lax.broadcasted_iota(jnp.int32, s.shape, 1)
                    s = jnp.where(col <= row + (tiles - 1) * t, s, NEG)
                m_old = m_ref[g]
                m_new = jnp.maximum(m_old, jnp.max(s, axis=-1, keepdims=True))
                p = jnp.exp(s - m_new).astype(BF16)
                acc_ref[g] = jnp.exp(m_old - m_new) * acc_ref[g] + _dot_nn(p, v_ref[rows, hl])
                m_ref[g] = m_new

        def wide(kb, carry):
            step(kb * (FWD_KEY_TILES * t), FWD_KEY_TILES, False)
            return carry

        full_groups = qi // FWD_KEY_TILES
        lax.fori_loop(0, full_groups, wide, 0)
        for tiles in range(1, min(FWD_KEY_TILES, S // t) + 1):
            @pl.when(qi - full_groups * FWD_KEY_TILES == tiles - 1)
            def _():
                step(full_groups * (FWD_KEY_TILES * t), tiles, True)

        for g in range(FWD_HEADS_PER_STEP):
            hl = _head_lanes(g)
            acc = acc_ref[g]
            l = -acc[:, STAT_COL_V:STAT_COL_V + 1]
            o_ref[:, hl] = (acc / l).astype(BF16)
            qa_ref[:, hl] = _put_stats(q_ref[:, hl], m_ref[g] + jnp.log(l), STAT_COL_QK)

    blk, head = _flash_specs(S, t, FWD_HEADS_PER_STEP)
    return pl.pallas_call(
        body, name="mla_flash_fwd", grid=(MLA_HEADS // FWD_HEADS_PER_STEP, S // t),
        in_specs=[blk, head, head], out_specs=[blk, blk],
        out_shape=[jax.ShapeDtypeStruct(q.shape, BF16), jax.ShapeDtypeStruct(q.shape, BF16)],
        scratch_shapes=[pltpu.VMEM((FWD_HEADS_PER_STEP, t, 1), F32), pltpu.VMEM((FWD_HEADS_PER_STEP, t, HEAD_PAD), F32)],
        compiler_params=_params(("parallel", "arbitrary")),
    )(q, k, v)


def _flash_bwd(qa, k, v, doa):
    S = qa.shape[0]
    t = _pick(S, (1024, 512, 256, 128))
    n = S // t
    half = t // 2

    def body(qa_ref, k_ref, v_ref, do_ref, dq_ref, dk_ref, dv_ref, dk_acc, dv_acc):
        kj = pl.program_id(1)

        @pl.when(kj == 0)
        def _():
            dq_ref[...] = jnp.zeros_like(dq_ref)

        dk_acc[...] = jnp.zeros_like(dk_acc)
        dv_acc[...] = jnp.zeros_like(dv_acc)

        def step(q_first, q_len, keys, diag):
            rows = pl.ds(pl.multiple_of(q_first, q_len), q_len)
            for g in range(HEADS_PER_STEP):
                hl = _head_lanes(g)
                qa, do, kk = qa_ref[rows, hl], do_ref[rows, hl], k_ref[keys, hl]
                st = _dot_nt(kk, qa)
                if diag:
                    st = jnp.where(_lower_tri(st.shape, False), st, NEG)
                pt = jnp.exp(st)
                dst = (pt * _dot_nt(v_ref[keys, hl], do)).astype(BF16)
                dv_acc[keys, hl] += _dot_nn(pt.astype(BF16), do)
                dk_acc[keys, hl] += _dot_nn(dst, qa)
                dq_ref[rows, hl] += _dot_tn(dst, kk)

        def loop(qi, carry):
            step(qi * t, t, slice(0, t), False)
            return carry

        lo, hi = slice(0, half), slice(half, t)
        step(kj * t, half, lo, True)
        step(kj * t + half, half, lo, False)
        step(kj * t + half, half, hi, True)
        lax.fori_loop(kj + 1, n, loop, 0)
        dk_ref[...] = dk_acc[...].astype(BF16)
        dv_ref[...] = dv_acc[...].astype(BF16)

    blk, head = _flash_specs(S, t, HEADS_PER_STEP)
    w = HEADS_PER_STEP * HEAD_PAD
    return pl.pallas_call(
        body, name="mla_flash_bwd", grid=(MLA_HEADS // HEADS_PER_STEP, n),
        in_specs=[head, blk, blk, head],
        out_specs=[head, blk, pl.BlockSpec((t, w), lambda h, i: (i, h + qa.shape[1] // w))],
        out_shape=[jax.ShapeDtypeStruct(qa.shape, F32), jax.ShapeDtypeStruct(qa.shape, BF16),
                   jax.ShapeDtypeStruct((S, 2 * qa.shape[1]), BF16)],
        scratch_shapes=[pltpu.VMEM((t, w), F32), pltpu.VMEM((t, w), F32)],
        compiler_params=_params(("parallel", "arbitrary")),
    )(qa, k, v, doa)


def _xattn_fwd(xq, kvx):
    W = X_HEADS * X_HEAD_DIM

    def body(q_ref, kv_ref, o_ref):
        for h in range(X_HEADS):
            lo = h * X_HEAD_DIM
            s = _dot_nt(q_ref[:, lo:lo + X_HEAD_DIM], kv_ref[:, lo:lo + X_HEAD_DIM])
            p = jnp.exp(s - jnp.max(s, axis=-1, keepdims=True))
            p = p / jnp.sum(p, axis=-1, keepdims=True)
            o_ref[:, lo:lo + X_HEAD_DIM] = _dot_nn(p.astype(BF16), kv_ref[:, W + lo:W + lo + X_HEAD_DIM]).astype(BF16)

    return _rows("xattn_fwd", body, [xq], [kvx], [(W, BF16)])[0]


def _xattn_bwd(xq, kvx, dox):
    W = X_HEADS * X_HEAD_DIM
    scale = X_HEAD_DIM ** -0.5

    def body(q_ref, do_ref, kv_ref, dq_ref, dkv_ref):
        @pl.when(pl.program_id(0) == 0)
        def _():
            dkv_ref[...] = jnp.zeros_like(dkv_ref)

        for h in range(X_HEADS):
            lo = h * X_HEAD_DIM
            q, k = q_ref[:, lo:lo + X_HEAD_DIM], kv_ref[:, lo:lo + X_HEAD_DIM]
            v, do = kv_ref[:, W + lo:W + lo + X_HEAD_DIM], do_ref[:, lo:lo + X_HEAD_DIM]
            s = _dot_nt(q, k)
            p = jnp.exp(s - jnp.max(s, axis=-1, keepdims=True))
            p = p / jnp.sum(p, axis=-1, keepdims=True)
            dp = _dot_nt(do, v)
            ds = (p * (dp - jnp.sum(dp * p, axis=-1, keepdims=True))).astype(BF16)
            dq_ref[:, lo:lo + X_HEAD_DIM] = (_dot_nn(ds, k) * scale).astype(BF16)
            dkv_ref[:, lo:lo + X_HEAD_DIM] += _dot_tn(ds, q)
            dkv_ref[:, W + lo:W + lo + X_HEAD_DIM] += _dot_tn(p.astype(BF16), do)

    return _rows("xattn_bwd", body, [xq, dox], [kvx], [(W, BF16)], [kvx.shape])


def _adamw(name, w, g, m, v):
    c1 = 1.0 / (1.0 - ADAM_B1 ** ADAM_STEP)
    c2 = 1.0 / (1.0 - ADAM_B2 ** ADAM_STEP)
    lead = (0,) * (w.ndim - 2)
    w2 = w.reshape((1,) * (2 - w.ndim) + w.shape) if w.ndim < 2 else w
    m2, v2 = m.reshape(w2.shape), v.reshape(w2.shape)
    g2 = g.reshape(w2.shape[-2:])
    R, C = g2.shape
    t = _pick(R, (256, 128, 64, 32, 16, 8))

    def body(w_ref, g_ref, m_ref, v_ref, go_ref, d_ref, nm_ref, nv_ref):
        g = g_ref[...]
        nm = ADAM_B1 * m_ref[lead] + (1.0 - ADAM_B1) * g
        nv = ADAM_B2 * v_ref[lead] + (1.0 - ADAM_B2) * (g * g)
        go_ref[lead] = g
        d_ref[lead] = -ADAM_LR * ((nm * c1) / (jnp.sqrt(nv * c2) + ADAM_EPS) + ADAM_WD * w_ref[lead])
        nm_ref[lead] = nm
        nv_ref[lead] = nv

    full = pl.BlockSpec((1,) * len(lead) + (t, C), lambda i: lead + (i, 0))
    outs = pl.pallas_call(
        body, name=name, grid=(R // t,), in_specs=[full, pl.BlockSpec((t, C), lambda i: (i, 0)), full, full],
        out_specs=[full] * 4, out_shape=[jax.ShapeDtypeStruct(w2.shape, F32)] * 4, compiler_params=_params(("parallel",)),
    )(w2, g2, m2, v2)
    return [o.reshape(w.shape) for o in outs]


def _place():
    x, y, c = lax.axis_index("x"), lax.axis_index("y"), lax.axis_index("c")
    return x, y, c, [(1 - x, y), (x, 1 - y), (1 - x, 1 - y)]


_ANY = pl.BlockSpec(memory_space=pl.ANY)


_HBM = pl.BlockSpec(memory_space=pltpu.HBM)
_SEM = pl.BlockSpec(memory_space=pltpu.SEMAPHORE)
_SIDE_EFFECT = pltpu.SideEffectType.DATAFLOW_SIDE_EFFECTING


def _gather_copy(src, land, slot, send, recv, k, chip, c):
    return pltpu.make_async_remote_copy(src_ref=src, dst_ref=land.at[slot], send_sem=send.at[k], recv_sem=recv.at[k],
                                        device_id=(chip[0], chip[1], c), device_id_type=MESH)


def _gather_start(shards, groups):
    n, ng = len(shards), len(groups)
    mine = 2 * lax.axis_index("x") + lax.axis_index("y")
    lands = [lax.dynamic_update_slice(lax.empty((N_CHIPS,) + s.shape, s.dtype), s[None], (mine,) + (0,) * s.ndim)
             for s in shards]

    def body(*refs):
        ins, lnd = refs[:n], refs[n:2 * n]
        sends, recvs = refs[2 * n:2 * n + ng], refs[2 * n + ng:2 * n + 2 * ng]
        token = refs[-1]
        x, y, c, chips = _place()
        for gi, group in enumerate(groups):
            for pos, w in enumerate(group):
                for j, chip in enumerate(chips):
                    _gather_copy(ins[w], lnd[w], 2 * x + y, sends[gi], recvs[gi], 3 * pos + j, chip, c).start()
        token[...] = jnp.zeros_like(token)

    sems = [pltpu.SemaphoreType.DMA((3 * len(g),)) for g in groups]
    res = pl.pallas_call(
        body, name="gather_weights_start",
        out_shape=sems + sems + [pltpu.HBM(a.shape, a.dtype) for a in shards + lands] + [jax.ShapeDtypeStruct((8, LANES), F32)],
        in_specs=[_HBM] * (2 * n),
        out_specs=[_SEM] * (2 * ng) + [_HBM] * (2 * n) + [pl.BlockSpec(memory_space=pltpu.VMEM)],
        input_output_aliases={i: 2 * ng + i for i in range(2 * n)},
        compiler_params=pltpu.CompilerParams(has_side_effects=_SIDE_EFFECT),
    )(*[pltpu.with_memory_space_constraint(a, pltpu.HBM) for a in shards + lands])
    sem_pairs = list(zip(res[:ng], res[ng:2 * ng]))
    return sem_pairs, res[2 * ng:2 * ng + n], res[2 * ng + n:2 * ng + 2 * n], res[-1]


def _gather_wait(name, sem_pair, shards_thru, lands_thru, after):
    m = len(shards_thru)
    after = after if isinstance(after, (tuple, list)) else (after,)

    def body(*refs):
        ins, lnd = refs[:m], refs[m:2 * m]
        send, recv = refs[2 * m], refs[2 * m + 1]
        x, y, c, chips = _place()
        for pos in range(m):
            for j, chip in enumerate(chips):
                cp = _gather_copy(ins[pos], lnd[pos], 2 * chip[0] + chip[1], send, recv, 3 * pos + j, chip, c)
                cp.wait_send()
                cp.wait_recv()

    res = pl.pallas_call(
        body, name=name,
        out_shape=[pltpu.HBM(a.shape, a.dtype) for a in list(shards_thru) + list(lands_thru)],
        in_specs=[_HBM] * (2 * m) + [_SEM, _SEM] + [_ANY] * len(after), out_specs=[_HBM] * (2 * m),
        input_output_aliases={i: i for i in range(2 * m)},
        compiler_params=pltpu.CompilerParams(has_side_effects=_SIDE_EFFECT),
    )(*shards_thru, *lands_thru, *sem_pair, *after)
    return res[m:]


def _pair_exchange(name, packs):
    n = len(packs)

    def body(*refs):
        ins, outs, send, recv = refs[:n], refs[n:2 * n], refs[2 * n], refs[2 * n + 1]
        x, y, c, _ = _place()
        cps = []
        for g in range(n):
            cp = pltpu.make_async_remote_copy(src_ref=ins[g].at[:, pl.ds(1 - c, 1)], dst_ref=outs[g], send_sem=send.at[g],
                                              recv_sem=recv.at[g], device_id=(x, y, 1 - c), device_id_type=MESH)
            cp.start()
            cps.append(cp)
        for cp in cps:
            cp.wait()

    return pl.pallas_call(
        body, name=name, in_specs=[_ANY] * n, out_specs=[_ANY] * n,
        out_shape=[jax.ShapeDtypeStruct((N_CHIPS, 1) + p.shape[2:], p.dtype) for p in packs],
        scratch_shapes=[pltpu.SemaphoreType.DMA((n,)), pltpu.SemaphoreType.DMA((n,))],
    )(*packs)


def _chip_copy(src, land, src_slot, dst_slot, send, recv, k, chip, c):
    return pltpu.make_async_remote_copy(src_ref=src.at[src_slot], dst_ref=land.at[dst_slot], send_sem=send.at[k],
                                        recv_sem=recv.at[k], device_id=(chip[0], chip[1], c), device_id_type=MESH)


def _chip_exchange_start(name, parts):
    n = len(parts)
    lands = [lax.empty(p.shape, p.dtype) for p in parts]

    def body(*refs):
        ins, lnd, send, recv, token = refs[:n], refs[n:2 * n], refs[2 * n], refs[2 * n + 1], refs[-1]
        x, y, c, chips = _place()
        for g in range(n):
            for j, chip in enumerate(chips):
                _chip_copy(ins[g], lnd[g], 2 * chip[0] + chip[1], 2 * x + y, send, recv, 3 * g + j, chip, c).start()
        token[...] = jnp.zeros_like(token)

    sems = [pltpu.SemaphoreType.DMA((3 * n,))] * 2
    res = pl.pallas_call(
        body, name=name,
        out_shape=sems + [pltpu.HBM(a.shape, a.dtype) for a in list(parts) + lands] + [jax.ShapeDtypeStruct((8, LANES), F32)],
        in_specs=[_HBM] * (2 * n),
        out_specs=[_SEM] * 2 + [_HBM] * (2 * n) + [pl.BlockSpec(memory_space=pltpu.VMEM)],
        input_output_aliases={i: 2 + i for i in range(2 * n)},
        compiler_params=pltpu.CompilerParams(has_side_effects=_SIDE_EFFECT),
    )(*[pltpu.with_memory_space_constraint(a, pltpu.HBM) for a in list(parts) + lands])
    return res[:2], res[2:2 + n], res[2 + n:2 + 2 * n], res[-1]


def _chip_exchange_wait(name, sems, parts_thru, lands_thru, after):
    n = len(parts_thru)

    def body(*refs):
        ins, lnd, send, recv = refs[:n], refs[n:2 * n], refs[2 * n], refs[2 * n + 1]
        x, y, c, chips = _place()
        for g in range(n):
            for j, chip in enumerate(chips):
                cp = _chip_copy(ins[g], lnd[g], 2 * x + y, 2 * chip[0] + chip[1], send, recv, 3 * g + j, chip, c)
                cp.wait_send()
                cp.wait_recv()

    res = pl.pallas_call(
        body, name=name,
        out_shape=[pltpu.HBM(a.shape, a.dtype) for a in list(parts_thru) + list(lands_thru)],
        in_specs=[_HBM] * (2 * n) + [_SEM, _SEM, _ANY], out_specs=[_HBM] * (2 * n),
        input_output_aliases={i: i for i in range(2 * n)},
        compiler_params=pltpu.CompilerParams(has_side_effects=_SIDE_EFFECT),
    )(*parts_thru, *lands_thru, *sems, after)
    return res[:n], res[n:]


def _pair_share(halves):
    n = len(halves)

    def body(*refs):
        outs, send, recv = refs[n:2 * n], refs[2 * n], refs[2 * n + 1]
        x, y, c, _ = _place()
        cps = []
        for g in range(n):
            cp = pltpu.make_async_remote_copy(src_ref=outs[g].at[c], dst_ref=outs[g].at[c], send_sem=send.at[g],
                                              recv_sem=recv.at[g], device_id=(x, y, 1 - c), device_id_type=MESH)
            cp.start()
            cps.append(cp)
        for g in range(n):
            pltpu.make_async_remote_copy(src_ref=outs[g].at[c], dst_ref=outs[g].at[1 - c], send_sem=send.at[g],
                                         recv_sem=recv.at[g], device_id=(x, y, 1 - c), device_id_type=MESH).wait_recv()
        for cp in cps:
            cp.wait_send()

    return pl.pallas_call(
        body, name="grad_pair_share", in_specs=[_ANY] * n, out_specs=[_ANY] * n,
        out_shape=[jax.ShapeDtypeStruct(h.shape, h.dtype) for h in halves],
        input_output_aliases={g: g for g in range(n)},
        scratch_shapes=[pltpu.SemaphoreType.DMA((n,)), pltpu.SemaphoreType.DMA((n,))],
    )(*halves)


def _sum_over_devices(block):
    R, L = block.shape

    def gather(b_ref, o_ref, send, recv, loc):
        x, y, c, _ = _place()
        lc = pltpu.make_async_copy(b_ref, o_ref.at[4 * x + 2 * y + c], loc)
        lc.start()
        peers = [(1 - x if dx else x, 1 - y if dy else y, 1 - c if dc else c)
                 for dx in (0, 1) for dy in (0, 1) for dc in (0, 1) if dx or dy or dc]
        cps = []
        for j, peer in enumerate(peers):
            cp = pltpu.make_async_remote_copy(src_ref=b_ref, dst_ref=o_ref.at[4 * x + 2 * y + c], send_sem=send.at[j],
                                              recv_sem=recv.at[j], device_id=peer, device_id_type=MESH)
            cp.start()
            cps.append(cp)
        for j, (px, py, pc) in enumerate(peers):
            pltpu.make_async_remote_copy(src_ref=b_ref, dst_ref=o_ref.at[4 * px + 2 * py + pc], send_sem=send.at[j],
                                         recv_sem=recv.at[j], device_id=(px, py, pc), device_id_type=MESH).wait_recv()
        for cp in cps:
            cp.wait_send()
        lc.wait()

    blocks = pl.pallas_call(
        gather, name="small_grads_gather", in_specs=[_ANY], out_specs=_ANY,
        out_shape=jax.ShapeDtypeStruct((N_DEV, R, L), F32),
        scratch_shapes=[pltpu.SemaphoreType.DMA((N_DEV - 1,)), pltpu.SemaphoreType.DMA((N_DEV - 1,)), pltpu.SemaphoreType.DMA],
    )(block)

    def add(b_ref, o_ref):
        total = b_ref[0]
        for d in range(1, N_DEV):
            total = total + b_ref[d]
        o_ref[...] = total

    return pl.pallas_call(add, name="small_grads_add", out_shape=jax.ShapeDtypeStruct((R, L), F32))(blocks)


def _pair_add(name, pack, got):
    _, _, R, C = pack.shape
    t = _pick(R, ROW_TILES)
    c = lax.axis_index("c").astype(jnp.int32).reshape(1)

    def body(c_ref, p_ref, g_ref, o_ref):
        o_ref[...] = (p_ref[...].astype(F32) + g_ref[...].astype(F32)).astype(BF16)

    return pl.pallas_call(
        body, name=name,
        grid_spec=pltpu.PrefetchScalarGridSpec(
            num_scalar_prefetch=1, grid=(N_CHIPS, R // t),
            in_specs=[pl.BlockSpec((None, None, t, C), lambda k, i, c_ref: (k, c_ref[0], i, 0)),
                      pl.BlockSpec((None, None, t, C), lambda k, i, c_ref: (k, 0, i, 0))],
            out_specs=pl.BlockSpec((None, t, C), lambda k, i, c_ref: (k, i, 0))),
        out_shape=jax.ShapeDtypeStruct((N_CHIPS, R, C), BF16), compiler_params=_params(("parallel", "parallel")),
    )(c, pack, got)


def _chip_add(name, own, got):
    _, R, C = own.shape
    t = _pick(R, ROW_TILES)
    x, y, c, _ = _place()
    place = jnp.stack([c, 2 * x + y]).astype(jnp.int32)

    def body(place_ref, own_ref, g1_ref, g2_ref, g3_ref, o_ref):
        o_ref[...] = ((own_ref[...].astype(F32) + g1_ref[...].astype(F32)) + g2_ref[...].astype(F32)) + g3_ref[...].astype(F32)

    def other(d):
        return pl.BlockSpec((None, t, C), lambda i, place_ref: ((place_ref[1] + d) % N_CHIPS, i, 0))

    return pl.pallas_call(
        body, name=name,
        grid_spec=pltpu.PrefetchScalarGridSpec(
            num_scalar_prefetch=1, grid=(R // t,),
            in_specs=[pl.BlockSpec((None, t, C), lambda i, place_ref: (place_ref[1], i, 0)), other(1), other(2), other(3)],
            out_specs=pl.BlockSpec((None, t, C), lambda i, place_ref: (place_ref[0], i, 0))),
        out_shape=jax.ShapeDtypeStruct((2, R, C), F32), compiler_params=_params(("parallel",)),
    )(place, own, got, got, got)


_CUT = (2 * CONV_CH, 2 * CONV_CH + Q_LORA, 2 * CONV_CH + Q_LORA + KV_LORA, 2 * CONV_CH + Q_LORA + KV_LORA + QK_ROPE)
W_IN_GATES = (0, 2 * D_MODEL)
W_IN_CONV = (W_IN_GATES[1], 2 * CONV_CH)
W_IN_CQ = (W_IN_CONV[0] + W_IN_CONV[1], Q_LORA)
W_IN_KR = (W_IN_CQ[0] + W_IN_CQ[1], HEAD_PAD)
W_IN_CKV = (W_IN_KR[0] + W_IN_KR[1], KV_LORA)
W_IN_LORA = (W_IN_CQ[0], Q_LORA + HEAD_PAD + KV_LORA)
W_IN_COLS = W_IN_CKV[0] + W_IN_CKV[1]


def _pad_last(a, n):
    return jnp.pad(a, [(0, 0)] * (a.ndim - 1) + [(0, n - a.shape[-1])])


def _layout_w_in(w_in):
    kr = jnp.pad(w_in[:, _CUT[2]:_CUT[3]], ((0, 0), (QK_NOPE, HEAD_PAD - QK_NOPE - QK_ROPE)))
    return jnp.concatenate([w_in[:, _CUT[3]:], w_in[:, :_CUT[0]], w_in[:, _CUT[0]:_CUT[1]], kr, w_in[:, _CUT[1]:_CUT[2]]], axis=1)


def _layout_weights(w):
    out = dict(w)
    if "w_uq" in w:
        out["w_uq"] = _pad_last(w["w_uq"].reshape(Q_LORA, MLA_HEADS, QK_NOPE + QK_ROPE), HEAD_PAD).reshape(Q_LORA, MLA_HEADS * HEAD_PAD)
    if "w_ukv" in w:
        ukv = w["w_ukv"].reshape(KV_LORA, MLA_HEADS, QK_NOPE + V_DIM)
        uk = _pad_last(ukv[:, :, :QK_NOPE], HEAD_PAD).reshape(KV_LORA, MLA_HEADS * HEAD_PAD)
        uv = _pad_last(ukv[:, :, QK_NOPE:], HEAD_PAD).reshape(KV_LORA, MLA_HEADS * HEAD_PAD)
        out["w_ukv"] = jnp.concatenate([uk, uv], axis=1)
    if "w_mla_out" in w:
        mo = jnp.pad(w["w_mla_out"].reshape(MLA_HEADS, V_DIM, D_MODEL), ((0, 0), (0, HEAD_PAD - V_DIM), (0, 0)))
        out["w_mla_out"] = mo.reshape(MLA_HEADS * HEAD_PAD, D_MODEL)
    return out


def _unlayout_grads(g):
    out = dict(g)
    if "w_in" in g:
        gi = g["w_in"]
        win = lambda w: gi[:, w[0]:w[0] + w[1]]
        kr = gi[:, W_IN_KR[0] + QK_NOPE:W_IN_KR[0] + QK_NOPE + QK_ROPE]
        out["w_in"] = jnp.concatenate([win(W_IN_CONV), win(W_IN_CQ), win(W_IN_CKV), kr, win(W_IN_GATES)], axis=1)
    if "w_uq" in g:
        out["w_uq"] = g["w_uq"].reshape(Q_LORA, MLA_HEADS, HEAD_PAD)[:, :, :QK_NOPE + QK_ROPE].reshape(Q_LORA, -1)
    if "w_ukv" in g:
        gk = g["w_ukv"][:, :MLA_HEADS * HEAD_PAD].reshape(KV_LORA, MLA_HEADS, HEAD_PAD)[:, :, :QK_NOPE]
        gv = g["w_ukv"][:, MLA_HEADS * HEAD_PAD:].reshape(KV_LORA, MLA_HEADS, HEAD_PAD)[:, :, :V_DIM]
        out["w_ukv"] = jnp.concatenate([gk, gv], axis=2).reshape(KV_LORA, -1)
    if "w_mla_out" in g:
        out["w_mla_out"] = g["w_mla_out"].reshape(MLA_HEADS, HEAD_PAD, D_MODEL)[:, :V_DIM].reshape(MLA_HEADS * V_DIM, D_MODEL)
    return out


def _rope_tables(positions):
    half = QK_ROPE // 2
    inv_freq = ROPE_THETA ** (-jnp.arange(half, dtype=F32) / half)
    ang = positions.astype(F32).reshape(-1, 1) * inv_freq
    cos, sin = jnp.cos(ang), jnp.sin(ang)
    S = cos.shape[0]
    z16, z32, z64 = jnp.zeros((S, half), F32), jnp.zeros((S, QK_ROPE), F32), jnp.zeros((S, QK_NOPE), F32)
    c = jnp.concatenate([jnp.ones((S, QK_NOPE), F32), cos, cos, z32], axis=1)
    sa = jnp.concatenate([z64, -sin, z16, z32], axis=1)
    sb = jnp.concatenate([z64, z16, sin, z32], axis=1)
    return c, sa, sb


def _local_step(x, mem, positions, target, weight_fns, early_grads_fn, late_grads_fn, sm):
    S = x.shape[0]
    HW = MLA_HEADS * HEAD_PAD
    rope_c, rope_sa, rope_sb = _rope_tables(positions)
    qk_scale = (QK_NOPE + QK_ROPE) ** -0.5

    def k_rms1(x_ref, g_ref, u_ref):
        u_ref[...] = _rms_fwd(x_ref[...], g_ref[...]).astype(BF16)

    u1, = _rows("rms_mix", k_rms1, [x], [sm["norm_mix_g"]], [(D_MODEL, BF16)])
    w_in, conv_w = weight_fns[0]((u1, rope_c, rope_sa, rope_sb))

    def epi_glu(acc, xs, outs):
        a, gt = acc[:, 0:CONV_CH], acc[:, CONV_CH:2 * CONV_CH]
        outs[0][...] = acc[...].astype(BF16)
        outs[1][...] = (a * _sigmoid(gt)).astype(BF16)

    conv_in, z0 = _mm("proj_conv", u1, w_in, "nn", [(2 * CONV_CH, BF16), (CONV_CH, BF16)], epi_glu, b_cols=W_IN_CONV)
    c_q = _mm_plain("proj_cq", u1, w_in, "nn", dtype=BF16, b_cols=W_IN_CQ)
    c_kv = _mm_plain("proj_ckv", u1, w_in, "nn", dtype=BF16, b_cols=W_IN_CKV)
    kr_raw = _mm_plain("proj_krope", u1, w_in, "nn", b_cols=W_IN_KR)

    def epi_sigmoid(acc, xs, outs):
        outs[0][...] = _sigmoid(acc[...]).astype(BF16)

    gates, = _mm("proj_gates", u1, w_in, "nn", [(2 * D_MODEL, BF16)], epi_sigmoid, b_cols=W_IN_GATES)

    z1, z3 = _conv_fwd(z0, conv_w, sm["conv_b"], sm["conv_ln_g"], sm["conv_ln_b"])
    wl = weight_fns[1](z1)
    conv_out = _mm_plain("conv_out", z3, wl["w_conv_out"], "nn", dtype=BF16)

    def k_lora_norm(cq_ref, ckv_ref, gq_ref, gkv_ref, qn_ref, kvn_ref):
        qn_ref[...] = _rms_fwd(cq_ref[...].astype(F32), gq_ref[...]).astype(BF16)
        kvn_ref[...] = _rms_fwd(ckv_ref[...].astype(F32), gkv_ref[...]).astype(BF16)

    qn, kvn = _rows("lora_norm", k_lora_norm, [c_q, c_kv], [sm["q_norm_g"], sm["kv_norm_g"]],
                    [(Q_LORA, BF16), (KV_LORA, BF16)])

    def epi_q(acc, xs, outs):
        c, sa, sb = xs[0][...], xs[1][...], xs[2][...]
        for h in range(MLA_HEADS):
            lo = h * HEAD_PAD
            outs[0][:, lo:lo + HEAD_PAD] = (_rope(acc[:, lo:lo + HEAD_PAD], c, sa, sb, 1.0) * qk_scale).astype(BF16)

    q_att, = _mm("q_up", qn, wl["w_uq"], "nn", [(HW, BF16)], epi_q, row_x=[rope_c, rope_sa, rope_sb], tn=HW)

    def epi_kv(acc, xs, outs):
        kr = _rope(xs[0][...], xs[1][...], xs[2][...], xs[3][...], 1.0)
        kr = kr + _neg_ones(kr.shape, STAT_COL_QK)
        vconst = _neg_ones(kr.shape, STAT_COL_V)
        for h in range(MLA_HEADS):
            lo = h * HEAD_PAD
            outs[0][:, lo:lo + HEAD_PAD] = (acc[:, lo:lo + HEAD_PAD] + kr).astype(BF16)
            outs[1][:, lo:lo + HEAD_PAD] = (acc[:, HW + lo:HW + lo + HEAD_PAD] + vconst).astype(BF16)

    k_att, v_att = _mm("kv_up", kvn, wl["w_ukv"], "nn", [(HW, BF16), (HW, BF16)], epi_kv,
                       row_x=[kr_raw, rope_c, rope_sa, rope_sb], tn=2 * HW)

    o_att, q_aug = _flash_fwd(q_att, k_att, v_att)
    wl.update(weight_fns[2](o_att))

    def epi_merge(acc, xs, outs):
        for rows in _row_chunks(acc.shape[0]):
            mo = acc[rows, :]
            g0, g1 = xs[0][rows, 0:D_MODEL].astype(F32), xs[0][rows, D_MODEL:].astype(F32)
            outs[0][rows, :] = mo.astype(BF16)
            outs[1][rows, :] = (g0 * xs[1][rows, :].astype(F32) + g1 * mo).astype(BF16)

    mla_out, merged = _mm("mla_out_merge", o_att, wl["w_mla_out"], "nn", [(D_MODEL, BF16), (D_MODEL, BF16)], epi_merge,
                          row_x=[gates, conv_out], tn=D_MODEL)

    def epi_res_norm(acc, xs, outs):
        h = xs[0][...] + acc[...]
        outs[0][...] = h
        outs[1][...] = _rms_fwd(h, xs[1][...]).astype(BF16)

    h1, u2 = _mm("mix_out", merged, wl["w_out"], "nn", [(D_MODEL, F32), (D_MODEL, BF16)], epi_res_norm,
                 row_x=[x], vec_x=[sm["norm_xattn_g"]], tn=D_MODEL)

    xscale = X_HEAD_DIM ** -0.5

    def epi_scale(acc, xs, outs):
        outs[0][...] = (acc[...] * xscale).astype(BF16)

    xq, = _mm("xattn_q", u2, wl["w_xq"], "nn", [(X_HEADS * X_HEAD_DIM, BF16)], epi_scale)

    def k_mem_norm(m_ref, g_ref, o_ref):
        o_ref[...] = _rms_fwd(m_ref[...], g_ref[...]).astype(BF16)

    mem_n, = _rows("mem_norm", k_mem_norm, [mem], [sm["norm_mem_g"]], [(D_MODEL, BF16)])
    kvx = _mm_plain("xattn_kv", mem_n, wl["w_xkv"], "nn", dtype=BF16)
    ox = _xattn_fwd(xq, kvx)
    h2, u3 = _mm("xattn_out", ox, wl["w_xo"], "nn", [(D_MODEL, F32), (D_MODEL, BF16)], epi_res_norm,
                 row_x=[h1], vec_x=[sm["norm_mlp_g"]], tn=D_MODEL)

    def epi_relu2(acc, xs, outs):
        r = jnp.maximum(acc[...], 0.0)
        outs[0][...] = (r * r).astype(BF16)

    hid, = _mm("mlp_up", u3, wl["w_mlp1"], "nn", [(D_FF, BF16)], epi_relu2)

    def epi_final(acc, xs, outs):
        g = xs[2][...]
        for rows in _row_chunks(acc.shape[0]):
            h = xs[0][rows, :] + acc[rows, :]
            e = _rms_fwd(h, g) - xs[1][rows, :]
            part = 0.5 * jnp.sum(jnp.mean(e * e, axis=-1, keepdims=True), axis=0, keepdims=True)
            dh, dg = _rms_bwd(h, g, e * (1.0 / D_MODEL))
            outs[0][rows, :] = dh
            outs[1][rows, :] = dh.astype(BF16)
            _accum(outs[2], jnp.broadcast_to(part, outs[2].shape), first=rows.start == 0)
            _accum(outs[3], dg, first=rows.start == 0)

    dh3, dh3b, loss, g_final = _mm("mlp_down_loss", hid, wl["w_mlp2"], "nn", [(D_MODEL, F32), (D_MODEL, BF16)], epi_final,
                                   row_x=[h2, target], vec_x=[sm["final_norm_g"]], sums=[(1, LANES), (1, D_MODEL)],
                                   tn=D_MODEL, tk=1024)

    def epi_drelu2(acc, xs, outs):
        outs[0][...] = (acc[...] * (2.0 * jnp.sqrt(xs[0][...].astype(F32)))).astype(BF16)

    da1, = _mm("mlp_down_dx", dh3b, wl["w_mlp2"], "nt", [(D_FF, BF16)], epi_drelu2, tile_x=[hid])
    g_mlp2 = _mm_plain("mlp_down_dw", hid, dh3b, "tn")
    g_mlp1 = _mm_plain("mlp_up_dw", u3, da1, "tn")

    def epi_norm_bwd(acc, xs, outs):
        for rows in _row_chunks(acc.shape[0]):
            dx, dg = _rms_bwd(xs[0][rows, :], xs[2][...], acc[rows, :])
            dh = xs[1][rows, :] + dx
            outs[0][rows, :] = dh
            if len(outs) == 3:
                outs[1][rows, :] = dh.astype(BF16)
            _accum(outs[-1], dg, first=rows.start == 0)

    def dx_norm_bwd(name, dy, w, xin, dres, vecs, with_bf16=True):
        outs = [(D_MODEL, F32), (D_MODEL, BF16)] if with_bf16 else [(D_MODEL, F32)]
        return _mm(name, dy, w, "nt", outs, epi_norm_bwd, row_x=[xin, dres], vec_x=vecs, sums=[(1, D_MODEL)],
                   tn=D_MODEL, tk=_pick(dy.shape[1], (1024, 768, 512)))

    dh2, dh2b, g_norm_mlp = dx_norm_bwd("mlp_up_dx_norm", da1, wl["w_mlp1"], h2, dh3, [sm["norm_mlp_g"]])

    dox = _mm_plain("xattn_out_dx", dh2b, wl["w_xo"], "nt", dtype=BF16)
    g_xo = _mm_plain("xattn_out_dw", ox, dh2b, "tn")
    dxq, dkvx = _xattn_bwd(xq, kvx, dox)
    g_xq = _mm_plain("xattn_q_dw", u2, dxq, "tn")
    g_xkv = _mm_plain("xattn_kv_dw", mem_n, dkvx, "tn")
    dmem_n = _mm_plain("xattn_kv_dx", dkvx, wl["w_xkv"], "nt")

    def k_mem_bwd(m_ref, d_ref, g_ref, dg_ref):
        _, dg = _rms_bwd(m_ref[...], g_ref[...], d_ref[...])
        _accum(dg_ref, dg)

    g_norm_mem, = _rows("mem_norm_bwd", k_mem_bwd, [mem, dmem_n], [sm["norm_mem_g"]], [], [(1, D_MODEL)])
    token = early_grads_fn(dict(w_mlp1=g_mlp1, w_mlp2=g_mlp2, w_xo=g_xo, w_xq=g_xq, w_xkv=g_xkv))
    dh1, dh1b, g_norm_xattn = dx_norm_bwd("xattn_q_dx_norm", dxq, wl["w_xq"], h1, dh2, [sm["norm_xattn_g"], token])

    dmerged = _mm_plain("mix_out_dx", dh1b, wl["w_out"], "nt", dtype=BF16)
    g_out = _mm_plain("mix_out_dw", merged, dh1b, "tn")

    def k_merge_bwd(dm_ref, g_ref, co_ref, mo_ref, dco_ref, dmo_ref, dgl_ref):
        dm = dm_ref[...].astype(F32)
        g0, g1 = g_ref[:, 0:D_MODEL].astype(F32), g_ref[:, D_MODEL:].astype(F32)
        dco_ref[...] = (dm * g0).astype(BF16)
        dmo_ref[...] = (dm * g1).astype(BF16)
        dgl_ref[:, 0:D_MODEL] = (dm * co_ref[...].astype(F32) * g0 * (1.0 - g0)).astype(BF16)
        dgl_ref[:, D_MODEL:] = (dm * mo_ref[...].astype(F32) * g1 * (1.0 - g1)).astype(BF16)

    dconv_out, dmla_out, dproj = _rows("merge_bwd", k_merge_bwd, [dmerged, gates, conv_out, mla_out], [],
                                       [(D_MODEL, BF16), (D_MODEL, BF16), (2 * D_MODEL, BF16)], tile=256,
                                       window=(2, W_IN_GATES, W_IN_COLS, None))

    def epi_do(acc, xs, outs):
        for h in range(MLA_HEADS):
            lo = h * HEAD_PAD
            do = acc[:, lo:lo + HEAD_PAD]
            delta = jnp.sum(do * xs[0][:, lo:lo + HEAD_PAD].astype(F32), axis=-1, keepdims=True)
            outs[0][:, lo:lo + HEAD_PAD] = _put_stats(do.astype(BF16), delta, STAT_COL_V)

    do_aug, = _mm("mla_out_dx", dmla_out, wl["w_mla_out"], "nt", [(HW, BF16)], epi_do, row_x=[o_att], tn=HW)
    g_mla_out = _mm_plain("mla_out_dw", o_att, dmla_out, "tn")
    dq_att, dk_att, dkv_cat = _flash_bwd(q_aug, k_att, v_att, do_aug)

    def k_rope_bwd(dq_ref, dk_ref, c_ref, sa_ref, sb_ref, dqr_ref, dkv_ref, dkr_ref):
        c, sa, sb = c_ref[...], sa_ref[...], sb_ref[...]
        lane = lax.broadcasted_iota(jnp.int32, c.shape, 1)
        nope = (lane < QK_NOPE).astype(F32)
        ropem = ((lane >= QK_NOPE) & (lane < QK_NOPE + QK_ROPE)).astype(F32)
        dkr = jnp.zeros(c.shape, F32)
        for h in range(MLA_HEADS):
            lo = h * HEAD_PAD
            dqr_ref[:, lo:lo + HEAD_PAD] = (_rope(dq_ref[:, lo:lo + HEAD_PAD], c, sa, sb, -1.0) * qk_scale).astype(BF16)
            dk = dk_ref[:, lo:lo + HEAD_PAD].astype(F32)
            dkv_ref[:, lo:lo + HEAD_PAD] = (dk * nope).astype(BF16)
            dkr = dkr + dk
        dkr_ref[...] = (_rope(dkr * ropem, c, sa, sb, -1.0) * ropem).astype(BF16)

    dq_raw, dkv_cat, dkr = _rows("rope_bwd", k_rope_bwd, [dq_att, dk_att, rope_c, rope_sa, rope_sb], [],
                                 [(HW, BF16), (HW, BF16), (HEAD_PAD, BF16)], tile=256,
                                 window=(1, (0, HW), 2 * HW, dkv_cat))
    g_uq = _mm_plain("q_up_dw", qn, dq_raw, "tn")
    dqn = _mm_plain("q_up_dx", dq_raw, wl["w_uq"], "nt")
    g_ukv = _mm_plain("kv_up_dw", kvn, dkv_cat, "tn")
    dkvn = _mm_plain("kv_up_dx", dkv_cat, wl["w_ukv"], "nt")

    def k_lora_bwd(cq_ref, ckv_ref, dqn_ref, dkvn_ref, dkr_ref, gq_ref, gkv_ref, out_ref, dgq_ref, dgkv_ref):
        dcq, dgq = _rms_bwd(cq_ref[...].astype(F32), gq_ref[...], dqn_ref[...])
        dckv, dgkv = _rms_bwd(ckv_ref[...].astype(F32), gkv_ref[...], dkvn_ref[...])
        out_ref[:, 0:Q_LORA] = dcq.astype(BF16)
        out_ref[:, Q_LORA:Q_LORA + HEAD_PAD] = dkr_ref[...]
        out_ref[:, Q_LORA + HEAD_PAD:] = dckv.astype(BF16)
        _accum(dgq_ref, dgq)
        _accum(dgkv_ref, dgkv)

    dproj, g_q_norm, g_kv_norm = _rows("lora_norm_bwd", k_lora_bwd, [c_q, c_kv, dqn, dkvn, dkr],
                                       [sm["q_norm_g"], sm["kv_norm_g"]], [(W_IN_LORA[1], BF16)],
                                       [(1, Q_LORA), (1, KV_LORA)], window=(0, W_IN_LORA, W_IN_COLS, dproj))

    dz3 = _mm_plain("conv_out_dx", dconv_out, wl["w_conv_out"], "nt", dtype=BF16)
    g_conv_out = _mm_plain("conv_out_dw", z3, dconv_out, "tn")
    dz1, g_ln_g, g_ln_b, g_conv_b = _conv_bwd_norm(dz3, z1, sm["conv_ln_g"], sm["conv_ln_b"])
    dproj, g_conv_w = _conv_bwd_taps(dz1, z0, conv_in, conv_w, dproj, W_IN_CONV)

    g_in = _mm_plain("proj_dw", u1, dproj, "tn")
    token = late_grads_fn(dict(w_in=g_in, conv_w=g_conv_w[:CONV_WIDTH], w_conv_out=g_conv_out, w_uq=g_uq, w_ukv=g_ukv,
                               w_mla_out=g_mla_out, w_out=g_out))
    grad_x, g_norm_mix = dx_norm_bwd("proj_dx_norm", dproj, w_in, x, dh1, [sm["norm_mix_g"], token], with_bf16=False)

    small = dict(norm_mix_g=g_norm_mix, conv_b=g_conv_b, conv_ln_g=g_ln_g, conv_ln_b=g_ln_b, q_norm_g=g_q_norm,
                 kv_norm_g=g_kv_norm, norm_xattn_g=g_norm_xattn, norm_mem_g=g_norm_mem, norm_mlp_g=g_norm_mlp,
                 final_norm_g=g_final)
    return loss, grad_x, small


def _shard(a, k, axis):
    n = a.shape[axis] // N_CHIPS
    return lax.slice_in_dim(a, k * n, (k + 1) * n, axis=axis)


def _pack_small(grads, loss):
    flat = jnp.concatenate([grads[n].reshape(-1) for n in SMALL] + [loss.reshape(-1)[:1]])
    rows = -(-flat.shape[0] // (8 * LANES)) * 8
    return jnp.pad(flat, (0, rows * LANES - flat.shape[0])).reshape(rows, LANES)


def _pack_groups(shapes, names):
    groups = {}
    for n in names:
        groups.setdefault(shapes[n][1], []).append(n)
    return groups


def _pad_rows(a, mult):
    return jnp.pad(a, ((0, -a.shape[0] % mult), (0, 0)))


def _pack_grads(grads, shapes, names):
    packs = []
    for width, group in _pack_groups(shapes, names).items():
        per_chip = [jnp.concatenate([_pad_rows(_shard(grads[n], k, SHARD_AXIS[n]).astype(BF16), PACK_ROW_ALIGN) for n in group])
                    for k in range(N_CHIPS)]
        rows = per_chip[0].shape[0]
        packs.append(jnp.stack(per_chip).reshape(N_CHIPS, 2, rows // 2, width))
    return packs


def _unpack_grads(fulls, shapes, names):
    out = {}
    for full, group in zip(fulls, _pack_groups(shapes, names).values()):
        flat, at = full.reshape(-1, full.shape[-1]), 0
        for n in group:
            rows = shapes[n][0]
            out[n] = flat[at:at + rows]
            at += rows + (-rows % PACK_ROW_ALIGN)
    return out


def _unpack(flat, names, shapes):
    out, at = {}, 0
    for n in names:
        size = math.prod(shapes[n])
        out[n] = flat[at:at + size].reshape(shapes[n])
        at += size
    return out, at


def kernel(x, mem, positions, norm_mix_g, w_in, conv_w, conv_b, conv_ln_g, conv_ln_b, w_conv_out, q_norm_g, w_uq, kv_norm_g, w_ukv, w_mla_out, w_out, norm_xattn_g, norm_mem_g, w_xq, w_xkv, w_xo, norm_mlp_g, w_mlp1, w_mlp2, final_norm_g, loss_target, m_norm_mix_g, m_w_in, m_conv_w, m_conv_b, m_conv_ln_g, m_conv_ln_b, m_w_conv_out, m_q_norm_g, m_w_uq, m_kv_norm_g, m_w_ukv, m_w_mla_out, m_w_out, m_norm_xattn_g, m_norm_mem_g, m_w_xq, m_w_xkv, m_w_xo, m_norm_mlp_g, m_w_mlp1, m_w_mlp2, m_final_norm_g, v_norm_mix_g, v_w_in, v_conv_w, v_conv_b, v_conv_ln_g, v_conv_ln_b, v_w_conv_out, v_q_norm_g, v_w_uq, v_kv_norm_g, v_w_ukv, v_w_mla_out, v_w_out, v_norm_xattn_g, v_norm_mem_g, v_w_xq, v_w_xkv, v_w_xo, v_norm_mlp_g, v_w_mlp1, v_w_mlp2, v_final_norm_g):
    args = dict(locals())
    w = {n: args[n] for n in WEIGHTS}
    m = {n: args["m_" + n] for n in WEIGHTS}
    v = {n: args["v_" + n] for n in WEIGHTS}

    shards = [w[n][0].astype(F32 if n == "conv_w" else BF16) for n in BIG]
    bounds = (0,) + WEIGHT_WAITS + (len(BIG),)
    spans = [slice(lo, hi) for lo, hi in zip(bounds[:-1], bounds[1:])]
    sem_pairs, shards_thru, lands_thru, token = _gather_start(shards, [list(range(len(BIG)))[sp] for sp in spans])

    def unshard(n, g):
        ax = SHARD_AXIS[n]
        return jnp.moveaxis(g, 0, ax).reshape(g.shape[1:1 + ax] + (N_CHIPS * g.shape[1 + ax],) + g.shape[2 + ax:])

    def wait_fn(i):
        def fn(after):
            lands = _gather_wait(f"gather_weights_wait_{i}", sem_pairs[i], shards_thru[spans[i]], lands_thru[spans[i]], after)
            full = {n: unshard(n, g) for n, g in zip(BIG[spans[i]], lands)}
            return (_layout_w_in(full["w_in"]), full["conv_w"]) if i == 0 else _layout_weights(full)
        return fn

    sm = {n: w[n].reshape(1, -1) for n in SMALL}
    sm["norm_mix_g"] = sm["norm_mix_g"] + token[0, 0]

    shapes = {n: w[n].shape[1:] if n in BIG else w[n].shape for n in WEIGHTS}
    late_names = [n for n in BIG if n not in EARLY_GRADS]
    inflight = {}

    def send_grads(tag, names):
        def fn(g):
            packs = _pack_grads(_unlayout_grads(g), shapes, names)
            got = _pair_exchange(f"grad_pair_exchange_{tag}", packs)
            pairs = [_pair_add(f"grad_pair_add_{tag}_{i}", p, r) for i, (p, r) in enumerate(zip(packs, got))]
            *inflight[tag], token = _chip_exchange_start(f"grad_chip_exchange_{tag}_start", pairs)
            return token
        return fn

    loss, grad_x, g_small = _local_step(x[0], mem[0], positions, loss_target[0], [wait_fn(i) for i in range(3)],
                                        send_grads("early", EARLY_GRADS), send_grads("late", late_names), sm)

    halves, counts = [], {}
    for tag in ("late", "early"):
        own, got = _chip_exchange_wait(f"grad_chip_exchange_{tag}_wait", *inflight[tag], grad_x)
        halves += [_chip_add(f"grad_chip_add_{tag}_{i}", p, g) for i, (p, g) in enumerate(zip(own, got))]
        counts[tag] = len(own)
    fulls = _pair_share(halves)
    g_sum = _unpack_grads(fulls[:counts["late"]], shapes, late_names)
    g_sum.update(_unpack_grads(fulls[counts["late"]:], shapes, EARLY_GRADS))
    small_flat = _sum_over_devices(_pack_small(g_small, loss)).reshape(-1)
    g_small, at = _unpack(small_flat, SMALL, shapes)
    g_sum.update(g_small)
    loss_sum = small_flat[at]

    out_g, out_d, out_m, out_v = [], [], [], []
    for n in WEIGHTS:
        g, d, nm, nv = _adamw("adamw_" + n, w[n], g_sum[n], m[n], v[n])
        out_g.append(g)
        out_d.append(d)
        out_m.append(nm)
        out_v.append(nv)
    return (loss_sum, grad_x[None], *out_g, *out_d, *out_m, *out_v)
```

```python
import math

import jax
import jax.numpy as jnp
from jax import lax
from jax.experimental import pallas as pl
from jax.experimental.pallas import tpu as pltpu

F32 = jnp.float32
BF16 = jnp.bfloat16
MESH = pl.DeviceIdType.MESH

D_MODEL = 1024
CONV_CH = 512
CONV_WIDTH = 31
MLA_HEADS = 8
QK_NOPE = 64
QK_ROPE = 32
V_DIM = 64
Q_LORA = 384
KV_LORA = 256
MEM_LEN = 256
X_HEADS = 4
X_HEAD_DIM = 128
D_FF = 4096
ROPE_THETA = 10000.0
EPS = 1e-6
HEAD_PAD = 128
STAT_COL_QK = QK_NOPE + QK_ROPE
STAT_COL_V = V_DIM
HALO = 32
N_CHIPS = 4
LANES = 128

ADAM_LR = 0.001
ADAM_B1 = 0.9
ADAM_B2 = 0.999
ADAM_EPS = 1e-08
ADAM_WD = 0.01
ADAM_STEP = 10

VMEM_LIMIT = 52 * 1024 * 1024
ROW_TILES = (1024, 512, 256, 128, 64, 32, 16)
PACK_ROW_ALIGN = 32
N_DEV = 8
NEG = -1e30

BIG = ["w_in", "conv_w", "w_conv_out", "w_uq", "w_ukv", "w_mla_out", "w_out", "w_xq", "w_xkv", "w_xo", "w_mlp1", "w_mlp2"]
WEIGHT_WAITS = (2, 5)
SHARD_AXIS = {"w_in": 1, "w_conv_out": 1, "w_uq": 1, "w_ukv": 1, "w_mla_out": 1, "w_out": 0, "w_xq": 0, "w_xkv": 0,
              "w_xo": 1, "w_mlp1": 1, "w_mlp2": 0, "conv_w": 1}
EARLY_GRADS = ["w_mlp1", "w_mlp2", "w_xkv", "w_xq", "w_xo"]
SMALL = ["norm_mix_g", "conv_b", "conv_ln_g", "conv_ln_b", "q_norm_g", "kv_norm_g", "norm_xattn_g", "norm_mem_g",
         "norm_mlp_g", "final_norm_g"]
WEIGHTS = ["norm_mix_g", "w_in", "conv_w", "conv_b", "conv_ln_g", "conv_ln_b", "w_conv_out", "q_norm_g", "w_uq",
           "kv_norm_g", "w_ukv", "w_mla_out", "w_out", "norm_xattn_g", "norm_mem_g", "w_xq", "w_xkv", "w_xo",
           "norm_mlp_g", "w_mlp1", "w_mlp2", "final_norm_g"]


def _pick(n, prefs):
    for p in prefs:
        if n % p == 0:
            return p
    return n


def _params(sem):
    return pltpu.CompilerParams(dimension_semantics=sem, vmem_limit_bytes=VMEM_LIMIT)


_DIMS = {"nn": (((1,), (0,)), ((), ())), "nt": (((1,), (1,)), ((), ())), "tn": (((0,), (0,)), ((), ()))}


def _mm(name, a, b, mode, outs, epi, row_x=(), tile_x=(), vec_x=(), sums=(), tm=None, tn=None, tk=None, b_cols=None):
    if mode == "nn":
        (M, K), (_, N) = a.shape, b.shape
        if b_cols is not None:
            N = b_cols[1]
    elif mode == "nt":
        (M, K), (N, _) = a.shape, b.shape
    else:
        (K, M), (_, N) = a.shape, b.shape
    tm = tm or _pick(M, (1024, 512, 384, 256, 128))
    tn = tn or _pick(N, (1024, 768, 512, 384, 256, 128))
    tk = tk or _pick(K, (2048, 1920, 1024, 768, 512, 384, 256, 128))
    nk = K // tk
    rows_inner = nk == 1 and N // tn > 1
    grid = (N // tn, M // tm, nk) if rows_inner else (M // tm, N // tn, nk)

    def spec(shape, f):
        return pl.BlockSpec(shape, (lambda j, i, k: f(i, j, k)) if rows_inner else f)

    b_off = 0
    if b_cols is not None:
        assert mode == "nn" and b_cols[0] % tn == 0, (name, b_cols, tn)
        b_off = b_cols[0] // tn
    a_spec = spec((tk, tm), lambda i, j, k: (k, i)) if mode == "tn" else spec((tm, tk), lambda i, j, k: (i, k))
    b_spec = spec((tn, tk), lambda i, j, k: (j, k)) if mode == "nt" else spec((tk, tn), lambda i, j, k: (k, j + b_off))
    in_specs = [a_spec, b_spec]
    in_specs += [spec((tm, r.shape[1]), lambda i, j, k: (i, 0)) for r in row_x]
    in_specs += [spec((tm, tn), lambda i, j, k: (i, j)) for _ in tile_x]
    in_specs += [spec(v.shape, lambda i, j, k: (0, 0)) for v in vec_x]
    out_specs, out_shape = [], []
    for w, dt in outs:
        if tn == N:
            out_specs.append(spec((tm, w), lambda i, j, k: (i, 0)))
        else:
            assert w == N, (name, w, N)
            out_specs.append(spec((tm, tn), lambda i, j, k: (i, j)))
        out_shape.append(jax.ShapeDtypeStruct((M, w), dt))
    for shp in sums:
        assert tn == N and not rows_inner, name
        out_specs.append(spec(shp, lambda i, j, k: (0, 0)))
        out_shape.append(jax.ShapeDtypeStruct(shp, F32))
    nx = len(row_x) + len(tile_x) + len(vec_x)
    dims = _DIMS[mode]

    def body(a_ref, b_ref, *rest):
        x_refs, out_refs, acc_ref = rest[:nx], rest[nx:nx + len(outs) + len(sums)], rest[-1]
        av, bv = a_ref[...], b_ref[...]
        if av.dtype != BF16:
            av = av.astype(BF16)
        if bv.dtype != BF16:
            bv = bv.astype(BF16)
        prod = lax.dot_general(av, bv, dims, preferred_element_type=F32)
        if nk == 1:
            acc_ref[...] = prod
            epi(acc_ref, x_refs, out_refs)
        else:
            k = pl.program_id(2)

            @pl.when(k == 0)
            def _():
                acc_ref[...] = prod

            @pl.when(k > 0)
            def _():
                acc_ref[...] += prod

            @pl.when(k == nk - 1)
            def _():
                epi(acc_ref, x_refs, out_refs)

    res = pl.pallas_call(
        body, name=name, grid=grid, in_specs=in_specs, out_specs=out_specs, out_shape=out_shape,
        scratch_shapes=[pltpu.VMEM((tm, tn), F32)],
        compiler_params=_params(("arbitrary",) * 3 if sums else ("parallel", "parallel", "arbitrary")),
    )(a, b, *row_x, *tile_x, *vec_x)
    return res


def _epi_store(acc_ref, x_refs, out_refs):
    for o in out_refs:
        o[...] = acc_ref[...].astype(o.dtype)


def _mm_plain(name, a, b, mode, dtype=F32, **kw):
    n = kw["b_cols"][1] if kw.get("b_cols") else (b.shape[0] if mode == "nt" else b.shape[1])
    return _mm(name, a, b, mode, [(n, dtype)], _epi_store, **kw)[0]


def _rows(name, body, row_ins, vec_ins, row_outs, acc_outs=(), tile=512, window=None):
    S = row_ins[0].shape[0]
    t = _pick(S, (tile, 256, 128, 64, 32, 16, 8))
    in_specs = [pl.BlockSpec((t, r.shape[1]), lambda i: (i, 0)) for r in row_ins]
    in_specs += [pl.BlockSpec(v.shape, lambda i: (0, 0)) for v in vec_ins]
    out_specs = [pl.BlockSpec((t, w), lambda i: (i, 0)) for w, _ in row_outs]
    out_specs += [pl.BlockSpec(shp, lambda i: (0, 0)) for shp in acc_outs]
    out_shape = [jax.ShapeDtypeStruct((S, w), dt) for w, dt in row_outs]
    out_shape += [jax.ShapeDtypeStruct(shp, F32) for shp in acc_outs]
    extra, aliases, n_in = [], {}, len(row_ins) + len(vec_ins)
    if window is not None:
        k, (first, width), total, buffer = window
        assert first % width == 0 and row_outs[k][0] == width, (name, window[:3])
        out_specs[k] = pl.BlockSpec((t, width), lambda i: (i, first // width))
        out_shape[k] = jax.ShapeDtypeStruct((S, total), row_outs[k][1])
        if buffer is not None:
            extra, aliases = [buffer], {n_in: k}
            in_specs.append(pl.BlockSpec(memory_space=pl.ANY))

    def call(*refs):
        body(*refs[:n_in], *refs[n_in + len(extra):])

    sem = ("arbitrary",) if acc_outs else ("parallel",)
    return pl.pallas_call(
        call, name=name, grid=(S // t,), in_specs=in_specs, out_specs=out_specs,
        out_shape=out_shape, input_output_aliases=aliases, compiler_params=_params(sem),
    )(*row_ins, *vec_ins, *extra)


def _accum(ref, val, first=True):
    if first:
        @pl.when(pl.program_id(0) == 0)
        def _():
            ref[...] = jnp.zeros_like(ref)

    ref[...] += val


EPILOGUE_ROWS = 256


def _row_chunks(n):
    step = min(EPILOGUE_ROWS, n)
    return [slice(r, r + step) for r in range(0, n, step)]


def _colsum(v):
    return jnp.sum(v, axis=0, keepdims=True)


def _rms_fwd(x, g):
    r = lax.rsqrt(jnp.mean(x * x, axis=-1, keepdims=True) + EPS)
    return x * r * g


def _rms_bwd(x, g, du):
    r = lax.rsqrt(jnp.mean(x * x, axis=-1, keepdims=True) + EPS)
    xn = x * r
    gdu = du * g
    dx = r * (gdu - xn * jnp.mean(xn * gdu, axis=-1, keepdims=True))
    return dx, _colsum(du * xn)


def _sigmoid(v):
    return 1.0 / (1.0 + jnp.exp(-v))


def _rope(v, c, sa, sb, sign):
    return v * c + sign * (pltpu.roll(v, HEAD_PAD - QK_ROPE // 2, 1) * sa + pltpu.roll(v, QK_ROPE // 2, 1) * sb)


def _split3(v):
    hi = v.astype(BF16)
    r1 = v - hi.astype(F32)
    mid = r1.astype(BF16)
    lo = (r1 - mid.astype(F32)).astype(BF16)
    return hi, mid, lo


def _put_stats(base, stat, col):
    hi, mid, lo = _split3(stat)
    lane = lax.broadcasted_iota(jnp.int32, base.shape, 1)
    out = jnp.where(lane == col, hi, base)
    out = jnp.where(lane == col + 1, mid, out)
    return jnp.where(lane == col + 2, lo, out)


def _neg_ones(shape, col):
    lane = lax.broadcasted_iota(jnp.int32, shape, 1)
    return jnp.where((lane >= col) & (lane < col + 3), -1.0, 0.0).astype(F32)


def _shifted(ext, t):
    p = ext.shape[0]
    for b in range(8):
        rb = ext if b == 0 else pltpu.roll(ext, p - b, 0)
        for a in range(HALO // 8 + 1):
            if 8 * a + b <= HALO:
                yield 8 * a + b, rb[8 * a:8 * a + t]


def _conv_fwd(z0, conv_w, conv_b, ln_g, ln_b):
    S, C = z0.shape
    t = _pick(S, (512, 256, 128, 64, 32))
    per = t // HALO

    def body(cur_ref, prev_ref, w_ref, b_ref, g_ref, beta_ref, z1_ref, z3_ref, ext_ref):
        i = pl.program_id(0)
        ext_ref[0:HALO, :] = jnp.where(i > 0, prev_ref[...].astype(F32), 0.0)
        ext_ref[HALO:, :] = cur_ref[...].astype(F32)
        ext = ext_ref[...]
        acc = jnp.zeros((t, C), F32)
        for d, win in _shifted(ext, t):
            k = d - (HALO - CONV_WIDTH + 1)
            if 0 <= k < CONV_WIDTH:
                acc = acc + win * w_ref[k:k + 1, :]
        z1 = acc + b_ref[...]
        z1_ref[...] = z1
        mu = jnp.mean(z1, axis=-1, keepdims=True)
        zc = z1 - mu
        rs = lax.rsqrt(jnp.mean(zc * zc, axis=-1, keepdims=True) + EPS)
        z2 = zc * rs * g_ref[...] + beta_ref[...]
        z3_ref[...] = (z2 * _sigmoid(z2)).astype(BF16)

    vec = lambda v: pl.BlockSpec(v.shape, lambda i: (0, 0))
    return pl.pallas_call(
        body, name="conv_fwd", grid=(S // t,),
        in_specs=[pl.BlockSpec((t, C), lambda i: (i, 0)),
                  pl.BlockSpec((HALO, C), lambda i: (jnp.maximum(i * per - 1, 0), 0)),
                  vec(conv_w), vec(conv_b), vec(ln_g), vec(ln_b)],
        out_specs=[pl.BlockSpec((t, C), lambda i: (i, 0)), pl.BlockSpec((t, C), lambda i: (i, 0))],
        out_shape=[jax.ShapeDtypeStruct((S, C), F32), jax.ShapeDtypeStruct((S, C), BF16)],
        scratch_shapes=[pltpu.VMEM((t + HALO, C), F32)],
        compiler_params=_params(("parallel",)),
    )(z0, z0, conv_w, conv_b, ln_g, ln_b)


def _conv_bwd_norm(dz3, z1, ln_g, ln_b):
    C = z1.shape[1]

    def body(dz3_ref, z1_ref, g_ref, beta_ref, dz1_ref, dg_ref, dbeta_ref, dbias_ref):
        z1 = z1_ref[...]
        mu = jnp.mean(z1, axis=-1, keepdims=True)
        zc = z1 - mu
        rs = lax.rsqrt(jnp.mean(zc * zc, axis=-1, keepdims=True) + EPS)
        xh = zc * rs
        z2 = xh * g_ref[...] + beta_ref[...]
        sg = _sigmoid(z2)
        dz2 = dz3_ref[...].astype(F32) * (sg * (1.0 + z2 * (1.0 - sg)))
        dxh = dz2 * g_ref[...]
        dz1 = rs * (dxh - jnp.mean(dxh, axis=-1, keepdims=True) - xh * jnp.mean(dxh * xh, axis=-1, keepdims=True))
        dz1_ref[...] = dz1
        _accum(dg_ref, _colsum(dz2 * xh))
        _accum(dbeta_ref, _colsum(dz2))
        _accum(dbias_ref, _colsum(dz1))

    return _rows("conv_bwd_norm", body, [dz3, z1], [ln_g, ln_b], [(C, F32)], [(1, C)] * 3)


def _conv_bwd_taps(dz1, z0, conv_in, conv_w, dproj, window):
    S, C = z0.shape
    t = _pick(S, (512, 256, 128, 64, 32))
    per = t // HALO
    last = S // HALO - 1
    nt = S // t
    assert window[1] == 2 * C and window[0] % window[1] == 0, window

    def body(dcur_ref, dnext_ref, zcur_ref, zprev_ref, cin_ref, w_ref, _, dcin_ref, dw_ref, dext_ref, zext_ref):
        i = pl.program_id(0)
        dcur = dcur_ref[...]
        dext_ref[0:t, :] = dcur
        dext_ref[t:, :] = jnp.where(i < nt - 1, dnext_ref[...], 0.0)
        zext_ref[0:HALO, :] = jnp.where(i > 0, zprev_ref[...].astype(F32), 0.0)
        zext_ref[HALO:, :] = zcur_ref[...].astype(F32)

        @pl.when(i == 0)
        def _():
            dw_ref[...] = jnp.zeros_like(dw_ref)

        dz0 = jnp.zeros((t, C), F32)
        for d, win in _shifted(dext_ref[...], t):
            k = CONV_WIDTH - 1 - d
            if 0 <= k < CONV_WIDTH:
                dz0 = dz0 + win * w_ref[k:k + 1, :]
        for d, win in _shifted(zext_ref[...], t):
            k = d - (HALO - CONV_WIDTH + 1)
            if 0 <= k < CONV_WIDTH:
                dw_ref[k:k + 1, :] += _colsum(dcur * win)
        a = cin_ref[:, 0:C].astype(F32)
        sg = _sigmoid(cin_ref[:, C:2 * C].astype(F32))
        dcin_ref[:, 0:C] = (dz0 * sg).astype(BF16)
        dcin_ref[:, C:2 * C] = (dz0 * a * sg * (1.0 - sg)).astype(BF16)

    return pl.pallas_call(
        body, name="conv_bwd_taps", grid=(nt,),
        in_specs=[pl.BlockSpec((t, C), lambda i: (i, 0)),
                  pl.BlockSpec((HALO, C), lambda i: (jnp.minimum((i + 1) * per, last), 0)),
                  pl.BlockSpec((t, C), lambda i: (i, 0)),
                  pl.BlockSpec((HALO, C), lambda i: (jnp.maximum(i * per - 1, 0), 0)),
                  pl.BlockSpec((t, 2 * C), lambda i: (i, 0)),
                  pl.BlockSpec(conv_w.shape, lambda i: (0, 0)),
                  pl.BlockSpec(memory_space=pl.ANY)],
        out_specs=[pl.BlockSpec((t, 2 * C), lambda i: (i, window[0] // window[1])), pl.BlockSpec((HALO, C), lambda i: (0, 0))],
        out_shape=[jax.ShapeDtypeStruct(dproj.shape, BF16), jax.ShapeDtypeStruct((HALO, C), F32)],
        input_output_aliases={6: 0},
        scratch_shapes=[pltpu.VMEM((t + HALO, C), F32), pltpu.VMEM((t + HALO, C), F32)],
        compiler_params=_params(("arbitrary",)),
    )(dz1, dz1, z0, z0, conv_in, conv_w, dproj)


def _lower_tri(shape, rows_are_queries):
    row = lax.broadcasted_iota(jnp.int32, shape, 0)
    col = lax.broadcasted_iota(jnp.int32, shape, 1)
    return (col <= row) if rows_are_queries else (row <= col)


HEADS_PER_STEP = 2
FWD_HEADS_PER_STEP = 2
FWD_KEY_TILES = 8


def _flash_specs(S, t, heads):
    w = heads * HEAD_PAD
    blk = pl.BlockSpec((t, w), lambda h, i: (i, h))
    head = pl.BlockSpec((S, w), lambda h, i: (0, h))
    return blk, head


def _head_lanes(g):
    return slice(g * HEAD_PAD, (g + 1) * HEAD_PAD)


def _dot_nt(a, b):
    return lax.dot_general(a, b, _DIMS["nt"], preferred_element_type=F32)


def _dot_nn(a, b):
    return lax.dot_general(a, b, _DIMS["nn"], preferred_element_type=F32)


def _dot_tn(a, b):
    return lax.dot_general(a, b, _DIMS["tn"], preferred_element_type=F32)


def _flash_fwd(q, k, v):
    S = q.shape[0]
    t = _pick(S, (512, 256, 128))

    def body(q_ref, k_ref, v_ref, o_ref, qa_ref, m_ref, acc_ref):
        qi = pl.program_id(1)
        m_ref[...] = jnp.full_like(m_ref, NEG)
        acc_ref[...] = jnp.zeros_like(acc_ref)

        def step(first, tiles, diag):
            width = tiles * t
            rows = pl.ds(pl.multiple_of(first, t), width)
            for g in range(FWD_HEADS_PER_STEP):
                hl = _head_lanes(g)
                s = _dot_nt(q_ref[:, hl], k_ref[rows, hl])
                if diag:
                    row = lax.broadcasted_iota(jnp.int32, s.shape, 0)
                    col = lax.broadcasted_iota(jnp.int32, s.shape, 1)
                    s = jnp.where(col <= row + (tiles - 1) * t, s, NEG)
                m_old = m_ref[g]
                m_new = jnp.maximum(m_old, jnp.max(s, axis=-1, keepdims=True))
                p = jnp.exp(s - m_new).astype(BF16)
                acc_ref[g] = jnp.exp(m_old - m_new) * acc_ref[g] + _dot_nn(p, v_ref[rows, hl])
                m_ref[g] = m_new

        def wide(kb, carry):
            step(kb * (FWD_KEY_TILES * t), FWD_KEY_TILES, False)
            return carry

        full_groups = qi // FWD_KEY_TILES
        lax.fori_loop(0, full_groups, wide, 0)
        for tiles in range(1, min(FWD_KEY_TILES, S // t) + 1):
            @pl.when(qi - full_groups * FWD_KEY_TILES == tiles - 1)
            def _():
                step(full_groups * (FWD_KEY_TILES * t), tiles, True)

        for g in range(FWD_HEADS_PER_STEP):
            hl = _head_lanes(g)
            acc = acc_ref[g]
            l = -acc[:, STAT_COL_V:STAT_COL_V + 1]
            o_ref[:, hl] = (acc / l).astype(BF16)
            qa_ref[:, hl] = _put_stats(q_ref[:, hl], m_ref[g] + jnp.log(l), STAT_COL_QK)

    blk, head = _flash_specs(S, t, FWD_HEADS_PER_STEP)
    return pl.pallas_call(
        body, name="mla_flash_fwd", grid=(MLA_HEADS // FWD_HEADS_PER_STEP, S // t),
        in_specs=[blk, head, head], out_specs=[blk, blk],
        out_shape=[jax.ShapeDtypeStruct(q.shape, BF16), jax.ShapeDtypeStruct(q.shape, BF16)],
        scratch_shapes=[pltpu.VMEM((FWD_HEADS_PER_STEP, t, 1), F32), pltpu.VMEM((FWD_HEADS_PER_STEP, t, HEAD_PAD), F32)],
        compiler_params=_params(("parallel", "arbitrary")),
    )(q, k, v)


def _flash_bwd(qa, k, v, doa):
    S = qa.shape[0]
    t = _pick(S, (1024, 512, 256, 128))
    n = S // t
    half = t // 2

    def body(qa_ref, k_ref, v_ref, do_ref, dq_ref, dk_ref, dv_ref, dk_acc, dv_acc):
        kj = pl.program_id(1)

        @pl.when(kj == 0)
        def _():
            dq_ref[...] = jnp.zeros_like(dq_ref)

        dk_acc[...] = jnp.zeros_like(dk_acc)
        dv_acc[...] = jnp.zeros_like(dv_acc)

        def step(q_first, q_len, keys, diag):
            rows = pl.ds(pl.multiple_of(q_first, q_len), q_len)
            for g in range(HEADS_PER_STEP):
                hl = _head_lanes(g)
                qa, do, kk = qa_ref[rows, hl], do_ref[rows, hl], k_ref[keys, hl]
                st = _dot_nt(kk, qa)
                if diag:
                    st = jnp.where(_lower_tri(st.shape, False), st, NEG)
                pt = jnp.exp(st)
                dst = (pt * _dot_nt(v_ref[keys, hl], do)).astype(BF16)
                dv_acc[keys, hl] += _dot_nn(pt.astype(BF16), do)
                dk_acc[keys, hl] += _dot_nn(dst, qa)
                dq_ref[rows, hl] += _dot_tn(dst, kk)

        def loop(qi, carry):
            step(qi * t, t, slice(0, t), False)
            return carry

        lo, hi = slice(0, half), slice(half, t)
        step(kj * t, half, lo, True)
        step(kj * t + half, half, lo, False)
        step(kj * t + half, half, hi, True)
        lax.fori_loop(kj + 1, n, loop, 0)
        dk_ref[...] = dk_acc[...].astype(BF16)
        dv_ref[...] = dv_acc[...].astype(BF16)

    blk, head = _flash_specs(S, t, HEADS_PER_STEP)
    w = HEADS_PER_STEP * HEAD_PAD
    return pl.pallas_call(
        body, name="mla_flash_bwd", grid=(MLA_HEADS // HEADS_PER_STEP, n),
        in_specs=[head, blk, blk, head],
        out_specs=[head, blk, pl.BlockSpec((t, w), lambda h, i: (i, h + qa.shape[1] // w))],
        out_shape=[jax.ShapeDtypeStruct(qa.shape, F32), jax.ShapeDtypeStruct(qa.shape, BF16),
                   jax.ShapeDtypeStruct((S, 2 * qa.shape[1]), BF16)],
        scratch_shapes=[pltpu.VMEM((t, w), F32), pltpu.VMEM((t, w), F32)],
        compiler_params=_params(("parallel", "arbitrary")),
    )(qa, k, v, doa)


def _xattn_fwd(xq, kvx):
    W = X_HEADS * X_HEAD_DIM

    def body(q_ref, kv_ref, o_ref):
        for h in range(X_HEADS):
            lo = h * X_HEAD_DIM
            s = _dot_nt(q_ref[:, lo:lo + X_HEAD_DIM], kv_ref[:, lo:lo + X_HEAD_DIM])
            p = jnp.exp(s - jnp.max(s, axis=-1, keepdims=True))
            p = p / jnp.sum(p, axis=-1, keepdims=True)
            o_ref[:, lo:lo + X_HEAD_DIM] = _dot_nn(p.astype(BF16), kv_ref[:, W + lo:W + lo + X_HEAD_DIM]).astype(BF16)

    return _rows("xattn_fwd", body, [xq], [kvx], [(W, BF16)])[0]


def _xattn_bwd(xq, kvx, dox):
    W = X_HEADS * X_HEAD_DIM
    scale = X_HEAD_DIM ** -0.5

    def body(q_ref, do_ref, kv_ref, dq_ref, dkv_ref):
        @pl.when(pl.program_id(0) == 0)
        def _():
            dkv_ref[...] = jnp.zeros_like(dkv_ref)

        for h in range(X_HEADS):
            lo = h * X_HEAD_DIM
            q, k = q_ref[:, lo:lo + X_HEAD_DIM], kv_ref[:, lo:lo + X_HEAD_DIM]
            v, do = kv_ref[:, W + lo:W + lo + X_HEAD_DIM], do_ref[:, lo:lo + X_HEAD_DIM]
            s = _dot_nt(q, k)
            p = jnp.exp(s - jnp.max(s, axis=-1, keepdims=True))
            p = p / jnp.sum(p, axis=-1, keepdims=True)
            dp = _dot_nt(do, v)
            ds = (p * (dp - jnp.sum(dp * p, axis=-1, keepdims=True))).astype(BF16)
            dq_ref[:, lo:lo + X_HEAD_DIM] = (_dot_nn(ds, k) * scale).astype(BF16)
            dkv_ref[:, lo:lo + X_HEAD_DIM] += _dot_tn(ds, q)
            dkv_ref[:, W + lo:W + lo + X_HEAD_DIM] += _dot_tn(p.astype(BF16), do)

    return _rows("xattn_bwd", body, [xq, dox], [kvx], [(W, BF16)], [kvx.shape])


def _adamw(name, w, g, m, v):
    c1 = 1.0 / (1.0 - ADAM_B1 ** ADAM_STEP)
    c2 = 1.0 / (1.0 - ADAM_B2 ** ADAM_STEP)
    lead = (0,) * (w.ndim - 2)
    w2 = w.reshape((1,) * (2 - w.ndim) + w.shape) if w.ndim < 2 else w
    m2, v2 = m.reshape(w2.shape), v.reshape(w2.shape)
    g2 = g.reshape(w2.shape[-2:])
    R, C = g2.shape
    t = _pick(R, (256, 128, 64, 32, 16, 8))

    def body(w_ref, g_ref, m_ref, v_ref, go_ref, d_ref, nm_ref, nv_ref):
        g = g_ref[...]
        nm = ADAM_B1 * m_ref[lead] + (1.0 - ADAM_B1) * g
        nv = ADAM_B2 * v_ref[lead] + (1.0 - ADAM_B2) * (g * g)
        go_ref[lead] = g
        d_ref[lead] = -ADAM_LR * ((nm * c1) / (jnp.sqrt(nv * c2) + ADAM_EPS) + ADAM_WD * w_ref[lead])
        nm_ref[lead] = nm
        nv_ref[lead] = nv

    full = pl.BlockSpec((1,) * len(lead) + (t, C), lambda i: lead + (i, 0))
    outs = pl.pallas_call(
        body, name=name, grid=(R // t,), in_specs=[full, pl.BlockSpec((t, C), lambda i: (i, 0)), full, full],
        out_specs=[full] * 4, out_shape=[jax.ShapeDtypeStruct(w2.shape, F32)] * 4, compiler_params=_params(("parallel",)),
    )(w2, g2, m2, v2)
    return [o.reshape(w.shape) for o in outs]


def _place():
    x, y, c = lax.axis_index("x"), lax.axis_index("y"), lax.axis_index("c")
    return x, y, c, [(1 - x, y), (x, 1 - y), (1 - x, 1 - y)]


_ANY = pl.BlockSpec(memory_space=pl.ANY)


_HBM = pl.BlockSpec(memory_space=pltpu.HBM)
_SEM = pl.BlockSpec(memory_space=pltpu.SEMAPHORE)
_SIDE_EFFECT = pltpu.SideEffectType.DATAFLOW_SIDE_EFFECTING


def _gather_copy(src, land, slot, send, recv, k, chip, c):
    return pltpu.make_async_remote_copy(src_ref=src, dst_ref=land.at[slot], send_sem=send.at[k], recv_sem=recv.at[k],
                                        device_id=(chip[0], chip[1], c), device_id_type=MESH)


def _gather_start(shards, groups):
    n, ng = len(shards), len(groups)
    mine = 2 * lax.axis_index("x") + lax.axis_index("y")
    lands = [lax.dynamic_update_slice(lax.empty((N_CHIPS,) + s.shape, s.dtype), s[None], (mine,) + (0,) * s.ndim)
             for s in shards]

    def body(*refs):
        ins, lnd = refs[:n], refs[n:2 * n]
        sends, recvs = refs[2 * n:2 * n + ng], refs[2 * n + ng:2 * n + 2 * ng]
        token = refs[-1]
        x, y, c, chips = _place()
        for gi, group in enumerate(groups):
            for pos, w in enumerate(group):
                for j, chip in enumerate(chips):
                    _gather_copy(ins[w], lnd[w], 2 * x + y, sends[gi], recvs[gi], 3 * pos + j, chip, c).start()
        token[...] = jnp.zeros_like(token)

    sems = [pltpu.SemaphoreType.DMA((3 * len(g),)) for g in groups]
    res = pl.pallas_call(
        body, name="gather_weights_start",
        out_shape=sems + sems + [pltpu.HBM(a.shape, a.dtype) for a in shards + lands] + [jax.ShapeDtypeStruct((8, LANES), F32)],
        in_specs=[_HBM] * (2 * n),
        out_specs=[_SEM] * (2 * ng) + [_HBM] * (2 * n) + [pl.BlockSpec(memory_space=pltpu.VMEM)],
        input_output_aliases={i: 2 * ng + i for i in range(2 * n)},
        compiler_params=pltpu.CompilerParams(has_side_effects=_SIDE_EFFECT),
    )(*[pltpu.with_memory_space_constraint(a, pltpu.HBM) for a in shards + lands])
    sem_pairs = list(zip(res[:ng], res[ng:2 * ng]))
    return sem_pairs, res[2 * ng:2 * ng + n], res[2 * ng + n:2 * ng + 2 * n], res[-1]


def _gather_wait(name, sem_pair, shards_thru, lands_thru, after):
    m = len(shards_thru)
    after = after if isinstance(after, (tuple, list)) else (after,)

    def body(*refs):
        ins, lnd = refs[:m], refs[m:2 * m]
        send, recv = refs[2 * m], refs[2 * m + 1]
        x, y, c, chips = _place()
        for pos in range(m):
            for j, chip in enumerate(chips):
                cp = _gather_copy(ins[pos], lnd[pos], 2 * chip[0] + chip[1], send, recv, 3 * pos + j, chip, c)
                cp.wait_send()
                cp.wait_recv()

    res = pl.pallas_call(
        body, name=name,
        out_shape=[pltpu.HBM(a.shape, a.dtype) for a in list(shards_thru) + list(lands_thru)],
        in_specs=[_HBM] * (2 * m) + [_SEM, _SEM] + [_ANY] * len(after), out_specs=[_HBM] * (2 * m),
        input_output_aliases={i: i for i in range(2 * m)},
        compiler_params=pltpu.CompilerParams(has_side_effects=_SIDE_EFFECT),
    )(*shards_thru, *lands_thru, *sem_pair, *after)
    return res[m:]


def _pair_exchange(name, packs):
    n = len(packs)

    def body(*refs):
        ins, outs, send, recv = refs[:n], refs[n:2 * n], refs[2 * n], refs[2 * n + 1]
        x, y, c, _ = _place()
        cps = []
        for g in range(n):
            cp = pltpu.make_async_remote_copy(src_ref=ins[g].at[:, pl.ds(1 - c, 1)], dst_ref=outs[g], send_sem=send.at[g],
                                              recv_sem=recv.at[g], device_id=(x, y, 1 - c), device_id_type=MESH)
            cp.start()
            cps.append(cp)
        for cp in cps:
            cp.wait()

    return pl.pallas_call(
        body, name=name, in_specs=[_ANY] * n, out_specs=[_ANY] * n,
        out_shape=[jax.ShapeDtypeStruct((N_CHIPS, 1) + p.shape[2:], p.dtype) for p in packs],
        scratch_shapes=[pltpu.SemaphoreType.DMA((n,)), pltpu.SemaphoreType.DMA((n,))],
    )(*packs)


def _chip_copy(src, land, src_slot, dst_slot, send, recv, k, chip, c):
    return pltpu.make_async_remote_copy(src_ref=src.at[src_slot], dst_ref=land.at[dst_slot], send_sem=send.at[k],
                                        recv_sem=recv.at[k], device_id=(chip[0], chip[1], c), device_id_type=MESH)


def _chip_exchange_start(name, parts):
    n = len(parts)
    lands = [lax.empty(p.shape, p.dtype) for p in parts]

    def body(*refs):
        ins, lnd, send, recv, token = refs[:n], refs[n:2 * n], refs[2 * n], refs[2 * n + 1], refs[-1]
        x, y, c, chips = _place()
        for g in range(n):
            for j, chip in enumerate(chips):
                _chip_copy(ins[g], lnd[g], 2 * chip[0] + chip[1], 2 * x + y, send, recv, 3 * g + j, chip, c).start()
        token[...] = jnp.zeros_like(token)

    sems = [pltpu.SemaphoreType.DMA((3 * n,))] * 2
    res = pl.pallas_call(
        body, name=name,
        out_shape=sems + [pltpu.HBM(a.shape, a.dtype) for a in list(parts) + lands] + [jax.ShapeDtypeStruct((8, LANES), F32)],
        in_specs=[_HBM] * (2 * n),
        out_specs=[_SEM] * 2 + [_HBM] * (2 * n) + [pl.BlockSpec(memory_space=pltpu.VMEM)],
        input_output_aliases={i: 2 + i for i in range(2 * n)},
        compiler_params=pltpu.CompilerParams(has_side_effects=_SIDE_EFFECT),
    )(*[pltpu.with_memory_space_constraint(a, pltpu.HBM) for a in list(parts) + lands])
    return res[:2], res[2:2 + n], res[2 + n:2 + 2 * n], res[-1]


def _chip_exchange_wait(name, sems, parts_thru, lands_thru, after):
    n = len(parts_thru)

    def body(*refs):
        ins, lnd, send, recv = refs[:n], refs[n:2 * n], refs[2 * n], refs[2 * n + 1]
        x, y, c, chips = _place()
        for g in range(n):
            for j, chip in enumerate(chips):
                cp = _chip_copy(ins[g], lnd[g], 2 * x + y, 2 * chip[0] + chip[1], send, recv, 3 * g + j, chip, c)
                cp.wait_send()
                cp.wait_recv()

    res = pl.pallas_call(
        body, name=name,
        out_shape=[pltpu.HBM(a.shape, a.dtype) for a in list(parts_thru) + list(lands_thru)],
        in_specs=[_HBM] * (2 * n) + [_SEM, _SEM, _ANY], out_specs=[_HBM] * (2 * n),
        input_output_aliases={i: i for i in range(2 * n)},
        compiler_params=pltpu.CompilerParams(has_side_effects=_SIDE_EFFECT),
    )(*parts_thru, *lands_thru, *sems, after)
    return res[:n], res[n:]


def _pair_share(halves):
    n = len(halves)

    def body(*refs):
        outs, send, recv = refs[n:2 * n], refs[2 * n], refs[2 * n + 1]
        x, y, c, _ = _place()
        cps = []
        for g in range(n):
            cp = pltpu.make_async_remote_copy(src_ref=outs[g].at[c], dst_ref=outs[g].at[c], send_sem=send.at[g],
                                              recv_sem=recv.at[g], device_id=(x, y, 1 - c), device_id_type=MESH)
            cp.start()
            cps.append(cp)
        for g in range(n):
            pltpu.make_async_remote_copy(src_ref=outs[g].at[c], dst_ref=outs[g].at[1 - c], send_sem=send.at[g],
                                         recv_sem=recv.at[g], device_id=(x, y, 1 - c), device_id_type=MESH).wait_recv()
        for cp in cps:
            cp.wait_send()

    return pl.pallas_call(
        body, name="grad_pair_share", in_specs=[_ANY] * n, out_specs=[_ANY] * n,
        out_shape=[jax.ShapeDtypeStruct(h.shape, h.dtype) for h in halves],
        input_output_aliases={g: g for g in range(n)},
        scratch_shapes=[pltpu.SemaphoreType.DMA((n,)), pltpu.SemaphoreType.DMA((n,))],
    )(*halves)


def _sum_over_devices(block):
    R, L = block.shape

    def gather(b_ref, o_ref, send, recv, loc):
        x, y, c, _ = _place()
        lc = pltpu.make_async_copy(b_ref, o_ref.at[4 * x + 2 * y + c], loc)
        lc.start()
        peers = [(1 - x if dx else x, 1 - y if dy else y, 1 - c if dc else c)
                 for dx in (0, 1) for dy in (0, 1) for dc in (0, 1) if dx or dy or dc]
        cps = []
        for j, peer in enumerate(peers):
            cp = pltpu.make_async_remote_copy(src_ref=b_ref, dst_ref=o_ref.at[4 * x + 2 * y + c], send_sem=send.at[j],
                                              recv_sem=recv.at[j], device_id=peer, device_id_type=MESH)
            cp.start()
            cps.append(cp)
        for j, (px, py, pc) in enumerate(peers):
            pltpu.make_async_remote_copy(src_ref=b_ref, dst_ref=o_ref.at[4 * px + 2 * py + pc], send_sem=send.at[j],
                                         recv_sem=recv.at[j], device_id=(px, py, pc), device_id_type=MESH).wait_recv()
        for cp in cps:
            cp.wait_send()
        lc.wait()

    blocks = pl.pallas_call(
        gather, name="small_grads_gather", in_specs=[_ANY], out_specs=_ANY,
        out_shape=jax.ShapeDtypeStruct((N_DEV, R, L), F32),
        scratch_shapes=[pltpu.SemaphoreType.DMA((N_DEV - 1,)), pltpu.SemaphoreType.DMA((N_DEV - 1,)), pltpu.SemaphoreType.DMA],
    )(block)

    def add(b_ref, o_ref):
        total = b_ref[0]
        for d in range(1, N_DEV):
            total = total + b_ref[d]
        o_ref[...] = total

    return pl.pallas_call(add, name="small_grads_add", out_shape=jax.ShapeDtypeStruct((R, L), F32))(blocks)


def _pair_add(name, pack, got):
    _, _, R, C = pack.shape
    t = _pick(R, ROW_TILES)
    c = lax.axis_index("c").astype(jnp.int32).reshape(1)

    def body(c_ref, p_ref, g_ref, o_ref):
        o_ref[...] = (p_ref[...].astype(F32) + g_ref[...].astype(F32)).astype(BF16)

    return pl.pallas_call(
        body, name=name,
        grid_spec=pltpu.PrefetchScalarGridSpec(
            num_scalar_prefetch=1, grid=(N_CHIPS, R // t),
            in_specs=[pl.BlockSpec((None, None, t, C), lambda k, i, c_ref: (k, c_ref[0], i, 0)),
                      pl.BlockSpec((None, None, t, C), lambda k, i, c_ref: (k, 0, i, 0))],
            out_specs=pl.BlockSpec((None, t, C), lambda k, i, c_ref: (k, i, 0))),
        out_shape=jax.ShapeDtypeStruct((N_CHIPS, R, C), BF16), compiler_params=_params(("parallel", "parallel")),
    )(c, pack, got)


def _chip_add(name, own, got):
    _, R, C = own.shape
    t = _pick(R, ROW_TILES)
    x, y, c, _ = _place()
    place = jnp.stack([c, 2 * x + y]).astype(jnp.int32)

    def body(place_ref, own_ref, g1_ref, g2_ref, g3_ref, o_ref):
        o_ref[...] = ((own_ref[...].astype(F32) + g1_ref[...].astype(F32)) + g2_ref[...].astype(F32)) + g3_ref[...].astype(F32)

    def other(d):
        return pl.BlockSpec((None, t, C), lambda i, place_ref: ((place_ref[1] + d) % N_CHIPS, i, 0))

    return pl.pallas_call(
        body, name=name,
        grid_spec=pltpu.PrefetchScalarGridSpec(
            num_scalar_prefetch=1, grid=(R // t,),
            in_specs=[pl.BlockSpec((None, t, C), lambda i, place_ref: (place_ref[1], i, 0)), other(1), other(2), other(3)],
            out_specs=pl.BlockSpec((None, t, C), lambda i, place_ref: (place_ref[0], i, 0))),
        out_shape=jax.ShapeDtypeStruct((2, R, C), F32), compiler_params=_params(("parallel",)),
    )(place, own, got, got, got)


_CUT = (2 * CONV_CH, 2 * CONV_CH + Q_LORA, 2 * CONV_CH + Q_LORA + KV_LORA, 2 * CONV_CH + Q_LORA + KV_LORA + QK_ROPE)
W_IN_GATES = (0, 2 * D_MODEL)
W_IN_CONV = (W_IN_GATES[1], 2 * CONV_CH)
W_IN_CQ = (W_IN_CONV[0] + W_IN_CONV[1], Q_LORA)
W_IN_KR = (W_IN_CQ[0] + W_IN_CQ[1], HEAD_PAD)
W_IN_CKV = (W_IN_KR[0] + W_IN_KR[1], KV_LORA)
W_IN_LORA = (W_IN_CQ[0], Q_LORA + HEAD_PAD + KV_LORA)
W_IN_COLS = W_IN_CKV[0] + W_IN_CKV[1]


def _pad_last(a, n):
    return jnp.pad(a, [(0, 0)] * (a.ndim - 1) + [(0, n - a.shape[-1])])


def _layout_w_in(w_in):
    kr = jnp.pad(w_in[:, _CUT[2]:_CUT[3]], ((0, 0), (QK_NOPE, HEAD_PAD - QK_NOPE - QK_ROPE)))
    return jnp.concatenate([w_in[:, _CUT[3]:], w_in[:, :_CUT[0]], w_in[:, _CUT[0]:_CUT[1]], kr, w_in[:, _CUT[1]:_CUT[2]]], axis=1)


def _layout_weights(w):
    out = dict(w)
    if "w_uq" in w:
        out["w_uq"] = _pad_last(w["w_uq"].reshape(Q_LORA, MLA_HEADS, QK_NOPE + QK_ROPE), HEAD_PAD).reshape(Q_LORA, MLA_HEADS * HEAD_PAD)
    if "w_ukv" in w:
        ukv = w["w_ukv"].reshape(KV_LORA, MLA_HEADS, QK_NOPE + V_DIM)
        uk = _pad_last(ukv[:, :, :QK_NOPE], HEAD_PAD).reshape(KV_LORA, MLA_HEADS * HEAD_PAD)
        uv = _pad_last(ukv[:, :, QK_NOPE:], HEAD_PAD).reshape(KV_LORA, MLA_HEADS * HEAD_PAD)
        out["w_ukv"] = jnp.concatenate([uk, uv], axis=1)
    if "w_mla_out" in w:
        mo = jnp.pad(w["w_mla_out"].reshape(MLA_HEADS, V_DIM, D_MODEL), ((0, 0), (0, HEAD_PAD - V_DIM), (0, 0)))
        out["w_mla_out"] = mo.reshape(MLA_HEADS * HEAD_PAD, D_MODEL)
    return out


def _unlayout_grads(g):
    out = dict(g)
    if "w_in" in g:
        gi = g["w_in"]
        win = lambda w: gi[:, w[0]:w[0] + w[1]]
        kr = gi[:, W_IN_KR[0] + QK_NOPE:W_IN_KR[0] + QK_NOPE + QK_ROPE]
        out["w_in"] = jnp.concatenate([win(W_IN_CONV), win(W_IN_CQ), win(W_IN_CKV), kr, win(W_IN_GATES)], axis=1)
    if "w_uq" in g:
        out["w_uq"] = g["w_uq"].reshape(Q_LORA, MLA_HEADS, HEAD_PAD)[:, :, :QK_NOPE + QK_ROPE].reshape(Q_LORA, -1)
    if "w_ukv" in g:
        gk = g["w_ukv"][:, :MLA_HEADS * HEAD_PAD].reshape(KV_LORA, MLA_HEADS, HEAD_PAD)[:, :, :QK_NOPE]
        gv = g["w_ukv"][:, MLA_HEADS * HEAD_PAD:].reshape(KV_LORA, MLA_HEADS, HEAD_PAD)[:, :, :V_DIM]
        out["w_ukv"] = jnp.concatenate([gk, gv], axis=2).reshape(KV_LORA, -1)
    if "w_mla_out" in g:
        out["w_mla_out"] = g["w_mla_out"].reshape(MLA_HEADS, HEAD_PAD, D_MODEL)[:, :V_DIM].reshape(MLA_HEADS * V_DIM, D_MODEL)
    return out


def _rope_tables(positions):
    half = QK_ROPE // 2
    inv_freq = ROPE_THETA ** (-jnp.arange(half, dtype=F32) / half)
    ang = positions.astype(F32).reshape(-1, 1) * inv_freq
    cos, sin = jnp.cos(ang), jnp.sin(ang)
    S = cos.shape[0]
    z16, z32, z64 = jnp.zeros((S, half), F32), jnp.zeros((S, QK_ROPE), F32), jnp.zeros((S, QK_NOPE), F32)
    c = jnp.concatenate([jnp.ones((S, QK_NOPE), F32), cos, cos, z32], axis=1)
    sa = jnp.concatenate([z64, -sin, z16, z32], axis=1)
    sb = jnp.concatenate([z64, z16, sin, z32], axis=1)
    return c, sa, sb


def _local_step(x, mem, positions, target, weight_fns, early_grads_fn, late_grads_fn, sm):
    S = x.shape[0]
    HW = MLA_HEADS * HEAD_PAD
    rope_c, rope_sa, rope_sb = _rope_tables(positions)
    qk_scale = (QK_NOPE + QK_ROPE) ** -0.5

    def k_rms1(x_ref, g_ref, u_ref):
        u_ref[...] = _rms_fwd(x_ref[...], g_ref[...]).astype(BF16)

    u1, = _rows("rms_mix", k_rms1, [x], [sm["norm_mix_g"]], [(D_MODEL, BF16)])
    w_in, conv_w = weight_fns[0]((u1, rope_c, rope_sa, rope_sb))

    def epi_glu(acc, xs, outs):
        a, gt = acc[:, 0:CONV_CH], acc[:, CONV_CH:2 * CONV_CH]
        outs[0][...] = acc[...].astype(BF16)
        outs[1][...] = (a * _sigmoid(gt)).astype(BF16)

    conv_in, z0 = _mm("proj_conv", u1, w_in, "nn", [(2 * CONV_CH, BF16), (CONV_CH, BF16)], epi_glu, b_cols=W_IN_CONV)
    c_q = _mm_plain("proj_cq", u1, w_in, "nn", dtype=BF16, b_cols=W_IN_CQ)
    c_kv = _mm_plain("proj_ckv", u1, w_in, "nn", dtype=BF16, b_cols=W_IN_CKV)
    kr_raw = _mm_plain("proj_krope", u1, w_in, "nn", b_cols=W_IN_KR)

    def epi_sigmoid(acc, xs, outs):
        outs[0][...] = _sigmoid(acc[...]).astype(BF16)

    gates, = _mm("proj_gates", u1, w_in, "nn", [(2 * D_MODEL, BF16)], epi_sigmoid, b_cols=W_IN_GATES)

    z1, z3 = _conv_fwd(z0, conv_w, sm["conv_b"], sm["conv_ln_g"], sm["conv_ln_b"])
    wl = weight_fns[1](z1)
    conv_out = _mm_plain("conv_out", z3, wl["w_conv_out"], "nn", dtype=BF16)

    def k_lora_norm(cq_ref, ckv_ref, gq_ref, gkv_ref, qn_ref, kvn_ref):
        qn_ref[...] = _rms_fwd(cq_ref[...].astype(F32), gq_ref[...]).astype(BF16)
        kvn_ref[...] = _rms_fwd(ckv_ref[...].astype(F32), gkv_ref[...]).astype(BF16)

    qn, kvn = _rows("lora_norm", k_lora_norm, [c_q, c_kv], [sm["q_norm_g"], sm["kv_norm_g"]],
                    [(Q_LORA, BF16), (KV_LORA, BF16)])

    def epi_q(acc, xs, outs):
        c, sa, sb = xs[0][...], xs[1][...], xs[2][...]
        for h in range(MLA_HEADS):
            lo = h * HEAD_PAD
            outs[0][:, lo:lo + HEAD_PAD] = (_rope(acc[:, lo:lo + HEAD_PAD], c, sa, sb, 1.0) * qk_scale).astype(BF16)

    q_att, = _mm("q_up", qn, wl["w_uq"], "nn", [(HW, BF16)], epi_q, row_x=[rope_c, rope_sa, rope_sb], tn=HW)

    def epi_kv(acc, xs, outs):
        kr = _rope(xs[0][...], xs[1][...], xs[2][...], xs[3][...], 1.0)
        kr = kr + _neg_ones(kr.shape, STAT_COL_QK)
        vconst = _neg_ones(kr.shape, STAT_COL_V)
        for h in range(MLA_HEADS):
            lo = h * HEAD_PAD
            outs[0][:, lo:lo + HEAD_PAD] = (acc[:, lo:lo + HEAD_PAD] + kr).astype(BF16)
            outs[1][:, lo:lo + HEAD_PAD] = (acc[:, HW + lo:HW + lo + HEAD_PAD] + vconst).astype(BF16)

    k_att, v_att = _mm("kv_up", kvn, wl["w_ukv"], "nn", [(HW, BF16), (HW, BF16)], epi_kv,
                       row_x=[kr_raw, rope_c, rope_sa, rope_sb], tn=2 * HW)

    o_att, q_aug = _flash_fwd(q_att, k_att, v_att)
    wl.update(weight_fns[2](o_att))

    def epi_merge(acc, xs, outs):
        for rows in _row_chunks(acc.shape[0]):
            mo = acc[rows, :]
            g0, g1 = xs[0][rows, 0:D_MODEL].astype(F32), xs[0][rows, D_MODEL:].astype(F32)
            outs[0][rows, :] = mo.astype(BF16)
            outs[1][rows, :] = (g0 * xs[1][rows, :].astype(F32) + g1 * mo).astype(BF16)

    mla_out, merged = _mm("mla_out_merge", o_att, wl["w_mla_out"], "nn", [(D_MODEL, BF16), (D_MODEL, BF16)], epi_merge,
                          row_x=[gates, conv_out], tn=D_MODEL)

    def epi_res_norm(acc, xs, outs):
        h = xs[0][...] + acc[...]
        outs[0][...] = h
        outs[1][...] = _rms_fwd(h, xs[1][...]).astype(BF16)

    h1, u2 = _mm("mix_out", merged, wl["w_out"], "nn", [(D_MODEL, F32), (D_MODEL, BF16)], epi_res_norm,
                 row_x=[x], vec_x=[sm["norm_xattn_g"]], tn=D_MODEL)

    xscale = X_HEAD_DIM ** -0.5

    def epi_scale(acc, xs, outs):
        outs[0][...] = (acc[...] * xscale).astype(BF16)

    xq, = _mm("xattn_q", u2, wl["w_xq"], "nn", [(X_HEADS * X_HEAD_DIM, BF16)], epi_scale)

    def k_mem_norm(m_ref, g_ref, o_ref):
        o_ref[...] = _rms_fwd(m_ref[...], g_ref[...]).astype(BF16)

    mem_n, = _rows("mem_norm", k_mem_norm, [mem], [sm["norm_mem_g"]], [(D_MODEL, BF16)])
    kvx = _mm_plain("xattn_kv", mem_n, wl["w_xkv"], "nn", dtype=BF16)
    ox = _xattn_fwd(xq, kvx)
    h2, u3 = _mm("xattn_out", ox, wl["w_xo"], "nn", [(D_MODEL, F32), (D_MODEL, BF16)], epi_res_norm,
                 row_x=[h1], vec_x=[sm["norm_mlp_g"]], tn=D_MODEL)

    def epi_relu2(acc, xs, outs):
        r = jnp.maximum(acc[...], 0.0)
        outs[0][...] = (r * r).astype(BF16)

    hid, = _mm("mlp_up", u3, wl["w_mlp1"], "nn", [(D_FF, BF16)], epi_relu2)

    def epi_final(acc, xs, outs):
        g = xs[2][...]
        for rows in _row_chunks(acc.shape[0]):
            h = xs[0][rows, :] + acc[rows, :]
            e = _rms_fwd(h, g) - xs[1][rows, :]
            part = 0.5 * jnp.sum(jnp.mean(e * e, axis=-1, keepdims=True), axis=0, keepdims=True)
            dh, dg = _rms_bwd(h, g, e * (1.0 / D_MODEL))
            outs[0][rows, :] = dh
            outs[1][rows, :] = dh.astype(BF16)
            _accum(outs[2], jnp.broadcast_to(part, outs[2].shape), first=rows.start == 0)
            _accum(outs[3], dg, first=rows.start == 0)

    dh3, dh3b, loss, g_final = _mm("mlp_down_loss", hid, wl["w_mlp2"], "nn", [(D_MODEL, F32), (D_MODEL, BF16)], epi_final,
                                   row_x=[h2, target], vec_x=[sm["final_norm_g"]], sums=[(1, LANES), (1, D_MODEL)],
                                   tn=D_MODEL, tk=1024)

    def epi_drelu2(acc, xs, outs):
        outs[0][...] = (acc[...] * (2.0 * jnp.sqrt(xs[0][...].astype(F32)))).astype(BF16)

    da1, = _mm("mlp_down_dx", dh3b, wl["w_mlp2"], "nt", [(D_FF, BF16)], epi_drelu2, tile_x=[hid])
    g_mlp2 = _mm_plain("mlp_down_dw", hid, dh3b, "tn", dtype=BF16)
    g_mlp1 = _mm_plain("mlp_up_dw", u3, da1, "tn", dtype=BF16)

    def epi_norm_bwd(acc, xs, outs):
        for rows in _row_chunks(acc.shape[0]):
            dx, dg = _rms_bwd(xs[0][rows, :], xs[2][...], acc[rows, :])
            dh = xs[1][rows, :] + dx
            outs[0][rows, :] = dh
            if len(outs) == 3:
                outs[1][rows, :] = dh.astype(BF16)
            _accum(outs[-1], dg, first=rows.start == 0)

    def dx_norm_bwd(name, dy, w, xin, dres, vecs, with_bf16=True):
        outs = [(D_MODEL, F32), (D_MODEL, BF16)] if with_bf16 else [(D_MODEL, F32)]
        return _mm(name, dy, w, "nt", outs, epi_norm_bwd, row_x=[xin, dres], vec_x=vecs, sums=[(1, D_MODEL)],
                   tn=D_MODEL, tk=_pick(dy.shape[1], (1024, 768, 512)))

    dh2, dh2b, g_norm_mlp = dx_norm_bwd("mlp_up_dx_norm", da1, wl["w_mlp1"], h2, dh3, [sm["norm_mlp_g"]])

    dox = _mm_plain("xattn_out_dx", dh2b, wl["w_xo"], "nt", dtype=BF16)
    g_xo = _mm_plain("xattn_out_dw", ox, dh2b, "tn", dtype=BF16)
    dxq, dkvx = _xattn_bwd(xq, kvx, dox)
    g_xq = _mm_plain("xattn_q_dw", u2, dxq, "tn", dtype=BF16)
    g_xkv = _mm_plain("xattn_kv_dw", mem_n, dkvx, "tn", dtype=BF16)
    dmem_n = _mm_plain("xattn_kv_dx", dkvx, wl["w_xkv"], "nt")

    def k_mem_bwd(m_ref, d_ref, g_ref, dg_ref):
        _, dg = _rms_bwd(m_ref[...], g_ref[...], d_ref[...])
        _accum(dg_ref, dg)

    g_norm_mem, = _rows("mem_norm_bwd", k_mem_bwd, [mem, dmem_n], [sm["norm_mem_g"]], [], [(1, D_MODEL)])
    token = early_grads_fn(dict(w_mlp1=g_mlp1, w_mlp2=g_mlp2, w_xo=g_xo, w_xq=g_xq, w_xkv=g_xkv))
    dh1, dh1b, g_norm_xattn = dx_norm_bwd("xattn_q_dx_norm", dxq, wl["w_xq"], h1, dh2, [sm["norm_xattn_g"], token])

    dmerged = _mm_plain("mix_out_dx", dh1b, wl["w_out"], "nt", dtype=BF16)
    g_out = _mm_plain("mix_out_dw", merged, dh1b, "tn", dtype=BF16)

    def k_merge_bwd(dm_ref, g_ref, co_ref, mo_ref, dco_ref, dmo_ref, dgl_ref):
        dm = dm_ref[...].astype(F32)
        g0, g1 = g_ref[:, 0:D_MODEL].astype(F32), g_ref[:, D_MODEL:].astype(F32)
        dco_ref[...] = (dm * g0).astype(BF16)
        dmo_ref[...] = (dm * g1).astype(BF16)
        dgl_ref[:, 0:D_MODEL] = (dm * co_ref[...].astype(F32) * g0 * (1.0 - g0)).astype(BF16)
        dgl_ref[:, D_MODEL:] = (dm * mo_ref[...].astype(F32) * g1 * (1.0 - g1)).astype(BF16)

    dconv_out, dmla_out, dproj = _rows("merge_bwd", k_merge_bwd, [dmerged, gates, conv_out, mla_out], [],
                                       [(D_MODEL, BF16), (D_MODEL, BF16), (2 * D_MODEL, BF16)], tile=256,
                                       window=(2, W_IN_GATES, W_IN_COLS, None))

    def epi_do(acc, xs, outs):
        for h in range(MLA_HEADS):
            lo = h * HEAD_PAD
            do = acc[:, lo:lo + HEAD_PAD]
            delta = jnp.sum(do * xs[0][:, lo:lo + HEAD_PAD].astype(F32), axis=-1, keepdims=True)
            outs[0][:, lo:lo + HEAD_PAD] = _put_stats(do.astype(BF16), delta, STAT_COL_V)

    do_aug, = _mm("mla_out_dx", dmla_out, wl["w_mla_out"], "nt", [(HW, BF16)], epi_do, row_x=[o_att], tn=HW)
    g_mla_out = _mm_plain("mla_out_dw", o_att, dmla_out, "tn", dtype=BF16)
    dq_att, dk_att, dkv_cat = _flash_bwd(q_aug, k_att, v_att, do_aug)

    def k_rope_bwd(dq_ref, dk_ref, c_ref, sa_ref, sb_ref, dqr_ref, dkv_ref, dkr_ref):
        c, sa, sb = c_ref[...], sa_ref[...], sb_ref[...]
        lane = lax.broadcasted_iota(jnp.int32, c.shape, 1)
        nope = (lane < QK_NOPE).astype(F32)
        ropem = ((lane >= QK_NOPE) & (lane < QK_NOPE + QK_ROPE)).astype(F32)
        dkr = jnp.zeros(c.shape, F32)
        for h in range(MLA_HEADS):
            lo = h * HEAD_PAD
            dqr_ref[:, lo:lo + HEAD_PAD] = (_rope(dq_ref[:, lo:lo + HEAD_PAD], c, sa, sb, -1.0) * qk_scale).astype(BF16)
            dk = dk_ref[:, lo:lo + HEAD_PAD].astype(F32)
            dkv_ref[:, lo:lo + HEAD_PAD] = (dk * nope).astype(BF16)
            dkr = dkr + dk
        dkr_ref[...] = (_rope(dkr * ropem, c, sa, sb, -1.0) * ropem).astype(BF16)

    dq_raw, dkv_cat, dkr = _rows("rope_bwd", k_rope_bwd, [dq_att, dk_att, rope_c, rope_sa, rope_sb], [],
                                 [(HW, BF16), (HW, BF16), (HEAD_PAD, BF16)], tile=256,
                                 window=(1, (0, HW), 2 * HW, dkv_cat))
    g_uq = _mm_plain("q_up_dw", qn, dq_raw, "tn", dtype=BF16)
    dqn = _mm_plain("q_up_dx", dq_raw, wl["w_uq"], "nt")
    g_ukv = _mm_plain("kv_up_dw", kvn, dkv_cat, "tn", dtype=BF16)
    dkvn = _mm_plain("kv_up_dx", dkv_cat, wl["w_ukv"], "nt")

    def k_lora_bwd(cq_ref, ckv_ref, dqn_ref, dkvn_ref, dkr_ref, gq_ref, gkv_ref, out_ref, dgq_ref, dgkv_ref):
        dcq, dgq = _rms_bwd(cq_ref[...].astype(F32), gq_ref[...], dqn_ref[...])
        dckv, dgkv = _rms_bwd(ckv_ref[...].astype(F32), gkv_ref[...], dkvn_ref[...])
        out_ref[:, 0:Q_LORA] = dcq.astype(BF16)
        out_ref[:, Q_LORA:Q_LORA + HEAD_PAD] = dkr_ref[...]
        out_ref[:, Q_LORA + HEAD_PAD:] = dckv.astype(BF16)
        _accum(dgq_ref, dgq)
        _accum(dgkv_ref, dgkv)

    dproj, g_q_norm, g_kv_norm = _rows("lora_norm_bwd", k_lora_bwd, [c_q, c_kv, dqn, dkvn, dkr],
                                       [sm["q_norm_g"], sm["kv_norm_g"]], [(W_IN_LORA[1], BF16)],
                                       [(1, Q_LORA), (1, KV_LORA)], window=(0, W_IN_LORA, W_IN_COLS, dproj))

    dz3 = _mm_plain("conv_out_dx", dconv_out, wl["w_conv_out"], "nt", dtype=BF16)
    g_conv_out = _mm_plain("conv_out_dw", z3, dconv_out, "tn", dtype=BF16)
    dz1, g_ln_g, g_ln_b, g_conv_b = _conv_bwd_norm(dz3, z1, sm["conv_ln_g"], sm["conv_ln_b"])
    dproj, g_conv_w = _conv_bwd_taps(dz1, z0, conv_in, conv_w, dproj, W_IN_CONV)

    g_in = _mm_plain("proj_dw", u1, dproj, "tn", dtype=BF16)
    token = late_grads_fn(dict(w_in=g_in, conv_w=g_conv_w[:CONV_WIDTH], w_conv_out=g_conv_out, w_uq=g_uq, w_ukv=g_ukv,
                               w_mla_out=g_mla_out, w_out=g_out))
    grad_x, g_norm_mix = dx_norm_bwd("proj_dx_norm", dproj, w_in, x, dh1, [sm["norm_mix_g"], token], with_bf16=False)

    small = dict(norm_mix_g=g_norm_mix, conv_b=g_conv_b, conv_ln_g=g_ln_g, conv_ln_b=g_ln_b, q_norm_g=g_q_norm,
                 kv_norm_g=g_kv_norm, norm_xattn_g=g_norm_xattn, norm_mem_g=g_norm_mem, norm_mlp_g=g_norm_mlp,
                 final_norm_g=g_final)
    return loss, grad_x, small


def _shard(a, k, axis):
    n = a.shape[axis] // N_CHIPS
    return lax.slice_in_dim(a, k * n, (k + 1) * n, axis=axis)


def _pack_small(grads, loss):
    flat = jnp.concatenate([grads[n].reshape(-1) for n in SMALL] + [loss.reshape(-1)[:1]])
    rows = -(-flat.shape[0] // (8 * LANES)) * 8
    return jnp.pad(flat, (0, rows * LANES - flat.shape[0])).reshape(rows, LANES)


def _pack_groups(shapes, names):
    groups = {}
    for n in names:
        groups.setdefault(shapes[n][1], []).append(n)
    return groups


def _pad_rows(a, mult):
    return jnp.pad(a, ((0, -a.shape[0] % mult), (0, 0)))


def _pack_grads(grads, shapes, names):
    packs = []
    for width, group in _pack_groups(shapes, names).items():
        per_chip = [jnp.concatenate([_pad_rows(_shard(grads[n], k, SHARD_AXIS[n]).astype(BF16), PACK_ROW_ALIGN) for n in group])
                    for k in range(N_CHIPS)]
        rows = per_chip[0].shape[0]
        packs.append(jnp.stack(per_chip).reshape(N_CHIPS, 2, rows // 2, width))
    return packs


def _unpack_grads(fulls, shapes, names):
    out = {}
    for full, group in zip(fulls, _pack_groups(shapes, names).values()):
        flat, at = full.reshape(-1, full.shape[-1]), 0
        for n in group:
            rows = shapes[n][0]
            out[n] = flat[at:at + rows]
            at += rows + (-rows % PACK_ROW_ALIGN)
    return out


def _unpack(flat, names, shapes):
    out, at = {}, 0
    for n in names:
        size = math.prod(shapes[n])
        out[n] = flat[at:at + size].reshape(shapes[n])
        at += size
    return out, at


def kernel(x, mem, positions, norm_mix_g, w_in, conv_w, conv_b, conv_ln_g, conv_ln_b, w_conv_out, q_norm_g, w_uq, kv_norm_g, w_ukv, w_mla_out, w_out, norm_xattn_g, norm_mem_g, w_xq, w_xkv, w_xo, norm_mlp_g, w_mlp1, w_mlp2, final_norm_g, loss_target, m_norm_mix_g, m_w_in, m_conv_w, m_conv_b, m_conv_ln_g, m_conv_ln_b, m_w_conv_out, m_q_norm_g, m_w_uq, m_kv_norm_g, m_w_ukv, m_w_mla_out, m_w_out, m_norm_xattn_g, m_norm_mem_g, m_w_xq, m_w_xkv, m_w_xo, m_norm_mlp_g, m_w_mlp1, m_w_mlp2, m_final_norm_g, v_norm_mix_g, v_w_in, v_conv_w, v_conv_b, v_conv_ln_g, v_conv_ln_b, v_w_conv_out, v_q_norm_g, v_w_uq, v_kv_norm_g, v_w_ukv, v_w_mla_out, v_w_out, v_norm_xattn_g, v_norm_mem_g, v_w_xq, v_w_xkv, v_w_xo, v_norm_mlp_g, v_w_mlp1, v_w_mlp2, v_final_norm_g):
    args = dict(locals())
    w = {n: args[n] for n in WEIGHTS}
    m = {n: args["m_" + n] for n in WEIGHTS}
    v = {n: args["v_" + n] for n in WEIGHTS}

    shards = [w[n][0].astype(F32 if n == "conv_w" else BF16) for n in BIG]
    bounds = (0,) + WEIGHT_WAITS + (len(BIG),)
    spans = [slice(lo, hi) for lo, hi in zip(bounds[:-1], bounds[1:])]
    sem_pairs, shards_thru, lands_thru, token = _gather_start(shards, [list(range(len(BIG)))[sp] for sp in spans])

    def unshard(n, g):
        ax = SHARD_AXIS[n]
        return jnp.moveaxis(g, 0, ax).reshape(g.shape[1:1 + ax] + (N_CHIPS * g.shape[1 + ax],) + g.shape[2 + ax:])

    def wait_fn(i):
        def fn(after):
            lands = _gather_wait(f"gather_weights_wait_{i}", sem_pairs[i], shards_thru[spans[i]], lands_thru[spans[i]], after)
            full = {n: unshard(n, g) for n, g in zip(BIG[spans[i]], lands)}
            return (_layout_w_in(full["w_in"]), full["conv_w"]) if i == 0 else _layout_weights(full)
        return fn

    sm = {n: w[n].reshape(1, -1) for n in SMALL}
    sm["norm_mix_g"] = sm["norm_mix_g"] + token[0, 0]

    shapes = {n: w[n].shape[1:] if n in BIG else w[n].shape for n in WEIGHTS}
    late_names = [n for n in BIG if n not in EARLY_GRADS]
    inflight = {}

    def send_grads(tag, names):
        def fn(g):
            packs = _pack_grads(_unlayout_grads(g), shapes, names)
            got = _pair_exchange(f"grad_pair_exchange_{tag}", packs)
            pairs = [_pair_add(f"grad_pair_add_{tag}_{i}", p, r) for i, (p, r) in enumerate(zip(packs, got))]
            *inflight[tag], token = _chip_exchange_start(f"grad_chip_exchange_{tag}_start", pairs)
            return token
        return fn

    loss, grad_x, g_small = _local_step(x[0], mem[0], positions, loss_target[0], [wait_fn(i) for i in range(3)],
                                        send_grads("early", EARLY_GRADS), send_grads("late", late_names), sm)

    halves, counts = [], {}
    for tag in ("late", "early"):
        own, got = _chip_exchange_wait(f"grad_chip_exchange_{tag}_wait", *inflight[tag], grad_x)
        halves += [_chip_add(f"grad_chip_add_{tag}_{i}", p, g) for i, (p, g) in enumerate(zip(own, got))]
        counts[tag] = len(own)
    fulls = _pair_share(halves)
    g_sum = _unpack_grads(fulls[:counts["late"]], shapes, late_names)
    g_sum.update(_unpack_grads(fulls[counts["late"]:], shapes, EARLY_GRADS))
    small_flat = _sum_over_devices(_pack_small(g_small, loss)).reshape(-1)
    g_small, at = _unpack(small_flat, SMALL, shapes)
    g_sum.update(g_small)
    loss_sum = small_flat[at]

    out_g, out_d, out_m, out_v = [], [], [], []
    for n in WEIGHTS:
        g, d, nm, nv = _adamw("adamw_" + n, w[n], g_sum[n], m[n], v[n])
        out_g.append(g)
        out_d.append(d)
        out_m.append(nm)
        out_v.append(nv)
    return (loss_sum, grad_x[None], *out_g, *out_d, *out_m, *out_v)
```

```python
import math

import jax
import jax.numpy as jnp
from jax import lax
from jax.experimental import pallas as pl
from jax.experimental.pallas import tpu as pltpu

F32 = jnp.float32
BF16 = jnp.bfloat16
MESH = pl.DeviceIdType.MESH

D_MODEL = 1024
CONV_CH = 512
CONV_WIDTH = 31
MLA_HEADS = 8
QK_NOPE = 64
QK_ROPE = 32
V_DIM = 64
Q_LORA = 384
KV_LORA = 256
MEM_LEN = 256
X_HEADS = 4
X_HEAD_DIM = 128
D_FF = 4096
ROPE_THETA = 10000.0
EPS = 1e-6
HEAD_PAD = 128
STAT_COL_QK = QK_NOPE + QK_ROPE
STAT_COL_V = V_DIM
HALO = 32
N_CHIPS = 4
LANES = 128

ADAM_LR = 0.001
ADAM_B1 = 0.9
ADAM_B2 = 0.999
ADAM_EPS = 1e-08
ADAM_WD = 0.01
ADAM_STEP = 10

VMEM_LIMIT = 52 * 1024 * 1024
ROW_TILES = (1024, 512, 256, 128, 64, 32, 16)
PACK_ROW_ALIGN = 32
N_DEV = 8
NEG = -1e30

BIG = ["w_in", "conv_w", "w_conv_out", "w_uq", "w_ukv", "w_mla_out", "w_out", "w_xq", "w_xkv", "w_xo", "w_mlp1", "w_mlp2"]
WEIGHT_WAITS = (2, 5)
SHARD_AXIS = {"w_in": 1, "w_conv_out": 1, "w_uq": 1, "w_ukv": 1, "w_mla_out": 1, "w_out": 0, "w_xq": 0, "w_xkv": 0,
              "w_xo": 1, "w_mlp1": 1, "w_mlp2": 0, "conv_w": 1}
EARLY_GRADS = ["w_mlp1", "w_mlp2", "w_xkv", "w_xq", "w_xo"]
SMALL = ["norm_mix_g", "conv_b", "conv_ln_g", "conv_ln_b", "q_norm_g", "kv_norm_g", "norm_xattn_g", "norm_mem_g",
         "norm_mlp_g", "final_norm_g"]
WEIGHTS = ["norm_mix_g", "w_in", "conv_w", "conv_b", "conv_ln_g", "conv_ln_b", "w_conv_out", "q_norm_g", "w_uq",
           "kv_norm_g", "w_ukv", "w_mla_out", "w_out", "norm_xattn_g", "norm_mem_g", "w_xq", "w_xkv", "w_xo",
           "norm_mlp_g", "w_mlp1", "w_mlp2", "final_norm_g"]


def _pick(n, prefs):
    for p in prefs:
        if n % p == 0:
            return p
    return n


def _params(sem):
    return pltpu.CompilerParams(dimension_semantics=sem, vmem_limit_bytes=VMEM_LIMIT)


_DIMS = {"nn": (((1,), (0,)), ((), ())), "nt": (((1,), (1,)), ((), ())), "tn": (((0,), (0,)), ((), ()))}


def _mm(name, a, b, mode, outs, epi, row_x=(), tile_x=(), vec_x=(), sums=(), tm=None, tn=None, tk=None, b_cols=None):
    if mode == "nn":
        (M, K), (_, N) = a.shape, b.shape
        if b_cols is not None:
            N = b_cols[1]
    elif mode == "nt":
        (M, K), (N, _) = a.shape, b.shape
    else:
        (K, M), (_, N) = a.shape, b.shape
    tm = tm or _pick(M, (1024, 512, 384, 256, 128))
    tn = tn or _pick(N, (1024, 768, 512, 384, 256, 128))
    tk = tk or _pick(K, (2048, 1920, 1024, 768, 512, 384, 256, 128))
    nk = K // tk
    rows_inner = nk == 1 and N // tn > 1
    grid = (N // tn, M // tm, nk) if rows_inner else (M // tm, N // tn, nk)

    def spec(shape, f):
        return pl.BlockSpec(shape, (lambda j, i, k: f(i, j, k)) if rows_inner else f)

    b_off = 0
    if b_cols is not None:
        assert mode == "nn" and b_cols[0] % tn == 0, (name, b_cols, tn)
        b_off = b_cols[0] // tn
    a_spec = spec((tk, tm), lambda i, j, k: (k, i)) if mode == "tn" else spec((tm, tk), lambda i, j, k: (i, k))
    b_spec = spec((tn, tk), lambda i, j, k: (j, k)) if mode == "nt" else spec((tk, tn), lambda i, j, k: (k, j + b_off))
    in_specs = [a_spec, b_spec]
    in_specs += [spec((tm, r.shape[1]), lambda i, j, k: (i, 0)) for r in row_x]
    in_specs += [spec((tm, tn), lambda i, j, k: (i, j)) for _ in tile_x]
    in_specs += [spec(v.shape, lambda i, j, k: (0, 0)) for v in vec_x]
    out_specs, out_shape = [], []
    for w, dt in outs:
        if tn == N:
            out_specs.append(spec((tm, w), lambda i, j, k: (i, 0)))
        else:
            assert w == N, (name, w, N)
            out_specs.append(spec((tm, tn), lambda i, j, k: (i, j)))
        out_shape.append(jax.ShapeDtypeStruct((M, w), dt))
    for shp in sums:
        assert tn == N and not rows_inner, name
        out_specs.append(spec(shp, lambda i, j, k: (0, 0)))
        out_shape.append(jax.ShapeDtypeStruct(shp, F32))
    nx = len(row_x) + len(tile_x) + len(vec_x)
    dims = _DIMS[mode]

    def body(a_ref, b_ref, *rest):
        x_refs, out_refs, acc_ref = rest[:nx], rest[nx:nx + len(outs) + len(sums)], rest[-1]
        av, bv = a_ref[...], b_ref[...]
        if av.dtype != BF16:
            av = av.astype(BF16)
        if bv.dtype != BF16:
            bv = bv.astype(BF16)
        prod = lax.dot_general(av, bv, dims, preferred_element_type=F32)
        if nk == 1:
            acc_ref[...] = prod
            epi(acc_ref, x_refs, out_refs)
        else:
            k = pl.program_id(2)

            @pl.when(k == 0)
            def _():
                acc_ref[...] = prod

            @pl.when(k > 0)
            def _():
                acc_ref[...] += prod

            @pl.when(k == nk - 1)
            def _():
                epi(acc_ref, x_refs, out_refs)

    res = pl.pallas_call(
        body, name=name, grid=grid, in_specs=in_specs, out_specs=out_specs, out_shape=out_shape,
        scratch_shapes=[pltpu.VMEM((tm, tn), F32)],
        compiler_params=_params(("arbitrary",) * 3 if sums else ("parallel", "parallel", "arbitrary")),
    )(a, b, *row_x, *tile_x, *vec_x)
    return res


def _epi_store(acc_ref, x_refs, out_refs):
    for o in out_refs:
        o[...] = acc_ref[...].astype(o.dtype)


def _mm_plain(name, a, b, mode, dtype=F32, **kw):
    n = kw["b_cols"][1] if kw.get("b_cols") else (b.shape[0] if mode == "nt" else b.shape[1])
    return _mm(name, a, b, mode, [(n, dtype)], _epi_store, **kw)[0]


def _rows(name, body, row_ins, vec_ins, row_outs, acc_outs=(), tile=512, window=None):
    S = row_ins[0].shape[0]
    t = _pick(S, (tile, 256, 128, 64, 32, 16, 8))
    in_specs = [pl.BlockSpec((t, r.shape[1]), lambda i: (i, 0)) for r in row_ins]
    in_specs += [pl.BlockSpec(v.shape, lambda i: (0, 0)) for v in vec_ins]
    out_specs = [pl.BlockSpec((t, w), lambda i: (i, 0)) for w, _ in row_outs]
    out_specs += [pl.BlockSpec(shp, lambda i: (0, 0)) for shp in acc_outs]
    out_shape = [jax.ShapeDtypeStruct((S, w), dt) for w, dt in row_outs]
    out_shape += [jax.ShapeDtypeStruct(shp, F32) for shp in acc_outs]
    extra, aliases, n_in = [], {}, len(row_ins) + len(vec_ins)
    if window is not None:
        k, (first, width), total, buffer = window
        assert first % width == 0 and row_outs[k][0] == width, (name, window[:3])
        out_specs[k] = pl.BlockSpec((t, width), lambda i: (i, first // width))
        out_shape[k] = jax.ShapeDtypeStruct((S, total), row_outs[k][1])
        if buffer is not None:
            extra, aliases = [buffer], {n_in: k}
            in_specs.append(pl.BlockSpec(memory_space=pl.ANY))

    def call(*refs):
        body(*refs[:n_in], *refs[n_in + len(extra):])

    sem = ("arbitrary",) if acc_outs else ("parallel",)
    return pl.pallas_call(
        call, name=name, grid=(S // t,), in_specs=in_specs, out_specs=out_specs,
        out_shape=out_shape, input_output_aliases=aliases, compiler_params=_params(sem),
    )(*row_ins, *vec_ins, *extra)


def _accum(ref, val, first=True):
    if first:
        @pl.when(pl.program_id(0) == 0)
        def _():
            ref[...] = jnp.zeros_like(ref)

    ref[...] += val


EPILOGUE_ROWS = 256


def _row_chunks(n):
    step = min(EPILOGUE_ROWS, n)
    return [slice(r, r + step) for r in range(0, n, step)]


def _colsum(v):
    return jnp.sum(v, axis=0, keepdims=True)


def _rms_fwd(x, g):
    r = lax.rsqrt(jnp.mean(x * x, axis=-1, keepdims=True) + EPS)
    return x * r * g


def _rms_bwd(x, g, du):
    r = lax.rsqrt(jnp.mean(x * x, axis=-1, keepdims=True) + EPS)
    xn = x * r
    gdu = du * g
    dx = r * (gdu - xn * jnp.mean(xn * gdu, axis=-1, keepdims=True))
    return dx, _colsum(du * xn)


def _sigmoid(v):
    return 1.0 / (1.0 + jnp.exp(-v))


def _rope(v, c, sa, sb, sign):
    return v * c + sign * (pltpu.roll(v, HEAD_PAD - QK_ROPE // 2, 1) * sa + pltpu.roll(v, QK_ROPE // 2, 1) * sb)


def _split3(v):
    hi = v.astype(BF16)
    r1 = v - hi.astype(F32)
    mid = r1.astype(BF16)
    lo = (r1 - mid.astype(F32)).astype(BF16)
    return hi, mid, lo


def _put_stats(base, stat, col):
    hi, mid, lo = _split3(stat)
    lane = lax.broadcasted_iota(jnp.int32, base.shape, 1)
    out = jnp.where(lane == col, hi, base)
    out = jnp.where(lane == col + 1, mid, out)
    return jnp.where(lane == col + 2, lo, out)


def _neg_ones(shape, col):
    lane = lax.broadcasted_iota(jnp.int32, shape, 1)
    return jnp.where((lane >= col) & (lane < col + 3), -1.0, 0.0).astype(F32)


def _shifted(ext, t):
    p = ext.shape[0]
    for b in range(8):
        rb = ext if b == 0 else pltpu.roll(ext, p - b, 0)
        for a in range(HALO // 8 + 1):
            if 8 * a + b <= HALO:
                yield 8 * a + b, rb[8 * a:8 * a + t]


def _conv_fwd(z0, conv_w, conv_b, ln_g, ln_b):
    S, C = z0.shape
    t = _pick(S, (512, 256, 128, 64, 32))
    per = t // HALO

    def body(cur_ref, prev_ref, w_ref, b_ref, g_ref, beta_ref, z1_ref, z3_ref, ext_ref):
        i = pl.program_id(0)
        ext_ref[0:HALO, :] = jnp.where(i > 0, prev_ref[...].astype(F32), 0.0)
        ext_ref[HALO:, :] = cur_ref[...].astype(F32)
        ext = ext_ref[...]
        acc = jnp.zeros((t, C), F32)
        for d, win in _shifted(ext, t):
            k = d - (HALO - CONV_WIDTH + 1)
            if 0 <= k < CONV_WIDTH:
                acc = acc + win * w_ref[k:k + 1, :]
        z1 = acc + b_ref[...]
        z1_ref[...] = z1
        mu = jnp.mean(z1, axis=-1, keepdims=True)
        zc = z1 - mu
        rs = lax.rsqrt(jnp.mean(zc * zc, axis=-1, keepdims=True) + EPS)
        z2 = zc * rs * g_ref[...] + beta_ref[...]
        z3_ref[...] = (z2 * _sigmoid(z2)).astype(BF16)

    vec = lambda v: pl.BlockSpec(v.shape, lambda i: (0, 0))
    return pl.pallas_call(
        body, name="conv_fwd", grid=(S // t,),
        in_specs=[pl.BlockSpec((t, C), lambda i: (i, 0)),
                  pl.BlockSpec((HALO, C), lambda i: (jnp.maximum(i * per - 1, 0), 0)),
                  vec(conv_w), vec(conv_b), vec(ln_g), vec(ln_b)],
        out_specs=[pl.BlockSpec((t, C), lambda i: (i, 0)), pl.BlockSpec((t, C), lambda i: (i, 0))],
        out_shape=[jax.ShapeDtypeStruct((S, C), F32), jax.ShapeDtypeStruct((S, C), BF16)],
        scratch_shapes=[pltpu.VMEM((t + HALO, C), F32)],
        compiler_params=_params(("parallel",)),
    )(z0, z0, conv_w, conv_b, ln_g, ln_b)


def _conv_bwd_norm(dz3, z1, ln_g, ln_b):
    C = z1.shape[1]

    def body(dz3_ref, z1_ref, g_ref, beta_ref, dz1_ref, dg_ref, dbeta_ref, dbias_ref):
        z1 = z1_ref[...]
        mu = jnp.mean(z1, axis=-1, keepdims=True)
        zc = z1 - mu
        rs = lax.rsqrt(jnp.mean(zc * zc, axis=-1, keepdims=True) + EPS)
        xh = zc * rs
        z2 = xh * g_ref[...] + beta_ref[...]
        sg = _sigmoid(z2)
        dz2 = dz3_ref[...].astype(F32) * (sg * (1.0 + z2 * (1.0 - sg)))
        dxh = dz2 * g_ref[...]
        dz1 = rs * (dxh - jnp.mean(dxh, axis=-1, keepdims=True) - xh * jnp.mean(dxh * xh, axis=-1, keepdims=True))
        dz1_ref[...] = dz1
        _accum(dg_ref, _colsum(dz2 * xh))
        _accum(dbeta_ref, _colsum(dz2))
        _accum(dbias_ref, _colsum(dz1))

    return _rows("conv_bwd_norm", body, [dz3, z1], [ln_g, ln_b], [(C, F32)], [(1, C)] * 3)


def _conv_bwd_taps(dz1, z0, conv_in, conv_w, dproj, window):
    S, C = z0.shape
    t = _pick(S, (512, 256, 128, 64, 32))
    per = t // HALO
    last = S // HALO - 1
    nt = S // t
    assert window[1] == 2 * C and window[0] % window[1] == 0, window

    def body(dcur_ref, dnext_ref, zcur_ref, zprev_ref, cin_ref, w_ref, _, dcin_ref, dw_ref, dext_ref, zext_ref):
        i = pl.program_id(0)
        dcur = dcur_ref[...]
        dext_ref[0:t, :] = dcur
        dext_ref[t:, :] = jnp.where(i < nt - 1, dnext_ref[...], 0.0)
        zext_ref[0:HALO, :] = jnp.where(i > 0, zprev_ref[...].astype(F32), 0.0)
        zext_ref[HALO:, :] = zcur_ref[...].astype(F32)

        @pl.when(i == 0)
        def _():
            dw_ref[...] = jnp.zeros_like(dw_ref)

        dz0 = jnp.zeros((t, C), F32)
        for d, win in _shifted(dext_ref[...], t):
            k = CONV_WIDTH - 1 - d
            if 0 <= k < CONV_WIDTH:
                dz0 = dz0 + win * w_ref[k:k + 1, :]
        for d, win in _shifted(zext_ref[...], t):
            k = d - (HALO - CONV_WIDTH + 1)
            if 0 <= k < CONV_WIDTH:
                dw_ref[k:k + 1, :] += _colsum(dcur * win)
        a = cin_ref[:, 0:C].astype(F32)
        sg = _sigmoid(cin_ref[:, C:2 * C].astype(F32))
        dcin_ref[:, 0:C] = (dz0 * sg).astype(BF16)
        dcin_ref[:, C:2 * C] = (dz0 * a * sg * (1.0 - sg)).astype(BF16)

    return pl.pallas_call(
        body, name="conv_bwd_taps", grid=(nt,),
        in_specs=[pl.BlockSpec((t, C), lambda i: (i, 0)),
                  pl.BlockSpec((HALO, C), lambda i: (jnp.minimum((i + 1) * per, last), 0)),
                  pl.BlockSpec((t, C), lambda i: (i, 0)),
                  pl.BlockSpec((HALO, C), lambda i: (jnp.maximum(i * per - 1, 0), 0)),
                  pl.BlockSpec((t, 2 * C), lambda i: (i, 0)),
                  pl.BlockSpec(conv_w.shape, lambda i: (0, 0)),
                  pl.BlockSpec(memory_space=pl.ANY)],
        out_specs=[pl.BlockSpec((t, 2 * C), lambda i: (i, window[0] // window[1])), pl.BlockSpec((HALO, C), lambda i: (0, 0))],
        out_shape=[jax.ShapeDtypeStruct(dproj.shape, BF16), jax.ShapeDtypeStruct((HALO, C), F32)],
        input_output_aliases={6: 0},
        scratch_shapes=[pltpu.VMEM((t + HALO, C), F32), pltpu.VMEM((t + HALO, C), F32)],
        compiler_params=_params(("arbitrary",)),
    )(dz1, dz1, z0, z0, conv_in, conv_w, dproj)


def _lower_tri(shape, rows_are_queries):
    row = lax.broadcasted_iota(jnp.int32, shape, 0)
    col = lax.broadcasted_iota(jnp.int32, shape, 1)
    return (col <= row) if rows_are_queries else (row <= col)


HEADS_PER_STEP = 2
FWD_HEADS_PER_STEP = 2
FWD_KEY_TILES = 8


def _flash_specs(S, t, heads):
    w = heads * HEAD_PAD
    blk = pl.BlockSpec((t, w), lambda h, i: (i, h))
    head = pl.BlockSpec((S, w), lambda h, i: (0, h))
    return blk, head


def _head_lanes(g):
    return slice(g * HEAD_PAD, (g + 1) * HEAD_PAD)


def _dot_nt(a, b):
    return lax.dot_general(a, b, _DIMS["nt"], preferred_element_type=F32)


def _dot_nn(a, b):
    return lax.dot_general(a, b, _DIMS["nn"], preferred_element_type=F32)


def _dot_tn(a, b):
    return lax.dot_general(a, b, _DIMS["tn"], preferred_element_type=F32)


def _flash_fwd(q, k, v):
    S = q.shape[0]
    t = _pick(S, (512, 256, 128))

    def body(q_ref, k_ref, v_ref, o_ref, qa_ref, m_ref, acc_ref):
        qi = pl.program_id(1)
        m_ref[...] = jnp.full_like(m_ref, NEG)
        acc_ref[...] = jnp.zeros_like(acc_ref)

        def step(first, tiles, diag):
            width = tiles * t
            rows = pl.ds(pl.multiple_of(first, t), width)
            for g in range(FWD_HEADS_PER_STEP):
                hl = _head_lanes(g)
                s = _dot_nt(q_ref[:, hl], k_ref[rows, hl])
                if diag:
                    row = lax.broadcasted_iota(jnp.int32, s.shape, 0)
                    col = lax.broadcasted_iota(jnp.int32, s.shape, 1)
                    s = jnp.where(col <= row + (tiles - 1) * t, s, NEG)
                m_old = m_ref[g]
                m_new = jnp.maximum(m_old, jnp.max(s, axis=-1, keepdims=True))
                p = jnp.exp(s - m_new).astype(BF16)
                acc_ref[g] = jnp.exp(m_old - m_new) * acc_ref[g] + _dot_nn(p, v_ref[rows, hl])
                m_ref[g] = m_new

        def wide(kb, carry):
            step(kb * (FWD_KEY_TILES * t), FWD_KEY_TILES, False)
            return carry

        full_groups = qi // FWD_KEY_TILES
        lax.fori_loop(0, full_groups, wide, 0)
        for tiles in range(1, min(FWD_KEY_TILES, S // t) + 1):
            @pl.when(qi - full_groups * FWD_KEY_TILES == tiles - 1)
            def _():
                step(full_groups * (FWD_KEY_TILES * t), tiles, True)

        for g in range(FWD_HEADS_PER_STEP):
            hl = _head_lanes(g)
            acc = acc_ref[g]
            l = -acc[:, STAT_COL_V:STAT_COL_V + 1]
            o_ref[:, hl] = (acc / l).astype(BF16)
            qa_ref[:, hl] = _put_stats(q_ref[:, hl], m_ref[g] + jnp.log(l), STAT_COL_QK)

    blk, head = _flash_specs(S, t, FWD_HEADS_PER_STEP)
    return pl.pallas_call(
        body, name="mla_flash_fwd", grid=(MLA_HEADS // FWD_HEADS_PER_STEP, S // t),
        in_specs=[blk, head, head], out_specs=[blk, blk],
        out_shape=[jax.ShapeDtypeStruct(q.shape, BF16), jax.ShapeDtypeStruct(q.shape, BF16)],
        scratch_shapes=[pltpu.VMEM((FWD_HEADS_PER_STEP, t, 1), F32), pltpu.VMEM((FWD_HEADS_PER_STEP, t, HEAD_PAD), F32)],
        compiler_params=_params(("parallel", "arbitrary")),
    )(q, k, v)


def _flash_bwd(qa, k, v, doa):
    S = qa.shape[0]
    t = _pick(S, (1024, 512, 256, 128))
    n = S // t
    half = t // 2

    def body(qa_ref, k_ref, v_ref, do_ref, dq_ref, dk_ref, dv_ref, dk_acc, dv_acc):
        kj = pl.program_id(1)

        @pl.when(kj == 0)
        def _():
            dq_ref[...] = jnp.zeros_like(dq_ref)

        dk_acc[...] = jnp.zeros_like(dk_acc)
        dv_acc[...] = jnp.zeros_like(dv_acc)

        def step(q_first, q_len, keys, diag):
            rows = pl.ds(pl.multiple_of(q_first, q_len), q_len)
            for g in range(HEADS_PER_STEP):
                hl = _head_lanes(g)
                qa, do, kk = qa_ref[rows, hl], do_ref[rows, hl], k_ref[keys, hl]
                st = _dot_nt(kk, qa)
                if diag:
                    st = jnp.where(_lower_tri(st.shape, False), st, NEG)
                pt = jnp.exp(st)
                dst = (pt * _dot_nt(v_ref[keys, hl], do)).astype(BF16)
                dv_acc[keys, hl] += _dot_nn(pt.astype(BF16), do)
                dk_acc[keys, hl] += _dot_nn(dst, qa)
                dq_ref[rows, hl] += _dot_tn(dst, kk)

        def loop(qi, carry):
            step(qi * t, t, slice(0, t), False)
            return carry

        lo, hi = slice(0, half), slice(half, t)
        step(kj * t, half, lo, True)
        step(kj * t + half, half, lo, False)
        step(kj * t + half, half, hi, True)
        lax.fori_loop(kj + 1, n, loop, 0)
        dk_ref[...] = dk_acc[...].astype(BF16)
        dv_ref[...] = dv_acc[...].astype(BF16)

    blk, head = _flash_specs(S, t, HEADS_PER_STEP)
    w = HEADS_PER_STEP * HEAD_PAD
    return pl.pallas_call(
        body, name="mla_flash_bwd", grid=(MLA_HEADS // HEADS_PER_STEP, n),
        in_specs=[head, blk, blk, head],
        out_specs=[head, blk, pl.BlockSpec((t, w), lambda h, i: (i, h + qa.shape[1] // w))],
        out_shape=[jax.ShapeDtypeStruct(qa.shape, F32), jax.ShapeDtypeStruct(qa.shape, BF16),
                   jax.ShapeDtypeStruct((S, 2 * qa.shape[1]), BF16)],
        scratch_shapes=[pltpu.VMEM((t, w), F32), pltpu.VMEM((t, w), F32)],
        compiler_params=_params(("parallel", "arbitrary")),
    )(qa, k, v, doa)


def _xattn_fwd(xq, kvx):
    W = X_HEADS * X_HEAD_DIM

    def body(q_ref, kv_ref, o_ref):
        for h in range(X_HEADS):
            lo = h * X_HEAD_DIM
            s = _dot_nt(q_ref[:, lo:lo + X_HEAD_DIM], kv_ref[:, lo:lo + X_HEAD_DIM])
            p = jnp.exp(s - jnp.max(s, axis=-1, keepdims=True))
            p = p / jnp.sum(p, axis=-1, keepdims=True)
            o_ref[:, lo:lo + X_HEAD_DIM] = _dot_nn(p.astype(BF16), kv_ref[:, W + lo:W + lo + X_HEAD_DIM]).astype(BF16)

    return _rows("xattn_fwd", body, [xq], [kvx], [(W, BF16)])[0]


def _xattn_bwd(xq, kvx, dox):
    W = X_HEADS * X_HEAD_DIM
    scale = X_HEAD_DIM ** -0.5

    def body(q_ref, do_ref, kv_ref, dq_ref, dkv_ref):
        @pl.when(pl.program_id(0) == 0)
        def _():
            dkv_ref[...] = jnp.zeros_like(dkv_ref)

        for h in range(X_HEADS):
            lo = h * X_HEAD_DIM
            q, k = q_ref[:, lo:lo + X_HEAD_DIM], kv_ref[:, lo:lo + X_HEAD_DIM]
            v, do = kv_ref[:, W + lo:W + lo + X_HEAD_DIM], do_ref[:, lo:lo + X_HEAD_DIM]
            s = _dot_nt(q, k)
            p = jnp.exp(s - jnp.max(s, axis=-1, keepdims=True))
            p = p / jnp.sum(p, axis=-1, keepdims=True)
            dp = _dot_nt(do, v)
            ds = (p * (dp - jnp.sum(dp * p, axis=-1, keepdims=True))).astype(BF16)
            dq_ref[:, lo:lo + X_HEAD_DIM] = (_dot_nn(ds, k) * scale).astype(BF16)
            dkv_ref[:, lo:lo + X_HEAD_DIM] += _dot_tn(ds, q)
            dkv_ref[:, W + lo:W + lo + X_HEAD_DIM] += _dot_tn(p.astype(BF16), do)

    return _rows("xattn_bwd", body, [xq, dox], [kvx], [(W, BF16)], [kvx.shape])


def _adamw(name, w, g, m, v):
    c1 = 1.0 / (1.0 - ADAM_B1 ** ADAM_STEP)
    c2 = 1.0 / (1.0 - ADAM_B2 ** ADAM_STEP)
    lead = (0,) * (w.ndim - 2)
    w2 = w.reshape((1,) * (2 - w.ndim) + w.shape) if w.ndim < 2 else w
    m2, v2 = m.reshape(w2.shape), v.reshape(w2.shape)
    g2 = g.reshape(w2.shape[-2:])
    R, C = g2.shape
    t = _pick(R, (256, 128, 64, 32, 16, 8))

    def body(w_ref, g_ref, m_ref, v_ref, go_ref, d_ref, nm_ref, nv_ref):
        g = g_ref[...]
        nm = ADAM_B1 * m_ref[lead] + (1.0 - ADAM_B1) * g
        nv = ADAM_B2 * v_ref[lead] + (1.0 - ADAM_B2) * (g * g)
        go_ref[lead] = g
        d_ref[lead] = -ADAM_LR * ((nm * c1) / (jnp.sqrt(nv * c2) + ADAM_EPS) + ADAM_WD * w_ref[lead])
        nm_ref[lead] = nm
        nv_ref[lead] = nv

    full = pl.BlockSpec((1,) * len(lead) + (t, C), lambda i: lead + (i, 0))
    outs = pl.pallas_call(
        body, name=name, grid=(R // t,), in_specs=[full, pl.BlockSpec((t, C), lambda i: (i, 0)), full, full],
        out_specs=[full] * 4, out_shape=[jax.ShapeDtypeStruct(w2.shape, F32)] * 4, compiler_params=_params(("parallel",)),
    )(w2, g2, m2, v2)
    return [o.reshape(w.shape) for o in outs]


def _place():
    x, y, c = lax.axis_index("x"), lax.axis_index("y"), lax.axis_index("c")
    return x, y, c, [(1 - x, y), (x, 1 - y), (1 - x, 1 - y)]


_ANY = pl.BlockSpec(memory_space=pl.ANY)


_HBM = pl.BlockSpec(memory_space=pltpu.HBM)
_SEM = pl.BlockSpec(memory_space=pltpu.SEMAPHORE)
_SIDE_EFFECT = pltpu.SideEffectType.DATAFLOW_SIDE_EFFECTING


def _gather_copy(src, land, slot, send, recv, k, chip, c):
    return pltpu.make_async_remote_copy(src_ref=src, dst_ref=land.at[slot], send_sem=send.at[k], recv_sem=recv.at[k],
                                        device_id=(chip[0], chip[1], c), device_id_type=MESH)


def _gather_start(shards, groups):
    n, ng = len(shards), len(groups)
    mine = 2 * lax.axis_index("x") + lax.axis_index("y")
    lands = [lax.dynamic_update_slice(lax.empty((N_CHIPS,) + s.shape, s.dtype), s[None], (mine,) + (0,) * s.ndim)
             for s in shards]

    def body(*refs):
        ins, lnd = refs[:n], refs[n:2 * n]
        sends, recvs = refs[2 * n:2 * n + ng], refs[2 * n + ng:2 * n + 2 * ng]
        token = refs[-1]
        x, y, c, chips = _place()
        for gi, group in enumerate(groups):
            for pos, w in enumerate(group):
                for j, chip in enumerate(chips):
                    _gather_copy(ins[w], lnd[w], 2 * x + y, sends[gi], recvs[gi], 3 * pos + j, chip, c).start()
        token[...] = jnp.zeros_like(token)

    sems = [pltpu.SemaphoreType.DMA((3 * len(g),)) for g in groups]
    res = pl.pallas_call(
        body, name="gather_weights_start",
        out_shape=sems + sems + [pltpu.HBM(a.shape, a.dtype) for a in shards + lands] + [jax.ShapeDtypeStruct((8, LANES), F32)],
        in_specs=[_HBM] * (2 * n),
        out_specs=[_SEM] * (2 * ng) + [_HBM] * (2 * n) + [pl.BlockSpec(memory_space=pltpu.VMEM)],
        input_output_aliases={i: 2 * ng + i for i in range(2 * n)},
        compiler_params=pltpu.CompilerParams(has_side_effects=_SIDE_EFFECT),
    )(*[pltpu.with_memory_space_constraint(a, pltpu.HBM) for a in shards + lands])
    sem_pairs = list(zip(res[:ng], res[ng:2 * ng]))
    return sem_pairs, res[2 * ng:2 * ng + n], res[2 * ng + n:2 * ng + 2 * n], res[-1]


def _gather_wait(name, sem_pair, shards_thru, lands_thru, after):
    m = len(shards_thru)
    after = after if isinstance(after, (tuple, list)) else (after,)

    def body(*refs):
        ins, lnd = refs[:m], refs[m:2 * m]
        send, recv = refs[2 * m], refs[2 * m + 1]
        x, y, c, chips = _place()
        for pos in range(m):
            for j, chip in enumerate(chips):
                cp = _gather_copy(ins[pos], lnd[pos], 2 * chip[0] + chip[1], send, recv, 3 * pos + j, chip, c)
                cp.wait_send()
                cp.wait_recv()

    res = pl.pallas_call(
        body, name=name,
        out_shape=[pltpu.HBM(a.shape, a.dtype) for a in list(shards_thru) + list(lands_thru)],
        in_specs=[_HBM] * (2 * m) + [_SEM, _SEM] + [_ANY] * len(after), out_specs=[_HBM] * (2 * m),
        input_output_aliases={i: i for i in range(2 * m)},
        compiler_params=pltpu.CompilerParams(has_side_effects=_SIDE_EFFECT),
    )(*shards_thru, *lands_thru, *sem_pair, *after)
    return res[m:]


def _pair_exchange(name, packs):
    n = len(packs)

    def body(*refs):
        ins, outs, send, recv = refs[:n], refs[n:2 * n], refs[2 * n], refs[2 * n + 1]
        x, y, c, _ = _place()
        cps = []
        for g in range(n):
            cp = pltpu.make_async_remote_copy(src_ref=ins[g].at[:, pl.ds(1 - c, 1)], dst_ref=outs[g], send_sem=send.at[g],
                                              recv_sem=recv.at[g], device_id=(x, y, 1 - c), device_id_type=MESH)
            cp.start()
            cps.append(cp)
        for cp in cps:
            cp.wait()

    return pl.pallas_call(
        body, name=name, in_specs=[_ANY] * n, out_specs=[_ANY] * n,
        out_shape=[jax.ShapeDtypeStruct((N_CHIPS, 1) + p.shape[2:], p.dtype) for p in packs],
        scratch_shapes=[pltpu.SemaphoreType.DMA((n,)), pltpu.SemaphoreType.DMA((n,))],
    )(*packs)


def _chip_copy(src, land, src_slot, dst_slot, send, recv, k, chip, c):
    return pltpu.make_async_remote_copy(src_ref=src.at[src_slot], dst_ref=land.at[dst_slot], send_sem=send.at[k],
                                        recv_sem=recv.at[k], device_id=(chip[0], chip[1], c), device_id_type=MESH)


def _chip_exchange_start(name, parts):
    n = len(parts)
    lands = [lax.empty(p.shape, p.dtype) for p in parts]

    def body(*refs):
        ins, lnd, send, recv, token = refs[:n], refs[n:2 * n], refs[2 * n], refs[2 * n + 1], refs[-1]
        x, y, c, chips = _place()
        for g in range(n):
            for j, chip in enumerate(chips):
                _chip_copy(ins[g], lnd[g], 2 * chip[0] + chip[1], 2 * x + y, send, recv, 3 * g + j, chip, c).start()
        token[...] = jnp.zeros_like(token)

    sems = [pltpu.SemaphoreType.DMA((3 * n,))] * 2
    res = pl.pallas_call(
        body, name=name,
        out_shape=sems + [pltpu.HBM(a.shape, a.dtype) for a in list(parts) + lands] + [jax.ShapeDtypeStruct((8, LANES), F32)],
        in_specs=[_HBM] * (2 * n),
        out_specs=[_SEM] * 2 + [_HBM] * (2 * n) + [pl.BlockSpec(memory_space=pltpu.VMEM)],
        input_output_aliases={i: 2 + i for i in range(2 * n)},
        compiler_params=pltpu.CompilerParams(has_side_effects=_SIDE_EFFECT),
    )(*[pltpu.with_memory_space_constraint(a, pltpu.HBM) for a in list(parts) + lands])
    return res[:2], res[2:2 + n], res[2 + n:2 + 2 * n], res[-1]


def _chip_exchange_wait(name, sems, parts_thru, lands_thru, after):
    n = len(parts_thru)

    def body(*refs):
        ins, lnd, send, recv = refs[:n], refs[n:2 * n], refs[2 * n], refs[2 * n + 1]
        x, y, c, chips = _place()
        for g in range(n):
            for j, chip in enumerate(chips):
                cp = _chip_copy(ins[g], lnd[g], 2 * x + y, 2 * chip[0] + chip[1], send, recv, 3 * g + j, chip, c)
                cp.wait_send()
                cp.wait_recv()

    res = pl.pallas_call(
        body, name=name,
        out_shape=[pltpu.HBM(a.shape, a.dtype) for a in list(parts_thru) + list(lands_thru)],
        in_specs=[_HBM] * (2 * n) + [_SEM, _SEM, _ANY], out_specs=[_HBM] * (2 * n),
        input_output_aliases={i: i for i in range(2 * n)},
        compiler_params=pltpu.CompilerParams(has_side_effects=_SIDE_EFFECT),
    )(*parts_thru, *lands_thru, *sems, after)
    return res[:n], res[n:]


def _pair_share(halves):
    n = len(halves)

    def body(*refs):
        outs, send, recv = refs[n:2 * n], refs[2 * n], refs[2 * n + 1]
        x, y, c, _ = _place()
        cps = []
        for g in range(n):
            cp = pltpu.make_async_remote_copy(src_ref=outs[g].at[c], dst_ref=outs[g].at[c], send_sem=send.at[g],
                                              recv_sem=recv.at[g], device_id=(x, y, 1 - c), device_id_type=MESH)
            cp.start()
            cps.append(cp)
        for g in range(n):
            pltpu.make_async_remote_copy(src_ref=outs[g].at[c], dst_ref=outs[g].at[1 - c], send_sem=send.at[g],
                                         recv_sem=recv.at[g], device_id=(x, y, 1 - c), device_id_type=MESH).wait_recv()
        for cp in cps:
            cp.wait_send()

    return pl.pallas_call(
        body, name="grad_pair_share", in_specs=[_ANY] * n, out_specs=[_ANY] * n,
        out_shape=[jax.ShapeDtypeStruct(h.shape, h.dtype) for h in halves],
        input_output_aliases={g: g for g in range(n)},
        scratch_shapes=[pltpu.SemaphoreType.DMA((n,)), pltpu.SemaphoreType.DMA((n,))],
    )(*halves)


def _sum_over_devices(block):
    R, L = block.shape

    def gather(b_ref, o_ref, send, recv, loc):
        x, y, c, _ = _place()
        lc = pltpu.make_async_copy(b_ref, o_ref.at[4 * x + 2 * y + c], loc)
        lc.start()
        peers = [(1 - x if dx else x, 1 - y if dy else y, 1 - c if dc else c)
                 for dx in (0, 1) for dy in (0, 1) for dc in (0, 1) if dx or dy or dc]
        cps = []
        for j, peer in enumerate(peers):
            cp = pltpu.make_async_remote_copy(src_ref=b_ref, dst_ref=o_ref.at[4 * x + 2 * y + c], send_sem=send.at[j],
                                              recv_sem=recv.at[j], device_id=peer, device_id_type=MESH)
            cp.start()
            cps.append(cp)
        for j, (px, py, pc) in enumerate(peers):
            pltpu.make_async_remote_copy(src_ref=b_ref, dst_ref=o_ref.at[4 * px + 2 * py + pc], send_sem=send.at[j],
                                         recv_sem=recv.at[j], device_id=(px, py, pc), device_id_type=MESH).wait_recv()
        for cp in cps:
            cp.wait_send()
        lc.wait()

    blocks = pl.pallas_call(
        gather, name="small_grads_gather", in_specs=[_ANY], out_specs=_ANY,
        out_shape=jax.ShapeDtypeStruct((N_DEV, R, L), F32),
        scratch_shapes=[pltpu.SemaphoreType.DMA((N_DEV - 1,)), pltpu.SemaphoreType.DMA((N_DEV - 1,)), pltpu.SemaphoreType.DMA],
    )(block)

    def add(b_ref, o_ref):
        total = b_ref[0]
        for d in range(1, N_DEV):
            total = total + b_ref[d]
        o_ref[...] = total

    return pl.pallas_call(add, name="small_grads_add", out_shape=jax.ShapeDtypeStruct((R, L), F32))(blocks)


def _pair_add(name, pack, got):
    _, _, R, C = pack.shape
    t = _pick(R, ROW_TILES)
    c = lax.axis_index("c").astype(jnp.int32).reshape(1)

    def body(c_ref, p_ref, g_ref, o_ref):
        o_ref[...] = (p_ref[...].astype(F32) + g_ref[...].astype(F32)).astype(BF16)

    return pl.pallas_call(
        body, name=name,
        grid_spec=pltpu.PrefetchScalarGridSpec(
            num_scalar_prefetch=1, grid=(N_CHIPS, R // t),
            in_specs=[pl.BlockSpec((None, None, t, C), lambda k, i, c_ref: (k, c_ref[0], i, 0)),
                      pl.BlockSpec((None, None, t, C), lambda k, i, c_ref: (k, 0, i, 0))],
            out_specs=pl.BlockSpec((None, t, C), lambda k, i, c_ref: (k, i, 0))),
        out_shape=jax.ShapeDtypeStruct((N_CHIPS, R, C), BF16), compiler_params=_params(("parallel", "parallel")),
    )(c, pack, got)


def _chip_add(name, own, got):
    _, R, C = own.shape
    t = _pick(R, ROW_TILES)
    x, y, c, _ = _place()
    place = jnp.stack([c, 2 * x + y]).astype(jnp.int32)

    def body(place_ref, own_ref, g1_ref, g2_ref, g3_ref, o_ref):
        o_ref[...] = ((own_ref[...].astype(F32) + g1_ref[...].astype(F32)) + g2_ref[...].astype(F32)) + g3_ref[...].astype(F32)

    def other(d):
        return pl.BlockSpec((None, t, C), lambda i, place_ref: ((place_ref[1] + d) % N_CHIPS, i, 0))

    return pl.pallas_call(
        body, name=name,
        grid_spec=pltpu.PrefetchScalarGridSpec(
            num_scalar_prefetch=1, grid=(R // t,),
            in_specs=[pl.BlockSpec((None, t, C), lambda i, place_ref: (place_ref[1], i, 0)), other(1), other(2), other(3)],
            out_specs=pl.BlockSpec((None, t, C), lambda i, place_ref: (place_ref[0], i, 0))),
        out_shape=jax.ShapeDtypeStruct((2, R, C), F32), compiler_params=_params(("parallel",)),
    )(place, own, got, got, got)


_CUT = (2 * CONV_CH, 2 * CONV_CH + Q_LORA, 2 * CONV_CH + Q_LORA + KV_LORA, 2 * CONV_CH + Q_LORA + KV_LORA + QK_ROPE)
W_IN_GATES = (0, 2 * D_MODEL)
W_IN_CONV = (W_IN_GATES[1], 2 * CONV_CH)
W_IN_CQ = (W_IN_CONV[0] + W_IN_CONV[1], Q_LORA)
W_IN_KR = (W_IN_CQ[0] + W_IN_CQ[1], HEAD_PAD)
W_IN_CKV = (W_IN_KR[0] + W_IN_KR[1], KV_LORA)
W_IN_LORA = (W_IN_CQ[0], Q_LORA + HEAD_PAD + KV_LORA)
W_IN_COLS = W_IN_CKV[0] + W_IN_CKV[1]


def _pad_last(a, n):
    return jnp.pad(a, [(0, 0)] * (a.ndim - 1) + [(0, n - a.shape[-1])])


def _layout_w_in(w_in):
    kr = jnp.pad(w_in[:, _CUT[2]:_CUT[3]], ((0, 0), (QK_NOPE, HEAD_PAD - QK_NOPE - QK_ROPE)))
    return jnp.concatenate([w_in[:, _CUT[3]:], w_in[:, :_CUT[0]], w_in[:, _CUT[0]:_CUT[1]], kr, w_in[:, _CUT[1]:_CUT[2]]], axis=1)


def _layout_weights(w):
    out = dict(w)
    if "w_uq" in w:
        out["w_uq"] = _pad_last(w["w_uq"].reshape(Q_LORA, MLA_HEADS, QK_NOPE + QK_ROPE), HEAD_PAD).reshape(Q_LORA, MLA_HEADS * HEAD_PAD)
    if "w_ukv" in w:
        ukv = w["w_ukv"].reshape(KV_LORA, MLA_HEADS, QK_NOPE + V_DIM)
        uk = _pad_last(ukv[:, :, :QK_NOPE], HEAD_PAD).reshape(KV_LORA, MLA_HEADS * HEAD_PAD)
        uv = _pad_last(ukv[:, :, QK_NOPE:], HEAD_PAD).reshape(KV_LORA, MLA_HEADS * HEAD_PAD)
        out["w_ukv"] = jnp.concatenate([uk, uv], axis=1)
    if "w_mla_out" in w:
        mo = jnp.pad(w["w_mla_out"].reshape(MLA_HEADS, V_DIM, D_MODEL), ((0, 0), (0, HEAD_PAD - V_DIM), (0, 0)))
        out["w_mla_out"] = mo.reshape(MLA_HEADS * HEAD_PAD, D_MODEL)
    return out


def _unlayout_grads(g):
    out = dict(g)
    if "w_in" in g:
        gi = g["w_in"]
        win = lambda w: gi[:, w[0]:w[0] + w[1]]
        kr = gi[:, W_IN_KR[0] + QK_NOPE:W_IN_KR[0] + QK_NOPE + QK_ROPE]
        out["w_in"] = jnp.concatenate([win(W_IN_CONV), win(W_IN_CQ), win(W_IN_CKV), kr, win(W_IN_GATES)], axis=1)
    if "w_uq" in g:
        out["w_uq"] = g["w_uq"].reshape(Q_LORA, MLA_HEADS, HEAD_PAD)[:, :, :QK_NOPE + QK_ROPE].reshape(Q_LORA, -1)
    if "w_ukv" in g:
        gk = g["w_ukv"][:, :MLA_HEADS * HEAD_PAD].reshape(KV_LORA, MLA_HEADS, HEAD_PAD)[:, :, :QK_NOPE]
        gv = g["w_ukv"][:, MLA_HEADS * HEAD_PAD:].reshape(KV_LORA, MLA_HEADS, HEAD_PAD)[:, :, :V_DIM]
        out["w_ukv"] = jnp.concatenate([gk, gv], axis=2).reshape(KV_LORA, -1)
    if "w_mla_out" in g:
        out["w_mla_out"] = g["w_mla_out"].reshape(MLA_HEADS, HEAD_PAD, D_MODEL)[:, :V_DIM].reshape(MLA_HEADS * V_DIM, D_MODEL)
    return out


def _rope_tables(positions):
    half = QK_ROPE // 2
    inv_freq = ROPE_THETA ** (-jnp.arange(half, dtype=F32) / half)
    ang = positions.astype(F32).reshape(-1, 1) * inv_freq
    cos, sin = jnp.cos(ang), jnp.sin(ang)
    S = cos.shape[0]
    z16, z32, z64 = jnp.zeros((S, half), F32), jnp.zeros((S, QK_ROPE), F32), jnp.zeros((S, QK_NOPE), F32)
    c = jnp.concatenate([jnp.ones((S, QK_NOPE), F32), cos, cos, z32], axis=1)
    sa = jnp.concatenate([z64, -sin, z16, z32], axis=1)
    sb = jnp.concatenate([z64, z16, sin, z32], axis=1)
    return c, sa, sb


def _local_step(x, mem, positions, target, weight_fns, early_grads_fn, late_grads_fn, sm):
    S = x.shape[0]
    HW = MLA_HEADS * HEAD_PAD
    rope_c, rope_sa, rope_sb = _rope_tables(positions)
    qk_scale = (QK_NOPE + QK_ROPE) ** -0.5

    def k_rms1(x_ref, g_ref, u_ref):
        u_ref[...] = _rms_fwd(x_ref[...], g_ref[...]).astype(BF16)

    u1, = _rows("rms_mix", k_rms1, [x], [sm["norm_mix_g"]], [(D_MODEL, BF16)])
    w_in, conv_w = weight_fns[0]((u1, rope_c, rope_sa, rope_sb))

    def epi_glu(acc, xs, outs):
        a, gt = acc[:, 0:CONV_CH], acc[:, CONV_CH:2 * CONV_CH]
        outs[0][...] = acc[...].astype(BF16)
        outs[1][...] = (a * _sigmoid(gt)).astype(BF16)

    conv_in, z0 = _mm("proj_conv", u1, w_in, "nn", [(2 * CONV_CH, BF16), (CONV_CH, BF16)], epi_glu, b_cols=W_IN_CONV)
    c_q = _mm_plain("proj_cq", u1, w_in, "nn", dtype=BF16, b_cols=W_IN_CQ)
    c_kv = _mm_plain("proj_ckv", u1, w_in, "nn", dtype=BF16, b_cols=W_IN_CKV)
    kr_raw = _mm_plain("proj_krope", u1, w_in, "nn", b_cols=W_IN_KR)

    def epi_sigmoid(acc, xs, outs):
        outs[0][...] = _sigmoid(acc[...]).astype(BF16)

    gates, = _mm("proj_gates", u1, w_in, "nn", [(2 * D_MODEL, BF16)], epi_sigmoid, b_cols=W_IN_GATES)

    z1, z3 = _conv_fwd(z0, conv_w, sm["conv_b"], sm["conv_ln_g"], sm["conv_ln_b"])
    wl = weight_fns[1](z1)
    conv_out = _mm_plain("conv_out", z3, wl["w_conv_out"], "nn", dtype=BF16)

    def k_lora_norm(cq_ref, ckv_ref, gq_ref, gkv_ref, qn_ref, kvn_ref):
        qn_ref[...] = _rms_fwd(cq_ref[...].astype(F32), gq_ref[...]).astype(BF16)
        kvn_ref[...] = _rms_fwd(ckv_ref[...].astype(F32), gkv_ref[...]).astype(BF16)

    qn, kvn = _rows("lora_norm", k_lora_norm, [c_q, c_kv], [sm["q_norm_g"], sm["kv_norm_g"]],
                    [(Q_LORA, BF16), (KV_LORA, BF16)])

    def epi_q(acc, xs, outs):
        c, sa, sb = xs[0][...], xs[1][...], xs[2][...]
        for h in range(MLA_HEADS):
            lo = h * HEAD_PAD
            outs[0][:, lo:lo + HEAD_PAD] = (_rope(acc[:, lo:lo + HEAD_PAD], c, sa, sb, 1.0) * qk_scale).astype(BF16)

    q_att, = _mm("q_up", qn, wl["w_uq"], "nn", [(HW, BF16)], epi_q, row_x=[rope_c, rope_sa, rope_sb], tn=HW)

    def epi_kv(acc, xs, outs):
        kr = _rope(xs[0][...], xs[1][...], xs[2][...], xs[3][...], 1.0)
        kr = kr + _neg_ones(kr.shape, STAT_COL_QK)
        vconst = _neg_ones(kr.shape, STAT_COL_V)
        for h in range(MLA_HEADS):
            lo = h * HEAD_PAD
            outs[0][:, lo:lo + HEAD_PAD] = (acc[:, lo:lo + HEAD_PAD] + kr).astype(BF16)
            outs[1][:, lo:lo + HEAD_PAD] = (acc[:, HW + lo:HW + lo + HEAD_PAD] + vconst).astype(BF16)

    k_att, v_att = _mm("kv_up", kvn, wl["w_ukv"], "nn", [(HW, BF16), (HW, BF16)], epi_kv,
                       row_x=[kr_raw, rope_c, rope_sa, rope_sb], tn=2 * HW)

    o_att, q_aug = _flash_fwd(q_att, k_att, v_att)
    wl.update(weight_fns[2](o_att))

    def epi_merge(acc, xs, outs):
        for rows in _row_chunks(acc.shape[0]):
            mo = acc[rows, :]
            g0, g1 = xs[0][rows, 0:D_MODEL].astype(F32), xs[0][rows, D_MODEL:].astype(F32)
            outs[0][rows, :] = mo.astype(BF16)
            outs[1][rows, :] = (g0 * xs[1][rows, :].astype(F32) + g1 * mo).astype(BF16)

    mla_out, merged = _mm("mla_out_merge", o_att, wl["w_mla_out"], "nn", [(D_MODEL, BF16), (D_MODEL, BF16)], epi_merge,
                          row_x=[gates, conv_out], tn=D_MODEL)

    def epi_res_norm(acc, xs, outs):
        h = xs[0][...] + acc[...]
        outs[0][...] = h
        outs[1][...] = _rms_fwd(h, xs[1][...]).astype(BF16)

    h1, u2 = _mm("mix_out", merged, wl["w_out"], "nn", [(D_MODEL, F32), (D_MODEL, BF16)], epi_res_norm,
                 row_x=[x], vec_x=[sm["norm_xattn_g"]], tn=D_MODEL)

    xscale = X_HEAD_DIM ** -0.5

    def epi_scale(acc, xs, outs):
        outs[0][...] = (acc[...] * xscale).astype(BF16)

    xq, = _mm("xattn_q", u2, wl["w_xq"], "nn", [(X_HEADS * X_HEAD_DIM, BF16)], epi_scale)

    def k_mem_norm(m_ref, g_ref, o_ref):
        o_ref[...] = _rms_fwd(m_ref[...], g_ref[...]).astype(BF16)

    mem_n, = _rows("mem_norm", k_mem_norm, [mem], [sm["norm_mem_g"]], [(D_MODEL, BF16)])
    kvx = _mm_plain("xattn_kv", mem_n, wl["w_xkv"], "nn", dtype=BF16)
    ox = _xattn_fwd(xq, kvx)
    h2, u3 = _mm("xattn_out", ox, wl["w_xo"], "nn", [(D_MODEL, F32), (D_MODEL, BF16)], epi_res_norm,
                 row_x=[h1], vec_x=[sm["norm_mlp_g"]], tn=D_MODEL)

    def epi_relu2(acc, xs, outs):
        r = jnp.maximum(acc[...], 0.0)
        outs[0][...] = (r * r).astype(BF16)

    hid, = _mm("mlp_up", u3, wl["w_mlp1"], "nn", [(D_FF, BF16)], epi_relu2)

    def epi_final(acc, xs, outs):
        g = xs[2][...]
        for rows in _row_chunks(acc.shape[0]):
            h = xs[0][rows, :] + acc[rows, :]
            e = _rms_fwd(h, g) - xs[1][rows, :]
            part = 0.5 * jnp.sum(jnp.mean(e * e, axis=-1, keepdims=True), axis=0, keepdims=True)
            dh, dg = _rms_bwd(h, g, e * (1.0 / D_MODEL))
            outs[0][rows, :] = dh
            outs[1][rows, :] = dh.astype(BF16)
            _accum(outs[2], jnp.broadcast_to(part, outs[2].shape), first=rows.start == 0)
            _accum(outs[3], dg, first=rows.start == 0)

    dh3, dh3b, loss, g_final = _mm("mlp_down_loss", hid, wl["w_mlp2"], "nn", [(D_MODEL, F32), (D_MODEL, BF16)], epi_final,
                                   row_x=[h2, target], vec_x=[sm["final_norm_g"]], sums=[(1, LANES), (1, D_MODEL)],
                                   tn=D_MODEL, tk=1024)

    def epi_drelu2(acc, xs, outs):
        outs[0][...] = (acc[...] * (2.0 * jnp.sqrt(xs[0][...].astype(F32)))).astype(BF16)

    da1, = _mm("mlp_down_dx", dh3b, wl["w_mlp2"], "nt", [(D_FF, BF16)], epi_drelu2, tile_x=[hid])
    g_mlp2 = _mm_plain("mlp_down_dw", hid, dh3b, "tn", dtype=BF16)
    g_mlp1 = _mm_plain("mlp_up_dw", u3, da1, "tn", dtype=BF16)

    def epi_norm_bwd(acc, xs, outs):
        for rows in _row_chunks(acc.shape[0]):
            dx, dg = _rms_bwd(xs[0][rows, :], xs[2][...], acc[rows, :])
            dh = xs[1][rows, :] + dx
            outs[0][rows, :] = dh
            if len(outs) == 3:
                outs[1][rows, :] = dh.astype(BF16)
            _accum(outs[-1], dg, first=rows.start == 0)

    def dx_norm_bwd(name, dy, w, xin, dres, vecs, with_bf16=True):
        outs = [(D_MODEL, F32), (D_MODEL, BF16)] if with_bf16 else [(D_MODEL, F32)]
        return _mm(name, dy, w, "nt", outs, epi_norm_bwd, row_x=[xin, dres], vec_x=vecs, sums=[(1, D_MODEL)],
                   tn=D_MODEL, tk=_pick(dy.shape[1], (1024, 768, 512)))

    dh2, dh2b, g_norm_mlp = dx_norm_bwd("mlp_up_dx_norm", da1, wl["w_mlp1"], h2, dh3, [sm["norm_mlp_g"]])

    dox = _mm_plain("xattn_out_dx", dh2b, wl["w_xo"], "nt", dtype=BF16)
    g_xo = _mm_plain("xattn_out_dw", ox, dh2b, "tn", dtype=BF16)
    dxq, dkvx = _xattn_bwd(xq, kvx, dox)
    g_xq = _mm_plain("xattn_q_dw", u2, dxq, "tn", dtype=BF16)
    g_xkv = _mm_plain("xattn_kv_dw", mem_n, dkvx, "tn", dtype=BF16)
    dmem_n = _mm_plain("xattn_kv_dx", dkvx, wl["w_xkv"], "nt")

    def k_mem_bwd(m_ref, d_ref, g_ref, dg_ref):
        _, dg = _rms_bwd(m_ref[...], g_ref[...], d_ref[...])
        _accum(dg_ref, dg)

    g_norm_mem, = _rows("mem_norm_bwd", k_mem_bwd, [mem, dmem_n], [sm["norm_mem_g"]], [], [(1, D_MODEL)])
    token = early_grads_fn(dict(w_mlp1=g_mlp1, w_mlp2=g_mlp2, w_xo=g_xo, w_xq=g_xq, w_xkv=g_xkv))
    dh1, dh1b, g_norm_xattn = dx_norm_bwd("xattn_q_dx_norm", dxq, wl["w_xq"], h1, dh2, [sm["norm_xattn_g"], token])

    dmerged = _mm_plain("mix_out_dx", dh1b, wl["w_out"], "nt", dtype=BF16)
    g_out = _mm_plain("mix_out_dw", merged, dh1b, "tn", dtype=BF16)

    def k_merge_bwd(dm_ref, g_ref, co_ref, mo_ref, dco_ref, dmo_ref, dgl_ref):
        dm = dm_ref[...].astype(F32)
        g0, g1 = g_ref[:, 0:D_MODEL].astype(F32), g_ref[:, D_MODEL:].astype(F32)
        dco_ref[...] = (dm * g0).astype(BF16)
        dmo_ref[...] = (dm * g1).astype(BF16)
        dgl_ref[:, 0:D_MODEL] = (dm * co_ref[...].astype(F32) * g0 * (1.0 - g0)).astype(BF16)
        dgl_ref[:, D_MODEL:] = (dm * mo_ref[...].astype(F32) * g1 * (1.0 - g1)).astype(BF16)

    dconv_out, dmla_out, dproj = _rows("merge_bwd", k_merge_bwd, [dmerged, gates, conv_out, mla_out], [],
                                       [(D_MODEL, BF16), (D_MODEL, BF16), (2 * D_MODEL, BF16)], tile=256,
                                       window=(2, W_IN_GATES, W_IN_COLS, None))

    def epi_do(acc, xs, outs):
        for h in range(MLA_HEADS):
            lo = h * HEAD_PAD
            do = acc[:, lo:lo + HEAD_PAD]
            delta = jnp.sum(do * xs[0][:, lo:lo + HEAD_PAD].astype(F32), axis=-1, keepdims=True)
            outs[0][:, lo:lo + HEAD_PAD] = _put_stats(do.astype(BF16), delta, STAT_COL_V)

    do_aug, = _mm("mla_out_dx", dmla_out, wl["w_mla_out"], "nt", [(HW, BF16)], epi_do, row_x=[o_att], tn=HW)
    g_mla_out = _mm_plain("mla_out_dw", o_att, dmla_out, "tn", dtype=BF16)
    dq_att, dk_att, dkv_cat = _flash_bwd(q_aug, k_att, v_att, do_aug)

    def k_rope_bwd(dq_ref, dk_ref, c_ref, sa_ref, sb_ref, dqr_ref, dkv_ref, dkr_ref):
        c, sa, sb = c_ref[...], sa_ref[...], sb_ref[...]
        lane = lax.broadcasted_iota(jnp.int32, c.shape, 1)
        nope = (lane < QK_NOPE).astype(F32)
        ropem = ((lane >= QK_NOPE) & (lane < QK_NOPE + QK_ROPE)).astype(F32)
        dkr = jnp.zeros(c.shape, F32)
        for h in range(MLA_HEADS):
            lo = h * HEAD_PAD
            dqr_ref[:, lo:lo + HEAD_PAD] = (_rope(dq_ref[:, lo:lo + HEAD_PAD], c, sa, sb, -1.0) * qk_scale).astype(BF16)
            dk = dk_ref[:, lo:lo + HEAD_PAD].astype(F32)
            dkv_ref[:, lo:lo + HEAD_PAD] = (dk * nope).astype(BF16)
            dkr = dkr + dk
        dkr_ref[...] = (_rope(dkr * ropem, c, sa, sb, -1.0) * ropem).astype(BF16)

    dq_raw, dkv_cat, dkr = _rows("rope_bwd", k_rope_bwd, [dq_att, dk_att, rope_c, rope_sa, rope_sb], [],
                                 [(HW, BF16), (HW, BF16), (HEAD_PAD, BF16)], tile=256,
                                 window=(1, (0, HW), 2 * HW, dkv_cat))
    g_uq = _mm_plain("q_up_dw", qn, dq_raw, "tn", dtype=BF16)
    dqn = _mm_plain("q_up_dx", dq_raw, wl["w_uq"], "nt")
    g_ukv = _mm_plain("kv_up_dw", kvn, dkv_cat, "tn", dtype=BF16)
    dkvn = _mm_plain("kv_up_dx", dkv_cat, wl["w_ukv"], "nt")

    def k_lora_bwd(cq_ref, ckv_ref, dqn_ref, dkvn_ref, dkr_ref, gq_ref, gkv_ref, out_ref, dgq_ref, dgkv_ref):
        dcq, dgq = _rms_bwd(cq_ref[...].astype(F32), gq_ref[...], dqn_ref[...])
        dckv, dgkv = _rms_bwd(ckv_ref[...].astype(F32), gkv_ref[...], dkvn_ref[...])
        out_ref[:, 0:Q_LORA] = dcq.astype(BF16)
        out_ref[:, Q_LORA:Q_LORA + HEAD_PAD] = dkr_ref[...]
        out_ref[:, Q_LORA + HEAD_PAD:] = dckv.astype(BF16)
        _accum(dgq_ref, dgq)
        _accum(dgkv_ref, dgkv)

    dproj, g_q_norm, g_kv_norm = _rows("lora_norm_bwd", k_lora_bwd, [c_q, c_kv, dqn, dkvn, dkr],
                                       [sm["q_norm_g"], sm["kv_norm_g"]], [(W_IN_LORA[1], BF16)],
                                       [(1, Q_LORA), (1, KV_LORA)], window=(0, W_IN_LORA, W_IN_COLS, dproj))

    def epi_conv_norm_bwd(acc, xs, outs):
        g, beta = xs[1][...], xs[2][...]
        for rows in _row_chunks(acc.shape[0]):
            z1c = xs[0][rows, :]
            zc = z1c - jnp.mean(z1c, axis=-1, keepdims=True)
            rs = lax.rsqrt(jnp.mean(zc * zc, axis=-1, keepdims=True) + EPS)
            xh = zc * rs
            z2 = xh * g + beta
            sg = _sigmoid(z2)
            dz2 = acc[rows, :] * (sg * (1.0 + z2 * (1.0 - sg)))
            dxh = dz2 * g
            dz1c = rs * (dxh - jnp.mean(dxh, axis=-1, keepdims=True) - xh * jnp.mean(dxh * xh, axis=-1, keepdims=True))
            outs[0][rows, :] = dz1c
            first = rows.start == 0
            _accum(outs[1], _colsum(dz2 * xh), first=first)
            _accum(outs[2], _colsum(dz2), first=first)
            _accum(outs[3], _colsum(dz1c), first=first)

    dz1, g_ln_g, g_ln_b, g_conv_b = _mm("conv_out_dx_norm", dconv_out, wl["w_conv_out"], "nt", [(CONV_CH, F32)],
                                        epi_conv_norm_bwd, row_x=[z1], vec_x=[sm["conv_ln_g"], sm["conv_ln_b"]],
                                        sums=[(1, CONV_CH)] * 3, tn=CONV_CH)
    g_conv_out = _mm_plain("conv_out_dw", z3, dconv_out, "tn", dtype=BF16)
    dproj, g_conv_w = _conv_bwd_taps(dz1, z0, conv_in, conv_w, dproj, W_IN_CONV)

    g_in = _mm_plain("proj_dw", u1, dproj, "tn", dtype=BF16)
    token = late_grads_fn(dict(w_in=g_in, conv_w=g_conv_w[:CONV_WIDTH], w_conv_out=g_conv_out, w_uq=g_uq, w_ukv=g_ukv,
                               w_mla_out=g_mla_out, w_out=g_out))
    grad_x, g_norm_mix = dx_norm_bwd("proj_dx_norm", dproj, w_in, x, dh1, [sm["norm_mix_g"], token], with_bf16=False)

    small = dict(norm_mix_g=g_norm_mix, conv_b=g_conv_b, conv_ln_g=g_ln_g, conv_ln_b=g_ln_b, q_norm_g=g_q_norm,
                 kv_norm_g=g_kv_norm, norm_xattn_g=g_norm_xattn, norm_mem_g=g_norm_mem, norm_mlp_g=g_norm_mlp,
                 final_norm_g=g_final)
    return loss, grad_x, small


def _shard(a, k, axis):
    n = a.shape[axis] // N_CHIPS
    return lax.slice_in_dim(a, k * n, (k + 1) * n, axis=axis)


def _pack_small(grads, loss):
    flat = jnp.concatenate([grads[n].reshape(-1) for n in SMALL] + [loss.reshape(-1)[:1]])
    rows = -(-flat.shape[0] // (8 * LANES)) * 8
    return jnp.pad(flat, (0, rows * LANES - flat.shape[0])).reshape(rows, LANES)


def _pack_groups(shapes, names):
    groups = {}
    for n in names:
        groups.setdefault(shapes[n][1], []).append(n)
    return groups


def _pad_rows(a, mult):
    return jnp.pad(a, ((0, -a.shape[0] % mult), (0, 0)))


def _pack_grads(grads, shapes, names):
    packs = []
    for width, group in _pack_groups(shapes, names).items():
        per_chip = [jnp.concatenate([_pad_rows(_shard(grads[n], k, SHARD_AXIS[n]).astype(BF16), PACK_ROW_ALIGN) for n in group])
                    for k in range(N_CHIPS)]
        rows = per_chip[0].shape[0]
        packs.append(jnp.stack(per_chip).reshape(N_CHIPS, 2, rows // 2, width))
    return packs


def _unpack_grads(fulls, shapes, names):
    out = {}
    for full, group in zip(fulls, _pack_groups(shapes, names).values()):
        flat, at = full.reshape(-1, full.shape[-1]), 0
        for n in group:
            rows = shapes[n][0]
            out[n] = flat[at:at + rows]
            at += rows + (-rows % PACK_ROW_ALIGN)
    return out


def _unpack(flat, names, shapes):
    out, at = {}, 0
    for n in names:
        size = math.prod(shapes[n])
        out[n] = flat[at:at + size].reshape(shapes[n])
        at += size
    return out, at


def kernel(x, mem, positions, norm_mix_g, w_in, conv_w, conv_b, conv_ln_g, conv_ln_b, w_conv_out, q_norm_g, w_uq, kv_norm_g, w_ukv, w_mla_out, w_out, norm_xattn_g, norm_mem_g, w_xq, w_xkv, w_xo, norm_mlp_g, w_mlp1, w_mlp2, final_norm_g, loss_target, m_norm_mix_g, m_w_in, m_conv_w, m_conv_b, m_conv_ln_g, m_conv_ln_b, m_w_conv_out, m_q_norm_g, m_w_uq, m_kv_norm_g, m_w_ukv, m_w_mla_out, m_w_out, m_norm_xattn_g, m_norm_mem_g, m_w_xq, m_w_xkv, m_w_xo, m_norm_mlp_g, m_w_mlp1, m_w_mlp2, m_final_norm_g, v_norm_mix_g, v_w_in, v_conv_w, v_conv_b, v_conv_ln_g, v_conv_ln_b, v_w_conv_out, v_q_norm_g, v_w_uq, v_kv_norm_g, v_w_ukv, v_w_mla_out, v_w_out, v_norm_xattn_g, v_norm_mem_g, v_w_xq, v_w_xkv, v_w_xo, v_norm_mlp_g, v_w_mlp1, v_w_mlp2, v_final_norm_g):
    args = dict(locals())
    w = {n: args[n] for n in WEIGHTS}
    m = {n: args["m_" + n] for n in WEIGHTS}
    v = {n: args["v_" + n] for n in WEIGHTS}

    shards = [w[n][0].astype(F32 if n == "conv_w" else BF16) for n in BIG]
    bounds = (0,) + WEIGHT_WAITS + (len(BIG),)
    spans = [slice(lo, hi) for lo, hi in zip(bounds[:-1], bounds[1:])]
    sem_pairs, shards_thru, lands_thru, token = _gather_start(shards, [list(range(len(BIG)))[sp] for sp in spans])

    def unshard(n, g):
        ax = SHARD_AXIS[n]
        return jnp.moveaxis(g, 0, ax).reshape(g.shape[1:1 + ax] + (N_CHIPS * g.shape[1 + ax],) + g.shape[2 + ax:])

    def wait_fn(i):
        def fn(after):
            lands = _gather_wait(f"gather_weights_wait_{i}", sem_pairs[i], shards_thru[spans[i]], lands_thru[spans[i]], after)
            full = {n: unshard(n, g) for n, g in zip(BIG[spans[i]], lands)}
            return (_layout_w_in(full["w_in"]), full["conv_w"]) if i == 0 else _layout_weights(full)
        return fn

    sm = {n: w[n].reshape(1, -1) for n in SMALL}
    sm["norm_mix_g"] = sm["norm_mix_g"] + token[0, 0]

    shapes = {n: w[n].shape[1:] if n in BIG else w[n].shape for n in WEIGHTS}
    late_names = [n for n in BIG if n not in EARLY_GRADS]
    inflight = {}

    def send_grads(tag, names):
        def fn(g):
            packs = _pack_grads(_unlayout_grads(g), shapes, names)
            got = _pair_exchange(f"grad_pair_exchange_{tag}", packs)
            pairs = [_pair_add(f"grad_pair_add_{tag}_{i}", p, r) for i, (p, r) in enumerate(zip(packs, got))]
            *inflight[tag], token = _chip_exchange_start(f"grad_chip_exchange_{tag}_start", pairs)
            return token
        return fn

    loss, grad_x, g_small = _local_step(x[0], mem[0], positions, loss_target[0], [wait_fn(i) for i in range(3)],
                                        send_grads("early", EARLY_GRADS), send_grads("late", late_names), sm)

    halves, counts = [], {}
    for tag in ("late", "early"):
        own, got = _chip_exchange_wait(f"grad_chip_exchange_{tag}_wait", *inflight[tag], grad_x)
        halves += [_chip_add(f"grad_chip_add_{tag}_{i}", p, g) for i, (p, g) in enumerate(zip(own, got))]
        counts[tag] = len(own)
    fulls = _pair_share(halves)
    g_sum = _unpack_grads(fulls[:counts["late"]], shapes, late_names)
    g_sum.update(_unpack_grads(fulls[counts["late"]:], shapes, EARLY_GRADS))
    small_flat = _sum_over_devices(_pack_small(g_small, loss)).reshape(-1)
    g_small, at = _unpack(small_flat, SMALL, shapes)
    g_sum.update(g_small)
    loss_sum = small_flat[at]

    out_g, out_d, out_m, out_v = [], [], [], []
    for n in WEIGHTS:
        g, d, nm, nv = _adamw("adamw_" + n, w[n], g_sum[n], m[n], v[n])
        out_g.append(g)
        out_d.append(d)
        out_m.append(nm)
        out_v.append(nv)
    return (loss_sum, grad_x[None], *out_g, *out_d, *out_m, *out_v)
```
